```python
import math
import jax, jax.numpy as jnp
from jax import lax
import numpy as np

D_MODEL = 1024
BATCH = 32
SEQ = 256
DEPTH = 2
DEC_BATCH = 4
DEC_SEQ = 4096
PAST_LEN = 256

GRID_W = 64
D_MIX = D_MODEL
N_HEADS_A = 8
HEAD_DIM_A = 64
D_ATT = N_HEADS_A * HEAD_DIM_A
WIN_R = 8
WIN_C = 16
Q_BLOCK = 128
D_RNN = D_MODEL // 4
RG_BLOCKS = 4
RG_BLK = D_RNN // RG_BLOCKS
RG_CONV = 4
RG_C = 8.0
D_HY = D_MODEL // 4
HY_ORDER = 2
HY_CONV = 3
POS_EMB_DIM = 33
FILTER_HIDDEN = 64
N_GROUPS = 4
EXPERTS_PER_GROUP = 4
N_EXPERTS = N_GROUPS * EXPERTS_PER_GROUP
D_EXPERT = 256
TOP_K = 2
D_IN = 3 * D_ATT + 2 * D_RNN + 3 * D_HY
ALPHA = (2.0 * DEPTH) ** 0.25
OUT_SCALE = (8.0 * DEPTH) ** -0.25
LN_EPS = 1e-5
F32 = jnp.float32

kernel_name = 'hybrid_natten_rglru_hyena_hmoe_step'


def layer_norm(x, g=None, b=None):
    xf = x.astype(F32)
    mu = jnp.mean(xf, axis=-1, keepdims=True)
    var = jnp.mean(jnp.square(xf - mu), axis=-1, keepdims=True)
    y = (xf - mu) * lax.rsqrt(var + LN_EPS)
    if g is not None:
        y = y * g.astype(F32) + b.astype(F32)
    return y.astype(x.dtype)


def modulation(cond, w_mod, b_mod):
    m = jnp.einsum('...d,de->...e', jax.nn.silu(cond), w_mod) + b_mod
    return jnp.split(m, 6, axis=-1)


def depthwise_conv(x, w, b, pad_left):
    width = w.shape[0]
    L = x.shape[1]
    xp = jnp.pad(x, ((0, 0), (pad_left, width - 1 - pad_left), (0, 0)))
    y = b
    for k in range(width):
        y = y + xp[:, k:k + L] * w[k]
    return y


def split_projection(u, w_in):
    Bn, L, _ = u.shape
    z = jnp.einsum('bld,de->ble', u, w_in)
    q, k, v, xr, gate, hy = jnp.split(
        z, [D_ATT, 2 * D_ATT, 3 * D_ATT, 3 * D_ATT + D_RNN, 3 * D_ATT + 2 * D_RNN], axis=-1)
    heads = lambda t: t.reshape(Bn, L, N_HEADS_A, HEAD_DIM_A)
    return heads(q), heads(k), heads(v), xr, gate, hy


def context_attention(q, k, v):
    Bn, Lc, H, d = q.shape
    scale = d ** -0.5
    qb = q.reshape(Bn, Lc // Q_BLOCK, Q_BLOCK, H, d).transpose(1, 0, 2, 3, 4)

    def block(qi):
        s = jnp.einsum('bqhd,bkhd->bhqk', qi, k).astype(F32) * scale
        p = jax.nn.softmax(s, axis=-1).astype(v.dtype)
        return jnp.einsum('bhqk,bkhd->bqhd', p, v)

    o = lax.map(block, qb)
    return o.transpose(1, 0, 2, 3, 4).reshape(Bn, Lc, H * d)


def neighbourhood_attention(q, k, v, k_ctx, v_ctx, rpb):
    Bn, L, H, d = q.shape
    rows = L // GRID_W
    wr = min(WIN_R, rows)
    scale = d ** -0.5
    qg = q.reshape(Bn, rows, GRID_W, H, d).transpose(1, 0, 2, 3, 4)
    kg = k.reshape(Bn, rows, GRID_W, H, d)
    vg = v.reshape(Bn, rows, GRID_W, H, d)
    cols = jnp.arange(GRID_W)
    c_start = jnp.clip(cols - WIN_C // 2, 0, GRID_W - WIN_C)
    col_mask = (cols[None, :] >= c_start[:, None]) & (cols[None, :] < c_start[:, None] + WIN_C)
    dc_idx = jnp.clip(cols[None, :] - cols[:, None], -(WIN_C - 1), WIN_C - 1) + (WIN_C - 1)
    rpb_c = rpb[:, :, dc_idx]

    def row(args):
        r, qr = args
        start = jnp.clip(r - wr // 2, 0, rows - wr)
        kb = lax.dynamic_slice_in_dim(kg, start, wr, axis=1)
        vb = lax.dynamic_slice_in_dim(vg, start, wr, axis=1)
        dr_idx = start + jnp.arange(wr) - r + (WIN_R - 1)
        bias = jnp.take(rpb_c, dr_idx, axis=1).transpose(0, 2, 1, 3)
        s_lat = jnp.einsum('bqhd,bikhd->bhqik', qr, kb).astype(F32) * scale + bias.astype(F32)
        s_lat = jnp.where(col_mask[None, None, :, None, :], s_lat, -jnp.inf)
        s_ctx = jnp.einsum('bqhd,bchd->bhqc', qr, k_ctx).astype(F32) * scale
        s = jnp.concatenate([s_lat.reshape(Bn, H, GRID_W, wr * GRID_W), s_ctx], axis=-1)
        p = jax.nn.softmax(s, axis=-1).astype(v.dtype)
        p_lat = p[..., :wr * GRID_W].reshape(Bn, H, GRID_W, wr, GRID_W)
        p_ctx = p[..., wr * GRID_W:]
        return (jnp.einsum('bhqik,bikhd->bqhd', p_lat, vb)
                + jnp.einsum('bhqc,bchd->bqhd', p_ctx, v_ctx))

    o = lax.map(row, (jnp.arange(rows), qg))
    return o.transpose(1, 0, 2, 3, 4).reshape(Bn, L, H * d)


def linear_scan(a, b, h0):
    if h0 is not None:
        b = b.at[:, 0].add(a[:, 0] * h0)

    def combine(e1, e2):
        a1, b1 = e1
        a2, b2 = e2
        return a1 * a2, a2 * b1 + b2

    _, h = lax.associative_scan(combine, (a, b), axis=1)
    return h


def rglru(xr, gate, conv_w, conv_b, wa, ba, wx, bx, lam, h0):
    Bn, L, _ = xr.shape
    xc = depthwise_conv(xr, conv_w, conv_b, RG_CONV // 2)
    xb = xc.reshape(Bn, L, RG_BLOCKS, RG_BLK)
    r = jax.nn.sigmoid(jnp.einsum('blnk,enkj->eblnj', xb, wa).reshape(2, Bn, L, D_RNN)
                       + ba[:, None, None, :])
    i = jax.nn.sigmoid(jnp.einsum('blnk,enkj->eblnj', xb, wx).reshape(2, Bn, L, D_RNN)
                       + bx[:, None, None, :])
    log_a = -RG_C * r.astype(F32) * jax.nn.softplus(-lam.astype(F32))[:, None, None, :]
    a = jnp.exp(log_a)
    b = jnp.sqrt(-jnp.expm1(2.0 * log_a)) * i.astype(F32) * xc.astype(F32)[None]
    h0f = None if h0 is None else h0[:, 0].astype(F32)
    h0b = None if h0 is None else h0[:, 1].astype(F32)
    h_f = linear_scan(a[0], b[0], h0f)
    h_b = jnp.flip(linear_scan(jnp.flip(a[1], axis=1), jnp.flip(b[1], axis=1), h0b), axis=1)
    y = (h_f + h_b).astype(xr.dtype) * jax.nn.gelu(gate)
    return y, h_f, h_b


def hyena_filters(L, w1, b1, w2, b2, w3, freq, decay):
    pos = jnp.arange(L, dtype=F32)
    t = jnp.linspace(0.0, 1.0, L, dtype=F32)
    bands = (POS_EMB_DIM - 1) // 2
    f = jnp.linspace(1e-4, bands - 1, bands, dtype=F32)
    ang = (2.0 * math.pi / L) * pos[:, None] * f[None, :]
    feats = jnp.concatenate([t[:, None], jnp.cos(ang), -jnp.sin(ang)], axis=-1)
    freq = freq.astype(F32)
    h = jnp.sin(freq[0] * (feats @ w1.astype(F32) + b1.astype(F32)))
    h = jnp.sin(freq[1] * (h @ w2.astype(F32) + b2.astype(F32)))
    h = (h @ w3.astype(F32)).reshape(L, 2 * HY_ORDER, D_HY).transpose(1, 0, 2)
    window = jnp.exp(-t[None, :, None] * jnp.abs(decay.astype(F32))[:, None, :])
    return h * window


def hyena(hy, conv_w, conv_b, w1, b1, w2, b2, w3, freq, decay, skip):
    Bn, L, _ = hy.shape
    hc = depthwise_conv(hy, conv_w, conv_b, HY_CONV // 2)
    v, x1, x2 = jnp.split(hc, 3, axis=-1)
    filt = hyena_filters(L, w1, b1, w2, b2, w3, freq, decay)
    f_fwd, f_bwd = filt[:HY_ORDER], filt[HY_ORDER:]
    circ_filter = jnp.concatenate(
        [f_fwd, jnp.zeros((HY_ORDER, 1, D_HY), F32), jnp.flip(f_bwd[:, 1:], axis=1)], axis=1)
    filt_f = jnp.fft.rfft(circ_filter, axis=1)
    z = v.astype(F32)
    for n, g in enumerate((x1, x2)):
        zf = jnp.fft.rfft(z, n=2 * L, axis=1)
        conv = jnp.fft.irfft(zf * filt_f[n][None], n=2 * L, axis=1)[:, :L]
        z = g.astype(F32) * (conv + skip[n].astype(F32) * z)
    return z.astype(hy.dtype)


def hier_moe(u, wg, bg, we, be, w_gate, w_up, w_down):
    Bn, L, _ = u.shape
    g_logits = (jnp.einsum('bld,dg->blg', u, wg) + bg).astype(F32)
    g_idx = jnp.argmax(g_logits, axis=-1)
    p_group = jnp.take_along_axis(jax.nn.softmax(g_logits, axis=-1), g_idx[..., None], axis=-1)
    e_logits = (jnp.einsum('bld,de->ble', u, we) + be).astype(F32)
    e_logits = e_logits.reshape(Bn, L, N_GROUPS, EXPERTS_PER_GROUP)
    e_sel = jnp.take_along_axis(e_logits, g_idx[..., None, None], axis=2)[:, :, 0]
    top_p, top_i = lax.top_k(jax.nn.softmax(e_sel, axis=-1), TOP_K)
    top_p = top_p / jnp.sum(top_p, axis=-1, keepdims=True)
    expert_id = g_idx[..., None] * EXPERTS_PER_GROUP + top_i
    combine = jnp.sum(jax.nn.one_hot(expert_id, N_EXPERTS, dtype=F32)
                      * (p_group * top_p)[..., None], axis=-2).astype(u.dtype)
    h = jax.nn.silu(jnp.einsum('bld,edf->blef', u, w_gate)) * jnp.einsum('bld,edf->blef', u, w_up)
    return jnp.einsum('blef,ble,efd->bld', h, combine, w_down)


def setup_inputs(seed: int = 0) -> dict:
    key = jax.random.key(seed)
    keys = iter(jax.random.split(key, 64))

    def nrm(shape, scale=1.0):
        return jax.random.normal(next(keys), shape, jnp.float32) * scale

    x_prompt = nrm((BATCH, SEQ, D_MODEL))
    x_sample = nrm((DEC_BATCH, DEC_SEQ, D_MODEL))
    cache_k = nrm((DEC_BATCH, DEPTH, PAST_LEN, N_HEADS_A, HEAD_DIM_A))
    cache_v = nrm((DEC_BATCH, DEPTH, PAST_LEN, N_HEADS_A, HEAD_DIM_A))
    state_rglru = nrm((DEC_BATCH, DEPTH, 2, D_RNN), 0.5)
    c = nrm((DEC_BATCH, D_MODEL))
    c_ctx = nrm((D_MODEL,))
    w_mod = nrm((DEPTH, D_MODEL, 6 * D_MODEL), D_MODEL ** -0.5)
    b_mod = nrm((DEPTH, 6 * D_MODEL), 0.02)
    w_in = nrm((DEPTH, D_MODEL, D_IN), D_MODEL ** -0.5)
    attn_rpb = nrm((DEPTH, N_HEADS_A, 2 * WIN_R - 1, 2 * WIN_C - 1), 0.1)
    rg_conv_w = nrm((DEPTH, RG_CONV, D_RNN), RG_CONV ** -0.5)
    rg_conv_b = nrm((DEPTH, D_RNN), 0.01)
    rg_wa = nrm((DEPTH, 2, RG_BLOCKS, RG_BLK, RG_BLK), RG_BLK ** -0.5)
    rg_ba = nrm((DEPTH, 2, D_RNN), 0.01)
    rg_wx = nrm((DEPTH, 2, RG_BLOCKS, RG_BLK, RG_BLK), RG_BLK ** -0.5)
    rg_bx = nrm((DEPTH, 2, D_RNN), 0.01)
    a0 = jax.random.uniform(next(keys), (DEPTH, 2, D_RNN), jnp.float32, 0.9, 0.999)
    p = a0 ** (1.0 / RG_C)
    rg_lambda = jnp.log(p) - jnp.log1p(-p)
    hy_conv_w = nrm((DEPTH, HY_CONV, 3 * D_HY), HY_CONV ** -0.5)
    hy_conv_b = nrm((DEPTH, 3 * D_HY), 0.01)
    hy_w1 = nrm((DEPTH, POS_EMB_DIM, FILTER_HIDDEN), POS_EMB_DIM ** -0.5)
    hy_b1 = nrm((DEPTH, FILTER_HIDDEN), 0.1)
    hy_w2 = nrm((DEPTH, FILTER_HIDDEN, FILTER_HIDDEN), FILTER_HIDDEN ** -0.5)
    hy_b2 = nrm((DEPTH, FILTER_HIDDEN), 0.1)
    hy_w3 = nrm((DEPTH, FILTER_HIDDEN, 2 * HY_ORDER * D_HY), 0.1 * FILTER_HIDDEN ** -0.5)
    hy_freq = 1.0 + nrm((DEPTH, 2, FILTER_HIDDEN), 0.05)
    decay_base = jnp.linspace(math.log(100.0) / 1.5, math.log(100.0) / 0.3, D_HY, dtype=jnp.float32)
    hy_decay = decay_base * (1.0 + nrm((DEPTH, 2 * HY_ORDER, D_HY), 0.05))
    hy_skip = nrm((DEPTH, HY_ORDER, D_HY), 0.5)
    w_out = nrm((DEPTH, D_MIX, D_MODEL), OUT_SCALE * D_MIX ** -0.5)
    ln1_g = 1.0 + nrm((DEPTH, D_MODEL), 0.01)
    ln1_b = nrm((DEPTH, D_MODEL), 0.01)
    router_wg = nrm((DEPTH, D_MODEL, N_GROUPS), D_MODEL ** -0.5)
    router_bg = nrm((DEPTH, N_GROUPS), 0.01)
    router_we = nrm((DEPTH, D_MODEL, N_EXPERTS), D_MODEL ** -0.5)
    router_be = nrm((DEPTH, N_EXPERTS), 0.01)
    moe_w_gate = nrm((DEPTH, N_EXPERTS, D_MODEL, D_EXPERT), D_MODEL ** -0.5)
    moe_w_up = nrm((DEPTH, N_EXPERTS, D_MODEL, D_EXPERT), D_MODEL ** -0.5)
    moe_w_down = nrm((DEPTH, N_EXPERTS, D_EXPERT, D_MODEL), OUT_SCALE * D_EXPERT ** -0.5)
    ln2_g = 1.0 + nrm((DEPTH, D_MODEL), 0.01)
    ln2_b = nrm((DEPTH, D_MODEL), 0.01)
    return {'x_prompt': x_prompt, 'x_sample': x_sample, 'cache_k': cache_k, 'cache_v': cache_v,
            'state_rglru': state_rglru, 'c': c, 'c_ctx': c_ctx, 'w_mod': w_mod, 'b_mod': b_mod,
            'w_in': w_in, 'attn_rpb': attn_rpb, 'rg_conv_w': rg_conv_w, 'rg_conv_b': rg_conv_b,
            'rg_wa': rg_wa, 'rg_ba': rg_ba, 'rg_wx': rg_wx, 'rg_bx': rg_bx, 'rg_lambda': rg_lambda,
            'hy_conv_w': hy_conv_w, 'hy_conv_b': hy_conv_b, 'hy_w1': hy_w1, 'hy_b1': hy_b1,
            'hy_w2': hy_w2, 'hy_b2': hy_b2, 'hy_w3': hy_w3, 'hy_freq': hy_freq, 'hy_decay': hy_decay,
            'hy_skip': hy_skip, 'w_out': w_out, 'ln1_g': ln1_g, 'ln1_b': ln1_b,
            'router_wg': router_wg, 'router_bg': router_bg, 'router_we': router_we,
            'router_be': router_be, 'moe_w_gate': moe_w_gate, 'moe_w_up': moe_w_up,
            'moe_w_down': moe_w_down, 'ln2_g': ln2_g, 'ln2_b': ln2_b}


def reference(x_prompt, x_sample, cache_k, cache_v, state_rglru, c, c_ctx, w_mod, b_mod, w_in,
              attn_rpb, rg_conv_w, rg_conv_b, rg_wa, rg_ba, rg_wx, rg_bx, rg_lambda,
              hy_conv_w, hy_conv_b, hy_w1, hy_b1, hy_w2, hy_b2, hy_w3, hy_freq, hy_decay, hy_skip,
              w_out, ln1_g, ln1_b, router_wg, router_bg, router_we, router_be,
              moe_w_gate, moe_w_up, moe_w_down, ln2_g, ln2_b):
    xp = x_prompt
    xs = x_sample
    new_k, new_v, new_h = [], [], []
    for l in range(DEPTH):
        rg_p = (rg_conv_w[l], rg_conv_b[l], rg_wa[l], rg_ba[l], rg_wx[l], rg_bx[l], rg_lambda[l])
        hy_p = (hy_conv_w[l], hy_conv_b[l], hy_w1[l], hy_b1[l], hy_w2[l], hy_b2[l], hy_w3[l],
                hy_freq[l], hy_decay[l], hy_skip[l])
        moe_p = (router_wg[l], router_bg[l], router_we[l], router_be[l],
                 moe_w_gate[l], moe_w_up[l], moe_w_down[l])

        sh1, sc1, g1, sh2, sc2, g2 = modulation(c_ctx[None, None, :], w_mod[l], b_mod[l])
        u = layer_norm(xp) * (1.0 + sc1) + sh1
        q, k, v, xr, gate, hy = split_projection(u, w_in[l])
        att = context_attention(q, k, v)
        rg, h_f, h_b = rglru(xr, gate, *rg_p, None)
        hyo = hyena(hy, *hy_p)
        mix = jnp.concatenate([att, rg, hyo], axis=-1) @ w_out[l]
        xp = layer_norm(ALPHA * xp + g1 * mix, ln1_g[l], ln1_b[l])
        u = layer_norm(xp) * (1.0 + sc2) + sh2
        xp = layer_norm(ALPHA * xp + g2 * hier_moe(u, *moe_p), ln2_g[l], ln2_b[l])
        new_k.append(k)
        new_v.append(v)
        new_h.append(jnp.stack([h_f[:, -1], h_b[:, 0]], axis=1).astype(x_prompt.dtype))

        sh1, sc1, g1, sh2, sc2, g2 = modulation(c[:, None, :], w_mod[l], b_mod[l])
        u = layer_norm(xs) * (1.0 + sc1) + sh1
        q, k, v, xr, gate, hy = split_projection(u, w_in[l])
        att = neighbourhood_attention(q, k, v, cache_k[:, l], cache_v[:, l], attn_rpb[l])
        rg, _, _ = rglru(xr, gate, *rg_p, state_rglru[:, l])
        hyo = hyena(hy, *hy_p)
        mix = jnp.concatenate([att, rg, hyo], axis=-1) @ w_out[l]
        xs = layer_norm(ALPHA * xs + g1 * mix, ln1_g[l], ln1_b[l])
        u = layer_norm(xs) * (1.0 + sc2) + sh2
        xs = layer_norm(ALPHA * xs + g2 * hier_moe(u, *moe_p), ln2_g[l], ln2_b[l])

    return (xp, xs, jnp.stack(new_k, axis=1), jnp.stack(new_v, axis=1), jnp.stack(new_h, axis=1))
```

```python
import functools
import math

import numpy as np
import jax
import jax.numpy as jnp
from jax import lax
from jax.experimental import pallas as pl
from jax.experimental.pallas import tpu as pltpu

F32 = jnp.float32
BF16 = jnp.bfloat16
HIGHEST = lax.Precision.HIGHEST

DEPTH = 2
D_MODEL = 1024
N_HEADS = 8
HEAD_DIM = 64
D_ATT = N_HEADS * HEAD_DIM
D_RNN = 256
D_HY = 256
GRID_W = 64
WIN_R = 8
WIN_C = 16
RG_CONV = 4
RG_C = 8.0
HY_CONV = 3
POS_EMB_DIM = 33
N_GROUPS = 4
EXPERTS_PER_GROUP = 4
N_EXPERTS = 16
D_EXPERT = 256
ALPHA = (2.0 * DEPTH) ** 0.25
LN_EPS = 1e-5
NEG = -1e30

Q_ROWS = 4
K_ROWS = Q_ROWS + WIN_R
LANES = 128
HALO = 8
FFT_N1 = 64
VMEM_LIMIT = 52 * 1024 * 1024


def _cp(*sem):
    return pltpu.CompilerParams(dimension_semantics=sem, vmem_limit_bytes=VMEM_LIMIT)


def _ln(x):
    mu = jnp.mean(x, axis=-1, keepdims=True)
    xc = x - mu
    var = jnp.mean(xc * xc, axis=-1, keepdims=True)
    return xc * lax.rsqrt(var + LN_EPS)


def _mod_kernel(c_ref, w_ref, b_ref, o_ref):
    c = c_ref[...]
    s = c * jax.nn.sigmoid(c)
    o_ref[0] = jnp.dot(s, w_ref[0], precision=HIGHEST, preferred_element_type=F32) + b_ref[0]


def _modulation(cond8, w_mod, b_mod):
    depth, d, n = w_mod.shape
    tn = n // 4
    return pl.pallas_call(
        _mod_kernel,
        grid=(depth, n // tn),
        in_specs=[pl.BlockSpec((8, d), lambda l, j: (0, 0)),
                  pl.BlockSpec((1, d, tn), lambda l, j: (l, 0, j)),
                  pl.BlockSpec((1, 1, tn), lambda l, j: (l, 0, j))],
        out_specs=pl.BlockSpec((1, 8, tn), lambda l, j: (l, 0, j)),
        out_shape=jax.ShapeDtypeStruct((depth, 8, n), F32),
        compiler_params=_cp("arbitrary", "arbitrary"),
        name="modulation",
    )(cond8, w_mod, b_mod.reshape(depth, 1, n))


def _inproj_kernel(x_ref, mod_ref, w_ref, q_ref, k_ref, v_ref, xr_ref, gt_ref, hv_ref, h1_ref, h2_ref):
    m = mod_ref[0]
    u = (_ln(x_ref[...]) * (1.0 + m[1:2]) + m[0:1]).astype(BF16)

    def proj(lo, hi):
        return jnp.dot(u, w_ref[:, lo:hi], preferred_element_type=F32)

    o = 0
    q_ref[0] = (proj(o, o + D_ATT) * (HEAD_DIM ** -0.5)).astype(q_ref.dtype)
    o += D_ATT
    k_ref[0] = proj(o, o + D_ATT).astype(k_ref.dtype)
    o += D_ATT
    v_ref[0] = proj(o, o + D_ATT).astype(v_ref.dtype)
    o += D_ATT
    for ref in (xr_ref, gt_ref, hv_ref, h1_ref, h2_ref):
        ref[...] = proj(o, o + D_RNN)
        o += D_RNN


def _inproj(x2d, mod_l, w_bf, *, nb, L, tm, cond_base, cond_per_batch, kv_dtype):
    tpb = L // tm
    d = x2d.shape[1]
    d_in = w_bf.shape[1]
    cond = (lambda i: cond_base + i // tpb) if cond_per_batch else (lambda i: cond_base)
    att_spec = pl.BlockSpec((1, tm, D_ATT), lambda i: (i // tpb, i % tpb, 0))
    t_spec = pl.BlockSpec((tm, D_RNN), lambda i: (i % tpb, i // tpb))
    att_shape = lambda dt: jax.ShapeDtypeStruct((nb, L, D_ATT), dt)
    t_shape = jax.ShapeDtypeStruct((L, nb * D_RNN), F32)
    return pl.pallas_call(
        _inproj_kernel,
        grid=(nb * tpb,),
        in_specs=[pl.BlockSpec((tm, d), lambda i: (i, 0)),
                  pl.BlockSpec((1, 6, d), lambda i: (cond(i), 0, 0)),
                  pl.BlockSpec((d, d_in), lambda i: (0, 0))],
        out_specs=[att_spec, att_spec, att_spec, t_spec, t_spec, t_spec, t_spec, t_spec],
        out_shape=[att_shape(BF16), att_shape(kv_dtype), att_shape(kv_dtype),
                   t_shape, t_shape, t_shape, t_shape, t_shape],
        compiler_params=_cp("arbitrary"),
        name="inproj",
    )(x2d, mod_l, w_bf)


def _nt(a, b):
    return lax.dot_general(a, b, (((1,), (1,)), ((), ())), preferred_element_type=F32)


def _ctx_attn_kernel(q_ref, k_ref, v_ref, o_ref):
    q = q_ref[0]
    k = k_ref[0].astype(BF16)
    v = v_ref[0].astype(BF16)
    outs = []
    for h in range(N_HEADS):
        sl = slice(h * HEAD_DIM, (h + 1) * HEAD_DIM)
        s = _nt(q[:, sl], k[:, sl])
        e = jnp.exp(s - jnp.max(s, axis=-1, keepdims=True))
        den = jnp.sum(e, axis=-1, keepdims=True)
        outs.append(jnp.dot(e.astype(BF16), v[:, sl], preferred_element_type=F32) / den)
    o_ref[0] = jnp.concatenate(outs, axis=-1).astype(o_ref.dtype)


def _ctx_attention(q, k, v):
    nb, L, _ = q.shape
    spec = pl.BlockSpec((1, L, D_ATT), lambda b: (b, 0, 0))
    return pl.pallas_call(
        _ctx_attn_kernel,
        grid=(nb,),
        in_specs=[spec, spec, spec],
        out_specs=spec,
        out_shape=jax.ShapeDtypeStruct((nb, L, D_ATT), BF16),
        compiler_params=_cp("arbitrary"),
        name="ctx_attn",
    )(q, k, v)


def _nbr_bias_tables(rpb, rows):
    ng = rows // Q_ROWS
    cols = np.arange(GRID_W)
    c_start = np.clip(cols - WIN_C // 2, 0, GRID_W - WIN_C)
    col_mask = (cols[None, :] >= c_start[:, None]) & (cols[None, :] < c_start[:, None] + WIN_C)
    dc_idx = np.clip(cols[None, :] - cols[:, None], -(WIN_C - 1), WIN_C - 1) + (WIN_C - 1)
    tables = []
    for g in (0, 1, ng - 1):
        r = Q_ROWS * g + np.arange(Q_ROWS)
        r0 = int(np.clip(Q_ROWS * g - WIN_R // 2, 0, rows - K_ROWS))
        start = np.clip(r - WIN_R // 2, 0, rows - WIN_R)
        krow = r0 + np.arange(K_ROWS)
        valid = (krow[None, :] >= start[:, None]) & (krow[None, :] < start[:, None] + WIN_R)
        dr = np.clip(krow[None, :] - r[:, None] + (WIN_R - 1), 0, 2 * WIN_R - 2)
        b = rpb[:, dr][:, :, :, dc_idx]
        mask = valid[:, :, None, None] & col_mask[None, None]
        b = jnp.where(mask[None], b.astype(F32), NEG)
        tables.append(b.transpose(0, 1, 3, 2, 4).reshape(N_HEADS, Q_ROWS * GRID_W, K_ROWS * GRID_W))
    return jnp.stack(tables)


def _nbr_attn_kernel(q_ref, k_ref, v_ref, kc_ref, vc_ref, bias_ref, o_ref, *, rows):
    g = pl.program_id(1)
    r0 = jnp.clip(Q_ROWS * g - WIN_R // 2, 0, rows - K_ROWS)
    start = pl.multiple_of(r0 * GRID_W, Q_ROWS * GRID_W)
    nk = K_ROWS * GRID_W
    kw = k_ref[0, pl.ds(start, nk), :]
    vw = v_ref[0, pl.ds(start, nk), :]
    kc = kc_ref[0].astype(BF16)
    vc = vc_ref[0].astype(BF16)
    q = q_ref[0]
    outs = []
    for h in range(N_HEADS):
        sl = slice(h * HEAD_DIM, (h + 1) * HEAD_DIM)
        qh = q[:, sl]
        s1 = _nt(qh, kw[:, sl]) + bias_ref[0, h]
        s2 = _nt(qh, kc[:, sl])
        m = jnp.maximum(jnp.max(s1, axis=-1, keepdims=True), jnp.max(s2, axis=-1, keepdims=True))
        e1 = jnp.exp(s1 - m)
        e2 = jnp.exp(s2 - m)
        den = jnp.sum(e1, axis=-1, keepdims=True) + jnp.sum(e2, axis=-1, keepdims=True)
        o = (jnp.dot(e1.astype(BF16), vw[:, sl], preferred_element_type=F32)
             + jnp.dot(e2.astype(BF16), vc[:, sl], preferred_element_type=F32))
        outs.append(o / den)
    o_ref[0] = jnp.concatenate(outs, axis=-1).astype(o_ref.dtype)


def _nbr_attention(q, k, v, kc, vc, bias):
    nb, L, _ = q.shape
    rows = L // GRID_W
    ng = rows // Q_ROWS
    past = kc.shape[1]
    tq = Q_ROWS * GRID_W
    full = pl.BlockSpec((1, L, D_ATT), lambda b, g: (b, 0, 0))
    ctx = pl.BlockSpec((1, past, D_ATT), lambda b, g: (b, 0, 0))
    qspec = pl.BlockSpec((1, tq, D_ATT), lambda b, g: (b, g, 0))
    bspec = pl.BlockSpec((1, N_HEADS, tq, K_ROWS * GRID_W),
                         lambda b, g: (jnp.minimum(g, 1) + g // (ng - 1), 0, 0, 0))
    return pl.pallas_call(
        functools.partial(_nbr_attn_kernel, rows=rows),
        grid=(nb, ng),
        in_specs=[qspec, full, full, ctx, ctx, bspec],
        out_specs=qspec,
        out_shape=jax.ShapeDtypeStruct((nb, L, D_ATT), BF16),
        compiler_params=_cp("arbitrary", "arbitrary"),
        name="nbr_attn",
    )(q, k, v, kc, vc, bias)


def _fill_padded(xp_ref, src_ref, L):
    z = jnp.zeros((HALO, xp_ref.shape[1]), F32)
    xp_ref[0:HALO, :] = z
    xp_ref[HALO:HALO + L, :] = src_ref[...]
    xp_ref[HALO + L:2 * HALO + L, :] = z


def _rg_gates_kernel(x_ref, cw_ref, cb_ref, w_ref, b_ref, lam_ref,
                     af_ref, bf_ref, ab_ref, bb_ref, xp_ref, *, L, tc):
    _fill_padded(xp_ref, x_ref, L)
    sp = jax.nn.softplus(-lam_ref[0])
    for t0 in range(0, L, tc):
        xc = cb_ref[...]
        for k in range(RG_CONV):
            xc = xc + cw_ref[k:k + 1, :] * xp_ref[pl.ds(HALO + t0 + k - RG_CONV // 2, tc), :]
        z = jnp.dot(xc.astype(BF16), w_ref[0], preferred_element_type=F32) + b_ref[0]
        for e, (a_ref, o_ref) in enumerate(((af_ref, bf_ref), (ab_ref, bb_ref))):
            r = jax.nn.sigmoid(z[:, e * LANES:(e + 1) * LANES])
            i = jax.nn.sigmoid(z[:, (2 + e) * LANES:(3 + e) * LANES])
            log_a = (-RG_C) * r * sp[:, e * LANES:(e + 1) * LANES]
            a = jnp.exp(log_a)
            one_minus_a2 = -jnp.tanh(log_a) * (a * a + 1.0)
            a_ref[pl.ds(t0, tc), :] = a
            o_ref[pl.ds(t0, tc), :] = jnp.sqrt(one_minus_a2) * i * xc


def _blockdiag2(w2):
    z = jnp.zeros_like(w2[0])
    return jnp.concatenate([jnp.concatenate([w2[0], z], 1), jnp.concatenate([z, w2[1]], 1)], 0)


def _rg_weights(wa, ba, wx, bx, lam):
    ws, bs, ls = [], [], []
    for hf in range(D_RNN // LANES):
        blk = slice(2 * hf, 2 * hf + 2)
        ch = slice(hf * LANES, (hf + 1) * LANES)
        ws.append(jnp.concatenate([_blockdiag2(wa[0, blk]), _blockdiag2(wa[1, blk]),
                                   _blockdiag2(wx[0, blk]), _blockdiag2(wx[1, blk])], axis=1))
        bs.append(jnp.concatenate([ba[0, ch], ba[1, ch], bx[0, ch], bx[1, ch]])[None])
        ls.append(jnp.concatenate([lam[0, ch], lam[1, ch]])[None])
    return jnp.stack(ws).astype(BF16), jnp.stack(bs), jnp.stack(ls)


def _rg_gates(xr_t, conv_w, conv_b, w_bd, b_cat, lam_cat):
    L, cols = xr_t.shape
    nh = D_RNN // LANES
    tc = min(L, 512)
    col = pl.BlockSpec((L, LANES), lambda c: (0, c))
    shape = jax.ShapeDtypeStruct((L, cols), F32)
    return pl.pallas_call(
        functools.partial(_rg_gates_kernel, L=L, tc=tc),
        grid=(cols // LANES,),
        in_specs=[col,
                  pl.BlockSpec((RG_CONV, LANES), lambda c: (0, c % nh)),
                  pl.BlockSpec((1, LANES), lambda c: (0, c % nh)),
                  pl.BlockSpec((1, LANES, 4 * LANES), lambda c: (c % nh, 0, 0)),
                  pl.BlockSpec((1, 1, 4 * LANES), lambda c: (c % nh, 0, 0)),
                  pl.BlockSpec((1, 1, 2 * LANES), lambda c: (c % nh, 0, 0))],
        out_specs=[col, col, col, col],
        out_shape=[shape, shape, shape, shape],
        scratch_shapes=[pltpu.VMEM((L + 2 * HALO, LANES), F32)],
        compiler_params=_cp("arbitrary"),
        name="rg_gates",
    )(xr_t, conv_w, conv_b.reshape(1, D_RNN), w_bd, b_cat, lam_cat)


def _rg_scan_kernel(af_ref, bf_ref, ab_ref, bb_ref, h0f_ref, h0b_ref, hf_ref, hb_ref, cf_ref, cb_ref, *, tc):
    @pl.when(pl.program_id(1) == 0)
    def _():
        cf_ref[...] = h0f_ref[...]
        cb_ref[...] = h0b_ref[...]

    def body(t, carry):
        hf, hb = carry
        hf = af_ref[pl.ds(t, 1), :] * hf + bf_ref[pl.ds(t, 1), :]
        hf_ref[pl.ds(t, 1), :] = hf
        tb = tc - 1 - t
        hb = ab_ref[pl.ds(tb, 1), :] * hb + bb_ref[pl.ds(tb, 1), :]
        hb_ref[pl.ds(tb, 1), :] = hb
        return hf, hb

    hf, hb = lax.fori_loop(0, tc, body, (cf_ref[...], cb_ref[...]), unroll=8)
    cf_ref[...] = hf
    cb_ref[...] = hb


def _rg_scan(a_f, b_f, a_b, b_b, h0f, h0b):
    L, cols = a_f.shape
    cw = min(cols, 1024)
    tc = min(L, 512)
    nt = L // tc
    fwd = pl.BlockSpec((tc, cw), lambda c, i: (i, c))
    bwd = pl.BlockSpec((tc, cw), lambda c, i: (nt - 1 - i, c))
    row = pl.BlockSpec((1, cw), lambda c, i: (0, c))
    shape = jax.ShapeDtypeStruct((L, cols), F32)
    return pl.pallas_call(
        functools.partial(_rg_scan_kernel, tc=tc),
        grid=(cols // cw, nt),
        in_specs=[fwd, fwd, bwd, bwd, row, row],
        out_specs=[fwd, bwd],
        out_shape=[shape, shape],
        scratch_shapes=[pltpu.VMEM((1, cw), F32), pltpu.VMEM((1, cw), F32)],
        compiler_params=_cp("arbitrary", "arbitrary"),
        name="rg_scan",
    )(a_f, b_f, a_b, b_b, h0f, h0b)


def _hy_filter_kernel(feat_ref, w1_ref, b1_ref, w2_ref, b2_ref, w3_ref, fr_ref, dec_ref, o_ref, *, tm):
    feats = feat_ref[...]
    dot = functools.partial(jnp.dot, precision=HIGHEST, preferred_element_type=F32)
    h = jnp.sin(fr_ref[0:1, :] * (dot(feats, w1_ref[...]) + b1_ref[...]))
    h = jnp.sin(fr_ref[1:2, :] * (dot(h, w2_ref[...]) + b2_ref[...]))
    h = dot(h, w3_ref[...])
    filt = h * jnp.exp(-feats[:, 0:1] * jnp.abs(dec_ref[...]))
    row = pl.program_id(0) * tm + lax.broadcasted_iota(jnp.int32, filt.shape, 0)
    lane = lax.broadcasted_iota(jnp.int32, filt.shape, 1)
    o_ref[...] = jnp.where((row == 0) & (lane >= 2 * D_HY), 0.0, filt)


def _hy_features(L):
    pos = np.arange(L, dtype=np.float64)
    t = np.linspace(0.0, 1.0, L)
    bands = (POS_EMB_DIM - 1) // 2
    f = np.linspace(1e-4, bands - 1, bands)
    ang = (2.0 * math.pi / L) * pos[:, None] * f[None, :]
    feats = np.concatenate([t[:, None], np.cos(ang), -np.sin(ang)], axis=-1)
    out = np.zeros((L, LANES), np.float32)
    out[:, :POS_EMB_DIM] = feats
    return out


def _hy_filters(L, w1, b1, w2, b2, w3, freq, decay):
    tm = min(L, 512)
    hid = w2.shape[0]
    n = w3.shape[1]
    w1p = jnp.zeros((LANES, hid), F32).at[:POS_EMB_DIM].set(w1)
    c = lambda s: pl.BlockSpec(s, lambda i: (0, 0))
    return pl.pallas_call(
        functools.partial(_hy_filter_kernel, tm=tm),
        grid=(L // tm,),
        in_specs=[pl.BlockSpec((tm, LANES), lambda i: (i, 0)),
                  c((LANES, hid)), c((1, hid)), c((hid, hid)), c((1, hid)), c((hid, n)), c((2, hid)), c((1, n))],
        out_specs=pl.BlockSpec((tm, n), lambda i: (i, 0)),
        out_shape=jax.ShapeDtypeStruct((L, n), F32),
        compiler_params=_cp("arbitrary"),
        name="hy_filter",
    )(jnp.asarray(_hy_features(L)), w1p, b1[None], w2, b2[None], w3, freq, decay.reshape(1, n))


def _conv3_kernel(v_ref, a_ref, b_ref, w_ref, cb_ref, ov_ref, oa_ref, ob_ref, xp_ref, *, L, tc):
    for p, (src, dst) in enumerate(((v_ref, ov_ref), (a_ref, oa_ref), (b_ref, ob_ref))):
        _fill_padded(xp_ref, src, L)
        for t0 in range(0, L, tc):
            y = cb_ref[p]
            for k in range(HY_CONV):
                y = y + w_ref[p, k:k + 1, :] * xp_ref[pl.ds(HALO + t0 + k - HY_CONV // 2, tc), :]
            dst[pl.ds(t0, tc), :] = y


def _hy_conv3(v_t, a_t, b_t, conv_w, conv_b):
    L, cols = v_t.shape
    nh = D_HY // LANES
    tc = min(L, 512)
    col = pl.BlockSpec((L, LANES), lambda c: (0, c))
    shape = jax.ShapeDtypeStruct((L, cols), F32)
    w = conv_w.reshape(HY_CONV, 3, D_HY).transpose(1, 0, 2)
    cb = conv_b.reshape(3, 1, D_HY)
    return pl.pallas_call(
        functools.partial(_conv3_kernel, L=L, tc=tc),
        grid=(cols // LANES,),
        in_specs=[col, col, col,
                  pl.BlockSpec((3, HY_CONV, LANES), lambda c: (0, 0, c % nh)),
                  pl.BlockSpec((3, 1, LANES), lambda c: (0, 0, c % nh))],
        out_specs=[col, col, col],
        out_shape=[shape, shape, shape],
        scratch_shapes=[pltpu.VMEM((L + 2 * HALO, LANES), F32)],
        compiler_params=_cp("arbitrary"),
        name="hy_conv3",
    )(v_t, a_t, b_t, w, cb)


def _phase(p, n2):
    ang = 2.0 * np.pi * (np.asarray(p, np.int64) % n2) / n2
    return np.cos(ang), np.sin(ang)


def _dft_direct_tables(L):
    k = np.arange(L)[:, None]
    t = np.arange(L)[None, :]
    c, s = _phase((2 * k + 1) * t, 4 * L)
    fwd = np.concatenate([c, -s], axis=0)
    inv = (1.0 / L) * np.concatenate([c.T, -s.T], axis=1)
    return fwd.astype(np.float32), inv.astype(np.float32)


def _dft_two_level_tables(L, n1):
    N = 2 * L
    n2 = N // n1
    h1 = n1 // 2
    i2 = np.arange(n2)[:, None, None]
    k1 = np.arange(n1)[None, :, None]
    i1 = np.arange(h1)[None, None, :]
    c, s = _phase(2 * n2 * i1 * k1 + 2 * i2 * k1 + n2 * i1 + i2, 2 * N)
    t1 = np.concatenate([c, -s], axis=1)
    ct, st = c.transpose(0, 2, 1), s.transpose(0, 2, 1)
    t1i = (2.0 / N) * np.concatenate([ct, -st], axis=2)
    kk = np.arange(n2 // 2)[:, None]
    nn = np.arange(n2)[None, :]
    c2, s2 = _phase(nn * kk, n2)
    t2 = np.block([[c2, s2], [-s2, c2]])
    t2i = np.block([[c2.T, -s2.T], [s2.T, c2.T]])
    return tuple(x.astype(np.float32) for x in (t1, t2, t2i, t1i))


def _filter_spectrum(fre, fim, order):
    fw = slice(order * D_HY, (order + 1) * D_HY)
    bw = slice((2 + order) * D_HY, (3 + order) * D_HY)
    return fre[:, fw] + fre[:, bw], fim[:, fw] - fim[:, bw]


def _spectral_product(xr, xi, fr, fi, nbc):
    yr, yi = [], []
    for b in range(nbc):
        sl = slice(b * D_HY, (b + 1) * D_HY)
        yr.append(xr[:, sl] * fr - xi[:, sl] * fi)
        yi.append(xr[:, sl] * fi + xi[:, sl] * fr)
    return jnp.concatenate([jnp.concatenate(yr, axis=1), jnp.concatenate(yi, axis=1)], axis=0)


def _matmul_kernel(a_ref, b_ref, o_ref):
    o_ref[...] = jnp.dot(a_ref[...], b_ref[...], precision=HIGHEST, preferred_element_type=F32)


def _hy_direct_kernel(z_ref, g_ref, f_ref, skip_ref, t_ref, ti_ref, o_ref, *, L, nbc, order):
    z = z_ref[...]
    x = jnp.dot(t_ref[...], z.astype(BF16), preferred_element_type=F32)
    fr, fi = _filter_spectrum(f_ref[0:L, :], f_ref[L:2 * L, :], order)
    y = _spectral_product(x[:L], x[L:], fr, fi, nbc).astype(BF16)
    conv = jnp.dot(ti_ref[...], y, preferred_element_type=F32)
    o_ref[...] = g_ref[...] * (conv + skip_ref[...] * z)


def _hyena_direct(z, g1, g2, filt, skip_t):
    L, cols = z.shape
    fwd, inv = _dft_direct_tables(L)
    n = filt.shape[1]
    spec = pl.pallas_call(
        _matmul_kernel,
        out_shape=jax.ShapeDtypeStruct((2 * L, n), F32),
        name="hy_fspec",
    )(jnp.asarray(fwd), filt)
    tn = min(cols, 4 * D_HY)
    nbc = tn // D_HY
    col = pl.BlockSpec((L, tn), lambda c: (0, c))
    const = lambda s: pl.BlockSpec(s, lambda c: (0, 0))
    t_bf, ti_bf = jnp.asarray(fwd).astype(BF16), jnp.asarray(inv).astype(BF16)
    for order, g in enumerate((g1, g2)):
        z = pl.pallas_call(
            functools.partial(_hy_direct_kernel, L=L, nbc=nbc, order=order),
            grid=(cols // tn,),
            in_specs=[col, col, const((2 * L, n)), const((1, tn)), const((2 * L, L)), const((L, 2 * L))],
            out_specs=col,
            out_shape=jax.ShapeDtypeStruct((L, cols), F32),
            compiler_params=_cp("arbitrary"),
            name="hy_direct",
        )(z, g, spec, skip_t[order:order + 1], t_bf, ti_bf)
    return z


def _hy_stage1_kernel(x_ref, t1_ref, re_ref, im_ref, *, G, cols, n1):
    for g in range(G):
        sl = slice(g * cols, (g + 1) * cols)
        a = jnp.dot(t1_ref[g], x_ref[:, sl].astype(BF16), preferred_element_type=F32)
        re_ref[:, sl] = a[:n1].astype(BF16)
        im_ref[:, sl] = a[n1:].astype(BF16)


def _hy_stage1(x, t1_bf, *, G=8):
    L, cols = x.shape
    n2, two_n1, h1 = t1_bf.shape
    n1 = two_n1 // 2
    xv = x.reshape(h1, n2 * cols)
    ospec = pl.BlockSpec((n1, G * cols), lambda j: (0, j))
    oshape = jax.ShapeDtypeStruct((n1, n2 * cols), BF16)
    return pl.pallas_call(
        functools.partial(_hy_stage1_kernel, G=G, cols=cols, n1=n1),
        grid=(n2 // G,),
        in_specs=[pl.BlockSpec((h1, G * cols), lambda j: (0, j)),
                  pl.BlockSpec((G, two_n1, h1), lambda j: (j, 0, 0))],
        out_specs=[ospec, ospec],
        out_shape=[oshape, oshape],
        compiler_params=_cp("arbitrary"),
        name="hy_stage1",
    )(xv, t1_bf)


def _hy_fspec2_kernel(re_ref, im_ref, t2_ref, fre_ref, fim_ref, *, kb, k2):
    for j in range(kb):
        a2 = jnp.concatenate([re_ref[j], im_ref[j]], axis=0)
        x = jnp.dot(t2_ref[...], a2, preferred_element_type=F32)
        fre_ref[j] = x[:k2]
        fim_ref[j] = x[k2:]


def _hy_stage2_kernel(re_ref, im_ref, fre_ref, fim_ref, t2_ref, t2i_ref, ore_ref, oim_ref, *, kb, k2, n2, nbc, order):
    for j in range(kb):
        a2 = jnp.concatenate([re_ref[j], im_ref[j]], axis=0)
        x = jnp.dot(t2_ref[...], a2, preferred_element_type=F32)
        fr, fi = _filter_spectrum(fre_ref[j], fim_ref[j], order)
        y = _spectral_product(x[:k2], x[k2:], fr, fi, nbc).astype(BF16)
        a = jnp.dot(t2i_ref[...], y, preferred_element_type=F32)
        ore_ref[j] = a[:n2].astype(BF16)
        oim_ref[j] = a[n2:].astype(BF16)


def _hy_stage3_kernel(re_ref, im_ref, t1i_ref, z_ref, g_ref, skip_ref, o_ref, *, G, cols):
    for g in range(G):
        sl = slice(g * cols, (g + 1) * cols)
        a2 = jnp.concatenate([re_ref[:, sl], im_ref[:, sl]], axis=0)
        conv = jnp.dot(t1i_ref[g], a2, preferred_element_type=F32)
        o_ref[:, sl] = g_ref[:, sl] * (conv + skip_ref[...] * z_ref[:, sl])


def _hyena_two_level(z, g1, g2, filt, skip_t, *, n1=FFT_N1, G=8, kb=4):
    L, cols = z.shape
    t1, t2, t2i, t1i = _dft_two_level_tables(L, n1)
    n2 = t1.shape[0]
    h1, k2 = n1 // 2, n2 // 2
    nbc = cols // D_HY
    t1_bf, t2_bf, t2i_bf, t1i_bf = (jnp.asarray(t).astype(BF16) for t in (t1, t2, t2i, t1i))
    nf = filt.shape[1]

    f_re, f_im = _hy_stage1(filt, t1_bf, G=G)
    blk = lambda c: pl.BlockSpec((kb, n2, c), lambda i: (i, 0, 0))
    hblk = lambda c: pl.BlockSpec((kb, k2, c), lambda i: (i, 0, 0))
    const = lambda s: pl.BlockSpec(s, lambda i: (0, 0))
    fshape = jax.ShapeDtypeStruct((n1, k2, nf), F32)
    fre, fim = pl.pallas_call(
        functools.partial(_hy_fspec2_kernel, kb=kb, k2=k2),
        grid=(n1 // kb,),
        in_specs=[blk(nf), blk(nf), const((n2, 2 * n2))],
        out_specs=[hblk(nf), hblk(nf)],
        out_shape=[fshape, fshape],
        compiler_params=_cp("arbitrary"),
        name="hy_fspec2",
    )(f_re.reshape(n1, n2, nf), f_im.reshape(n1, n2, nf), t2_bf)

    wide = pl.BlockSpec((n1, G * cols), lambda j: (0, j))
    half = pl.BlockSpec((h1, G * cols), lambda j: (0, j))
    ashape = jax.ShapeDtypeStruct((n1, n2, cols), BF16)
    for order, g in enumerate((g1, g2)):
        a_re, a_im = _hy_stage1(z, t1_bf, G=G)
        b_re, b_im = pl.pallas_call(
            functools.partial(_hy_stage2_kernel, kb=kb, k2=k2, n2=n2, nbc=nbc, order=order),
            grid=(n1 // kb,),
            in_specs=[blk(cols), blk(cols), hblk(nf), hblk(nf), const((n2, 2 * n2)), const((2 * n2, n2))],
            out_specs=[blk(cols), blk(cols)],
            out_shape=[ashape, ashape],
            compiler_params=_cp("arbitrary"),
            name="hy_stage2",
        )(a_re.reshape(n1, n2, cols), a_im.reshape(n1, n2, cols), fre, fim, t2_bf, t2i_bf)
        z = pl.pallas_call(
            functools.partial(_hy_stage3_kernel, G=G, cols=cols),
            grid=(n2 // G,),
            in_specs=[wide, wide, pl.BlockSpec((G, h1, 2 * n1), lambda j: (j, 0, 0)), half, half,
                      pl.BlockSpec((1, cols), lambda j: (0, 0))],
            out_specs=half,
            out_shape=jax.ShapeDtypeStruct((h1, n2 * cols), F32),
            compiler_params=_cp("arbitrary"),
            name="hy_stage3",
        )(b_re.reshape(n1, n2 * cols), b_im.reshape(n1, n2 * cols), t1i_bf,
          z.reshape(h1, n2 * cols), g.reshape(h1, n2 * cols), skip_t[order:order + 1]).reshape(L, cols)
    return z


def _hyena(hv_t, h1_t, h2_t, conv_w, conv_b, filt, skip):
    L, cols = hv_t.shape
    z, g1, g2 = _hy_conv3(hv_t, h1_t, h2_t, conv_w, conv_b)
    skip_t = jnp.tile(skip, (1, cols // D_HY))
    if L % (FFT_N1 * 8) == 0 and L >= 2048:
        return _hyena_two_level(z, g1, g2, filt, skip_t)
    return _hyena_direct(z, g1, g2, filt, skip_t)


def _route(logits):
    lane = lax.broadcasted_iota(jnp.int32, logits.shape, 1)
    is_g = (lane >= N_EXPERTS) & (lane < N_EXPERTS + N_GROUPS)
    big = jnp.int32(1 << 20)
    gl = jnp.where(is_g, logits, NEG)
    gmax = jnp.max(gl, axis=-1, keepdims=True)
    gidx = jnp.min(jnp.where(gl == gmax, lane - N_EXPERTS, big), axis=-1, keepdims=True)
    p_group = 1.0 / jnp.sum(jnp.where(is_g, jnp.exp(gl - gmax), 0.0), axis=-1, keepdims=True)
    sel = (lane < N_EXPERTS) & ((lane // EXPERTS_PER_GROUP) == gidx)
    el = jnp.where(sel, logits, NEG)
    ee = jnp.where(sel, jnp.exp(el - jnp.max(el, axis=-1, keepdims=True)), 0.0)
    pe = ee / jnp.sum(ee, axis=-1, keepdims=True)
    p1 = jnp.max(jnp.where(sel, pe, -1.0), axis=-1, keepdims=True)
    i1 = jnp.min(jnp.where(sel & (pe == p1), lane, big), axis=-1, keepdims=True)
    rest = sel & (lane != i1)
    p2 = jnp.max(jnp.where(rest, pe, -1.0), axis=-1, keepdims=True)
    i2 = jnp.min(jnp.where(rest & (pe == p2), lane, big), axis=-1, keepdims=True)
    den = p1 + p2
    return jnp.where(lane == i1, p_group * (p1 / den), jnp.where(lane == i2, p_group * (p2 / den), 0.0))


def _outproj_kernel(att_ref, hf_ref, hb_ref, gt_ref, hy_ref, x_ref, mod_ref, w_ref, g_ref, b_ref,
                    wr_ref, br_ref, x1_ref, u2_ref, cmb_ref):
    rg = (hf_ref[...] + hb_ref[...]) * jax.nn.gelu(gt_ref[...])
    mix = (jnp.dot(att_ref[0], w_ref[0:D_ATT, :], preferred_element_type=F32)
           + jnp.dot(rg.astype(BF16), w_ref[D_ATT:D_ATT + D_RNN, :], preferred_element_type=F32)
           + jnp.dot(hy_ref[...].astype(BF16), w_ref[D_ATT + D_RNN:, :], preferred_element_type=F32))
    m = mod_ref[0]
    x1 = _ln(ALPHA * x_ref[...] + m[2:3] * mix) * g_ref[...] + b_ref[...]
    u2 = _ln(x1) * (1.0 + m[4:5]) + m[3:4]
    logits = jnp.dot(u2, wr_ref[...], precision=HIGHEST, preferred_element_type=F32) + br_ref[...]
    x1_ref[...] = x1
    u2_ref[...] = u2.astype(BF16)
    cmb_ref[...] = _route(logits)


def _outproj(att, hf_t, hb_t, gt_t, hy_t, x2d, mod_l, w_bf, ln_g, ln_b, wr, br, *, nb, L, tm,
             cond_base, cond_per_batch):
    tpb = L // tm
    d = x2d.shape[1]
    cond = (lambda i: cond_base + i // tpb) if cond_per_batch else (lambda i: cond_base)
    t_spec = pl.BlockSpec((tm, D_RNN), lambda i: (i % tpb, i // tpb))
    tok = lambda c: pl.BlockSpec((tm, c), lambda i: (i, 0))
    const = lambda s: pl.BlockSpec(s, lambda i: (0, 0))
    n = nb * L
    return pl.pallas_call(
        _outproj_kernel,
        grid=(nb * tpb,),
        in_specs=[pl.BlockSpec((1, tm, D_ATT), lambda i: (i // tpb, i % tpb, 0)),
                  t_spec, t_spec, t_spec, t_spec, tok(d),
                  pl.BlockSpec((1, 6, d), lambda i: (cond(i), 0, 0)),
                  const((d, d)), const((1, d)), const((1, d)), const((d, LANES)), const((1, LANES))],
        out_specs=[tok(d), tok(d), tok(LANES)],
        out_shape=[jax.ShapeDtypeStruct((n, d), F32), jax.ShapeDtypeStruct((n, d), BF16),
                   jax.ShapeDtypeStruct((n, LANES), F32)],
        compiler_params=_cp("arbitrary"),
        name="outproj",
    )(att, hf_t, hb_t, gt_t, hy_t, x2d, mod_l, w_bf, ln_g[None], ln_b[None], wr, br)


def _moe_kernel(u_ref, cmb_ref, x1_ref, mod_ref, wgu_ref, wd_ref, g_ref, b_ref, o_ref, acc_ref):
    e = pl.program_id(1)

    @pl.when(e == 0)
    def _():
        acc_ref[...] = jnp.zeros_like(acc_ref)

    gu = jnp.dot(u_ref[...], wgu_ref[0], preferred_element_type=F32)
    gate = gu[:, :D_EXPERT]
    h = gate * jax.nn.sigmoid(gate) * gu[:, D_EXPERT:]
    cmb = cmb_ref[...]
    lane = lax.broadcasted_iota(jnp.int32, cmb.shape, 1)
    ce = jnp.sum(jnp.where(lane == e, cmb, 0.0), axis=-1, keepdims=True)
    acc_ref[...] += jnp.dot((h * ce).astype(BF16), wd_ref[0], preferred_element_type=F32)

    @pl.when(e == pl.num_programs(1) - 1)
    def _():
        m = mod_ref[0]
        o_ref[...] = _ln(ALPHA * x1_ref[...] + m[5:6] * acc_ref[...]) * g_ref[...] + b_ref[...]


def _moe(u2, cmb, x1, mod_l, wgu, wd, ln_g, ln_b, *, tm, tiles_per_cond, cond_base, cond_per_batch):
    n, d = x1.shape
    ne = wgu.shape[0]
    cond = (lambda i: cond_base + i // tiles_per_cond) if cond_per_batch else (lambda i: cond_base)
    tok = lambda c: pl.BlockSpec((tm, c), lambda i, e: (i, 0))
    const = lambda s: pl.BlockSpec(s, lambda i, e: (0, 0))
    return pl.pallas_call(
        _moe_kernel,
        grid=(n // tm, ne),
        in_specs=[tok(d), tok(LANES), tok(d),
                  pl.BlockSpec((1, 6, d), lambda i, e: (cond(i), 0, 0)),
                  pl.BlockSpec((1, d, 2 * D_EXPERT), lambda i, e: (e, 0, 0)),
                  pl.BlockSpec((1, D_EXPERT, d), lambda i, e: (e, 0, 0)),
                  const((1, d)), const((1, d))],
        out_specs=tok(d),
        out_shape=jax.ShapeDtypeStruct((n, d), F32),
        scratch_shapes=[pltpu.VMEM((tm, d), F32)],
        compiler_params=_cp("arbitrary", "arbitrary"),
        name="moe",
    )(u2, cmb, x1, mod_l, wgu, wd, ln_g[None], ln_b[None])


def _stream_layer(x2d, mod_l, p, *, nb, L, tm, cond_base, cond_per_batch, ctx, cache=None, h0=None):
    kw = dict(nb=nb, L=L, tm=tm, cond_base=cond_base, cond_per_batch=cond_per_batch)
    q, k, v, xr_t, gt_t, hv_t, h1_t, h2_t = _inproj(x2d, mod_l, p["w_in"], kv_dtype=F32 if ctx else BF16, **kw)
    if ctx:
        att = _ctx_attention(q, k, v)
    else:
        att = _nbr_attention(q, k, v, cache[0], cache[1], p["nbr_bias"])
    a_f, b_f, a_b, b_b = _rg_gates(xr_t, p["rg_conv_w"], p["rg_conv_b"], *p["rg_w"])
    hf_t, hb_t = _rg_scan(a_f, b_f, a_b, b_b, h0[0], h0[1])
    hy_t = _hyena(hv_t, h1_t, h2_t, p["hy_conv_w"], p["hy_conv_b"], p["hy_filt"][L], p["hy_skip"])
    x1, u2, cmb = _outproj(att, hf_t, hb_t, gt_t, hy_t, x2d, mod_l, p["w_out"], p["ln1_g"], p["ln1_b"],
                           p["router_w"], p["router_b"], **kw)
    tm_moe = min(1024, nb * L)
    x2 = _moe(u2, cmb, x1, mod_l, p["moe_wgu"], p["moe_wd"], p["ln2_g"], p["ln2_b"], tm=tm_moe,
              tiles_per_cond=max(L // tm_moe, 1), cond_base=cond_base, cond_per_batch=cond_per_batch)
    return x2, k, v, hf_t, hb_t


def kernel(x_prompt, x_sample, cache_k, cache_v, state_rglru, c, c_ctx, w_mod, b_mod, w_in, attn_rpb, rg_conv_w, rg_conv_b, rg_wa, rg_ba, rg_wx, rg_bx, rg_lambda, hy_conv_w, hy_conv_b, hy_w1, hy_b1, hy_w2, hy_b2, hy_w3, hy_freq, hy_decay, hy_skip, w_out, ln1_g, ln1_b, router_wg, router_bg, router_we, router_be, moe_w_gate, moe_w_up, moe_w_down, ln2_g, ln2_b):
    nbc, Lc, d = x_prompt.shape
    nbl, Ll, _ = x_sample.shape
    depth = w_mod.shape[0]
    past = cache_k.shape[2]

    cond8 = jnp.zeros((8, d), F32).at[0].set(c_ctx).at[1:1 + nbl].set(c)
    mod = _modulation(cond8, w_mod, b_mod).reshape(depth, 8, 6, d)

    xp = x_prompt.reshape(nbc * Lc, d)
    xs = x_sample.reshape(nbl * Ll, d)
    new_k, new_v, new_h = [], [], []
    for l in range(depth):
        pad = jnp.zeros((d, LANES - N_EXPERTS - N_GROUPS), F32)
        p = {
            "w_in": w_in[l].astype(BF16),
            "nbr_bias": _nbr_bias_tables(attn_rpb[l], Ll // GRID_W),
            "rg_conv_w": rg_conv_w[l], "rg_conv_b": rg_conv_b[l],
            "rg_w": _rg_weights(rg_wa[l], rg_ba[l], rg_wx[l], rg_bx[l], rg_lambda[l]),
            "hy_conv_w": hy_conv_w[l], "hy_conv_b": hy_conv_b[l], "hy_skip": hy_skip[l],
            "hy_filt": {L: _hy_filters(L, hy_w1[l], hy_b1[l], hy_w2[l], hy_b2[l], hy_w3[l], hy_freq[l], hy_decay[l])
                        for L in sorted({Lc, Ll})},
            "w_out": w_out[l].astype(BF16), "ln1_g": ln1_g[l], "ln1_b": ln1_b[l],
            "router_w": jnp.concatenate([router_we[l], router_wg[l], pad], axis=1),
            "router_b": jnp.concatenate([router_be[l], router_bg[l], pad[0]])[None],
            "moe_wgu": jnp.concatenate([moe_w_gate[l], moe_w_up[l]], axis=-1).astype(BF16),
            "moe_wd": moe_w_down[l].astype(BF16),
            "ln2_g": ln2_g[l], "ln2_b": ln2_b[l],
        }
        zeros_h = jnp.zeros((1, nbc * D_RNN), F32)
        xp, k, v, hf_t, hb_t = _stream_layer(
            xp, mod[l], p, nb=nbc, L=Lc, tm=Lc, cond_base=0, cond_per_batch=False, ctx=True,
            h0=(zeros_h, zeros_h))
        new_k.append(k.reshape(nbc, Lc, N_HEADS, HEAD_DIM))
        new_v.append(v.reshape(nbc, Lc, N_HEADS, HEAD_DIM))
        new_h.append(jnp.stack([hf_t[Lc - 1].reshape(nbc, D_RNN), hb_t[0].reshape(nbc, D_RNN)], axis=1))

        cache = (cache_k[:, l].reshape(nbl, past, D_ATT), cache_v[:, l].reshape(nbl, past, D_ATT))
        h0 = (state_rglru[:, l, 0].reshape(1, nbl * D_RNN), state_rglru[:, l, 1].reshape(1, nbl * D_RNN))
        xs, _, _, _, _ = _stream_layer(
            xs, mod[l], p, nb=nbl, L=Ll, tm=min(Ll, 512), cond_base=1, cond_per_batch=True, ctx=False,
            cache=cache, h0=h0)

    return (xp.reshape(nbc, Lc, d), xs.reshape(nbl, Ll, d),
            jnp.stack(new_k, axis=1), jnp.stack(new_v, axis=1), jnp.stack(new_h, axis=1))
```

```python
import functools
import math

import numpy as np
import jax
import jax.numpy as jnp
from jax import lax
from jax.experimental import pallas as pl
from jax.experimental.pallas import tpu as pltpu

F32 = jnp.float32
BF16 = jnp.bfloat16
HIGHEST = lax.Precision.HIGHEST

DEPTH = 2
D_MODEL = 1024
N_HEADS = 8
HEAD_DIM = 64
D_ATT = N_HEADS * HEAD_DIM
D_RNN = 256
D_HY = 256
GRID_W = 64
WIN_R = 8
WIN_C = 16
RG_CONV = 4
RG_C = 8.0
HY_CONV = 3
POS_EMB_DIM = 33
N_GROUPS = 4
EXPERTS_PER_GROUP = 4
N_EXPERTS = 16
D_EXPERT = 256
ALPHA = (2.0 * DEPTH) ** 0.25
LN_EPS = 1e-5
NEG = -1e30

Q_ROWS = 4
K_ROWS = Q_ROWS + WIN_R
LANES = 128
HALO = 8
MOE_TILE = 512
MOE_BLK = 128
FFT_N1 = 64
VMEM_LIMIT = 52 * 1024 * 1024


def _cp(*sem):
    return pltpu.CompilerParams(dimension_semantics=sem, vmem_limit_bytes=VMEM_LIMIT)


def _ln(x):
    mu = jnp.mean(x, axis=-1, keepdims=True)
    xc = x - mu
    var = jnp.mean(xc * xc, axis=-1, keepdims=True)
    return xc * lax.rsqrt(var + LN_EPS)


def _mod_kernel(c_ref, w_ref, b_ref, o_ref):
    c = c_ref[...]
    s = c * jax.nn.sigmoid(c)
    o_ref[0] = jnp.dot(s, w_ref[0], precision=HIGHEST, preferred_element_type=F32) + b_ref[0]


def _modulation(cond8, w_mod, b_mod):
    depth, d, n = w_mod.shape
    tn = n // 4
    return pl.pallas_call(
        _mod_kernel,
        grid=(depth, n // tn),
        in_specs=[pl.BlockSpec((8, d), lambda l, j: (0, 0)),
                  pl.BlockSpec((1, d, tn), lambda l, j: (l, 0, j)),
                  pl.BlockSpec((1, 1, tn), lambda l, j: (l, 0, j))],
        out_specs=pl.BlockSpec((1, 8, tn), lambda l, j: (l, 0, j)),
        out_shape=jax.ShapeDtypeStruct((depth, 8, n), F32),
        compiler_params=_cp("arbitrary", "arbitrary"),
        name="modulation",
    )(cond8, w_mod, b_mod.reshape(depth, 1, n))


def _inproj_kernel(x_ref, mod_ref, w_ref, q_ref, k_ref, v_ref, xr_ref, gt_ref, hv_ref, h1_ref, h2_ref):
    m = mod_ref[0]
    u = (_ln(x_ref[...]) * (1.0 + m[1:2]) + m[0:1]).astype(BF16)

    def proj(lo, hi):
        return jnp.dot(u, w_ref[:, lo:hi], preferred_element_type=F32)

    o = 0
    q_ref[0] = (proj(o, o + D_ATT) * (HEAD_DIM ** -0.5)).astype(q_ref.dtype)
    o += D_ATT
    k_ref[0] = proj(o, o + D_ATT).astype(k_ref.dtype)
    o += D_ATT
    v_ref[0] = proj(o, o + D_ATT).astype(v_ref.dtype)
    o += D_ATT
    for ref in (xr_ref, gt_ref, hv_ref, h1_ref, h2_ref):
        ref[...] = proj(o, o + D_RNN)
        o += D_RNN


def _inproj(x2d, mod_l, w_bf, *, nb, L, tm, cond_base, cond_per_batch, kv_dtype):
    tpb = L // tm
    d = x2d.shape[1]
    d_in = w_bf.shape[1]
    cond = (lambda i: cond_base + i // tpb) if cond_per_batch else (lambda i: cond_base)
    att_spec = pl.BlockSpec((1, tm, D_ATT), lambda i: (i // tpb, i % tpb, 0))
    t_spec = pl.BlockSpec((tm, D_RNN), lambda i: (i % tpb, i // tpb))
    att_shape = lambda dt: jax.ShapeDtypeStruct((nb, L, D_ATT), dt)
    t_shape = jax.ShapeDtypeStruct((L, nb * D_RNN), F32)
    return pl.pallas_call(
        _inproj_kernel,
        grid=(nb * tpb,),
        in_specs=[pl.BlockSpec((tm, d), lambda i: (i, 0)),
                  pl.BlockSpec((1, 6, d), lambda i: (cond(i), 0, 0)),
                  pl.BlockSpec((d, d_in), lambda i: (0, 0))],
        out_specs=[att_spec, att_spec, att_spec, t_spec, t_spec, t_spec, t_spec, t_spec],
        out_shape=[att_shape(BF16), att_shape(kv_dtype), att_shape(kv_dtype),
                   t_shape, t_shape, t_shape, t_shape, t_shape],
        compiler_params=_cp("arbitrary"),
        name="inproj",
    )(x2d, mod_l, w_bf)


def _nt(a, b):
    return lax.dot_general(a, b, (((1,), (1,)), ((), ())), preferred_element_type=F32)


def _ctx_attn_kernel(q_ref, k_ref, v_ref, o_ref):
    q = q_ref[0]
    k = k_ref[0].astype(BF16)
    v = v_ref[0].astype(BF16)
    outs = []
    for h in range(N_HEADS):
        sl = slice(h * HEAD_DIM, (h + 1) * HEAD_DIM)
        s = _nt(q[:, sl], k[:, sl])
        e = jnp.exp(s - jnp.max(s, axis=-1, keepdims=True))
        den = jnp.sum(e, axis=-1, keepdims=True)
        outs.append(jnp.dot(e.astype(BF16), v[:, sl], preferred_element_type=F32) / den)
    o_ref[0] = jnp.concatenate(outs, axis=-1).astype(o_ref.dtype)


def _ctx_attention(q, k, v):
    nb, L, _ = q.shape
    spec = pl.BlockSpec((1, L, D_ATT), lambda b: (b, 0, 0))
    return pl.pallas_call(
        _ctx_attn_kernel,
        grid=(nb,),
        in_specs=[spec, spec, spec],
        out_specs=spec,
        out_shape=jax.ShapeDtypeStruct((nb, L, D_ATT), BF16),
        compiler_params=_cp("arbitrary"),
        name="ctx_attn",
    )(q, k, v)


def _nbr_bias_tables(rpb, rows):
    ng = rows // Q_ROWS
    cols = np.arange(GRID_W)
    c_start = np.clip(cols - WIN_C // 2, 0, GRID_W - WIN_C)
    col_mask = (cols[None, :] >= c_start[:, None]) & (cols[None, :] < c_start[:, None] + WIN_C)
    dc_idx = np.clip(cols[None, :] - cols[:, None], -(WIN_C - 1), WIN_C - 1) + (WIN_C - 1)
    sel = (dc_idx[None] == np.arange(2 * WIN_C - 1)[:, None, None]).astype(np.float32)
    colb = jnp.einsum('hab,bqk->haqk', rpb.astype(F32), jnp.asarray(sel), precision=HIGHEST)
    colb = jnp.where(col_mask[None, None], colb, NEG)
    masked = jnp.full((N_HEADS, GRID_W, GRID_W), NEG, F32)
    tables = []
    for g in (0, 1, ng - 1):
        r = Q_ROWS * g + np.arange(Q_ROWS)
        r0 = int(np.clip(Q_ROWS * g - WIN_R // 2, 0, rows - K_ROWS))
        start = np.clip(r - WIN_R // 2, 0, rows - WIN_R)
        krow = r0 + np.arange(K_ROWS)
        valid = (krow[None, :] >= start[:, None]) & (krow[None, :] < start[:, None] + WIN_R)
        dr = krow[None, :] - r[:, None] + (WIN_R - 1)
        tables.append(jnp.concatenate(
            [jnp.concatenate([colb[:, int(dr[j, i])] if valid[j, i] else masked for i in range(K_ROWS)], axis=-1)
             for j in range(Q_ROWS)], axis=-2))
    return jnp.stack(tables)


def _nbr_attn_kernel(q_ref, k_ref, v_ref, kc_ref, vc_ref, bias_ref, o_ref, *, rows):
    g = pl.program_id(1)
    r0 = jnp.clip(Q_ROWS * g - WIN_R // 2, 0, rows - K_ROWS)
    start = pl.multiple_of(r0 * GRID_W, Q_ROWS * GRID_W)
    nk = K_ROWS * GRID_W
    kw = k_ref[0, pl.ds(start, nk), :]
    vw = v_ref[0, pl.ds(start, nk), :]
    kc = kc_ref[0].astype(BF16)
    vc = vc_ref[0].astype(BF16)
    q = q_ref[0]
    outs = []
    for h in range(N_HEADS):
        sl = slice(h * HEAD_DIM, (h + 1) * HEAD_DIM)
        qh = q[:, sl]
        s1 = _nt(qh, kw[:, sl]) + bias_ref[0, h]
        s2 = _nt(qh, kc[:, sl])
        m = jnp.maximum(jnp.max(s1, axis=-1, keepdims=True), jnp.max(s2, axis=-1, keepdims=True))
        e1 = jnp.exp(s1 - m)
        e2 = jnp.exp(s2 - m)
        den = jnp.sum(e1, axis=-1, keepdims=True) + jnp.sum(e2, axis=-1, keepdims=True)
        o = (jnp.dot(e1.astype(BF16), vw[:, sl], preferred_element_type=F32)
             + jnp.dot(e2.astype(BF16), vc[:, sl], preferred_element_type=F32))
        outs.append(o / den)
    o_ref[0] = jnp.concatenate(outs, axis=-1).astype(o_ref.dtype)


def _nbr_attention(q, k, v, kc, vc, bias):
    nb, L, _ = q.shape
    rows = L // GRID_W
    ng = rows // Q_ROWS
    past = kc.shape[1]
    tq = Q_ROWS * GRID_W
    full = pl.BlockSpec((1, L, D_ATT), lambda b, g: (b, 0, 0))
    ctx = pl.BlockSpec((1, past, D_ATT), lambda b, g: (b, 0, 0))
    qspec = pl.BlockSpec((1, tq, D_ATT), lambda b, g: (b, g, 0))
    bspec = pl.BlockSpec((1, N_HEADS, tq, K_ROWS * GRID_W),
                         lambda b, g: (jnp.minimum(g, 1) + g // (ng - 1), 0, 0, 0))
    return pl.pallas_call(
        functools.partial(_nbr_attn_kernel, rows=rows),
        grid=(nb, ng),
        in_specs=[qspec, full, full, ctx, ctx, bspec],
        out_specs=qspec,
        out_shape=jax.ShapeDtypeStruct((nb, L, D_ATT), BF16),
        compiler_params=_cp("arbitrary", "arbitrary"),
        name="nbr_attn",
    )(q, k, v, kc, vc, bias)


def _fill_padded(xp_ref, src_ref, L):
    z = jnp.zeros((HALO, xp_ref.shape[1]), F32)
    xp_ref[0:HALO, :] = z
    xp_ref[HALO:HALO + L, :] = src_ref[...]
    xp_ref[HALO + L:2 * HALO + L, :] = z


def _rg_gates_kernel(x_ref, cw_ref, cb_ref, w_ref, b_ref, lam_ref,
                     af_ref, bf_ref, ab_ref, bb_ref, xp_ref, *, L, tc):
    _fill_padded(xp_ref, x_ref, L)
    sp = jax.nn.softplus(-lam_ref[0])
    for t0 in range(0, L, tc):
        xc = cb_ref[...]
        for k in range(RG_CONV):
            xc = xc + cw_ref[k:k + 1, :] * xp_ref[pl.ds(HALO + t0 + k - RG_CONV // 2, tc), :]
        z = jnp.dot(xc.astype(BF16), w_ref[0], preferred_element_type=F32) + b_ref[0]
        for e, (a_ref, o_ref) in enumerate(((af_ref, bf_ref), (ab_ref, bb_ref))):
            r = jax.nn.sigmoid(z[:, e * LANES:(e + 1) * LANES])
            i = jax.nn.sigmoid(z[:, (2 + e) * LANES:(3 + e) * LANES])
            log_a = (-RG_C) * r * sp[:, e * LANES:(e + 1) * LANES]
            a = jnp.exp(log_a)
            one_minus_a2 = -jnp.tanh(log_a) * (a * a + 1.0)
            a_ref[pl.ds(t0, tc), :] = a
            o_ref[pl.ds(t0, tc), :] = jnp.sqrt(one_minus_a2) * i * xc


def _blockdiag2(w2):
    z = jnp.zeros_like(w2[0])
    return jnp.concatenate([jnp.concatenate([w2[0], z], 1), jnp.concatenate([z, w2[1]], 1)], 0)


def _rg_weights(wa, ba, wx, bx, lam):
    ws, bs, ls = [], [], []
    for hf in range(D_RNN // LANES):
        blk = slice(2 * hf, 2 * hf + 2)
        ch = slice(hf * LANES, (hf + 1) * LANES)
        ws.append(jnp.concatenate([_blockdiag2(wa[0, blk]), _blockdiag2(wa[1, blk]),
                                   _blockdiag2(wx[0, blk]), _blockdiag2(wx[1, blk])], axis=1))
        bs.append(jnp.concatenate([ba[0, ch], ba[1, ch], bx[0, ch], bx[1, ch]])[None])
        ls.append(jnp.concatenate([lam[0, ch], lam[1, ch]])[None])
    return jnp.stack(ws).astype(BF16), jnp.stack(bs), jnp.stack(ls)


def _rg_gates(xr_t, conv_w, conv_b, w_bd, b_cat, lam_cat):
    L, cols = xr_t.shape
    nh = D_RNN // LANES
    tc = min(L, 512)
    col = pl.BlockSpec((L, LANES), lambda c: (0, c))
    shape = jax.ShapeDtypeStruct((L, cols), F32)
    return pl.pallas_call(
        functools.partial(_rg_gates_kernel, L=L, tc=tc),
        grid=(cols // LANES,),
        in_specs=[col,
                  pl.BlockSpec((RG_CONV, LANES), lambda c: (0, c % nh)),
                  pl.BlockSpec((1, LANES), lambda c: (0, c % nh)),
                  pl.BlockSpec((1, LANES, 4 * LANES), lambda c: (c % nh, 0, 0)),
                  pl.BlockSpec((1, 1, 4 * LANES), lambda c: (c % nh, 0, 0)),
                  pl.BlockSpec((1, 1, 2 * LANES), lambda c: (c % nh, 0, 0))],
        out_specs=[col, col, col, col],
        out_shape=[shape, shape, shape, shape],
        scratch_shapes=[pltpu.VMEM((L + 2 * HALO, LANES), F32)],
        compiler_params=_cp("arbitrary"),
        name="rg_gates",
    )(xr_t, conv_w, conv_b.reshape(1, D_RNN), w_bd, b_cat, lam_cat)


def _rg_scan_kernel(af_ref, bf_ref, ab_ref, bb_ref, h0f_ref, h0b_ref, hf_ref, hb_ref, cf_ref, cb_ref, *, tc):
    @pl.when(pl.program_id(1) == 0)
    def _():
        cf_ref[...] = h0f_ref[...]
        cb_ref[...] = h0b_ref[...]

    def body(t, carry):
        hf, hb = carry
        hf = af_ref[pl.ds(t, 1), :] * hf + bf_ref[pl.ds(t, 1), :]
        hf_ref[pl.ds(t, 1), :] = hf
        tb = tc - 1 - t
        hb = ab_ref[pl.ds(tb, 1), :] * hb + bb_ref[pl.ds(tb, 1), :]
        hb_ref[pl.ds(tb, 1), :] = hb
        return hf, hb

    hf, hb = lax.fori_loop(0, tc, body, (cf_ref[...], cb_ref[...]), unroll=8)
    cf_ref[...] = hf
    cb_ref[...] = hb


def _rg_scan(a_f, b_f, a_b, b_b, h0f, h0b):
    L, cols = a_f.shape
    cw = min(cols, 1024)
    tc = min(L, 512)
    nt = L // tc
    fwd = pl.BlockSpec((tc, cw), lambda c, i: (i, c))
    bwd = pl.BlockSpec((tc, cw), lambda c, i: (nt - 1 - i, c))
    row = pl.BlockSpec((1, cw), lambda c, i: (0, c))
    shape = jax.ShapeDtypeStruct((L, cols), F32)
    return pl.pallas_call(
        functools.partial(_rg_scan_kernel, tc=tc),
        grid=(cols // cw, nt),
        in_specs=[fwd, fwd, bwd, bwd, row, row],
        out_specs=[fwd, bwd],
        out_shape=[shape, shape],
        scratch_shapes=[pltpu.VMEM((1, cw), F32), pltpu.VMEM((1, cw), F32)],
        compiler_params=_cp("arbitrary", "arbitrary"),
        name="rg_scan",
    )(a_f, b_f, a_b, b_b, h0f, h0b)


def _hy_filter_kernel(feat_ref, w1_ref, b1_ref, w2_ref, b2_ref, w3_ref, fr_ref, dec_ref, o_ref, *, tm):
    feats = feat_ref[...]
    dot = functools.partial(jnp.dot, precision=HIGHEST, preferred_element_type=F32)
    h = jnp.sin(fr_ref[0:1, :] * (dot(feats, w1_ref[...]) + b1_ref[...]))
    h = jnp.sin(fr_ref[1:2, :] * (dot(h, w2_ref[...]) + b2_ref[...]))
    h = dot(h, w3_ref[...])
    filt = h * jnp.exp(-feats[:, 0:1] * jnp.abs(dec_ref[...]))
    row = pl.program_id(0) * tm + lax.broadcasted_iota(jnp.int32, filt.shape, 0)
    lane = lax.broadcasted_iota(jnp.int32, filt.shape, 1)
    o_ref[...] = jnp.where((row == 0) & (lane >= 2 * D_HY), 0.0, filt)


def _hy_features(L):
    pos = np.arange(L, dtype=np.float64)
    t = np.linspace(0.0, 1.0, L)
    bands = (POS_EMB_DIM - 1) // 2
    f = np.linspace(1e-4, bands - 1, bands)
    ang = (2.0 * math.pi / L) * pos[:, None] * f[None, :]
    feats = np.concatenate([t[:, None], np.cos(ang), -np.sin(ang)], axis=-1)
    out = np.zeros((L, LANES), np.float32)
    out[:, :POS_EMB_DIM] = feats
    return out


def _hy_filters(L, w1, b1, w2, b2, w3, freq, decay):
    tm = min(L, 512)
    hid = w2.shape[0]
    n = w3.shape[1]
    w1p = jnp.zeros((LANES, hid), F32).at[:POS_EMB_DIM].set(w1)
    c = lambda s: pl.BlockSpec(s, lambda i: (0, 0))
    return pl.pallas_call(
        functools.partial(_hy_filter_kernel, tm=tm),
        grid=(L // tm,),
        in_specs=[pl.BlockSpec((tm, LANES), lambda i: (i, 0)),
                  c((LANES, hid)), c((1, hid)), c((hid, hid)), c((1, hid)), c((hid, n)), c((2, hid)), c((1, n))],
        out_specs=pl.BlockSpec((tm, n), lambda i: (i, 0)),
        out_shape=jax.ShapeDtypeStruct((L, n), F32),
        compiler_params=_cp("arbitrary"),
        name="hy_filter",
    )(jnp.asarray(_hy_features(L)), w1p, b1[None], w2, b2[None], w3, freq, decay.reshape(1, n))


def _conv3_kernel(v_ref, a_ref, b_ref, w_ref, cb_ref, ov_ref, oa_ref, ob_ref, xp_ref, *, L, tc):
    for p, (src, dst) in enumerate(((v_ref, ov_ref), (a_ref, oa_ref), (b_ref, ob_ref))):
        _fill_padded(xp_ref, src, L)
        for t0 in range(0, L, tc):
            y = cb_ref[p]
            for k in range(HY_CONV):
                y = y + w_ref[p, k:k + 1, :] * xp_ref[pl.ds(HALO + t0 + k - HY_CONV // 2, tc), :]
            dst[pl.ds(t0, tc), :] = y


def _hy_conv3(v_t, a_t, b_t, conv_w, conv_b):
    L, cols = v_t.shape
    nh = D_HY // LANES
    tc = min(L, 512)
    col = pl.BlockSpec((L, LANES), lambda c: (0, c))
    shape = jax.ShapeDtypeStruct((L, cols), F32)
    w = conv_w.reshape(HY_CONV, 3, D_HY).transpose(1, 0, 2)
    cb = conv_b.reshape(3, 1, D_HY)
    return pl.pallas_call(
        functools.partial(_conv3_kernel, L=L, tc=tc),
        grid=(cols // LANES,),
        in_specs=[col, col, col,
                  pl.BlockSpec((3, HY_CONV, LANES), lambda c: (0, 0, c % nh)),
                  pl.BlockSpec((3, 1, LANES), lambda c: (0, 0, c % nh))],
        out_specs=[col, col, col],
        out_shape=[shape, shape, shape],
        scratch_shapes=[pltpu.VMEM((L + 2 * HALO, LANES), F32)],
        compiler_params=_cp("arbitrary"),
        name="hy_conv3",
    )(v_t, a_t, b_t, w, cb)


def _phase(p, n2):
    ang = 2.0 * np.pi * (np.asarray(p, np.int64) % n2) / n2
    return np.cos(ang), np.sin(ang)


def _dft_direct_tables(L):
    k = np.arange(L)[:, None]
    t = np.arange(L)[None, :]
    c, s = _phase((2 * k + 1) * t, 4 * L)
    fwd = np.concatenate([c, -s], axis=0)
    inv = (1.0 / L) * np.concatenate([c.T, -s.T], axis=1)
    return fwd.astype(np.float32), inv.astype(np.float32)


def _dft_two_level_tables(L, n1):
    N = 2 * L
    n2 = N // n1
    h1 = n1 // 2
    i2 = np.arange(n2)[:, None, None]
    k1 = np.arange(n1)[None, :, None]
    i1 = np.arange(h1)[None, None, :]
    c, s = _phase(2 * n2 * i1 * k1 + 2 * i2 * k1 + n2 * i1 + i2, 2 * N)
    t1 = np.concatenate([c, -s], axis=1)
    ct, st = c.transpose(0, 2, 1), s.transpose(0, 2, 1)
    t1i = (2.0 / N) * np.concatenate([ct, -st], axis=2)
    kk = np.arange(n2 // 2)[:, None]
    nn = np.arange(n2)[None, :]
    c2, s2 = _phase(nn * kk, n2)
    t2 = np.block([[c2, s2], [-s2, c2]])
    t2i = np.block([[c2.T, -s2.T], [s2.T, c2.T]])
    return tuple(x.astype(np.float32) for x in (t1, t2, t2i, t1i))


def _filter_spectrum(fre, fim, order):
    fw = slice(order * D_HY, (order + 1) * D_HY)
    bw = slice((2 + order) * D_HY, (3 + order) * D_HY)
    return fre[:, fw] + fre[:, bw], fim[:, fw] - fim[:, bw]


def _spectral_product(xr, xi, fr, fi, nbc):
    yr, yi = [], []
    for b in range(nbc):
        sl = slice(b * D_HY, (b + 1) * D_HY)
        yr.append(xr[:, sl] * fr - xi[:, sl] * fi)
        yi.append(xr[:, sl] * fi + xi[:, sl] * fr)
    return jnp.concatenate([jnp.concatenate(yr, axis=1), jnp.concatenate(yi, axis=1)], axis=0)


def _matmul_kernel(a_ref, b_ref, o_ref):
    o_ref[...] = jnp.dot(a_ref[...], b_ref[...], precision=HIGHEST, preferred_element_type=F32)


def _hy_direct_kernel(z_ref, g_ref, f_ref, skip_ref, t_ref, ti_ref, o_ref, *, L, nbc, order):
    z = z_ref[...]
    x = jnp.dot(t_ref[...], z.astype(BF16), preferred_element_type=F32)
    fr, fi = _filter_spectrum(f_ref[0:L, :], f_ref[L:2 * L, :], order)
    y = _spectral_product(x[:L], x[L:], fr, fi, nbc).astype(BF16)
    conv = jnp.dot(ti_ref[...], y, preferred_element_type=F32)
    o_ref[...] = g_ref[...] * (conv + skip_ref[...] * z)


def _hyena_direct(z, g1, g2, filt, skip_t):
    L, cols = z.shape
    fwd, inv = _dft_direct_tables(L)
    n = filt.shape[1]
    spec = pl.pallas_call(
        _matmul_kernel,
        out_shape=jax.ShapeDtypeStruct((2 * L, n), F32),
        name="hy_fspec",
    )(jnp.asarray(fwd), filt)
    tn = min(cols, 4 * D_HY)
    nbc = tn // D_HY
    col = pl.BlockSpec((L, tn), lambda c: (0, c))
    const = lambda s: pl.BlockSpec(s, lambda c: (0, 0))
    t_bf, ti_bf = jnp.asarray(fwd).astype(BF16), jnp.asarray(inv).astype(BF16)
    for order, g in enumerate((g1, g2)):
        z = pl.pallas_call(
            functools.partial(_hy_direct_kernel, L=L, nbc=nbc, order=order),
            grid=(cols // tn,),
            in_specs=[col, col, const((2 * L, n)), const((1, tn)), const((2 * L, L)), const((L, 2 * L))],
            out_specs=col,
            out_shape=jax.ShapeDtypeStruct((L, cols), F32),
            compiler_params=_cp("arbitrary"),
            name="hy_direct",
        )(z, g, spec, skip_t[order:order + 1], t_bf, ti_bf)
    return z


def _hy_stage1_kernel(x_ref, t1_ref, re_ref, im_ref, *, G, cols, n1):
    for g in range(G):
        sl = slice(g * cols, (g + 1) * cols)
        a = jnp.dot(t1_ref[g], x_ref[:, sl].astype(BF16), preferred_element_type=F32)
        re_ref[:, sl] = a[:n1].astype(BF16)
        im_ref[:, sl] = a[n1:].astype(BF16)


def _hy_stage1(x, t1_bf, *, G=8):
    L, cols = x.shape
    n2, two_n1, h1 = t1_bf.shape
    n1 = two_n1 // 2
    xv = x.reshape(h1, n2 * cols)
    ospec = pl.BlockSpec((n1, G * cols), lambda j: (0, j))
    oshape = jax.ShapeDtypeStruct((n1, n2 * cols), BF16)
    return pl.pallas_call(
        functools.partial(_hy_stage1_kernel, G=G, cols=cols, n1=n1),
        grid=(n2 // G,),
        in_specs=[pl.BlockSpec((h1, G * cols), lambda j: (0, j)),
                  pl.BlockSpec((G, two_n1, h1), lambda j: (j, 0, 0))],
        out_specs=[ospec, ospec],
        out_shape=[oshape, oshape],
        compiler_params=_cp("arbitrary"),
        name="hy_stage1",
    )(xv, t1_bf)


def _hy_fspec2_kernel(re_ref, im_ref, t2_ref, fre_ref, fim_ref, *, kb, k2):
    for j in range(kb):
        a2 = jnp.concatenate([re_ref[j], im_ref[j]], axis=0)
        x = jnp.dot(t2_ref[...], a2, preferred_element_type=F32)
        fre_ref[j] = x[:k2]
        fim_ref[j] = x[k2:]


def _hy_stage2_kernel(re_ref, im_ref, fre_ref, fim_ref, t2_ref, t2i_ref, ore_ref, oim_ref, *, kb, k2, n2, nbc, order):
    for j in range(kb):
        a2 = jnp.concatenate([re_ref[j], im_ref[j]], axis=0)
        x = jnp.dot(t2_ref[...], a2, preferred_element_type=F32)
        fr, fi = _filter_spectrum(fre_ref[j], fim_ref[j], order)
        y = _spectral_product(x[:k2], x[k2:], fr, fi, nbc).astype(BF16)
        a = jnp.dot(t2i_ref[...], y, preferred_element_type=F32)
        ore_ref[j] = a[:n2].astype(BF16)
        oim_ref[j] = a[n2:].astype(BF16)


def _hy_stage3_kernel(re_ref, im_ref, t1i_ref, z_ref, g_ref, skip_ref, o_ref, *, G, cols):
    for g in range(G):
        sl = slice(g * cols, (g + 1) * cols)
        a2 = jnp.concatenate([re_ref[:, sl], im_ref[:, sl]], axis=0)
        conv = jnp.dot(t1i_ref[g], a2, preferred_element_type=F32)
        o_ref[:, sl] = g_ref[:, sl] * (conv + skip_ref[...] * z_ref[:, sl])


def _hyena_two_level(z, g1, g2, filt, skip_t, *, n1=FFT_N1, G=8, kb=4):
    L, cols = z.shape
    t1, t2, t2i, t1i = _dft_two_level_tables(L, n1)
    n2 = t1.shape[0]
    h1, k2 = n1 // 2, n2 // 2
    nbc = cols // D_HY
    t1_bf, t2_bf, t2i_bf, t1i_bf = (jnp.asarray(t).astype(BF16) for t in (t1, t2, t2i, t1i))
    nf = filt.shape[1]

    f_re, f_im = _hy_stage1(filt, t1_bf, G=G)
    blk = lambda c: pl.BlockSpec((kb, n2, c), lambda i: (i, 0, 0))
    hblk = lambda c: pl.BlockSpec((kb, k2, c), lambda i: (i, 0, 0))
    const = lambda s: pl.BlockSpec(s, lambda i: (0, 0))
    fshape = jax.ShapeDtypeStruct((n1, k2, nf), F32)
    fre, fim = pl.pallas_call(
        functools.partial(_hy_fspec2_kernel, kb=kb, k2=k2),
        grid=(n1 // kb,),
        in_specs=[blk(nf), blk(nf), const((n2, 2 * n2))],
        out_specs=[hblk(nf), hblk(nf)],
        out_shape=[fshape, fshape],
        compiler_params=_cp("arbitrary"),
        name="hy_fspec2",
    )(f_re.reshape(n1, n2, nf), f_im.reshape(n1, n2, nf), t2_bf)

    wide = pl.BlockSpec((n1, G * cols), lambda j: (0, j))
    half = pl.BlockSpec((h1, G * cols), lambda j: (0, j))
    ashape = jax.ShapeDtypeStruct((n1, n2, cols), BF16)
    for order, g in enumerate((g1, g2)):
        a_re, a_im = _hy_stage1(z, t1_bf, G=G)
        b_re, b_im = pl.pallas_call(
            functools.partial(_hy_stage2_kernel, kb=kb, k2=k2, n2=n2, nbc=nbc, order=order),
            grid=(n1 // kb,),
            in_specs=[blk(cols), blk(cols), hblk(nf), hblk(nf), const((n2, 2 * n2)), const((2 * n2, n2))],
            out_specs=[blk(cols), blk(cols)],
            out_shape=[ashape, ashape],
            compiler_params=_cp("arbitrary"),
            name="hy_stage2",
        )(a_re.reshape(n1, n2, cols), a_im.reshape(n1, n2, cols), fre, fim, t2_bf, t2i_bf)
        z = pl.pallas_call(
            functools.partial(_hy_stage3_kernel, G=G, cols=cols),
            grid=(n2 // G,),
            in_specs=[wide, wide, pl.BlockSpec((G, h1, 2 * n1), lambda j: (j, 0, 0)), half, half,
                      pl.BlockSpec((1, cols), lambda j: (0, 0))],
            out_specs=half,
            out_shape=jax.ShapeDtypeStruct((h1, n2 * cols), F32),
            compiler_params=_cp("arbitrary"),
            name="hy_stage3",
        )(b_re.reshape(n1, n2 * cols), b_im.reshape(n1, n2 * cols), t1i_bf,
          z.reshape(h1, n2 * cols), g.reshape(h1, n2 * cols), skip_t[order:order + 1]).reshape(L, cols)
    return z


def _hyena(hv_t, h1_t, h2_t, conv_w, conv_b, filt, skip):
    L, cols = hv_t.shape
    z, g1, g2 = _hy_conv3(hv_t, h1_t, h2_t, conv_w, conv_b)
    skip_t = jnp.tile(skip, (1, cols // D_HY))
    if L % (FFT_N1 * 8) == 0 and L >= 2048:
        return _hyena_two_level(z, g1, g2, filt, skip_t)
    return _hyena_direct(z, g1, g2, filt, skip_t)


def _route(logits):
    lane = lax.broadcasted_iota(jnp.int32, logits.shape, 1)
    is_g = (lane >= N_EXPERTS) & (lane < N_EXPERTS + N_GROUPS)
    big = jnp.int32(1 << 20)
    gl = jnp.where(is_g, logits, NEG)
    gmax = jnp.max(gl, axis=-1, keepdims=True)
    gidx = jnp.min(jnp.where(gl == gmax, lane - N_EXPERTS, big), axis=-1, keepdims=True)
    p_group = 1.0 / jnp.sum(jnp.where(is_g, jnp.exp(gl - gmax), 0.0), axis=-1, keepdims=True)
    sel = (lane < N_EXPERTS) & ((lane // EXPERTS_PER_GROUP) == gidx)
    el = jnp.where(sel, logits, NEG)
    ee = jnp.where(sel, jnp.exp(el - jnp.max(el, axis=-1, keepdims=True)), 0.0)
    pe = ee / jnp.sum(ee, axis=-1, keepdims=True)
    p1 = jnp.max(jnp.where(sel, pe, -1.0), axis=-1, keepdims=True)
    i1 = jnp.min(jnp.where(sel & (pe == p1), lane, big), axis=-1, keepdims=True)
    rest = sel & (lane != i1)
    p2 = jnp.max(jnp.where(rest, pe, -1.0), axis=-1, keepdims=True)
    i2 = jnp.min(jnp.where(rest & (pe == p2), lane, big), axis=-1, keepdims=True)
    den = p1 + p2
    cmb = jnp.where(lane == i1, p_group * (p1 / den), jnp.where(lane == i2, p_group * (p2 / den), 0.0))
    return jnp.where(lane == N_EXPERTS, gidx.astype(F32), cmb)


def _outproj_kernel(att_ref, hf_ref, hb_ref, gt_ref, hy_ref, x_ref, mod_ref, w_ref, g_ref, b_ref,
                    wr_ref, br_ref, x1_ref, u2_ref, cmb_ref):
    rg = (hf_ref[...] + hb_ref[...]) * jax.nn.gelu(gt_ref[...])
    mix = (jnp.dot(att_ref[0], w_ref[0:D_ATT, :], preferred_element_type=F32)
           + jnp.dot(rg.astype(BF16), w_ref[D_ATT:D_ATT + D_RNN, :], preferred_element_type=F32)
           + jnp.dot(hy_ref[...].astype(BF16), w_ref[D_ATT + D_RNN:, :], preferred_element_type=F32))
    m = mod_ref[0]
    x1 = _ln(ALPHA * x_ref[...] + m[2:3] * mix) * g_ref[...] + b_ref[...]
    u2 = _ln(x1) * (1.0 + m[4:5]) + m[3:4]
    logits = jnp.dot(u2, wr_ref[...], precision=HIGHEST, preferred_element_type=F32) + br_ref[...]
    x1_ref[...] = x1
    u2_ref[...] = u2.astype(BF16)
    cmb_ref[...] = _route(logits)


def _outproj(att, hf_t, hb_t, gt_t, hy_t, x2d, mod_l, w_bf, ln_g, ln_b, wr, br, *, nb, L, tm,
             cond_base, cond_per_batch):
    tpb = L // tm
    d = x2d.shape[1]
    cond = (lambda i: cond_base + i // tpb) if cond_per_batch else (lambda i: cond_base)
    t_spec = pl.BlockSpec((tm, D_RNN), lambda i: (i % tpb, i // tpb))
    tok = lambda c: pl.BlockSpec((tm, c), lambda i: (i, 0))
    const = lambda s: pl.BlockSpec(s, lambda i: (0, 0))
    n = nb * L
    return pl.pallas_call(
        _outproj_kernel,
        grid=(nb * tpb,),
        in_specs=[pl.BlockSpec((1, tm, D_ATT), lambda i: (i // tpb, i % tpb, 0)),
                  t_spec, t_spec, t_spec, t_spec, tok(d),
                  pl.BlockSpec((1, 6, d), lambda i: (cond(i), 0, 0)),
                  const((d, d)), const((1, d)), const((1, d)), const((d, LANES)), const((1, LANES))],
        out_specs=[tok(d), tok(d), tok(LANES)],
        out_shape=[jax.ShapeDtypeStruct((n, d), F32), jax.ShapeDtypeStruct((n, d), BF16),
                   jax.ShapeDtypeStruct((n, LANES), F32)],
        compiler_params=_cp("arbitrary"),
        name="outproj",
    )(att, hf_t, hb_t, gt_t, hy_t, x2d, mod_l, w_bf, ln_g[None], ln_b[None], wr, br)


def _moe_kernel(u_ref, cmb_ref, x1_ref, mod_ref, wgu_ref, wd_ref, g_ref, b_ref, o_ref, ys_ref, *, T, BLK):
    SL = T + N_GROUPS * BLK
    u = u_ref[...]
    cmb = cmb_ref[...]
    lane_f = lax.broadcasted_iota(jnp.int32, (T, LANES), 1).astype(F32)
    gid_col = jnp.sum(jnp.where(lane_f == float(N_EXPERTS), cmb, 0.0), axis=-1, keepdims=True)
    gid_row = cmb.T[N_EXPERTS:N_EXPERTS + 1, :]
    sub = lax.broadcasted_iota(jnp.int32, (8, T), 0).astype(F32)
    onehot = jnp.where(lane_f == gid_col, 1.0, 0.0)
    onehot_t = jnp.where(sub == gid_row, 1.0, 0.0)
    ri = lax.broadcasted_iota(jnp.int32, (T, T), 0)
    ci = lax.broadcasted_iota(jnp.int32, (T, T), 1)
    before = jnp.where(ci < ri, 1.0, 0.0).astype(BF16)
    before_t = jnp.where(ri < ci, 1.0, 0.0).astype(BF16)
    rank_col = jnp.sum(onehot * jnp.dot(before, onehot.astype(BF16), preferred_element_type=F32),
                       axis=-1, keepdims=True)
    rank_row = jnp.sum(onehot_t * jnp.dot(onehot_t.astype(BF16), before_t, preferred_element_type=F32),
                       axis=0, keepdims=True)
    counts = jnp.sum(onehot, axis=0, keepdims=True)
    lane_row = lax.broadcasted_iota(jnp.int32, (1, LANES), 1)
    sub_col = lax.broadcasted_iota(jnp.int32, (8, 1), 0)
    base = jnp.int32(0)
    bases = []
    base_row = jnp.zeros((1, LANES), jnp.int32)
    base_sub = jnp.zeros((8, 1), jnp.int32)
    for c in range(N_GROUPS):
        n_c = jnp.sum(jnp.where(lane_row == c, counts, 0.0)).astype(jnp.int32)
        bases.append(base)
        base_row = jnp.where(lane_row == c, base, base_row)
        base_sub = jnp.where(sub_col == c, base, base_sub)
        base = base + ((n_c + (BLK - 1)) // BLK) * BLK
    total = base
    dest_col = (jnp.sum(onehot * base_row.astype(F32), axis=-1, keepdims=True) + rank_col).astype(jnp.int32)
    dest_row = (jnp.sum(onehot_t * base_sub.astype(F32), axis=0, keepdims=True) + rank_row).astype(jnp.int32)
    cmb_hi = cmb.astype(BF16)
    cmb_lo = (cmb - cmb_hi.astype(F32)).astype(BF16)
    blk_rows = lax.broadcasted_iota(jnp.int32, (BLK, T), 0)
    blk_lane = lax.broadcasted_iota(jnp.int32, (BLK, LANES), 1)

    def block(b, carry):
        r0 = b * BLK
        rows = pl.ds(pl.multiple_of(r0, BLK), BLK)

        @pl.when(r0 < total)
        def _():
            grp = ((r0 >= bases[1]).astype(jnp.int32) + (r0 >= bases[2]).astype(jnp.int32)
                   + (r0 >= bases[3]).astype(jnp.int32))
            p = jnp.where(blk_rows + r0 == dest_row, 1.0, 0.0).astype(BF16)
            xb = jnp.dot(p, u, preferred_element_type=F32).astype(BF16)
            wb = (jnp.dot(p, cmb_hi, preferred_element_type=F32)
                  + jnp.dot(p, cmb_lo, preferred_element_type=F32))
            acc = jnp.zeros((BLK, u.shape[1]), F32)
            for j in range(EXPERTS_PER_GROUP):
                e = grp * EXPERTS_PER_GROUP + j
                gu = jnp.dot(xb, wgu_ref[e], preferred_element_type=F32)
                gate = gu[:, :D_EXPERT]
                w = jnp.sum(jnp.where(blk_lane == e, wb, 0.0), axis=-1, keepdims=True)
                h = gate * jax.nn.sigmoid(gate) * gu[:, D_EXPERT:] * w
                acc = acc + jnp.dot(h.astype(BF16), wd_ref[e], preferred_element_type=F32)
            ys_ref[rows, :] = acc.astype(BF16)

        @pl.when(r0 >= total)
        def _():
            ys_ref[rows, :] = jnp.zeros((BLK, u.shape[1]), BF16)

        return carry

    lax.fori_loop(0, SL // BLK, block, 0)
    p_t = jnp.where(lax.broadcasted_iota(jnp.int32, (T, SL), 1) == dest_col, 1.0, 0.0).astype(BF16)
    moe = jnp.dot(p_t, ys_ref[...], preferred_element_type=F32)
    m = mod_ref[0]
    o_ref[...] = _ln(ALPHA * x1_ref[...] + m[5:6] * moe) * g_ref[...] + b_ref[...]


def _moe(u2, cmb, x1, mod_l, wgu, wd, ln_g, ln_b, *, tm, tiles_per_cond, cond_base, cond_per_batch):
    n, d = x1.shape
    ne = wgu.shape[0]
    cond = (lambda i: cond_base + i // tiles_per_cond) if cond_per_batch else (lambda i: cond_base)
    tok = lambda c: pl.BlockSpec((tm, c), lambda i: (i, 0))
    const = lambda s: pl.BlockSpec(s, lambda i: (0,) * len(s))
    resident = lambda s: pl.BlockSpec(s, lambda i: (0,) * len(s), pipeline_mode=pl.Buffered(1))
    return pl.pallas_call(
        functools.partial(_moe_kernel, T=tm, BLK=MOE_BLK),
        grid=(n // tm,),
        in_specs=[tok(d), tok(LANES), tok(d),
                  pl.BlockSpec((1, 6, d), lambda i: (cond(i), 0, 0)),
                  resident((ne, d, 2 * D_EXPERT)), resident((ne, D_EXPERT, d)),
                  const((1, d)), const((1, d))],
        out_specs=tok(d),
        out_shape=jax.ShapeDtypeStruct((n, d), F32),
        scratch_shapes=[pltpu.VMEM((tm + N_GROUPS * MOE_BLK, d), BF16)],
        compiler_params=_cp("arbitrary"),
        name="moe",
    )(u2, cmb, x1, mod_l, wgu, wd, ln_g[None], ln_b[None])


def _stream_layer(x2d, mod_l, p, *, nb, L, tm, cond_base, cond_per_batch, ctx, cache=None, h0=None):
    kw = dict(nb=nb, L=L, tm=tm, cond_base=cond_base, cond_per_batch=cond_per_batch)
    q, k, v, xr_t, gt_t, hv_t, h1_t, h2_t = _inproj(x2d, mod_l, p["w_in"], kv_dtype=F32 if ctx else BF16, **kw)
    if ctx:
        att = _ctx_attention(q, k, v)
    else:
        att = _nbr_attention(q, k, v, cache[0], cache[1], p["nbr_bias"])
    a_f, b_f, a_b, b_b = _rg_gates(xr_t, p["rg_conv_w"], p["rg_conv_b"], *p["rg_w"])
    hf_t, hb_t = _rg_scan(a_f, b_f, a_b, b_b, h0[0], h0[1])
    hy_t = _hyena(hv_t, h1_t, h2_t, p["hy_conv_w"], p["hy_conv_b"], p["hy_filt"][L], p["hy_skip"])
    x1, u2, cmb = _outproj(att, hf_t, hb_t, gt_t, hy_t, x2d, mod_l, p["w_out"], p["ln1_g"], p["ln1_b"],
                           p["router_w"], p["router_b"], **kw)
    tm_moe = min(MOE_TILE, nb * L)
    x2 = _moe(u2, cmb, x1, mod_l, p["moe_wgu"], p["moe_wd"], p["ln2_g"], p["ln2_b"], tm=tm_moe,
              tiles_per_cond=max(L // tm_moe, 1), cond_base=cond_base, cond_per_batch=cond_per_batch)
    return x2, k, v, hf_t, hb_t


def kernel(x_prompt, x_sample, cache_k, cache_v, state_rglru, c, c_ctx, w_mod, b_mod, w_in, attn_rpb, rg_conv_w, rg_conv_b, rg_wa, rg_ba, rg_wx, rg_bx, rg_lambda, hy_conv_w, hy_conv_b, hy_w1, hy_b1, hy_w2, hy_b2, hy_w3, hy_freq, hy_decay, hy_skip, w_out, ln1_g, ln1_b, router_wg, router_bg, router_we, router_be, moe_w_gate, moe_w_up, moe_w_down, ln2_g, ln2_b):
    nbc, Lc, d = x_prompt.shape
    nbl, Ll, _ = x_sample.shape
    depth = w_mod.shape[0]
    past = cache_k.shape[2]

    cond8 = jnp.zeros((8, d), F32).at[0].set(c_ctx).at[1:1 + nbl].set(c)
    mod = _modulation(cond8, w_mod, b_mod).reshape(depth, 8, 6, d)

    xp = x_prompt.reshape(nbc * Lc, d)
    xs = x_sample.reshape(nbl * Ll, d)
    new_k, new_v, new_h = [], [], []
    for l in range(depth):
        pad = jnp.zeros((d, LANES - N_EXPERTS - N_GROUPS), F32)
        p = {
            "w_in": w_in[l].astype(BF16),
            "nbr_bias": _nbr_bias_tables(attn_rpb[l], Ll // GRID_W),
            "rg_conv_w": rg_conv_w[l], "rg_conv_b": rg_conv_b[l],
            "rg_w": _rg_weights(rg_wa[l], rg_ba[l], rg_wx[l], rg_bx[l], rg_lambda[l]),
            "hy_conv_w": hy_conv_w[l], "hy_conv_b": hy_conv_b[l], "hy_skip": hy_skip[l],
            "hy_filt": {L: _hy_filters(L, hy_w1[l], hy_b1[l], hy_w2[l], hy_b2[l], hy_w3[l], hy_freq[l], hy_decay[l])
                        for L in sorted({Lc, Ll})},
            "w_out": w_out[l].astype(BF16), "ln1_g": ln1_g[l], "ln1_b": ln1_b[l],
            "router_w": jnp.concatenate([router_we[l], router_wg[l], pad], axis=1),
            "router_b": jnp.concatenate([router_be[l], router_bg[l], pad[0]])[None],
            "moe_wgu": jnp.concatenate([moe_w_gate[l], moe_w_up[l]], axis=-1).astype(BF16),
            "moe_wd": moe_w_down[l].astype(BF16),
            "ln2_g": ln2_g[l], "ln2_b": ln2_b[l],
        }
        zeros_h = jnp.zeros((1, nbc * D_RNN), F32)
        xp, k, v, hf_t, hb_t = _stream_layer(
            xp, mod[l], p, nb=nbc, L=Lc, tm=Lc, cond_base=0, cond_per_batch=False, ctx=True,
            h0=(zeros_h, zeros_h))
        new_k.append(k.reshape(nbc, Lc, N_HEADS, HEAD_DIM))
        new_v.append(v.reshape(nbc, Lc, N_HEADS, HEAD_DIM))
        new_h.append(jnp.stack([hf_t[Lc - 1].reshape(nbc, D_RNN), hb_t[0].reshape(nbc, D_RNN)], axis=1))

        cache = (cache_k[:, l].reshape(nbl, past, D_ATT), cache_v[:, l].reshape(nbl, past, D_ATT))
        h0 = (state_rglru[:, l, 0].reshape(1, nbl * D_RNN), state_rglru[:, l, 1].reshape(1, nbl * D_RNN))
        xs, _, _, _, _ = _stream_layer(
            xs, mod[l], p, nb=nbl, L=Ll, tm=min(Ll, 512), cond_base=1, cond_per_batch=True, ctx=False,
            cache=cache, h0=h0)

    return (xp.reshape(nbc, Lc, d), xs.reshape(nbl, Ll, d),
            jnp.stack(new_k, axis=1), jnp.stack(new_v, axis=1), jnp.stack(new_h, axis=1))
```

```python
import functools
import math

import numpy as np
import jax
import jax.numpy as jnp
from jax import lax
from jax.experimental import pallas as pl
from jax.experimental.pallas import tpu as pltpu

F32 = jnp.float32
BF16 = jnp.bfloat16
HIGHEST = lax.Precision.HIGHEST

DEPTH = 2
D_MODEL = 1024
N_HEADS = 8
HEAD_DIM = 64
D_ATT = N_HEADS * HEAD_DIM
D_RNN = 256
D_HY = 256
GRID_W = 64
WIN_R = 8
WIN_C = 16
RG_CONV = 4
RG_C = 8.0
HY_CONV = 3
POS_EMB_DIM = 33
N_GROUPS = 4
EXPERTS_PER_GROUP = 4
N_EXPERTS = 16
D_EXPERT = 256
ALPHA = (2.0 * DEPTH) ** 0.25
LN_EPS = 1e-5
NEG = -1e30

Q_ROWS = 4
K_ROWS = Q_ROWS + WIN_R
LANES = 128
HALO = 8
MOE_TILE = 512
MOE_BLK = 128
FFT_N1 = 64
VMEM_LIMIT = 52 * 1024 * 1024


def _cp(*sem):
    return pltpu.CompilerParams(dimension_semantics=sem, vmem_limit_bytes=VMEM_LIMIT)


def _ln(x):
    mu = jnp.mean(x, axis=-1, keepdims=True)
    xc = x - mu
    var = jnp.mean(xc * xc, axis=-1, keepdims=True)
    return xc * lax.rsqrt(var + LN_EPS)


def _mod_kernel(c_ref, w_ref, b_ref, o_ref):
    c = c_ref[...]
    s = c * jax.nn.sigmoid(c)
    o_ref[0] = jnp.dot(s, w_ref[0], precision=HIGHEST, preferred_element_type=F32) + b_ref[0]


def _modulation(cond8, w_mod, b_mod):
    depth, d, n = w_mod.shape
    tn = n // 4
    return pl.pallas_call(
        _mod_kernel,
        grid=(depth, n // tn),
        in_specs=[pl.BlockSpec((8, d), lambda l, j: (0, 0)),
                  pl.BlockSpec((1, d, tn), lambda l, j: (l, 0, j)),
                  pl.BlockSpec((1, 1, tn), lambda l, j: (l, 0, j))],
        out_specs=pl.BlockSpec((1, 8, tn), lambda l, j: (l, 0, j)),
        out_shape=jax.ShapeDtypeStruct((depth, 8, n), F32),
        compiler_params=_cp("arbitrary", "arbitrary"),
        name="modulation",
    )(cond8, w_mod, b_mod.reshape(depth, 1, n))


def _inproj_kernel(x_ref, mod_ref, w_ref, q_ref, k_ref, v_ref, xr_ref, gt_ref, hv_ref, h1_ref, h2_ref):
    m = mod_ref[0]
    u = (_ln(x_ref[...]) * (1.0 + m[1:2]) + m[0:1]).astype(BF16)

    def proj(lo, hi):
        return jnp.dot(u, w_ref[:, lo:hi], preferred_element_type=F32)

    o = 0
    q_ref[0] = (proj(o, o + D_ATT) * (HEAD_DIM ** -0.5)).astype(q_ref.dtype)
    o += D_ATT
    k_ref[0] = proj(o, o + D_ATT).astype(k_ref.dtype)
    o += D_ATT
    v_ref[0] = proj(o, o + D_ATT).astype(v_ref.dtype)
    o += D_ATT
    for ref in (xr_ref, gt_ref, hv_ref, h1_ref, h2_ref):
        ref[...] = proj(o, o + D_RNN)
        o += D_RNN


def _inproj(x2d, mod_l, w_bf, *, nb, L, tm, cond_base, cond_per_batch, kv_dtype):
    tpb = L // tm
    d = x2d.shape[1]
    d_in = w_bf.shape[1]
    cond = (lambda i: cond_base + i // tpb) if cond_per_batch else (lambda i: cond_base)
    att_spec = pl.BlockSpec((1, tm, D_ATT), lambda i: (i // tpb, i % tpb, 0))
    t_spec = pl.BlockSpec((tm, D_RNN), lambda i: (i % tpb, i // tpb))
    att_shape = lambda dt: jax.ShapeDtypeStruct((nb, L, D_ATT), dt)
    t_shape = jax.ShapeDtypeStruct((L, nb * D_RNN), F32)
    return pl.pallas_call(
        _inproj_kernel,
        grid=(nb * tpb,),
        in_specs=[pl.BlockSpec((tm, d), lambda i: (i, 0)),
                  pl.BlockSpec((1, 6, d), lambda i: (cond(i), 0, 0)),
                  pl.BlockSpec((d, d_in), lambda i: (0, 0))],
        out_specs=[att_spec, att_spec, att_spec, t_spec, t_spec, t_spec, t_spec, t_spec],
        out_shape=[att_shape(BF16), att_shape(kv_dtype), att_shape(kv_dtype),
                   t_shape, t_shape, t_shape, t_shape, t_shape],
        compiler_params=_cp("arbitrary"),
        name="inproj",
    )(x2d, mod_l, w_bf)


def _nt(a, b):
    return lax.dot_general(a, b, (((1,), (1,)), ((), ())), preferred_element_type=F32)


def _ctx_attn_kernel(q_ref, k_ref, v_ref, o_ref):
    q = q_ref[0]
    k = k_ref[0].astype(BF16)
    v = v_ref[0].astype(BF16)
    outs = []
    for h in range(N_HEADS):
        sl = slice(h * HEAD_DIM, (h + 1) * HEAD_DIM)
        s = _nt(q[:, sl], k[:, sl])
        e = jnp.exp(s - jnp.max(s, axis=-1, keepdims=True))
        den = jnp.sum(e, axis=-1, keepdims=True)
        outs.append(jnp.dot(e.astype(BF16), v[:, sl], preferred_element_type=F32) / den)
    o_ref[0] = jnp.concatenate(outs, axis=-1).astype(o_ref.dtype)


def _ctx_attention(q, k, v):
    nb, L, _ = q.shape
    spec = pl.BlockSpec((1, L, D_ATT), lambda b: (b, 0, 0))
    return pl.pallas_call(
        _ctx_attn_kernel,
        grid=(nb,),
        in_specs=[spec, spec, spec],
        out_specs=spec,
        out_shape=jax.ShapeDtypeStruct((nb, L, D_ATT), BF16),
        compiler_params=_cp("arbitrary"),
        name="ctx_attn",
    )(q, k, v)


def _nbr_bias_tables(rpb, rows):
    ng = rows // Q_ROWS
    cols = np.arange(GRID_W)
    c_start = np.clip(cols - WIN_C // 2, 0, GRID_W - WIN_C)
    col_mask = (cols[None, :] >= c_start[:, None]) & (cols[None, :] < c_start[:, None] + WIN_C)
    dc_idx = np.clip(cols[None, :] - cols[:, None], -(WIN_C - 1), WIN_C - 1) + (WIN_C - 1)
    sel = (dc_idx[None] == np.arange(2 * WIN_C - 1)[:, None, None]).astype(np.float32)
    colb = jnp.einsum('hab,bqk->haqk', rpb.astype(F32), jnp.asarray(sel), precision=HIGHEST)
    colb = jnp.where(col_mask[None, None], colb, NEG)
    masked = jnp.full((N_HEADS, GRID_W, GRID_W), NEG, F32)
    tables = []
    for g in (0, 1, ng - 1):
        r = Q_ROWS * g + np.arange(Q_ROWS)
        r0 = int(np.clip(Q_ROWS * g - WIN_R // 2, 0, rows - K_ROWS))
        start = np.clip(r - WIN_R // 2, 0, rows - WIN_R)
        krow = r0 + np.arange(K_ROWS)
        valid = (krow[None, :] >= start[:, None]) & (krow[None, :] < start[:, None] + WIN_R)
        dr = krow[None, :] - r[:, None] + (WIN_R - 1)
        tables.append(jnp.concatenate(
            [jnp.concatenate([colb[:, int(dr[j, i])] if valid[j, i] else masked for i in range(K_ROWS)], axis=-1)
             for j in range(Q_ROWS)], axis=-2))
    return jnp.stack(tables)


def _nbr_attn_kernel(q_ref, k_ref, v_ref, kc_ref, vc_ref, bias_ref, o_ref, *, rows):
    g = pl.program_id(1)
    r0 = jnp.clip(Q_ROWS * g - WIN_R // 2, 0, rows - K_ROWS)
    start = pl.multiple_of(r0 * GRID_W, Q_ROWS * GRID_W)
    nk = K_ROWS * GRID_W
    kw = k_ref[0, pl.ds(start, nk), :]
    vw = v_ref[0, pl.ds(start, nk), :]
    kc = kc_ref[0].astype(BF16)
    vc = vc_ref[0].astype(BF16)
    q = q_ref[0]
    outs = []
    for h in range(N_HEADS):
        sl = slice(h * HEAD_DIM, (h + 1) * HEAD_DIM)
        qh = q[:, sl]
        s1 = _nt(qh, kw[:, sl]) + bias_ref[0, h]
        s2 = _nt(qh, kc[:, sl])
        m = jnp.maximum(jnp.max(s1, axis=-1, keepdims=True), jnp.max(s2, axis=-1, keepdims=True))
        e1 = jnp.exp(s1 - m)
        e2 = jnp.exp(s2 - m)
        den = jnp.sum(e1, axis=-1, keepdims=True) + jnp.sum(e2, axis=-1, keepdims=True)
        o = (jnp.dot(e1.astype(BF16), vw[:, sl], preferred_element_type=F32)
             + jnp.dot(e2.astype(BF16), vc[:, sl], preferred_element_type=F32))
        outs.append(o / den)
    o_ref[0] = jnp.concatenate(outs, axis=-1).astype(o_ref.dtype)


def _nbr_attention(q, k, v, kc, vc, bias):
    nb, L, _ = q.shape
    rows = L // GRID_W
    ng = rows // Q_ROWS
    past = kc.shape[1]
    tq = Q_ROWS * GRID_W
    full = pl.BlockSpec((1, L, D_ATT), lambda b, g: (b, 0, 0))
    ctx = pl.BlockSpec((1, past, D_ATT), lambda b, g: (b, 0, 0))
    qspec = pl.BlockSpec((1, tq, D_ATT), lambda b, g: (b, g, 0))
    bspec = pl.BlockSpec((1, N_HEADS, tq, K_ROWS * GRID_W),
                         lambda b, g: (jnp.minimum(g, 1) + g // (ng - 1), 0, 0, 0))
    return pl.pallas_call(
        functools.partial(_nbr_attn_kernel, rows=rows),
        grid=(nb, ng),
        in_specs=[qspec, full, full, ctx, ctx, bspec],
        out_specs=qspec,
        out_shape=jax.ShapeDtypeStruct((nb, L, D_ATT), BF16),
        compiler_params=_cp("arbitrary", "arbitrary"),
        name="nbr_attn",
    )(q, k, v, kc, vc, bias)


def _fill_padded(xp_ref, src_ref, L):
    z = jnp.zeros((HALO, xp_ref.shape[1]), F32)
    xp_ref[0:HALO, :] = z
    xp_ref[HALO:HALO + L, :] = src_ref[...]
    xp_ref[HALO + L:2 * HALO + L, :] = z


def _rg_gates_kernel(x_ref, cw_ref, cb_ref, w_ref, b_ref, lam_ref,
                     af_ref, bf_ref, ab_ref, bb_ref, xp_ref, *, L, tc):
    _fill_padded(xp_ref, x_ref, L)
    sp = jax.nn.softplus(-lam_ref[0])
    for t0 in range(0, L, tc):
        xc = cb_ref[...]
        for k in range(RG_CONV):
            xc = xc + cw_ref[k:k + 1, :] * xp_ref[pl.ds(HALO + t0 + k - RG_CONV // 2, tc), :]
        z = jnp.dot(xc.astype(BF16), w_ref[0], preferred_element_type=F32) + b_ref[0]
        for e, (a_ref, o_ref) in enumerate(((af_ref, bf_ref), (ab_ref, bb_ref))):
            r = jax.nn.sigmoid(z[:, e * LANES:(e + 1) * LANES])
            i = jax.nn.sigmoid(z[:, (2 + e) * LANES:(3 + e) * LANES])
            log_a = (-RG_C) * r * sp[:, e * LANES:(e + 1) * LANES]
            a = jnp.exp(log_a)
            one_minus_a2 = -jnp.tanh(log_a) * (a * a + 1.0)
            a_ref[pl.ds(t0, tc), :] = a
            o_ref[pl.ds(t0, tc), :] = jnp.sqrt(one_minus_a2) * i * xc


def _blockdiag2(w2):
    z = jnp.zeros_like(w2[0])
    return jnp.concatenate([jnp.concatenate([w2[0], z], 1), jnp.concatenate([z, w2[1]], 1)], 0)


def _rg_weights(wa, ba, wx, bx, lam):
    ws, bs, ls = [], [], []
    for hf in range(D_RNN // LANES):
        blk = slice(2 * hf, 2 * hf + 2)
        ch = slice(hf * LANES, (hf + 1) * LANES)
        ws.append(jnp.concatenate([_blockdiag2(wa[0, blk]), _blockdiag2(wa[1, blk]),
                                   _blockdiag2(wx[0, blk]), _blockdiag2(wx[1, blk])], axis=1))
        bs.append(jnp.concatenate([ba[0, ch], ba[1, ch], bx[0, ch], bx[1, ch]])[None])
        ls.append(jnp.concatenate([lam[0, ch], lam[1, ch]])[None])
    return jnp.stack(ws).astype(BF16), jnp.stack(bs), jnp.stack(ls)


def _rg_gates(xr_t, conv_w, conv_b, w_bd, b_cat, lam_cat):
    L, cols = xr_t.shape
    nh = D_RNN // LANES
    tc = min(L, 512)
    col = pl.BlockSpec((L, LANES), lambda c: (0, c))
    shape = jax.ShapeDtypeStruct((L, cols), F32)
    return pl.pallas_call(
        functools.partial(_rg_gates_kernel, L=L, tc=tc),
        grid=(cols // LANES,),
        in_specs=[col,
                  pl.BlockSpec((RG_CONV, LANES), lambda c: (0, c % nh)),
                  pl.BlockSpec((1, LANES), lambda c: (0, c % nh)),
                  pl.BlockSpec((1, LANES, 4 * LANES), lambda c: (c % nh, 0, 0)),
                  pl.BlockSpec((1, 1, 4 * LANES), lambda c: (c % nh, 0, 0)),
                  pl.BlockSpec((1, 1, 2 * LANES), lambda c: (c % nh, 0, 0))],
        out_specs=[col, col, col, col],
        out_shape=[shape, shape, shape, shape],
        scratch_shapes=[pltpu.VMEM((L + 2 * HALO, LANES), F32)],
        compiler_params=_cp("arbitrary"),
        name="rg_gates",
    )(xr_t, conv_w, conv_b.reshape(1, D_RNN), w_bd, b_cat, lam_cat)


def _rg_scan_kernel(af_ref, bf_ref, ab_ref, bb_ref, h0f_ref, h0b_ref, hf_ref, hb_ref, cf_ref, cb_ref, *, tc):
    @pl.when(pl.program_id(1) == 0)
    def _():
        cf_ref[...] = h0f_ref[...]
        cb_ref[...] = h0b_ref[...]

    def body(t, carry):
        hf, hb = carry
        hf = af_ref[pl.ds(t, 1), :] * hf + bf_ref[pl.ds(t, 1), :]
        hf_ref[pl.ds(t, 1), :] = hf
        tb = tc - 1 - t
        hb = ab_ref[pl.ds(tb, 1), :] * hb + bb_ref[pl.ds(tb, 1), :]
        hb_ref[pl.ds(tb, 1), :] = hb
        return hf, hb

    hf, hb = lax.fori_loop(0, tc, body, (cf_ref[...], cb_ref[...]), unroll=8)
    cf_ref[...] = hf
    cb_ref[...] = hb


def _rg_scan(a_f, b_f, a_b, b_b, h0f, h0b):
    L, cols = a_f.shape
    cw = min(cols, 1024)
    tc = min(L, 512)
    nt = L // tc
    fwd = pl.BlockSpec((tc, cw), lambda c, i: (i, c))
    bwd = pl.BlockSpec((tc, cw), lambda c, i: (nt - 1 - i, c))
    row = pl.BlockSpec((1, cw), lambda c, i: (0, c))
    shape = jax.ShapeDtypeStruct((L, cols), F32)
    return pl.pallas_call(
        functools.partial(_rg_scan_kernel, tc=tc),
        grid=(cols // cw, nt),
        in_specs=[fwd, fwd, bwd, bwd, row, row],
        out_specs=[fwd, bwd],
        out_shape=[shape, shape],
        scratch_shapes=[pltpu.VMEM((1, cw), F32), pltpu.VMEM((1, cw), F32)],
        compiler_params=_cp("arbitrary", "arbitrary"),
        name="rg_scan",
    )(a_f, b_f, a_b, b_b, h0f, h0b)


def _hy_filter_kernel(feat_ref, w1_ref, b1_ref, w2_ref, b2_ref, w3_ref, fr_ref, dec_ref, o_ref, *, tm):
    feats = feat_ref[...]
    dot = functools.partial(jnp.dot, precision=HIGHEST, preferred_element_type=F32)
    h = jnp.sin(fr_ref[0:1, :] * (dot(feats, w1_ref[...]) + b1_ref[...]))
    h = jnp.sin(fr_ref[1:2, :] * (dot(h, w2_ref[...]) + b2_ref[...]))
    h = dot(h, w3_ref[...])
    filt = h * jnp.exp(-feats[:, 0:1] * jnp.abs(dec_ref[...]))
    row = pl.program_id(0) * tm + lax.broadcasted_iota(jnp.int32, filt.shape, 0)
    lane = lax.broadcasted_iota(jnp.int32, filt.shape, 1)
    o_ref[...] = jnp.where((row == 0) & (lane >= 2 * D_HY), 0.0, filt)


def _hy_features(L):
    pos = np.arange(L, dtype=np.float64)
    t = np.linspace(0.0, 1.0, L)
    bands = (POS_EMB_DIM - 1) // 2
    f = np.linspace(1e-4, bands - 1, bands)
    ang = (2.0 * math.pi / L) * pos[:, None] * f[None, :]
    feats = np.concatenate([t[:, None], np.cos(ang), -np.sin(ang)], axis=-1)
    out = np.zeros((L, LANES), np.float32)
    out[:, :POS_EMB_DIM] = feats
    return out


def _two_level(L):
    return L % (FFT_N1 * 8) == 0 and L >= 2048


def _digit_swap(x, L):
    n2 = 2 * L // FFT_N1
    return x.reshape(L // n2, n2, -1).transpose(1, 0, 2).reshape(L, -1)


def _hy_filters(L, w1, b1, w2, b2, w3, freq, decay):
    tm = min(L, 512)
    hid = w2.shape[0]
    n = w3.shape[1]
    w1p = jnp.zeros((LANES, hid), F32).at[:POS_EMB_DIM].set(w1)
    feats = _hy_features(L)
    if _two_level(L):
        feats = _digit_swap(feats, L)
    c = lambda s: pl.BlockSpec(s, lambda i: (0, 0))
    return pl.pallas_call(
        functools.partial(_hy_filter_kernel, tm=tm),
        grid=(L // tm,),
        in_specs=[pl.BlockSpec((tm, LANES), lambda i: (i, 0)),
                  c((LANES, hid)), c((1, hid)), c((hid, hid)), c((1, hid)), c((hid, n)), c((2, hid)), c((1, n))],
        out_specs=pl.BlockSpec((tm, n), lambda i: (i, 0)),
        out_shape=jax.ShapeDtypeStruct((L, n), F32),
        compiler_params=_cp("arbitrary"),
        name="hy_filter",
    )(jnp.asarray(feats), w1p, b1[None], w2, b2[None], w3, freq, decay.reshape(1, n))


def _conv3_kernel(v_ref, a_ref, b_ref, w_ref, cb_ref, ov_ref, oa_ref, ob_ref, xp_ref, *, L, tc, swap, n2):
    for p, (src, dst) in enumerate(((v_ref, ov_ref), (a_ref, oa_ref), (b_ref, ob_ref))):
        _fill_padded(xp_ref, src, L)
        if swap[p]:
            def body(i2, carry, p=p, dst=dst):
                y = cb_ref[p]
                for k in range(HY_CONV):
                    y = y + w_ref[p, k:k + 1, :] * xp_ref[pl.ds(HALO + i2 + k - HY_CONV // 2, L // n2, stride=n2), :]
                dst[i2] = y
                return carry

            lax.fori_loop(0, n2, body, 0)
        else:
            for t0 in range(0, L, tc):
                y = cb_ref[p]
                for k in range(HY_CONV):
                    y = y + w_ref[p, k:k + 1, :] * xp_ref[pl.ds(HALO + t0 + k - HY_CONV // 2, tc), :]
                dst[pl.ds(t0, tc), :] = y


def _hy_conv3(v_t, a_t, b_t, conv_w, conv_b, swap=(False, False, False)):
    L, cols = v_t.shape
    nh = D_HY // LANES
    tc = min(L, 512)
    n2 = 2 * L // FFT_N1
    col = pl.BlockSpec((L, LANES), lambda c: (0, c))
    col3 = pl.BlockSpec((n2, L // n2, LANES), lambda c: (0, 0, c))
    shape = jax.ShapeDtypeStruct((L, cols), F32)
    shape3 = jax.ShapeDtypeStruct((n2, L // n2, cols), F32)
    w = conv_w.reshape(HY_CONV, 3, D_HY).transpose(1, 0, 2)
    cb = conv_b.reshape(3, 1, D_HY)
    return pl.pallas_call(
        functools.partial(_conv3_kernel, L=L, tc=tc, swap=swap, n2=n2),
        grid=(cols // LANES,),
        in_specs=[col, col, col,
                  pl.BlockSpec((3, HY_CONV, LANES), lambda c: (0, 0, c % nh)),
                  pl.BlockSpec((3, 1, LANES), lambda c: (0, 0, c % nh))],
        out_specs=[col3 if s else col for s in swap],
        out_shape=[shape3 if s else shape for s in swap],
        scratch_shapes=[pltpu.VMEM((L + 2 * HALO, LANES), F32)],
        compiler_params=_cp("arbitrary"),
        name="hy_conv3",
    )(v_t, a_t, b_t, w, cb)


def _phase(p, n2):
    ang = 2.0 * np.pi * (np.asarray(p, np.int64) % n2) / n2
    return np.cos(ang), np.sin(ang)


def _dft_direct_tables(L):
    k = np.arange(L)[:, None]
    t = np.arange(L)[None, :]
    c, s = _phase((2 * k + 1) * t, 4 * L)
    fwd = np.concatenate([c, -s], axis=0)
    inv = (1.0 / L) * np.concatenate([c.T, -s.T], axis=1)
    return fwd.astype(np.float32), inv.astype(np.float32)


def _dft_two_level_tables(L, n1):
    N = 2 * L
    n2 = N // n1
    h1 = n1 // 2
    i2 = np.arange(n2)[:, None, None]
    k1 = np.arange(n1)[None, :, None]
    i1 = np.arange(h1)[None, None, :]
    c, s = _phase(2 * n2 * i1 * k1 + 2 * i2 * k1 + n2 * i1 + i2, 2 * N)
    t1 = np.concatenate([c, -s], axis=1)
    ct, st = c.transpose(0, 2, 1), s.transpose(0, 2, 1)
    t1i = (2.0 / N) * np.concatenate([ct, -st], axis=2)
    kk = np.arange(n2 // 2)[:, None]
    nn = np.arange(n2)[None, :]
    c2, s2 = _phase(nn * kk, n2)
    t2 = np.block([[c2, s2], [-s2, c2]])
    t2i = np.block([[c2.T, -s2.T], [s2.T, c2.T]])
    return tuple(x.astype(np.float32) for x in (t1, t2, t2i, t1i))


def _filter_spectrum(fre, fim, order, skip):
    fw = slice(order * D_HY, (order + 1) * D_HY)
    bw = slice((2 + order) * D_HY, (3 + order) * D_HY)
    return fre[:, fw] + fre[:, bw] + skip, fim[:, fw] - fim[:, bw]


def _spectral_product(xr, xi, fr, fi, nbc):
    yr, yi = [], []
    for b in range(nbc):
        sl = slice(b * D_HY, (b + 1) * D_HY)
        yr.append(xr[:, sl] * fr - xi[:, sl] * fi)
        yi.append(xr[:, sl] * fi + xi[:, sl] * fr)
    return jnp.concatenate([jnp.concatenate(yr, axis=1), jnp.concatenate(yi, axis=1)], axis=0)


def _matmul_kernel(a_ref, b_ref, o_ref):
    o_ref[...] = jnp.dot(a_ref[...], b_ref[...], precision=HIGHEST, preferred_element_type=F32)


def _hy_direct_kernel(z_ref, g_ref, f_ref, skip_ref, t_ref, ti_ref, o_ref, *, L, nbc, order):
    x = jnp.dot(t_ref[...], z_ref[...].astype(BF16), preferred_element_type=F32)
    fr, fi = _filter_spectrum(f_ref[0:L, :], f_ref[L:2 * L, :], order, skip_ref[0])
    y = _spectral_product(x[:L], x[L:], fr, fi, nbc).astype(BF16)
    conv = jnp.dot(ti_ref[...], y, preferred_element_type=F32)
    o_ref[...] = (g_ref[...] * conv).astype(o_ref.dtype)


def _hyena_direct(z, g1, g2, filt, skip):
    L, cols = z.shape
    fwd, inv = _dft_direct_tables(L)
    n = filt.shape[1]
    spec = pl.pallas_call(
        _matmul_kernel,
        out_shape=jax.ShapeDtypeStruct((2 * L, n), F32),
        name="hy_fspec",
    )(jnp.asarray(fwd), filt)
    tn = min(cols, 4 * D_HY)
    nbc = tn // D_HY
    col = pl.BlockSpec((L, tn), lambda c: (0, c))
    const = lambda s: pl.BlockSpec(s, lambda c: (0, 0))
    t_bf, ti_bf = jnp.asarray(fwd).astype(BF16), jnp.asarray(inv).astype(BF16)
    for order, g in enumerate((g1, g2)):
        z = pl.pallas_call(
            functools.partial(_hy_direct_kernel, L=L, nbc=nbc, order=order),
            grid=(cols // tn,),
            in_specs=[col, col, const((2 * L, n)), pl.BlockSpec((1, 1, D_HY), lambda c: (order, 0, 0)),
                      const((2 * L, L)), const((L, 2 * L))],
            out_specs=col,
            out_shape=jax.ShapeDtypeStruct((L, cols), F32 if order == 0 else BF16),
            compiler_params=_cp("arbitrary"),
            name="hy_direct",
        )(z, g, spec, skip[:, None, :], t_bf, ti_bf)
    return z


def _kron_tables(t1, t1i, G):
    n2, two_n1, h1 = t1.shape
    ng = n2 // G
    eye = jnp.asarray(np.eye(G, dtype=bool))
    a = jnp.asarray(t1).astype(BF16).reshape(ng, G, two_n1, h1).transpose(0, 2, 1, 3)
    fwd = jnp.where(eye[None, None, :, :, None], a[:, :, None, :, :], 0).reshape(ng, two_n1 * G, G * h1)
    b = jnp.asarray(t1i).astype(BF16).reshape(ng, G, h1, two_n1)
    inv = jnp.where(eye[None, :, None, None, :], b[:, :, :, :, None], 0)
    inv_swapped = inv.reshape(ng, G * h1, two_n1 * G)
    inv_natural = inv.transpose(0, 2, 1, 3, 4).reshape(ng, h1 * G, two_n1 * G)
    return fwd, inv_swapped, inv_natural


def _hy_stage1_kernel(x_ref, t_ref, re_ref, im_ref, *, G, n1, ct):
    h1, cols = x_ref.shape[1], x_ref.shape[2]
    x = x_ref[...].reshape(G * h1, cols).astype(BF16)
    for c0 in range(0, cols, ct):
        a = jnp.dot(t_ref[0], x[:, c0:c0 + ct], preferred_element_type=F32)
        re_ref[:, :, c0:c0 + ct] = a[:n1 * G].reshape(n1, G, ct).astype(BF16)
        im_ref[:, :, c0:c0 + ct] = a[n1 * G:].reshape(n1, G, ct).astype(BF16)


def _hy_stage1(x3, t_fwd, *, G):
    n2, h1, cols = x3.shape
    n1 = t_fwd.shape[1] // (2 * G)
    ospec = pl.BlockSpec((n1, G, cols), lambda j: (0, j, 0))
    oshape = jax.ShapeDtypeStruct((n1, n2, cols), BF16)
    return pl.pallas_call(
        functools.partial(_hy_stage1_kernel, G=G, n1=n1, ct=D_HY),
        grid=(n2 // G,),
        in_specs=[pl.BlockSpec((G, h1, cols), lambda j: (j, 0, 0)),
                  pl.BlockSpec((1,) + t_fwd.shape[1:], lambda j: (j, 0, 0))],
        out_specs=[ospec, ospec],
        out_shape=[oshape, oshape],
        compiler_params=_cp("arbitrary"),
        name="hy_stage1",
    )(x3, t_fwd)


def _hy_fspec2_kernel(re_ref, im_ref, t2_ref, fre_ref, fim_ref, *, kb, k2):
    for j in range(kb):
        a2 = jnp.concatenate([re_ref[j], im_ref[j]], axis=0)
        x = jnp.dot(t2_ref[...], a2, preferred_element_type=F32)
        fre_ref[j] = x[:k2]
        fim_ref[j] = x[k2:]


def _hy_stage2_kernel(re_ref, im_ref, fre_ref, fim_ref, skip_ref, t2_ref, t2i_ref, ore_ref, oim_ref,
                      *, kb, k2, n2, nbc, order):
    for j in range(kb):
        a2 = jnp.concatenate([re_ref[j], im_ref[j]], axis=0)
        x = jnp.dot(t2_ref[...], a2, preferred_element_type=F32)
        fr, fi = _filter_spectrum(fre_ref[j], fim_ref[j], order, skip_ref[0])
        y = _spectral_product(x[:k2], x[k2:], fr, fi, nbc).astype(BF16)
        a = jnp.dot(t2i_ref[...], y, preferred_element_type=F32)
        ore_ref[j] = a[:n2].astype(BF16)
        oim_ref[j] = a[n2:].astype(BF16)


def _hy_stage3_kernel(re_ref, im_ref, t_ref, g_ref, o_ref, *, G, n1, ct):
    cols = re_ref.shape[2]
    for c0 in range(0, cols, ct):
        sl = slice(c0, c0 + ct)
        a2 = jnp.concatenate([re_ref[:, :, sl].reshape(n1 * G, ct), im_ref[:, :, sl].reshape(n1 * G, ct)], axis=0)
        conv = jnp.dot(t_ref[0], a2, preferred_element_type=F32)
        g = g_ref[:, :, sl]
        o_ref[:, :, sl] = (g * conv.reshape(g.shape)).astype(o_ref.dtype)


def _hyena_two_level(z3, g1_3, g2, filt, skip, *, n1=FFT_N1, G=16, kb=4):
    n2, h1, cols = z3.shape
    L = n2 * h1
    t1, t2, t2i, t1i = _dft_two_level_tables(L, n1)
    k2 = n2 // 2
    nbc = cols // D_HY
    t2_bf, t2i_bf = jnp.asarray(t2).astype(BF16), jnp.asarray(t2i).astype(BF16)
    t_fwd, t_inv_swapped, t_inv_natural = _kron_tables(t1, t1i, G)
    nf = filt.shape[1]

    f_re, f_im = _hy_stage1(filt.reshape(n2, h1, nf), t_fwd, G=G)
    blk = lambda c: pl.BlockSpec((kb, n2, c), lambda i: (i, 0, 0))
    hblk = lambda c: pl.BlockSpec((kb, k2, c), lambda i: (i, 0, 0))
    const = lambda s: pl.BlockSpec(s, lambda i: (0, 0))
    fshape = jax.ShapeDtypeStruct((n1, k2, nf), F32)
    fre, fim = pl.pallas_call(
        functools.partial(_hy_fspec2_kernel, kb=kb, k2=k2),
        grid=(n1 // kb,),
        in_specs=[blk(nf), blk(nf), const((n2, 2 * n2))],
        out_specs=[hblk(nf), hblk(nf)],
        out_shape=[fshape, fshape],
        compiler_params=_cp("arbitrary"),
        name="hy_fspec2",
    )(f_re, f_im, t2_bf)

    ashape = jax.ShapeDtypeStruct((n1, n2, cols), BF16)
    grp = pl.BlockSpec((n1, G, cols), lambda j: (0, j, 0))
    tspec = lambda t: pl.BlockSpec((1,) + t.shape[1:], lambda j: (j, 0, 0))
    for order in range(2):
        a_re, a_im = _hy_stage1(z3, t_fwd, G=G)
        b_re, b_im = pl.pallas_call(
            functools.partial(_hy_stage2_kernel, kb=kb, k2=k2, n2=n2, nbc=nbc, order=order),
            grid=(n1 // kb,),
            in_specs=[blk(cols), blk(cols), hblk(nf), hblk(nf),
                      pl.BlockSpec((1, 1, D_HY), lambda i: (order, 0, 0)),
                      const((n2, 2 * n2)), const((2 * n2, n2))],
            out_specs=[blk(cols), blk(cols)],
            out_shape=[ashape, ashape],
            compiler_params=_cp("arbitrary"),
            name="hy_stage2",
        )(a_re, a_im, fre, fim, skip[:, None, :], t2_bf, t2i_bf)
        if order == 0:
            table, gate = t_inv_swapped, g1_3
            io = pl.BlockSpec((G, h1, cols), lambda j: (j, 0, 0))
            oshape = jax.ShapeDtypeStruct((n2, h1, cols), F32)
        else:
            table, gate = t_inv_natural, g2.reshape(h1, n2, cols)
            io = pl.BlockSpec((h1, G, cols), lambda j: (0, j, 0))
            oshape = jax.ShapeDtypeStruct((h1, n2, cols), BF16)
        z3 = pl.pallas_call(
            functools.partial(_hy_stage3_kernel, G=G, n1=n1, ct=D_HY),
            grid=(n2 // G,),
            in_specs=[grp, grp, tspec(table), io],
            out_specs=io,
            out_shape=oshape,
            compiler_params=_cp("arbitrary"),
            name="hy_stage3",
        )(b_re, b_im, table, gate)
    return z3.reshape(L, cols)


def _hyena(hv_t, h1_t, h2_t, conv_w, conv_b, filt, skip):
    L, cols = hv_t.shape
    if _two_level(L):
        z3, g1_3, g2 = _hy_conv3(hv_t, h1_t, h2_t, conv_w, conv_b, swap=(True, True, False))
        return _hyena_two_level(z3, g1_3, g2, filt, skip)
    z, g1, g2 = _hy_conv3(hv_t, h1_t, h2_t, conv_w, conv_b)
    return _hyena_direct(z, g1, g2, filt, skip)


def _route(logits):
    lane = lax.broadcasted_iota(jnp.int32, logits.shape, 1)
    is_g = (lane >= N_EXPERTS) & (lane < N_EXPERTS + N_GROUPS)
    big = jnp.int32(1 << 20)
    gl = jnp.where(is_g, logits, NEG)
    gmax = jnp.max(gl, axis=-1, keepdims=True)
    gidx = jnp.min(jnp.where(gl == gmax, lane - N_EXPERTS, big), axis=-1, keepdims=True)
    p_group = 1.0 / jnp.sum(jnp.where(is_g, jnp.exp(gl - gmax), 0.0), axis=-1, keepdims=True)
    sel = (lane < N_EXPERTS) & ((lane // EXPERTS_PER_GROUP) == gidx)
    el = jnp.where(sel, logits, NEG)
    ee = jnp.where(sel, jnp.exp(el - jnp.max(el, axis=-1, keepdims=True)), 0.0)
    pe = ee / jnp.sum(ee, axis=-1, keepdims=True)
    p1 = jnp.max(jnp.where(sel, pe, -1.0), axis=-1, keepdims=True)
    i1 = jnp.min(jnp.where(sel & (pe == p1), lane, big), axis=-1, keepdims=True)
    rest = sel & (lane != i1)
    p2 = jnp.max(jnp.where(rest, pe, -1.0), axis=-1, keepdims=True)
    i2 = jnp.min(jnp.where(rest & (pe == p2), lane, big), axis=-1, keepdims=True)
    den = p1 + p2
    cmb = jnp.where(lane == i1, p_group * (p1 / den), jnp.where(lane == i2, p_group * (p2 / den), 0.0))
    return jnp.where(lane == N_EXPERTS, gidx.astype(F32), cmb)


def _outproj_kernel(att_ref, hf_ref, hb_ref, gt_ref, hy_ref, x_ref, mod_ref, w_ref, g_ref, b_ref,
                    wr_ref, br_ref, x1_ref, u2_ref, cmb_ref):
    rg = (hf_ref[...] + hb_ref[...]) * jax.nn.gelu(gt_ref[...])
    mix = (jnp.dot(att_ref[0], w_ref[0:D_ATT, :], preferred_element_type=F32)
           + jnp.dot(rg.astype(BF16), w_ref[D_ATT:D_ATT + D_RNN, :], preferred_element_type=F32)
           + jnp.dot(hy_ref[...].astype(BF16), w_ref[D_ATT + D_RNN:, :], preferred_element_type=F32))
    m = mod_ref[0]
    x1 = _ln(ALPHA * x_ref[...] + m[2:3] * mix) * g_ref[...] + b_ref[...]
    u2 = _ln(x1) * (1.0 + m[4:5]) + m[3:4]
    u_hi = u2.astype(BF16)
    u_lo = (u2 - u_hi.astype(F32)).astype(BF16)
    wr = wr_ref[...]
    w_hi = wr.astype(BF16)
    w_lo = (wr - w_hi.astype(F32)).astype(BF16)
    hi = jnp.dot(u_hi, jnp.concatenate([w_hi, w_lo], axis=1), preferred_element_type=F32)
    lo = jnp.dot(u_lo, w_hi, preferred_element_type=F32)
    logits = hi[:, 0:LANES] + hi[:, LANES:] + lo + br_ref[...]
    x1_ref[...] = x1
    u2_ref[...] = u_hi
    cmb_ref[...] = _route(logits)


def _outproj(att, hf_t, hb_t, gt_t, hy_t, x2d, mod_l, w_bf, ln_g, ln_b, wr, br, *, nb, L, tm,
             cond_base, cond_per_batch):
    tpb = L // tm
    d = x2d.shape[1]
    cond = (lambda i: cond_base + i // tpb) if cond_per_batch else (lambda i: cond_base)
    t_spec = pl.BlockSpec((tm, D_RNN), lambda i: (i % tpb, i // tpb))
    tok = lambda c: pl.BlockSpec((tm, c), lambda i: (i, 0))
    const = lambda s: pl.BlockSpec(s, lambda i: (0, 0))
    n = nb * L
    return pl.pallas_call(
        _outproj_kernel,
        grid=(nb * tpb,),
        in_specs=[pl.BlockSpec((1, tm, D_ATT), lambda i: (i // tpb, i % tpb, 0)),
                  t_spec, t_spec, t_spec, t_spec, tok(d),
                  pl.BlockSpec((1, 6, d), lambda i: (cond(i), 0, 0)),
                  const((d, d)), const((1, d)), const((1, d)), const((d, LANES)), const((1, LANES))],
        out_specs=[tok(d), tok(d), tok(LANES)],
        out_shape=[jax.ShapeDtypeStruct((n, d), F32), jax.ShapeDtypeStruct((n, d), BF16),
                   jax.ShapeDtypeStruct((n, LANES), F32)],
        compiler_params=_cp("arbitrary"),
        name="outproj",
    )(att, hf_t, hb_t, gt_t, hy_t, x2d, mod_l, w_bf, ln_g[None], ln_b[None], wr, br)


def _moe_kernel(u_ref, cmb_ref, x1_ref, mod_ref, wgu_ref, wd_ref, g_ref, b_ref, o_ref, ys_ref, *, T, BLK):
    SL = T + N_GROUPS * BLK
    u = u_ref[...]
    cmb = cmb_ref[...]
    lane_f = lax.broadcasted_iota(jnp.int32, (T, LANES), 1).astype(F32)
    gid_col = jnp.sum(jnp.where(lane_f == float(N_EXPERTS), cmb, 0.0), axis=-1, keepdims=True)
    gid_row = cmb.T[N_EXPERTS:N_EXPERTS + 1, :]
    sub = lax.broadcasted_iota(jnp.int32, (8, T), 0).astype(F32)
    onehot = jnp.where(lane_f == gid_col, 1.0, 0.0)
    onehot_t = jnp.where(sub == gid_row, 1.0, 0.0)
    ri = lax.broadcasted_iota(jnp.int32, (T, T), 0)
    ci = lax.broadcasted_iota(jnp.int32, (T, T), 1)
    before = jnp.where(ci < ri, 1.0, 0.0).astype(BF16)
    before_t = jnp.where(ri < ci, 1.0, 0.0).astype(BF16)
    rank_col = jnp.sum(onehot * jnp.dot(before, onehot.astype(BF16), preferred_element_type=F32),
                       axis=-1, keepdims=True)
    rank_row = jnp.sum(onehot_t * jnp.dot(onehot_t.astype(BF16), before_t, preferred_element_type=F32),
                       axis=0, keepdims=True)
    counts = jnp.sum(onehot, axis=0, keepdims=True)
    lane_row = lax.broadcasted_iota(jnp.int32, (1, LANES), 1)
    sub_col = lax.broadcasted_iota(jnp.int32, (8, 1), 0)
    base = jnp.int32(0)
    bases = []
    base_row = jnp.zeros((1, LANES), jnp.int32)
    base_sub = jnp.zeros((8, 1), jnp.int32)
    for c in range(N_GROUPS):
        n_c = jnp.sum(jnp.where(lane_row == c, counts, 0.0)).astype(jnp.int32)
        bases.append(base)
        base_row = jnp.where(lane_row == c, base, base_row)
        base_sub = jnp.where(sub_col == c, base, base_sub)
        base = base + ((n_c + (BLK - 1)) // BLK) * BLK
    total = base
    dest_col = (jnp.sum(onehot * base_row.astype(F32), axis=-1, keepdims=True) + rank_col).astype(jnp.int32)
    dest_row = (jnp.sum(onehot_t * base_sub.astype(F32), axis=0, keepdims=True) + rank_row).astype(jnp.int32)
    cmb_hi = cmb.astype(BF16)
    cmb_lo = (cmb - cmb_hi.astype(F32)).astype(BF16)
    blk_rows = lax.broadcasted_iota(jnp.int32, (BLK, T), 0)
    blk_lane = lax.broadcasted_iota(jnp.int32, (BLK, LANES), 1)

    def block(b, carry):
        r0 = b * BLK
        rows = pl.ds(pl.multiple_of(r0, BLK), BLK)

        @pl.when(r0 < total)
        def _():
            grp = ((r0 >= bases[1]).astype(jnp.int32) + (r0 >= bases[2]).astype(jnp.int32)
                   + (r0 >= bases[3]).astype(jnp.int32))
            p = jnp.where(blk_rows + r0 == dest_row, 1.0, 0.0).astype(BF16)
            xb = jnp.dot(p, u, preferred_element_type=F32).astype(BF16)
            wb = (jnp.dot(p, cmb_hi, preferred_element_type=F32)
                  + jnp.dot(p, cmb_lo, preferred_element_type=F32))
            acc = jnp.zeros((BLK, u.shape[1]), F32)
            for j in range(EXPERTS_PER_GROUP):
                e = grp * EXPERTS_PER_GROUP + j
                gu = jnp.dot(xb, wgu_ref[e], preferred_element_type=F32)
                gate = gu[:, :D_EXPERT]
                w = jnp.sum(jnp.where(blk_lane == e, wb, 0.0), axis=-1, keepdims=True)
                h = gate * jax.nn.sigmoid(gate) * gu[:, D_EXPERT:] * w
                acc = acc + jnp.dot(h.astype(BF16), wd_ref[e], preferred_element_type=F32)
            ys_ref[rows, :] = acc.astype(BF16)

        @pl.when(r0 >= total)
        def _():
            ys_ref[rows, :] = jnp.zeros((BLK, u.shape[1]), BF16)

        return carry

    lax.fori_loop(0, SL // BLK, block, 0)
    p_t = jnp.where(lax.broadcasted_iota(jnp.int32, (T, SL), 1) == dest_col, 1.0, 0.0).astype(BF16)
    moe = jnp.dot(p_t, ys_ref[...], preferred_element_type=F32)
    m = mod_ref[0]
    o_ref[...] = _ln(ALPHA * x1_ref[...] + m[5:6] * moe) * g_ref[...] + b_ref[...]


def _moe(u2, cmb, x1, mod_l, wgu, wd, ln_g, ln_b, *, tm, tiles_per_cond, cond_base, cond_per_batch):
    n, d = x1.shape
    ne = wgu.shape[0]
    cond = (lambda i: cond_base + i // tiles_per_cond) if cond_per_batch else (lambda i: cond_base)
    tok = lambda c: pl.BlockSpec((tm, c), lambda i: (i, 0))
    const = lambda s: pl.BlockSpec(s, lambda i: (0,) * len(s))
    resident = lambda s: pl.BlockSpec(s, lambda i: (0,) * len(s), pipeline_mode=pl.Buffered(1))
    return pl.pallas_call(
        functools.partial(_moe_kernel, T=tm, BLK=MOE_BLK),
        grid=(n // tm,),
        in_specs=[tok(d), tok(LANES), tok(d),
                  pl.BlockSpec((1, 6, d), lambda i: (cond(i), 0, 0)),
                  resident((ne, d, 2 * D_EXPERT)), resident((ne, D_EXPERT, d)),
                  const((1, d)), const((1, d))],
        out_specs=tok(d),
        out_shape=jax.ShapeDtypeStruct((n, d), F32),
        scratch_shapes=[pltpu.VMEM((tm + N_GROUPS * MOE_BLK, d), BF16)],
        compiler_params=_cp("arbitrary"),
        name="moe",
    )(u2, cmb, x1, mod_l, wgu, wd, ln_g[None], ln_b[None])


def _stream_layer(x2d, mod_l, p, *, nb, L, tm, cond_base, cond_per_batch, ctx, cache=None, h0=None):
    kw = dict(nb=nb, L=L, tm=tm, cond_base=cond_base, cond_per_batch=cond_per_batch)
    q, k, v, xr_t, gt_t, hv_t, h1_t, h2_t = _inproj(x2d, mod_l, p["w_in"], kv_dtype=F32 if ctx else BF16, **kw)
    if ctx:
        att = _ctx_attention(q, k, v)
    else:
        att = _nbr_attention(q, k, v, cache[0], cache[1], p["nbr_bias"])
    a_f, b_f, a_b, b_b = _rg_gates(xr_t, p["rg_conv_w"], p["rg_conv_b"], *p["rg_w"])
    hf_t, hb_t = _rg_scan(a_f, b_f, a_b, b_b, h0[0], h0[1])
    hy_t = _hyena(hv_t, h1_t, h2_t, p["hy_conv_w"], p["hy_conv_b"], p["hy_filt"][L], p["hy_skip"])
    x1, u2, cmb = _outproj(att, hf_t, hb_t, gt_t, hy_t, x2d, mod_l, p["w_out"], p["ln1_g"], p["ln1_b"],
                           p["router_w"], p["router_b"], **kw)
    tm_moe = min(MOE_TILE, nb * L)
    x2 = _moe(u2, cmb, x1, mod_l, p["moe_wgu"], p["moe_wd"], p["ln2_g"], p["ln2_b"], tm=tm_moe,
              tiles_per_cond=max(L // tm_moe, 1), cond_base=cond_base, cond_per_batch=cond_per_batch)
    return x2, k, v, hf_t, hb_t


def kernel(x_prompt, x_sample, cache_k, cache_v, state_rglru, c, c_ctx, w_mod, b_mod, w_in, attn_rpb, rg_conv_w, rg_conv_b, rg_wa, rg_ba, rg_wx, rg_bx, rg_lambda, hy_conv_w, hy_conv_b, hy_w1, hy_b1, hy_w2, hy_b2, hy_w3, hy_freq, hy_decay, hy_skip, w_out, ln1_g, ln1_b, router_wg, router_bg, router_we, router_be, moe_w_gate, moe_w_up, moe_w_down, ln2_g, ln2_b):
    nbc, Lc, d = x_prompt.shape
    nbl, Ll, _ = x_sample.shape
    depth = w_mod.shape[0]
    past = cache_k.shape[2]

    cond8 = jnp.zeros((8, d), F32).at[0].set(c_ctx).at[1:1 + nbl].set(c)
    mod = _modulation(cond8, w_mod, b_mod).reshape(depth, 8, 6, d)

    xp = x_prompt.reshape(nbc * Lc, d)
    xs = x_sample.reshape(nbl * Ll, d)
    new_k, new_v, new_h = [], [], []
    for l in range(depth):
        pad = jnp.zeros((d, LANES - N_EXPERTS - N_GROUPS), F32)
        p = {
            "w_in": w_in[l].astype(BF16),
            "nbr_bias": _nbr_bias_tables(attn_rpb[l], Ll // GRID_W),
            "rg_conv_w": rg_conv_w[l], "rg_conv_b": rg_conv_b[l],
            "rg_w": _rg_weights(rg_wa[l], rg_ba[l], rg_wx[l], rg_bx[l], rg_lambda[l]),
            "hy_conv_w": hy_conv_w[l], "hy_conv_b": hy_conv_b[l], "hy_skip": hy_skip[l],
            "hy_filt": {L: _hy_filters(L, hy_w1[l], hy_b1[l], hy_w2[l], hy_b2[l], hy_w3[l], hy_freq[l], hy_decay[l])
                        for L in sorted({Lc, Ll})},
            "w_out": w_out[l].astype(BF16), "ln1_g": ln1_g[l], "ln1_b": ln1_b[l],
            "router_w": jnp.concatenate([router_we[l], router_wg[l], pad], axis=1),
            "router_b": jnp.concatenate([router_be[l], router_bg[l], pad[0]])[None],
            "moe_wgu": jnp.concatenate([moe_w_gate[l], moe_w_up[l]], axis=-1).astype(BF16),
            "moe_wd": moe_w_down[l].astype(BF16),
            "ln2_g": ln2_g[l], "ln2_b": ln2_b[l],
        }
        zeros_h = jnp.zeros((1, nbc * D_RNN), F32)
        xp, k, v, hf_t, hb_t = _stream_layer(
            xp, mod[l], p, nb=nbc, L=Lc, tm=Lc, cond_base=0, cond_per_batch=False, ctx=True,
            h0=(zeros_h, zeros_h))
        new_k.append(k.reshape(nbc, Lc, N_HEADS, HEAD_DIM))
        new_v.append(v.reshape(nbc, Lc, N_HEADS, HEAD_DIM))
        new_h.append(jnp.stack([hf_t[Lc - 1].reshape(nbc, D_RNN), hb_t[0].reshape(nbc, D_RNN)], axis=1))

        cache = (cache_k[:, l].reshape(nbl, past, D_ATT), cache_v[:, l].reshape(nbl, past, D_ATT))
        h0 = (state_rglru[:, l, 0].reshape(1, nbl * D_RNN), state_rglru[:, l, 1].reshape(1, nbl * D_RNN))
        xs, _, _, _, _ = _stream_layer(
            xs, mod[l], p, nb=nbl, L=Ll, tm=min(Ll, 512), cond_base=1, cond_per_batch=True, ctx=False,
            cache=cache, h0=h0)

    return (xp.reshape(nbc, Lc, d), xs.reshape(nbl, Ll, d),
            jnp.stack(new_k, axis=1), jnp.stack(new_v, axis=1), jnp.stack(new_h, axis=1))
```

```python
import functools
import math

import numpy as np
import jax
import jax.numpy as jnp
from jax import lax
from jax.experimental import pallas as pl
from jax.experimental.pallas import tpu as pltpu

F32 = jnp.float32
BF16 = jnp.bfloat16
HIGHEST = lax.Precision.HIGHEST

DEPTH = 2
D_MODEL = 1024
N_HEADS = 8
HEAD_DIM = 64
D_ATT = N_HEADS * HEAD_DIM
D_RNN = 256
D_HY = 256
GRID_W = 64
WIN_R = 8
WIN_C = 16
RG_CONV = 4
RG_C = 8.0
HY_CONV = 3
POS_EMB_DIM = 33
N_GROUPS = 4
EXPERTS_PER_GROUP = 4
N_EXPERTS = 16
D_EXPERT = 256
ALPHA = (2.0 * DEPTH) ** 0.25
LN_EPS = 1e-5
NEG = -1e30

Q_ROWS = 4
K_ROWS = Q_ROWS + WIN_R
LANES = 128
HALO = 8
MOE_TILE = 512
MOE_BLK = 128
STRIDE_PAD = 4
FFT_G = 16
FFT_N1 = 64
VMEM_LIMIT = 52 * 1024 * 1024


def _cp(*sem):
    return pltpu.CompilerParams(dimension_semantics=sem, vmem_limit_bytes=VMEM_LIMIT)


def _ln(x):
    mu = jnp.mean(x, axis=-1, keepdims=True)
    xc = x - mu
    var = jnp.mean(xc * xc, axis=-1, keepdims=True)
    return xc * lax.rsqrt(var + LN_EPS)


def _mod_kernel(c_ref, w_ref, b_ref, o_ref):
    c = c_ref[...]
    s = c * jax.nn.sigmoid(c)
    o_ref[0] = jnp.dot(s, w_ref[0], precision=HIGHEST, preferred_element_type=F32) + b_ref[0]


def _modulation(cond8, w_mod, b_mod):
    depth, d, n = w_mod.shape
    tn = n // 4
    return pl.pallas_call(
        _mod_kernel,
        grid=(depth, n // tn),
        in_specs=[pl.BlockSpec((8, d), lambda l, j: (0, 0)),
                  pl.BlockSpec((1, d, tn), lambda l, j: (l, 0, j)),
                  pl.BlockSpec((1, 1, tn), lambda l, j: (l, 0, j))],
        out_specs=pl.BlockSpec((1, 8, tn), lambda l, j: (l, 0, j)),
        out_shape=jax.ShapeDtypeStruct((depth, 8, n), F32),
        compiler_params=_cp("arbitrary", "arbitrary"),
        name="modulation",
    )(cond8, w_mod, b_mod.reshape(depth, 1, n))


def _inproj_kernel(x_ref, mod_ref, w_ref, q_ref, k_ref, v_ref, xr_ref, gt_ref, hv_ref, h1_ref, h2_ref):
    m = mod_ref[0]
    u = (_ln(x_ref[...]) * (1.0 + m[1:2]) + m[0:1]).astype(BF16)

    def proj(lo, hi):
        return jnp.dot(u, w_ref[:, lo:hi], preferred_element_type=F32)

    o = 0
    q_ref[0] = (proj(o, o + D_ATT) * (HEAD_DIM ** -0.5)).astype(q_ref.dtype)
    o += D_ATT
    k_ref[0] = proj(o, o + D_ATT).astype(k_ref.dtype)
    o += D_ATT
    v_ref[0] = proj(o, o + D_ATT).astype(v_ref.dtype)
    o += D_ATT
    for ref in (xr_ref, gt_ref, hv_ref, h1_ref, h2_ref):
        ref[...] = proj(o, o + D_RNN)
        o += D_RNN


def _inproj(x2d, mod_l, w_bf, *, nb, L, tm, cond_base, cond_per_batch, kv_dtype):
    tpb = L // tm
    d = x2d.shape[1]
    d_in = w_bf.shape[1]
    cond = (lambda i: cond_base + i // tpb) if cond_per_batch else (lambda i: cond_base)
    att_spec = pl.BlockSpec((1, tm, D_ATT), lambda i: (i // tpb, i % tpb, 0))
    t_spec = pl.BlockSpec((tm, D_RNN), lambda i: (i % tpb, i // tpb))
    att_shape = lambda dt: jax.ShapeDtypeStruct((nb, L, D_ATT), dt)
    t_shape = jax.ShapeDtypeStruct((L, nb * D_RNN), F32)
    return pl.pallas_call(
        _inproj_kernel,
        grid=(nb * tpb,),
        in_specs=[pl.BlockSpec((tm, d), lambda i: (i, 0)),
                  pl.BlockSpec((1, 6, d), lambda i: (cond(i), 0, 0)),
                  pl.BlockSpec((d, d_in), lambda i: (0, 0))],
        out_specs=[att_spec, att_spec, att_spec, t_spec, t_spec, t_spec, t_spec, t_spec],
        out_shape=[att_shape(BF16), att_shape(kv_dtype), att_shape(kv_dtype),
                   t_shape, t_shape, t_shape, t_shape, t_shape],
        compiler_params=_cp("arbitrary"),
        name="inproj",
    )(x2d, mod_l, w_bf)


def _nt(a, b):
    return lax.dot_general(a, b, (((1,), (1,)), ((), ())), preferred_element_type=F32)


def _ctx_attn_kernel(q_ref, k_ref, v_ref, o_ref):
    q = q_ref[0]
    k = k_ref[0].astype(BF16)
    v = v_ref[0].astype(BF16)
    outs = []
    for h in range(N_HEADS):
        sl = slice(h * HEAD_DIM, (h + 1) * HEAD_DIM)
        s = _nt(q[:, sl], k[:, sl])
        e = jnp.exp(s - jnp.max(s, axis=-1, keepdims=True))
        den = jnp.sum(e, axis=-1, keepdims=True)
        outs.append(jnp.dot(e.astype(BF16), v[:, sl], preferred_element_type=F32) / den)
    o_ref[0] = jnp.concatenate(outs, axis=-1).astype(o_ref.dtype)


def _ctx_attention(q, k, v):
    nb, L, _ = q.shape
    spec = pl.BlockSpec((1, L, D_ATT), lambda b: (b, 0, 0))
    return pl.pallas_call(
        _ctx_attn_kernel,
        grid=(nb,),
        in_specs=[spec, spec, spec],
        out_specs=spec,
        out_shape=jax.ShapeDtypeStruct((nb, L, D_ATT), BF16),
        compiler_params=_cp("arbitrary"),
        name="ctx_attn",
    )(q, k, v)


def _nbr_bias_tables(rpb, rows):
    ng = rows // Q_ROWS
    cols = np.arange(GRID_W)
    c_start = np.clip(cols - WIN_C // 2, 0, GRID_W - WIN_C)
    col_mask = (cols[None, :] >= c_start[:, None]) & (cols[None, :] < c_start[:, None] + WIN_C)
    dc_idx = np.clip(cols[None, :] - cols[:, None], -(WIN_C - 1), WIN_C - 1) + (WIN_C - 1)
    sel = (dc_idx[None] == np.arange(2 * WIN_C - 1)[:, None, None]).astype(np.float32)
    colb = jnp.einsum('hab,bqk->haqk', rpb.astype(F32), jnp.asarray(sel), precision=HIGHEST)
    colb = jnp.where(col_mask[None, None], colb, NEG)
    masked = jnp.full((N_HEADS, GRID_W, GRID_W), NEG, F32)
    tables = []
    for g in (0, 1, ng - 1):
        r = Q_ROWS * g + np.arange(Q_ROWS)
        r0 = int(np.clip(Q_ROWS * g - WIN_R // 2, 0, rows - K_ROWS))
        start = np.clip(r - WIN_R // 2, 0, rows - WIN_R)
        krow = r0 + np.arange(K_ROWS)
        valid = (krow[None, :] >= start[:, None]) & (krow[None, :] < start[:, None] + WIN_R)
        dr = krow[None, :] - r[:, None] + (WIN_R - 1)
        tables.append(jnp.concatenate(
            [jnp.concatenate([colb[:, int(dr[j, i])] if valid[j, i] else masked for i in range(K_ROWS)], axis=-1)
             for j in range(Q_ROWS)], axis=-2))
    return jnp.stack(tables)


def _nbr_attn_kernel(q_ref, k_ref, v_ref, kc_ref, vc_ref, bias_ref, o_ref, *, rows):
    g = pl.program_id(1)
    r0 = jnp.clip(Q_ROWS * g - WIN_R // 2, 0, rows - K_ROWS)
    start = pl.multiple_of(r0 * GRID_W, Q_ROWS * GRID_W)
    nk = K_ROWS * GRID_W
    kw = k_ref[0, pl.ds(start, nk), :]
    vw = v_ref[0, pl.ds(start, nk), :]
    kc = kc_ref[0].astype(BF16)
    vc = vc_ref[0].astype(BF16)
    q = q_ref[0]
    outs = []
    for h in range(N_HEADS):
        sl = slice(h * HEAD_DIM, (h + 1) * HEAD_DIM)
        qh = q[:, sl]
        s1 = _nt(qh, kw[:, sl]) + bias_ref[0, h]
        s2 = _nt(qh, kc[:, sl])
        m = jnp.maximum(jnp.max(s1, axis=-1, keepdims=True), jnp.max(s2, axis=-1, keepdims=True))
        e1 = jnp.exp(s1 - m)
        e2 = jnp.exp(s2 - m)
        den = jnp.sum(e1, axis=-1, keepdims=True) + jnp.sum(e2, axis=-1, keepdims=True)
        o = (jnp.dot(e1.astype(BF16), vw[:, sl], preferred_element_type=F32)
             + jnp.dot(e2.astype(BF16), vc[:, sl], preferred_element_type=F32))
        outs.append(o / den)
    o_ref[0] = jnp.concatenate(outs, axis=-1).astype(o_ref.dtype)


def _nbr_attention(q, k, v, kc, vc, bias):
    nb, L, _ = q.shape
    rows = L // GRID_W
    ng = rows // Q_ROWS
    past = kc.shape[1]
    tq = Q_ROWS * GRID_W
    full = pl.BlockSpec((1, L, D_ATT), lambda b, g: (b, 0, 0))
    ctx = pl.BlockSpec((1, past, D_ATT), lambda b, g: (b, 0, 0))
    qspec = pl.BlockSpec((1, tq, D_ATT), lambda b, g: (b, g, 0))
    bspec = pl.BlockSpec((1, N_HEADS, tq, K_ROWS * GRID_W),
                         lambda b, g: (jnp.minimum(g, 1) + g // (ng - 1), 0, 0, 0))
    return pl.pallas_call(
        functools.partial(_nbr_attn_kernel, rows=rows),
        grid=(nb, ng),
        in_specs=[qspec, full, full, ctx, ctx, bspec],
        out_specs=qspec,
        out_shape=jax.ShapeDtypeStruct((nb, L, D_ATT), BF16),
        compiler_params=_cp("arbitrary", "arbitrary"),
        name="nbr_attn",
    )(q, k, v, kc, vc, bias)


def _fill_padded(xp_ref, src_ref, L):
    z = jnp.zeros((HALO, xp_ref.shape[1]), F32)
    xp_ref[0:HALO, :] = z
    xp_ref[HALO:HALO + L, :] = src_ref[...]
    xp_ref[HALO + L:2 * HALO + L, :] = z


def _rg_gates_kernel(x_ref, cw_ref, cb_ref, w_ref, b_ref, lam_ref,
                     af_ref, bf_ref, ab_ref, bb_ref, xp_ref, *, L, tc):
    _fill_padded(xp_ref, x_ref, L)
    sp = jax.nn.softplus(-lam_ref[0])
    for t0 in range(0, L, tc):
        xc = cb_ref[...]
        for k in range(RG_CONV):
            xc = xc + cw_ref[k:k + 1, :] * xp_ref[pl.ds(HALO + t0 + k - RG_CONV // 2, tc), :]
        z = jnp.dot(xc.astype(BF16), w_ref[0], preferred_element_type=F32) + b_ref[0]
        for e, (a_ref, o_ref) in enumerate(((af_ref, bf_ref), (ab_ref, bb_ref))):
            r = jax.nn.sigmoid(z[:, e * LANES:(e + 1) * LANES])
            i = jax.nn.sigmoid(z[:, (2 + e) * LANES:(3 + e) * LANES])
            log_a = (-RG_C) * r * sp[:, e * LANES:(e + 1) * LANES]
            a = jnp.exp(log_a)
            one_minus_a2 = -jnp.tanh(log_a) * (a * a + 1.0)
            a_ref[pl.ds(t0, tc), :] = a
            o_ref[pl.ds(t0, tc), :] = jnp.sqrt(one_minus_a2) * i * xc


def _blockdiag2(w2):
    z = jnp.zeros_like(w2[0])
    return jnp.concatenate([jnp.concatenate([w2[0], z], 1), jnp.concatenate([z, w2[1]], 1)], 0)


def _rg_weights(wa, ba, wx, bx, lam):
    ws, bs, ls = [], [], []
    for hf in range(D_RNN // LANES):
        blk = slice(2 * hf, 2 * hf + 2)
        ch = slice(hf * LANES, (hf + 1) * LANES)
        ws.append(jnp.concatenate([_blockdiag2(wa[0, blk]), _blockdiag2(wa[1, blk]),
                                   _blockdiag2(wx[0, blk]), _blockdiag2(wx[1, blk])], axis=1))
        bs.append(jnp.concatenate([ba[0, ch], ba[1, ch], bx[0, ch], bx[1, ch]])[None])
        ls.append(jnp.concatenate([lam[0, ch], lam[1, ch]])[None])
    return jnp.stack(ws).astype(BF16), jnp.stack(bs), jnp.stack(ls)


def _rg_gates(xr_t, conv_w, conv_b, w_bd, b_cat, lam_cat):
    L, cols = xr_t.shape
    nh = D_RNN // LANES
    tc = min(L, 512)
    col = pl.BlockSpec((L, LANES), lambda c: (0, c))
    shape = jax.ShapeDtypeStruct((L, cols), F32)
    return pl.pallas_call(
        functools.partial(_rg_gates_kernel, L=L, tc=tc),
        grid=(cols // LANES,),
        in_specs=[col,
                  pl.BlockSpec((RG_CONV, LANES), lambda c: (0, c % nh)),
                  pl.BlockSpec((1, LANES), lambda c: (0, c % nh)),
                  pl.BlockSpec((1, LANES, 4 * LANES), lambda c: (c % nh, 0, 0)),
                  pl.BlockSpec((1, 1, 4 * LANES), lambda c: (c % nh, 0, 0)),
                  pl.BlockSpec((1, 1, 2 * LANES), lambda c: (c % nh, 0, 0))],
        out_specs=[col, col, col, col],
        out_shape=[shape, shape, shape, shape],
        scratch_shapes=[pltpu.VMEM((L + 2 * HALO, LANES), F32)],
        compiler_params=_cp("arbitrary"),
        name="rg_gates",
    )(xr_t, conv_w, conv_b.reshape(1, D_RNN), w_bd, b_cat, lam_cat)


def _rg_scan_kernel(af_ref, bf_ref, ab_ref, bb_ref, h0f_ref, h0b_ref, hf_ref, hb_ref, cf_ref, cb_ref, *, tc):
    @pl.when(pl.program_id(1) == 0)
    def _():
        cf_ref[...] = h0f_ref[...]
        cb_ref[...] = h0b_ref[...]

    def body(t, carry):
        hf, hb = carry
        hf = af_ref[pl.ds(t, 1), :] * hf + bf_ref[pl.ds(t, 1), :]
        hf_ref[pl.ds(t, 1), :] = hf
        tb = tc - 1 - t
        hb = ab_ref[pl.ds(tb, 1), :] * hb + bb_ref[pl.ds(tb, 1), :]
        hb_ref[pl.ds(tb, 1), :] = hb
        return hf, hb

    hf, hb = lax.fori_loop(0, tc, body, (cf_ref[...], cb_ref[...]), unroll=8)
    cf_ref[...] = hf
    cb_ref[...] = hb


def _rg_scan(a_f, b_f, a_b, b_b, h0f, h0b):
    L, cols = a_f.shape
    cw = min(cols, 1024)
    tc = min(L, 512)
    nt = L // tc
    fwd = pl.BlockSpec((tc, cw), lambda c, i: (i, c))
    bwd = pl.BlockSpec((tc, cw), lambda c, i: (nt - 1 - i, c))
    row = pl.BlockSpec((1, cw), lambda c, i: (0, c))
    shape = jax.ShapeDtypeStruct((L, cols), F32)
    return pl.pallas_call(
        functools.partial(_rg_scan_kernel, tc=tc),
        grid=(cols // cw, nt),
        in_specs=[fwd, fwd, bwd, bwd, row, row],
        out_specs=[fwd, bwd],
        out_shape=[shape, shape],
        scratch_shapes=[pltpu.VMEM((1, cw), F32), pltpu.VMEM((1, cw), F32)],
        compiler_params=_cp("arbitrary", "arbitrary"),
        name="rg_scan",
    )(a_f, b_f, a_b, b_b, h0f, h0b)


def _hy_filter_kernel(feat_ref, w1_ref, b1_ref, w2_ref, b2_ref, w3_ref, fr_ref, dec_ref, o_ref, *, tm):
    feats = feat_ref[...]
    dot = functools.partial(jnp.dot, precision=HIGHEST, preferred_element_type=F32)
    h = jnp.sin(fr_ref[0:1, :] * (dot(feats, w1_ref[...]) + b1_ref[...]))
    h = jnp.sin(fr_ref[1:2, :] * (dot(h, w2_ref[...]) + b2_ref[...]))
    h = dot(h, w3_ref[...])
    filt = h * jnp.exp(-feats[:, 0:1] * jnp.abs(dec_ref[...]))
    row = pl.program_id(0) * tm + lax.broadcasted_iota(jnp.int32, filt.shape, 0)
    lane = lax.broadcasted_iota(jnp.int32, filt.shape, 1)
    o_ref[...] = jnp.where((row == 0) & (lane >= 2 * D_HY), 0.0, filt)


def _hy_features(L):
    pos = np.arange(L, dtype=np.float64)
    t = np.linspace(0.0, 1.0, L)
    bands = (POS_EMB_DIM - 1) // 2
    f = np.linspace(1e-4, bands - 1, bands)
    ang = (2.0 * math.pi / L) * pos[:, None] * f[None, :]
    feats = np.concatenate([t[:, None], np.cos(ang), -np.sin(ang)], axis=-1)
    out = np.zeros((L, LANES), np.float32)
    out[:, :POS_EMB_DIM] = feats
    return out


def _two_level(L):
    return L % (FFT_N1 * 8) == 0 and L >= 2048


def _digit_swap(x, L):
    n2 = 2 * L // FFT_N1
    return x.reshape(L // n2, n2, -1).transpose(1, 0, 2).reshape(L, -1)


def _hy_filters(L, w1, b1, w2, b2, w3, freq, decay):
    tm = min(L, 512)
    hid = w2.shape[0]
    n = w3.shape[1]
    w1p = jnp.zeros((LANES, hid), F32).at[:POS_EMB_DIM].set(w1)
    feats = _hy_features(L)
    if _two_level(L):
        feats = _digit_swap(feats, L)
    c = lambda s: pl.BlockSpec(s, lambda i: (0, 0))
    return pl.pallas_call(
        functools.partial(_hy_filter_kernel, tm=tm),
        grid=(L // tm,),
        in_specs=[pl.BlockSpec((tm, LANES), lambda i: (i, 0)),
                  c((LANES, hid)), c((1, hid)), c((hid, hid)), c((1, hid)), c((hid, n)), c((2, hid)), c((1, n))],
        out_specs=pl.BlockSpec((tm, n), lambda i: (i, 0)),
        out_shape=jax.ShapeDtypeStruct((L, n), F32),
        compiler_params=_cp("arbitrary"),
        name="hy_filter",
    )(jnp.asarray(feats), w1p, b1[None], w2, b2[None], w3, freq, decay.reshape(1, n))


def _conv3_kernel(v_ref, a_ref, b_ref, w_ref, cb_ref, ov_ref, oa_ref, ob_ref, xp_ref, xs_ref, *, L, tc, swap, n2):
    h1 = L // n2
    pitch = n2 + STRIDE_PAD
    for p, (src, dst) in enumerate(((v_ref, ov_ref), (a_ref, oa_ref), (b_ref, ob_ref))):
        _fill_padded(xp_ref, src, L)
        if swap[p]:
            for i1 in range(h1):
                xs_ref[pl.ds(i1 * pitch, n2 + HY_CONV - 1), :] = (
                    xp_ref[pl.ds(HALO + i1 * n2 - HY_CONV // 2, n2 + HY_CONV - 1), :])

            def body(i2, carry, p=p, dst=dst):
                y = cb_ref[p]
                for k in range(HY_CONV):
                    y = y + w_ref[p, k:k + 1, :] * xs_ref[pl.ds(i2 + k, h1, stride=pitch), :]
                dst[i2] = y
                return carry

            lax.fori_loop(0, n2, body, 0)
        else:
            for t0 in range(0, L, tc):
                y = cb_ref[p]
                for k in range(HY_CONV):
                    y = y + w_ref[p, k:k + 1, :] * xp_ref[pl.ds(HALO + t0 + k - HY_CONV // 2, tc), :]
                dst[pl.ds(t0, tc), :] = y


def _hy_conv3(v_t, a_t, b_t, conv_w, conv_b, swap=(False, False, False)):
    L, cols = v_t.shape
    nh = D_HY // LANES
    tc = min(L, 512)
    n2 = 2 * L // FFT_N1
    col = pl.BlockSpec((L, LANES), lambda c: (0, c))
    col3 = pl.BlockSpec((n2, L // n2, LANES), lambda c: (0, 0, c))
    shape = jax.ShapeDtypeStruct((L, cols), F32)
    shape3 = jax.ShapeDtypeStruct((n2, L // n2, cols), F32)
    w = conv_w.reshape(HY_CONV, 3, D_HY).transpose(1, 0, 2)
    cb = conv_b.reshape(3, 1, D_HY)
    return pl.pallas_call(
        functools.partial(_conv3_kernel, L=L, tc=tc, swap=swap, n2=n2),
        grid=(cols // LANES,),
        in_specs=[col, col, col,
                  pl.BlockSpec((3, HY_CONV, LANES), lambda c: (0, 0, c % nh)),
                  pl.BlockSpec((3, 1, LANES), lambda c: (0, 0, c % nh))],
        out_specs=[col3 if s else col for s in swap],
        out_shape=[shape3 if s else shape for s in swap],
        scratch_shapes=[pltpu.VMEM((L + 2 * HALO, LANES), F32),
                        pltpu.VMEM(((L // n2) * (n2 + STRIDE_PAD) + HALO, LANES), F32)],
        compiler_params=_cp("arbitrary"),
        name="hy_conv3",
    )(v_t, a_t, b_t, w, cb)


def _phase(p, n2):
    ang = 2.0 * np.pi * (np.asarray(p, np.int64) % n2) / n2
    return np.cos(ang), np.sin(ang)


def _dft_direct_tables(L):
    k = np.arange(L)[:, None]
    t = np.arange(L)[None, :]
    c, s = _phase((2 * k + 1) * t, 4 * L)
    fwd = np.concatenate([c, -s], axis=0)
    inv = (1.0 / L) * np.concatenate([c.T, -s.T], axis=1)
    return fwd.astype(np.float32), inv.astype(np.float32)


def _dft_two_level_tables(L, n1):
    N = 2 * L
    n2 = N // n1
    h1 = n1 // 2
    i2 = np.arange(n2)[:, None, None]
    k1 = np.arange(n1)[None, :, None]
    i1 = np.arange(h1)[None, None, :]
    c, s = _phase(2 * n2 * i1 * k1 + 2 * i2 * k1 + n2 * i1 + i2, 2 * N)
    t1 = np.concatenate([c, -s], axis=1)
    ct, st = c.transpose(0, 2, 1), s.transpose(0, 2, 1)
    t1i = (2.0 / N) * np.concatenate([ct, -st], axis=2)
    kk = np.arange(n2 // 2)[:, None]
    nn = np.arange(n2)[None, :]
    c2, s2 = _phase(nn * kk, n2)
    t2 = np.block([[c2, s2], [-s2, c2]])
    t2i = np.block([[c2.T, -s2.T], [s2.T, c2.T]])
    return tuple(x.astype(np.float32) for x in (t1, t2, t2i, t1i))


def _filter_spectrum(fre, fim, order, skip):
    fw = slice(order * D_HY, (order + 1) * D_HY)
    bw = slice((2 + order) * D_HY, (3 + order) * D_HY)
    return fre[:, fw] + fre[:, bw] + skip, fim[:, fw] - fim[:, bw]


def _spectral_product(xr, xi, fr, fi, nbc):
    yr, yi = [], []
    for b in range(nbc):
        sl = slice(b * D_HY, (b + 1) * D_HY)
        yr.append(xr[:, sl] * fr - xi[:, sl] * fi)
        yi.append(xr[:, sl] * fi + xi[:, sl] * fr)
    return jnp.concatenate([jnp.concatenate(yr, axis=1), jnp.concatenate(yi, axis=1)], axis=0)


def _matmul_kernel(a_ref, b_ref, o_ref):
    o_ref[...] = jnp.dot(a_ref[...], b_ref[...], precision=HIGHEST, preferred_element_type=F32)


def _hy_direct_kernel(z_ref, g_ref, f_ref, skip_ref, t_ref, ti_ref, o_ref, *, L, nbc, order):
    x = jnp.dot(t_ref[...], z_ref[...].astype(BF16), preferred_element_type=F32)
    fr, fi = _filter_spectrum(f_ref[0:L, :], f_ref[L:2 * L, :], order, skip_ref[0])
    y = _spectral_product(x[:L], x[L:], fr, fi, nbc).astype(BF16)
    conv = jnp.dot(ti_ref[...], y, preferred_element_type=F32)
    o_ref[...] = (g_ref[...] * conv).astype(o_ref.dtype)


def _hyena_direct(z, g1, g2, filt, skip):
    L, cols = z.shape
    fwd, inv = _dft_direct_tables(L)
    n = filt.shape[1]
    spec = pl.pallas_call(
        _matmul_kernel,
        out_shape=jax.ShapeDtypeStruct((2 * L, n), F32),
        name="hy_fspec",
    )(jnp.asarray(fwd), filt)
    tn = min(cols, 4 * D_HY)
    nbc = tn // D_HY
    col = pl.BlockSpec((L, tn), lambda c: (0, c))
    const = lambda s: pl.BlockSpec(s, lambda c: (0, 0))
    t_bf, ti_bf = jnp.asarray(fwd).astype(BF16), jnp.asarray(inv).astype(BF16)
    for order, g in enumerate((g1, g2)):
        z = pl.pallas_call(
            functools.partial(_hy_direct_kernel, L=L, nbc=nbc, order=order),
            grid=(cols // tn,),
            in_specs=[col, col, const((2 * L, n)), pl.BlockSpec((1, 1, D_HY), lambda c: (order, 0, 0)),
                      const((2 * L, L)), const((L, 2 * L))],
            out_specs=col,
            out_shape=jax.ShapeDtypeStruct((L, cols), F32 if order == 0 else BF16),
            compiler_params=_cp("arbitrary"),
            name="hy_direct",
        )(z, g, spec, skip[:, None, :], t_bf, ti_bf)
    return z


def _kron_tables(t1, t1i, G):
    n2, two_n1, h1 = t1.shape
    ng = n2 // G
    eye = np.eye(two_n1, dtype=np.float32)
    rep_rows = jnp.asarray(np.repeat(eye, G, axis=0))
    rep_cols = jnp.asarray(np.repeat(eye, G, axis=1))
    c_fwd = t1.reshape(ng, G, two_n1, h1).transpose(0, 2, 1, 3).reshape(ng, two_n1, G * h1)
    c_swapped = t1i.reshape(ng, G * h1, two_n1)
    c_natural = t1i.reshape(ng, G, h1, two_n1).transpose(0, 2, 1, 3).reshape(ng, h1 * G, two_n1)

    def expand(compact, rep, left, row_key, lane_key):
        r, c = (rep.shape[0], compact.shape[2]) if left else (compact.shape[1], rep.shape[1])
        return pl.pallas_call(
            functools.partial(_kron_expand_kernel, left=left, row_key=row_key, lane_key=lane_key),
            grid=(ng,),
            in_specs=[pl.BlockSpec((1,) + compact.shape[1:], lambda i: (i, 0, 0)),
                      pl.BlockSpec(rep.shape, lambda i: (0, 0))],
            out_specs=pl.BlockSpec((1, r, c), lambda i: (i, 0, 0)),
            out_shape=jax.ShapeDtypeStruct((ng, r, c), BF16),
            compiler_params=_cp("arbitrary"),
            name="hy_kron",
        )(jnp.asarray(compact), rep)

    fwd = expand(c_fwd, rep_rows, True, lambda r: r % G, lambda c: c // h1)
    inv_swapped = expand(c_swapped, rep_cols, False, lambda r: r // h1, lambda c: c % G)
    inv_natural = expand(c_natural, rep_cols, False, lambda r: r % G, lambda c: c % G)
    return fwd, inv_swapped, inv_natural


def _kron_expand_kernel(c_ref, rep_ref, o_ref, *, left, row_key, lane_key):
    comp = c_ref[0].astype(BF16)
    rep = rep_ref[...].astype(BF16)
    full = (jnp.dot(rep, comp, preferred_element_type=F32) if left
            else jnp.dot(comp, rep, preferred_element_type=F32))
    rows = lax.broadcasted_iota(jnp.int32, full.shape, 0)
    lanes = lax.broadcasted_iota(jnp.int32, full.shape, 1)
    o_ref[0] = jnp.where(row_key(rows) == lane_key(lanes), full, 0.0).astype(BF16)


def _hy_stage1_kernel(x_ref, t_ref, re_ref, im_ref, *, G, n1, ct):
    h1, cols = x_ref.shape[1], x_ref.shape[2]
    x = x_ref[...].reshape(G * h1, cols).astype(BF16)
    for c0 in range(0, cols, ct):
        a = jnp.dot(t_ref[0], x[:, c0:c0 + ct], preferred_element_type=F32)
        re_ref[:, :, c0:c0 + ct] = a[:n1 * G].reshape(n1, G, ct).astype(BF16)
        im_ref[:, :, c0:c0 + ct] = a[n1 * G:].reshape(n1, G, ct).astype(BF16)


def _hy_stage1(x3, t_fwd, *, G):
    n2, h1, cols = x3.shape
    n1 = t_fwd.shape[1] // (2 * G)
    ospec = pl.BlockSpec((n1, G, cols), lambda j: (0, j, 0))
    oshape = jax.ShapeDtypeStruct((n1, n2, cols), BF16)
    return pl.pallas_call(
        functools.partial(_hy_stage1_kernel, G=G, n1=n1, ct=D_HY),
        grid=(n2 // G,),
        in_specs=[pl.BlockSpec((G, h1, cols), lambda j: (j, 0, 0)),
                  pl.BlockSpec((1,) + t_fwd.shape[1:], lambda j: (j, 0, 0))],
        out_specs=[ospec, ospec],
        out_shape=[oshape, oshape],
        compiler_params=_cp("arbitrary"),
        name="hy_stage1",
    )(x3, t_fwd)


def _hy_fspec2_kernel(re_ref, im_ref, t2_ref, fre_ref, fim_ref, *, kb, k2):
    for j in range(kb):
        a2 = jnp.concatenate([re_ref[j], im_ref[j]], axis=0)
        x = jnp.dot(t2_ref[...], a2, preferred_element_type=F32)
        fre_ref[j] = x[:k2]
        fim_ref[j] = x[k2:]


def _hy_stage2_kernel(re_ref, im_ref, fre_ref, fim_ref, skip_ref, t2_ref, t2i_ref, ore_ref, oim_ref,
                      *, kb, k2, n2, nbc, order):
    for j in range(kb):
        a2 = jnp.concatenate([re_ref[j], im_ref[j]], axis=0)
        x = jnp.dot(t2_ref[...], a2, preferred_element_type=F32)
        fr, fi = _filter_spectrum(fre_ref[j], fim_ref[j], order, skip_ref[0])
        y = _spectral_product(x[:k2], x[k2:], fr, fi, nbc).astype(BF16)
        a = jnp.dot(t2i_ref[...], y, preferred_element_type=F32)
        ore_ref[j] = a[:n2].astype(BF16)
        oim_ref[j] = a[n2:].astype(BF16)


def _hy_stage3_kernel(re_ref, im_ref, t_ref, g_ref, o_ref, *, G, n1, ct):
    cols = re_ref.shape[2]
    for c0 in range(0, cols, ct):
        sl = slice(c0, c0 + ct)
        a2 = jnp.concatenate([re_ref[:, :, sl].reshape(n1 * G, ct), im_ref[:, :, sl].reshape(n1 * G, ct)], axis=0)
        conv = jnp.dot(t_ref[0], a2, preferred_element_type=F32)
        g = g_ref[:, :, sl]
        o_ref[:, :, sl] = (g * conv.reshape(g.shape)).astype(o_ref.dtype)


def _two_level_tables(L, *, n1=FFT_N1, G=FFT_G):
    t1, t2, t2i, t1i = _dft_two_level_tables(L, n1)
    return (jnp.asarray(t2).astype(BF16), jnp.asarray(t2i).astype(BF16)) + _kron_tables(t1, t1i, G)


def _hyena_two_level(z3, g1_3, g2, filt, skip, tables, *, n1=FFT_N1, G=FFT_G, kb=4):
    n2, h1, cols = z3.shape
    L = n2 * h1
    k2 = n2 // 2
    nbc = cols // D_HY
    t2_bf, t2i_bf, t_fwd, t_inv_swapped, t_inv_natural = tables
    nf = filt.shape[1]

    f_re, f_im = _hy_stage1(filt.reshape(n2, h1, nf), t_fwd, G=G)
    blk = lambda c: pl.BlockSpec((kb, n2, c), lambda i: (i, 0, 0))
    hblk = lambda c: pl.BlockSpec((kb, k2, c), lambda i: (i, 0, 0))
    const = lambda s: pl.BlockSpec(s, lambda i: (0, 0))
    fshape = jax.ShapeDtypeStruct((n1, k2, nf), F32)
    fre, fim = pl.pallas_call(
        functools.partial(_hy_fspec2_kernel, kb=kb, k2=k2),
        grid=(n1 // kb,),
        in_specs=[blk(nf), blk(nf), const((n2, 2 * n2))],
        out_specs=[hblk(nf), hblk(nf)],
        out_shape=[fshape, fshape],
        compiler_params=_cp("arbitrary"),
        name="hy_fspec2",
    )(f_re, f_im, t2_bf)

    ashape = jax.ShapeDtypeStruct((n1, n2, cols), BF16)
    grp = pl.BlockSpec((n1, G, cols), lambda j: (0, j, 0))
    tspec = lambda t: pl.BlockSpec((1,) + t.shape[1:], lambda j: (j, 0, 0))
    for order in range(2):
        a_re, a_im = _hy_stage1(z3, t_fwd, G=G)
        b_re, b_im = pl.pallas_call(
            functools.partial(_hy_stage2_kernel, kb=kb, k2=k2, n2=n2, nbc=nbc, order=order),
            grid=(n1 // kb,),
            in_specs=[blk(cols), blk(cols), hblk(nf), hblk(nf),
                      pl.BlockSpec((1, 1, D_HY), lambda i: (order, 0, 0)),
                      const((n2, 2 * n2)), const((2 * n2, n2))],
            out_specs=[blk(cols), blk(cols)],
            out_shape=[ashape, ashape],
            compiler_params=_cp("arbitrary"),
            name="hy_stage2",
        )(a_re, a_im, fre, fim, skip[:, None, :], t2_bf, t2i_bf)
        if order == 0:
            table, gate = t_inv_swapped, g1_3
            io = pl.BlockSpec((G, h1, cols), lambda j: (j, 0, 0))
            oshape = jax.ShapeDtypeStruct((n2, h1, cols), F32)
        else:
            table, gate = t_inv_natural, g2.reshape(h1, n2, cols)
            io = pl.BlockSpec((h1, G, cols), lambda j: (0, j, 0))
            oshape = jax.ShapeDtypeStruct((h1, n2, cols), BF16)
        z3 = pl.pallas_call(
            functools.partial(_hy_stage3_kernel, G=G, n1=n1, ct=D_HY),
            grid=(n2 // G,),
            in_specs=[grp, grp, tspec(table), io],
            out_specs=io,
            out_shape=oshape,
            compiler_params=_cp("arbitrary"),
            name="hy_stage3",
        )(b_re, b_im, table, gate)
    return z3.reshape(L, cols)


def _hyena(hv_t, h1_t, h2_t, conv_w, conv_b, filt, skip, tables=None):
    L, cols = hv_t.shape
    if _two_level(L):
        z3, g1_3, g2 = _hy_conv3(hv_t, h1_t, h2_t, conv_w, conv_b, swap=(True, True, False))
        return _hyena_two_level(z3, g1_3, g2, filt, skip, tables or _two_level_tables(L))
    z, g1, g2 = _hy_conv3(hv_t, h1_t, h2_t, conv_w, conv_b)
    return _hyena_direct(z, g1, g2, filt, skip)


def _route(logits):
    lane = lax.broadcasted_iota(jnp.int32, logits.shape, 1)
    is_g = (lane >= N_EXPERTS) & (lane < N_EXPERTS + N_GROUPS)
    big = jnp.int32(1 << 20)
    gl = jnp.where(is_g, logits, NEG)
    gmax = jnp.max(gl, axis=-1, keepdims=True)
    gidx = jnp.min(jnp.where(gl == gmax, lane - N_EXPERTS, big), axis=-1, keepdims=True)
    p_group = 1.0 / jnp.sum(jnp.where(is_g, jnp.exp(gl - gmax), 0.0), axis=-1, keepdims=True)
    sel = (lane < N_EXPERTS) & ((lane // EXPERTS_PER_GROUP) == gidx)
    el = jnp.where(sel, logits, NEG)
    ee = jnp.where(sel, jnp.exp(el - jnp.max(el, axis=-1, keepdims=True)), 0.0)
    pe = ee / jnp.sum(ee, axis=-1, keepdims=True)
    p1 = jnp.max(jnp.where(sel, pe, -1.0), axis=-1, keepdims=True)
    i1 = jnp.min(jnp.where(sel & (pe == p1), lane, big), axis=-1, keepdims=True)
    rest = sel & (lane != i1)
    p2 = jnp.max(jnp.where(rest, pe, -1.0), axis=-1, keepdims=True)
    i2 = jnp.min(jnp.where(rest & (pe == p2), lane, big), axis=-1, keepdims=True)
    den = p1 + p2
    cmb = jnp.where(lane == i1, p_group * (p1 / den), jnp.where(lane == i2, p_group * (p2 / den), 0.0))
    return jnp.where(lane == N_EXPERTS, gidx.astype(F32), cmb)


def _outproj_kernel(att_ref, hf_ref, hb_ref, gt_ref, hy_ref, x_ref, mod_ref, w_ref, g_ref, b_ref,
                    wr_ref, br_ref, x1_ref, u2_ref, cmb_ref):
    rg = (hf_ref[...] + hb_ref[...]) * jax.nn.gelu(gt_ref[...])
    mix = (jnp.dot(att_ref[0], w_ref[0:D_ATT, :], preferred_element_type=F32)
           + jnp.dot(rg.astype(BF16), w_ref[D_ATT:D_ATT + D_RNN, :], preferred_element_type=F32)
           + jnp.dot(hy_ref[...].astype(BF16), w_ref[D_ATT + D_RNN:, :], preferred_element_type=F32))
    m = mod_ref[0]
    x1 = _ln(ALPHA * x_ref[...] + m[2:3] * mix) * g_ref[...] + b_ref[...]
    u2 = _ln(x1) * (1.0 + m[4:5]) + m[3:4]
    u_hi = u2.astype(BF16)
    u_lo = (u2 - u_hi.astype(F32)).astype(BF16)
    wr = wr_ref[...]
    w_hi = wr.astype(BF16)
    w_lo = (wr - w_hi.astype(F32)).astype(BF16)
    hi = jnp.dot(u_hi, jnp.concatenate([w_hi, w_lo], axis=1), preferred_element_type=F32)
    lo = jnp.dot(u_lo, w_hi, preferred_element_type=F32)
    logits = hi[:, 0:LANES] + hi[:, LANES:] + lo + br_ref[...]
    x1_ref[...] = x1
    u2_ref[...] = u_hi
    cmb_ref[...] = _route(logits)


def _outproj(att, hf_t, hb_t, gt_t, hy_t, x2d, mod_l, w_bf, ln_g, ln_b, wr, br, *, nb, L, tm,
             cond_base, cond_per_batch):
    tpb = L // tm
    d = x2d.shape[1]
    cond = (lambda i: cond_base + i // tpb) if cond_per_batch else (lambda i: cond_base)
    t_spec = pl.BlockSpec((tm, D_RNN), lambda i: (i % tpb, i // tpb))
    tok = lambda c: pl.BlockSpec((tm, c), lambda i: (i, 0))
    const = lambda s: pl.BlockSpec(s, lambda i: (0, 0))
    n = nb * L
    return pl.pallas_call(
        _outproj_kernel,
        grid=(nb * tpb,),
        in_specs=[pl.BlockSpec((1, tm, D_ATT), lambda i: (i // tpb, i % tpb, 0)),
                  t_spec, t_spec, t_spec, t_spec, tok(d),
                  pl.BlockSpec((1, 6, d), lambda i: (cond(i), 0, 0)),
                  const((d, d)), const((1, d)), const((1, d)), const((d, LANES)), const((1, LANES))],
        out_specs=[tok(d), tok(d), tok(LANES)],
        out_shape=[jax.ShapeDtypeStruct((n, d), F32), jax.ShapeDtypeStruct((n, d), BF16),
                   jax.ShapeDtypeStruct((n, LANES), F32)],
        compiler_params=_cp("arbitrary"),
        name="outproj",
    )(att, hf_t, hb_t, gt_t, hy_t, x2d, mod_l, w_bf, ln_g[None], ln_b[None], wr, br)


def _moe_kernel(u_ref, cmb_ref, x1_ref, mod_ref, wgu_ref, wd_ref, g_ref, b_ref, o_ref, ys_ref, *, T, BLK):
    SL = T + N_GROUPS * BLK
    u = u_ref[...]
    cmb = cmb_ref[...]
    lane_f = lax.broadcasted_iota(jnp.int32, (T, LANES), 1).astype(F32)
    gid_col = jnp.sum(jnp.where(lane_f == float(N_EXPERTS), cmb, 0.0), axis=-1, keepdims=True)
    gid_row = cmb.T[N_EXPERTS:N_EXPERTS + 1, :]
    sub = lax.broadcasted_iota(jnp.int32, (8, T), 0).astype(F32)
    onehot = jnp.where(lane_f == gid_col, 1.0, 0.0)
    onehot_t = jnp.where(sub == gid_row, 1.0, 0.0)
    ri = lax.broadcasted_iota(jnp.int32, (T, T), 0)
    ci = lax.broadcasted_iota(jnp.int32, (T, T), 1)
    before = jnp.where(ci < ri, 1.0, 0.0).astype(BF16)
    before_t = jnp.where(ri < ci, 1.0, 0.0).astype(BF16)
    rank_col = jnp.sum(onehot * jnp.dot(before, onehot.astype(BF16), preferred_element_type=F32),
                       axis=-1, keepdims=True)
    rank_row = jnp.sum(onehot_t * jnp.dot(onehot_t.astype(BF16), before_t, preferred_element_type=F32),
                       axis=0, keepdims=True)
    counts = jnp.sum(onehot, axis=0, keepdims=True)
    lane_row = lax.broadcasted_iota(jnp.int32, (1, LANES), 1)
    sub_col = lax.broadcasted_iota(jnp.int32, (8, 1), 0)
    base = jnp.int32(0)
    bases = []
    base_row = jnp.zeros((1, LANES), jnp.int32)
    base_sub = jnp.zeros((8, 1), jnp.int32)
    for c in range(N_GROUPS):
        n_c = jnp.sum(jnp.where(lane_row == c, counts, 0.0)).astype(jnp.int32)
        bases.append(base)
        base_row = jnp.where(lane_row == c, base, base_row)
        base_sub = jnp.where(sub_col == c, base, base_sub)
        base = base + ((n_c + (BLK - 1)) // BLK) * BLK
    total = base
    dest_col = (jnp.sum(onehot * base_row.astype(F32), axis=-1, keepdims=True) + rank_col).astype(jnp.int32)
    dest_row = (jnp.sum(onehot_t * base_sub.astype(F32), axis=0, keepdims=True) + rank_row).astype(jnp.int32)
    cmb_hi = cmb.astype(BF16)
    cmb_lo = (cmb - cmb_hi.astype(F32)).astype(BF16)
    blk_rows = lax.broadcasted_iota(jnp.int32, (BLK, T), 0)
    blk_lane = lax.broadcasted_iota(jnp.int32, (BLK, LANES), 1)

    def block(b, carry):
        r0 = b * BLK
        rows = pl.ds(pl.multiple_of(r0, BLK), BLK)

        @pl.when(r0 < total)
        def _():
            grp = ((r0 >= bases[1]).astype(jnp.int32) + (r0 >= bases[2]).astype(jnp.int32)
                   + (r0 >= bases[3]).astype(jnp.int32))
            p = jnp.where(blk_rows + r0 == dest_row, 1.0, 0.0).astype(BF16)
            xb = jnp.dot(p, u, preferred_element_type=F32).astype(BF16)
            wb = (jnp.dot(p, cmb_hi, preferred_element_type=F32)
                  + jnp.dot(p, cmb_lo, preferred_element_type=F32))
            acc = jnp.zeros((BLK, u.shape[1]), F32)
            for j in range(EXPERTS_PER_GROUP):
                e = grp * EXPERTS_PER_GROUP + j
                gu = jnp.dot(xb, wgu_ref[e], preferred_element_type=F32)
                gate = gu[:, :D_EXPERT]
                w = jnp.sum(jnp.where(blk_lane == e, wb, 0.0), axis=-1, keepdims=True)
                h = gate * jax.nn.sigmoid(gate) * gu[:, D_EXPERT:] * w
                acc = acc + jnp.dot(h.astype(BF16), wd_ref[e], preferred_element_type=F32)
            ys_ref[rows, :] = acc.astype(BF16)

        @pl.when(r0 >= total)
        def _():
            ys_ref[rows, :] = jnp.zeros((BLK, u.shape[1]), BF16)

        return carry

    lax.fori_loop(0, SL // BLK, block, 0)
    p_t = jnp.where(lax.broadcasted_iota(jnp.int32, (T, SL), 1) == dest_col, 1.0, 0.0).astype(BF16)
    moe = jnp.dot(p_t, ys_ref[...], preferred_element_type=F32)
    m = mod_ref[0]
    o_ref[...] = _ln(ALPHA * x1_ref[...] + m[5:6] * moe) * g_ref[...] + b_ref[...]


def _moe(u2, cmb, x1, mod_l, wgu, wd, ln_g, ln_b, *, tm, tiles_per_cond, cond_base, cond_per_batch):
    n, d = x1.shape
    ne = wgu.shape[0]
    cond = (lambda i: cond_base + i // tiles_per_cond) if cond_per_batch else (lambda i: cond_base)
    tok = lambda c: pl.BlockSpec((tm, c), lambda i: (i, 0))
    const = lambda s: pl.BlockSpec(s, lambda i: (0,) * len(s))
    resident = lambda s: pl.BlockSpec(s, lambda i: (0,) * len(s), pipeline_mode=pl.Buffered(1))
    return pl.pallas_call(
        functools.partial(_moe_kernel, T=tm, BLK=MOE_BLK),
        grid=(n // tm,),
        in_specs=[tok(d), tok(LANES), tok(d),
                  pl.BlockSpec((1, 6, d), lambda i: (cond(i), 0, 0)),
                  resident((ne, d, 2 * D_EXPERT)), resident((ne, D_EXPERT, d)),
                  const((1, d)), const((1, d))],
        out_specs=tok(d),
        out_shape=jax.ShapeDtypeStruct((n, d), F32),
        scratch_shapes=[pltpu.VMEM((tm + N_GROUPS * MOE_BLK, d), BF16)],
        compiler_params=_cp("arbitrary"),
        name="moe",
    )(u2, cmb, x1, mod_l, wgu, wd, ln_g[None], ln_b[None])


def _stream_layer(x2d, mod_l, p, *, nb, L, tm, cond_base, cond_per_batch, ctx, cache=None, h0=None):
    kw = dict(nb=nb, L=L, tm=tm, cond_base=cond_base, cond_per_batch=cond_per_batch)
    q, k, v, xr_t, gt_t, hv_t, h1_t, h2_t = _inproj(x2d, mod_l, p["w_in"], kv_dtype=F32 if ctx else BF16, **kw)
    if ctx:
        att = _ctx_attention(q, k, v)
    else:
        att = _nbr_attention(q, k, v, cache[0], cache[1], p["nbr_bias"])
    a_f, b_f, a_b, b_b = _rg_gates(xr_t, p["rg_conv_w"], p["rg_conv_b"], *p["rg_w"])
    hf_t, hb_t = _rg_scan(a_f, b_f, a_b, b_b, h0[0], h0[1])
    hy_t = _hyena(hv_t, h1_t, h2_t, p["hy_conv_w"], p["hy_conv_b"], p["hy_filt"][L], p["hy_skip"],
                  p["hy_tables"].get(L))
    x1, u2, cmb = _outproj(att, hf_t, hb_t, gt_t, hy_t, x2d, mod_l, p["w_out"], p["ln1_g"], p["ln1_b"],
                           p["router_w"], p["router_b"], **kw)
    tm_moe = min(MOE_TILE, nb * L)
    x2 = _moe(u2, cmb, x1, mod_l, p["moe_wgu"], p["moe_wd"], p["ln2_g"], p["ln2_b"], tm=tm_moe,
              tiles_per_cond=max(L // tm_moe, 1), cond_base=cond_base, cond_per_batch=cond_per_batch)
    return x2, k, v, hf_t, hb_t


def kernel(x_prompt, x_sample, cache_k, cache_v, state_rglru, c, c_ctx, w_mod, b_mod, w_in, attn_rpb, rg_conv_w, rg_conv_b, rg_wa, rg_ba, rg_wx, rg_bx, rg_lambda, hy_conv_w, hy_conv_b, hy_w1, hy_b1, hy_w2, hy_b2, hy_w3, hy_freq, hy_decay, hy_skip, w_out, ln1_g, ln1_b, router_wg, router_bg, router_we, router_be, moe_w_gate, moe_w_up, moe_w_down, ln2_g, ln2_b):
    nbc, Lc, d = x_prompt.shape
    nbl, Ll, _ = x_sample.shape
    depth = w_mod.shape[0]
    past = cache_k.shape[2]

    cond8 = jnp.zeros((8, d), F32).at[0].set(c_ctx).at[1:1 + nbl].set(c)
    mod = _modulation(cond8, w_mod, b_mod).reshape(depth, 8, 6, d)

    xp = x_prompt.reshape(nbc * Lc, d)
    xs = x_sample.reshape(nbl * Ll, d)
    new_k, new_v, new_h = [], [], []
    hy_tables = {L: _two_level_tables(L) for L in {Lc, Ll} if _two_level(L)}
    for l in range(depth):
        pad = jnp.zeros((d, LANES - N_EXPERTS - N_GROUPS), F32)
        p = {
            "w_in": w_in[l].astype(BF16),
            "nbr_bias": _nbr_bias_tables(attn_rpb[l], Ll // GRID_W),
            "rg_conv_w": rg_conv_w[l], "rg_conv_b": rg_conv_b[l],
            "rg_w": _rg_weights(rg_wa[l], rg_ba[l], rg_wx[l], rg_bx[l], rg_lambda[l]),
            "hy_conv_w": hy_conv_w[l], "hy_conv_b": hy_conv_b[l], "hy_skip": hy_skip[l], "hy_tables": hy_tables,
            "hy_filt": {L: _hy_filters(L, hy_w1[l], hy_b1[l], hy_w2[l], hy_b2[l], hy_w3[l], hy_freq[l], hy_decay[l])
                        for L in sorted({Lc, Ll})},
            "w_out": w_out[l].astype(BF16), "ln1_g": ln1_g[l], "ln1_b": ln1_b[l],
            "router_w": jnp.concatenate([router_we[l], router_wg[l], pad], axis=1),
            "router_b": jnp.concatenate([router_be[l], router_bg[l], pad[0]])[None],
            "moe_wgu": jnp.concatenate([moe_w_gate[l], moe_w_up[l]], axis=-1).astype(BF16),
            "moe_wd": moe_w_down[l].astype(BF16),
            "ln2_g": ln2_g[l], "ln2_b": ln2_b[l],
        }
        zeros_h = jnp.zeros((1, nbc * D_RNN), F32)
        xp, k, v, hf_t, hb_t = _stream_layer(
            xp, mod[l], p, nb=nbc, L=Lc, tm=Lc, cond_base=0, cond_per_batch=False, ctx=True,
            h0=(zeros_h, zeros_h))
        new_k.append(k.reshape(nbc, Lc, N_HEADS, HEAD_DIM))
        new_v.append(v.reshape(nbc, Lc, N_HEADS, HEAD_DIM))
        new_h.append(jnp.stack([hf_t[Lc - 1].reshape(nbc, D_RNN), hb_t[0].reshape(nbc, D_RNN)], axis=1))

        cache = (cache_k[:, l].reshape(nbl, past, D_ATT), cache_v[:, l].reshape(nbl, past, D_ATT))
        h0 = (state_rglru[:, l, 0].reshape(1, nbl * D_RNN), state_rglru[:, l, 1].reshape(1, nbl * D_RNN))
        xs, _, _, _, _ = _stream_layer(
            xs, mod[l], p, nb=nbl, L=Ll, tm=min(Ll, 512), cond_base=1, cond_per_batch=True, ctx=False,
            cache=cache, h0=h0)

    return (xp.reshape(nbc, Lc, d), xs.reshape(nbl, Ll, d),
            jnp.stack(new_k, axis=1), jnp.stack(new_v, axis=1), jnp.stack(new_h, axis=1))
```

```python
import functools
import math

import numpy as np
import jax
import jax.numpy as jnp
from jax import lax
from jax.experimental import pallas as pl
from jax.experimental.pallas import tpu as pltpu

F32 = jnp.float32
BF16 = jnp.bfloat16
HIGHEST = lax.Precision.HIGHEST

DEPTH = 2
D_MODEL = 1024
N_HEADS = 8
HEAD_DIM = 64
D_ATT = N_HEADS * HEAD_DIM
D_RNN = 256
D_HY = 256
GRID_W = 64
WIN_R = 8
WIN_C = 16
RG_CONV = 4
RG_C = 8.0
HY_CONV = 3
POS_EMB_DIM = 33
N_GROUPS = 4
EXPERTS_PER_GROUP = 4
N_EXPERTS = 16
D_EXPERT = 256
ALPHA = (2.0 * DEPTH) ** 0.25
LN_EPS = 1e-5
NEG = -1e30

Q_ROWS = 4
K_ROWS = Q_ROWS + WIN_R
LANES = 128
HALO = 8
MOE_TILE = 512
MOE_BLK = 128
STRIDE_PAD = 4
FFT_G = 16
FFT_N1 = 64
VMEM_LIMIT = 52 * 1024 * 1024


def _cp(*sem):
    return pltpu.CompilerParams(dimension_semantics=sem, vmem_limit_bytes=VMEM_LIMIT)


def _sigmoid(x):
    return 0.5 * jnp.tanh(0.5 * x) + 0.5


def _ln(x):
    mu = jnp.mean(x, axis=-1, keepdims=True)
    xc = x - mu
    var = jnp.mean(xc * xc, axis=-1, keepdims=True)
    return xc * lax.rsqrt(var + LN_EPS)


def _mod_kernel(c_ref, w_ref, b_ref, o_ref):
    c = c_ref[...]
    s = c * jax.nn.sigmoid(c)
    o_ref[0] = jnp.dot(s, w_ref[0], precision=HIGHEST, preferred_element_type=F32) + b_ref[0]


def _modulation(cond8, w_mod, b_mod):
    depth, d, n = w_mod.shape
    tn = n // 4
    return pl.pallas_call(
        _mod_kernel,
        grid=(depth, n // tn),
        in_specs=[pl.BlockSpec((8, d), lambda l, j: (0, 0)),
                  pl.BlockSpec((1, d, tn), lambda l, j: (l, 0, j)),
                  pl.BlockSpec((1, 1, tn), lambda l, j: (l, 0, j))],
        out_specs=pl.BlockSpec((1, 8, tn), lambda l, j: (l, 0, j)),
        out_shape=jax.ShapeDtypeStruct((depth, 8, n), F32),
        compiler_params=_cp("arbitrary", "arbitrary"),
        name="modulation",
    )(cond8, w_mod, b_mod.reshape(depth, 1, n))


def _inproj_kernel(x_ref, mod_ref, w_ref, q_ref, k_ref, v_ref, xr_ref, gt_ref, hv_ref, h1_ref, h2_ref):
    m = mod_ref[0]
    u = (_ln(x_ref[...]) * (1.0 + m[1:2]) + m[0:1]).astype(BF16)

    def proj(lo, hi):
        return jnp.dot(u, w_ref[:, lo:hi], preferred_element_type=F32)

    o = 0
    q_ref[0] = (proj(o, o + D_ATT) * (HEAD_DIM ** -0.5)).astype(q_ref.dtype)
    o += D_ATT
    k_ref[0] = proj(o, o + D_ATT).astype(k_ref.dtype)
    o += D_ATT
    v_ref[0] = proj(o, o + D_ATT).astype(v_ref.dtype)
    o += D_ATT
    for ref in (xr_ref, gt_ref, hv_ref, h1_ref, h2_ref):
        ref[...] = proj(o, o + D_RNN)
        o += D_RNN


def _inproj(x2d, mod_l, w_bf, *, nb, L, tm, cond_base, cond_per_batch, kv_dtype):
    tpb = L // tm
    d = x2d.shape[1]
    d_in = w_bf.shape[1]
    cond = (lambda i: cond_base + i // tpb) if cond_per_batch else (lambda i: cond_base)
    att_spec = pl.BlockSpec((1, tm, D_ATT), lambda i: (i // tpb, i % tpb, 0))
    t_spec = pl.BlockSpec((tm, D_RNN), lambda i: (i % tpb, i // tpb))
    att_shape = lambda dt: jax.ShapeDtypeStruct((nb, L, D_ATT), dt)
    t_shape = jax.ShapeDtypeStruct((L, nb * D_RNN), F32)
    return pl.pallas_call(
        _inproj_kernel,
        grid=(nb * tpb,),
        in_specs=[pl.BlockSpec((tm, d), lambda i: (i, 0)),
                  pl.BlockSpec((1, 6, d), lambda i: (cond(i), 0, 0)),
                  pl.BlockSpec((d, d_in), lambda i: (0, 0))],
        out_specs=[att_spec, att_spec, att_spec, t_spec, t_spec, t_spec, t_spec, t_spec],
        out_shape=[att_shape(BF16), att_shape(kv_dtype), att_shape(kv_dtype),
                   t_shape, t_shape, t_shape, t_shape, t_shape],
        compiler_params=_cp("arbitrary"),
        name="inproj",
    )(x2d, mod_l, w_bf)


def _nt(a, b):
    return lax.dot_general(a, b, (((1,), (1,)), ((), ())), preferred_element_type=F32)


def _ctx_attn_kernel(q_ref, k_ref, v_ref, o_ref):
    q = q_ref[0]
    k = k_ref[0].astype(BF16)
    v = v_ref[0].astype(BF16)
    outs = []
    for h in range(N_HEADS):
        sl = slice(h * HEAD_DIM, (h + 1) * HEAD_DIM)
        s = _nt(q[:, sl], k[:, sl])
        e = jnp.exp(s - jnp.max(s, axis=-1, keepdims=True))
        den = jnp.sum(e, axis=-1, keepdims=True)
        outs.append(jnp.dot(e.astype(BF16), v[:, sl], preferred_element_type=F32) / den)
    o_ref[0] = jnp.concatenate(outs, axis=-1).astype(o_ref.dtype)


def _ctx_attention(q, k, v):
    nb, L, _ = q.shape
    spec = pl.BlockSpec((1, L, D_ATT), lambda b: (b, 0, 0))
    return pl.pallas_call(
        _ctx_attn_kernel,
        grid=(nb,),
        in_specs=[spec, spec, spec],
        out_specs=spec,
        out_shape=jax.ShapeDtypeStruct((nb, L, D_ATT), BF16),
        compiler_params=_cp("arbitrary"),
        name="ctx_attn",
    )(q, k, v)


N_DR = 2 * WIN_R - 1
PAIR_LEFT = N_DR - 1
PAIR_RIGHT = 2 * N_DR - 1
PAIR_NONE = 3 * N_DR - 1


def _nbr_bias_tiles(rpb):
    cols = np.arange(GRID_W)
    c_start = np.clip(cols - WIN_C // 2, 0, GRID_W - WIN_C)
    col_mask = (cols[None, :] >= c_start[:, None]) & (cols[None, :] < c_start[:, None] + WIN_C)
    dc_idx = np.clip(cols[None, :] - cols[:, None], -(WIN_C - 1), WIN_C - 1) + (WIN_C - 1)
    sel = (dc_idx[None] == np.arange(2 * WIN_C - 1)[:, None, None]).astype(np.float32)
    colb = jnp.einsum('hab,bqk->haqk', rpb.astype(F32), jnp.asarray(sel), precision=HIGHEST)
    colb = jnp.where(col_mask[None, None], colb, NEG)
    neg = jnp.full_like(colb, NEG)
    both = jnp.concatenate([colb[:, :-1], colb[:, 1:]], axis=-1)
    left_masked = jnp.concatenate([neg, colb], axis=-1)
    right_masked = jnp.concatenate([colb, neg], axis=-1)
    none = jnp.concatenate([neg[:, :1], neg[:, :1]], axis=-1)
    return jnp.concatenate([both, left_masked, right_masked, none], axis=1)


def _nbr_attn_kernel(q_ref, k_ref, v_ref, kc_ref, vc_ref, bias_ref, o_ref, *, rows):
    g = pl.program_id(1)
    r0 = jnp.clip(Q_ROWS * g - WIN_R // 2, 0, rows - K_ROWS)
    start = pl.multiple_of(r0 * GRID_W, Q_ROWS * GRID_W)
    nk = K_ROWS * GRID_W
    kw = k_ref[0, pl.ds(start, nk), :]
    vw = v_ref[0, pl.ds(start, nk), :]
    kc = kc_ref[0].astype(BF16)
    vc = vc_ref[0].astype(BF16)
    q = q_ref[0]
    tile = []
    for j in range(Q_ROWS):
        r = Q_ROWS * g + j
        lo = jnp.clip(r - WIN_R // 2, 0, rows - WIN_R) - r + (WIN_R - 1)
        tile.append([])
        for p in range(K_ROWS // 2):
            d0 = r0 + 2 * p - r + (WIN_R - 1)
            lv = (d0 >= lo) & (d0 < lo + WIN_R)
            rv = (d0 + 1 >= lo) & (d0 + 1 < lo + WIN_R)
            idx = jnp.where(lv & rv, d0, jnp.where(rv, PAIR_LEFT + d0 + 1, jnp.where(lv, PAIR_RIGHT + d0, PAIR_NONE)))
            tile[j].append(jnp.clip(idx, 0, PAIR_NONE))
    outs = []
    for h in range(N_HEADS):
        sl = slice(h * HEAD_DIM, (h + 1) * HEAD_DIM)
        qh = q[:, sl]
        bias = jnp.concatenate(
            [jnp.concatenate([bias_ref[h, tile[j][p]] for p in range(K_ROWS // 2)], axis=-1)
             for j in range(Q_ROWS)], axis=0)
        s1 = _nt(qh, kw[:, sl]) + bias
        s2 = _nt(qh, kc[:, sl])
        m = jnp.maximum(jnp.max(s1, axis=-1, keepdims=True), jnp.max(s2, axis=-1, keepdims=True))
        e1 = jnp.exp(s1 - m)
        e2 = jnp.exp(s2 - m)
        den = jnp.sum(e1, axis=-1, keepdims=True) + jnp.sum(e2, axis=-1, keepdims=True)
        o = (jnp.dot(e1.astype(BF16), vw[:, sl], preferred_element_type=F32)
             + jnp.dot(e2.astype(BF16), vc[:, sl], preferred_element_type=F32))
        outs.append(o / den)
    o_ref[0] = jnp.concatenate(outs, axis=-1).astype(o_ref.dtype)


def _nbr_attention(q, k, v, kc, vc, bias):
    nb, L, _ = q.shape
    rows = L // GRID_W
    ng = rows // Q_ROWS
    past = kc.shape[1]
    tq = Q_ROWS * GRID_W
    full = pl.BlockSpec((1, L, D_ATT), lambda b, g: (b, 0, 0))
    ctx = pl.BlockSpec((1, past, D_ATT), lambda b, g: (b, 0, 0))
    qspec = pl.BlockSpec((1, tq, D_ATT), lambda b, g: (b, g, 0))
    bspec = pl.BlockSpec(bias.shape, lambda b, g: (0, 0, 0, 0), pipeline_mode=pl.Buffered(1))
    return pl.pallas_call(
        functools.partial(_nbr_attn_kernel, rows=rows),
        grid=(nb, ng),
        in_specs=[qspec, full, full, ctx, ctx, bspec],
        out_specs=qspec,
        out_shape=jax.ShapeDtypeStruct((nb, L, D_ATT), BF16),
        compiler_params=_cp("arbitrary", "arbitrary"),
        name="nbr_attn",
    )(q, k, v, kc, vc, bias)


def _fill_padded(xp_ref, src_ref, L):
    z = jnp.zeros((HALO, xp_ref.shape[1]), F32)
    xp_ref[0:HALO, :] = z
    xp_ref[HALO:HALO + L, :] = src_ref[...]
    xp_ref[HALO + L:2 * HALO + L, :] = z


def _rg_gates_kernel(x_ref, cw_ref, cb_ref, w_ref, b_ref, lam_ref,
                     af_ref, bf_ref, ab_ref, bb_ref, xp_ref, *, L, tc):
    _fill_padded(xp_ref, x_ref, L)
    sp = jax.nn.softplus(-lam_ref[0])
    for t0 in range(0, L, tc):
        xc = cb_ref[...]
        for k in range(RG_CONV):
            xc = xc + cw_ref[k:k + 1, :] * xp_ref[pl.ds(HALO + t0 + k - RG_CONV // 2, tc), :]
        z = jnp.dot(xc.astype(BF16), w_ref[0], preferred_element_type=F32) + b_ref[0]
        for e, (a_ref, o_ref) in enumerate(((af_ref, bf_ref), (ab_ref, bb_ref))):
            r = _sigmoid(z[:, e * LANES:(e + 1) * LANES])
            i = _sigmoid(z[:, (2 + e) * LANES:(3 + e) * LANES])
            log_a = (-RG_C) * r * sp[:, e * LANES:(e + 1) * LANES]
            a = jnp.exp(log_a)
            one_minus_a2 = -jnp.tanh(log_a) * (a * a + 1.0)
            a_ref[pl.ds(t0, tc), :] = a
            o_ref[pl.ds(t0, tc), :] = jnp.sqrt(one_minus_a2) * i * xc


def _blockdiag2(w2):
    z = jnp.zeros_like(w2[0])
    return jnp.concatenate([jnp.concatenate([w2[0], z], 1), jnp.concatenate([z, w2[1]], 1)], 0)


def _rg_weights(wa, ba, wx, bx, lam):
    ws, bs, ls = [], [], []
    for hf in range(D_RNN // LANES):
        blk = slice(2 * hf, 2 * hf + 2)
        ch = slice(hf * LANES, (hf + 1) * LANES)
        ws.append(jnp.concatenate([_blockdiag2(wa[0, blk]), _blockdiag2(wa[1, blk]),
                                   _blockdiag2(wx[0, blk]), _blockdiag2(wx[1, blk])], axis=1))
        bs.append(jnp.concatenate([ba[0, ch], ba[1, ch], bx[0, ch], bx[1, ch]])[None])
        ls.append(jnp.concatenate([lam[0, ch], lam[1, ch]])[None])
    return jnp.stack(ws).astype(BF16), jnp.stack(bs), jnp.stack(ls)


def _rg_gates(xr_t, conv_w, conv_b, w_bd, b_cat, lam_cat):
    L, cols = xr_t.shape
    nh = D_RNN // LANES
    tc = min(L, 512)
    col = pl.BlockSpec((L, LANES), lambda c: (0, c))
    shape = jax.ShapeDtypeStruct((L, cols), F32)
    return pl.pallas_call(
        functools.partial(_rg_gates_kernel, L=L, tc=tc),
        grid=(cols // LANES,),
        in_specs=[col,
                  pl.BlockSpec((RG_CONV, LANES), lambda c: (0, c % nh)),
                  pl.BlockSpec((1, LANES), lambda c: (0, c % nh)),
                  pl.BlockSpec((1, LANES, 4 * LANES), lambda c: (c % nh, 0, 0)),
                  pl.BlockSpec((1, 1, 4 * LANES), lambda c: (c % nh, 0, 0)),
                  pl.BlockSpec((1, 1, 2 * LANES), lambda c: (c % nh, 0, 0))],
        out_specs=[col, col, col, col],
        out_shape=[shape, shape, shape, shape],
        scratch_shapes=[pltpu.VMEM((L + 2 * HALO, LANES), F32)],
        compiler_params=_cp("arbitrary"),
        name="rg_gates",
    )(xr_t, conv_w, conv_b.reshape(1, D_RNN), w_bd, b_cat, lam_cat)


def _rg_scan_kernel(af_ref, bf_ref, ab_ref, bb_ref, h0f_ref, h0b_ref, hf_ref, hb_ref, cf_ref, cb_ref, *, tc):
    @pl.when(pl.program_id(1) == 0)
    def _():
        cf_ref[...] = h0f_ref[...]
        cb_ref[...] = h0b_ref[...]

    def body(t, carry):
        hf, hb = carry
        hf = af_ref[pl.ds(t, 1), :] * hf + bf_ref[pl.ds(t, 1), :]
        hf_ref[pl.ds(t, 1), :] = hf
        tb = tc - 1 - t
        hb = ab_ref[pl.ds(tb, 1), :] * hb + bb_ref[pl.ds(tb, 1), :]
        hb_ref[pl.ds(tb, 1), :] = hb
        return hf, hb

    hf, hb = lax.fori_loop(0, tc, body, (cf_ref[...], cb_ref[...]), unroll=8)
    cf_ref[...] = hf
    cb_ref[...] = hb


def _rg_scan(a_f, b_f, a_b, b_b, h0f, h0b):
    L, cols = a_f.shape
    cw = min(cols, 1024)
    tc = min(L, 512)
    nt = L // tc
    fwd = pl.BlockSpec((tc, cw), lambda c, i: (i, c))
    bwd = pl.BlockSpec((tc, cw), lambda c, i: (nt - 1 - i, c))
    row = pl.BlockSpec((1, cw), lambda c, i: (0, c))
    shape = jax.ShapeDtypeStruct((L, cols), F32)
    return pl.pallas_call(
        functools.partial(_rg_scan_kernel, tc=tc),
        grid=(cols // cw, nt),
        in_specs=[fwd, fwd, bwd, bwd, row, row],
        out_specs=[fwd, bwd],
        out_shape=[shape, shape],
        scratch_shapes=[pltpu.VMEM((1, cw), F32), pltpu.VMEM((1, cw), F32)],
        compiler_params=_cp("arbitrary", "arbitrary"),
        name="rg_scan",
    )(a_f, b_f, a_b, b_b, h0f, h0b)


def _hy_filter_kernel(feat_ref, w1_ref, b1_ref, w2_ref, b2_ref, w3_ref, fr_ref, dec_ref, o_ref, *, tm):
    feats = feat_ref[...]
    dot = functools.partial(jnp.dot, precision=HIGHEST, preferred_element_type=F32)
    h = jnp.sin(fr_ref[0:1, :] * (dot(feats, w1_ref[...]) + b1_ref[...]))
    h = jnp.sin(fr_ref[1:2, :] * (dot(h, w2_ref[...]) + b2_ref[...]))
    h = dot(h, w3_ref[...])
    filt = h * jnp.exp(-feats[:, 0:1] * jnp.abs(dec_ref[...]))
    row = pl.program_id(0) * tm + lax.broadcasted_iota(jnp.int32, filt.shape, 0)
    lane = lax.broadcasted_iota(jnp.int32, filt.shape, 1)
    o_ref[...] = jnp.where((row == 0) & (lane >= 2 * D_HY), 0.0, filt)


def _hy_features(L):
    pos = np.arange(L, dtype=np.float64)
    t = np.linspace(0.0, 1.0, L)
    bands = (POS_EMB_DIM - 1) // 2
    f = np.linspace(1e-4, bands - 1, bands)
    ang = (2.0 * math.pi / L) * pos[:, None] * f[None, :]
    feats = np.concatenate([t[:, None], np.cos(ang), -np.sin(ang)], axis=-1)
    out = np.zeros((L, LANES), np.float32)
    out[:, :POS_EMB_DIM] = feats
    return out


def _two_level(L):
    return L % (FFT_N1 * 8) == 0 and L >= 2048


def _digit_swap(x, L):
    n2 = 2 * L // FFT_N1
    return x.reshape(L // n2, n2, -1).transpose(1, 0, 2).reshape(L, -1)


def _hy_filters(L, w1, b1, w2, b2, w3, freq, decay):
    tm = min(L, 512)
    hid = w2.shape[0]
    n = w3.shape[1]
    w1p = jnp.zeros((LANES, hid), F32).at[:POS_EMB_DIM].set(w1)
    feats = _hy_features(L)
    if _two_level(L):
        feats = _digit_swap(feats, L)
    c = lambda s: pl.BlockSpec(s, lambda i: (0, 0))
    return pl.pallas_call(
        functools.partial(_hy_filter_kernel, tm=tm),
        grid=(L // tm,),
        in_specs=[pl.BlockSpec((tm, LANES), lambda i: (i, 0)),
                  c((LANES, hid)), c((1, hid)), c((hid, hid)), c((1, hid)), c((hid, n)), c((2, hid)), c((1, n))],
        out_specs=pl.BlockSpec((tm, n), lambda i: (i, 0)),
        out_shape=jax.ShapeDtypeStruct((L, n), F32),
        compiler_params=_cp("arbitrary"),
        name="hy_filter",
    )(jnp.asarray(feats), w1p, b1[None], w2, b2[None], w3, freq, decay.reshape(1, n))


def _conv3_kernel(v_ref, a_ref, b_ref, w_ref, cb_ref, ov_ref, oa_ref, ob_ref, xp_ref, xs_ref, *, L, tc, swap, n2):
    h1 = L // n2
    pitch = n2 + STRIDE_PAD
    for p, (src, dst) in enumerate(((v_ref, ov_ref), (a_ref, oa_ref), (b_ref, ob_ref))):
        _fill_padded(xp_ref, src, L)
        if swap[p]:
            for i1 in range(h1):
                xs_ref[pl.ds(i1 * pitch, n2 + HY_CONV - 1), :] = (
                    xp_ref[pl.ds(HALO + i1 * n2 - HY_CONV // 2, n2 + HY_CONV - 1), :])

            def body(i2, carry, p=p, dst=dst):
                y = cb_ref[p]
                for k in range(HY_CONV):
                    y = y + w_ref[p, k:k + 1, :] * xs_ref[pl.ds(i2 + k, h1, stride=pitch), :]
                dst[i2] = y
                return carry

            lax.fori_loop(0, n2, body, 0)
        else:
            for t0 in range(0, L, tc):
                y = cb_ref[p]
                for k in range(HY_CONV):
                    y = y + w_ref[p, k:k + 1, :] * xp_ref[pl.ds(HALO + t0 + k - HY_CONV // 2, tc), :]
                dst[pl.ds(t0, tc), :] = y


def _hy_conv3(v_t, a_t, b_t, conv_w, conv_b, swap=(False, False, False)):
    L, cols = v_t.shape
    nh = D_HY // LANES
    tc = min(L, 512)
    n2 = 2 * L // FFT_N1
    col = pl.BlockSpec((L, LANES), lambda c: (0, c))
    col3 = pl.BlockSpec((n2, L // n2, LANES), lambda c: (0, 0, c))
    shape = jax.ShapeDtypeStruct((L, cols), F32)
    shape3 = jax.ShapeDtypeStruct((n2, L // n2, cols), F32)
    w = conv_w.reshape(HY_CONV, 3, D_HY).transpose(1, 0, 2)
    cb = conv_b.reshape(3, 1, D_HY)
    return pl.pallas_call(
        functools.partial(_conv3_kernel, L=L, tc=tc, swap=swap, n2=n2),
        grid=(cols // LANES,),
        in_specs=[col, col, col,
                  pl.BlockSpec((3, HY_CONV, LANES), lambda c: (0, 0, c % nh)),
                  pl.BlockSpec((3, 1, LANES), lambda c: (0, 0, c % nh))],
        out_specs=[col3 if s else col for s in swap],
        out_shape=[shape3 if s else shape for s in swap],
        scratch_shapes=[pltpu.VMEM((L + 2 * HALO, LANES), F32),
                        pltpu.VMEM(((L // n2) * (n2 + STRIDE_PAD) + HALO, LANES), F32)],
        compiler_params=_cp("arbitrary"),
        name="hy_conv3",
    )(v_t, a_t, b_t, w, cb)


def _phase(p, n2):
    ang = 2.0 * np.pi * (np.asarray(p, np.int64) % n2) / n2
    return np.cos(ang), np.sin(ang)


def _dft_direct_tables(L):
    k = np.arange(L)[:, None]
    t = np.arange(L)[None, :]
    c, s = _phase((2 * k + 1) * t, 4 * L)
    fwd = np.concatenate([c, -s], axis=0)
    inv = (1.0 / L) * np.concatenate([c.T, -s.T], axis=1)
    return fwd.astype(np.float32), inv.astype(np.float32)


def _dft_two_level_tables(L, n1):
    N = 2 * L
    n2 = N // n1
    h1 = n1 // 2
    i2 = np.arange(n2)[:, None, None]
    k1 = np.arange(n1)[None, :, None]
    i1 = np.arange(h1)[None, None, :]
    c, s = _phase(2 * n2 * i1 * k1 + 2 * i2 * k1 + n2 * i1 + i2, 2 * N)
    t1 = np.concatenate([c, -s], axis=1)
    ct, st = c.transpose(0, 2, 1), s.transpose(0, 2, 1)
    t1i = (2.0 / N) * np.concatenate([ct, -st], axis=2)
    kk = np.arange(n2 // 2)[:, None]
    nn = np.arange(n2)[None, :]
    c2, s2 = _phase(nn * kk, n2)
    t2 = np.block([[c2, s2], [-s2, c2]])
    t2i = np.block([[c2.T, -s2.T], [s2.T, c2.T]])
    return tuple(x.astype(np.float32) for x in (t1, t2, t2i, t1i))


def _filter_spectrum(fre, fim, order, skip):
    fw = slice(order * D_HY, (order + 1) * D_HY)
    bw = slice((2 + order) * D_HY, (3 + order) * D_HY)
    return fre[:, fw] + fre[:, bw] + skip, fim[:, fw] - fim[:, bw]


def _spectral_product(xr, xi, fr, fi, nbc):
    yr, yi = [], []
    for b in range(nbc):
        sl = slice(b * D_HY, (b + 1) * D_HY)
        yr.append(xr[:, sl] * fr - xi[:, sl] * fi)
        yi.append(xr[:, sl] * fi + xi[:, sl] * fr)
    return jnp.concatenate([jnp.concatenate(yr, axis=1), jnp.concatenate(yi, axis=1)], axis=0)


def _matmul_kernel(a_ref, b_ref, o_ref):
    o_ref[...] = jnp.dot(a_ref[...], b_ref[...], precision=HIGHEST, preferred_element_type=F32)


def _hy_direct_kernel(z_ref, g_ref, f_ref, skip_ref, t_ref, ti_ref, o_ref, *, L, nbc, order):
    x = jnp.dot(t_ref[...], z_ref[...].astype(BF16), preferred_element_type=F32)
    fr, fi = _filter_spectrum(f_ref[0:L, :], f_ref[L:2 * L, :], order, skip_ref[0])
    y = _spectral_product(x[:L], x[L:], fr, fi, nbc).astype(BF16)
    conv = jnp.dot(ti_ref[...], y, preferred_element_type=F32)
    o_ref[...] = (g_ref[...] * conv).astype(o_ref.dtype)


def _hyena_direct(z, g1, g2, filt, skip):
    L, cols = z.shape
    fwd, inv = _dft_direct_tables(L)
    n = filt.shape[1]
    spec = pl.pallas_call(
        _matmul_kernel,
        out_shape=jax.ShapeDtypeStruct((2 * L, n), F32),
        name="hy_fspec",
    )(jnp.asarray(fwd), filt)
    tn = min(cols, 4 * D_HY)
    nbc = tn // D_HY
    col = pl.BlockSpec((L, tn), lambda c: (0, c))
    const = lambda s: pl.BlockSpec(s, lambda c: (0, 0))
    t_bf, ti_bf = jnp.asarray(fwd).astype(BF16), jnp.asarray(inv).astype(BF16)
    for order, g in enumerate((g1, g2)):
        z = pl.pallas_call(
            functools.partial(_hy_direct_kernel, L=L, nbc=nbc, order=order),
            grid=(cols // tn,),
            in_specs=[col, col, const((2 * L, n)), pl.BlockSpec((1, 1, D_HY), lambda c: (order, 0, 0)),
                      const((2 * L, L)), const((L, 2 * L))],
            out_specs=col,
            out_shape=jax.ShapeDtypeStruct((L, cols), F32 if order == 0 else BF16),
            compiler_params=_cp("arbitrary"),
            name="hy_direct",
        )(z, g, spec, skip[:, None, :], t_bf, ti_bf)
    return z


def _kron_tables(t1, t1i, G):
    n2, two_n1, h1 = t1.shape
    ng = n2 // G
    eye = np.eye(two_n1, dtype=np.float32)
    rep_rows = jnp.asarray(np.repeat(eye, G, axis=0))
    rep_cols = jnp.asarray(np.repeat(eye, G, axis=1))
    c_fwd = t1.reshape(ng, G, two_n1, h1).transpose(0, 2, 1, 3).reshape(ng, two_n1, G * h1)
    c_swapped = t1i.reshape(ng, G * h1, two_n1)
    c_natural = t1i.reshape(ng, G, h1, two_n1).transpose(0, 2, 1, 3).reshape(ng, h1 * G, two_n1)

    def expand(compact, rep, left, row_key, lane_key):
        r, c = (rep.shape[0], compact.shape[2]) if left else (compact.shape[1], rep.shape[1])
        return pl.pallas_call(
            functools.partial(_kron_expand_kernel, left=left, row_key=row_key, lane_key=lane_key),
            grid=(ng,),
            in_specs=[pl.BlockSpec((1,) + compact.shape[1:], lambda i: (i, 0, 0)),
                      pl.BlockSpec(rep.shape, lambda i: (0, 0))],
            out_specs=pl.BlockSpec((1, r, c), lambda i: (i, 0, 0)),
            out_shape=jax.ShapeDtypeStruct((ng, r, c), BF16),
            compiler_params=_cp("arbitrary"),
            name="hy_kron",
        )(jnp.asarray(compact), rep)

    fwd = expand(c_fwd, rep_rows, True, lambda r: r % G, lambda c: c // h1)
    inv_swapped = expand(c_swapped, rep_cols, False, lambda r: r // h1, lambda c: c % G)
    inv_natural = expand(c_natural, rep_cols, False, lambda r: r % G, lambda c: c % G)
    return fwd, inv_swapped, inv_natural


def _kron_expand_kernel(c_ref, rep_ref, o_ref, *, left, row_key, lane_key):
    comp = c_ref[0].astype(BF16)
    rep = rep_ref[...].astype(BF16)
    full = (jnp.dot(rep, comp, preferred_element_type=F32) if left
            else jnp.dot(comp, rep, preferred_element_type=F32))
    rows = lax.broadcasted_iota(jnp.int32, full.shape, 0)
    lanes = lax.broadcasted_iota(jnp.int32, full.shape, 1)
    o_ref[0] = jnp.where(row_key(rows) == lane_key(lanes), full, 0.0).astype(BF16)


def _hy_stage1_kernel(x_ref, t_ref, re_ref, im_ref, *, G, n1, ct):
    h1, cols = x_ref.shape[1], x_ref.shape[2]
    x = x_ref[...].reshape(G * h1, cols).astype(BF16)
    for c0 in range(0, cols, ct):
        a = jnp.dot(t_ref[0], x[:, c0:c0 + ct], preferred_element_type=F32)
        re_ref[:, :, c0:c0 + ct] = a[:n1 * G].reshape(n1, G, ct).astype(BF16)
        im_ref[:, :, c0:c0 + ct] = a[n1 * G:].reshape(n1, G, ct).astype(BF16)


def _hy_stage1(x3, t_fwd, *, G):
    n2, h1, cols = x3.shape
    n1 = t_fwd.shape[1] // (2 * G)
    ospec = pl.BlockSpec((n1, G, cols), lambda j: (0, j, 0))
    oshape = jax.ShapeDtypeStruct((n1, n2, cols), BF16)
    return pl.pallas_call(
        functools.partial(_hy_stage1_kernel, G=G, n1=n1, ct=D_HY),
        grid=(n2 // G,),
        in_specs=[pl.BlockSpec((G, h1, cols), lambda j: (j, 0, 0)),
                  pl.BlockSpec((1,) + t_fwd.shape[1:], lambda j: (j, 0, 0))],
        out_specs=[ospec, ospec],
        out_shape=[oshape, oshape],
        compiler_params=_cp("arbitrary"),
        name="hy_stage1",
    )(x3, t_fwd)


def _hy_fspec2_kernel(re_ref, im_ref, t2_ref, fre_ref, fim_ref, *, kb, k2):
    for j in range(kb):
        a2 = jnp.concatenate([re_ref[j], im_ref[j]], axis=0)
        x = jnp.dot(t2_ref[...], a2, preferred_element_type=F32)
        fre_ref[j] = x[:k2]
        fim_ref[j] = x[k2:]


def _hy_stage2_kernel(re_ref, im_ref, fre_ref, fim_ref, skip_ref, t2_ref, t2i_ref, ore_ref, oim_ref,
                      *, kb, k2, n2, nbc, order):
    for j in range(kb):
        a2 = jnp.concatenate([re_ref[j], im_ref[j]], axis=0)
        x = jnp.dot(t2_ref[...], a2, preferred_element_type=F32)
        fr, fi = _filter_spectrum(fre_ref[j], fim_ref[j], order, skip_ref[0])
        y = _spectral_product(x[:k2], x[k2:], fr, fi, nbc).astype(BF16)
        a = jnp.dot(t2i_ref[...], y, preferred_element_type=F32)
        ore_ref[j] = a[:n2].astype(BF16)
        oim_ref[j] = a[n2:].astype(BF16)


def _hy_stage3_kernel(re_ref, im_ref, t_ref, g_ref, o_ref, *, G, n1, ct):
    cols = re_ref.shape[2]
    for c0 in range(0, cols, ct):
        sl = slice(c0, c0 + ct)
        a2 = jnp.concatenate([re_ref[:, :, sl].reshape(n1 * G, ct), im_ref[:, :, sl].reshape(n1 * G, ct)], axis=0)
        conv = jnp.dot(t_ref[0], a2, preferred_element_type=F32)
        g = g_ref[:, :, sl]
        o_ref[:, :, sl] = (g * conv.reshape(g.shape)).astype(o_ref.dtype)


def _two_level_tables(L, *, n1=FFT_N1, G=FFT_G):
    t1, t2, t2i, t1i = _dft_two_level_tables(L, n1)
    return (jnp.asarray(t2).astype(BF16), jnp.asarray(t2i).astype(BF16)) + _kron_tables(t1, t1i, G)


def _hyena_two_level(z3, g1_3, g2, filt, skip, tables, *, n1=FFT_N1, G=FFT_G, kb=4):
    n2, h1, cols = z3.shape
    L = n2 * h1
    k2 = n2 // 2
    nbc = cols // D_HY
    t2_bf, t2i_bf, t_fwd, t_inv_swapped, t_inv_natural = tables
    nf = filt.shape[1]

    f_re, f_im = _hy_stage1(filt.reshape(n2, h1, nf), t_fwd, G=G)
    blk = lambda c: pl.BlockSpec((kb, n2, c), lambda i: (i, 0, 0))
    hblk = lambda c: pl.BlockSpec((kb, k2, c), lambda i: (i, 0, 0))
    const = lambda s: pl.BlockSpec(s, lambda i: (0, 0))
    fshape = jax.ShapeDtypeStruct((n1, k2, nf), F32)
    fre, fim = pl.pallas_call(
        functools.partial(_hy_fspec2_kernel, kb=kb, k2=k2),
        grid=(n1 // kb,),
        in_specs=[blk(nf), blk(nf), const((n2, 2 * n2))],
        out_specs=[hblk(nf), hblk(nf)],
        out_shape=[fshape, fshape],
        compiler_params=_cp("arbitrary"),
        name="hy_fspec2",
    )(f_re, f_im, t2_bf)

    ashape = jax.ShapeDtypeStruct((n1, n2, cols), BF16)
    grp = pl.BlockSpec((n1, G, cols), lambda j: (0, j, 0))
    tspec = lambda t: pl.BlockSpec((1,) + t.shape[1:], lambda j: (j, 0, 0))
    for order in range(2):
        a_re, a_im = _hy_stage1(z3, t_fwd, G=G)
        b_re, b_im = pl.pallas_call(
            functools.partial(_hy_stage2_kernel, kb=kb, k2=k2, n2=n2, nbc=nbc, order=order),
            grid=(n1 // kb,),
            in_specs=[blk(cols), blk(cols), hblk(nf), hblk(nf),
                      pl.BlockSpec((1, 1, D_HY), lambda i: (order, 0, 0)),
                      const((n2, 2 * n2)), const((2 * n2, n2))],
            out_specs=[blk(cols), blk(cols)],
            out_shape=[ashape, ashape],
            compiler_params=_cp("arbitrary"),
            name="hy_stage2",
        )(a_re, a_im, fre, fim, skip[:, None, :], t2_bf, t2i_bf)
        if order == 0:
            table, gate = t_inv_swapped, g1_3
            io = pl.BlockSpec((G, h1, cols), lambda j: (j, 0, 0))
            oshape = jax.ShapeDtypeStruct((n2, h1, cols), F32)
        else:
            table, gate = t_inv_natural, g2.reshape(h1, n2, cols)
            io = pl.BlockSpec((h1, G, cols), lambda j: (0, j, 0))
            oshape = jax.ShapeDtypeStruct((h1, n2, cols), BF16)
        z3 = pl.pallas_call(
            functools.partial(_hy_stage3_kernel, G=G, n1=n1, ct=D_HY),
            grid=(n2 // G,),
            in_specs=[grp, grp, tspec(table), io],
            out_specs=io,
            out_shape=oshape,
            compiler_params=_cp("arbitrary"),
            name="hy_stage3",
        )(b_re, b_im, table, gate)
    return z3.reshape(L, cols)


def _hyena(hv_t, h1_t, h2_t, conv_w, conv_b, filt, skip, tables=None):
    L, cols = hv_t.shape
    if _two_level(L):
        z3, g1_3, g2 = _hy_conv3(hv_t, h1_t, h2_t, conv_w, conv_b, swap=(True, True, False))
        return _hyena_two_level(z3, g1_3, g2, filt, skip, tables or _two_level_tables(L))
    z, g1, g2 = _hy_conv3(hv_t, h1_t, h2_t, conv_w, conv_b)
    return _hyena_direct(z, g1, g2, filt, skip)


def _route(logits):
    lane = lax.broadcasted_iota(jnp.int32, logits.shape, 1)
    is_g = (lane >= N_EXPERTS) & (lane < N_EXPERTS + N_GROUPS)
    big = jnp.int32(1 << 20)
    gl = jnp.where(is_g, logits, NEG)
    gmax = jnp.max(gl, axis=-1, keepdims=True)
    gidx = jnp.min(jnp.where(gl == gmax, lane - N_EXPERTS, big), axis=-1, keepdims=True)
    p_group = 1.0 / jnp.sum(jnp.where(is_g, jnp.exp(gl - gmax), 0.0), axis=-1, keepdims=True)
    sel = (lane < N_EXPERTS) & ((lane // EXPERTS_PER_GROUP) == gidx)
    el = jnp.where(sel, logits, NEG)
    ee = jnp.where(sel, jnp.exp(el - jnp.max(el, axis=-1, keepdims=True)), 0.0)
    pe = ee / jnp.sum(ee, axis=-1, keepdims=True)
    p1 = jnp.max(jnp.where(sel, pe, -1.0), axis=-1, keepdims=True)
    i1 = jnp.min(jnp.where(sel & (pe == p1), lane, big), axis=-1, keepdims=True)
    rest = sel & (lane != i1)
    p2 = jnp.max(jnp.where(rest, pe, -1.0), axis=-1, keepdims=True)
    i2 = jnp.min(jnp.where(rest & (pe == p2), lane, big), axis=-1, keepdims=True)
    den = p1 + p2
    cmb = jnp.where(lane == i1, p_group * (p1 / den), jnp.where(lane == i2, p_group * (p2 / den), 0.0))
    return jnp.where(lane == N_EXPERTS, gidx.astype(F32), cmb)


def _outproj_kernel(att_ref, hf_ref, hb_ref, gt_ref, hy_ref, x_ref, mod_ref, w_ref, g_ref, b_ref,
                    wr_ref, br_ref, x1_ref, u2_ref, cmb_ref):
    rg = (hf_ref[...] + hb_ref[...]) * jax.nn.gelu(gt_ref[...])
    mix = (jnp.dot(att_ref[0], w_ref[0:D_ATT, :], preferred_element_type=F32)
           + jnp.dot(rg.astype(BF16), w_ref[D_ATT:D_ATT + D_RNN, :], preferred_element_type=F32)
           + jnp.dot(hy_ref[...].astype(BF16), w_ref[D_ATT + D_RNN:, :], preferred_element_type=F32))
    m = mod_ref[0]
    x1 = _ln(ALPHA * x_ref[...] + m[2:3] * mix) * g_ref[...] + b_ref[...]
    u2 = _ln(x1) * (1.0 + m[4:5]) + m[3:4]
    u_hi = u2.astype(BF16)
    u_lo = (u2 - u_hi.astype(F32)).astype(BF16)
    wr = wr_ref[...]
    w_hi = wr.astype(BF16)
    w_lo = (wr - w_hi.astype(F32)).astype(BF16)
    hi = jnp.dot(u_hi, jnp.concatenate([w_hi, w_lo], axis=1), preferred_element_type=F32)
    lo = jnp.dot(u_lo, w_hi, preferred_element_type=F32)
    logits = hi[:, 0:LANES] + hi[:, LANES:] + lo + br_ref[...]
    x1_ref[...] = x1
    u2_ref[...] = u_hi
    cmb_ref[...] = _route(logits)


def _outproj(att, hf_t, hb_t, gt_t, hy_t, x2d, mod_l, w_bf, ln_g, ln_b, wr, br, *, nb, L, tm,
             cond_base, cond_per_batch):
    tpb = L // tm
    d = x2d.shape[1]
    cond = (lambda i: cond_base + i // tpb) if cond_per_batch else (lambda i: cond_base)
    t_spec = pl.BlockSpec((tm, D_RNN), lambda i: (i % tpb, i // tpb))
    tok = lambda c: pl.BlockSpec((tm, c), lambda i: (i, 0))
    const = lambda s: pl.BlockSpec(s, lambda i: (0, 0))
    n = nb * L
    return pl.pallas_call(
        _outproj_kernel,
        grid=(nb * tpb,),
        in_specs=[pl.BlockSpec((1, tm, D_ATT), lambda i: (i // tpb, i % tpb, 0)),
                  t_spec, t_spec, t_spec, t_spec, tok(d),
                  pl.BlockSpec((1, 6, d), lambda i: (cond(i), 0, 0)),
                  const((d, d)), const((1, d)), const((1, d)), const((d, LANES)), const((1, LANES))],
        out_specs=[tok(d), tok(d), tok(LANES)],
        out_shape=[jax.ShapeDtypeStruct((n, d), F32), jax.ShapeDtypeStruct((n, d), BF16),
                   jax.ShapeDtypeStruct((n, LANES), F32)],
        compiler_params=_cp("arbitrary"),
        name="outproj",
    )(att, hf_t, hb_t, gt_t, hy_t, x2d, mod_l, w_bf, ln_g[None], ln_b[None], wr, br)


def _moe_kernel(u_ref, cmb_ref, x1_ref, mod_ref, wgu_ref, wd_ref, g_ref, b_ref, o_ref, ys_ref, *, T, BLK):
    SL = T + N_GROUPS * BLK
    u = u_ref[...]
    cmb = cmb_ref[...]
    lane_f = lax.broadcasted_iota(jnp.int32, (T, LANES), 1).astype(F32)
    gid_col = jnp.sum(jnp.where(lane_f == float(N_EXPERTS), cmb, 0.0), axis=-1, keepdims=True)
    gid_row = cmb.T[N_EXPERTS:N_EXPERTS + 1, :]
    sub = lax.broadcasted_iota(jnp.int32, (8, T), 0).astype(F32)
    onehot = jnp.where(lane_f == gid_col, 1.0, 0.0)
    onehot_t = jnp.where(sub == gid_row, 1.0, 0.0)
    ri = lax.broadcasted_iota(jnp.int32, (T, T), 0)
    ci = lax.broadcasted_iota(jnp.int32, (T, T), 1)
    before = jnp.where(ci < ri, 1.0, 0.0).astype(BF16)
    before_t = jnp.where(ri < ci, 1.0, 0.0).astype(BF16)
    rank_col = jnp.sum(onehot * jnp.dot(before, onehot.astype(BF16), preferred_element_type=F32),
                       axis=-1, keepdims=True)
    rank_row = jnp.sum(onehot_t * jnp.dot(onehot_t.astype(BF16), before_t, preferred_element_type=F32),
                       axis=0, keepdims=True)
    counts = jnp.sum(onehot, axis=0, keepdims=True)
    lane_row = lax.broadcasted_iota(jnp.int32, (1, LANES), 1)
    sub_col = lax.broadcasted_iota(jnp.int32, (8, 1), 0)
    base = jnp.int32(0)
    bases = []
    base_row = jnp.zeros((1, LANES), jnp.int32)
    base_sub = jnp.zeros((8, 1), jnp.int32)
    for c in range(N_GROUPS):
        n_c = jnp.sum(jnp.where(lane_row == c, counts, 0.0)).astype(jnp.int32)
        bases.append(base)
        base_row = jnp.where(lane_row == c, base, base_row)
        base_sub = jnp.where(sub_col == c, base, base_sub)
        base = base + ((n_c + (BLK - 1)) // BLK) * BLK
    total = base
    dest_col = (jnp.sum(onehot * base_row.astype(F32), axis=-1, keepdims=True) + rank_col).astype(jnp.int32)
    dest_row = (jnp.sum(onehot_t * base_sub.astype(F32), axis=0, keepdims=True) + rank_row).astype(jnp.int32)
    cmb_hi = cmb.astype(BF16)
    cmb_lo = (cmb - cmb_hi.astype(F32)).astype(BF16)
    blk_rows = lax.broadcasted_iota(jnp.int32, (BLK, T), 0)
    blk_lane = lax.broadcasted_iota(jnp.int32, (BLK, LANES), 1)

    def block(b, carry):
        r0 = b * BLK
        rows = pl.ds(pl.multiple_of(r0, BLK), BLK)

        @pl.when(r0 < total)
        def _():
            grp = ((r0 >= bases[1]).astype(jnp.int32) + (r0 >= bases[2]).astype(jnp.int32)
                   + (r0 >= bases[3]).astype(jnp.int32))
            p = jnp.where(blk_rows + r0 == dest_row, 1.0, 0.0).astype(BF16)
            xb = jnp.dot(p, u, preferred_element_type=F32).astype(BF16)
            wb = (jnp.dot(p, cmb_hi, preferred_element_type=F32)
                  + jnp.dot(p, cmb_lo, preferred_element_type=F32))
            acc = jnp.zeros((BLK, u.shape[1]), F32)
            for j in range(EXPERTS_PER_GROUP):
                e = grp * EXPERTS_PER_GROUP + j
                gu = jnp.dot(xb, wgu_ref[e], preferred_element_type=F32)
                gate = gu[:, :D_EXPERT]
                w = jnp.sum(jnp.where(blk_lane == e, wb, 0.0), axis=-1, keepdims=True)
                h = gate * _sigmoid(gate) * gu[:, D_EXPERT:] * w
                acc = acc + jnp.dot(h.astype(BF16), wd_ref[e], preferred_element_type=F32)
            ys_ref[rows, :] = acc.astype(BF16)

        @pl.when(r0 >= total)
        def _():
            ys_ref[rows, :] = jnp.zeros((BLK, u.shape[1]), BF16)

        return carry

    lax.fori_loop(0, SL // BLK, block, 0)
    p_t = jnp.where(lax.broadcasted_iota(jnp.int32, (T, SL), 1) == dest_col, 1.0, 0.0).astype(BF16)
    moe = jnp.dot(p_t, ys_ref[...], preferred_element_type=F32)
    m = mod_ref[0]
    o_ref[...] = _ln(ALPHA * x1_ref[...] + m[5:6] * moe) * g_ref[...] + b_ref[...]


def _moe(u2, cmb, x1, mod_l, wgu, wd, ln_g, ln_b, *, tm, tiles_per_cond, cond_base, cond_per_batch):
    n, d = x1.shape
    ne = wgu.shape[0]
    cond = (lambda i: cond_base + i // tiles_per_cond) if cond_per_batch else (lambda i: cond_base)
    tok = lambda c: pl.BlockSpec((tm, c), lambda i: (i, 0))
    const = lambda s: pl.BlockSpec(s, lambda i: (0,) * len(s))
    resident = lambda s: pl.BlockSpec(s, lambda i: (0,) * len(s), pipeline_mode=pl.Buffered(1))
    return pl.pallas_call(
        functools.partial(_moe_kernel, T=tm, BLK=MOE_BLK),
        grid=(n // tm,),
        in_specs=[tok(d), tok(LANES), tok(d),
                  pl.BlockSpec((1, 6, d), lambda i: (cond(i), 0, 0)),
                  resident((ne, d, 2 * D_EXPERT)), resident((ne, D_EXPERT, d)),
                  const((1, d)), const((1, d))],
        out_specs=tok(d),
        out_shape=jax.ShapeDtypeStruct((n, d), F32),
        scratch_shapes=[pltpu.VMEM((tm + N_GROUPS * MOE_BLK, d), BF16)],
        compiler_params=_cp("arbitrary"),
        name="moe",
    )(u2, cmb, x1, mod_l, wgu, wd, ln_g[None], ln_b[None])


def _stream_layer(x2d, mod_l, p, *, nb, L, tm, cond_base, cond_per_batch, ctx, cache=None, h0=None):
    kw = dict(nb=nb, L=L, tm=tm, cond_base=cond_base, cond_per_batch=cond_per_batch)
    q, k, v, xr_t, gt_t, hv_t, h1_t, h2_t = _inproj(x2d, mod_l, p["w_in"], kv_dtype=F32 if ctx else BF16, **kw)
    if ctx:
        att = _ctx_attention(q, k, v)
    else:
        att = _nbr_attention(q, k, v, cache[0], cache[1], p["nbr_bias"])
    a_f, b_f, a_b, b_b = _rg_gates(xr_t, p["rg_conv_w"], p["rg_conv_b"], *p["rg_w"])
    hf_t, hb_t = _rg_scan(a_f, b_f, a_b, b_b, h0[0], h0[1])
    hy_t = _hyena(hv_t, h1_t, h2_t, p["hy_conv_w"], p["hy_conv_b"], p["hy_filt"][L], p["hy_skip"],
                  p["hy_tables"].get(L))
    x1, u2, cmb = _outproj(att, hf_t, hb_t, gt_t, hy_t, x2d, mod_l, p["w_out"], p["ln1_g"], p["ln1_b"],
                           p["router_w"], p["router_b"], **kw)
    tm_moe = min(MOE_TILE, nb * L)
    x2 = _moe(u2, cmb, x1, mod_l, p["moe_wgu"], p["moe_wd"], p["ln2_g"], p["ln2_b"], tm=tm_moe,
              tiles_per_cond=max(L // tm_moe, 1), cond_base=cond_base, cond_per_batch=cond_per_batch)
    return x2, k, v, hf_t, hb_t


def kernel(x_prompt, x_sample, cache_k, cache_v, state_rglru, c, c_ctx, w_mod, b_mod, w_in, attn_rpb, rg_conv_w, rg_conv_b, rg_wa, rg_ba, rg_wx, rg_bx, rg_lambda, hy_conv_w, hy_conv_b, hy_w1, hy_b1, hy_w2, hy_b2, hy_w3, hy_freq, hy_decay, hy_skip, w_out, ln1_g, ln1_b, router_wg, router_bg, router_we, router_be, moe_w_gate, moe_w_up, moe_w_down, ln2_g, ln2_b):
    nbc, Lc, d = x_prompt.shape
    nbl, Ll, _ = x_sample.shape
    depth = w_mod.shape[0]
    past = cache_k.shape[2]

    cond8 = jnp.zeros((8, d), F32).at[0].set(c_ctx).at[1:1 + nbl].set(c)
    mod = _modulation(cond8, w_mod, b_mod).reshape(depth, 8, 6, d)

    xp = x_prompt.reshape(nbc * Lc, d)
    xs = x_sample.reshape(nbl * Ll, d)
    new_k, new_v, new_h = [], [], []
    hy_tables = {L: _two_level_tables(L) for L in {Lc, Ll} if _two_level(L)}
    for l in range(depth):
        pad = jnp.zeros((d, LANES - N_EXPERTS - N_GROUPS), F32)
        p = {
            "w_in": w_in[l].astype(BF16),
            "nbr_bias": _nbr_bias_tiles(attn_rpb[l]),
            "rg_conv_w": rg_conv_w[l], "rg_conv_b": rg_conv_b[l],
            "rg_w": _rg_weights(rg_wa[l], rg_ba[l], rg_wx[l], rg_bx[l], rg_lambda[l]),
            "hy_conv_w": hy_conv_w[l], "hy_conv_b": hy_conv_b[l], "hy_skip": hy_skip[l], "hy_tables": hy_tables,
            "hy_filt": {L: _hy_filters(L, hy_w1[l], hy_b1[l], hy_w2[l], hy_b2[l], hy_w3[l], hy_freq[l], hy_decay[l])
                        for L in sorted({Lc, Ll})},
            "w_out": w_out[l].astype(BF16), "ln1_g": ln1_g[l], "ln1_b": ln1_b[l],
            "router_w": jnp.concatenate([router_we[l], router_wg[l], pad], axis=1),
            "router_b": jnp.concatenate([router_be[l], router_bg[l], pad[0]])[None],
            "moe_wgu": jnp.concatenate([moe_w_gate[l], moe_w_up[l]], axis=-1).astype(BF16),
            "moe_wd": moe_w_down[l].astype(BF16),
            "ln2_g": ln2_g[l], "ln2_b": ln2_b[l],
        }
        zeros_h = jnp.zeros((1, nbc * D_RNN), F32)
        xp, k, v, hf_t, hb_t = _stream_layer(
            xp, mod[l], p, nb=nbc, L=Lc, tm=Lc, cond_base=0, cond_per_batch=False, ctx=True,
            h0=(zeros_h, zeros_h))
        new_k.append(k.reshape(nbc, Lc, N_HEADS, HEAD_DIM))
        new_v.append(v.reshape(nbc, Lc, N_HEADS, HEAD_DIM))
        new_h.append(jnp.stack([hf_t[Lc - 1].reshape(nbc, D_RNN), hb_t[0].reshape(nbc, D_RNN)], axis=1))

        cache = (cache_k[:, l].reshape(nbl, past, D_ATT), cache_v[:, l].reshape(nbl, past, D_ATT))
        h0 = (state_rglru[:, l, 0].reshape(1, nbl * D_RNN), state_rglru[:, l, 1].reshape(1, nbl * D_RNN))
        xs, _, _, _, _ = _stream_layer(
            xs, mod[l], p, nb=nbl, L=Ll, tm=min(Ll, 512), cond_base=1, cond_per_batch=True, ctx=False,
            cache=cache, h0=h0)

    return (xp.reshape(nbc, Lc, d), xs.reshape(nbl, Ll, d),
            jnp.stack(new_k, axis=1), jnp.stack(new_v, axis=1), jnp.stack(new_h, axis=1))
```

```python
import functools
import math

import numpy as np
import jax
import jax.numpy as jnp
from jax import lax
from jax.experimental import pallas as pl
from jax.experimental.pallas import tpu as pltpu

F32 = jnp.float32
BF16 = jnp.bfloat16
HIGHEST = lax.Precision.HIGHEST

DEPTH = 2
D_MODEL = 1024
N_HEADS = 8
HEAD_DIM = 64
D_ATT = N_HEADS * HEAD_DIM
D_RNN = 256
D_HY = 256
GRID_W = 64
WIN_R = 8
WIN_C = 16
RG_CONV = 4
RG_C = 8.0
HY_CONV = 3
POS_EMB_DIM = 33
N_GROUPS = 4
EXPERTS_PER_GROUP = 4
N_EXPERTS = 16
D_EXPERT = 256
ALPHA = (2.0 * DEPTH) ** 0.25
LN_EPS = 1e-5
NEG = -1e30

Q_ROWS = 4
K_ROWS = Q_ROWS + WIN_R
LANES = 128
HALO = 8
MOE_TILE = 512
MOE_BLK = 128
STRIDE_PAD = 4
FFT_G = 16
FFT_N1 = 64
VMEM_LIMIT = 52 * 1024 * 1024


def _cp(*sem):
    return pltpu.CompilerParams(dimension_semantics=sem, vmem_limit_bytes=VMEM_LIMIT)


def _sigmoid(x):
    return 0.5 * jnp.tanh(0.5 * x) + 0.5


def _ln(x):
    mu = jnp.mean(x, axis=-1, keepdims=True)
    xc = x - mu
    var = jnp.mean(xc * xc, axis=-1, keepdims=True)
    return xc * lax.rsqrt(var + LN_EPS)


def _mod_kernel(c_ref, w_ref, b_ref, o_ref):
    c = c_ref[...]
    s = c * jax.nn.sigmoid(c)
    o_ref[0] = jnp.dot(s, w_ref[0], precision=HIGHEST, preferred_element_type=F32) + b_ref[0]


def _modulation(cond8, w_mod, b_mod):
    depth, d, n = w_mod.shape
    tn = n // 4
    return pl.pallas_call(
        _mod_kernel,
        grid=(depth, n // tn),
        in_specs=[pl.BlockSpec((8, d), lambda l, j: (0, 0)),
                  pl.BlockSpec((1, d, tn), lambda l, j: (l, 0, j)),
                  pl.BlockSpec((1, 1, tn), lambda l, j: (l, 0, j))],
        out_specs=pl.BlockSpec((1, 8, tn), lambda l, j: (l, 0, j)),
        out_shape=jax.ShapeDtypeStruct((depth, 8, n), F32),
        compiler_params=_cp("arbitrary", "arbitrary"),
        name="modulation",
    )(cond8, w_mod, b_mod.reshape(depth, 1, n))


def _inproj_kernel(x_ref, mod_ref, w_ref, *refs, n_prev):
    if n_prev:
        pk_ref, pv_ref = refs[:2]
        refs = refs[2:]
    q_ref, k_ref, v_ref, xr_ref, gt_ref, hv_ref, h1_ref, h2_ref = refs
    for i in range(n_prev):
        k_ref[0, i] = pk_ref[0, i]
        v_ref[0, i] = pv_ref[0, i]
    m = mod_ref[0]
    u = (_ln(x_ref[...]) * (1.0 + m[1:2]) + m[0:1]).astype(BF16)

    def proj(lo, hi):
        return jnp.dot(u, w_ref[:, lo:hi], preferred_element_type=F32)

    o = 0
    q_ref[0] = (proj(o, o + D_ATT) * (HEAD_DIM ** -0.5)).astype(q_ref.dtype)
    o += D_ATT
    k_ref[0, n_prev] = proj(o, o + D_ATT).astype(k_ref.dtype)
    o += D_ATT
    v_ref[0, n_prev] = proj(o, o + D_ATT).astype(v_ref.dtype)
    o += D_ATT
    for ref in (xr_ref, gt_ref, hv_ref, h1_ref, h2_ref):
        ref[...] = proj(o, o + D_RNN)
        o += D_RNN


def _inproj(x2d, mod_l, w_bf, *, nb, L, tm, cond_base, cond_per_batch, kv_dtype, prev_kv=None):
    tpb = L // tm
    d = x2d.shape[1]
    d_in = w_bf.shape[1]
    n_prev = prev_kv[0].shape[1] if prev_kv else 0
    cond = (lambda i: cond_base + i // tpb) if cond_per_batch else (lambda i: cond_base)
    att_spec = pl.BlockSpec((1, tm, D_ATT), lambda i: (i // tpb, i % tpb, 0))
    kv_spec = lambda n: pl.BlockSpec((1, n, tm, D_ATT), lambda i: (i // tpb, 0, i % tpb, 0))
    t_spec = pl.BlockSpec((tm, D_RNN), lambda i: (i % tpb, i // tpb))
    kv_shape = jax.ShapeDtypeStruct((nb, n_prev + 1, L, D_ATT), kv_dtype)
    t_shape = jax.ShapeDtypeStruct((L, nb * D_RNN), F32)
    return pl.pallas_call(
        functools.partial(_inproj_kernel, n_prev=n_prev),
        grid=(nb * tpb,),
        in_specs=[pl.BlockSpec((tm, d), lambda i: (i, 0)),
                  pl.BlockSpec((1, 6, d), lambda i: (cond(i), 0, 0)),
                  pl.BlockSpec((d, d_in), lambda i: (0, 0))] + ([kv_spec(n_prev)] * 2 if n_prev else []),
        out_specs=[att_spec, kv_spec(n_prev + 1), kv_spec(n_prev + 1), t_spec, t_spec, t_spec, t_spec, t_spec],
        out_shape=[jax.ShapeDtypeStruct((nb, L, D_ATT), BF16), kv_shape, kv_shape,
                   t_shape, t_shape, t_shape, t_shape, t_shape],
        compiler_params=_cp("arbitrary"),
        name="inproj",
    )(x2d, mod_l, w_bf, *(prev_kv or ()))


def _nt(a, b):
    return lax.dot_general(a, b, (((1,), (1,)), ((), ())), preferred_element_type=F32)


def _ctx_attn_kernel(q_ref, k_ref, v_ref, o_ref):
    q = q_ref[0]
    k = k_ref[0, 0].astype(BF16)
    v = v_ref[0, 0].astype(BF16)
    outs = []
    for h in range(N_HEADS):
        sl = slice(h * HEAD_DIM, (h + 1) * HEAD_DIM)
        s = _nt(q[:, sl], k[:, sl])
        e = jnp.exp(s - jnp.max(s, axis=-1, keepdims=True))
        den = jnp.sum(e, axis=-1, keepdims=True)
        outs.append(jnp.dot(e.astype(BF16), v[:, sl], preferred_element_type=F32) / den)
    o_ref[0] = jnp.concatenate(outs, axis=-1).astype(o_ref.dtype)


def _ctx_attention(q, k, v):
    nb, L, _ = q.shape
    last = k.shape[1] - 1
    spec = pl.BlockSpec((1, L, D_ATT), lambda b: (b, 0, 0))
    kv_spec = pl.BlockSpec((1, 1, L, D_ATT), lambda b: (b, last, 0, 0))
    return pl.pallas_call(
        _ctx_attn_kernel,
        grid=(nb,),
        in_specs=[spec, kv_spec, kv_spec],
        out_specs=spec,
        out_shape=jax.ShapeDtypeStruct((nb, L, D_ATT), BF16),
        compiler_params=_cp("arbitrary"),
        name="ctx_attn",
    )(q, k, v)


N_DR = 2 * WIN_R - 1
PAIR_LEFT = N_DR - 1
PAIR_RIGHT = 2 * N_DR - 1
PAIR_NONE = 3 * N_DR - 1


def _nbr_bias_tiles(rpb):
    cols = np.arange(GRID_W)
    c_start = np.clip(cols - WIN_C // 2, 0, GRID_W - WIN_C)
    col_mask = (cols[None, :] >= c_start[:, None]) & (cols[None, :] < c_start[:, None] + WIN_C)
    dc_idx = np.clip(cols[None, :] - cols[:, None], -(WIN_C - 1), WIN_C - 1) + (WIN_C - 1)
    sel = (dc_idx[None] == np.arange(2 * WIN_C - 1)[:, None, None]).astype(np.float32)
    colb = jnp.einsum('hab,bqk->haqk', rpb.astype(F32), jnp.asarray(sel), precision=HIGHEST)
    colb = jnp.where(col_mask[None, None], colb, NEG)
    neg = jnp.full_like(colb, NEG)
    both = jnp.concatenate([colb[:, :-1], colb[:, 1:]], axis=-1)
    left_masked = jnp.concatenate([neg, colb], axis=-1)
    right_masked = jnp.concatenate([colb, neg], axis=-1)
    none = jnp.concatenate([neg[:, :1], neg[:, :1]], axis=-1)
    return jnp.concatenate([both, left_masked, right_masked, none], axis=1)


def _nbr_attn_kernel(q_ref, k_ref, v_ref, kc_ref, vc_ref, bias_ref, o_ref, *, rows):
    g = pl.program_id(1)
    r0 = jnp.clip(Q_ROWS * g - WIN_R // 2, 0, rows - K_ROWS)
    start = pl.multiple_of(r0 * GRID_W, Q_ROWS * GRID_W)
    nk = K_ROWS * GRID_W
    kw = k_ref[0, 0, pl.ds(start, nk), :]
    vw = v_ref[0, 0, pl.ds(start, nk), :]
    kc = kc_ref[0].astype(BF16)
    vc = vc_ref[0].astype(BF16)
    q = q_ref[0]
    tile = []
    for j in range(Q_ROWS):
        r = Q_ROWS * g + j
        lo = jnp.clip(r - WIN_R // 2, 0, rows - WIN_R) - r + (WIN_R - 1)
        tile.append([])
        for p in range(K_ROWS // 2):
            d0 = r0 + 2 * p - r + (WIN_R - 1)
            lv = (d0 >= lo) & (d0 < lo + WIN_R)
            rv = (d0 + 1 >= lo) & (d0 + 1 < lo + WIN_R)
            idx = jnp.where(lv & rv, d0, jnp.where(rv, PAIR_LEFT + d0 + 1, jnp.where(lv, PAIR_RIGHT + d0, PAIR_NONE)))
            tile[j].append(jnp.clip(idx, 0, PAIR_NONE))
    outs = []
    for h in range(N_HEADS):
        sl = slice(h * HEAD_DIM, (h + 1) * HEAD_DIM)
        qh = q[:, sl]
        bias = jnp.concatenate(
            [jnp.concatenate([bias_ref[h, tile[j][p]] for p in range(K_ROWS // 2)], axis=-1)
             for j in range(Q_ROWS)], axis=0)
        s1 = _nt(qh, kw[:, sl]) + bias
        s2 = _nt(qh, kc[:, sl])
        m = jnp.maximum(jnp.max(s1, axis=-1, keepdims=True), jnp.max(s2, axis=-1, keepdims=True))
        e1 = jnp.exp(s1 - m)
        e2 = jnp.exp(s2 - m)
        den = jnp.sum(e1, axis=-1, keepdims=True) + jnp.sum(e2, axis=-1, keepdims=True)
        o = (jnp.dot(e1.astype(BF16), vw[:, sl], preferred_element_type=F32)
             + jnp.dot(e2.astype(BF16), vc[:, sl], preferred_element_type=F32))
        outs.append(o / den)
    o_ref[0] = jnp.concatenate(outs, axis=-1).astype(o_ref.dtype)


def _nbr_attention(q, k, v, kc, vc, bias):
    nb, L, _ = q.shape
    rows = L // GRID_W
    ng = rows // Q_ROWS
    past = kc.shape[1]
    tq = Q_ROWS * GRID_W
    full = pl.BlockSpec((1, 1, L, D_ATT), lambda b, g: (b, 0, 0, 0))
    ctx = pl.BlockSpec((1, past, D_ATT), lambda b, g: (b, 0, 0))
    qspec = pl.BlockSpec((1, tq, D_ATT), lambda b, g: (b, g, 0))
    bspec = pl.BlockSpec(bias.shape, lambda b, g: (0, 0, 0, 0), pipeline_mode=pl.Buffered(1))
    return pl.pallas_call(
        functools.partial(_nbr_attn_kernel, rows=rows),
        grid=(nb, ng),
        in_specs=[qspec, full, full, ctx, ctx, bspec],
        out_specs=qspec,
        out_shape=jax.ShapeDtypeStruct((nb, L, D_ATT), BF16),
        compiler_params=_cp("arbitrary", "arbitrary"),
        name="nbr_attn",
    )(q, k, v, kc, vc, bias)


def _fill_padded(xp_ref, src_ref, L):
    z = jnp.zeros((HALO, xp_ref.shape[1]), F32)
    xp_ref[0:HALO, :] = z
    xp_ref[HALO:HALO + L, :] = src_ref[...]
    xp_ref[HALO + L:2 * HALO + L, :] = z


def _rg_gates_kernel(x_ref, cw_ref, cb_ref, w_ref, b_ref, lam_ref,
                     af_ref, bf_ref, ab_ref, bb_ref, xp_ref, *, L, tc):
    _fill_padded(xp_ref, x_ref, L)
    sp = jax.nn.softplus(-lam_ref[0])
    for t0 in range(0, L, tc):
        xc = cb_ref[...]
        for k in range(RG_CONV):
            xc = xc + cw_ref[k:k + 1, :] * xp_ref[pl.ds(HALO + t0 + k - RG_CONV // 2, tc), :]
        z = jnp.dot(xc.astype(BF16), w_ref[0], preferred_element_type=F32) + b_ref[0]
        for e, (a_ref, o_ref) in enumerate(((af_ref, bf_ref), (ab_ref, bb_ref))):
            r = _sigmoid(z[:, e * LANES:(e + 1) * LANES])
            i = _sigmoid(z[:, (2 + e) * LANES:(3 + e) * LANES])
            log_a = (-RG_C) * r * sp[:, e * LANES:(e + 1) * LANES]
            a = jnp.exp(log_a)
            one_minus_a2 = -jnp.tanh(log_a) * (a * a + 1.0)
            a_ref[pl.ds(t0, tc), :] = a
            o_ref[pl.ds(t0, tc), :] = jnp.sqrt(one_minus_a2) * i * xc


def _blockdiag2(w2):
    z = jnp.zeros_like(w2[0])
    return jnp.concatenate([jnp.concatenate([w2[0], z], 1), jnp.concatenate([z, w2[1]], 1)], 0)


def _rg_weights(wa, ba, wx, bx, lam):
    ws, bs, ls = [], [], []
    for hf in range(D_RNN // LANES):
        blk = slice(2 * hf, 2 * hf + 2)
        ch = slice(hf * LANES, (hf + 1) * LANES)
        ws.append(jnp.concatenate([_blockdiag2(wa[0, blk]), _blockdiag2(wa[1, blk]),
                                   _blockdiag2(wx[0, blk]), _blockdiag2(wx[1, blk])], axis=1))
        bs.append(jnp.concatenate([ba[0, ch], ba[1, ch], bx[0, ch], bx[1, ch]])[None])
        ls.append(jnp.concatenate([lam[0, ch], lam[1, ch]])[None])
    return jnp.stack(ws).astype(BF16), jnp.stack(bs), jnp.stack(ls)


def _rg_gates(xr_t, conv_w, conv_b, w_bd, b_cat, lam_cat):
    L, cols = xr_t.shape
    nh = D_RNN // LANES
    tc = min(L, 512)
    col = pl.BlockSpec((L, LANES), lambda c: (0, c))
    shape = jax.ShapeDtypeStruct((L, cols), F32)
    return pl.pallas_call(
        functools.partial(_rg_gates_kernel, L=L, tc=tc),
        grid=(cols // LANES,),
        in_specs=[col,
                  pl.BlockSpec((RG_CONV, LANES), lambda c: (0, c % nh)),
                  pl.BlockSpec((1, LANES), lambda c: (0, c % nh)),
                  pl.BlockSpec((1, LANES, 4 * LANES), lambda c: (c % nh, 0, 0)),
                  pl.BlockSpec((1, 1, 4 * LANES), lambda c: (c % nh, 0, 0)),
                  pl.BlockSpec((1, 1, 2 * LANES), lambda c: (c % nh, 0, 0))],
        out_specs=[col, col, col, col],
        out_shape=[shape, shape, shape, shape],
        scratch_shapes=[pltpu.VMEM((L + 2 * HALO, LANES), F32)],
        compiler_params=_cp("arbitrary"),
        name="rg_gates",
    )(xr_t, conv_w, conv_b.reshape(1, D_RNN), w_bd, b_cat, lam_cat)


def _rg_scan_kernel(af_ref, bf_ref, ab_ref, bb_ref, h0f_ref, h0b_ref, hf_ref, hb_ref, cf_ref, cb_ref, *, tc):
    @pl.when(pl.program_id(1) == 0)
    def _():
        cf_ref[...] = h0f_ref[...]
        cb_ref[...] = h0b_ref[...]

    def body(t, carry):
        hf, hb = carry
        hf = af_ref[pl.ds(t, 1), :] * hf + bf_ref[pl.ds(t, 1), :]
        hf_ref[pl.ds(t, 1), :] = hf
        tb = tc - 1 - t
        hb = ab_ref[pl.ds(tb, 1), :] * hb + bb_ref[pl.ds(tb, 1), :]
        hb_ref[pl.ds(tb, 1), :] = hb
        return hf, hb

    hf, hb = lax.fori_loop(0, tc, body, (cf_ref[...], cb_ref[...]), unroll=8)
    cf_ref[...] = hf
    cb_ref[...] = hb


def _rg_scan(a_f, b_f, a_b, b_b, h0f, h0b):
    L, cols = a_f.shape
    cw = min(cols, 1024)
    tc = min(L, 512)
    nt = L // tc
    fwd = pl.BlockSpec((tc, cw), lambda c, i: (i, c))
    bwd = pl.BlockSpec((tc, cw), lambda c, i: (nt - 1 - i, c))
    row = pl.BlockSpec((1, cw), lambda c, i: (0, c))
    shape = jax.ShapeDtypeStruct((L, cols), F32)
    return pl.pallas_call(
        functools.partial(_rg_scan_kernel, tc=tc),
        grid=(cols // cw, nt),
        in_specs=[fwd, fwd, bwd, bwd, row, row],
        out_specs=[fwd, bwd],
        out_shape=[shape, shape],
        scratch_shapes=[pltpu.VMEM((1, cw), F32), pltpu.VMEM((1, cw), F32)],
        compiler_params=_cp("arbitrary", "arbitrary"),
        name="rg_scan",
    )(a_f, b_f, a_b, b_b, h0f, h0b)


def _hy_filter_kernel(feat_ref, w1_ref, b1_ref, w2_ref, b2_ref, w3_ref, fr_ref, dec_ref, o_ref, *, tm):
    feats = feat_ref[...]
    dot = functools.partial(jnp.dot, precision=HIGHEST, preferred_element_type=F32)
    h = jnp.sin(fr_ref[0:1, :] * (dot(feats, w1_ref[...]) + b1_ref[...]))
    h = jnp.sin(fr_ref[1:2, :] * (dot(h, w2_ref[...]) + b2_ref[...]))
    h = dot(h, w3_ref[...])
    filt = h * jnp.exp(-feats[:, 0:1] * jnp.abs(dec_ref[...]))
    row = pl.program_id(0) * tm + lax.broadcasted_iota(jnp.int32, filt.shape, 0)
    lane = lax.broadcasted_iota(jnp.int32, filt.shape, 1)
    o_ref[...] = jnp.where((row == 0) & (lane >= 2 * D_HY), 0.0, filt)


def _hy_features(L):
    pos = np.arange(L, dtype=np.float64)
    t = np.linspace(0.0, 1.0, L)
    bands = (POS_EMB_DIM - 1) // 2
    f = np.linspace(1e-4, bands - 1, bands)
    ang = (2.0 * math.pi / L) * pos[:, None] * f[None, :]
    feats = np.concatenate([t[:, None], np.cos(ang), -np.sin(ang)], axis=-1)
    out = np.zeros((L, LANES), np.float32)
    out[:, :POS_EMB_DIM] = feats
    return out


def _two_level(L):
    return L % (FFT_N1 * 8) == 0 and L >= 2048


def _digit_swap(x, L):
    n2 = 2 * L // FFT_N1
    return x.reshape(L // n2, n2, -1).transpose(1, 0, 2).reshape(L, -1)


def _hy_filters(L, w1, b1, w2, b2, w3, freq, decay):
    tm = min(L, 512)
    hid = w2.shape[0]
    n = w3.shape[1]
    w1p = jnp.zeros((LANES, hid), F32).at[:POS_EMB_DIM].set(w1)
    feats = _hy_features(L)
    if _two_level(L):
        feats = _digit_swap(feats, L)
    c = lambda s: pl.BlockSpec(s, lambda i: (0, 0))
    return pl.pallas_call(
        functools.partial(_hy_filter_kernel, tm=tm),
        grid=(L // tm,),
        in_specs=[pl.BlockSpec((tm, LANES), lambda i: (i, 0)),
                  c((LANES, hid)), c((1, hid)), c((hid, hid)), c((1, hid)), c((hid, n)), c((2, hid)), c((1, n))],
        out_specs=pl.BlockSpec((tm, n), lambda i: (i, 0)),
        out_shape=jax.ShapeDtypeStruct((L, n), F32),
        compiler_params=_cp("arbitrary"),
        name="hy_filter",
    )(jnp.asarray(feats), w1p, b1[None], w2, b2[None], w3, freq, decay.reshape(1, n))


def _conv3_kernel(v_ref, a_ref, b_ref, w_ref, cb_ref, ov_ref, oa_ref, ob_ref, xp_ref, xs_ref, *, L, tc, swap, n2):
    h1 = L // n2
    pitch = n2 + STRIDE_PAD
    for p, (src, dst) in enumerate(((v_ref, ov_ref), (a_ref, oa_ref), (b_ref, ob_ref))):
        _fill_padded(xp_ref, src, L)
        if swap[p]:
            for i1 in range(h1):
                xs_ref[pl.ds(i1 * pitch, n2 + HY_CONV - 1), :] = (
                    xp_ref[pl.ds(HALO + i1 * n2 - HY_CONV // 2, n2 + HY_CONV - 1), :])

            def body(i2, carry, p=p, dst=dst):
                y = cb_ref[p]
                for k in range(HY_CONV):
                    y = y + w_ref[p, k:k + 1, :] * xs_ref[pl.ds(i2 + k, h1, stride=pitch), :]
                dst[i2] = y
                return carry

            lax.fori_loop(0, n2, body, 0)
        else:
            for t0 in range(0, L, tc):
                y = cb_ref[p]
                for k in range(HY_CONV):
                    y = y + w_ref[p, k:k + 1, :] * xp_ref[pl.ds(HALO + t0 + k - HY_CONV // 2, tc), :]
                dst[pl.ds(t0, tc), :] = y


def _hy_conv3(v_t, a_t, b_t, conv_w, conv_b, swap=(False, False, False)):
    L, cols = v_t.shape
    nh = D_HY // LANES
    tc = min(L, 512)
    n2 = 2 * L // FFT_N1
    col = pl.BlockSpec((L, LANES), lambda c: (0, c))
    col3 = pl.BlockSpec((n2, L // n2, LANES), lambda c: (0, 0, c))
    shape = jax.ShapeDtypeStruct((L, cols), F32)
    shape3 = jax.ShapeDtypeStruct((n2, L // n2, cols), F32)
    w = conv_w.reshape(HY_CONV, 3, D_HY).transpose(1, 0, 2)
    cb = conv_b.reshape(3, 1, D_HY)
    return pl.pallas_call(
        functools.partial(_conv3_kernel, L=L, tc=tc, swap=swap, n2=n2),
        grid=(cols // LANES,),
        in_specs=[col, col, col,
                  pl.BlockSpec((3, HY_CONV, LANES), lambda c: (0, 0, c % nh)),
                  pl.BlockSpec((3, 1, LANES), lambda c: (0, 0, c % nh))],
        out_specs=[col3 if s else col for s in swap],
        out_shape=[shape3 if s else shape for s in swap],
        scratch_shapes=[pltpu.VMEM((L + 2 * HALO, LANES), F32),
                        pltpu.VMEM(((L // n2) * (n2 + STRIDE_PAD) + HALO, LANES), F32)],
        compiler_params=_cp("arbitrary"),
        name="hy_conv3",
    )(v_t, a_t, b_t, w, cb)


def _phase(p, n2):
    ang = 2.0 * np.pi * (np.asarray(p, np.int64) % n2) / n2
    return np.cos(ang), np.sin(ang)


def _dft_direct_tables(L):
    k = np.arange(L)[:, None]
    t = np.arange(L)[None, :]
    c, s = _phase((2 * k + 1) * t, 4 * L)
    fwd = np.concatenate([c, -s], axis=0)
    inv = (1.0 / L) * np.concatenate([c.T, -s.T], axis=1)
    return fwd.astype(np.float32), inv.astype(np.float32)


def _dft_two_level_tables(L, n1):
    N = 2 * L
    n2 = N // n1
    h1 = n1 // 2
    i2 = np.arange(n2)[:, None, None]
    k1 = np.arange(n1)[None, :, None]
    i1 = np.arange(h1)[None, None, :]
    c, s = _phase(2 * n2 * i1 * k1 + 2 * i2 * k1 + n2 * i1 + i2, 2 * N)
    t1 = np.concatenate([c, -s], axis=1)
    ct, st = c.transpose(0, 2, 1), s.transpose(0, 2, 1)
    t1i = (2.0 / N) * np.concatenate([ct, -st], axis=2)
    kk = np.arange(n2 // 2)[:, None]
    nn = np.arange(n2)[None, :]
    c2, s2 = _phase(nn * kk, n2)
    t2 = np.block([[c2, s2], [-s2, c2]])
    t2i = np.block([[c2.T, -s2.T], [s2.T, c2.T]])
    return tuple(x.astype(np.float32) for x in (t1, t2, t2i, t1i))


def _filter_spectrum(fre, fim, order, skip):
    fw = slice(order * D_HY, (order + 1) * D_HY)
    bw = slice((2 + order) * D_HY, (3 + order) * D_HY)
    return fre[:, fw] + fre[:, bw] + skip, fim[:, fw] - fim[:, bw]


def _spectral_product(xr, xi, fr, fi, nbc):
    yr, yi = [], []
    for b in range(nbc):
        sl = slice(b * D_HY, (b + 1) * D_HY)
        yr.append(xr[:, sl] * fr - xi[:, sl] * fi)
        yi.append(xr[:, sl] * fi + xi[:, sl] * fr)
    return jnp.concatenate([jnp.concatenate(yr, axis=1), jnp.concatenate(yi, axis=1)], axis=0)


def _matmul_kernel(a_ref, b_ref, o_ref):
    o_ref[...] = jnp.dot(a_ref[...], b_ref[...], precision=HIGHEST, preferred_element_type=F32)


def _hy_direct_kernel(z_ref, g_ref, f_ref, skip_ref, t_ref, ti_ref, o_ref, *, L, nbc, order):
    x = jnp.dot(t_ref[...], z_ref[...].astype(BF16), preferred_element_type=F32)
    fr, fi = _filter_spectrum(f_ref[0:L, :], f_ref[L:2 * L, :], order, skip_ref[0])
    y = _spectral_product(x[:L], x[L:], fr, fi, nbc).astype(BF16)
    conv = jnp.dot(ti_ref[...], y, preferred_element_type=F32)
    o_ref[...] = (g_ref[...] * conv).astype(o_ref.dtype)


def _hyena_direct(z, g1, g2, filt, skip):
    L, cols = z.shape
    fwd, inv = _dft_direct_tables(L)
    n = filt.shape[1]
    spec = pl.pallas_call(
        _matmul_kernel,
        out_shape=jax.ShapeDtypeStruct((2 * L, n), F32),
        name="hy_fspec",
    )(jnp.asarray(fwd), filt)
    tn = min(cols, 4 * D_HY)
    nbc = tn // D_HY
    col = pl.BlockSpec((L, tn), lambda c: (0, c))
    const = lambda s: pl.BlockSpec(s, lambda c: (0, 0))
    t_bf, ti_bf = jnp.asarray(fwd).astype(BF16), jnp.asarray(inv).astype(BF16)
    for order, g in enumerate((g1, g2)):
        z = pl.pallas_call(
            functools.partial(_hy_direct_kernel, L=L, nbc=nbc, order=order),
            grid=(cols // tn,),
            in_specs=[col, col, const((2 * L, n)), pl.BlockSpec((1, 1, D_HY), lambda c: (order, 0, 0)),
                      const((2 * L, L)), const((L, 2 * L))],
            out_specs=col,
            out_shape=jax.ShapeDtypeStruct((L, cols), F32 if order == 0 else BF16),
            compiler_params=_cp("arbitrary"),
            name="hy_direct",
        )(z, g, spec, skip[:, None, :], t_bf, ti_bf)
    return z


def _kron_tables(t1, t1i, G):
    n2, two_n1, h1 = t1.shape
    ng = n2 // G
    eye = np.eye(two_n1, dtype=np.float32)
    rep_rows = jnp.asarray(np.repeat(eye, G, axis=0))
    rep_cols = jnp.asarray(np.repeat(eye, G, axis=1))
    c_fwd = t1.reshape(ng, G, two_n1, h1).transpose(0, 2, 1, 3).reshape(ng, two_n1, G * h1)
    c_swapped = t1i.reshape(ng, G * h1, two_n1)
    c_natural = t1i.reshape(ng, G, h1, two_n1).transpose(0, 2, 1, 3).reshape(ng, h1 * G, two_n1)

    def expand(compact, rep, left, row_key, lane_key):
        r, c = (rep.shape[0], compact.shape[2]) if left else (compact.shape[1], rep.shape[1])
        return pl.pallas_call(
            functools.partial(_kron_expand_kernel, left=left, row_key=row_key, lane_key=lane_key),
            grid=(ng,),
            in_specs=[pl.BlockSpec((1,) + compact.shape[1:], lambda i: (i, 0, 0)),
                      pl.BlockSpec(rep.shape, lambda i: (0, 0))],
            out_specs=pl.BlockSpec((1, r, c), lambda i: (i, 0, 0)),
            out_shape=jax.ShapeDtypeStruct((ng, r, c), BF16),
            compiler_params=_cp("arbitrary"),
            name="hy_kron",
        )(jnp.asarray(compact), rep)

    fwd = expand(c_fwd, rep_rows, True, lambda r: r % G, lambda c: c // h1)
    inv_swapped = expand(c_swapped, rep_cols, False, lambda r: r // h1, lambda c: c % G)
    inv_natural = expand(c_natural, rep_cols, False, lambda r: r % G, lambda c: c % G)
    return fwd, inv_swapped, inv_natural


def _kron_expand_kernel(c_ref, rep_ref, o_ref, *, left, row_key, lane_key):
    comp = c_ref[0].astype(BF16)
    rep = rep_ref[...].astype(BF16)
    full = (jnp.dot(rep, comp, preferred_element_type=F32) if left
            else jnp.dot(comp, rep, preferred_element_type=F32))
    rows = lax.broadcasted_iota(jnp.int32, full.shape, 0)
    lanes = lax.broadcasted_iota(jnp.int32, full.shape, 1)
    o_ref[0] = jnp.where(row_key(rows) == lane_key(lanes), full, 0.0).astype(BF16)


def _hy_stage1_kernel(x_ref, t_ref, re_ref, im_ref, *, G, n1, ct):
    h1, cols = x_ref.shape[1], x_ref.shape[2]
    x = x_ref[...].reshape(G * h1, cols).astype(BF16)
    for c0 in range(0, cols, ct):
        a = jnp.dot(t_ref[0], x[:, c0:c0 + ct], preferred_element_type=F32)
        re_ref[:, :, c0:c0 + ct] = a[:n1 * G].reshape(n1, G, ct).astype(BF16)
        im_ref[:, :, c0:c0 + ct] = a[n1 * G:].reshape(n1, G, ct).astype(BF16)


def _hy_stage1(x3, t_fwd, *, G):
    n2, h1, cols = x3.shape
    n1 = t_fwd.shape[1] // (2 * G)
    ospec = pl.BlockSpec((n1, G, cols), lambda j: (0, j, 0))
    oshape = jax.ShapeDtypeStruct((n1, n2, cols), BF16)
    return pl.pallas_call(
        functools.partial(_hy_stage1_kernel, G=G, n1=n1, ct=D_HY),
        grid=(n2 // G,),
        in_specs=[pl.BlockSpec((G, h1, cols), lambda j: (j, 0, 0)),
                  pl.BlockSpec((1,) + t_fwd.shape[1:], lambda j: (j, 0, 0))],
        out_specs=[ospec, ospec],
        out_shape=[oshape, oshape],
        compiler_params=_cp("arbitrary"),
        name="hy_stage1",
    )(x3, t_fwd)


def _hy_fspec2_kernel(re_ref, im_ref, t2_ref, fre_ref, fim_ref, *, kb, k2):
    for j in range(kb):
        a2 = jnp.concatenate([re_ref[j], im_ref[j]], axis=0)
        x = jnp.dot(t2_ref[...], a2, preferred_element_type=F32)
        fre_ref[j] = x[:k2]
        fim_ref[j] = x[k2:]


def _hy_stage2_kernel(re_ref, im_ref, fre_ref, fim_ref, skip_ref, t2_ref, t2i_ref, ore_ref, oim_ref,
                      *, kb, k2, n2, nbc, order):
    for j in range(kb):
        a2 = jnp.concatenate([re_ref[j], im_ref[j]], axis=0)
        x = jnp.dot(t2_ref[...], a2, preferred_element_type=F32)
        fr, fi = _filter_spectrum(fre_ref[j], fim_ref[j], order, skip_ref[0])
        y = _spectral_product(x[:k2], x[k2:], fr, fi, nbc).astype(BF16)
        a = jnp.dot(t2i_ref[...], y, preferred_element_type=F32)
        ore_ref[j] = a[:n2].astype(BF16)
        oim_ref[j] = a[n2:].astype(BF16)


def _hy_stage3_kernel(re_ref, im_ref, t_ref, g_ref, o_ref, *, G, n1, ct):
    cols = re_ref.shape[2]
    for c0 in range(0, cols, ct):
        sl = slice(c0, c0 + ct)
        a2 = jnp.concatenate([re_ref[:, :, sl].reshape(n1 * G, ct), im_ref[:, :, sl].reshape(n1 * G, ct)], axis=0)
        conv = jnp.dot(t_ref[0], a2, preferred_element_type=F32)
        g = g_ref[:, :, sl]
        o_ref[:, :, sl] = (g * conv.reshape(g.shape)).astype(o_ref.dtype)


def _two_level_tables(L, *, n1=FFT_N1, G=FFT_G):
    t1, t2, t2i, t1i = _dft_two_level_tables(L, n1)
    return (jnp.asarray(t2).astype(BF16), jnp.asarray(t2i).astype(BF16)) + _kron_tables(t1, t1i, G)


def _hyena_two_level(z3, g1_3, g2, filt, skip, tables, *, n1=FFT_N1, G=FFT_G, kb=4):
    n2, h1, cols = z3.shape
    L = n2 * h1
    k2 = n2 // 2
    nbc = cols // D_HY
    t2_bf, t2i_bf, t_fwd, t_inv_swapped, t_inv_natural = tables
    nf = filt.shape[1]

    f_re, f_im = _hy_stage1(filt.reshape(n2, h1, nf), t_fwd, G=G)
    blk = lambda c: pl.BlockSpec((kb, n2, c), lambda i: (i, 0, 0))
    hblk = lambda c: pl.BlockSpec((kb, k2, c), lambda i: (i, 0, 0))
    const = lambda s: pl.BlockSpec(s, lambda i: (0, 0))
    fshape = jax.ShapeDtypeStruct((n1, k2, nf), F32)
    fre, fim = pl.pallas_call(
        functools.partial(_hy_fspec2_kernel, kb=kb, k2=k2),
        grid=(n1 // kb,),
        in_specs=[blk(nf), blk(nf), const((n2, 2 * n2))],
        out_specs=[hblk(nf), hblk(nf)],
        out_shape=[fshape, fshape],
        compiler_params=_cp("arbitrary"),
        name="hy_fspec2",
    )(f_re, f_im, t2_bf)

    ashape = jax.ShapeDtypeStruct((n1, n2, cols), BF16)
    grp = pl.BlockSpec((n1, G, cols), lambda j: (0, j, 0))
    tspec = lambda t: pl.BlockSpec((1,) + t.shape[1:], lambda j: (j, 0, 0))
    for order in range(2):
        a_re, a_im = _hy_stage1(z3, t_fwd, G=G)
        b_re, b_im = pl.pallas_call(
            functools.partial(_hy_stage2_kernel, kb=kb, k2=k2, n2=n2, nbc=nbc, order=order),
            grid=(n1 // kb,),
            in_specs=[blk(cols), blk(cols), hblk(nf), hblk(nf),
                      pl.BlockSpec((1, 1, D_HY), lambda i: (order, 0, 0)),
                      const((n2, 2 * n2)), const((2 * n2, n2))],
            out_specs=[blk(cols), blk(cols)],
            out_shape=[ashape, ashape],
            compiler_params=_cp("arbitrary"),
            name="hy_stage2",
        )(a_re, a_im, fre, fim, skip[:, None, :], t2_bf, t2i_bf)
        if order == 0:
            table, gate = t_inv_swapped, g1_3
            io = pl.BlockSpec((G, h1, cols), lambda j: (j, 0, 0))
            oshape = jax.ShapeDtypeStruct((n2, h1, cols), F32)
        else:
            table, gate = t_inv_natural, g2.reshape(h1, n2, cols)
            io = pl.BlockSpec((h1, G, cols), lambda j: (0, j, 0))
            oshape = jax.ShapeDtypeStruct((h1, n2, cols), BF16)
        z3 = pl.pallas_call(
            functools.partial(_hy_stage3_kernel, G=G, n1=n1, ct=D_HY),
            grid=(n2 // G,),
            in_specs=[grp, grp, tspec(table), io],
            out_specs=io,
            out_shape=oshape,
            compiler_params=_cp("arbitrary"),
            name="hy_stage3",
        )(b_re, b_im, table, gate)
    return z3.reshape(L, cols)


def _hyena(hv_t, h1_t, h2_t, conv_w, conv_b, filt, skip, tables=None):
    L, cols = hv_t.shape
    if _two_level(L):
        z3, g1_3, g2 = _hy_conv3(hv_t, h1_t, h2_t, conv_w, conv_b, swap=(True, True, False))
        return _hyena_two_level(z3, g1_3, g2, filt, skip, tables or _two_level_tables(L))
    z, g1, g2 = _hy_conv3(hv_t, h1_t, h2_t, conv_w, conv_b)
    return _hyena_direct(z, g1, g2, filt, skip)


def _route(logits):
    lane = lax.broadcasted_iota(jnp.int32, logits.shape, 1)
    is_g = (lane >= N_EXPERTS) & (lane < N_EXPERTS + N_GROUPS)
    big = jnp.int32(1 << 20)
    gl = jnp.where(is_g, logits, NEG)
    gmax = jnp.max(gl, axis=-1, keepdims=True)
    gidx = jnp.min(jnp.where(gl == gmax, lane - N_EXPERTS, big), axis=-1, keepdims=True)
    p_group = 1.0 / jnp.sum(jnp.where(is_g, jnp.exp(gl - gmax), 0.0), axis=-1, keepdims=True)
    sel = (lane < N_EXPERTS) & ((lane // EXPERTS_PER_GROUP) == gidx)
    el = jnp.where(sel, logits, NEG)
    ee = jnp.where(sel, jnp.exp(el - jnp.max(el, axis=-1, keepdims=True)), 0.0)
    pe = ee / jnp.sum(ee, axis=-1, keepdims=True)
    p1 = jnp.max(jnp.where(sel, pe, -1.0), axis=-1, keepdims=True)
    i1 = jnp.min(jnp.where(sel & (pe == p1), lane, big), axis=-1, keepdims=True)
    rest = sel & (lane != i1)
    p2 = jnp.max(jnp.where(rest, pe, -1.0), axis=-1, keepdims=True)
    i2 = jnp.min(jnp.where(rest & (pe == p2), lane, big), axis=-1, keepdims=True)
    den = p1 + p2
    cmb = jnp.where(lane == i1, p_group * (p1 / den), jnp.where(lane == i2, p_group * (p2 / den), 0.0))
    return jnp.where(lane == N_EXPERTS, gidx.astype(F32), cmb)


def _outproj_kernel(att_ref, hf_ref, hb_ref, gt_ref, hy_ref, x_ref, mod_ref, w_ref, g_ref, b_ref,
                    wr_ref, br_ref, x1_ref, u2_ref, cmb_ref):
    rg = (hf_ref[...] + hb_ref[...]) * jax.nn.gelu(gt_ref[...])
    mix = (jnp.dot(att_ref[0], w_ref[0:D_ATT, :], preferred_element_type=F32)
           + jnp.dot(rg.astype(BF16), w_ref[D_ATT:D_ATT + D_RNN, :], preferred_element_type=F32)
           + jnp.dot(hy_ref[...].astype(BF16), w_ref[D_ATT + D_RNN:, :], preferred_element_type=F32))
    m = mod_ref[0]
    x1 = _ln(ALPHA * x_ref[...] + m[2:3] * mix) * g_ref[...] + b_ref[...]
    u2 = _ln(x1) * (1.0 + m[4:5]) + m[3:4]
    u_hi = u2.astype(BF16)
    u_lo = (u2 - u_hi.astype(F32)).astype(BF16)
    wr = wr_ref[...]
    w_hi = wr.astype(BF16)
    w_lo = (wr - w_hi.astype(F32)).astype(BF16)
    hi = jnp.dot(u_hi, jnp.concatenate([w_hi, w_lo], axis=1), preferred_element_type=F32)
    lo = jnp.dot(u_lo, w_hi, preferred_element_type=F32)
    logits = hi[:, 0:LANES] + hi[:, LANES:] + lo + br_ref[...]
    x1_ref[...] = x1
    u2_ref[...] = u_hi
    cmb_ref[...] = _route(logits)


def _outproj(att, hf_t, hb_t, gt_t, hy_t, x2d, mod_l, w_bf, ln_g, ln_b, wr, br, *, nb, L, tm,
             cond_base, cond_per_batch):
    tpb = L // tm
    d = x2d.shape[1]
    cond = (lambda i: cond_base + i // tpb) if cond_per_batch else (lambda i: cond_base)
    t_spec = pl.BlockSpec((tm, D_RNN), lambda i: (i % tpb, i // tpb))
    tok = lambda c: pl.BlockSpec((tm, c), lambda i: (i, 0))
    const = lambda s: pl.BlockSpec(s, lambda i: (0, 0))
    n = nb * L
    return pl.pallas_call(
        _outproj_kernel,
        grid=(nb * tpb,),
        in_specs=[pl.BlockSpec((1, tm, D_ATT), lambda i: (i // tpb, i % tpb, 0)),
                  t_spec, t_spec, t_spec, t_spec, tok(d),
                  pl.BlockSpec((1, 6, d), lambda i: (cond(i), 0, 0)),
                  const((d, d)), const((1, d)), const((1, d)), const((d, LANES)), const((1, LANES))],
        out_specs=[tok(d), tok(d), tok(LANES)],
        out_shape=[jax.ShapeDtypeStruct((n, d), F32), jax.ShapeDtypeStruct((n, d), BF16),
                   jax.ShapeDtypeStruct((n, LANES), F32)],
        compiler_params=_cp("arbitrary"),
        name="outproj",
    )(att, hf_t, hb_t, gt_t, hy_t, x2d, mod_l, w_bf, ln_g[None], ln_b[None], wr, br)


def _moe_kernel(u_ref, cmb_ref, x1_ref, mod_ref, wgu_ref, wd_ref, g_ref, b_ref, o_ref, ys_ref, *, T, BLK):
    SL = T + N_GROUPS * BLK
    u = u_ref[...]
    cmb = cmb_ref[...]
    lane_f = lax.broadcasted_iota(jnp.int32, (T, LANES), 1).astype(F32)
    gid_col = jnp.sum(jnp.where(lane_f == float(N_EXPERTS), cmb, 0.0), axis=-1, keepdims=True)
    gid_row = cmb.T[N_EXPERTS:N_EXPERTS + 1, :]
    sub = lax.broadcasted_iota(jnp.int32, (8, T), 0).astype(F32)
    onehot = jnp.where(lane_f == gid_col, 1.0, 0.0)
    onehot_t = jnp.where(sub == gid_row, 1.0, 0.0)
    ri = lax.broadcasted_iota(jnp.int32, (T, T), 0)
    ci = lax.broadcasted_iota(jnp.int32, (T, T), 1)
    before = jnp.where(ci < ri, 1.0, 0.0).astype(BF16)
    before_t = jnp.where(ri < ci, 1.0, 0.0).astype(BF16)
    rank_col = jnp.sum(onehot * jnp.dot(before, onehot.astype(BF16), preferred_element_type=F32),
                       axis=-1, keepdims=True)
    rank_row = jnp.sum(onehot_t * jnp.dot(onehot_t.astype(BF16), before_t, preferred_element_type=F32),
                       axis=0, keepdims=True)
    counts = jnp.sum(onehot, axis=0, keepdims=True)
    lane_row = lax.broadcasted_iota(jnp.int32, (1, LANES), 1)
    sub_col = lax.broadcasted_iota(jnp.int32, (8, 1), 0)
    base = jnp.int32(0)
    bases = []
    base_row = jnp.zeros((1, LANES), jnp.int32)
    base_sub = jnp.zeros((8, 1), jnp.int32)
    for c in range(N_GROUPS):
        n_c = jnp.sum(jnp.where(lane_row == c, counts, 0.0)).astype(jnp.int32)
        bases.append(base)
        base_row = jnp.where(lane_row == c, base, base_row)
        base_sub = jnp.where(sub_col == c, base, base_sub)
        base = base + ((n_c + (BLK - 1)) // BLK) * BLK
    total = base
    dest_col = (jnp.sum(onehot * base_row.astype(F32), axis=-1, keepdims=True) + rank_col).astype(jnp.int32)
    dest_row = (jnp.sum(onehot_t * base_sub.astype(F32), axis=0, keepdims=True) + rank_row).astype(jnp.int32)
    cmb_hi = cmb.astype(BF16)
    cmb_lo = (cmb - cmb_hi.astype(F32)).astype(BF16)
    blk_rows = lax.broadcasted_iota(jnp.int32, (BLK, T), 0)
    blk_lane = lax.broadcasted_iota(jnp.int32, (BLK, LANES), 1)

    def block(b, carry):
        r0 = b * BLK
        rows = pl.ds(pl.multiple_of(r0, BLK), BLK)

        @pl.when(r0 < total)
        def _():
            grp = ((r0 >= bases[1]).astype(jnp.int32) + (r0 >= bases[2]).astype(jnp.int32)
                   + (r0 >= bases[3]).astype(jnp.int32))
            p = jnp.where(blk_rows + r0 == dest_row, 1.0, 0.0).astype(BF16)
            xb = jnp.dot(p, u, preferred_element_type=F32).astype(BF16)
            wb = (jnp.dot(p, cmb_hi, preferred_element_type=F32)
                  + jnp.dot(p, cmb_lo, preferred_element_type=F32))
            gu = jnp.dot(xb, wgu_ref[grp], preferred_element_type=F32)
            hs = []
            for j in range(EXPERTS_PER_GROUP):
                gate = gu[:, 2 * j * D_EXPERT:(2 * j + 1) * D_EXPERT]
                up = gu[:, (2 * j + 1) * D_EXPERT:(2 * j + 2) * D_EXPERT]
                w = jnp.sum(jnp.where(blk_lane == grp * EXPERTS_PER_GROUP + j, wb, 0.0), axis=-1, keepdims=True)
                hs.append((gate * _sigmoid(gate) * up * w).astype(BF16))
            ys_ref[rows, :] = jnp.dot(jnp.concatenate(hs, axis=1), wd_ref[grp],
                                      preferred_element_type=F32).astype(BF16)

        @pl.when(r0 >= total)
        def _():
            ys_ref[rows, :] = jnp.zeros((BLK, u.shape[1]), BF16)

        return carry

    lax.fori_loop(0, SL // BLK, block, 0)
    p_t = jnp.where(lax.broadcasted_iota(jnp.int32, (T, SL), 1) == dest_col, 1.0, 0.0).astype(BF16)
    moe = jnp.dot(p_t, ys_ref[...], preferred_element_type=F32)
    m = mod_ref[0]
    o_ref[...] = _ln(ALPHA * x1_ref[...] + m[5:6] * moe) * g_ref[...] + b_ref[...]


def _group_gate_up(w_gate, w_up):
    _, d, f = w_gate.shape
    gu = jnp.concatenate([w_gate, w_up], axis=-1).astype(BF16).reshape(N_GROUPS, EXPERTS_PER_GROUP, d, 2 * f)
    return gu.transpose(0, 2, 1, 3).reshape(N_GROUPS, d, EXPERTS_PER_GROUP * 2 * f)


def _moe(u2, cmb, x1, mod_l, wgu, wd, ln_g, ln_b, *, tm, tiles_per_cond, cond_base, cond_per_batch):
    n, d = x1.shape
    cond = (lambda i: cond_base + i // tiles_per_cond) if cond_per_batch else (lambda i: cond_base)
    tok = lambda c: pl.BlockSpec((tm, c), lambda i: (i, 0))
    const = lambda s: pl.BlockSpec(s, lambda i: (0,) * len(s))
    resident = lambda s: pl.BlockSpec(s, lambda i: (0,) * len(s), pipeline_mode=pl.Buffered(1))
    return pl.pallas_call(
        functools.partial(_moe_kernel, T=tm, BLK=MOE_BLK),
        grid=(n // tm,),
        in_specs=[tok(d), tok(LANES), tok(d),
                  pl.BlockSpec((1, 6, d), lambda i: (cond(i), 0, 0)),
                  resident(wgu.shape), resident(wd.shape),
                  const((1, d)), const((1, d))],
        out_specs=tok(d),
        out_shape=jax.ShapeDtypeStruct((n, d), F32),
        scratch_shapes=[pltpu.VMEM((tm + N_GROUPS * MOE_BLK, d), BF16)],
        compiler_params=_cp("arbitrary"),
        name="moe",
    )(u2, cmb, x1, mod_l, wgu, wd, ln_g[None], ln_b[None])


def _stream_layer(x2d, mod_l, p, *, nb, L, tm, cond_base, cond_per_batch, ctx, cache=None, h0=None, prev_kv=None):
    kw = dict(nb=nb, L=L, tm=tm, cond_base=cond_base, cond_per_batch=cond_per_batch)
    q, k, v, xr_t, gt_t, hv_t, h1_t, h2_t = _inproj(x2d, mod_l, p["w_in"], kv_dtype=F32 if ctx else BF16,
                                                    prev_kv=prev_kv, **kw)
    if ctx:
        att = _ctx_attention(q, k, v)
    else:
        att = _nbr_attention(q, k, v, cache[0], cache[1], p["nbr_bias"])
    a_f, b_f, a_b, b_b = _rg_gates(xr_t, p["rg_conv_w"], p["rg_conv_b"], *p["rg_w"])
    hf_t, hb_t = _rg_scan(a_f, b_f, a_b, b_b, h0[0], h0[1])
    hy_t = _hyena(hv_t, h1_t, h2_t, p["hy_conv_w"], p["hy_conv_b"], p["hy_filt"][L], p["hy_skip"],
                  p["hy_tables"].get(L))
    x1, u2, cmb = _outproj(att, hf_t, hb_t, gt_t, hy_t, x2d, mod_l, p["w_out"], p["ln1_g"], p["ln1_b"],
                           p["router_w"], p["router_b"], **kw)
    tm_moe = min(MOE_TILE, nb * L)
    x2 = _moe(u2, cmb, x1, mod_l, p["moe_wgu"], p["moe_wd"], p["ln2_g"], p["ln2_b"], tm=tm_moe,
              tiles_per_cond=max(L // tm_moe, 1), cond_base=cond_base, cond_per_batch=cond_per_batch)
    return x2, k, v, hf_t, hb_t


def kernel(x_prompt, x_sample, cache_k, cache_v, state_rglru, c, c_ctx, w_mod, b_mod, w_in, attn_rpb, rg_conv_w, rg_conv_b, rg_wa, rg_ba, rg_wx, rg_bx, rg_lambda, hy_conv_w, hy_conv_b, hy_w1, hy_b1, hy_w2, hy_b2, hy_w3, hy_freq, hy_decay, hy_skip, w_out, ln1_g, ln1_b, router_wg, router_bg, router_we, router_be, moe_w_gate, moe_w_up, moe_w_down, ln2_g, ln2_b):
    nbc, Lc, d = x_prompt.shape
    nbl, Ll, _ = x_sample.shape
    depth = w_mod.shape[0]
    past = cache_k.shape[2]

    cond8 = jnp.zeros((8, d), F32).at[0].set(c_ctx).at[1:1 + nbl].set(c)
    mod = _modulation(cond8, w_mod, b_mod).reshape(depth, 8, 6, d)

    xp = x_prompt.reshape(nbc * Lc, d)
    xs = x_sample.reshape(nbl * Ll, d)
    kv, new_h = None, []
    hy_tables = {L: _two_level_tables(L) for L in {Lc, Ll} if _two_level(L)}
    for l in range(depth):
        pad = jnp.zeros((d, LANES - N_EXPERTS - N_GROUPS), F32)
        p = {
            "w_in": w_in[l].astype(BF16),
            "nbr_bias": _nbr_bias_tiles(attn_rpb[l]),
            "rg_conv_w": rg_conv_w[l], "rg_conv_b": rg_conv_b[l],
            "rg_w": _rg_weights(rg_wa[l], rg_ba[l], rg_wx[l], rg_bx[l], rg_lambda[l]),
            "hy_conv_w": hy_conv_w[l], "hy_conv_b": hy_conv_b[l], "hy_skip": hy_skip[l], "hy_tables": hy_tables,
            "hy_filt": {L: _hy_filters(L, hy_w1[l], hy_b1[l], hy_w2[l], hy_b2[l], hy_w3[l], hy_freq[l], hy_decay[l])
                        for L in sorted({Lc, Ll})},
            "w_out": w_out[l].astype(BF16), "ln1_g": ln1_g[l], "ln1_b": ln1_b[l],
            "router_w": jnp.concatenate([router_we[l], router_wg[l], pad], axis=1),
            "router_b": jnp.concatenate([router_be[l], router_bg[l], pad[0]])[None],
            "moe_wgu": _group_gate_up(moe_w_gate[l], moe_w_up[l]),
            "moe_wd": moe_w_down[l].astype(BF16).reshape(N_GROUPS, EXPERTS_PER_GROUP * D_EXPERT, d),
            "ln2_g": ln2_g[l], "ln2_b": ln2_b[l],
        }
        zeros_h = jnp.zeros((1, nbc * D_RNN), F32)
        xp, k, v, hf_t, hb_t = _stream_layer(
            xp, mod[l], p, nb=nbc, L=Lc, tm=Lc, cond_base=0, cond_per_batch=False, ctx=True,
            h0=(zeros_h, zeros_h), prev_kv=kv)
        kv = (k, v)
        new_h.append(jnp.stack([hf_t[Lc - 1].reshape(nbc, D_RNN), hb_t[0].reshape(nbc, D_RNN)], axis=1))

        cache = (cache_k[:, l].reshape(nbl, past, D_ATT), cache_v[:, l].reshape(nbl, past, D_ATT))
        h0 = (state_rglru[:, l, 0].reshape(1, nbl * D_RNN), state_rglru[:, l, 1].reshape(1, nbl * D_RNN))
        xs, _, _, _, _ = _stream_layer(
            xs, mod[l], p, nb=nbl, L=Ll, tm=min(Ll, 512), cond_base=1, cond_per_batch=True, ctx=False,
            cache=cache, h0=h0)

    return (xp.reshape(nbc, Lc, d), xs.reshape(nbl, Ll, d),
            kv[0].reshape(nbc, depth, Lc, N_HEADS, HEAD_DIM), kv[1].reshape(nbc, depth, Lc, N_HEADS, HEAD_DIM),
            jnp.stack(new_h, axis=1))
```

```python
import functools
import math

import numpy as np
import jax
import jax.numpy as jnp
from jax import lax
from jax.experimental import pallas as pl
from jax.experimental.pallas import tpu as pltpu

F32 = jnp.float32
BF16 = jnp.bfloat16
HIGHEST = lax.Precision.HIGHEST

DEPTH = 2
D_MODEL = 1024
N_HEADS = 8
HEAD_DIM = 64
D_ATT = N_HEADS * HEAD_DIM
D_RNN = 256
D_HY = 256
GRID_W = 64
WIN_R = 8
WIN_C = 16
RG_CONV = 4
RG_C = 8.0
HY_CONV = 3
POS_EMB_DIM = 33
N_GROUPS = 4
EXPERTS_PER_GROUP = 4
N_EXPERTS = 16
D_EXPERT = 256
ALPHA = (2.0 * DEPTH) ** 0.25
LN_EPS = 1e-5
NEG = -1e30

Q_ROWS = 4
K_ROWS = Q_ROWS + WIN_R
LANES = 128
HALO = 8
MOE_TILE = 512
MOE_BLK = 128
STRIDE_PAD = 4
FFT_G = 16
FFT_N1 = 64
VMEM_LIMIT = 52 * 1024 * 1024


def _cp(*sem):
    return pltpu.CompilerParams(dimension_semantics=sem, vmem_limit_bytes=VMEM_LIMIT)


def _sigmoid(x):
    return 0.5 * jnp.tanh(0.5 * x) + 0.5


def _ln(x):
    mu = jnp.mean(x, axis=-1, keepdims=True)
    xc = x - mu
    var = jnp.mean(xc * xc, axis=-1, keepdims=True)
    return xc * lax.rsqrt(var + LN_EPS)


def _mod_kernel(c_ref, w_ref, b_ref, o_ref):
    c = c_ref[...]
    s = c * jax.nn.sigmoid(c)
    o_ref[0] = jnp.dot(s, w_ref[0], precision=HIGHEST, preferred_element_type=F32) + b_ref[0]


def _modulation(cond8, w_mod, b_mod):
    depth, d, n = w_mod.shape
    tn = n // 4
    return pl.pallas_call(
        _mod_kernel,
        grid=(depth, n // tn),
        in_specs=[pl.BlockSpec((8, d), lambda l, j: (0, 0)),
                  pl.BlockSpec((1, d, tn), lambda l, j: (l, 0, j)),
                  pl.BlockSpec((1, 1, tn), lambda l, j: (l, 0, j))],
        out_specs=pl.BlockSpec((1, 8, tn), lambda l, j: (l, 0, j)),
        out_shape=jax.ShapeDtypeStruct((depth, 8, n), F32),
        compiler_params=_cp("arbitrary", "arbitrary"),
        name="modulation",
    )(cond8, w_mod, b_mod.reshape(depth, 1, n))


def _inproj_kernel(x_ref, mod_ref, w_ref, *refs, n_prev):
    if n_prev:
        pk_ref, pv_ref = refs[:2]
        refs = refs[2:]
    q_ref, k_ref, v_ref, xr_ref, gt_ref, hv_ref, h1_ref, h2_ref = refs
    for i in range(n_prev):
        k_ref[0, i] = pk_ref[0, i]
        v_ref[0, i] = pv_ref[0, i]
    m = mod_ref[0]
    u = (_ln(x_ref[...]) * (1.0 + m[1:2]) + m[0:1]).astype(BF16)

    def proj(lo, hi):
        return jnp.dot(u, w_ref[:, lo:hi], preferred_element_type=F32)

    o = 0
    q_ref[0] = (proj(o, o + D_ATT) * (HEAD_DIM ** -0.5)).astype(q_ref.dtype)
    o += D_ATT
    k_ref[0, n_prev] = proj(o, o + D_ATT).astype(k_ref.dtype)
    o += D_ATT
    v_ref[0, n_prev] = proj(o, o + D_ATT).astype(v_ref.dtype)
    o += D_ATT
    for ref in (xr_ref, gt_ref, hv_ref, h1_ref, h2_ref):
        ref[...] = proj(o, o + D_RNN)
        o += D_RNN


def _inproj(x2d, mod_l, w_bf, *, nb, L, tm, cond_base, cond_per_batch, kv_dtype, prev_kv=None):
    tpb = L // tm
    d = x2d.shape[1]
    d_in = w_bf.shape[1]
    n_prev = prev_kv[0].shape[1] if prev_kv else 0
    cond = (lambda i: cond_base + i // tpb) if cond_per_batch else (lambda i: cond_base)
    att_spec = pl.BlockSpec((1, tm, D_ATT), lambda i: (i // tpb, i % tpb, 0))
    kv_spec = lambda n: pl.BlockSpec((1, n, tm, D_ATT), lambda i: (i // tpb, 0, i % tpb, 0))
    t_spec = pl.BlockSpec((tm, D_RNN), lambda i: (i % tpb, i // tpb))
    kv_shape = jax.ShapeDtypeStruct((nb, n_prev + 1, L, D_ATT), kv_dtype)
    t_shape = jax.ShapeDtypeStruct((L, nb * D_RNN), F32)
    return pl.pallas_call(
        functools.partial(_inproj_kernel, n_prev=n_prev),
        grid=(nb * tpb,),
        in_specs=[pl.BlockSpec((tm, d), lambda i: (i, 0)),
                  pl.BlockSpec((1, 6, d), lambda i: (cond(i), 0, 0)),
                  pl.BlockSpec((d, d_in), lambda i: (0, 0))] + ([kv_spec(n_prev)] * 2 if n_prev else []),
        out_specs=[att_spec, kv_spec(n_prev + 1), kv_spec(n_prev + 1), t_spec, t_spec, t_spec, t_spec, t_spec],
        out_shape=[jax.ShapeDtypeStruct((nb, L, D_ATT), BF16), kv_shape, kv_shape,
                   t_shape, t_shape, t_shape, t_shape, t_shape],
        compiler_params=_cp("arbitrary"),
        name="inproj",
    )(x2d, mod_l, w_bf, *(prev_kv or ()))


def _nt(a, b):
    return lax.dot_general(a, b, (((1,), (1,)), ((), ())), preferred_element_type=F32)


def _ctx_attn_kernel(q_ref, k_ref, v_ref, o_ref):
    q = q_ref[0]
    k = k_ref[0, 0].astype(BF16)
    v = v_ref[0, 0].astype(BF16)
    outs = []
    for h in range(N_HEADS):
        sl = slice(h * HEAD_DIM, (h + 1) * HEAD_DIM)
        s = _nt(q[:, sl], k[:, sl])
        e = jnp.exp(s - jnp.max(s, axis=-1, keepdims=True))
        den = jnp.sum(e, axis=-1, keepdims=True)
        outs.append(jnp.dot(e.astype(BF16), v[:, sl], preferred_element_type=F32) / den)
    o_ref[0] = jnp.concatenate(outs, axis=-1).astype(o_ref.dtype)


def _ctx_attention(q, k, v):
    nb, L, _ = q.shape
    last = k.shape[1] - 1
    spec = pl.BlockSpec((1, L, D_ATT), lambda b: (b, 0, 0))
    kv_spec = pl.BlockSpec((1, 1, L, D_ATT), lambda b: (b, last, 0, 0))
    return pl.pallas_call(
        _ctx_attn_kernel,
        grid=(nb,),
        in_specs=[spec, kv_spec, kv_spec],
        out_specs=spec,
        out_shape=jax.ShapeDtypeStruct((nb, L, D_ATT), BF16),
        compiler_params=_cp("arbitrary"),
        name="ctx_attn",
    )(q, k, v)


N_DR = 2 * WIN_R - 1
PAIR_LEFT = N_DR - 1
PAIR_RIGHT = 2 * N_DR - 1
PAIR_NONE = 3 * N_DR - 1


def _nbr_bias_tiles(rpb):
    cols = np.arange(GRID_W)
    c_start = np.clip(cols - WIN_C // 2, 0, GRID_W - WIN_C)
    col_mask = (cols[None, :] >= c_start[:, None]) & (cols[None, :] < c_start[:, None] + WIN_C)
    dc_idx = np.clip(cols[None, :] - cols[:, None], -(WIN_C - 1), WIN_C - 1) + (WIN_C - 1)
    sel = (dc_idx[None] == np.arange(2 * WIN_C - 1)[:, None, None]).astype(np.float32)
    colb = jnp.einsum('hab,bqk->haqk', rpb.astype(F32), jnp.asarray(sel), precision=HIGHEST)
    colb = jnp.where(col_mask[None, None], colb, NEG)
    neg = jnp.full_like(colb, NEG)
    both = jnp.concatenate([colb[:, :-1], colb[:, 1:]], axis=-1)
    left_masked = jnp.concatenate([neg, colb], axis=-1)
    right_masked = jnp.concatenate([colb, neg], axis=-1)
    none = jnp.concatenate([neg[:, :1], neg[:, :1]], axis=-1)
    return jnp.concatenate([both, left_masked, right_masked, none], axis=1)


def _nbr_attn_kernel(q_ref, k_ref, v_ref, kc_ref, vc_ref, bias_ref, o_ref, *, rows):
    g = pl.program_id(1)
    r0 = jnp.clip(Q_ROWS * g - WIN_R // 2, 0, rows - K_ROWS)
    start = pl.multiple_of(r0 * GRID_W, Q_ROWS * GRID_W)
    nk = K_ROWS * GRID_W
    kw = k_ref[0, 0, pl.ds(start, nk), :]
    vw = v_ref[0, 0, pl.ds(start, nk), :]
    kc = kc_ref[0].astype(BF16)
    vc = vc_ref[0].astype(BF16)
    q = q_ref[0]
    tile = []
    for j in range(Q_ROWS):
        r = Q_ROWS * g + j
        lo = jnp.clip(r - WIN_R // 2, 0, rows - WIN_R) - r + (WIN_R - 1)
        tile.append([])
        for p in range(K_ROWS // 2):
            d0 = r0 + 2 * p - r + (WIN_R - 1)
            lv = (d0 >= lo) & (d0 < lo + WIN_R)
            rv = (d0 + 1 >= lo) & (d0 + 1 < lo + WIN_R)
            idx = jnp.where(lv & rv, d0, jnp.where(rv, PAIR_LEFT + d0 + 1, jnp.where(lv, PAIR_RIGHT + d0, PAIR_NONE)))
            tile[j].append(jnp.clip(idx, 0, PAIR_NONE))
    outs = []
    for h in range(N_HEADS):
        sl = slice(h * HEAD_DIM, (h + 1) * HEAD_DIM)
        qh = q[:, sl]
        bias = jnp.concatenate(
            [jnp.concatenate([bias_ref[h, tile[j][p]] for p in range(K_ROWS // 2)], axis=-1)
             for j in range(Q_ROWS)], axis=0)
        s1 = _nt(qh, kw[:, sl]) + bias
        s2 = _nt(qh, kc[:, sl])
        m = jnp.maximum(jnp.max(s1, axis=-1, keepdims=True), jnp.max(s2, axis=-1, keepdims=True))
        e1 = jnp.exp(s1 - m)
        e2 = jnp.exp(s2 - m)
        den = jnp.sum(e1, axis=-1, keepdims=True) + jnp.sum(e2, axis=-1, keepdims=True)
        o = (jnp.dot(e1.astype(BF16), vw[:, sl], preferred_element_type=F32)
             + jnp.dot(e2.astype(BF16), vc[:, sl], preferred_element_type=F32))
        outs.append(o / den)
    o_ref[0] = jnp.concatenate(outs, axis=-1).astype(o_ref.dtype)


def _nbr_attention(q, k, v, kc, vc, bias):
    nb, L, _ = q.shape
    rows = L // GRID_W
    ng = rows // Q_ROWS
    past = kc.shape[1]
    tq = Q_ROWS * GRID_W
    full = pl.BlockSpec((1, 1, L, D_ATT), lambda b, g: (b, 0, 0, 0))
    ctx = pl.BlockSpec((1, past, D_ATT), lambda b, g: (b, 0, 0))
    qspec = pl.BlockSpec((1, tq, D_ATT), lambda b, g: (b, g, 0))
    bspec = pl.BlockSpec(bias.shape, lambda b, g: (0, 0, 0, 0), pipeline_mode=pl.Buffered(1))
    return pl.pallas_call(
        functools.partial(_nbr_attn_kernel, rows=rows),
        grid=(nb, ng),
        in_specs=[qspec, full, full, ctx, ctx, bspec],
        out_specs=qspec,
        out_shape=jax.ShapeDtypeStruct((nb, L, D_ATT), BF16),
        compiler_params=_cp("arbitrary", "arbitrary"),
        name="nbr_attn",
    )(q, k, v, kc, vc, bias)


def _fill_padded(xp_ref, src_ref, L):
    z = jnp.zeros((HALO, xp_ref.shape[1]), F32)
    xp_ref[0:HALO, :] = z
    xp_ref[HALO:HALO + L, :] = src_ref[...]
    xp_ref[HALO + L:2 * HALO + L, :] = z


def _rg_gates_kernel(x_ref, cw_ref, cb_ref, w_ref, b_ref, lam_ref,
                     af_ref, bf_ref, ab_ref, bb_ref, xp_ref, *, L, tc, m):
    nh = cw_ref.shape[0]
    for sc in range(m):
        half = (pl.program_id(0) * m + sc) % nh
        cols = slice(sc * LANES, (sc + 1) * LANES)
        _fill_padded(xp_ref, x_ref.at[:, cols], L)
        sp = jax.nn.softplus(-lam_ref[half])
        cw = cw_ref[half]
        for t0 in range(0, L, tc):
            xc = cb_ref[half]
            for k in range(RG_CONV):
                xc = xc + cw[k:k + 1, :] * xp_ref[pl.ds(HALO + t0 + k - RG_CONV // 2, tc), :]
            z = jnp.dot(xc.astype(BF16), w_ref[half], preferred_element_type=F32) + b_ref[half]
            for e, (a_ref, o_ref) in enumerate(((af_ref, bf_ref), (ab_ref, bb_ref))):
                r = _sigmoid(z[:, e * LANES:(e + 1) * LANES])
                i = _sigmoid(z[:, (2 + e) * LANES:(3 + e) * LANES])
                log_a = (-RG_C) * r * sp[:, e * LANES:(e + 1) * LANES]
                a = jnp.exp(log_a)
                one_minus_a2 = -jnp.tanh(log_a) * (a * a + 1.0)
                a_ref[pl.ds(t0, tc), cols] = a
                o_ref[pl.ds(t0, tc), cols] = jnp.sqrt(one_minus_a2) * i * xc


def _blockdiag2(w2):
    z = jnp.zeros_like(w2[0])
    return jnp.concatenate([jnp.concatenate([w2[0], z], 1), jnp.concatenate([z, w2[1]], 1)], 0)


def _rg_weights(wa, ba, wx, bx, lam):
    ws, bs, ls = [], [], []
    for hf in range(D_RNN // LANES):
        blk = slice(2 * hf, 2 * hf + 2)
        ch = slice(hf * LANES, (hf + 1) * LANES)
        ws.append(jnp.concatenate([_blockdiag2(wa[0, blk]), _blockdiag2(wa[1, blk]),
                                   _blockdiag2(wx[0, blk]), _blockdiag2(wx[1, blk])], axis=1))
        bs.append(jnp.concatenate([ba[0, ch], ba[1, ch], bx[0, ch], bx[1, ch]])[None])
        ls.append(jnp.concatenate([lam[0, ch], lam[1, ch]])[None])
    return jnp.stack(ws).astype(BF16), jnp.stack(bs), jnp.stack(ls)


def _lane_chunks(L, cols):
    m = max(1, min(8, (1 << 20) // (L * LANES * 4)))
    while cols % (m * LANES):
        m //= 2
    return m


def _rg_gates(xr_t, conv_w, conv_b, w_bd, b_cat, lam_cat):
    L, cols = xr_t.shape
    nh = D_RNN // LANES
    tc = min(L, 512)
    m = _lane_chunks(L, cols)
    col = pl.BlockSpec((L, m * LANES), lambda c: (0, c))
    const = lambda a: pl.BlockSpec(a.shape, lambda c: (0,) * a.ndim)
    shape = jax.ShapeDtypeStruct((L, cols), F32)
    cw = conv_w.reshape(RG_CONV, nh, LANES).transpose(1, 0, 2)
    cb = conv_b.reshape(nh, 1, LANES)
    return pl.pallas_call(
        functools.partial(_rg_gates_kernel, L=L, tc=tc, m=m),
        grid=(cols // (m * LANES),),
        in_specs=[col, const(cw), const(cb), const(w_bd), const(b_cat), const(lam_cat)],
        out_specs=[col, col, col, col],
        out_shape=[shape, shape, shape, shape],
        scratch_shapes=[pltpu.VMEM((L + 2 * HALO, LANES), F32)],
        compiler_params=_cp("arbitrary"),
        name="rg_gates",
    )(xr_t, cw, cb, w_bd, b_cat, lam_cat)


def _rg_scan_kernel(af_ref, bf_ref, ab_ref, bb_ref, h0f_ref, h0b_ref, hf_ref, hb_ref, cf_ref, cb_ref, *, tc):
    @pl.when(pl.program_id(1) == 0)
    def _():
        cf_ref[...] = h0f_ref[...]
        cb_ref[...] = h0b_ref[...]

    def body(t, carry):
        hf, hb = carry
        hf = af_ref[pl.ds(t, 1), :] * hf + bf_ref[pl.ds(t, 1), :]
        hf_ref[pl.ds(t, 1), :] = hf
        tb = tc - 1 - t
        hb = ab_ref[pl.ds(tb, 1), :] * hb + bb_ref[pl.ds(tb, 1), :]
        hb_ref[pl.ds(tb, 1), :] = hb
        return hf, hb

    hf, hb = lax.fori_loop(0, tc, body, (cf_ref[...], cb_ref[...]), unroll=8)
    cf_ref[...] = hf
    cb_ref[...] = hb


def _rg_scan(a_f, b_f, a_b, b_b, h0f, h0b):
    L, cols = a_f.shape
    cw = min(cols, 1024)
    tc = min(L, 512)
    nt = L // tc
    fwd = pl.BlockSpec((tc, cw), lambda c, i: (i, c))
    bwd = pl.BlockSpec((tc, cw), lambda c, i: (nt - 1 - i, c))
    row = pl.BlockSpec((1, cw), lambda c, i: (0, c))
    shape = jax.ShapeDtypeStruct((L, cols), F32)
    return pl.pallas_call(
        functools.partial(_rg_scan_kernel, tc=tc),
        grid=(cols // cw, nt),
        in_specs=[fwd, fwd, bwd, bwd, row, row],
        out_specs=[fwd, bwd],
        out_shape=[shape, shape],
        scratch_shapes=[pltpu.VMEM((1, cw), F32), pltpu.VMEM((1, cw), F32)],
        compiler_params=_cp("arbitrary", "arbitrary"),
        name="rg_scan",
    )(a_f, b_f, a_b, b_b, h0f, h0b)


def _hy_filter_kernel(feat_ref, w1_ref, b1_ref, w2_ref, b2_ref, w3_ref, fr_ref, dec_ref, o_ref, *, tm):
    feats = feat_ref[...]
    dot = functools.partial(jnp.dot, precision=HIGHEST, preferred_element_type=F32)
    h = jnp.sin(fr_ref[0:1, :] * (dot(feats, w1_ref[...]) + b1_ref[...]))
    h = jnp.sin(fr_ref[1:2, :] * (dot(h, w2_ref[...]) + b2_ref[...]))
    h = dot(h, w3_ref[...])
    filt = h * jnp.exp(-feats[:, 0:1] * jnp.abs(dec_ref[...]))
    row = pl.program_id(0) * tm + lax.broadcasted_iota(jnp.int32, filt.shape, 0)
    lane = lax.broadcasted_iota(jnp.int32, filt.shape, 1)
    o_ref[...] = jnp.where((row == 0) & (lane >= 2 * D_HY), 0.0, filt)


def _hy_features(L):
    pos = np.arange(L, dtype=np.float64)
    t = np.linspace(0.0, 1.0, L)
    bands = (POS_EMB_DIM - 1) // 2
    f = np.linspace(1e-4, bands - 1, bands)
    ang = (2.0 * math.pi / L) * pos[:, None] * f[None, :]
    feats = np.concatenate([t[:, None], np.cos(ang), -np.sin(ang)], axis=-1)
    out = np.zeros((L, LANES), np.float32)
    out[:, :POS_EMB_DIM] = feats
    return out


def _two_level(L):
    return L % (FFT_N1 * 8) == 0 and L >= 2048


def _digit_swap(x, L):
    n2 = 2 * L // FFT_N1
    return x.reshape(L // n2, n2, -1).transpose(1, 0, 2).reshape(L, -1)


def _hy_filters(L, w1, b1, w2, b2, w3, freq, decay):
    tm = min(L, 512)
    hid = w2.shape[0]
    n = w3.shape[1]
    w1p = jnp.zeros((LANES, hid), F32).at[:POS_EMB_DIM].set(w1)
    feats = _hy_features(L)
    if _two_level(L):
        feats = _digit_swap(feats, L)
    c = lambda s: pl.BlockSpec(s, lambda i: (0, 0))
    return pl.pallas_call(
        functools.partial(_hy_filter_kernel, tm=tm),
        grid=(L // tm,),
        in_specs=[pl.BlockSpec((tm, LANES), lambda i: (i, 0)),
                  c((LANES, hid)), c((1, hid)), c((hid, hid)), c((1, hid)), c((hid, n)), c((2, hid)), c((1, n))],
        out_specs=pl.BlockSpec((tm, n), lambda i: (i, 0)),
        out_shape=jax.ShapeDtypeStruct((L, n), F32),
        compiler_params=_cp("arbitrary"),
        name="hy_filter",
    )(jnp.asarray(feats), w1p, b1[None], w2, b2[None], w3, freq, decay.reshape(1, n))


def _conv3_kernel(v_ref, a_ref, b_ref, w_ref, cb_ref, ov_ref, oa_ref, ob_ref, xp_ref, xs_ref,
                  *, L, tc, swap, n2, m):
    h1 = L // n2
    pitch = n2 + STRIDE_PAD
    nh = w_ref.shape[0]
    for sc in range(m):
        half = (pl.program_id(0) * m + sc) % nh
        cols = slice(sc * LANES, (sc + 1) * LANES)
        for p, (src, dst) in enumerate(((v_ref, ov_ref), (a_ref, oa_ref), (b_ref, ob_ref))):
            _fill_padded(xp_ref, src.at[:, cols], L)
            w = w_ref[half, p]
            cb = cb_ref[half, p]
            if swap[p]:
                for i1 in range(h1):
                    xs_ref[pl.ds(i1 * pitch, n2 + HY_CONV - 1), :] = (
                        xp_ref[pl.ds(HALO + i1 * n2 - HY_CONV // 2, n2 + HY_CONV - 1), :])

                def body(i2, carry, dst=dst, w=w, cb=cb):
                    y = cb
                    for k in range(HY_CONV):
                        y = y + w[k:k + 1, :] * xs_ref[pl.ds(i2 + k, h1, stride=pitch), :]
                    dst[i2] = y
                    return carry

                lax.fori_loop(0, n2, body, 0)
            else:
                for t0 in range(0, L, tc):
                    y = cb
                    for k in range(HY_CONV):
                        y = y + w[k:k + 1, :] * xp_ref[pl.ds(HALO + t0 + k - HY_CONV // 2, tc), :]
                    dst[pl.ds(t0, tc), cols] = y


def _hy_conv3(v_t, a_t, b_t, conv_w, conv_b, swap=(False, False, False)):
    L, cols = v_t.shape
    nh = D_HY // LANES
    tc = min(L, 512)
    n2 = 2 * L // FFT_N1
    m = 1 if any(swap) else _lane_chunks(L, cols)
    col = pl.BlockSpec((L, m * LANES), lambda c: (0, c))
    col3 = pl.BlockSpec((n2, L // n2, LANES), lambda c: (0, 0, c))
    shape = jax.ShapeDtypeStruct((L, cols), F32)
    shape3 = jax.ShapeDtypeStruct((n2, L // n2, cols), F32)
    w = conv_w.reshape(HY_CONV, 3, nh, LANES).transpose(2, 1, 0, 3)
    cb = conv_b.reshape(3, nh, 1, LANES).transpose(1, 0, 2, 3)
    const = lambda a: pl.BlockSpec(a.shape, lambda c: (0,) * a.ndim)
    return pl.pallas_call(
        functools.partial(_conv3_kernel, L=L, tc=tc, swap=swap, n2=n2, m=m),
        grid=(cols // (m * LANES),),
        in_specs=[col, col, col, const(w), const(cb)],
        out_specs=[col3 if s else col for s in swap],
        out_shape=[shape3 if s else shape for s in swap],
        scratch_shapes=[pltpu.VMEM((L + 2 * HALO, LANES), F32),
                        pltpu.VMEM(((L // n2) * (n2 + STRIDE_PAD) + HALO, LANES), F32)],
        compiler_params=_cp("arbitrary"),
        name="hy_conv3",
    )(v_t, a_t, b_t, w, cb)


def _phase(p, n2):
    ang = 2.0 * np.pi * (np.asarray(p, np.int64) % n2) / n2
    return np.cos(ang), np.sin(ang)


def _dft_direct_tables(L):
    k = np.arange(L)[:, None]
    t = np.arange(L)[None, :]
    c, s = _phase((2 * k + 1) * t, 4 * L)
    fwd = np.concatenate([c, -s], axis=0)
    inv = (1.0 / L) * np.concatenate([c.T, -s.T], axis=1)
    return fwd.astype(np.float32), inv.astype(np.float32)


def _dft_two_level_tables(L, n1):
    N = 2 * L
    n2 = N // n1
    h1 = n1 // 2
    i2 = np.arange(n2)[:, None, None]
    k1 = np.arange(n1)[None, :, None]
    i1 = np.arange(h1)[None, None, :]
    c, s = _phase(2 * n2 * i1 * k1 + 2 * i2 * k1 + n2 * i1 + i2, 2 * N)
    t1 = np.concatenate([c, -s], axis=1)
    ct, st = c.transpose(0, 2, 1), s.transpose(0, 2, 1)
    t1i = (2.0 / N) * np.concatenate([ct, -st], axis=2)
    kk = np.arange(n2 // 2)[:, None]
    nn = np.arange(n2)[None, :]
    c2, s2 = _phase(nn * kk, n2)
    t2 = np.block([[c2, s2], [-s2, c2]])
    t2i = np.block([[c2.T, -s2.T], [s2.T, c2.T]])
    return tuple(x.astype(np.float32) for x in (t1, t2, t2i, t1i))


def _filter_spectrum(fre, fim, order, skip):
    fw = slice(order * D_HY, (order + 1) * D_HY)
    bw = slice((2 + order) * D_HY, (3 + order) * D_HY)
    return fre[:, fw] + fre[:, bw] + skip, fim[:, fw] - fim[:, bw]


def _spectral_product(xr, xi, fr, fi, nbc):
    yr, yi = [], []
    for b in range(nbc):
        sl = slice(b * D_HY, (b + 1) * D_HY)
        yr.append(xr[:, sl] * fr - xi[:, sl] * fi)
        yi.append(xr[:, sl] * fi + xi[:, sl] * fr)
    return jnp.concatenate([jnp.concatenate(yr, axis=1), jnp.concatenate(yi, axis=1)], axis=0)


def _matmul_kernel(a_ref, b_ref, o_ref):
    o_ref[...] = jnp.dot(a_ref[...], b_ref[...], precision=HIGHEST, preferred_element_type=F32)


def _hy_direct_kernel(z_ref, g_ref, f_ref, skip_ref, t_ref, ti_ref, o_ref, *, L, nbc, order):
    x = jnp.dot(t_ref[...], z_ref[...].astype(BF16), preferred_element_type=F32)
    fr, fi = _filter_spectrum(f_ref[0:L, :], f_ref[L:2 * L, :], order, skip_ref[0])
    y = _spectral_product(x[:L], x[L:], fr, fi, nbc).astype(BF16)
    conv = jnp.dot(ti_ref[...], y, preferred_element_type=F32)
    o_ref[...] = (g_ref[...] * conv).astype(o_ref.dtype)


def _hyena_direct(z, g1, g2, filt, skip):
    L, cols = z.shape
    fwd, inv = _dft_direct_tables(L)
    n = filt.shape[1]
    spec = pl.pallas_call(
        _matmul_kernel,
        out_shape=jax.ShapeDtypeStruct((2 * L, n), F32),
        name="hy_fspec",
    )(jnp.asarray(fwd), filt)
    tn = min(cols, 4 * D_HY)
    nbc = tn // D_HY
    col = pl.BlockSpec((L, tn), lambda c: (0, c))
    const = lambda s: pl.BlockSpec(s, lambda c: (0, 0))
    t_bf, ti_bf = jnp.asarray(fwd).astype(BF16), jnp.asarray(inv).astype(BF16)
    for order, g in enumerate((g1, g2)):
        z = pl.pallas_call(
            functools.partial(_hy_direct_kernel, L=L, nbc=nbc, order=order),
            grid=(cols // tn,),
            in_specs=[col, col, const((2 * L, n)), pl.BlockSpec((1, 1, D_HY), lambda c: (order, 0, 0)),
                      const((2 * L, L)), const((L, 2 * L))],
            out_specs=col,
            out_shape=jax.ShapeDtypeStruct((L, cols), F32 if order == 0 else BF16),
            compiler_params=_cp("arbitrary"),
            name="hy_direct",
        )(z, g, spec, skip[:, None, :], t_bf, ti_bf)
    return z


def _kron_tables(t1, t1i, G):
    n2, two_n1, h1 = t1.shape
    ng = n2 // G
    eye = np.eye(two_n1, dtype=np.float32)
    rep_rows = jnp.asarray(np.repeat(eye, G, axis=0))
    rep_cols = jnp.asarray(np.repeat(eye, G, axis=1))
    c_fwd = t1.reshape(ng, G, two_n1, h1).transpose(0, 2, 1, 3).reshape(ng, two_n1, G * h1)
    c_swapped = t1i.reshape(ng, G * h1, two_n1)
    c_natural = t1i.reshape(ng, G, h1, two_n1).transpose(0, 2, 1, 3).reshape(ng, h1 * G, two_n1)

    def expand(compact, rep, left, row_key, lane_key):
        r, c = (rep.shape[0], compact.shape[2]) if left else (compact.shape[1], rep.shape[1])
        return pl.pallas_call(
            functools.partial(_kron_expand_kernel, left=left, row_key=row_key, lane_key=lane_key),
            grid=(ng,),
            in_specs=[pl.BlockSpec((1,) + compact.shape[1:], lambda i: (i, 0, 0)),
                      pl.BlockSpec(rep.shape, lambda i: (0, 0))],
            out_specs=pl.BlockSpec((1, r, c), lambda i: (i, 0, 0)),
            out_shape=jax.ShapeDtypeStruct((ng, r, c), BF16),
            compiler_params=_cp("arbitrary"),
            name="hy_kron",
        )(jnp.asarray(compact), rep)

    fwd = expand(c_fwd, rep_rows, True, lambda r: r % G, lambda c: c // h1)
    inv_swapped = expand(c_swapped, rep_cols, False, lambda r: r // h1, lambda c: c % G)
    inv_natural = expand(c_natural, rep_cols, False, lambda r: r % G, lambda c: c % G)
    return fwd, inv_swapped, inv_natural


def _kron_expand_kernel(c_ref, rep_ref, o_ref, *, left, row_key, lane_key):
    comp = c_ref[0].astype(BF16)
    rep = rep_ref[...].astype(BF16)
    full = (jnp.dot(rep, comp, preferred_element_type=F32) if left
            else jnp.dot(comp, rep, preferred_element_type=F32))
    rows = lax.broadcasted_iota(jnp.int32, full.shape, 0)
    lanes = lax.broadcasted_iota(jnp.int32, full.shape, 1)
    o_ref[0] = jnp.where(row_key(rows) == lane_key(lanes), full, 0.0).astype(BF16)


def _hy_stage1_kernel(x_ref, t_ref, re_ref, im_ref, *, G, n1, ct):
    h1, cols = x_ref.shape[1], x_ref.shape[2]
    x = x_ref[...].reshape(G * h1, cols).astype(BF16)
    for c0 in range(0, cols, ct):
        a = jnp.dot(t_ref[0], x[:, c0:c0 + ct], preferred_element_type=F32)
        re_ref[:, :, c0:c0 + ct] = a[:n1 * G].reshape(n1, G, ct).astype(BF16)
        im_ref[:, :, c0:c0 + ct] = a[n1 * G:].reshape(n1, G, ct).astype(BF16)


def _hy_stage1(x3, t_fwd, *, G):
    n2, h1, cols = x3.shape
    n1 = t_fwd.shape[1] // (2 * G)
    ospec = pl.BlockSpec((n1, G, cols), lambda j: (0, j, 0))
    oshape = jax.ShapeDtypeStruct((n1, n2, cols), BF16)
    return pl.pallas_call(
        functools.partial(_hy_stage1_kernel, G=G, n1=n1, ct=D_HY),
        grid=(n2 // G,),
        in_specs=[pl.BlockSpec((G, h1, cols), lambda j: (j, 0, 0)),
                  pl.BlockSpec((1,) + t_fwd.shape[1:], lambda j: (j, 0, 0))],
        out_specs=[ospec, ospec],
        out_shape=[oshape, oshape],
        compiler_params=_cp("arbitrary"),
        name="hy_stage1",
    )(x3, t_fwd)


def _hy_fspec2_kernel(re_ref, im_ref, t2_ref, fre_ref, fim_ref, *, kb, k2):
    for j in range(kb):
        a2 = jnp.concatenate([re_ref[j], im_ref[j]], axis=0)
        x = jnp.dot(t2_ref[...], a2, preferred_element_type=F32)
        fre_ref[j] = x[:k2]
        fim_ref[j] = x[k2:]


def _hy_stage2_kernel(re_ref, im_ref, fre_ref, fim_ref, skip_ref, t2_ref, t2i_ref, ore_ref, oim_ref,
                      *, kb, k2, n2, nbc, order):
    for j in range(kb):
        a2 = jnp.concatenate([re_ref[j], im_ref[j]], axis=0)
        x = jnp.dot(t2_ref[...], a2, preferred_element_type=F32)
        fr, fi = _filter_spectrum(fre_ref[j], fim_ref[j], order, skip_ref[0])
        y = _spectral_product(x[:k2], x[k2:], fr, fi, nbc).astype(BF16)
        a = jnp.dot(t2i_ref[...], y, preferred_element_type=F32)
        ore_ref[j] = a[:n2].astype(BF16)
        oim_ref[j] = a[n2:].astype(BF16)


def _hy_stage3_kernel(re_ref, im_ref, t_ref, g_ref, o_ref, *, G, n1, ct):
    cols = re_ref.shape[2]
    for c0 in range(0, cols, ct):
        sl = slice(c0, c0 + ct)
        a2 = jnp.concatenate([re_ref[:, :, sl].reshape(n1 * G, ct), im_ref[:, :, sl].reshape(n1 * G, ct)], axis=0)
        conv = jnp.dot(t_ref[0], a2, preferred_element_type=F32)
        g = g_ref[:, :, sl]
        o_ref[:, :, sl] = (g * conv.reshape(g.shape)).astype(o_ref.dtype)


def _two_level_tables(L, *, n1=FFT_N1, G=FFT_G):
    t1, t2, t2i, t1i = _dft_two_level_tables(L, n1)
    return (jnp.asarray(t2).astype(BF16), jnp.asarray(t2i).astype(BF16)) + _kron_tables(t1, t1i, G)


def _hyena_two_level(z3, g1_3, g2, filt, skip, tables, *, n1=FFT_N1, G=FFT_G, kb=4):
    n2, h1, cols = z3.shape
    L = n2 * h1
    k2 = n2 // 2
    nbc = cols // D_HY
    t2_bf, t2i_bf, t_fwd, t_inv_swapped, t_inv_natural = tables
    nf = filt.shape[1]

    f_re, f_im = _hy_stage1(filt.reshape(n2, h1, nf), t_fwd, G=G)
    blk = lambda c: pl.BlockSpec((kb, n2, c), lambda i: (i, 0, 0))
    hblk = lambda c: pl.BlockSpec((kb, k2, c), lambda i: (i, 0, 0))
    const = lambda s: pl.BlockSpec(s, lambda i: (0, 0))
    fshape = jax.ShapeDtypeStruct((n1, k2, nf), F32)
    fre, fim = pl.pallas_call(
        functools.partial(_hy_fspec2_kernel, kb=kb, k2=k2),
        grid=(n1 // kb,),
        in_specs=[blk(nf), blk(nf), const((n2, 2 * n2))],
        out_specs=[hblk(nf), hblk(nf)],
        out_shape=[fshape, fshape],
        compiler_params=_cp("arbitrary"),
        name="hy_fspec2",
    )(f_re, f_im, t2_bf)

    ashape = jax.ShapeDtypeStruct((n1, n2, cols), BF16)
    grp = pl.BlockSpec((n1, G, cols), lambda j: (0, j, 0))
    tspec = lambda t: pl.BlockSpec((1,) + t.shape[1:], lambda j: (j, 0, 0))
    for order in range(2):
        a_re, a_im = _hy_stage1(z3, t_fwd, G=G)
        b_re, b_im = pl.pallas_call(
            functools.partial(_hy_stage2_kernel, kb=kb, k2=k2, n2=n2, nbc=nbc, order=order),
            grid=(n1 // kb,),
            in_specs=[blk(cols), blk(cols), hblk(nf), hblk(nf),
                      pl.BlockSpec((1, 1, D_HY), lambda i: (order, 0, 0)),
                      const((n2, 2 * n2)), const((2 * n2, n2))],
            out_specs=[blk(cols), blk(cols)],
            out_shape=[ashape, ashape],
            compiler_params=_cp("arbitrary"),
            name="hy_stage2",
        )(a_re, a_im, fre, fim, skip[:, None, :], t2_bf, t2i_bf)
        if order == 0:
            table, gate = t_inv_swapped, g1_3
            io = pl.BlockSpec((G, h1, cols), lambda j: (j, 0, 0))
            oshape = jax.ShapeDtypeStruct((n2, h1, cols), F32)
        else:
            table, gate = t_inv_natural, g2.reshape(h1, n2, cols)
            io = pl.BlockSpec((h1, G, cols), lambda j: (0, j, 0))
            oshape = jax.ShapeDtypeStruct((h1, n2, cols), BF16)
        z3 = pl.pallas_call(
            functools.partial(_hy_stage3_kernel, G=G, n1=n1, ct=D_HY),
            grid=(n2 // G,),
            in_specs=[grp, grp, tspec(table), io],
            out_specs=io,
            out_shape=oshape,
            compiler_params=_cp("arbitrary"),
            name="hy_stage3",
        )(b_re, b_im, table, gate)
    return z3.reshape(L, cols)


def _hyena(hv_t, h1_t, h2_t, conv_w, conv_b, filt, skip, tables=None):
    L, cols = hv_t.shape
    if _two_level(L):
        z3, g1_3, g2 = _hy_conv3(hv_t, h1_t, h2_t, conv_w, conv_b, swap=(True, True, False))
        return _hyena_two_level(z3, g1_3, g2, filt, skip, tables or _two_level_tables(L))
    z, g1, g2 = _hy_conv3(hv_t, h1_t, h2_t, conv_w, conv_b)
    return _hyena_direct(z, g1, g2, filt, skip)


def _route(logits):
    lane = lax.broadcasted_iota(jnp.int32, logits.shape, 1)
    is_g = (lane >= N_EXPERTS) & (lane < N_EXPERTS + N_GROUPS)
    big = jnp.int32(1 << 20)
    gl = jnp.where(is_g, logits, NEG)
    gmax = jnp.max(gl, axis=-1, keepdims=True)
    gidx = jnp.min(jnp.where(gl == gmax, lane - N_EXPERTS, big), axis=-1, keepdims=True)
    p_group = 1.0 / jnp.sum(jnp.where(is_g, jnp.exp(gl - gmax), 0.0), axis=-1, keepdims=True)
    sel = (lane < N_EXPERTS) & ((lane // EXPERTS_PER_GROUP) == gidx)
    el = jnp.where(sel, logits, NEG)
    ee = jnp.where(sel, jnp.exp(el - jnp.max(el, axis=-1, keepdims=True)), 0.0)
    pe = ee / jnp.sum(ee, axis=-1, keepdims=True)
    p1 = jnp.max(jnp.where(sel, pe, -1.0), axis=-1, keepdims=True)
    i1 = jnp.min(jnp.where(sel & (pe == p1), lane, big), axis=-1, keepdims=True)
    rest = sel & (lane != i1)
    p2 = jnp.max(jnp.where(rest, pe, -1.0), axis=-1, keepdims=True)
    i2 = jnp.min(jnp.where(rest & (pe == p2), lane, big), axis=-1, keepdims=True)
    den = p1 + p2
    cmb = jnp.where(lane == i1, p_group * (p1 / den), jnp.where(lane == i2, p_group * (p2 / den), 0.0))
    return jnp.where(lane == N_EXPERTS, gidx.astype(F32), cmb)


def _outproj_kernel(att_ref, hf_ref, hb_ref, gt_ref, hy_ref, x_ref, mod_ref, w_ref, g_ref, b_ref,
                    wr_ref, br_ref, x1_ref, u2_ref, cmb_ref):
    rg = (hf_ref[...] + hb_ref[...]) * jax.nn.gelu(gt_ref[...])
    mix = (jnp.dot(att_ref[0], w_ref[0:D_ATT, :], preferred_element_type=F32)
           + jnp.dot(rg.astype(BF16), w_ref[D_ATT:D_ATT + D_RNN, :], preferred_element_type=F32)
           + jnp.dot(hy_ref[...].astype(BF16), w_ref[D_ATT + D_RNN:, :], preferred_element_type=F32))
    m = mod_ref[0]
    x1 = _ln(ALPHA * x_ref[...] + m[2:3] * mix) * g_ref[...] + b_ref[...]
    u2 = _ln(x1) * (1.0 + m[4:5]) + m[3:4]
    u_hi = u2.astype(BF16)
    u_lo = (u2 - u_hi.astype(F32)).astype(BF16)
    wr = wr_ref[...]
    w_hi = wr.astype(BF16)
    w_lo = (wr - w_hi.astype(F32)).astype(BF16)
    hi = jnp.dot(u_hi, jnp.concatenate([w_hi, w_lo], axis=1), preferred_element_type=F32)
    lo = jnp.dot(u_lo, w_hi, preferred_element_type=F32)
    logits = hi[:, 0:LANES] + hi[:, LANES:] + lo + br_ref[...]
    x1_ref[...] = x1
    u2_ref[...] = u_hi
    cmb_ref[...] = _route(logits)


def _outproj(att, hf_t, hb_t, gt_t, hy_t, x2d, mod_l, w_bf, ln_g, ln_b, wr, br, *, nb, L, tm,
             cond_base, cond_per_batch):
    tpb = L // tm
    d = x2d.shape[1]
    cond = (lambda i: cond_base + i // tpb) if cond_per_batch else (lambda i: cond_base)
    t_spec = pl.BlockSpec((tm, D_RNN), lambda i: (i % tpb, i // tpb))
    tok = lambda c: pl.BlockSpec((tm, c), lambda i: (i, 0))
    const = lambda s: pl.BlockSpec(s, lambda i: (0, 0))
    n = nb * L
    return pl.pallas_call(
        _outproj_kernel,
        grid=(nb * tpb,),
        in_specs=[pl.BlockSpec((1, tm, D_ATT), lambda i: (i // tpb, i % tpb, 0)),
                  t_spec, t_spec, t_spec, t_spec, tok(d),
                  pl.BlockSpec((1, 6, d), lambda i: (cond(i), 0, 0)),
                  const((d, d)), const((1, d)), const((1, d)), const((d, LANES)), const((1, LANES))],
        out_specs=[tok(d), tok(d), tok(LANES)],
        out_shape=[jax.ShapeDtypeStruct((n, d), F32), jax.ShapeDtypeStruct((n, d), BF16),
                   jax.ShapeDtypeStruct((n, LANES), F32)],
        compiler_params=_cp("arbitrary"),
        name="outproj",
    )(att, hf_t, hb_t, gt_t, hy_t, x2d, mod_l, w_bf, ln_g[None], ln_b[None], wr, br)


def _moe_kernel(u_ref, cmb_ref, x1_ref, mod_ref, wgu_ref, wd_ref, g_ref, b_ref, o_ref, ys_ref, *, T, BLK):
    SL = T + N_GROUPS * BLK
    u = u_ref[...]
    cmb = cmb_ref[...]
    lane_f = lax.broadcasted_iota(jnp.int32, (T, LANES), 1).astype(F32)
    gid_col = jnp.sum(jnp.where(lane_f == float(N_EXPERTS), cmb, 0.0), axis=-1, keepdims=True)
    onehot = jnp.where(lane_f == gid_col, 1.0, 0.0)
    ri = lax.broadcasted_iota(jnp.int32, (T, T), 0)
    ci = lax.broadcasted_iota(jnp.int32, (T, T), 1)
    before = jnp.where(ci < ri, 1.0, 0.0).astype(BF16)
    rank_col = jnp.sum(onehot * jnp.dot(before, onehot.astype(BF16), preferred_element_type=F32),
                       axis=-1, keepdims=True)
    counts = jnp.sum(onehot, axis=0, keepdims=True)
    lane_row = lax.broadcasted_iota(jnp.int32, (1, LANES), 1)
    base = jnp.int32(0)
    bases = []
    base_row = jnp.zeros((1, LANES), jnp.int32)
    for c in range(N_GROUPS):
        n_c = jnp.sum(jnp.where(lane_row == c, counts, 0.0)).astype(jnp.int32)
        bases.append(base)
        base_row = jnp.where(lane_row == c, base, base_row)
        base = base + ((n_c + (BLK - 1)) // BLK) * BLK
    total = base
    dest_f = jnp.sum(onehot * base_row.astype(F32), axis=-1, keepdims=True) + rank_col
    dest_col = dest_f.astype(jnp.int32)
    dest_row = jnp.broadcast_to(dest_f, (T, LANES)).T[0:1, :].astype(jnp.int32)
    cmb_hi = cmb.astype(BF16)
    cmb_lo = (cmb - cmb_hi.astype(F32)).astype(BF16)
    blk_rows = lax.broadcasted_iota(jnp.int32, (BLK, T), 0)
    blk_lane = lax.broadcasted_iota(jnp.int32, (BLK, LANES), 1)

    def block(b, carry):
        r0 = b * BLK
        rows = pl.ds(pl.multiple_of(r0, BLK), BLK)

        @pl.when(r0 < total)
        def _():
            grp = ((r0 >= bases[1]).astype(jnp.int32) + (r0 >= bases[2]).astype(jnp.int32)
                   + (r0 >= bases[3]).astype(jnp.int32))
            p = jnp.where(blk_rows + r0 == dest_row, 1.0, 0.0).astype(BF16)
            xb = jnp.dot(p, u, preferred_element_type=F32).astype(BF16)
            wb = (jnp.dot(p, cmb_hi, preferred_element_type=F32)
                  + jnp.dot(p, cmb_lo, preferred_element_type=F32))
            gu = jnp.dot(xb, wgu_ref[grp], preferred_element_type=F32)
            hs = []
            for j in range(EXPERTS_PER_GROUP):
                gate = gu[:, 2 * j * D_EXPERT:(2 * j + 1) * D_EXPERT]
                up = gu[:, (2 * j + 1) * D_EXPERT:(2 * j + 2) * D_EXPERT]
                w = jnp.sum(jnp.where(blk_lane == grp * EXPERTS_PER_GROUP + j, wb, 0.0), axis=-1, keepdims=True)
                hs.append((gate * _sigmoid(gate) * up * w).astype(BF16))
            ys_ref[rows, :] = jnp.dot(jnp.concatenate(hs, axis=1), wd_ref[grp],
                                      preferred_element_type=F32).astype(BF16)

        @pl.when(r0 >= total)
        def _():
            ys_ref[rows, :] = jnp.zeros((BLK, u.shape[1]), BF16)

        return carry

    lax.fori_loop(0, SL // BLK, block, 0)
    p_t = jnp.where(lax.broadcasted_iota(jnp.int32, (T, SL), 1) == dest_col, 1.0, 0.0).astype(BF16)
    moe = jnp.dot(p_t, ys_ref[...], preferred_element_type=F32)
    m = mod_ref[0]
    o_ref[...] = _ln(ALPHA * x1_ref[...] + m[5:6] * moe) * g_ref[...] + b_ref[...]


def _group_gate_up(w_gate, w_up):
    _, d, f = w_gate.shape
    gu = jnp.concatenate([w_gate, w_up], axis=-1).astype(BF16).reshape(N_GROUPS, EXPERTS_PER_GROUP, d, 2 * f)
    return gu.transpose(0, 2, 1, 3).reshape(N_GROUPS, d, EXPERTS_PER_GROUP * 2 * f)


def _moe(u2, cmb, x1, mod_l, wgu, wd, ln_g, ln_b, *, tm, tiles_per_cond, cond_base, cond_per_batch):
    n, d = x1.shape
    cond = (lambda i: cond_base + i // tiles_per_cond) if cond_per_batch else (lambda i: cond_base)
    tok = lambda c: pl.BlockSpec((tm, c), lambda i: (i, 0))
    const = lambda s: pl.BlockSpec(s, lambda i: (0,) * len(s))
    resident = lambda s: pl.BlockSpec(s, lambda i: (0,) * len(s), pipeline_mode=pl.Buffered(1))
    return pl.pallas_call(
        functools.partial(_moe_kernel, T=tm, BLK=MOE_BLK),
        grid=(n // tm,),
        in_specs=[tok(d), tok(LANES), tok(d),
                  pl.BlockSpec((1, 6, d), lambda i: (cond(i), 0, 0)),
                  resident(wgu.shape), resident(wd.shape),
                  const((1, d)), const((1, d))],
        out_specs=tok(d),
        out_shape=jax.ShapeDtypeStruct((n, d), F32),
        scratch_shapes=[pltpu.VMEM((tm + N_GROUPS * MOE_BLK, d), BF16)],
        compiler_params=_cp("arbitrary"),
        name="moe",
    )(u2, cmb, x1, mod_l, wgu, wd, ln_g[None], ln_b[None])


def _stream_layer(x2d, mod_l, p, *, nb, L, tm, cond_base, cond_per_batch, ctx, cache=None, h0=None, prev_kv=None):
    kw = dict(nb=nb, L=L, tm=tm, cond_base=cond_base, cond_per_batch=cond_per_batch)
    q, k, v, xr_t, gt_t, hv_t, h1_t, h2_t = _inproj(x2d, mod_l, p["w_in"], kv_dtype=F32 if ctx else BF16,
                                                    prev_kv=prev_kv, **kw)
    if ctx:
        att = _ctx_attention(q, k, v)
    else:
        att = _nbr_attention(q, k, v, cache[0], cache[1], p["nbr_bias"])
    a_f, b_f, a_b, b_b = _rg_gates(xr_t, p["rg_conv_w"], p["rg_conv_b"], *p["rg_w"])
    hf_t, hb_t = _rg_scan(a_f, b_f, a_b, b_b, h0[0], h0[1])
    hy_t = _hyena(hv_t, h1_t, h2_t, p["hy_conv_w"], p["hy_conv_b"], p["hy_filt"][L], p["hy_skip"],
                  p["hy_tables"].get(L))
    x1, u2, cmb = _outproj(att, hf_t, hb_t, gt_t, hy_t, x2d, mod_l, p["w_out"], p["ln1_g"], p["ln1_b"],
                           p["router_w"], p["router_b"], **kw)
    tm_moe = min(MOE_TILE, nb * L)
    x2 = _moe(u2, cmb, x1, mod_l, p["moe_wgu"], p["moe_wd"], p["ln2_g"], p["ln2_b"], tm=tm_moe,
              tiles_per_cond=max(L // tm_moe, 1), cond_base=cond_base, cond_per_batch=cond_per_batch)
    return x2, k, v, hf_t, hb_t


def kernel(x_prompt, x_sample, cache_k, cache_v, state_rglru, c, c_ctx, w_mod, b_mod, w_in, attn_rpb, rg_conv_w, rg_conv_b, rg_wa, rg_ba, rg_wx, rg_bx, rg_lambda, hy_conv_w, hy_conv_b, hy_w1, hy_b1, hy_w2, hy_b2, hy_w3, hy_freq, hy_decay, hy_skip, w_out, ln1_g, ln1_b, router_wg, router_bg, router_we, router_be, moe_w_gate, moe_w_up, moe_w_down, ln2_g, ln2_b):
    nbc, Lc, d = x_prompt.shape
    nbl, Ll, _ = x_sample.shape
    depth = w_mod.shape[0]
    past = cache_k.shape[2]

    cond8 = jnp.zeros((8, d), F32).at[0].set(c_ctx).at[1:1 + nbl].set(c)
    mod = _modulation(cond8, w_mod, b_mod).reshape(depth, 8, 6, d)

    xp = x_prompt.reshape(nbc * Lc, d)
    xs = x_sample.reshape(nbl * Ll, d)
    kv, new_h = None, []
    hy_tables = {L: _two_level_tables(L) for L in {Lc, Ll} if _two_level(L)}
    for l in range(depth):
        pad = jnp.zeros((d, LANES - N_EXPERTS - N_GROUPS), F32)
        p = {
            "w_in": w_in[l].astype(BF16),
            "nbr_bias": _nbr_bias_tiles(attn_rpb[l]),
            "rg_conv_w": rg_conv_w[l], "rg_conv_b": rg_conv_b[l],
            "rg_w": _rg_weights(rg_wa[l], rg_ba[l], rg_wx[l], rg_bx[l], rg_lambda[l]),
            "hy_conv_w": hy_conv_w[l], "hy_conv_b": hy_conv_b[l], "hy_skip": hy_skip[l], "hy_tables": hy_tables,
            "hy_filt": {L: _hy_filters(L, hy_w1[l], hy_b1[l], hy_w2[l], hy_b2[l], hy_w3[l], hy_freq[l], hy_decay[l])
                        for L in sorted({Lc, Ll})},
            "w_out": w_out[l].astype(BF16), "ln1_g": ln1_g[l], "ln1_b": ln1_b[l],
            "router_w": jnp.concatenate([router_we[l], router_wg[l], pad], axis=1),
            "router_b": jnp.concatenate([router_be[l], router_bg[l], pad[0]])[None],
            "moe_wgu": _group_gate_up(moe_w_gate[l], moe_w_up[l]),
            "moe_wd": moe_w_down[l].astype(BF16).reshape(N_GROUPS, EXPERTS_PER_GROUP * D_EXPERT, d),
            "ln2_g": ln2_g[l], "ln2_b": ln2_b[l],
        }
        zeros_h = jnp.zeros((1, nbc * D_RNN), F32)
        xp, k, v, hf_t, hb_t = _stream_layer(
            xp, mod[l], p, nb=nbc, L=Lc, tm=Lc, cond_base=0, cond_per_batch=False, ctx=True,
            h0=(zeros_h, zeros_h), prev_kv=kv)
        kv = (k, v)
        new_h.append(jnp.stack([hf_t[Lc - 1].reshape(nbc, D_RNN), hb_t[0].reshape(nbc, D_RNN)], axis=1))

        cache = (cache_k[:, l].reshape(nbl, past, D_ATT), cache_v[:, l].reshape(nbl, past, D_ATT))
        h0 = (state_rglru[:, l, 0].reshape(1, nbl * D_RNN), state_rglru[:, l, 1].reshape(1, nbl * D_RNN))
        xs, _, _, _, _ = _stream_layer(
            xs, mod[l], p, nb=nbl, L=Ll, tm=min(Ll, 512), cond_base=1, cond_per_batch=True, ctx=False,
            cache=cache, h0=h0)

    return (xp.reshape(nbc, Lc, d), xs.reshape(nbl, Ll, d),
            kv[0].reshape(nbc, depth, Lc, N_HEADS, HEAD_DIM), kv[1].reshape(nbc, depth, Lc, N_HEADS, HEAD_DIM),
            jnp.stack(new_h, axis=1))
```

```python
import functools
import math

import numpy as np
import jax
import jax.numpy as jnp
from jax import lax
from jax.experimental import pallas as pl
from jax.experimental.pallas import tpu as pltpu

F32 = jnp.float32
BF16 = jnp.bfloat16
HIGHEST = lax.Precision.HIGHEST

DEPTH = 2
D_MODEL = 1024
N_HEADS = 8
HEAD_DIM = 64
D_ATT = N_HEADS * HEAD_DIM
D_RNN = 256
D_HY = 256
GRID_W = 64
WIN_R = 8
WIN_C = 16
RG_CONV = 4
RG_C = 8.0
HY_CONV = 3
POS_EMB_DIM = 33
N_GROUPS = 4
EXPERTS_PER_GROUP = 4
N_EXPERTS = 16
D_EXPERT = 256
ALPHA = (2.0 * DEPTH) ** 0.25
LN_EPS = 1e-5
NEG = -1e30

Q_ROWS = 4
K_ROWS = Q_ROWS + WIN_R
LANES = 128
HALO = 8
MOE_TILE = 512
MOE_BLK = 128
STRIDE_PAD = 4
FFT_G = 16
FFT_N1 = 64
VMEM_LIMIT = 52 * 1024 * 1024


def _cp(*sem):
    return pltpu.CompilerParams(dimension_semantics=sem, vmem_limit_bytes=VMEM_LIMIT)


def _sigmoid(x):
    return 0.5 * jnp.tanh(0.5 * x) + 0.5


def _ln(x):
    mu = jnp.mean(x, axis=-1, keepdims=True)
    xc = x - mu
    var = jnp.mean(xc * xc, axis=-1, keepdims=True)
    return xc * lax.rsqrt(var + LN_EPS)


def _mod_kernel(c_ref, w_ref, b_ref, o_ref):
    c = c_ref[...]
    s = c * jax.nn.sigmoid(c)
    o_ref[0] = jnp.dot(s, w_ref[0], precision=HIGHEST, preferred_element_type=F32) + b_ref[0]


def _modulation(cond8, w_mod, b_mod):
    depth, d, n = w_mod.shape
    tn = n // 4
    return pl.pallas_call(
        _mod_kernel,
        grid=(depth, n // tn),
        in_specs=[pl.BlockSpec((8, d), lambda l, j: (0, 0)),
                  pl.BlockSpec((1, d, tn), lambda l, j: (l, 0, j)),
                  pl.BlockSpec((1, 1, tn), lambda l, j: (l, 0, j))],
        out_specs=pl.BlockSpec((1, 8, tn), lambda l, j: (l, 0, j)),
        out_shape=jax.ShapeDtypeStruct((depth, 8, n), F32),
        compiler_params=_cp("arbitrary", "arbitrary"),
        name="modulation",
    )(cond8, w_mod, b_mod.reshape(depth, 1, n))


def _inproj_kernel(x_ref, mod_ref, w_ref, *refs, n_prev):
    if n_prev:
        pk_ref, pv_ref = refs[:2]
        refs = refs[2:]
    q_ref, k_ref, v_ref, xr_ref, gt_ref, hv_ref, h1_ref, h2_ref = refs
    for i in range(n_prev):
        k_ref[0, i] = pk_ref[0, i]
        v_ref[0, i] = pv_ref[0, i]
    m = mod_ref[0]
    u = (_ln(x_ref[...]) * (1.0 + m[1:2]) + m[0:1]).astype(BF16)

    def proj(lo, hi):
        return jnp.dot(u, w_ref[:, lo:hi], preferred_element_type=F32)

    o = 0
    q_ref[0] = (proj(o, o + D_ATT) * (HEAD_DIM ** -0.5)).astype(q_ref.dtype)
    o += D_ATT
    k_ref[0, n_prev] = proj(o, o + D_ATT).astype(k_ref.dtype)
    o += D_ATT
    v_ref[0, n_prev] = proj(o, o + D_ATT).astype(v_ref.dtype)
    o += D_ATT
    for ref in (xr_ref, gt_ref, hv_ref, h1_ref, h2_ref):
        ref[...] = proj(o, o + D_RNN).astype(ref.dtype)
        o += D_RNN


def _inproj(x2d, mod_l, w_bf, *, nb, L, tm, cond_base, cond_per_batch, kv_dtype, prev_kv=None):
    tpb = L // tm
    d = x2d.shape[1]
    d_in = w_bf.shape[1]
    n_prev = prev_kv[0].shape[1] if prev_kv else 0
    cond = (lambda i: cond_base + i // tpb) if cond_per_batch else (lambda i: cond_base)
    att_spec = pl.BlockSpec((1, tm, D_ATT), lambda i: (i // tpb, i % tpb, 0))
    kv_spec = lambda n: pl.BlockSpec((1, n, tm, D_ATT), lambda i: (i // tpb, 0, i % tpb, 0))
    t_spec = pl.BlockSpec((tm, D_RNN), lambda i: (i % tpb, i // tpb))
    kv_shape = jax.ShapeDtypeStruct((nb, n_prev + 1, L, D_ATT), kv_dtype)
    t_shape = jax.ShapeDtypeStruct((L, nb * D_RNN), BF16)
    return pl.pallas_call(
        functools.partial(_inproj_kernel, n_prev=n_prev),
        grid=(nb * tpb,),
        in_specs=[pl.BlockSpec((tm, d), lambda i: (i, 0)),
                  pl.BlockSpec((1, 6, d), lambda i: (cond(i), 0, 0)),
                  pl.BlockSpec((d, d_in), lambda i: (0, 0))] + ([kv_spec(n_prev)] * 2 if n_prev else []),
        out_specs=[att_spec, kv_spec(n_prev + 1), kv_spec(n_prev + 1), t_spec, t_spec, t_spec, t_spec, t_spec],
        out_shape=[jax.ShapeDtypeStruct((nb, L, D_ATT), BF16), kv_shape, kv_shape,
                   t_shape, t_shape, t_shape, t_shape, t_shape],
        compiler_params=_cp("arbitrary"),
        name="inproj",
    )(x2d, mod_l, w_bf, *(prev_kv or ()))


def _nt(a, b):
    return lax.dot_general(a, b, (((1,), (1,)), ((), ())), preferred_element_type=F32)


def _ctx_attn_kernel(q_ref, k_ref, v_ref, o_ref):
    q = q_ref[0]
    k = k_ref[0, 0].astype(BF16)
    v = v_ref[0, 0].astype(BF16)
    outs = []
    for h in range(N_HEADS):
        sl = slice(h * HEAD_DIM, (h + 1) * HEAD_DIM)
        s = _nt(q[:, sl], k[:, sl])
        e = jnp.exp(s - jnp.max(s, axis=-1, keepdims=True))
        den = jnp.sum(e, axis=-1, keepdims=True)
        outs.append(jnp.dot(e.astype(BF16), v[:, sl], preferred_element_type=F32) / den)
    o_ref[0] = jnp.concatenate(outs, axis=-1).astype(o_ref.dtype)


def _ctx_attention(q, k, v):
    nb, L, _ = q.shape
    last = k.shape[1] - 1
    spec = pl.BlockSpec((1, L, D_ATT), lambda b: (b, 0, 0))
    kv_spec = pl.BlockSpec((1, 1, L, D_ATT), lambda b: (b, last, 0, 0))
    return pl.pallas_call(
        _ctx_attn_kernel,
        grid=(nb,),
        in_specs=[spec, kv_spec, kv_spec],
        out_specs=spec,
        out_shape=jax.ShapeDtypeStruct((nb, L, D_ATT), BF16),
        compiler_params=_cp("arbitrary"),
        name="ctx_attn",
    )(q, k, v)


N_DR = 2 * WIN_R - 1
PAIR_LEFT = N_DR - 1
PAIR_RIGHT = 2 * N_DR - 1
PAIR_NONE = 3 * N_DR - 1


def _nbr_bias_tiles(rpb):
    cols = np.arange(GRID_W)
    c_start = np.clip(cols - WIN_C // 2, 0, GRID_W - WIN_C)
    col_mask = (cols[None, :] >= c_start[:, None]) & (cols[None, :] < c_start[:, None] + WIN_C)
    dc_idx = np.clip(cols[None, :] - cols[:, None], -(WIN_C - 1), WIN_C - 1) + (WIN_C - 1)
    sel = (dc_idx[None] == np.arange(2 * WIN_C - 1)[:, None, None]).astype(np.float32)
    colb = jnp.einsum('hab,bqk->haqk', rpb.astype(F32), jnp.asarray(sel), precision=HIGHEST)
    colb = jnp.where(col_mask[None, None], colb, NEG)
    neg = jnp.full_like(colb, NEG)
    both = jnp.concatenate([colb[:, :-1], colb[:, 1:]], axis=-1)
    left_masked = jnp.concatenate([neg, colb], axis=-1)
    right_masked = jnp.concatenate([colb, neg], axis=-1)
    none = jnp.concatenate([neg[:, :1], neg[:, :1]], axis=-1)
    return jnp.concatenate([both, left_masked, right_masked, none], axis=1)


def _nbr_attn_kernel(q_ref, k_ref, v_ref, kc_ref, vc_ref, bias_ref, o_ref, *, rows):
    g = pl.program_id(1)
    r0 = jnp.clip(Q_ROWS * g - WIN_R // 2, 0, rows - K_ROWS)
    start = pl.multiple_of(r0 * GRID_W, Q_ROWS * GRID_W)
    nk = K_ROWS * GRID_W
    kw = k_ref[0, 0, pl.ds(start, nk), :]
    vw = v_ref[0, 0, pl.ds(start, nk), :]
    kc = kc_ref[0].astype(BF16)
    vc = vc_ref[0].astype(BF16)
    q = q_ref[0]
    tile = []
    for j in range(Q_ROWS):
        r = Q_ROWS * g + j
        lo = jnp.clip(r - WIN_R // 2, 0, rows - WIN_R) - r + (WIN_R - 1)
        tile.append([])
        for p in range(K_ROWS // 2):
            d0 = r0 + 2 * p - r + (WIN_R - 1)
            lv = (d0 >= lo) & (d0 < lo + WIN_R)
            rv = (d0 + 1 >= lo) & (d0 + 1 < lo + WIN_R)
            idx = jnp.where(lv & rv, d0, jnp.where(rv, PAIR_LEFT + d0 + 1, jnp.where(lv, PAIR_RIGHT + d0, PAIR_NONE)))
            tile[j].append(jnp.clip(idx, 0, PAIR_NONE))
    outs = []
    for h in range(N_HEADS):
        sl = slice(h * HEAD_DIM, (h + 1) * HEAD_DIM)
        qh = q[:, sl]
        bias = jnp.concatenate(
            [jnp.concatenate([bias_ref[h, tile[j][p]] for p in range(K_ROWS // 2)], axis=-1)
             for j in range(Q_ROWS)], axis=0)
        s1 = _nt(qh, kw[:, sl]) + bias
        s2 = _nt(qh, kc[:, sl])
        m = jnp.maximum(jnp.max(s1, axis=-1, keepdims=True), jnp.max(s2, axis=-1, keepdims=True))
        e1 = jnp.exp(s1 - m)
        e2 = jnp.exp(s2 - m)
        den = jnp.sum(e1, axis=-1, keepdims=True) + jnp.sum(e2, axis=-1, keepdims=True)
        o = (jnp.dot(e1.astype(BF16), vw[:, sl], preferred_element_type=F32)
             + jnp.dot(e2.astype(BF16), vc[:, sl], preferred_element_type=F32))
        outs.append(o / den)
    o_ref[0] = jnp.concatenate(outs, axis=-1).astype(o_ref.dtype)


def _nbr_attention(q, k, v, kc, vc, bias):
    nb, L, _ = q.shape
    rows = L // GRID_W
    ng = rows // Q_ROWS
    past = kc.shape[1]
    tq = Q_ROWS * GRID_W
    full = pl.BlockSpec((1, 1, L, D_ATT), lambda b, g: (b, 0, 0, 0))
    ctx = pl.BlockSpec((1, past, D_ATT), lambda b, g: (b, 0, 0))
    qspec = pl.BlockSpec((1, tq, D_ATT), lambda b, g: (b, g, 0))
    bspec = pl.BlockSpec(bias.shape, lambda b, g: (0, 0, 0, 0), pipeline_mode=pl.Buffered(1))
    return pl.pallas_call(
        functools.partial(_nbr_attn_kernel, rows=rows),
        grid=(nb, ng),
        in_specs=[qspec, full, full, ctx, ctx, bspec],
        out_specs=qspec,
        out_shape=jax.ShapeDtypeStruct((nb, L, D_ATT), BF16),
        compiler_params=_cp("arbitrary", "arbitrary"),
        name="nbr_attn",
    )(q, k, v, kc, vc, bias)


def _fill_padded(xp_ref, src_ref, L):
    z = jnp.zeros((HALO, xp_ref.shape[1]), F32)
    xp_ref[0:HALO, :] = z
    xp_ref[HALO:HALO + L, :] = src_ref[...].astype(F32)
    xp_ref[HALO + L:2 * HALO + L, :] = z


def _rg_gates_kernel(x_ref, cw_ref, cb_ref, w_ref, b_ref, lam_ref,
                     af_ref, bf_ref, ab_ref, bb_ref, xp_ref, *, L, tc, m):
    nh = cw_ref.shape[0]
    for sc in range(m):
        half = (pl.program_id(0) * m + sc) % nh
        cols = slice(sc * LANES, (sc + 1) * LANES)
        _fill_padded(xp_ref, x_ref.at[:, cols], L)
        sp = jax.nn.softplus(-lam_ref[half])
        cw = cw_ref[half]
        for t0 in range(0, L, tc):
            xc = cb_ref[half]
            for k in range(RG_CONV):
                xc = xc + cw[k:k + 1, :] * xp_ref[pl.ds(HALO + t0 + k - RG_CONV // 2, tc), :]
            z = jnp.dot(xc.astype(BF16), w_ref[half], preferred_element_type=F32) + b_ref[half]
            for e, (a_ref, o_ref) in enumerate(((af_ref, bf_ref), (ab_ref, bb_ref))):
                r = _sigmoid(z[:, e * LANES:(e + 1) * LANES])
                i = _sigmoid(z[:, (2 + e) * LANES:(3 + e) * LANES])
                log_a = (-RG_C) * r * sp[:, e * LANES:(e + 1) * LANES]
                a = jnp.exp(log_a)
                one_minus_a2 = -jnp.tanh(log_a) * (a * a + 1.0)
                a_ref[pl.ds(t0, tc), cols] = a
                o_ref[pl.ds(t0, tc), cols] = jnp.sqrt(one_minus_a2) * i * xc


def _blockdiag2(w2):
    z = jnp.zeros_like(w2[0])
    return jnp.concatenate([jnp.concatenate([w2[0], z], 1), jnp.concatenate([z, w2[1]], 1)], 0)


def _rg_weights(wa, ba, wx, bx, lam):
    ws, bs, ls = [], [], []
    for hf in range(D_RNN // LANES):
        blk = slice(2 * hf, 2 * hf + 2)
        ch = slice(hf * LANES, (hf + 1) * LANES)
        ws.append(jnp.concatenate([_blockdiag2(wa[0, blk]), _blockdiag2(wa[1, blk]),
                                   _blockdiag2(wx[0, blk]), _blockdiag2(wx[1, blk])], axis=1))
        bs.append(jnp.concatenate([ba[0, ch], ba[1, ch], bx[0, ch], bx[1, ch]])[None])
        ls.append(jnp.concatenate([lam[0, ch], lam[1, ch]])[None])
    return jnp.stack(ws).astype(BF16), jnp.stack(bs), jnp.stack(ls)


def _lane_chunks(L, cols):
    m = max(1, min(8, (1 << 20) // (L * LANES * 4)))
    while cols % (m * LANES):
        m //= 2
    return m


def _rg_gates(xr_t, conv_w, conv_b, w_bd, b_cat, lam_cat):
    L, cols = xr_t.shape
    nh = D_RNN // LANES
    tc = min(L, 512)
    m = _lane_chunks(L, cols)
    col = pl.BlockSpec((L, m * LANES), lambda c: (0, c))
    const = lambda a: pl.BlockSpec(a.shape, lambda c: (0,) * a.ndim)
    shape = jax.ShapeDtypeStruct((L, cols), F32)
    cw = conv_w.reshape(RG_CONV, nh, LANES).transpose(1, 0, 2)
    cb = conv_b.reshape(nh, 1, LANES)
    return pl.pallas_call(
        functools.partial(_rg_gates_kernel, L=L, tc=tc, m=m),
        grid=(cols // (m * LANES),),
        in_specs=[col, const(cw), const(cb), const(w_bd), const(b_cat), const(lam_cat)],
        out_specs=[col, col, col, col],
        out_shape=[shape, shape, shape, shape],
        scratch_shapes=[pltpu.VMEM((L + 2 * HALO, LANES), F32)],
        compiler_params=_cp("arbitrary"),
        name="rg_gates",
    )(xr_t, cw, cb, w_bd, b_cat, lam_cat)


def _rg_scan_kernel(af_ref, bf_ref, ab_ref, bb_ref, h0f_ref, h0b_ref, hf_ref, hb_ref, cf_ref, cb_ref, *, tc):
    @pl.when(pl.program_id(1) == 0)
    def _():
        cf_ref[...] = h0f_ref[...]
        cb_ref[...] = h0b_ref[...]

    def body(t, carry):
        hf, hb = carry
        hf = af_ref[pl.ds(t, 1), :] * hf + bf_ref[pl.ds(t, 1), :]
        hf_ref[pl.ds(t, 1), :] = hf
        tb = tc - 1 - t
        hb = ab_ref[pl.ds(tb, 1), :] * hb + bb_ref[pl.ds(tb, 1), :]
        hb_ref[pl.ds(tb, 1), :] = hb
        return hf, hb

    hf, hb = lax.fori_loop(0, tc, body, (cf_ref[...], cb_ref[...]), unroll=8)
    cf_ref[...] = hf
    cb_ref[...] = hb


def _rg_scan(a_f, b_f, a_b, b_b, h0f, h0b):
    L, cols = a_f.shape
    cw = min(cols, 1024)
    tc = min(L, 512)
    nt = L // tc
    fwd = pl.BlockSpec((tc, cw), lambda c, i: (i, c))
    bwd = pl.BlockSpec((tc, cw), lambda c, i: (nt - 1 - i, c))
    row = pl.BlockSpec((1, cw), lambda c, i: (0, c))
    shape = jax.ShapeDtypeStruct((L, cols), F32)
    return pl.pallas_call(
        functools.partial(_rg_scan_kernel, tc=tc),
        grid=(cols // cw, nt),
        in_specs=[fwd, fwd, bwd, bwd, row, row],
        out_specs=[fwd, bwd],
        out_shape=[shape, shape],
        scratch_shapes=[pltpu.VMEM((1, cw), F32), pltpu.VMEM((1, cw), F32)],
        compiler_params=_cp("arbitrary", "arbitrary"),
        name="rg_scan",
    )(a_f, b_f, a_b, b_b, h0f, h0b)


def _hy_filter_kernel(feat_ref, w1_ref, b1_ref, w2_ref, b2_ref, w3_ref, fr_ref, dec_ref, o_ref, *, tm):
    feats = feat_ref[...]
    dot = functools.partial(jnp.dot, precision=HIGHEST, preferred_element_type=F32)
    h = jnp.sin(fr_ref[0:1, :] * (dot(feats, w1_ref[...]) + b1_ref[...]))
    h = jnp.sin(fr_ref[1:2, :] * (dot(h, w2_ref[...]) + b2_ref[...]))
    h = dot(h, w3_ref[...])
    filt = h * jnp.exp(-feats[:, 0:1] * jnp.abs(dec_ref[...]))
    row = pl.program_id(0) * tm + lax.broadcasted_iota(jnp.int32, filt.shape, 0)
    lane = lax.broadcasted_iota(jnp.int32, filt.shape, 1)
    o_ref[...] = jnp.where((row == 0) & (lane >= 2 * D_HY), 0.0, filt)


def _hy_features(L):
    pos = np.arange(L, dtype=np.float64)
    t = np.linspace(0.0, 1.0, L)
    bands = (POS_EMB_DIM - 1) // 2
    f = np.linspace(1e-4, bands - 1, bands)
    ang = (2.0 * math.pi / L) * pos[:, None] * f[None, :]
    feats = np.concatenate([t[:, None], np.cos(ang), -np.sin(ang)], axis=-1)
    out = np.zeros((L, LANES), np.float32)
    out[:, :POS_EMB_DIM] = feats
    return out


def _two_level(L):
    return L % (FFT_N1 * 8) == 0 and L >= 2048


def _digit_swap(x, L):
    n2 = 2 * L // FFT_N1
    return x.reshape(L // n2, n2, -1).transpose(1, 0, 2).reshape(L, -1)


def _hy_filters(L, w1, b1, w2, b2, w3, freq, decay):
    tm = min(L, 512)
    hid = w2.shape[0]
    n = w3.shape[1]
    w1p = jnp.zeros((LANES, hid), F32).at[:POS_EMB_DIM].set(w1)
    feats = _hy_features(L)
    if _two_level(L):
        feats = _digit_swap(feats, L)
    c = lambda s: pl.BlockSpec(s, lambda i: (0, 0))
    return pl.pallas_call(
        functools.partial(_hy_filter_kernel, tm=tm),
        grid=(L // tm,),
        in_specs=[pl.BlockSpec((tm, LANES), lambda i: (i, 0)),
                  c((LANES, hid)), c((1, hid)), c((hid, hid)), c((1, hid)), c((hid, n)), c((2, hid)), c((1, n))],
        out_specs=pl.BlockSpec((tm, n), lambda i: (i, 0)),
        out_shape=jax.ShapeDtypeStruct((L, n), F32),
        compiler_params=_cp("arbitrary"),
        name="hy_filter",
    )(jnp.asarray(feats), w1p, b1[None], w2, b2[None], w3, freq, decay.reshape(1, n))


def _conv3_kernel(v_ref, a_ref, b_ref, w_ref, cb_ref, ov_ref, oa_ref, ob_ref, xp_ref, xs_ref,
                  *, L, tc, swap, n2, m):
    h1 = L // n2
    pitch = n2 + STRIDE_PAD
    nh = w_ref.shape[0]
    for sc in range(m):
        half = (pl.program_id(0) * m + sc) % nh
        cols = slice(sc * LANES, (sc + 1) * LANES)
        for p, (src, dst) in enumerate(((v_ref, ov_ref), (a_ref, oa_ref), (b_ref, ob_ref))):
            _fill_padded(xp_ref, src.at[:, cols], L)
            w = w_ref[half, p]
            cb = cb_ref[half, p]
            if swap[p]:
                for i1 in range(h1):
                    xs_ref[pl.ds(i1 * pitch, n2 + HY_CONV - 1), :] = (
                        xp_ref[pl.ds(HALO + i1 * n2 - HY_CONV // 2, n2 + HY_CONV - 1), :])

                def body(i2, carry, dst=dst, w=w, cb=cb):
                    y = cb
                    for k in range(HY_CONV):
                        y = y + w[k:k + 1, :] * xs_ref[pl.ds(i2 + k, h1, stride=pitch), :]
                    dst[i2] = y
                    return carry

                lax.fori_loop(0, n2, body, 0)
            else:
                for t0 in range(0, L, tc):
                    y = cb
                    for k in range(HY_CONV):
                        y = y + w[k:k + 1, :] * xp_ref[pl.ds(HALO + t0 + k - HY_CONV // 2, tc), :]
                    dst[pl.ds(t0, tc), cols] = y


def _hy_conv3(v_t, a_t, b_t, conv_w, conv_b, swap=(False, False, False)):
    L, cols = v_t.shape
    nh = D_HY // LANES
    tc = min(L, 512)
    n2 = 2 * L // FFT_N1
    m = 1 if any(swap) else _lane_chunks(L, cols)
    col = pl.BlockSpec((L, m * LANES), lambda c: (0, c))
    col3 = pl.BlockSpec((n2, L // n2, LANES), lambda c: (0, 0, c))
    shape = jax.ShapeDtypeStruct((L, cols), F32)
    shape3 = jax.ShapeDtypeStruct((n2, L // n2, cols), F32)
    w = conv_w.reshape(HY_CONV, 3, nh, LANES).transpose(2, 1, 0, 3)
    cb = conv_b.reshape(3, nh, 1, LANES).transpose(1, 0, 2, 3)
    const = lambda a: pl.BlockSpec(a.shape, lambda c: (0,) * a.ndim)
    return pl.pallas_call(
        functools.partial(_conv3_kernel, L=L, tc=tc, swap=swap, n2=n2, m=m),
        grid=(cols // (m * LANES),),
        in_specs=[col, col, col, const(w), const(cb)],
        out_specs=[col3 if s else col for s in swap],
        out_shape=[shape3 if s else shape for s in swap],
        scratch_shapes=[pltpu.VMEM((L + 2 * HALO, LANES), F32),
                        pltpu.VMEM(((L // n2) * (n2 + STRIDE_PAD) + HALO, LANES), F32)],
        compiler_params=_cp("arbitrary"),
        name="hy_conv3",
    )(v_t, a_t, b_t, w, cb)


def _phase(p, n2):
    ang = 2.0 * np.pi * (np.asarray(p, np.int64) % n2) / n2
    return np.cos(ang), np.sin(ang)


def _dft_direct_tables(L):
    k = np.arange(L)[:, None]
    t = np.arange(L)[None, :]
    c, s = _phase((2 * k + 1) * t, 4 * L)
    fwd = np.concatenate([c, -s], axis=0)
    inv = (1.0 / L) * np.concatenate([c.T, -s.T], axis=1)
    return fwd.astype(np.float32), inv.astype(np.float32)


def _dft_two_level_tables(L, n1):
    N = 2 * L
    n2 = N // n1
    h1 = n1 // 2
    i2 = np.arange(n2)[:, None, None]
    k1 = np.arange(n1)[None, :, None]
    i1 = np.arange(h1)[None, None, :]
    c, s = _phase(2 * n2 * i1 * k1 + 2 * i2 * k1 + n2 * i1 + i2, 2 * N)
    t1 = np.concatenate([c, -s], axis=1)
    ct, st = c.transpose(0, 2, 1), s.transpose(0, 2, 1)
    t1i = (2.0 / N) * np.concatenate([ct, -st], axis=2)
    kk = np.arange(n2 // 2)[:, None]
    nn = np.arange(n2)[None, :]
    c2, s2 = _phase(nn * kk, n2)
    t2 = np.block([[c2, s2], [-s2, c2]])
    t2i = np.block([[c2.T, -s2.T], [s2.T, c2.T]])
    return tuple(x.astype(np.float32) for x in (t1, t2, t2i, t1i))


def _filter_spectrum(fre, fim, order, skip):
    fw = slice(order * D_HY, (order + 1) * D_HY)
    bw = slice((2 + order) * D_HY, (3 + order) * D_HY)
    return fre[:, fw] + fre[:, bw] + skip, fim[:, fw] - fim[:, bw]


def _spectral_product(xr, xi, fr, fi, nbc):
    yr, yi = [], []
    for b in range(nbc):
        sl = slice(b * D_HY, (b + 1) * D_HY)
        yr.append(xr[:, sl] * fr - xi[:, sl] * fi)
        yi.append(xr[:, sl] * fi + xi[:, sl] * fr)
    return jnp.concatenate([jnp.concatenate(yr, axis=1), jnp.concatenate(yi, axis=1)], axis=0)


def _matmul_kernel(a_ref, b_ref, o_ref):
    o_ref[...] = jnp.dot(a_ref[...], b_ref[...], precision=HIGHEST, preferred_element_type=F32)


def _hy_direct_kernel(z_ref, g_ref, f_ref, skip_ref, t_ref, ti_ref, o_ref, *, L, nbc, order):
    x = jnp.dot(t_ref[...], z_ref[...].astype(BF16), preferred_element_type=F32)
    fr, fi = _filter_spectrum(f_ref[0:L, :], f_ref[L:2 * L, :], order, skip_ref[0])
    y = _spectral_product(x[:L], x[L:], fr, fi, nbc).astype(BF16)
    conv = jnp.dot(ti_ref[...], y, preferred_element_type=F32)
    o_ref[...] = (g_ref[...] * conv).astype(o_ref.dtype)


def _hyena_direct(z, g1, g2, filt, skip):
    L, cols = z.shape
    fwd, inv = _dft_direct_tables(L)
    n = filt.shape[1]
    spec = pl.pallas_call(
        _matmul_kernel,
        out_shape=jax.ShapeDtypeStruct((2 * L, n), F32),
        name="hy_fspec",
    )(jnp.asarray(fwd), filt)
    tn = min(cols, 4 * D_HY)
    nbc = tn // D_HY
    col = pl.BlockSpec((L, tn), lambda c: (0, c))
    const = lambda s: pl.BlockSpec(s, lambda c: (0, 0))
    t_bf, ti_bf = jnp.asarray(fwd).astype(BF16), jnp.asarray(inv).astype(BF16)
    for order, g in enumerate((g1, g2)):
        z = pl.pallas_call(
            functools.partial(_hy_direct_kernel, L=L, nbc=nbc, order=order),
            grid=(cols // tn,),
            in_specs=[col, col, const((2 * L, n)), pl.BlockSpec((1, 1, D_HY), lambda c: (order, 0, 0)),
                      const((2 * L, L)), const((L, 2 * L))],
            out_specs=col,
            out_shape=jax.ShapeDtypeStruct((L, cols), F32 if order == 0 else BF16),
            compiler_params=_cp("arbitrary"),
            name="hy_direct",
        )(z, g, spec, skip[:, None, :], t_bf, ti_bf)
    return z


def _kron_tables(t1, t1i, G):
    n2, two_n1, h1 = t1.shape
    ng = n2 // G
    eye = np.eye(two_n1, dtype=np.float32)
    rep_rows = jnp.asarray(np.repeat(eye, G, axis=0))
    rep_cols = jnp.asarray(np.repeat(eye, G, axis=1))
    c_fwd = t1.reshape(ng, G, two_n1, h1).transpose(0, 2, 1, 3).reshape(ng, two_n1, G * h1)
    c_swapped = t1i.reshape(ng, G * h1, two_n1)
    c_natural = t1i.reshape(ng, G, h1, two_n1).transpose(0, 2, 1, 3).reshape(ng, h1 * G, two_n1)

    def expand(compact, rep, left, row_key, lane_key):
        r, c = (rep.shape[0], compact.shape[2]) if left else (compact.shape[1], rep.shape[1])
        return pl.pallas_call(
            functools.partial(_kron_expand_kernel, left=left, row_key=row_key, lane_key=lane_key),
            grid=(ng,),
            in_specs=[pl.BlockSpec((1,) + compact.shape[1:], lambda i: (i, 0, 0)),
                      pl.BlockSpec(rep.shape, lambda i: (0, 0))],
            out_specs=pl.BlockSpec((1, r, c), lambda i: (i, 0, 0)),
            out_shape=jax.ShapeDtypeStruct((ng, r, c), BF16),
            compiler_params=_cp("arbitrary"),
            name="hy_kron",
        )(jnp.asarray(compact), rep)

    fwd = expand(c_fwd, rep_rows, True, lambda r: r % G, lambda c: c // h1)
    inv_swapped = expand(c_swapped, rep_cols, False, lambda r: r // h1, lambda c: c % G)
    inv_natural = expand(c_natural, rep_cols, False, lambda r: r % G, lambda c: c % G)
    return fwd, inv_swapped, inv_natural


def _kron_expand_kernel(c_ref, rep_ref, o_ref, *, left, row_key, lane_key):
    comp = c_ref[0].astype(BF16)
    rep = rep_ref[...].astype(BF16)
    full = (jnp.dot(rep, comp, preferred_element_type=F32) if left
            else jnp.dot(comp, rep, preferred_element_type=F32))
    rows = lax.broadcasted_iota(jnp.int32, full.shape, 0)
    lanes = lax.broadcasted_iota(jnp.int32, full.shape, 1)
    o_ref[0] = jnp.where(row_key(rows) == lane_key(lanes), full, 0.0).astype(BF16)


def _hy_stage1_kernel(x_ref, t_ref, re_ref, im_ref, *, G, n1, ct):
    h1, cols = x_ref.shape[1], x_ref.shape[2]
    x = x_ref[...].reshape(G * h1, cols).astype(BF16)
    for c0 in range(0, cols, ct):
        a = jnp.dot(t_ref[0], x[:, c0:c0 + ct], preferred_element_type=F32)
        re_ref[:, :, c0:c0 + ct] = a[:n1 * G].reshape(n1, G, ct).astype(BF16)
        im_ref[:, :, c0:c0 + ct] = a[n1 * G:].reshape(n1, G, ct).astype(BF16)


def _hy_stage1(x3, t_fwd, *, G):
    n2, h1, cols = x3.shape
    n1 = t_fwd.shape[1] // (2 * G)
    ospec = pl.BlockSpec((n1, G, cols), lambda j: (0, j, 0))
    oshape = jax.ShapeDtypeStruct((n1, n2, cols), BF16)
    return pl.pallas_call(
        functools.partial(_hy_stage1_kernel, G=G, n1=n1, ct=D_HY),
        grid=(n2 // G,),
        in_specs=[pl.BlockSpec((G, h1, cols), lambda j: (j, 0, 0)),
                  pl.BlockSpec((1,) + t_fwd.shape[1:], lambda j: (j, 0, 0))],
        out_specs=[ospec, ospec],
        out_shape=[oshape, oshape],
        compiler_params=_cp("arbitrary"),
        name="hy_stage1",
    )(x3, t_fwd)


def _hy_fspec2_kernel(re_ref, im_ref, t2_ref, fre_ref, fim_ref, *, kb, k2):
    for j in range(kb):
        a2 = jnp.concatenate([re_ref[j], im_ref[j]], axis=0)
        x = jnp.dot(t2_ref[...], a2, preferred_element_type=F32)
        fre_ref[j] = x[:k2]
        fim_ref[j] = x[k2:]


def _hy_stage2_kernel(re_ref, im_ref, fre_ref, fim_ref, skip_ref, t2_ref, t2i_ref, ore_ref, oim_ref,
                      *, kb, k2, n2, nbc, order):
    for j in range(kb):
        a2 = jnp.concatenate([re_ref[j], im_ref[j]], axis=0)
        x = jnp.dot(t2_ref[...], a2, preferred_element_type=F32)
        fr, fi = _filter_spectrum(fre_ref[j], fim_ref[j], order, skip_ref[0])
        y = _spectral_product(x[:k2], x[k2:], fr, fi, nbc).astype(BF16)
        a = jnp.dot(t2i_ref[...], y, preferred_element_type=F32)
        ore_ref[j] = a[:n2].astype(BF16)
        oim_ref[j] = a[n2:].astype(BF16)


def _hy_stage3_kernel(re_ref, im_ref, t_ref, g_ref, o_ref, *, G, n1, ct):
    cols = re_ref.shape[2]
    for c0 in range(0, cols, ct):
        sl = slice(c0, c0 + ct)
        a2 = jnp.concatenate([re_ref[:, :, sl].reshape(n1 * G, ct), im_ref[:, :, sl].reshape(n1 * G, ct)], axis=0)
        conv = jnp.dot(t_ref[0], a2, preferred_element_type=F32)
        g = g_ref[:, :, sl]
        o_ref[:, :, sl] = (g * conv.reshape(g.shape)).astype(o_ref.dtype)


def _two_level_tables(L, *, n1=FFT_N1, G=FFT_G):
    t1, t2, t2i, t1i = _dft_two_level_tables(L, n1)
    return (jnp.asarray(t2).astype(BF16), jnp.asarray(t2i).astype(BF16)) + _kron_tables(t1, t1i, G)


def _hyena_two_level(z3, g1_3, g2, filt, skip, tables, *, n1=FFT_N1, G=FFT_G, kb=4):
    n2, h1, cols = z3.shape
    L = n2 * h1
    k2 = n2 // 2
    nbc = cols // D_HY
    t2_bf, t2i_bf, t_fwd, t_inv_swapped, t_inv_natural = tables
    nf = filt.shape[1]

    f_re, f_im = _hy_stage1(filt.reshape(n2, h1, nf), t_fwd, G=G)
    blk = lambda c: pl.BlockSpec((kb, n2, c), lambda i: (i, 0, 0))
    hblk = lambda c: pl.BlockSpec((kb, k2, c), lambda i: (i, 0, 0))
    const = lambda s: pl.BlockSpec(s, lambda i: (0, 0))
    fshape = jax.ShapeDtypeStruct((n1, k2, nf), F32)
    fre, fim = pl.pallas_call(
        functools.partial(_hy_fspec2_kernel, kb=kb, k2=k2),
        grid=(n1 // kb,),
        in_specs=[blk(nf), blk(nf), const((n2, 2 * n2))],
        out_specs=[hblk(nf), hblk(nf)],
        out_shape=[fshape, fshape],
        compiler_params=_cp("arbitrary"),
        name="hy_fspec2",
    )(f_re, f_im, t2_bf)

    ashape = jax.ShapeDtypeStruct((n1, n2, cols), BF16)
    grp = pl.BlockSpec((n1, G, cols), lambda j: (0, j, 0))
    tspec = lambda t: pl.BlockSpec((1,) + t.shape[1:], lambda j: (j, 0, 0))
    for order in range(2):
        a_re, a_im = _hy_stage1(z3, t_fwd, G=G)
        b_re, b_im = pl.pallas_call(
            functools.partial(_hy_stage2_kernel, kb=kb, k2=k2, n2=n2, nbc=nbc, order=order),
            grid=(n1 // kb,),
            in_specs=[blk(cols), blk(cols), hblk(nf), hblk(nf),
                      pl.BlockSpec((1, 1, D_HY), lambda i: (order, 0, 0)),
                      const((n2, 2 * n2)), const((2 * n2, n2))],
            out_specs=[blk(cols), blk(cols)],
            out_shape=[ashape, ashape],
            compiler_params=_cp("arbitrary"),
            name="hy_stage2",
        )(a_re, a_im, fre, fim, skip[:, None, :], t2_bf, t2i_bf)
        if order == 0:
            table, gate = t_inv_swapped, g1_3
            io = pl.BlockSpec((G, h1, cols), lambda j: (j, 0, 0))
            oshape = jax.ShapeDtypeStruct((n2, h1, cols), F32)
        else:
            table, gate = t_inv_natural, g2.reshape(h1, n2, cols)
            io = pl.BlockSpec((h1, G, cols), lambda j: (0, j, 0))
            oshape = jax.ShapeDtypeStruct((h1, n2, cols), BF16)
        z3 = pl.pallas_call(
            functools.partial(_hy_stage3_kernel, G=G, n1=n1, ct=D_HY),
            grid=(n2 // G,),
            in_specs=[grp, grp, tspec(table), io],
            out_specs=io,
            out_shape=oshape,
            compiler_params=_cp("arbitrary"),
            name="hy_stage3",
        )(b_re, b_im, table, gate)
    return z3.reshape(L, cols)


def _hyena(hv_t, h1_t, h2_t, conv_w, conv_b, filt, skip, tables=None):
    L, cols = hv_t.shape
    if _two_level(L):
        z3, g1_3, g2 = _hy_conv3(hv_t, h1_t, h2_t, conv_w, conv_b, swap=(True, True, False))
        return _hyena_two_level(z3, g1_3, g2, filt, skip, tables or _two_level_tables(L))
    z, g1, g2 = _hy_conv3(hv_t, h1_t, h2_t, conv_w, conv_b)
    return _hyena_direct(z, g1, g2, filt, skip)


def _route(logits):
    lane_i = lax.broadcasted_iota(jnp.int32, logits.shape, 1)
    lane = lane_i.astype(F32)
    is_g = (lane_i >= N_EXPERTS) & (lane_i < N_EXPERTS + N_GROUPS)
    big = float(1 << 20)
    gl = jnp.where(is_g, logits, NEG)
    gmax = jnp.max(gl, axis=-1, keepdims=True)
    gidx = jnp.min(jnp.where(gl == gmax, lane - N_EXPERTS, big), axis=-1, keepdims=True)
    p_group = 1.0 / jnp.sum(jnp.where(is_g, jnp.exp(gl - gmax), 0.0), axis=-1, keepdims=True)
    sel = (lane_i < N_EXPERTS) & ((lane_i // EXPERTS_PER_GROUP) == gidx.astype(jnp.int32))
    el = jnp.where(sel, logits, NEG)
    ee = jnp.where(sel, jnp.exp(el - jnp.max(el, axis=-1, keepdims=True)), 0.0)
    pe = ee / jnp.sum(ee, axis=-1, keepdims=True)
    p1 = jnp.max(jnp.where(sel, pe, -1.0), axis=-1, keepdims=True)
    i1 = jnp.min(jnp.where(sel & (pe == p1), lane, big), axis=-1, keepdims=True)
    rest = sel & (lane != i1)
    p2 = jnp.max(jnp.where(rest, pe, -1.0), axis=-1, keepdims=True)
    i2 = jnp.min(jnp.where(rest & (pe == p2), lane, big), axis=-1, keepdims=True)
    den = p1 + p2
    cmb = jnp.where(lane == i1, p_group * (p1 / den), jnp.where(lane == i2, p_group * (p2 / den), 0.0))
    return jnp.where(lane_i == N_EXPERTS, gidx, cmb)


def _outproj_kernel(att_ref, hf_ref, hb_ref, gt_ref, hy_ref, x_ref, mod_ref, w_ref, g_ref, b_ref,
                    wr_ref, br_ref, x1_ref, u2_ref, cmb_ref):
    rg = (hf_ref[...] + hb_ref[...]) * jax.nn.gelu(gt_ref[...].astype(F32))
    mix = (jnp.dot(att_ref[0], w_ref[0:D_ATT, :], preferred_element_type=F32)
           + jnp.dot(rg.astype(BF16), w_ref[D_ATT:D_ATT + D_RNN, :], preferred_element_type=F32)
           + jnp.dot(hy_ref[...].astype(BF16), w_ref[D_ATT + D_RNN:, :], preferred_element_type=F32))
    m = mod_ref[0]
    x1 = _ln(ALPHA * x_ref[...] + m[2:3] * mix) * g_ref[...] + b_ref[...]
    u2 = _ln(x1) * (1.0 + m[4:5]) + m[3:4]
    u_hi = u2.astype(BF16)
    u_lo = (u2 - u_hi.astype(F32)).astype(BF16)
    wr = wr_ref[...]
    w_hi = wr.astype(BF16)
    w_lo = (wr - w_hi.astype(F32)).astype(BF16)
    hi = jnp.dot(u_hi, jnp.concatenate([w_hi, w_lo], axis=1), preferred_element_type=F32)
    lo = jnp.dot(u_lo, w_hi, preferred_element_type=F32)
    logits = hi[:, 0:LANES] + hi[:, LANES:] + lo + br_ref[...]
    x1_ref[...] = x1
    u2_ref[...] = u_hi
    cmb_ref[...] = _route(logits)


def _outproj(att, hf_t, hb_t, gt_t, hy_t, x2d, mod_l, w_bf, ln_g, ln_b, wr, br, *, nb, L, tm,
             cond_base, cond_per_batch):
    tpb = L // tm
    d = x2d.shape[1]
    cond = (lambda i: cond_base + i // tpb) if cond_per_batch else (lambda i: cond_base)
    t_spec = pl.BlockSpec((tm, D_RNN), lambda i: (i % tpb, i // tpb))
    tok = lambda c: pl.BlockSpec((tm, c), lambda i: (i, 0))
    const = lambda s: pl.BlockSpec(s, lambda i: (0, 0))
    n = nb * L
    return pl.pallas_call(
        _outproj_kernel,
        grid=(nb * tpb,),
        in_specs=[pl.BlockSpec((1, tm, D_ATT), lambda i: (i // tpb, i % tpb, 0)),
                  t_spec, t_spec, t_spec, t_spec, tok(d),
                  pl.BlockSpec((1, 6, d), lambda i: (cond(i), 0, 0)),
                  const((d, d)), const((1, d)), const((1, d)), const((d, LANES)), const((1, LANES))],
        out_specs=[tok(d), tok(d), tok(LANES)],
        out_shape=[jax.ShapeDtypeStruct((n, d), F32), jax.ShapeDtypeStruct((n, d), BF16),
                   jax.ShapeDtypeStruct((n, LANES), F32)],
        compiler_params=_cp("arbitrary"),
        name="outproj",
    )(att, hf_t, hb_t, gt_t, hy_t, x2d, mod_l, w_bf, ln_g[None], ln_b[None], wr, br)


def _moe_kernel(u_ref, cmb_ref, x1_ref, mod_ref, wgu_ref, wd_ref, g_ref, b_ref, o_ref, ys_ref, *, T, BLK):
    SL = T + N_GROUPS * BLK
    u = u_ref[...]
    cmb = cmb_ref[...]
    lane_f = lax.broadcasted_iota(jnp.int32, (T, LANES), 1).astype(F32)
    gid_col = jnp.sum(jnp.where(lane_f == float(N_EXPERTS), cmb, 0.0), axis=-1, keepdims=True)
    onehot = jnp.where(lane_f == gid_col, 1.0, 0.0)
    ri = lax.broadcasted_iota(jnp.int32, (T, T), 0)
    ci = lax.broadcasted_iota(jnp.int32, (T, T), 1)
    before = jnp.where(ci < ri, 1.0, 0.0).astype(BF16)
    rank_col = jnp.sum(onehot * jnp.dot(before, onehot.astype(BF16), preferred_element_type=F32),
                       axis=-1, keepdims=True)
    counts = jnp.sum(onehot, axis=0, keepdims=True)
    lane_row = lax.broadcasted_iota(jnp.int32, (1, LANES), 1)
    base = jnp.int32(0)
    bases = []
    base_row = jnp.zeros((1, LANES), jnp.int32)
    for c in range(N_GROUPS):
        n_c = jnp.sum(jnp.where(lane_row == c, counts, 0.0)).astype(jnp.int32)
        bases.append(base)
        base_row = jnp.where(lane_row == c, base, base_row)
        base = base + ((n_c + (BLK - 1)) // BLK) * BLK
    total = base
    dest_f = jnp.sum(onehot * base_row.astype(F32), axis=-1, keepdims=True) + rank_col
    dest_col = dest_f.astype(jnp.int32)
    dest_row = jnp.broadcast_to(dest_f, (T, LANES)).T[0:1, :].astype(jnp.int32)
    cmb_hi = cmb.astype(BF16)
    cmb_lo = (cmb - cmb_hi.astype(F32)).astype(BF16)
    blk_rows = lax.broadcasted_iota(jnp.int32, (BLK, T), 0)
    blk_lane = lax.broadcasted_iota(jnp.int32, (BLK, LANES), 1)

    def block(b, carry):
        r0 = b * BLK
        rows = pl.ds(pl.multiple_of(r0, BLK), BLK)

        @pl.when(r0 < total)
        def _():
            grp = ((r0 >= bases[1]).astype(jnp.int32) + (r0 >= bases[2]).astype(jnp.int32)
                   + (r0 >= bases[3]).astype(jnp.int32))
            p = jnp.where(blk_rows + r0 == dest_row, 1.0, 0.0).astype(BF16)
            xb = jnp.dot(p, u, preferred_element_type=F32).astype(BF16)
            wb = (jnp.dot(p, cmb_hi, preferred_element_type=F32)
                  + jnp.dot(p, cmb_lo, preferred_element_type=F32))
            gu = jnp.dot(xb, wgu_ref[grp], preferred_element_type=F32)
            hs = []
            for j in range(EXPERTS_PER_GROUP):
                gate = gu[:, 2 * j * D_EXPERT:(2 * j + 1) * D_EXPERT]
                up = gu[:, (2 * j + 1) * D_EXPERT:(2 * j + 2) * D_EXPERT]
                w = jnp.sum(jnp.where(blk_lane == grp * EXPERTS_PER_GROUP + j, wb, 0.0), axis=-1, keepdims=True)
                hs.append((gate * _sigmoid(gate) * up * w).astype(BF16))
            ys_ref[rows, :] = jnp.dot(jnp.concatenate(hs, axis=1), wd_ref[grp],
                                      preferred_element_type=F32).astype(BF16)

        @pl.when(r0 >= total)
        def _():
            ys_ref[rows, :] = jnp.zeros((BLK, u.shape[1]), BF16)

        return carry

    lax.fori_loop(0, SL // BLK, block, 0)
    p_t = jnp.where(lax.broadcasted_iota(jnp.int32, (T, SL), 1) == dest_col, 1.0, 0.0).astype(BF16)
    moe = jnp.dot(p_t, ys_ref[...], preferred_element_type=F32)
    m = mod_ref[0]
    o_ref[...] = _ln(ALPHA * x1_ref[...] + m[5:6] * moe) * g_ref[...] + b_ref[...]


def _group_gate_up(w_gate, w_up):
    _, d, f = w_gate.shape
    gu = jnp.concatenate([w_gate, w_up], axis=-1).astype(BF16).reshape(N_GROUPS, EXPERTS_PER_GROUP, d, 2 * f)
    return gu.transpose(0, 2, 1, 3).reshape(N_GROUPS, d, EXPERTS_PER_GROUP * 2 * f)


def _moe(u2, cmb, x1, mod_l, wgu, wd, ln_g, ln_b, *, tm, tiles_per_cond, cond_base, cond_per_batch):
    n, d = x1.shape
    cond = (lambda i: cond_base + i // tiles_per_cond) if cond_per_batch else (lambda i: cond_base)
    tok = lambda c: pl.BlockSpec((tm, c), lambda i: (i, 0))
    const = lambda s: pl.BlockSpec(s, lambda i: (0,) * len(s))
    resident = lambda s: pl.BlockSpec(s, lambda i: (0,) * len(s), pipeline_mode=pl.Buffered(1))
    return pl.pallas_call(
        functools.partial(_moe_kernel, T=tm, BLK=MOE_BLK),
        grid=(n // tm,),
        in_specs=[tok(d), tok(LANES), tok(d),
                  pl.BlockSpec((1, 6, d), lambda i: (cond(i), 0, 0)),
                  resident(wgu.shape), resident(wd.shape),
                  const((1, d)), const((1, d))],
        out_specs=tok(d),
        out_shape=jax.ShapeDtypeStruct((n, d), F32),
        scratch_shapes=[pltpu.VMEM((tm + N_GROUPS * MOE_BLK, d), BF16)],
        compiler_params=_cp("arbitrary"),
        name="moe",
    )(u2, cmb, x1, mod_l, wgu, wd, ln_g[None], ln_b[None])


def _stream_layer(x2d, mod_l, p, *, nb, L, tm, cond_base, cond_per_batch, ctx, cache=None, h0=None, prev_kv=None):
    kw = dict(nb=nb, L=L, tm=tm, cond_base=cond_base, cond_per_batch=cond_per_batch)
    q, k, v, xr_t, gt_t, hv_t, h1_t, h2_t = _inproj(x2d, mod_l, p["w_in"], kv_dtype=F32 if ctx else BF16,
                                                    prev_kv=prev_kv, **kw)
    if ctx:
        att = _ctx_attention(q, k, v)
    else:
        att = _nbr_attention(q, k, v, cache[0], cache[1], p["nbr_bias"])
    a_f, b_f, a_b, b_b = _rg_gates(xr_t, p["rg_conv_w"], p["rg_conv_b"], *p["rg_w"])
    hf_t, hb_t = _rg_scan(a_f, b_f, a_b, b_b, h0[0], h0[1])
    hy_t = _hyena(hv_t, h1_t, h2_t, p["hy_conv_w"], p["hy_conv_b"], p["hy_filt"][L], p["hy_skip"],
                  p["hy_tables"].get(L))
    x1, u2, cmb = _outproj(att, hf_t, hb_t, gt_t, hy_t, x2d, mod_l, p["w_out"], p["ln1_g"], p["ln1_b"],
                           p["router_w"], p["router_b"], **kw)
    tm_moe = min(MOE_TILE, nb * L)
    x2 = _moe(u2, cmb, x1, mod_l, p["moe_wgu"], p["moe_wd"], p["ln2_g"], p["ln2_b"], tm=tm_moe,
              tiles_per_cond=max(L // tm_moe, 1), cond_base=cond_base, cond_per_batch=cond_per_batch)
    return x2, k, v, hf_t, hb_t


def kernel(x_prompt, x_sample, cache_k, cache_v, state_rglru, c, c_ctx, w_mod, b_mod, w_in, attn_rpb, rg_conv_w, rg_conv_b, rg_wa, rg_ba, rg_wx, rg_bx, rg_lambda, hy_conv_w, hy_conv_b, hy_w1, hy_b1, hy_w2, hy_b2, hy_w3, hy_freq, hy_decay, hy_skip, w_out, ln1_g, ln1_b, router_wg, router_bg, router_we, router_be, moe_w_gate, moe_w_up, moe_w_down, ln2_g, ln2_b):
    nbc, Lc, d = x_prompt.shape
    nbl, Ll, _ = x_sample.shape
    depth = w_mod.shape[0]
    past = cache_k.shape[2]

    cond8 = jnp.zeros((8, d), F32).at[0].set(c_ctx).at[1:1 + nbl].set(c)
    mod = _modulation(cond8, w_mod, b_mod).reshape(depth, 8, 6, d)

    xp = x_prompt.reshape(nbc * Lc, d)
    xs = x_sample.reshape(nbl * Ll, d)
    kv, new_h = None, []
    hy_tables = {L: _two_level_tables(L) for L in {Lc, Ll} if _two_level(L)}
    for l in range(depth):
        pad = jnp.zeros((d, LANES - N_EXPERTS - N_GROUPS), F32)
        p = {
            "w_in": w_in[l].astype(BF16),
            "nbr_bias": _nbr_bias_tiles(attn_rpb[l]),
            "rg_conv_w": rg_conv_w[l], "rg_conv_b": rg_conv_b[l],
            "rg_w": _rg_weights(rg_wa[l], rg_ba[l], rg_wx[l], rg_bx[l], rg_lambda[l]),
            "hy_conv_w": hy_conv_w[l], "hy_conv_b": hy_conv_b[l], "hy_skip": hy_skip[l], "hy_tables": hy_tables,
            "hy_filt": {L: _hy_filters(L, hy_w1[l], hy_b1[l], hy_w2[l], hy_b2[l], hy_w3[l], hy_freq[l], hy_decay[l])
                        for L in sorted({Lc, Ll})},
            "w_out": w_out[l].astype(BF16), "ln1_g": ln1_g[l], "ln1_b": ln1_b[l],
            "router_w": jnp.concatenate([router_we[l], router_wg[l], pad], axis=1),
            "router_b": jnp.concatenate([router_be[l], router_bg[l], pad[0]])[None],
            "moe_wgu": _group_gate_up(moe_w_gate[l], moe_w_up[l]),
            "moe_wd": moe_w_down[l].astype(BF16).reshape(N_GROUPS, EXPERTS_PER_GROUP * D_EXPERT, d),
            "ln2_g": ln2_g[l], "ln2_b": ln2_b[l],
        }
        zeros_h = jnp.zeros((1, nbc * D_RNN), F32)
        xp, k, v, hf_t, hb_t = _stream_layer(
            xp, mod[l], p, nb=nbc, L=Lc, tm=Lc, cond_base=0, cond_per_batch=False, ctx=True,
            h0=(zeros_h, zeros_h), prev_kv=kv)
        kv = (k, v)
        new_h.append(jnp.stack([hf_t[Lc - 1].reshape(nbc, D_RNN), hb_t[0].reshape(nbc, D_RNN)], axis=1))

        cache = (cache_k[:, l].reshape(nbl, past, D_ATT), cache_v[:, l].reshape(nbl, past, D_ATT))
        h0 = (state_rglru[:, l, 0].reshape(1, nbl * D_RNN), state_rglru[:, l, 1].reshape(1, nbl * D_RNN))
        xs, _, _, _, _ = _stream_layer(
            xs, mod[l], p, nb=nbl, L=Ll, tm=min(Ll, 512), cond_base=1, cond_per_batch=True, ctx=False,
            cache=cache, h0=h0)

    return (xp.reshape(nbc, Lc, d), xs.reshape(nbl, Ll, d),
            kv[0].reshape(nbc, depth, Lc, N_HEADS, HEAD_DIM), kv[1].reshape(nbc, depth, Lc, N_HEADS, HEAD_DIM),
            jnp.stack(new_h, axis=1))
```

```python
import functools
import math

import numpy as np
import jax
import jax.numpy as jnp
from jax import lax
from jax.experimental import pallas as pl
from jax.experimental.pallas import tpu as pltpu

F32 = jnp.float32
BF16 = jnp.bfloat16
HIGHEST = lax.Precision.HIGHEST

DEPTH = 2
D_MODEL = 1024
N_HEADS = 8
HEAD_DIM = 64
D_ATT = N_HEADS * HEAD_DIM
D_RNN = 256
D_HY = 256
GRID_W = 64
WIN_R = 8
WIN_C = 16
RG_CONV = 4
RG_C = 8.0
HY_CONV = 3
POS_EMB_DIM = 33
N_GROUPS = 4
EXPERTS_PER_GROUP = 4
N_EXPERTS = 16
D_EXPERT = 256
ALPHA = (2.0 * DEPTH) ** 0.25
LN_EPS = 1e-5
NEG = -1e30

Q_ROWS = 4
K_ROWS = Q_ROWS + WIN_R
LANES = 128
HALO = 8
INPROJ_TILE = 1024
MOE_TILE = 512
MOE_BLK = 128
STRIDE_PAD = 4
FFT_G = 16
FFT_N1 = 64
VMEM_LIMIT = 52 * 1024 * 1024


def _cp(*sem):
    return pltpu.CompilerParams(dimension_semantics=sem, vmem_limit_bytes=VMEM_LIMIT)


def _sigmoid(x):
    return 0.5 * jnp.tanh(0.5 * x) + 0.5


def _ln(x):
    mu = jnp.mean(x, axis=-1, keepdims=True)
    xc = x - mu
    var = jnp.mean(xc * xc, axis=-1, keepdims=True)
    return xc * lax.rsqrt(var + LN_EPS)


def _mod_kernel(c_ref, w_ref, b_ref, o_ref):
    c = c_ref[...]
    s = c * jax.nn.sigmoid(c)
    o_ref[0] = jnp.dot(s, w_ref[0], precision=HIGHEST, preferred_element_type=F32) + b_ref[0]


def _modulation(cond8, w_mod, b_mod):
    depth, d, n = w_mod.shape
    tn = n // 4
    return pl.pallas_call(
        _mod_kernel,
        grid=(depth, n // tn),
        in_specs=[pl.BlockSpec((8, d), lambda l, j: (0, 0)),
                  pl.BlockSpec((1, d, tn), lambda l, j: (l, 0, j)),
                  pl.BlockSpec((1, 1, tn), lambda l, j: (l, 0, j))],
        out_specs=pl.BlockSpec((1, 8, tn), lambda l, j: (l, 0, j)),
        out_shape=jax.ShapeDtypeStruct((depth, 8, n), F32),
        compiler_params=_cp("arbitrary", "arbitrary"),
        name="modulation",
    )(cond8, w_mod, b_mod.reshape(depth, 1, n))


def _inproj_kernel(x_ref, mod_ref, w_ref, *refs, n_prev):
    if n_prev:
        pk_ref, pv_ref = refs[:2]
        refs = refs[2:]
    q_ref, k_ref, v_ref, xr_ref, gt_ref, hv_ref, h1_ref, h2_ref = refs
    for i in range(n_prev):
        k_ref[0, i] = pk_ref[0, i]
        v_ref[0, i] = pv_ref[0, i]
    m = mod_ref[0]
    u = (_ln(x_ref[...]) * (1.0 + m[1:2]) + m[0:1]).astype(BF16)

    def proj(lo, hi):
        return jnp.dot(u, w_ref[:, lo:hi], preferred_element_type=F32)

    o = 0
    q_ref[0] = (proj(o, o + D_ATT) * (HEAD_DIM ** -0.5)).astype(q_ref.dtype)
    o += D_ATT
    k_ref[0, n_prev] = proj(o, o + D_ATT).astype(k_ref.dtype)
    o += D_ATT
    v_ref[0, n_prev] = proj(o, o + D_ATT).astype(v_ref.dtype)
    o += D_ATT
    for ref in (xr_ref, gt_ref, hv_ref, h1_ref, h2_ref):
        ref[...] = proj(o, o + D_RNN).astype(ref.dtype)
        o += D_RNN


def _inproj(x2d, mod_l, w_bf, *, nb, L, tm, cond_base, cond_per_batch, kv_dtype, prev_kv=None):
    tpb = L // tm
    d = x2d.shape[1]
    d_in = w_bf.shape[1]
    n_prev = prev_kv[0].shape[1] if prev_kv else 0
    cond = (lambda i: cond_base + i // tpb) if cond_per_batch else (lambda i: cond_base)
    att_spec = pl.BlockSpec((1, tm, D_ATT), lambda i: (i // tpb, i % tpb, 0))
    kv_spec = lambda n: pl.BlockSpec((1, n, tm, D_ATT), lambda i: (i // tpb, 0, i % tpb, 0))
    t_spec = pl.BlockSpec((tm, D_RNN), lambda i: (i % tpb, i // tpb))
    kv_shape = jax.ShapeDtypeStruct((nb, n_prev + 1, L, D_ATT), kv_dtype)
    t_shape = jax.ShapeDtypeStruct((L, nb * D_RNN), BF16)
    return pl.pallas_call(
        functools.partial(_inproj_kernel, n_prev=n_prev),
        grid=(nb * tpb,),
        in_specs=[pl.BlockSpec((tm, d), lambda i: (i, 0)),
                  pl.BlockSpec((1, 6, d), lambda i: (cond(i), 0, 0)),
                  pl.BlockSpec((d, d_in), lambda i: (0, 0))] + ([kv_spec(n_prev)] * 2 if n_prev else []),
        out_specs=[att_spec, kv_spec(n_prev + 1), kv_spec(n_prev + 1), t_spec, t_spec, t_spec, t_spec, t_spec],
        out_shape=[jax.ShapeDtypeStruct((nb, L, D_ATT), BF16), kv_shape, kv_shape,
                   t_shape, t_shape, t_shape, t_shape, t_shape],
        compiler_params=_cp("arbitrary"),
        name="inproj",
    )(x2d, mod_l, w_bf, *(prev_kv or ()))


def _nt(a, b):
    return lax.dot_general(a, b, (((1,), (1,)), ((), ())), preferred_element_type=F32)


def _ctx_attn_kernel(q_ref, k_ref, v_ref, o_ref):
    q = q_ref[0]
    k = k_ref[0, 0].astype(BF16)
    v = v_ref[0, 0].astype(BF16)
    outs = []
    for h in range(N_HEADS):
        sl = slice(h * HEAD_DIM, (h + 1) * HEAD_DIM)
        s = _nt(q[:, sl], k[:, sl])
        e = jnp.exp(s - jnp.max(s, axis=-1, keepdims=True))
        den = jnp.sum(e, axis=-1, keepdims=True)
        outs.append(jnp.dot(e.astype(BF16), v[:, sl], preferred_element_type=F32) / den)
    o_ref[0] = jnp.concatenate(outs, axis=-1).astype(o_ref.dtype)


def _ctx_attention(q, k, v):
    nb, L, _ = q.shape
    last = k.shape[1] - 1
    spec = pl.BlockSpec((1, L, D_ATT), lambda b: (b, 0, 0))
    kv_spec = pl.BlockSpec((1, 1, L, D_ATT), lambda b: (b, last, 0, 0))
    return pl.pallas_call(
        _ctx_attn_kernel,
        grid=(nb,),
        in_specs=[spec, kv_spec, kv_spec],
        out_specs=spec,
        out_shape=jax.ShapeDtypeStruct((nb, L, D_ATT), BF16),
        compiler_params=_cp("arbitrary"),
        name="ctx_attn",
    )(q, k, v)


N_DR = 2 * WIN_R - 1
PAIR_LEFT = N_DR - 1
PAIR_RIGHT = 2 * N_DR - 1
PAIR_NONE = 3 * N_DR - 1


def _nbr_bias_tiles(rpb):
    cols = np.arange(GRID_W)
    c_start = np.clip(cols - WIN_C // 2, 0, GRID_W - WIN_C)
    col_mask = (cols[None, :] >= c_start[:, None]) & (cols[None, :] < c_start[:, None] + WIN_C)
    dc_idx = np.clip(cols[None, :] - cols[:, None], -(WIN_C - 1), WIN_C - 1) + (WIN_C - 1)
    sel = (dc_idx[None] == np.arange(2 * WIN_C - 1)[:, None, None]).astype(np.float32)
    colb = jnp.einsum('hab,bqk->haqk', rpb.astype(F32), jnp.asarray(sel), precision=HIGHEST)
    colb = jnp.where(col_mask[None, None], colb, NEG)
    neg = jnp.full_like(colb, NEG)
    both = jnp.concatenate([colb[:, :-1], colb[:, 1:]], axis=-1)
    left_masked = jnp.concatenate([neg, colb], axis=-1)
    right_masked = jnp.concatenate([colb, neg], axis=-1)
    none = jnp.concatenate([neg[:, :1], neg[:, :1]], axis=-1)
    return jnp.concatenate([both, left_masked, right_masked, none], axis=1)


def _nbr_attn_kernel(q_ref, k_ref, v_ref, kc_ref, vc_ref, bias_ref, o_ref, *, rows):
    g = pl.program_id(1)
    r0 = jnp.clip(Q_ROWS * g - WIN_R // 2, 0, rows - K_ROWS)
    start = pl.multiple_of(r0 * GRID_W, Q_ROWS * GRID_W)
    nk = K_ROWS * GRID_W
    kw = k_ref[0, 0, pl.ds(start, nk), :]
    vw = v_ref[0, 0, pl.ds(start, nk), :]
    kc = kc_ref[0].astype(BF16)
    vc = vc_ref[0].astype(BF16)
    q = q_ref[0]
    tile = []
    for j in range(Q_ROWS):
        r = Q_ROWS * g + j
        lo = jnp.clip(r - WIN_R // 2, 0, rows - WIN_R) - r + (WIN_R - 1)
        tile.append([])
        for p in range(K_ROWS // 2):
            d0 = r0 + 2 * p - r + (WIN_R - 1)
            lv = (d0 >= lo) & (d0 < lo + WIN_R)
            rv = (d0 + 1 >= lo) & (d0 + 1 < lo + WIN_R)
            idx = jnp.where(lv & rv, d0, jnp.where(rv, PAIR_LEFT + d0 + 1, jnp.where(lv, PAIR_RIGHT + d0, PAIR_NONE)))
            tile[j].append(jnp.clip(idx, 0, PAIR_NONE))
    outs = []
    for h in range(N_HEADS):
        sl = slice(h * HEAD_DIM, (h + 1) * HEAD_DIM)
        qh = q[:, sl]
        bias = jnp.concatenate(
            [jnp.concatenate([bias_ref[h, tile[j][p]] for p in range(K_ROWS // 2)], axis=-1)
             for j in range(Q_ROWS)], axis=0)
        s1 = _nt(qh, kw[:, sl]) + bias
        s2 = _nt(qh, kc[:, sl])
        m = jnp.maximum(jnp.max(s1, axis=-1, keepdims=True), jnp.max(s2, axis=-1, keepdims=True))
        e1 = jnp.exp(s1 - m)
        e2 = jnp.exp(s2 - m)
        den = jnp.sum(e1, axis=-1, keepdims=True) + jnp.sum(e2, axis=-1, keepdims=True)
        o = (jnp.dot(e1.astype(BF16), vw[:, sl], preferred_element_type=F32)
             + jnp.dot(e2.astype(BF16), vc[:, sl], preferred_element_type=F32))
        outs.append(o / den)
    o_ref[0] = jnp.concatenate(outs, axis=-1).astype(o_ref.dtype)


def _nbr_attention(q, k, v, kc, vc, bias):
    nb, L, _ = q.shape
    rows = L // GRID_W
    ng = rows // Q_ROWS
    past = kc.shape[1]
    tq = Q_ROWS * GRID_W
    full = pl.BlockSpec((1, 1, L, D_ATT), lambda b, g: (b, 0, 0, 0))
    ctx = pl.BlockSpec((1, past, D_ATT), lambda b, g: (b, 0, 0))
    qspec = pl.BlockSpec((1, tq, D_ATT), lambda b, g: (b, g, 0))
    bspec = pl.BlockSpec(bias.shape, lambda b, g: (0, 0, 0, 0), pipeline_mode=pl.Buffered(1))
    return pl.pallas_call(
        functools.partial(_nbr_attn_kernel, rows=rows),
        grid=(nb, ng),
        in_specs=[qspec, full, full, ctx, ctx, bspec],
        out_specs=qspec,
        out_shape=jax.ShapeDtypeStruct((nb, L, D_ATT), BF16),
        compiler_params=_cp("arbitrary", "arbitrary"),
        name="nbr_attn",
    )(q, k, v, kc, vc, bias)


def _fill_padded(xp_ref, src_ref, L):
    z = jnp.zeros((HALO, xp_ref.shape[1]), F32)
    xp_ref[0:HALO, :] = z
    xp_ref[HALO:HALO + L, :] = src_ref[...].astype(F32)
    xp_ref[HALO + L:2 * HALO + L, :] = z


def _rg_gates_kernel(x_ref, cw_ref, cb_ref, w_ref, b_ref, lam_ref,
                     af_ref, bf_ref, ab_ref, bb_ref, xp_ref, *, L, tc, m):
    nh = cw_ref.shape[0]
    for sc in range(m):
        half = (pl.program_id(0) * m + sc) % nh
        cols = slice(sc * LANES, (sc + 1) * LANES)
        _fill_padded(xp_ref, x_ref.at[:, cols], L)
        sp = jax.nn.softplus(-lam_ref[half])
        cw = cw_ref[half]
        for t0 in range(0, L, tc):
            xc = cb_ref[half]
            for k in range(RG_CONV):
                xc = xc + cw[k:k + 1, :] * xp_ref[pl.ds(HALO + t0 + k - RG_CONV // 2, tc), :]
            z = jnp.dot(xc.astype(BF16), w_ref[half], preferred_element_type=F32) + b_ref[half]
            for e, (a_ref, o_ref) in enumerate(((af_ref, bf_ref), (ab_ref, bb_ref))):
                r = _sigmoid(z[:, e * LANES:(e + 1) * LANES])
                i = _sigmoid(z[:, (2 + e) * LANES:(3 + e) * LANES])
                log_a = (-RG_C) * r * sp[:, e * LANES:(e + 1) * LANES]
                a = jnp.exp(log_a)
                one_minus_a2 = -jnp.tanh(log_a) * (a * a + 1.0)
                a_ref[pl.ds(t0, tc), cols] = a
                o_ref[pl.ds(t0, tc), cols] = jnp.sqrt(one_minus_a2) * i * xc


def _blockdiag2(w2):
    z = jnp.zeros_like(w2[0])
    return jnp.concatenate([jnp.concatenate([w2[0], z], 1), jnp.concatenate([z, w2[1]], 1)], 0)


def _rg_weights(wa, ba, wx, bx, lam):
    ws, bs, ls = [], [], []
    for hf in range(D_RNN // LANES):
        blk = slice(2 * hf, 2 * hf + 2)
        ch = slice(hf * LANES, (hf + 1) * LANES)
        ws.append(jnp.concatenate([_blockdiag2(wa[0, blk]), _blockdiag2(wa[1, blk]),
                                   _blockdiag2(wx[0, blk]), _blockdiag2(wx[1, blk])], axis=1))
        bs.append(jnp.concatenate([ba[0, ch], ba[1, ch], bx[0, ch], bx[1, ch]])[None])
        ls.append(jnp.concatenate([lam[0, ch], lam[1, ch]])[None])
    return jnp.stack(ws).astype(BF16), jnp.stack(bs), jnp.stack(ls)


def _lane_chunks(L, cols):
    m = max(1, min(8, (1 << 20) // (L * LANES * 4)))
    while cols % (m * LANES):
        m //= 2
    return m


def _rg_gates(xr_t, conv_w, conv_b, w_bd, b_cat, lam_cat):
    L, cols = xr_t.shape
    nh = D_RNN // LANES
    tc = min(L, 512)
    m = _lane_chunks(L, cols)
    col = pl.BlockSpec((L, m * LANES), lambda c: (0, c))
    const = lambda a: pl.BlockSpec(a.shape, lambda c: (0,) * a.ndim)
    shape = jax.ShapeDtypeStruct((L, cols), F32)
    cw = conv_w.reshape(RG_CONV, nh, LANES).transpose(1, 0, 2)
    cb = conv_b.reshape(nh, 1, LANES)
    return pl.pallas_call(
        functools.partial(_rg_gates_kernel, L=L, tc=tc, m=m),
        grid=(cols // (m * LANES),),
        in_specs=[col, const(cw), const(cb), const(w_bd), const(b_cat), const(lam_cat)],
        out_specs=[col, col, col, col],
        out_shape=[shape, shape, shape, shape],
        scratch_shapes=[pltpu.VMEM((L + 2 * HALO, LANES), F32)],
        compiler_params=_cp("arbitrary"),
        name="rg_gates",
    )(xr_t, cw, cb, w_bd, b_cat, lam_cat)


def _rg_scan_kernel(af_ref, bf_ref, ab_ref, bb_ref, h0f_ref, h0b_ref, hf_ref, hb_ref, cf_ref, cb_ref, *, tc):
    @pl.when(pl.program_id(1) == 0)
    def _():
        cf_ref[...] = h0f_ref[...]
        cb_ref[...] = h0b_ref[...]

    def body(t, carry):
        hf, hb = carry
        hf = af_ref[pl.ds(t, 1), :] * hf + bf_ref[pl.ds(t, 1), :]
        hf_ref[pl.ds(t, 1), :] = hf
        tb = tc - 1 - t
        hb = ab_ref[pl.ds(tb, 1), :] * hb + bb_ref[pl.ds(tb, 1), :]
        hb_ref[pl.ds(tb, 1), :] = hb
        return hf, hb

    hf, hb = lax.fori_loop(0, tc, body, (cf_ref[...], cb_ref[...]), unroll=8)
    cf_ref[...] = hf
    cb_ref[...] = hb


def _rg_scan(a_f, b_f, a_b, b_b, h0f, h0b):
    L, cols = a_f.shape
    cw = min(cols, 1024)
    tc = min(L, 512)
    nt = L // tc
    fwd = pl.BlockSpec((tc, cw), lambda c, i: (i, c))
    bwd = pl.BlockSpec((tc, cw), lambda c, i: (nt - 1 - i, c))
    row = pl.BlockSpec((1, cw), lambda c, i: (0, c))
    shape = jax.ShapeDtypeStruct((L, cols), F32)
    return pl.pallas_call(
        functools.partial(_rg_scan_kernel, tc=tc),
        grid=(cols // cw, nt),
        in_specs=[fwd, fwd, bwd, bwd, row, row],
        out_specs=[fwd, bwd],
        out_shape=[shape, shape],
        scratch_shapes=[pltpu.VMEM((1, cw), F32), pltpu.VMEM((1, cw), F32)],
        compiler_params=_cp("arbitrary", "arbitrary"),
        name="rg_scan",
    )(a_f, b_f, a_b, b_b, h0f, h0b)


def _hy_filter_kernel(feat_ref, w1_ref, b1_ref, w2_ref, b2_ref, w3_ref, fr_ref, dec_ref, o_ref, *, tm):
    feats = feat_ref[...]
    dot = functools.partial(jnp.dot, precision=HIGHEST, preferred_element_type=F32)
    h = jnp.sin(fr_ref[0:1, :] * (dot(feats, w1_ref[...]) + b1_ref[...]))
    h = jnp.sin(fr_ref[1:2, :] * (dot(h, w2_ref[...]) + b2_ref[...]))
    h = dot(h, w3_ref[...])
    filt = h * jnp.exp(-feats[:, 0:1] * jnp.abs(dec_ref[...]))
    row = pl.program_id(0) * tm + lax.broadcasted_iota(jnp.int32, filt.shape, 0)
    lane = lax.broadcasted_iota(jnp.int32, filt.shape, 1)
    o_ref[...] = jnp.where((row == 0) & (lane >= 2 * D_HY), 0.0, filt)


def _hy_features(L):
    pos = np.arange(L, dtype=np.float64)
    t = np.linspace(0.0, 1.0, L)
    bands = (POS_EMB_DIM - 1) // 2
    f = np.linspace(1e-4, bands - 1, bands)
    ang = (2.0 * math.pi / L) * pos[:, None] * f[None, :]
    feats = np.concatenate([t[:, None], np.cos(ang), -np.sin(ang)], axis=-1)
    out = np.zeros((L, LANES), np.float32)
    out[:, :POS_EMB_DIM] = feats
    return out


def _two_level(L):
    return L % (FFT_N1 * 8) == 0 and L >= 2048


def _digit_swap(x, L):
    n2 = 2 * L // FFT_N1
    return x.reshape(L // n2, n2, -1).transpose(1, 0, 2).reshape(L, -1)


def _hy_filters(L, w1, b1, w2, b2, w3, freq, decay):
    tm = min(L, 512)
    hid = w2.shape[0]
    n = w3.shape[1]
    w1p = jnp.zeros((LANES, hid), F32).at[:POS_EMB_DIM].set(w1)
    feats = _hy_features(L)
    if _two_level(L):
        feats = _digit_swap(feats, L)
    c = lambda s: pl.BlockSpec(s, lambda i: (0, 0))
    return pl.pallas_call(
        functools.partial(_hy_filter_kernel, tm=tm),
        grid=(L // tm,),
        in_specs=[pl.BlockSpec((tm, LANES), lambda i: (i, 0)),
                  c((LANES, hid)), c((1, hid)), c((hid, hid)), c((1, hid)), c((hid, n)), c((2, hid)), c((1, n))],
        out_specs=pl.BlockSpec((tm, n), lambda i: (i, 0)),
        out_shape=jax.ShapeDtypeStruct((L, n), F32),
        compiler_params=_cp("arbitrary"),
        name="hy_filter",
    )(jnp.asarray(feats), w1p, b1[None], w2, b2[None], w3, freq, decay.reshape(1, n))


def _conv3_kernel(v_ref, a_ref, b_ref, w_ref, cb_ref, ov_ref, oa_ref, ob_ref, xp_ref, xs_ref,
                  *, L, tc, swap, n2, m):
    h1 = L // n2
    pitch = n2 + STRIDE_PAD
    nh = w_ref.shape[0]
    for sc in range(m):
        half = (pl.program_id(0) * m + sc) % nh
        cols = slice(sc * LANES, (sc + 1) * LANES)
        for p, (src, dst) in enumerate(((v_ref, ov_ref), (a_ref, oa_ref), (b_ref, ob_ref))):
            _fill_padded(xp_ref, src.at[:, cols], L)
            w = w_ref[half, p]
            cb = cb_ref[half, p]
            if swap[p]:
                for i1 in range(h1):
                    xs_ref[pl.ds(i1 * pitch, n2 + HY_CONV - 1), :] = (
                        xp_ref[pl.ds(HALO + i1 * n2 - HY_CONV // 2, n2 + HY_CONV - 1), :])

                def body(i2, carry, dst=dst, w=w, cb=cb):
                    y = cb
                    for k in range(HY_CONV):
                        y = y + w[k:k + 1, :] * xs_ref[pl.ds(i2 + k, h1, stride=pitch), :]
                    dst[i2] = y.astype(dst.dtype)
                    return carry

                lax.fori_loop(0, n2, body, 0)
            else:
                for t0 in range(0, L, tc):
                    y = cb
                    for k in range(HY_CONV):
                        y = y + w[k:k + 1, :] * xp_ref[pl.ds(HALO + t0 + k - HY_CONV // 2, tc), :]
                    dst[pl.ds(t0, tc), cols] = y


def _hy_conv3(v_t, a_t, b_t, conv_w, conv_b, swap=(False, False, False)):
    L, cols = v_t.shape
    nh = D_HY // LANES
    tc = min(L, 512)
    n2 = 2 * L // FFT_N1
    m = 1 if any(swap) else _lane_chunks(L, cols)
    col = pl.BlockSpec((L, m * LANES), lambda c: (0, c))
    col3 = pl.BlockSpec((n2, L // n2, LANES), lambda c: (0, 0, c))
    shape = jax.ShapeDtypeStruct((L, cols), F32)
    shape3 = jax.ShapeDtypeStruct((n2, L // n2, cols), F32)
    w = conv_w.reshape(HY_CONV, 3, nh, LANES).transpose(2, 1, 0, 3)
    cb = conv_b.reshape(3, nh, 1, LANES).transpose(1, 0, 2, 3)
    const = lambda a: pl.BlockSpec(a.shape, lambda c: (0,) * a.ndim)
    return pl.pallas_call(
        functools.partial(_conv3_kernel, L=L, tc=tc, swap=swap, n2=n2, m=m),
        grid=(cols // (m * LANES),),
        in_specs=[col, col, col, const(w), const(cb)],
        out_specs=[col3 if s else col for s in swap],
        out_shape=[shape, shape, shape] if not any(swap) else
                  [jax.ShapeDtypeStruct(shape3.shape, BF16 if i == 0 else F32) if s else shape
                   for i, s in enumerate(swap)],
        scratch_shapes=[pltpu.VMEM((L + 2 * HALO, LANES), F32),
                        pltpu.VMEM(((L // n2) * (n2 + STRIDE_PAD) + HALO, LANES), F32)],
        compiler_params=_cp("arbitrary"),
        name="hy_conv3",
    )(v_t, a_t, b_t, w, cb)


def _phase(p, n2):
    ang = 2.0 * np.pi * (np.asarray(p, np.int64) % n2) / n2
    return np.cos(ang), np.sin(ang)


def _dft_direct_tables(L):
    k = np.arange(L)[:, None]
    t = np.arange(L)[None, :]
    c, s = _phase((2 * k + 1) * t, 4 * L)
    fwd = np.concatenate([c, -s], axis=0)
    inv = (1.0 / L) * np.concatenate([c.T, -s.T], axis=1)
    return fwd.astype(np.float32), inv.astype(np.float32)


def _dft_two_level_tables(L, n1):
    N = 2 * L
    n2 = N // n1
    h1 = n1 // 2
    i2 = np.arange(n2)[:, None, None]
    k1 = np.arange(n1)[None, :, None]
    i1 = np.arange(h1)[None, None, :]
    c, s = _phase(2 * n2 * i1 * k1 + 2 * i2 * k1 + n2 * i1 + i2, 2 * N)
    t1 = np.concatenate([c, -s], axis=1)
    ct, st = c.transpose(0, 2, 1), s.transpose(0, 2, 1)
    t1i = (2.0 / N) * np.concatenate([ct, -st], axis=2)
    kk = np.arange(n2 // 2)[:, None]
    nn = np.arange(n2)[None, :]
    c2, s2 = _phase(nn * kk, n2)
    t2 = np.block([[c2, s2], [-s2, c2]])
    t2i = np.block([[c2.T, -s2.T], [s2.T, c2.T]])
    return tuple(x.astype(np.float32) for x in (t1, t2, t2i, t1i))


def _filter_spectrum(fre, fim, order, skip):
    fw = slice(order * D_HY, (order + 1) * D_HY)
    bw = slice((2 + order) * D_HY, (3 + order) * D_HY)
    return fre[:, fw] + fre[:, bw] + skip, fim[:, fw] - fim[:, bw]


def _spectral_product(xr, xi, fr, fi, nbc):
    yr, yi = [], []
    for b in range(nbc):
        sl = slice(b * D_HY, (b + 1) * D_HY)
        yr.append(xr[:, sl] * fr - xi[:, sl] * fi)
        yi.append(xr[:, sl] * fi + xi[:, sl] * fr)
    return jnp.concatenate([jnp.concatenate(yr, axis=1), jnp.concatenate(yi, axis=1)], axis=0)


def _matmul_kernel(a_ref, b_ref, o_ref):
    o_ref[...] = jnp.dot(a_ref[...], b_ref[...], precision=HIGHEST, preferred_element_type=F32)


def _hy_direct_kernel(z_ref, g_ref, f_ref, skip_ref, t_ref, ti_ref, o_ref, *, L, nbc, order):
    x = jnp.dot(t_ref[...], z_ref[...].astype(BF16), preferred_element_type=F32)
    fr, fi = _filter_spectrum(f_ref[0:L, :], f_ref[L:2 * L, :], order, skip_ref[0])
    y = _spectral_product(x[:L], x[L:], fr, fi, nbc).astype(BF16)
    conv = jnp.dot(ti_ref[...], y, preferred_element_type=F32)
    o_ref[...] = (g_ref[...] * conv).astype(o_ref.dtype)


def _hyena_direct(z, g1, g2, filt, skip):
    L, cols = z.shape
    fwd, inv = _dft_direct_tables(L)
    n = filt.shape[1]
    spec = pl.pallas_call(
        _matmul_kernel,
        out_shape=jax.ShapeDtypeStruct((2 * L, n), F32),
        name="hy_fspec",
    )(jnp.asarray(fwd), filt)
    tn = min(cols, 4 * D_HY)
    nbc = tn // D_HY
    col = pl.BlockSpec((L, tn), lambda c: (0, c))
    const = lambda s: pl.BlockSpec(s, lambda c: (0, 0))
    t_bf, ti_bf = jnp.asarray(fwd).astype(BF16), jnp.asarray(inv).astype(BF16)
    for order, g in enumerate((g1, g2)):
        z = pl.pallas_call(
            functools.partial(_hy_direct_kernel, L=L, nbc=nbc, order=order),
            grid=(cols // tn,),
            in_specs=[col, col, const((2 * L, n)), pl.BlockSpec((1, 1, D_HY), lambda c: (order, 0, 0)),
                      const((2 * L, L)), const((L, 2 * L))],
            out_specs=col,
            out_shape=jax.ShapeDtypeStruct((L, cols), F32 if order == 0 else BF16),
            compiler_params=_cp("arbitrary"),
            name="hy_direct",
        )(z, g, spec, skip[:, None, :], t_bf, ti_bf)
    return z


def _kron_tables(t1, t1i, G):
    n2, two_n1, h1 = t1.shape
    ng = n2 // G
    eye = np.eye(two_n1, dtype=np.float32)
    rep_rows = jnp.asarray(np.repeat(eye, G, axis=0))
    rep_cols = jnp.asarray(np.repeat(eye, G, axis=1))
    c_fwd = t1.reshape(ng, G, two_n1, h1).transpose(0, 2, 1, 3).reshape(ng, two_n1, G * h1)
    c_swapped = t1i.reshape(ng, G * h1, two_n1)
    c_natural = t1i.reshape(ng, G, h1, two_n1).transpose(0, 2, 1, 3).reshape(ng, h1 * G, two_n1)

    def expand(compact, rep, left, row_key, lane_key):
        r, c = (rep.shape[0], compact.shape[2]) if left else (compact.shape[1], rep.shape[1])
        return pl.pallas_call(
            functools.partial(_kron_expand_kernel, left=left, row_key=row_key, lane_key=lane_key),
            grid=(ng,),
            in_specs=[pl.BlockSpec((1,) + compact.shape[1:], lambda i: (i, 0, 0)),
                      pl.BlockSpec(rep.shape, lambda i: (0, 0))],
            out_specs=pl.BlockSpec((1, r, c), lambda i: (i, 0, 0)),
            out_shape=jax.ShapeDtypeStruct((ng, r, c), BF16),
            compiler_params=_cp("arbitrary"),
            name="hy_kron",
        )(jnp.asarray(compact), rep)

    fwd = expand(c_fwd, rep_rows, True, lambda r: r % G, lambda c: c // h1)
    inv_swapped = expand(c_swapped, rep_cols, False, lambda r: r // h1, lambda c: c % G)
    inv_natural = expand(c_natural, rep_cols, False, lambda r: r % G, lambda c: c % G)
    return fwd, inv_swapped, inv_natural


def _kron_expand_kernel(c_ref, rep_ref, o_ref, *, left, row_key, lane_key):
    comp = c_ref[0].astype(BF16)
    rep = rep_ref[...].astype(BF16)
    full = (jnp.dot(rep, comp, preferred_element_type=F32) if left
            else jnp.dot(comp, rep, preferred_element_type=F32))
    rows = lax.broadcasted_iota(jnp.int32, full.shape, 0)
    lanes = lax.broadcasted_iota(jnp.int32, full.shape, 1)
    o_ref[0] = jnp.where(row_key(rows) == lane_key(lanes), full, 0.0).astype(BF16)


def _hy_stage1_kernel(x_ref, t_ref, re_ref, im_ref, *, G, n1, ct):
    h1, cols = x_ref.shape[1], x_ref.shape[2]
    x = x_ref[...].reshape(G * h1, cols).astype(BF16)
    for c0 in range(0, cols, ct):
        a = jnp.dot(t_ref[0], x[:, c0:c0 + ct], preferred_element_type=F32)
        re_ref[:, :, c0:c0 + ct] = a[:n1 * G].reshape(n1, G, ct).astype(BF16)
        im_ref[:, :, c0:c0 + ct] = a[n1 * G:].reshape(n1, G, ct).astype(BF16)


def _hy_stage1(x3, t_fwd, *, G):
    n2, h1, cols = x3.shape
    n1 = t_fwd.shape[1] // (2 * G)
    ospec = pl.BlockSpec((n1, G, cols), lambda j: (0, j, 0))
    oshape = jax.ShapeDtypeStruct((n1, n2, cols), BF16)
    return pl.pallas_call(
        functools.partial(_hy_stage1_kernel, G=G, n1=n1, ct=D_HY),
        grid=(n2 // G,),
        in_specs=[pl.BlockSpec((G, h1, cols), lambda j: (j, 0, 0)),
                  pl.BlockSpec((1,) + t_fwd.shape[1:], lambda j: (j, 0, 0))],
        out_specs=[ospec, ospec],
        out_shape=[oshape, oshape],
        compiler_params=_cp("arbitrary"),
        name="hy_stage1",
    )(x3, t_fwd)


def _hy_fspec2_kernel(re_ref, im_ref, t2_ref, fre_ref, fim_ref, *, kb, k2):
    for j in range(kb):
        a2 = jnp.concatenate([re_ref[j], im_ref[j]], axis=0)
        x = jnp.dot(t2_ref[...], a2, preferred_element_type=F32)
        fre_ref[j] = x[:k2]
        fim_ref[j] = x[k2:]


def _hy_stage2_kernel(re_ref, im_ref, fre_ref, fim_ref, skip_ref, t2_ref, t2i_ref, ore_ref, oim_ref,
                      *, kb, k2, n2, nbc, order):
    for j in range(kb):
        a2 = jnp.concatenate([re_ref[j], im_ref[j]], axis=0)
        x = jnp.dot(t2_ref[...], a2, preferred_element_type=F32)
        fr, fi = _filter_spectrum(fre_ref[j], fim_ref[j], order, skip_ref[0])
        y = _spectral_product(x[:k2], x[k2:], fr, fi, nbc).astype(BF16)
        a = jnp.dot(t2i_ref[...], y, preferred_element_type=F32)
        ore_ref[j] = a[:n2].astype(BF16)
        oim_ref[j] = a[n2:].astype(BF16)


def _hy_stage3_kernel(re_ref, im_ref, t_ref, g_ref, o_ref, *, G, n1, ct):
    cols = re_ref.shape[2]
    for c0 in range(0, cols, ct):
        sl = slice(c0, c0 + ct)
        a2 = jnp.concatenate([re_ref[:, :, sl].reshape(n1 * G, ct), im_ref[:, :, sl].reshape(n1 * G, ct)], axis=0)
        conv = jnp.dot(t_ref[0], a2, preferred_element_type=F32)
        g = g_ref[:, :, sl]
        o_ref[:, :, sl] = (g * conv.reshape(g.shape)).astype(o_ref.dtype)


def _hy_stage31_kernel(re_ref, im_ref, ti_ref, g_ref, tf_ref, ore_ref, oim_ref, *, G, n1, ct):
    cols = re_ref.shape[2]
    for c0 in range(0, cols, ct):
        sl = slice(c0, c0 + ct)
        a2 = jnp.concatenate([re_ref[:, :, sl].reshape(n1 * G, ct), im_ref[:, :, sl].reshape(n1 * G, ct)], axis=0)
        conv = jnp.dot(ti_ref[0], a2, preferred_element_type=F32)
        z = g_ref[:, :, sl].reshape(conv.shape) * conv
        a = jnp.dot(tf_ref[0], z.astype(BF16), preferred_element_type=F32)
        ore_ref[:, :, sl] = a[:n1 * G].reshape(n1, G, ct).astype(BF16)
        oim_ref[:, :, sl] = a[n1 * G:].reshape(n1, G, ct).astype(BF16)


def _two_level_tables(L, *, n1=FFT_N1, G=FFT_G):
    t1, t2, t2i, t1i = _dft_two_level_tables(L, n1)
    return (jnp.asarray(t2).astype(BF16), jnp.asarray(t2i).astype(BF16)) + _kron_tables(t1, t1i, G)


def _hyena_two_level(z3, g1_3, g2, filt, skip, tables, *, n1=FFT_N1, G=FFT_G, kb=4):
    n2, h1, cols = z3.shape
    L = n2 * h1
    k2 = n2 // 2
    nbc = cols // D_HY
    t2_bf, t2i_bf, t_fwd, t_inv_swapped, t_inv_natural = tables
    nf = filt.shape[1]

    f_re, f_im = _hy_stage1(filt.reshape(n2, h1, nf), t_fwd, G=G)
    blk = lambda c: pl.BlockSpec((kb, n2, c), lambda i: (i, 0, 0))
    hblk = lambda c: pl.BlockSpec((kb, k2, c), lambda i: (i, 0, 0))
    const = lambda s: pl.BlockSpec(s, lambda i: (0, 0))
    fshape = jax.ShapeDtypeStruct((n1, k2, nf), F32)
    fre, fim = pl.pallas_call(
        functools.partial(_hy_fspec2_kernel, kb=kb, k2=k2),
        grid=(n1 // kb,),
        in_specs=[blk(nf), blk(nf), const((n2, 2 * n2))],
        out_specs=[hblk(nf), hblk(nf)],
        out_shape=[fshape, fshape],
        compiler_params=_cp("arbitrary"),
        name="hy_fspec2",
    )(f_re, f_im, t2_bf)

    ashape = jax.ShapeDtypeStruct((n1, n2, cols), BF16)
    grp = pl.BlockSpec((n1, G, cols), lambda j: (0, j, 0))
    tspec = lambda t: pl.BlockSpec((1,) + t.shape[1:], lambda j: (j, 0, 0))
    a_re, a_im = _hy_stage1(z3, t_fwd, G=G)
    for order in range(2):
        b_re, b_im = pl.pallas_call(
            functools.partial(_hy_stage2_kernel, kb=kb, k2=k2, n2=n2, nbc=nbc, order=order),
            grid=(n1 // kb,),
            in_specs=[blk(cols), blk(cols), hblk(nf), hblk(nf),
                      pl.BlockSpec((1, 1, D_HY), lambda i: (order, 0, 0)),
                      const((n2, 2 * n2)), const((2 * n2, n2))],
            out_specs=[blk(cols), blk(cols)],
            out_shape=[ashape, ashape],
            compiler_params=_cp("arbitrary"),
            name="hy_stage2",
        )(a_re, a_im, fre, fim, skip[:, None, :], t2_bf, t2i_bf)
        if order == 0:
            a_re, a_im = pl.pallas_call(
                functools.partial(_hy_stage31_kernel, G=G, n1=n1, ct=D_HY),
                grid=(n2 // G,),
                in_specs=[grp, grp, tspec(t_inv_swapped), pl.BlockSpec((G, h1, cols), lambda j: (j, 0, 0)),
                          tspec(t_fwd)],
                out_specs=[grp, grp],
                out_shape=[ashape, ashape],
                compiler_params=_cp("arbitrary"),
                name="hy_stage31",
            )(b_re, b_im, t_inv_swapped, g1_3, t_fwd)
        else:
            io = pl.BlockSpec((h1, G, cols), lambda j: (0, j, 0))
            out = pl.pallas_call(
                functools.partial(_hy_stage3_kernel, G=G, n1=n1, ct=D_HY),
                grid=(n2 // G,),
                in_specs=[grp, grp, tspec(t_inv_natural), io],
                out_specs=io,
                out_shape=jax.ShapeDtypeStruct((h1, n2, cols), BF16),
                compiler_params=_cp("arbitrary"),
                name="hy_stage3",
            )(b_re, b_im, t_inv_natural, g2.reshape(h1, n2, cols))
    return out.reshape(L, cols)


def _hyena(hv_t, h1_t, h2_t, conv_w, conv_b, filt, skip, tables=None):
    L, cols = hv_t.shape
    if _two_level(L):
        z3, g1_3, g2 = _hy_conv3(hv_t, h1_t, h2_t, conv_w, conv_b, swap=(True, True, False))
        return _hyena_two_level(z3, g1_3, g2, filt, skip, tables or _two_level_tables(L))
    z, g1, g2 = _hy_conv3(hv_t, h1_t, h2_t, conv_w, conv_b)
    return _hyena_direct(z, g1, g2, filt, skip)


def _route(logits):
    lane_i = lax.broadcasted_iota(jnp.int32, logits.shape, 1)
    lane = lane_i.astype(F32)
    is_g = (lane_i >= N_EXPERTS) & (lane_i < N_EXPERTS + N_GROUPS)
    big = float(1 << 20)
    gl = jnp.where(is_g, logits, NEG)
    gmax = jnp.max(gl, axis=-1, keepdims=True)
    gidx = jnp.min(jnp.where(gl == gmax, lane - N_EXPERTS, big), axis=-1, keepdims=True)
    p_group = 1.0 / jnp.sum(jnp.where(is_g, jnp.exp(gl - gmax), 0.0), axis=-1, keepdims=True)
    sel = (lane_i < N_EXPERTS) & ((lane_i // EXPERTS_PER_GROUP) == gidx.astype(jnp.int32))
    el = jnp.where(sel, logits, NEG)
    ee = jnp.where(sel, jnp.exp(el - jnp.max(el, axis=-1, keepdims=True)), 0.0)
    pe = ee / jnp.sum(ee, axis=-1, keepdims=True)
    p1 = jnp.max(jnp.where(sel, pe, -1.0), axis=-1, keepdims=True)
    i1 = jnp.min(jnp.where(sel & (pe == p1), lane, big), axis=-1, keepdims=True)
    rest = sel & (lane != i1)
    p2 = jnp.max(jnp.where(rest, pe, -1.0), axis=-1, keepdims=True)
    i2 = jnp.min(jnp.where(rest & (pe == p2), lane, big), axis=-1, keepdims=True)
    den = p1 + p2
    cmb = jnp.where(lane == i1, p_group * (p1 / den), jnp.where(lane == i2, p_group * (p2 / den), 0.0))
    return jnp.where(lane_i == N_EXPERTS, gidx, cmb)


def _outproj_kernel(att_ref, hf_ref, hb_ref, gt_ref, hy_ref, x_ref, mod_ref, w_ref, g_ref, b_ref,
                    wr_ref, br_ref, x1_ref, u2_ref, cmb_ref):
    rg = (hf_ref[...] + hb_ref[...]) * jax.nn.gelu(gt_ref[...].astype(F32))
    mix = (jnp.dot(att_ref[0], w_ref[0:D_ATT, :], preferred_element_type=F32)
           + jnp.dot(rg.astype(BF16), w_ref[D_ATT:D_ATT + D_RNN, :], preferred_element_type=F32)
           + jnp.dot(hy_ref[...].astype(BF16), w_ref[D_ATT + D_RNN:, :], preferred_element_type=F32))
    m = mod_ref[0]
    x1 = _ln(ALPHA * x_ref[...] + m[2:3] * mix) * g_ref[...] + b_ref[...]
    u2 = _ln(x1) * (1.0 + m[4:5]) + m[3:4]
    u_hi = u2.astype(BF16)
    u_lo = (u2 - u_hi.astype(F32)).astype(BF16)
    wr = wr_ref[...]
    w_hi = wr.astype(BF16)
    w_lo = (wr - w_hi.astype(F32)).astype(BF16)
    hi = jnp.dot(u_hi, jnp.concatenate([w_hi, w_lo], axis=1), preferred_element_type=F32)
    lo = jnp.dot(u_lo, w_hi, preferred_element_type=F32)
    logits = hi[:, 0:LANES] + hi[:, LANES:] + lo + br_ref[...]
    x1_ref[...] = x1
    u2_ref[...] = u_hi
    cmb_ref[...] = _route(logits)


def _outproj(att, hf_t, hb_t, gt_t, hy_t, x2d, mod_l, w_bf, ln_g, ln_b, wr, br, *, nb, L, tm,
             cond_base, cond_per_batch):
    tpb = L // tm
    d = x2d.shape[1]
    cond = (lambda i: cond_base + i // tpb) if cond_per_batch else (lambda i: cond_base)
    t_spec = pl.BlockSpec((tm, D_RNN), lambda i: (i % tpb, i // tpb))
    tok = lambda c: pl.BlockSpec((tm, c), lambda i: (i, 0))
    const = lambda s: pl.BlockSpec(s, lambda i: (0, 0))
    n = nb * L
    return pl.pallas_call(
        _outproj_kernel,
        grid=(nb * tpb,),
        in_specs=[pl.BlockSpec((1, tm, D_ATT), lambda i: (i // tpb, i % tpb, 0)),
                  t_spec, t_spec, t_spec, t_spec, tok(d),
                  pl.BlockSpec((1, 6, d), lambda i: (cond(i), 0, 0)),
                  const((d, d)), const((1, d)), const((1, d)), const((d, LANES)), const((1, LANES))],
        out_specs=[tok(d), tok(d), tok(LANES)],
        out_shape=[jax.ShapeDtypeStruct((n, d), F32), jax.ShapeDtypeStruct((n, d), BF16),
                   jax.ShapeDtypeStruct((n, LANES), F32)],
        compiler_params=_cp("arbitrary"),
        name="outproj",
    )(att, hf_t, hb_t, gt_t, hy_t, x2d, mod_l, w_bf, ln_g[None], ln_b[None], wr, br)


def _moe_kernel(u_ref, cmb_ref, x1_ref, mod_ref, wgu_ref, wd_ref, g_ref, b_ref, o_ref, ys_ref, *, T, BLK):
    SL = T + N_GROUPS * BLK
    u = u_ref[...]
    cmb = cmb_ref[...]
    lane_f = lax.broadcasted_iota(jnp.int32, (T, LANES), 1).astype(F32)
    gid_col = jnp.sum(jnp.where(lane_f == float(N_EXPERTS), cmb, 0.0), axis=-1, keepdims=True)
    onehot = jnp.where(lane_f == gid_col, 1.0, 0.0)
    ri = lax.broadcasted_iota(jnp.int32, (T, T), 0)
    ci = lax.broadcasted_iota(jnp.int32, (T, T), 1)
    before = jnp.where(ci < ri, 1.0, 0.0).astype(BF16)
    rank_col = jnp.sum(onehot * jnp.dot(before, onehot.astype(BF16), preferred_element_type=F32),
                       axis=-1, keepdims=True)
    counts = jnp.sum(onehot, axis=0, keepdims=True)
    lane_row = lax.broadcasted_iota(jnp.int32, (1, LANES), 1)
    base = jnp.int32(0)
    bases = []
    base_row = jnp.zeros((1, LANES), jnp.int32)
    for c in range(N_GROUPS):
        n_c = jnp.sum(jnp.where(lane_row == c, counts, 0.0)).astype(jnp.int32)
        bases.append(base)
        base_row = jnp.where(lane_row == c, base, base_row)
        base = base + ((n_c + (BLK - 1)) // BLK) * BLK
    total = base
    dest_f = jnp.sum(onehot * base_row.astype(F32), axis=-1, keepdims=True) + rank_col
    dest_col = dest_f.astype(jnp.int32)
    dest_row = jnp.broadcast_to(dest_f, (T, LANES)).T[0:1, :].astype(jnp.int32)
    cmb_hi = cmb.astype(BF16)
    cmb_lo = (cmb - cmb_hi.astype(F32)).astype(BF16)
    blk_rows = lax.broadcasted_iota(jnp.int32, (BLK, T), 0)
    blk_lane = lax.broadcasted_iota(jnp.int32, (BLK, LANES), 1)

    def block(b, carry):
        r0 = b * BLK
        rows = pl.ds(pl.multiple_of(r0, BLK), BLK)

        @pl.when(r0 < total)
        def _():
            grp = ((r0 >= bases[1]).astype(jnp.int32) + (r0 >= bases[2]).astype(jnp.int32)
                   + (r0 >= bases[3]).astype(jnp.int32))
            p = jnp.where(blk_rows + r0 == dest_row, 1.0, 0.0).astype(BF16)
            xb = jnp.dot(p, u, preferred_element_type=F32).astype(BF16)
            wb = (jnp.dot(p, cmb_hi, preferred_element_type=F32)
                  + jnp.dot(p, cmb_lo, preferred_element_type=F32))
            gu = jnp.dot(xb, wgu_ref[grp], preferred_element_type=F32)
            hs = []
            for j in range(EXPERTS_PER_GROUP):
                gate = gu[:, 2 * j * D_EXPERT:(2 * j + 1) * D_EXPERT]
                up = gu[:, (2 * j + 1) * D_EXPERT:(2 * j + 2) * D_EXPERT]
                w = jnp.sum(jnp.where(blk_lane == grp * EXPERTS_PER_GROUP + j, wb, 0.0), axis=-1, keepdims=True)
                hs.append((gate * _sigmoid(gate) * up * w).astype(BF16))
            ys_ref[rows, :] = jnp.dot(jnp.concatenate(hs, axis=1), wd_ref[grp],
                                      preferred_element_type=F32).astype(BF16)

        @pl.when(r0 >= total)
        def _():
            ys_ref[rows, :] = jnp.zeros((BLK, u.shape[1]), BF16)

        return carry

    lax.fori_loop(0, SL // BLK, block, 0)
    p_t = jnp.where(lax.broadcasted_iota(jnp.int32, (T, SL), 1) == dest_col, 1.0, 0.0).astype(BF16)
    moe = jnp.dot(p_t, ys_ref[...], preferred_element_type=F32)
    m = mod_ref[0]
    o_ref[...] = _ln(ALPHA * x1_ref[...] + m[5:6] * moe) * g_ref[...] + b_ref[...]


def _group_gate_up(w_gate, w_up):
    _, d, f = w_gate.shape
    gu = jnp.concatenate([w_gate, w_up], axis=-1).astype(BF16).reshape(N_GROUPS, EXPERTS_PER_GROUP, d, 2 * f)
    return gu.transpose(0, 2, 1, 3).reshape(N_GROUPS, d, EXPERTS_PER_GROUP * 2 * f)


def _moe(u2, cmb, x1, mod_l, wgu, wd, ln_g, ln_b, *, tm, tiles_per_cond, cond_base, cond_per_batch):
    n, d = x1.shape
    cond = (lambda i: cond_base + i // tiles_per_cond) if cond_per_batch else (lambda i: cond_base)
    tok = lambda c: pl.BlockSpec((tm, c), lambda i: (i, 0))
    const = lambda s: pl.BlockSpec(s, lambda i: (0,) * len(s))
    resident = lambda s: pl.BlockSpec(s, lambda i: (0,) * len(s), pipeline_mode=pl.Buffered(1))
    return pl.pallas_call(
        functools.partial(_moe_kernel, T=tm, BLK=MOE_BLK),
        grid=(n // tm,),
        in_specs=[tok(d), tok(LANES), tok(d),
                  pl.BlockSpec((1, 6, d), lambda i: (cond(i), 0, 0)),
                  resident(wgu.shape), resident(wd.shape),
                  const((1, d)), const((1, d))],
        out_specs=tok(d),
        out_shape=jax.ShapeDtypeStruct((n, d), F32),
        scratch_shapes=[pltpu.VMEM((tm + N_GROUPS * MOE_BLK, d), BF16)],
        compiler_params=_cp("arbitrary"),
        name="moe",
    )(u2, cmb, x1, mod_l, wgu, wd, ln_g[None], ln_b[None])


def _stream_layer(x2d, mod_l, p, *, nb, L, tm, cond_base, cond_per_batch, ctx, cache=None, h0=None, prev_kv=None):
    kw = dict(nb=nb, L=L, tm=tm, cond_base=cond_base, cond_per_batch=cond_per_batch)
    q, k, v, xr_t, gt_t, hv_t, h1_t, h2_t = _inproj(x2d, mod_l, p["w_in"], kv_dtype=F32 if ctx else BF16,
                                                    prev_kv=prev_kv, **{**kw, "tm": min(L, INPROJ_TILE)})
    if ctx:
        att = _ctx_attention(q, k, v)
    else:
        att = _nbr_attention(q, k, v, cache[0], cache[1], p["nbr_bias"])
    a_f, b_f, a_b, b_b = _rg_gates(xr_t, p["rg_conv_w"], p["rg_conv_b"], *p["rg_w"])
    hf_t, hb_t = _rg_scan(a_f, b_f, a_b, b_b, h0[0], h0[1])
    hy_t = _hyena(hv_t, h1_t, h2_t, p["hy_conv_w"], p["hy_conv_b"], p["hy_filt"][L], p["hy_skip"],
                  p["hy_tables"].get(L))
    x1, u2, cmb = _outproj(att, hf_t, hb_t, gt_t, hy_t, x2d, mod_l, p["w_out"], p["ln1_g"], p["ln1_b"],
                           p["router_w"], p["router_b"], **kw)
    tm_moe = min(MOE_TILE, nb * L)
    x2 = _moe(u2, cmb, x1, mod_l, p["moe_wgu"], p["moe_wd"], p["ln2_g"], p["ln2_b"], tm=tm_moe,
              tiles_per_cond=max(L // tm_moe, 1), cond_base=cond_base, cond_per_batch=cond_per_batch)
    return x2, k, v, hf_t, hb_t


def kernel(x_prompt, x_sample, cache_k, cache_v, state_rglru, c, c_ctx, w_mod, b_mod, w_in, attn_rpb, rg_conv_w, rg_conv_b, rg_wa, rg_ba, rg_wx, rg_bx, rg_lambda, hy_conv_w, hy_conv_b, hy_w1, hy_b1, hy_w2, hy_b2, hy_w3, hy_freq, hy_decay, hy_skip, w_out, ln1_g, ln1_b, router_wg, router_bg, router_we, router_be, moe_w_gate, moe_w_up, moe_w_down, ln2_g, ln2_b):
    nbc, Lc, d = x_prompt.shape
    nbl, Ll, _ = x_sample.shape
    depth = w_mod.shape[0]
    past = cache_k.shape[2]

    cond8 = jnp.zeros((8, d), F32).at[0].set(c_ctx).at[1:1 + nbl].set(c)
    mod = _modulation(cond8, w_mod, b_mod).reshape(depth, 8, 6, d)

    xp = x_prompt.reshape(nbc * Lc, d)
    xs = x_sample.reshape(nbl * Ll, d)
    kv, new_h = None, []
    hy_tables = {L: _two_level_tables(L) for L in {Lc, Ll} if _two_level(L)}
    for l in range(depth):
        pad = jnp.zeros((d, LANES - N_EXPERTS - N_GROUPS), F32)
        p = {
            "w_in": w_in[l].astype(BF16),
            "nbr_bias": _nbr_bias_tiles(attn_rpb[l]),
            "rg_conv_w": rg_conv_w[l], "rg_conv_b": rg_conv_b[l],
            "rg_w": _rg_weights(rg_wa[l], rg_ba[l], rg_wx[l], rg_bx[l], rg_lambda[l]),
            "hy_conv_w": hy_conv_w[l], "hy_conv_b": hy_conv_b[l], "hy_skip": hy_skip[l], "hy_tables": hy_tables,
            "hy_filt": {L: _hy_filters(L, hy_w1[l], hy_b1[l], hy_w2[l], hy_b2[l], hy_w3[l], hy_freq[l], hy_decay[l])
                        for L in sorted({Lc, Ll})},
            "w_out": w_out[l].astype(BF16), "ln1_g": ln1_g[l], "ln1_b": ln1_b[l],
            "router_w": jnp.concatenate([router_we[l], router_wg[l], pad], axis=1),
            "router_b": jnp.concatenate([router_be[l], router_bg[l], pad[0]])[None],
            "moe_wgu": _group_gate_up(moe_w_gate[l], moe_w_up[l]),
            "moe_wd": moe_w_down[l].astype(BF16).reshape(N_GROUPS, EXPERTS_PER_GROUP * D_EXPERT, d),
            "ln2_g": ln2_g[l], "ln2_b": ln2_b[l],
        }
        zeros_h = jnp.zeros((1, nbc * D_RNN), F32)
        xp, k, v, hf_t, hb_t = _stream_layer(
            xp, mod[l], p, nb=nbc, L=Lc, tm=Lc, cond_base=0, cond_per_batch=False, ctx=True,
            h0=(zeros_h, zeros_h), prev_kv=kv)
        kv = (k, v)
        new_h.append(jnp.stack([hf_t[Lc - 1].reshape(nbc, D_RNN), hb_t[0].reshape(nbc, D_RNN)], axis=1))

        cache = (cache_k[:, l].reshape(nbl, past, D_ATT), cache_v[:, l].reshape(nbl, past, D_ATT))
        h0 = (state_rglru[:, l, 0].reshape(1, nbl * D_RNN), state_rglru[:, l, 1].reshape(1, nbl * D_RNN))
        xs, _, _, _, _ = _stream_layer(
            xs, mod[l], p, nb=nbl, L=Ll, tm=min(Ll, 512), cond_base=1, cond_per_batch=True, ctx=False,
            cache=cache, h0=h0)

    return (xp.reshape(nbc, Lc, d), xs.reshape(nbl, Ll, d),
            kv[0].reshape(nbc, depth, Lc, N_HEADS, HEAD_DIM), kv[1].reshape(nbc, depth, Lc, N_HEADS, HEAD_DIM),
            jnp.stack(new_h, axis=1))
```

```python
import functools
import math

import numpy as np
import jax
import jax.numpy as jnp
from jax import lax
from jax.experimental import pallas as pl
from jax.experimental.pallas import tpu as pltpu

F32 = jnp.float32
BF16 = jnp.bfloat16
HIGHEST = lax.Precision.HIGHEST

DEPTH = 2
D_MODEL = 1024
N_HEADS = 8
HEAD_DIM = 64
D_ATT = N_HEADS * HEAD_DIM
D_RNN = 256
D_HY = 256
GRID_W = 64
WIN_R = 8
WIN_C = 16
RG_CONV = 4
RG_C = 8.0
HY_CONV = 3
POS_EMB_DIM = 33
N_GROUPS = 4
EXPERTS_PER_GROUP = 4
N_EXPERTS = 16
D_EXPERT = 256
ALPHA = (2.0 * DEPTH) ** 0.25
LN_EPS = 1e-5
NEG = -1e30

Q_ROWS = 4
K_ROWS = Q_ROWS + WIN_R
LANES = 128
HALO = 8
ROW_CHUNK = 512
SCAN_LANES = 1024
BLOCK_BYTES = 1 << 20
MOD_COL_TILES = 4
FFT_MIN_LEN = 2048
FFT_KB = 4
INPROJ_TILE = 1024
MOE_TILE = 512
MOE_BLK = 128
STRIDE_PAD = 4
FFT_G = 16
FFT_N1 = 64
VMEM_LIMIT = 52 * 1024 * 1024


def _cp(*sem):
    return pltpu.CompilerParams(dimension_semantics=sem, vmem_limit_bytes=VMEM_LIMIT)


def _sigmoid(x):
    return 0.5 * jnp.tanh(0.5 * x) + 0.5


def _ln(x):
    mu = jnp.mean(x, axis=-1, keepdims=True)
    xc = x - mu
    var = jnp.mean(xc * xc, axis=-1, keepdims=True)
    return xc * lax.rsqrt(var + LN_EPS)


def _mod_kernel(c_ref, w_ref, b_ref, o_ref):
    c = c_ref[...]
    s = c * jax.nn.sigmoid(c)
    o_ref[0] = jnp.dot(s, w_ref[0], precision=HIGHEST, preferred_element_type=F32) + b_ref[0]


def _modulation(cond8, w_mod, b_mod):
    depth, d, n = w_mod.shape
    tn = n // MOD_COL_TILES
    return pl.pallas_call(
        _mod_kernel,
        grid=(depth, n // tn),
        in_specs=[pl.BlockSpec((8, d), lambda l, j: (0, 0)),
                  pl.BlockSpec((1, d, tn), lambda l, j: (l, 0, j)),
                  pl.BlockSpec((1, 1, tn), lambda l, j: (l, 0, j))],
        out_specs=pl.BlockSpec((1, 8, tn), lambda l, j: (l, 0, j)),
        out_shape=jax.ShapeDtypeStruct((depth, 8, n), F32),
        compiler_params=_cp("arbitrary", "arbitrary"),
        name="modulation",
    )(cond8, w_mod, b_mod.reshape(depth, 1, n))


def _inproj_kernel(x_ref, mod_ref, w_ref, *refs, n_prev):
    if n_prev:
        pk_ref, pv_ref = refs[:2]
        refs = refs[2:]
    q_ref, k_ref, v_ref, xr_ref, gt_ref, hv_ref, h1_ref, h2_ref = refs
    for i in range(n_prev):
        k_ref[0, i] = pk_ref[0, i]
        v_ref[0, i] = pv_ref[0, i]
    m = mod_ref[0]
    u = (_ln(x_ref[...]) * (1.0 + m[1:2]) + m[0:1]).astype(BF16)

    def proj(lo, hi):
        return jnp.dot(u, w_ref[:, lo:hi], preferred_element_type=F32)

    o = 0
    q_ref[0] = (proj(o, o + D_ATT) * (HEAD_DIM ** -0.5)).astype(q_ref.dtype)
    o += D_ATT
    k_ref[0, n_prev] = proj(o, o + D_ATT).astype(k_ref.dtype)
    o += D_ATT
    v_ref[0, n_prev] = proj(o, o + D_ATT).astype(v_ref.dtype)
    o += D_ATT
    for ref in (xr_ref, gt_ref, hv_ref, h1_ref, h2_ref):
        ref[...] = proj(o, o + D_RNN).astype(ref.dtype)
        o += D_RNN


def _inproj(x2d, mod_l, w_bf, *, nb, L, tm, cond_base, cond_per_batch, kv_dtype, prev_kv=None):
    tpb = L // tm
    d = x2d.shape[1]
    d_in = w_bf.shape[1]
    n_prev = prev_kv[0].shape[1] if prev_kv else 0
    cond = (lambda i: cond_base + i // tpb) if cond_per_batch else (lambda i: cond_base)
    att_spec = pl.BlockSpec((1, tm, D_ATT), lambda i: (i // tpb, i % tpb, 0))
    kv_spec = lambda n: pl.BlockSpec((1, n, tm, D_ATT), lambda i: (i // tpb, 0, i % tpb, 0))
    t_spec = pl.BlockSpec((tm, D_RNN), lambda i: (i % tpb, i // tpb))
    kv_shape = jax.ShapeDtypeStruct((nb, n_prev + 1, L, D_ATT), kv_dtype)
    t_shape = jax.ShapeDtypeStruct((L, nb * D_RNN), BF16)
    return pl.pallas_call(
        functools.partial(_inproj_kernel, n_prev=n_prev),
        grid=(nb * tpb,),
        in_specs=[pl.BlockSpec((tm, d), lambda i: (i, 0)),
                  pl.BlockSpec((1, 6, d), lambda i: (cond(i), 0, 0)),
                  pl.BlockSpec((d, d_in), lambda i: (0, 0))] + ([kv_spec(n_prev)] * 2 if n_prev else []),
        out_specs=[att_spec, kv_spec(n_prev + 1), kv_spec(n_prev + 1), t_spec, t_spec, t_spec, t_spec, t_spec],
        out_shape=[jax.ShapeDtypeStruct((nb, L, D_ATT), BF16), kv_shape, kv_shape,
                   t_shape, t_shape, t_shape, t_shape, t_shape],
        compiler_params=_cp("arbitrary"),
        name="inproj",
    )(x2d, mod_l, w_bf, *(prev_kv or ()))


def _nt(a, b):
    return lax.dot_general(a, b, (((1,), (1,)), ((), ())), preferred_element_type=F32)


def _ctx_attn_kernel(q_ref, k_ref, v_ref, o_ref):
    q = q_ref[0]
    k = k_ref[0, 0].astype(BF16)
    v = v_ref[0, 0].astype(BF16)
    outs = []
    for h in range(N_HEADS):
        sl = slice(h * HEAD_DIM, (h + 1) * HEAD_DIM)
        s = _nt(q[:, sl], k[:, sl])
        e = jnp.exp(s - jnp.max(s, axis=-1, keepdims=True))
        den = jnp.sum(e, axis=-1, keepdims=True)
        outs.append(jnp.dot(e.astype(BF16), v[:, sl], preferred_element_type=F32) / den)
    o_ref[0] = jnp.concatenate(outs, axis=-1).astype(o_ref.dtype)


def _ctx_attention(q, k, v):
    nb, L, _ = q.shape
    last = k.shape[1] - 1
    spec = pl.BlockSpec((1, L, D_ATT), lambda b: (b, 0, 0))
    kv_spec = pl.BlockSpec((1, 1, L, D_ATT), lambda b: (b, last, 0, 0))
    return pl.pallas_call(
        _ctx_attn_kernel,
        grid=(nb,),
        in_specs=[spec, kv_spec, kv_spec],
        out_specs=spec,
        out_shape=jax.ShapeDtypeStruct((nb, L, D_ATT), BF16),
        compiler_params=_cp("arbitrary"),
        name="ctx_attn",
    )(q, k, v)


N_DR = 2 * WIN_R - 1
PAIR_LEFT = N_DR - 1
PAIR_RIGHT = 2 * N_DR - 1
PAIR_NONE = 3 * N_DR - 1


def _nbr_bias_tiles(rpb):
    cols = np.arange(GRID_W)
    c_start = np.clip(cols - WIN_C // 2, 0, GRID_W - WIN_C)
    col_mask = (cols[None, :] >= c_start[:, None]) & (cols[None, :] < c_start[:, None] + WIN_C)
    dc_idx = np.clip(cols[None, :] - cols[:, None], -(WIN_C - 1), WIN_C - 1) + (WIN_C - 1)
    sel = (dc_idx[None] == np.arange(2 * WIN_C - 1)[:, None, None]).astype(np.float32)
    colb = jnp.einsum('hab,bqk->haqk', rpb.astype(F32), jnp.asarray(sel), precision=HIGHEST)
    colb = jnp.where(col_mask[None, None], colb, NEG)
    neg = jnp.full_like(colb, NEG)
    both = jnp.concatenate([colb[:, :-1], colb[:, 1:]], axis=-1)
    left_masked = jnp.concatenate([neg, colb], axis=-1)
    right_masked = jnp.concatenate([colb, neg], axis=-1)
    none = jnp.concatenate([neg[:, :1], neg[:, :1]], axis=-1)
    return jnp.concatenate([both, left_masked, right_masked, none], axis=1)


def _nbr_attn_kernel(q_ref, k_ref, v_ref, kc_ref, vc_ref, bias_ref, o_ref, *, rows):
    g = pl.program_id(1)
    r0 = jnp.clip(Q_ROWS * g - WIN_R // 2, 0, rows - K_ROWS)
    start = pl.multiple_of(r0 * GRID_W, Q_ROWS * GRID_W)
    nk = K_ROWS * GRID_W
    kw = k_ref[0, 0, pl.ds(start, nk), :]
    vw = v_ref[0, 0, pl.ds(start, nk), :]
    kc = kc_ref[0].astype(BF16)
    vc = vc_ref[0].astype(BF16)
    q = q_ref[0]
    tile = []
    for j in range(Q_ROWS):
        r = Q_ROWS * g + j
        lo = jnp.clip(r - WIN_R // 2, 0, rows - WIN_R) - r + (WIN_R - 1)
        tile.append([])
        for p in range(K_ROWS // 2):
            d0 = r0 + 2 * p - r + (WIN_R - 1)
            lv = (d0 >= lo) & (d0 < lo + WIN_R)
            rv = (d0 + 1 >= lo) & (d0 + 1 < lo + WIN_R)
            idx = jnp.where(lv & rv, d0, jnp.where(rv, PAIR_LEFT + d0 + 1, jnp.where(lv, PAIR_RIGHT + d0, PAIR_NONE)))
            tile[j].append(jnp.clip(idx, 0, PAIR_NONE))
    outs = []
    for h in range(N_HEADS):
        sl = slice(h * HEAD_DIM, (h + 1) * HEAD_DIM)
        qh = q[:, sl]
        bias = jnp.concatenate(
            [jnp.concatenate([bias_ref[h, tile[j][p]] for p in range(K_ROWS // 2)], axis=-1)
             for j in range(Q_ROWS)], axis=0)
        s1 = _nt(qh, kw[:, sl]) + bias
        s2 = _nt(qh, kc[:, sl])
        m = jnp.maximum(jnp.max(s1, axis=-1, keepdims=True), jnp.max(s2, axis=-1, keepdims=True))
        e1 = jnp.exp(s1 - m)
        e2 = jnp.exp(s2 - m)
        den = jnp.sum(e1, axis=-1, keepdims=True) + jnp.sum(e2, axis=-1, keepdims=True)
        o = (jnp.dot(e1.astype(BF16), vw[:, sl], preferred_element_type=F32)
             + jnp.dot(e2.astype(BF16), vc[:, sl], preferred_element_type=F32))
        outs.append(o / den)
    o_ref[0] = jnp.concatenate(outs, axis=-1).astype(o_ref.dtype)


def _nbr_attention(q, k, v, kc, vc, bias):
    nb, L, _ = q.shape
    rows = L // GRID_W
    ng = rows // Q_ROWS
    past = kc.shape[1]
    tq = Q_ROWS * GRID_W
    full = pl.BlockSpec((1, 1, L, D_ATT), lambda b, g: (b, 0, 0, 0))
    ctx = pl.BlockSpec((1, past, D_ATT), lambda b, g: (b, 0, 0))
    qspec = pl.BlockSpec((1, tq, D_ATT), lambda b, g: (b, g, 0))
    bspec = pl.BlockSpec(bias.shape, lambda b, g: (0, 0, 0, 0), pipeline_mode=pl.Buffered(1))
    return pl.pallas_call(
        functools.partial(_nbr_attn_kernel, rows=rows),
        grid=(nb, ng),
        in_specs=[qspec, full, full, ctx, ctx, bspec],
        out_specs=qspec,
        out_shape=jax.ShapeDtypeStruct((nb, L, D_ATT), BF16),
        compiler_params=_cp("arbitrary", "arbitrary"),
        name="nbr_attn",
    )(q, k, v, kc, vc, bias)


def _fill_padded(xp_ref, src_ref, L):
    z = jnp.zeros((HALO, xp_ref.shape[1]), F32)
    xp_ref[0:HALO, :] = z
    xp_ref[HALO:HALO + L, :] = src_ref[...].astype(F32)
    xp_ref[HALO + L:2 * HALO + L, :] = z


def _rg_gates_kernel(x_ref, cw_ref, cb_ref, w_ref, b_ref, lam_ref,
                     af_ref, bf_ref, ab_ref, bb_ref, xp_ref, *, L, tc, m):
    nh = cw_ref.shape[0]
    for sc in range(m):
        half = (pl.program_id(0) * m + sc) % nh
        cols = slice(sc * LANES, (sc + 1) * LANES)
        _fill_padded(xp_ref, x_ref.at[:, cols], L)
        sp = jax.nn.softplus(-lam_ref[half])
        cw = cw_ref[half]
        for t0 in range(0, L, tc):
            xc = cb_ref[half]
            for k in range(RG_CONV):
                xc = xc + cw[k:k + 1, :] * xp_ref[pl.ds(HALO + t0 + k - RG_CONV // 2, tc), :]
            z = jnp.dot(xc.astype(BF16), w_ref[half], preferred_element_type=F32) + b_ref[half]
            for e, (a_ref, o_ref) in enumerate(((af_ref, bf_ref), (ab_ref, bb_ref))):
                r = _sigmoid(z[:, e * LANES:(e + 1) * LANES])
                i = _sigmoid(z[:, (2 + e) * LANES:(3 + e) * LANES])
                log_a = (-RG_C) * r * sp[:, e * LANES:(e + 1) * LANES]
                a = jnp.exp(log_a)
                one_minus_a2 = -jnp.tanh(log_a) * (a * a + 1.0)
                a_ref[pl.ds(t0, tc), cols] = a
                o_ref[pl.ds(t0, tc), cols] = jnp.sqrt(one_minus_a2) * i * xc


def _blockdiag2(w2):
    z = jnp.zeros_like(w2[0])
    return jnp.concatenate([jnp.concatenate([w2[0], z], 1), jnp.concatenate([z, w2[1]], 1)], 0)


def _rg_weights(wa, ba, wx, bx, lam):
    ws, bs, ls = [], [], []
    for hf in range(D_RNN // LANES):
        blk = slice(2 * hf, 2 * hf + 2)
        ch = slice(hf * LANES, (hf + 1) * LANES)
        ws.append(jnp.concatenate([_blockdiag2(wa[0, blk]), _blockdiag2(wa[1, blk]),
                                   _blockdiag2(wx[0, blk]), _blockdiag2(wx[1, blk])], axis=1))
        bs.append(jnp.concatenate([ba[0, ch], ba[1, ch], bx[0, ch], bx[1, ch]])[None])
        ls.append(jnp.concatenate([lam[0, ch], lam[1, ch]])[None])
    return jnp.stack(ws).astype(BF16), jnp.stack(bs), jnp.stack(ls)


def _lane_chunks(L, cols):
    m = max(1, min(8, BLOCK_BYTES // (L * LANES * 4)))
    while cols % (m * LANES):
        m //= 2
    return m


def _rg_gates(xr_t, conv_w, conv_b, w_bd, b_cat, lam_cat):
    L, cols = xr_t.shape
    nh = D_RNN // LANES
    tc = min(L, ROW_CHUNK)
    m = _lane_chunks(L, cols)
    col = pl.BlockSpec((L, m * LANES), lambda c: (0, c))
    const = lambda a: pl.BlockSpec(a.shape, lambda c: (0,) * a.ndim)
    shape = jax.ShapeDtypeStruct((L, cols), F32)
    cw = conv_w.reshape(RG_CONV, nh, LANES).transpose(1, 0, 2)
    cb = conv_b.reshape(nh, 1, LANES)
    return pl.pallas_call(
        functools.partial(_rg_gates_kernel, L=L, tc=tc, m=m),
        grid=(cols // (m * LANES),),
        in_specs=[col, const(cw), const(cb), const(w_bd), const(b_cat), const(lam_cat)],
        out_specs=[col, col, col, col],
        out_shape=[shape, shape, shape, shape],
        scratch_shapes=[pltpu.VMEM((L + 2 * HALO, LANES), F32)],
        compiler_params=_cp("arbitrary"),
        name="rg_gates",
    )(xr_t, cw, cb, w_bd, b_cat, lam_cat)


def _rg_scan_kernel(af_ref, bf_ref, ab_ref, bb_ref, h0f_ref, h0b_ref, hf_ref, hb_ref, cf_ref, cb_ref, *, tc):
    @pl.when(pl.program_id(1) == 0)
    def _():
        cf_ref[...] = h0f_ref[...]
        cb_ref[...] = h0b_ref[...]

    def body(t, carry):
        hf, hb = carry
        hf = af_ref[pl.ds(t, 1), :] * hf + bf_ref[pl.ds(t, 1), :]
        hf_ref[pl.ds(t, 1), :] = hf
        tb = tc - 1 - t
        hb = ab_ref[pl.ds(tb, 1), :] * hb + bb_ref[pl.ds(tb, 1), :]
        hb_ref[pl.ds(tb, 1), :] = hb
        return hf, hb

    hf, hb = lax.fori_loop(0, tc, body, (cf_ref[...], cb_ref[...]), unroll=8)
    cf_ref[...] = hf
    cb_ref[...] = hb


def _rg_scan(a_f, b_f, a_b, b_b, h0f, h0b):
    L, cols = a_f.shape
    cw = min(cols, SCAN_LANES)
    tc = min(L, ROW_CHUNK)
    nt = L // tc
    fwd = pl.BlockSpec((tc, cw), lambda c, i: (i, c))
    bwd = pl.BlockSpec((tc, cw), lambda c, i: (nt - 1 - i, c))
    row = pl.BlockSpec((1, cw), lambda c, i: (0, c))
    shape = jax.ShapeDtypeStruct((L, cols), F32)
    return pl.pallas_call(
        functools.partial(_rg_scan_kernel, tc=tc),
        grid=(cols // cw, nt),
        in_specs=[fwd, fwd, bwd, bwd, row, row],
        out_specs=[fwd, bwd],
        out_shape=[shape, shape],
        scratch_shapes=[pltpu.VMEM((1, cw), F32), pltpu.VMEM((1, cw), F32)],
        compiler_params=_cp("arbitrary", "arbitrary"),
        name="rg_scan",
    )(a_f, b_f, a_b, b_b, h0f, h0b)


def _hy_filter_kernel(feat_ref, w1_ref, b1_ref, w2_ref, b2_ref, w3_ref, fr_ref, dec_ref, o_ref, *, tm):
    feats = feat_ref[...]
    dot = functools.partial(jnp.dot, precision=HIGHEST, preferred_element_type=F32)
    h = jnp.sin(fr_ref[0:1, :] * (dot(feats, w1_ref[...]) + b1_ref[...]))
    h = jnp.sin(fr_ref[1:2, :] * (dot(h, w2_ref[...]) + b2_ref[...]))
    h = dot(h, w3_ref[...])
    filt = h * jnp.exp(-feats[:, 0:1] * jnp.abs(dec_ref[...]))
    row = pl.program_id(0) * tm + lax.broadcasted_iota(jnp.int32, filt.shape, 0)
    lane = lax.broadcasted_iota(jnp.int32, filt.shape, 1)
    o_ref[...] = jnp.where((row == 0) & (lane >= 2 * D_HY), 0.0, filt)


def _hy_features(L):
    pos = np.arange(L, dtype=np.float64)
    t = np.linspace(0.0, 1.0, L)
    bands = (POS_EMB_DIM - 1) // 2
    f = np.linspace(1e-4, bands - 1, bands)
    ang = (2.0 * math.pi / L) * pos[:, None] * f[None, :]
    feats = np.concatenate([t[:, None], np.cos(ang), -np.sin(ang)], axis=-1)
    out = np.zeros((L, LANES), np.float32)
    out[:, :POS_EMB_DIM] = feats
    return out


def _two_level(L):
    return L % (FFT_N1 * 8) == 0 and L >= FFT_MIN_LEN


def _digit_swap(x, L):
    n2 = 2 * L // FFT_N1
    return x.reshape(L // n2, n2, -1).transpose(1, 0, 2).reshape(L, -1)


def _hy_filters(L, w1, b1, w2, b2, w3, freq, decay):
    tm = min(L, ROW_CHUNK)
    hid = w2.shape[0]
    n = w3.shape[1]
    w1p = jnp.zeros((LANES, hid), F32).at[:POS_EMB_DIM].set(w1)
    feats = _hy_features(L)
    if _two_level(L):
        feats = _digit_swap(feats, L)
    c = lambda s: pl.BlockSpec(s, lambda i: (0, 0))
    return pl.pallas_call(
        functools.partial(_hy_filter_kernel, tm=tm),
        grid=(L // tm,),
        in_specs=[pl.BlockSpec((tm, LANES), lambda i: (i, 0)),
                  c((LANES, hid)), c((1, hid)), c((hid, hid)), c((1, hid)), c((hid, n)), c((2, hid)), c((1, n))],
        out_specs=pl.BlockSpec((tm, n), lambda i: (i, 0)),
        out_shape=jax.ShapeDtypeStruct((L, n), F32),
        compiler_params=_cp("arbitrary"),
        name="hy_filter",
    )(jnp.asarray(feats), w1p, b1[None], w2, b2[None], w3, freq, decay.reshape(1, n))


def _conv3_kernel(v_ref, a_ref, b_ref, w_ref, cb_ref, ov_ref, oa_ref, ob_ref, xp_ref, xs_ref,
                  *, L, tc, swap, n2, m):
    h1 = L // n2
    pitch = n2 + STRIDE_PAD
    nh = w_ref.shape[0]
    for sc in range(m):
        half = (pl.program_id(0) * m + sc) % nh
        cols = slice(sc * LANES, (sc + 1) * LANES)
        for p, (src, dst) in enumerate(((v_ref, ov_ref), (a_ref, oa_ref), (b_ref, ob_ref))):
            _fill_padded(xp_ref, src.at[:, cols], L)
            w = w_ref[half, p]
            cb = cb_ref[half, p]
            if swap[p]:
                for i1 in range(h1):
                    xs_ref[pl.ds(i1 * pitch, n2 + HY_CONV - 1), :] = (
                        xp_ref[pl.ds(HALO + i1 * n2 - HY_CONV // 2, n2 + HY_CONV - 1), :])

                def body(i2, carry, dst=dst, w=w, cb=cb):
                    y = cb
                    for k in range(HY_CONV):
                        y = y + w[k:k + 1, :] * xs_ref[pl.ds(i2 + k, h1, stride=pitch), :]
                    dst[i2] = y.astype(dst.dtype)
                    return carry

                lax.fori_loop(0, n2, body, 0)
            else:
                for t0 in range(0, L, tc):
                    y = cb
                    for k in range(HY_CONV):
                        y = y + w[k:k + 1, :] * xp_ref[pl.ds(HALO + t0 + k - HY_CONV // 2, tc), :]
                    dst[pl.ds(t0, tc), cols] = y


def _hy_conv3(v_t, a_t, b_t, conv_w, conv_b, swap=(False, False, False)):
    L, cols = v_t.shape
    nh = D_HY // LANES
    tc = min(L, ROW_CHUNK)
    n2 = 2 * L // FFT_N1
    m = 1 if any(swap) else _lane_chunks(L, cols)
    col = pl.BlockSpec((L, m * LANES), lambda c: (0, c))
    col3 = pl.BlockSpec((n2, L // n2, LANES), lambda c: (0, 0, c))
    shape = jax.ShapeDtypeStruct((L, cols), F32)
    shape3 = jax.ShapeDtypeStruct((n2, L // n2, cols), F32)
    w = conv_w.reshape(HY_CONV, 3, nh, LANES).transpose(2, 1, 0, 3)
    cb = conv_b.reshape(3, nh, 1, LANES).transpose(1, 0, 2, 3)
    const = lambda a: pl.BlockSpec(a.shape, lambda c: (0,) * a.ndim)
    return pl.pallas_call(
        functools.partial(_conv3_kernel, L=L, tc=tc, swap=swap, n2=n2, m=m),
        grid=(cols // (m * LANES),),
        in_specs=[col, col, col, const(w), const(cb)],
        out_specs=[col3 if s else col for s in swap],
        out_shape=[shape, shape, shape] if not any(swap) else
                  [jax.ShapeDtypeStruct(shape3.shape, BF16 if i == 0 else F32) if s else shape
                   for i, s in enumerate(swap)],
        scratch_shapes=[pltpu.VMEM((L + 2 * HALO, LANES), F32),
                        pltpu.VMEM(((L // n2) * (n2 + STRIDE_PAD) + HALO, LANES), F32)],
        compiler_params=_cp("arbitrary"),
        name="hy_conv3",
    )(v_t, a_t, b_t, w, cb)


def _phase(p, n2):
    ang = 2.0 * np.pi * (np.asarray(p, np.int64) % n2) / n2
    return np.cos(ang), np.sin(ang)


def _dft_direct_tables(L):
    k = np.arange(L)[:, None]
    t = np.arange(L)[None, :]
    c, s = _phase((2 * k + 1) * t, 4 * L)
    fwd = np.concatenate([c, -s], axis=0)
    inv = (1.0 / L) * np.concatenate([c.T, -s.T], axis=1)
    return fwd.astype(np.float32), inv.astype(np.float32)


def _dft_two_level_tables(L, n1):
    N = 2 * L
    n2 = N // n1
    h1 = n1 // 2
    i2 = np.arange(n2)[:, None, None]
    k1 = np.arange(n1)[None, :, None]
    i1 = np.arange(h1)[None, None, :]
    c, s = _phase(2 * n2 * i1 * k1 + 2 * i2 * k1 + n2 * i1 + i2, 2 * N)
    t1 = np.concatenate([c, -s], axis=1)
    ct, st = c.transpose(0, 2, 1), s.transpose(0, 2, 1)
    t1i = (2.0 / N) * np.concatenate([ct, -st], axis=2)
    kk = np.arange(n2 // 2)[:, None]
    nn = np.arange(n2)[None, :]
    c2, s2 = _phase(nn * kk, n2)
    t2 = np.block([[c2, s2], [-s2, c2]])
    t2i = np.block([[c2.T, -s2.T], [s2.T, c2.T]])
    return tuple(x.astype(np.float32) for x in (t1, t2, t2i, t1i))


def _filter_spectrum(fre, fim, order, skip):
    fw = slice(order * D_HY, (order + 1) * D_HY)
    bw = slice((2 + order) * D_HY, (3 + order) * D_HY)
    return fre[:, fw] + fre[:, bw] + skip, fim[:, fw] - fim[:, bw]


def _spectral_product(xr, xi, fr, fi, nbc):
    yr, yi = [], []
    for b in range(nbc):
        sl = slice(b * D_HY, (b + 1) * D_HY)
        yr.append(xr[:, sl] * fr - xi[:, sl] * fi)
        yi.append(xr[:, sl] * fi + xi[:, sl] * fr)
    return jnp.concatenate([jnp.concatenate(yr, axis=1), jnp.concatenate(yi, axis=1)], axis=0)


def _matmul_kernel(a_ref, b_ref, o_ref):
    o_ref[...] = jnp.dot(a_ref[...], b_ref[...], precision=HIGHEST, preferred_element_type=F32)


def _hy_direct_kernel(z_ref, g_ref, f_ref, skip_ref, t_ref, ti_ref, o_ref, *, L, nbc, order):
    x = jnp.dot(t_ref[...], z_ref[...].astype(BF16), preferred_element_type=F32)
    fr, fi = _filter_spectrum(f_ref[0:L, :], f_ref[L:2 * L, :], order, skip_ref[0])
    y = _spectral_product(x[:L], x[L:], fr, fi, nbc).astype(BF16)
    conv = jnp.dot(ti_ref[...], y, preferred_element_type=F32)
    o_ref[...] = (g_ref[...] * conv).astype(o_ref.dtype)


def _hyena_direct(z, g1, g2, filt, skip):
    L, cols = z.shape
    fwd, inv = _dft_direct_tables(L)
    n = filt.shape[1]
    spec = pl.pallas_call(
        _matmul_kernel,
        out_shape=jax.ShapeDtypeStruct((2 * L, n), F32),
        name="hy_fspec",
    )(jnp.asarray(fwd), filt)
    tn = min(cols, 4 * D_HY)
    nbc = tn // D_HY
    col = pl.BlockSpec((L, tn), lambda c: (0, c))
    const = lambda s: pl.BlockSpec(s, lambda c: (0, 0))
    t_bf, ti_bf = jnp.asarray(fwd).astype(BF16), jnp.asarray(inv).astype(BF16)
    for order, g in enumerate((g1, g2)):
        z = pl.pallas_call(
            functools.partial(_hy_direct_kernel, L=L, nbc=nbc, order=order),
            grid=(cols // tn,),
            in_specs=[col, col, const((2 * L, n)), pl.BlockSpec((1, 1, D_HY), lambda c: (order, 0, 0)),
                      const((2 * L, L)), const((L, 2 * L))],
            out_specs=col,
            out_shape=jax.ShapeDtypeStruct((L, cols), F32 if order == 0 else BF16),
            compiler_params=_cp("arbitrary"),
            name="hy_direct",
        )(z, g, spec, skip[:, None, :], t_bf, ti_bf)
    return z


def _kron_tables(t1, t1i, G):
    n2, two_n1, h1 = t1.shape
    ng = n2 // G
    eye = np.eye(two_n1, dtype=np.float32)
    rep_rows = jnp.asarray(np.repeat(eye, G, axis=0))
    rep_cols = jnp.asarray(np.repeat(eye, G, axis=1))
    c_fwd = t1.reshape(ng, G, two_n1, h1).transpose(0, 2, 1, 3).reshape(ng, two_n1, G * h1)
    c_swapped = t1i.reshape(ng, G * h1, two_n1)
    c_natural = t1i.reshape(ng, G, h1, two_n1).transpose(0, 2, 1, 3).reshape(ng, h1 * G, two_n1)

    def expand(compact, rep, left, row_key, lane_key):
        r, c = (rep.shape[0], compact.shape[2]) if left else (compact.shape[1], rep.shape[1])
        return pl.pallas_call(
            functools.partial(_kron_expand_kernel, left=left, row_key=row_key, lane_key=lane_key),
            grid=(ng,),
            in_specs=[pl.BlockSpec((1,) + compact.shape[1:], lambda i: (i, 0, 0)),
                      pl.BlockSpec(rep.shape, lambda i: (0, 0))],
            out_specs=pl.BlockSpec((1, r, c), lambda i: (i, 0, 0)),
            out_shape=jax.ShapeDtypeStruct((ng, r, c), BF16),
            compiler_params=_cp("arbitrary"),
            name="hy_kron",
        )(jnp.asarray(compact), rep)

    fwd = expand(c_fwd, rep_rows, True, lambda r: r % G, lambda c: c // h1)
    inv_swapped = expand(c_swapped, rep_cols, False, lambda r: r // h1, lambda c: c % G)
    inv_natural = expand(c_natural, rep_cols, False, lambda r: r % G, lambda c: c % G)
    return fwd, inv_swapped, inv_natural


def _kron_expand_kernel(c_ref, rep_ref, o_ref, *, left, row_key, lane_key):
    comp = c_ref[0].astype(BF16)
    rep = rep_ref[...].astype(BF16)
    full = (jnp.dot(rep, comp, preferred_element_type=F32) if left
            else jnp.dot(comp, rep, preferred_element_type=F32))
    rows = lax.broadcasted_iota(jnp.int32, full.shape, 0)
    lanes = lax.broadcasted_iota(jnp.int32, full.shape, 1)
    o_ref[0] = jnp.where(row_key(rows) == lane_key(lanes), full, 0.0).astype(BF16)


def _hy_stage1_kernel(x_ref, t_ref, re_ref, im_ref, *, G, n1, ct):
    h1, cols = x_ref.shape[1], x_ref.shape[2]
    x = x_ref[...].reshape(G * h1, cols).astype(BF16)
    for c0 in range(0, cols, ct):
        a = jnp.dot(t_ref[0], x[:, c0:c0 + ct], preferred_element_type=F32)
        re_ref[:, :, c0:c0 + ct] = a[:n1 * G].reshape(n1, G, ct).astype(BF16)
        im_ref[:, :, c0:c0 + ct] = a[n1 * G:].reshape(n1, G, ct).astype(BF16)


def _hy_stage1(x3, t_fwd, *, G):
    n2, h1, cols = x3.shape
    n1 = t_fwd.shape[1] // (2 * G)
    ospec = pl.BlockSpec((n1, G, cols), lambda j: (0, j, 0))
    oshape = jax.ShapeDtypeStruct((n1, n2, cols), BF16)
    return pl.pallas_call(
        functools.partial(_hy_stage1_kernel, G=G, n1=n1, ct=D_HY),
        grid=(n2 // G,),
        in_specs=[pl.BlockSpec((G, h1, cols), lambda j: (j, 0, 0)),
                  pl.BlockSpec((1,) + t_fwd.shape[1:], lambda j: (j, 0, 0))],
        out_specs=[ospec, ospec],
        out_shape=[oshape, oshape],
        compiler_params=_cp("arbitrary"),
        name="hy_stage1",
    )(x3, t_fwd)


def _hy_fspec2_kernel(re_ref, im_ref, t2_ref, fre_ref, fim_ref, *, kb, k2):
    for j in range(kb):
        a2 = jnp.concatenate([re_ref[j], im_ref[j]], axis=0)
        x = jnp.dot(t2_ref[...], a2, preferred_element_type=F32)
        fre_ref[j] = x[:k2]
        fim_ref[j] = x[k2:]


def _hy_stage2_kernel(re_ref, im_ref, fre_ref, fim_ref, skip_ref, t2_ref, t2i_ref, ore_ref, oim_ref,
                      *, kb, k2, n2, nbc, order):
    for j in range(kb):
        a2 = jnp.concatenate([re_ref[j], im_ref[j]], axis=0)
        x = jnp.dot(t2_ref[...], a2, preferred_element_type=F32)
        fr, fi = _filter_spectrum(fre_ref[j], fim_ref[j], order, skip_ref[0])
        y = _spectral_product(x[:k2], x[k2:], fr, fi, nbc).astype(BF16)
        a = jnp.dot(t2i_ref[...], y, preferred_element_type=F32)
        ore_ref[j] = a[:n2].astype(BF16)
        oim_ref[j] = a[n2:].astype(BF16)


def _hy_stage3_kernel(re_ref, im_ref, t_ref, g_ref, o_ref, *, G, n1, ct):
    cols = re_ref.shape[2]
    for c0 in range(0, cols, ct):
        sl = slice(c0, c0 + ct)
        a2 = jnp.concatenate([re_ref[:, :, sl].reshape(n1 * G, ct), im_ref[:, :, sl].reshape(n1 * G, ct)], axis=0)
        conv = jnp.dot(t_ref[0], a2, preferred_element_type=F32)
        g = g_ref[:, :, sl]
        o_ref[:, :, sl] = (g * conv.reshape(g.shape)).astype(o_ref.dtype)


def _hy_stage31_kernel(re_ref, im_ref, ti_ref, g_ref, tf_ref, ore_ref, oim_ref, *, G, n1, ct):
    cols = re_ref.shape[2]
    for c0 in range(0, cols, ct):
        sl = slice(c0, c0 + ct)
        a2 = jnp.concatenate([re_ref[:, :, sl].reshape(n1 * G, ct), im_ref[:, :, sl].reshape(n1 * G, ct)], axis=0)
        conv = jnp.dot(ti_ref[0], a2, preferred_element_type=F32)
        z = g_ref[:, :, sl].reshape(conv.shape) * conv
        a = jnp.dot(tf_ref[0], z.astype(BF16), preferred_element_type=F32)
        ore_ref[:, :, sl] = a[:n1 * G].reshape(n1, G, ct).astype(BF16)
        oim_ref[:, :, sl] = a[n1 * G:].reshape(n1, G, ct).astype(BF16)


def _two_level_tables(L, *, n1=FFT_N1, G=FFT_G):
    t1, t2, t2i, t1i = _dft_two_level_tables(L, n1)
    return (jnp.asarray(t2).astype(BF16), jnp.asarray(t2i).astype(BF16)) + _kron_tables(t1, t1i, G)


def _hyena_two_level(z3, g1_3, g2, filt, skip, tables, *, n1=FFT_N1, G=FFT_G, kb=FFT_KB):
    n2, h1, cols = z3.shape
    L = n2 * h1
    k2 = n2 // 2
    nbc = cols // D_HY
    t2_bf, t2i_bf, t_fwd, t_inv_swapped, t_inv_natural = tables
    nf = filt.shape[1]

    f_re, f_im = _hy_stage1(filt.reshape(n2, h1, nf), t_fwd, G=G)
    blk = lambda c: pl.BlockSpec((kb, n2, c), lambda i: (i, 0, 0))
    hblk = lambda c: pl.BlockSpec((kb, k2, c), lambda i: (i, 0, 0))
    const = lambda s: pl.BlockSpec(s, lambda i: (0, 0))
    fshape = jax.ShapeDtypeStruct((n1, k2, nf), F32)
    fre, fim = pl.pallas_call(
        functools.partial(_hy_fspec2_kernel, kb=kb, k2=k2),
        grid=(n1 // kb,),
        in_specs=[blk(nf), blk(nf), const((n2, 2 * n2))],
        out_specs=[hblk(nf), hblk(nf)],
        out_shape=[fshape, fshape],
        compiler_params=_cp("arbitrary"),
        name="hy_fspec2",
    )(f_re, f_im, t2_bf)

    ashape = jax.ShapeDtypeStruct((n1, n2, cols), BF16)
    grp = pl.BlockSpec((n1, G, cols), lambda j: (0, j, 0))
    tspec = lambda t: pl.BlockSpec((1,) + t.shape[1:], lambda j: (j, 0, 0))
    a_re, a_im = _hy_stage1(z3, t_fwd, G=G)
    for order in range(2):
        b_re, b_im = pl.pallas_call(
            functools.partial(_hy_stage2_kernel, kb=kb, k2=k2, n2=n2, nbc=nbc, order=order),
            grid=(n1 // kb,),
            in_specs=[blk(cols), blk(cols), hblk(nf), hblk(nf),
                      pl.BlockSpec((1, 1, D_HY), lambda i: (order, 0, 0)),
                      const((n2, 2 * n2)), const((2 * n2, n2))],
            out_specs=[blk(cols), blk(cols)],
            out_shape=[ashape, ashape],
            compiler_params=_cp("arbitrary"),
            name="hy_stage2",
        )(a_re, a_im, fre, fim, skip[:, None, :], t2_bf, t2i_bf)
        if order == 0:
            a_re, a_im = pl.pallas_call(
                functools.partial(_hy_stage31_kernel, G=G, n1=n1, ct=D_HY),
                grid=(n2 // G,),
                in_specs=[grp, grp, tspec(t_inv_swapped), pl.BlockSpec((G, h1, cols), lambda j: (j, 0, 0)),
                          tspec(t_fwd)],
                out_specs=[grp, grp],
                out_shape=[ashape, ashape],
                compiler_params=_cp("arbitrary"),
                name="hy_stage31",
            )(b_re, b_im, t_inv_swapped, g1_3, t_fwd)
        else:
            io = pl.BlockSpec((h1, G, cols), lambda j: (0, j, 0))
            out = pl.pallas_call(
                functools.partial(_hy_stage3_kernel, G=G, n1=n1, ct=D_HY),
                grid=(n2 // G,),
                in_specs=[grp, grp, tspec(t_inv_natural), io],
                out_specs=io,
                out_shape=jax.ShapeDtypeStruct((h1, n2, cols), BF16),
                compiler_params=_cp("arbitrary"),
                name="hy_stage3",
            )(b_re, b_im, t_inv_natural, g2.reshape(h1, n2, cols))
    return out.reshape(L, cols)


def _hyena(hv_t, h1_t, h2_t, conv_w, conv_b, filt, skip, tables=None):
    L, cols = hv_t.shape
    if _two_level(L):
        z3, g1_3, g2 = _hy_conv3(hv_t, h1_t, h2_t, conv_w, conv_b, swap=(True, True, False))
        return _hyena_two_level(z3, g1_3, g2, filt, skip, tables or _two_level_tables(L))
    z, g1, g2 = _hy_conv3(hv_t, h1_t, h2_t, conv_w, conv_b)
    return _hyena_direct(z, g1, g2, filt, skip)


def _route(logits):
    lane_i = lax.broadcasted_iota(jnp.int32, logits.shape, 1)
    lane = lane_i.astype(F32)
    is_g = (lane_i >= N_EXPERTS) & (lane_i < N_EXPERTS + N_GROUPS)
    big = float(LANES)
    gl = jnp.where(is_g, logits, NEG)
    gmax = jnp.max(gl, axis=-1, keepdims=True)
    gidx = jnp.min(jnp.where(gl == gmax, lane - N_EXPERTS, big), axis=-1, keepdims=True)
    p_group = 1.0 / jnp.sum(jnp.where(is_g, jnp.exp(gl - gmax), 0.0), axis=-1, keepdims=True)
    sel = (lane_i < N_EXPERTS) & ((lane_i // EXPERTS_PER_GROUP) == gidx.astype(jnp.int32))
    el = jnp.where(sel, logits, NEG)
    ee = jnp.where(sel, jnp.exp(el - jnp.max(el, axis=-1, keepdims=True)), 0.0)
    pe = ee / jnp.sum(ee, axis=-1, keepdims=True)
    p1 = jnp.max(jnp.where(sel, pe, -1.0), axis=-1, keepdims=True)
    i1 = jnp.min(jnp.where(sel & (pe == p1), lane, big), axis=-1, keepdims=True)
    rest = sel & (lane != i1)
    p2 = jnp.max(jnp.where(rest, pe, -1.0), axis=-1, keepdims=True)
    i2 = jnp.min(jnp.where(rest & (pe == p2), lane, big), axis=-1, keepdims=True)
    den = p1 + p2
    cmb = jnp.where(lane == i1, p_group * (p1 / den), jnp.where(lane == i2, p_group * (p2 / den), 0.0))
    return jnp.where(lane_i == N_EXPERTS, gidx, cmb)


def _outproj_kernel(att_ref, hf_ref, hb_ref, gt_ref, hy_ref, x_ref, mod_ref, w_ref, g_ref, b_ref,
                    wr_ref, br_ref, x1_ref, u2_ref, cmb_ref):
    rg = (hf_ref[...] + hb_ref[...]) * jax.nn.gelu(gt_ref[...].astype(F32))
    mix = (jnp.dot(att_ref[0], w_ref[0:D_ATT, :], preferred_element_type=F32)
           + jnp.dot(rg.astype(BF16), w_ref[D_ATT:D_ATT + D_RNN, :], preferred_element_type=F32)
           + jnp.dot(hy_ref[...].astype(BF16), w_ref[D_ATT + D_RNN:, :], preferred_element_type=F32))
    m = mod_ref[0]
    x1 = _ln(ALPHA * x_ref[...] + m[2:3] * mix) * g_ref[...] + b_ref[...]
    u2 = _ln(x1) * (1.0 + m[4:5]) + m[3:4]
    u_hi = u2.astype(BF16)
    u_lo = (u2 - u_hi.astype(F32)).astype(BF16)
    wr = wr_ref[...]
    w_hi = wr.astype(BF16)
    w_lo = (wr - w_hi.astype(F32)).astype(BF16)
    hi = jnp.dot(u_hi, jnp.concatenate([w_hi, w_lo], axis=1), preferred_element_type=F32)
    lo = jnp.dot(u_lo, w_hi, preferred_element_type=F32)
    logits = hi[:, 0:LANES] + hi[:, LANES:] + lo + br_ref[...]
    x1_ref[...] = x1
    u2_ref[...] = u_hi
    cmb_ref[...] = _route(logits)


def _outproj(att, hf_t, hb_t, gt_t, hy_t, x2d, mod_l, w_bf, ln_g, ln_b, wr, br, *, nb, L, tm,
             cond_base, cond_per_batch):
    tpb = L // tm
    d = x2d.shape[1]
    cond = (lambda i: cond_base + i // tpb) if cond_per_batch else (lambda i: cond_base)
    t_spec = pl.BlockSpec((tm, D_RNN), lambda i: (i % tpb, i // tpb))
    tok = lambda c: pl.BlockSpec((tm, c), lambda i: (i, 0))
    const = lambda s: pl.BlockSpec(s, lambda i: (0, 0))
    n = nb * L
    return pl.pallas_call(
        _outproj_kernel,
        grid=(nb * tpb,),
        in_specs=[pl.BlockSpec((1, tm, D_ATT), lambda i: (i // tpb, i % tpb, 0)),
                  t_spec, t_spec, t_spec, t_spec, tok(d),
                  pl.BlockSpec((1, 6, d), lambda i: (cond(i), 0, 0)),
                  const((d, d)), const((1, d)), const((1, d)), const((d, LANES)), const((1, LANES))],
        out_specs=[tok(d), tok(d), tok(LANES)],
        out_shape=[jax.ShapeDtypeStruct((n, d), F32), jax.ShapeDtypeStruct((n, d), BF16),
                   jax.ShapeDtypeStruct((n, LANES), F32)],
        compiler_params=_cp("arbitrary"),
        name="outproj",
    )(att, hf_t, hb_t, gt_t, hy_t, x2d, mod_l, w_bf, ln_g[None], ln_b[None], wr, br)


def _moe_kernel(u_ref, cmb_ref, x1_ref, mod_ref, wgu_ref, wd_ref, g_ref, b_ref, o_ref, ys_ref, *, T, BLK):
    SL = T + N_GROUPS * BLK
    u = u_ref[...]
    cmb = cmb_ref[...]
    lane_f = lax.broadcasted_iota(jnp.int32, (T, LANES), 1).astype(F32)
    gid_col = jnp.sum(jnp.where(lane_f == float(N_EXPERTS), cmb, 0.0), axis=-1, keepdims=True)
    onehot = jnp.where(lane_f == gid_col, 1.0, 0.0)
    ri = lax.broadcasted_iota(jnp.int32, (T, T), 0)
    ci = lax.broadcasted_iota(jnp.int32, (T, T), 1)
    before = jnp.where(ci < ri, 1.0, 0.0).astype(BF16)
    rank_col = jnp.sum(onehot * jnp.dot(before, onehot.astype(BF16), preferred_element_type=F32),
                       axis=-1, keepdims=True)
    counts = jnp.sum(onehot, axis=0, keepdims=True)
    lane_row = lax.broadcasted_iota(jnp.int32, (1, LANES), 1)
    base = jnp.int32(0)
    bases = []
    base_row = jnp.zeros((1, LANES), jnp.int32)
    for c in range(N_GROUPS):
        n_c = jnp.sum(jnp.where(lane_row == c, counts, 0.0)).astype(jnp.int32)
        bases.append(base)
        base_row = jnp.where(lane_row == c, base, base_row)
        base = base + ((n_c + (BLK - 1)) // BLK) * BLK
    total = base
    dest_f = jnp.sum(onehot * base_row.astype(F32), axis=-1, keepdims=True) + rank_col
    dest_col = dest_f.astype(jnp.int32)
    dest_row = jnp.broadcast_to(dest_f, (T, LANES)).T[0:1, :].astype(jnp.int32)
    cmb_hi = cmb.astype(BF16)
    cmb_lo = (cmb - cmb_hi.astype(F32)).astype(BF16)
    blk_rows = lax.broadcasted_iota(jnp.int32, (BLK, T), 0)
    blk_lane = lax.broadcasted_iota(jnp.int32, (BLK, LANES), 1)

    def block(b, carry):
        r0 = b * BLK
        rows = pl.ds(pl.multiple_of(r0, BLK), BLK)

        @pl.when(r0 < total)
        def _():
            grp = ((r0 >= bases[1]).astype(jnp.int32) + (r0 >= bases[2]).astype(jnp.int32)
                   + (r0 >= bases[3]).astype(jnp.int32))
            p = jnp.where(blk_rows + r0 == dest_row, 1.0, 0.0).astype(BF16)
            xb = jnp.dot(p, u, preferred_element_type=F32).astype(BF16)
            wb = (jnp.dot(p, cmb_hi, preferred_element_type=F32)
                  + jnp.dot(p, cmb_lo, preferred_element_type=F32))
            gu = jnp.dot(xb, wgu_ref[grp], preferred_element_type=F32)
            hs = []
            for j in range(EXPERTS_PER_GROUP):
                gate = gu[:, 2 * j * D_EXPERT:(2 * j + 1) * D_EXPERT]
                up = gu[:, (2 * j + 1) * D_EXPERT:(2 * j + 2) * D_EXPERT]
                w = jnp.sum(jnp.where(blk_lane == grp * EXPERTS_PER_GROUP + j, wb, 0.0), axis=-1, keepdims=True)
                hs.append((gate * _sigmoid(gate) * up * w).astype(BF16))
            ys_ref[rows, :] = jnp.dot(jnp.concatenate(hs, axis=1), wd_ref[grp],
                                      preferred_element_type=F32).astype(BF16)

        @pl.when(r0 >= total)
        def _():
            ys_ref[rows, :] = jnp.zeros((BLK, u.shape[1]), BF16)

        return carry

    lax.fori_loop(0, SL // BLK, block, 0)
    p_t = jnp.where(lax.broadcasted_iota(jnp.int32, (T, SL), 1) == dest_col, 1.0, 0.0).astype(BF16)
    moe = jnp.dot(p_t, ys_ref[...], preferred_element_type=F32)
    m = mod_ref[0]
    o_ref[...] = _ln(ALPHA * x1_ref[...] + m[5:6] * moe) * g_ref[...] + b_ref[...]


def _group_gate_up(w_gate, w_up):
    _, d, f = w_gate.shape
    gu = jnp.concatenate([w_gate, w_up], axis=-1).astype(BF16).reshape(N_GROUPS, EXPERTS_PER_GROUP, d, 2 * f)
    return gu.transpose(0, 2, 1, 3).reshape(N_GROUPS, d, EXPERTS_PER_GROUP * 2 * f)


def _moe(u2, cmb, x1, mod_l, wgu, wd, ln_g, ln_b, *, tm, tiles_per_cond, cond_base, cond_per_batch):
    n, d = x1.shape
    cond = (lambda i: cond_base + i // tiles_per_cond) if cond_per_batch else (lambda i: cond_base)
    tok = lambda c: pl.BlockSpec((tm, c), lambda i: (i, 0))
    const = lambda s: pl.BlockSpec(s, lambda i: (0,) * len(s))
    resident = lambda s: pl.BlockSpec(s, lambda i: (0,) * len(s), pipeline_mode=pl.Buffered(1))
    return pl.pallas_call(
        functools.partial(_moe_kernel, T=tm, BLK=MOE_BLK),
        grid=(n // tm,),
        in_specs=[tok(d), tok(LANES), tok(d),
                  pl.BlockSpec((1, 6, d), lambda i: (cond(i), 0, 0)),
                  resident(wgu.shape), resident(wd.shape),
                  const((1, d)), const((1, d))],
        out_specs=tok(d),
        out_shape=jax.ShapeDtypeStruct((n, d), F32),
        scratch_shapes=[pltpu.VMEM((tm + N_GROUPS * MOE_BLK, d), BF16)],
        compiler_params=_cp("arbitrary"),
        name="moe",
    )(u2, cmb, x1, mod_l, wgu, wd, ln_g[None], ln_b[None])


def _stream_layer(x2d, mod_l, p, *, nb, L, tm, cond_base, cond_per_batch, ctx, cache=None, h0=None, prev_kv=None):
    kw = dict(nb=nb, L=L, tm=tm, cond_base=cond_base, cond_per_batch=cond_per_batch)
    q, k, v, xr_t, gt_t, hv_t, h1_t, h2_t = _inproj(x2d, mod_l, p["w_in"], kv_dtype=F32 if ctx else BF16,
                                                    prev_kv=prev_kv, **{**kw, "tm": min(L, INPROJ_TILE)})
    if ctx:
        att = _ctx_attention(q, k, v)
    else:
        att = _nbr_attention(q, k, v, cache[0], cache[1], p["nbr_bias"])
    a_f, b_f, a_b, b_b = _rg_gates(xr_t, p["rg_conv_w"], p["rg_conv_b"], *p["rg_w"])
    hf_t, hb_t = _rg_scan(a_f, b_f, a_b, b_b, h0[0], h0[1])
    hy_t = _hyena(hv_t, h1_t, h2_t, p["hy_conv_w"], p["hy_conv_b"], p["hy_filt"][L], p["hy_skip"],
                  p["hy_tables"].get(L))
    x1, u2, cmb = _outproj(att, hf_t, hb_t, gt_t, hy_t, x2d, mod_l, p["w_out"], p["ln1_g"], p["ln1_b"],
                           p["router_w"], p["router_b"], **kw)
    tm_moe = min(MOE_TILE, nb * L)
    x2 = _moe(u2, cmb, x1, mod_l, p["moe_wgu"], p["moe_wd"], p["ln2_g"], p["ln2_b"], tm=tm_moe,
              tiles_per_cond=max(L // tm_moe, 1), cond_base=cond_base, cond_per_batch=cond_per_batch)
    return x2, k, v, hf_t, hb_t


def kernel(x_prompt, x_sample, cache_k, cache_v, state_rglru, c, c_ctx, w_mod, b_mod, w_in, attn_rpb, rg_conv_w, rg_conv_b, rg_wa, rg_ba, rg_wx, rg_bx, rg_lambda, hy_conv_w, hy_conv_b, hy_w1, hy_b1, hy_w2, hy_b2, hy_w3, hy_freq, hy_decay, hy_skip, w_out, ln1_g, ln1_b, router_wg, router_bg, router_we, router_be, moe_w_gate, moe_w_up, moe_w_down, ln2_g, ln2_b):
    nbc, Lc, d = x_prompt.shape
    nbl, Ll, _ = x_sample.shape
    depth = w_mod.shape[0]
    past = cache_k.shape[2]

    cond8 = jnp.zeros((8, d), F32).at[0].set(c_ctx).at[1:1 + nbl].set(c)
    mod = _modulation(cond8, w_mod, b_mod).reshape(depth, 8, 6, d)

    xp = x_prompt.reshape(nbc * Lc, d)
    xs = x_sample.reshape(nbl * Ll, d)
    kv, new_h = None, []
    hy_tables = {L: _two_level_tables(L) for L in {Lc, Ll} if _two_level(L)}
    for l in range(depth):
        pad = jnp.zeros((d, LANES - N_EXPERTS - N_GROUPS), F32)
        p = {
            "w_in": w_in[l].astype(BF16),
            "nbr_bias": _nbr_bias_tiles(attn_rpb[l]),
            "rg_conv_w": rg_conv_w[l], "rg_conv_b": rg_conv_b[l],
            "rg_w": _rg_weights(rg_wa[l], rg_ba[l], rg_wx[l], rg_bx[l], rg_lambda[l]),
            "hy_conv_w": hy_conv_w[l], "hy_conv_b": hy_conv_b[l], "hy_skip": hy_skip[l], "hy_tables": hy_tables,
            "hy_filt": {L: _hy_filters(L, hy_w1[l], hy_b1[l], hy_w2[l], hy_b2[l], hy_w3[l], hy_freq[l], hy_decay[l])
                        for L in sorted({Lc, Ll})},
            "w_out": w_out[l].astype(BF16), "ln1_g": ln1_g[l], "ln1_b": ln1_b[l],
            "router_w": jnp.concatenate([router_we[l], router_wg[l], pad], axis=1),
            "router_b": jnp.concatenate([router_be[l], router_bg[l], pad[0]])[None],
            "moe_wgu": _group_gate_up(moe_w_gate[l], moe_w_up[l]),
            "moe_wd": moe_w_down[l].astype(BF16).reshape(N_GROUPS, EXPERTS_PER_GROUP * D_EXPERT, d),
            "ln2_g": ln2_g[l], "ln2_b": ln2_b[l],
        }
        zeros_h = jnp.zeros((1, nbc * D_RNN), F32)
        xp, k, v, hf_t, hb_t = _stream_layer(
            xp, mod[l], p, nb=nbc, L=Lc, tm=Lc, cond_base=0, cond_per_batch=False, ctx=True,
            h0=(zeros_h, zeros_h), prev_kv=kv)
        kv = (k, v)
        new_h.append(jnp.stack([hf_t[Lc - 1].reshape(nbc, D_RNN), hb_t[0].reshape(nbc, D_RNN)], axis=1))

        cache = (cache_k[:, l].reshape(nbl, past, D_ATT), cache_v[:, l].reshape(nbl, past, D_ATT))
        h0 = (state_rglru[:, l, 0].reshape(1, nbl * D_RNN), state_rglru[:, l, 1].reshape(1, nbl * D_RNN))
        xs, _, _, _, _ = _stream_layer(
            xs, mod[l], p, nb=nbl, L=Ll, tm=min(Ll, ROW_CHUNK), cond_base=1, cond_per_batch=True, ctx=False,
            cache=cache, h0=h0)

    return (xp.reshape(nbc, Lc, d), xs.reshape(nbl, Ll, d),
            kv[0].reshape(nbc, depth, Lc, N_HEADS, HEAD_DIM), kv[1].reshape(nbc, depth, Lc, N_HEADS, HEAD_DIM),
            jnp.stack(new_h, axis=1))
```

```python
import functools
import math

import numpy as np
import jax
import jax.numpy as jnp
from jax import lax
from jax.experimental import pallas as pl
from jax.experimental.pallas import tpu as pltpu

F32 = jnp.float32
BF16 = jnp.bfloat16
HIGHEST = lax.Precision.HIGHEST

DEPTH = 2
D_MODEL = 1024
N_HEADS = 8
HEAD_DIM = 64
D_ATT = N_HEADS * HEAD_DIM
D_RNN = 256
D_HY = 256
GRID_W = 64
WIN_R = 8
WIN_C = 16
RG_CONV = 4
RG_C = 8.0
HY_CONV = 3
POS_EMB_DIM = 33
N_GROUPS = 4
EXPERTS_PER_GROUP = 4
N_EXPERTS = 16
D_EXPERT = 256
ALPHA = (2.0 * DEPTH) ** 0.25
LN_EPS = 1e-5
NEG = -1e30

Q_ROWS = 4
K_ROWS = Q_ROWS + WIN_R
LANES = 128
HALO = 8
ROW_CHUNK = 512
SCAN_LANES = 1024
BLOCK_BYTES = 1 << 20
MOD_COL_TILES = 4
FFT_MIN_LEN = 2048
FFT_KB = 4
INPROJ_TILE = 1024
MOE_TILE = 512
MOE_BLK = 128
STRIDE_PAD = 4
FFT_G = 16
FFT_N1 = 64
VMEM_LIMIT = 52 * 1024 * 1024


def _cp(*sem):
    return pltpu.CompilerParams(dimension_semantics=sem, vmem_limit_bytes=VMEM_LIMIT)


def _sigmoid(x):
    return 0.5 * jnp.tanh(0.5 * x) + 0.5


def _ln(x):
    mu = jnp.mean(x, axis=-1, keepdims=True)
    xc = x - mu
    var = jnp.mean(xc * xc, axis=-1, keepdims=True)
    return xc * lax.rsqrt(var + LN_EPS)


def _mod_kernel(c_ref, w_ref, b_ref, o_ref):
    c = c_ref[...]
    s = c * jax.nn.sigmoid(c)
    o_ref[0] = jnp.dot(s, w_ref[0], precision=HIGHEST, preferred_element_type=F32) + b_ref[0]


def _modulation(cond8, w_mod, b_mod):
    depth, d, n = w_mod.shape
    tn = n // MOD_COL_TILES
    return pl.pallas_call(
        _mod_kernel,
        grid=(depth, n // tn),
        in_specs=[pl.BlockSpec((8, d), lambda l, j: (0, 0)),
                  pl.BlockSpec((1, d, tn), lambda l, j: (l, 0, j)),
                  pl.BlockSpec((1, 1, tn), lambda l, j: (l, 0, j))],
        out_specs=pl.BlockSpec((1, 8, tn), lambda l, j: (l, 0, j)),
        out_shape=jax.ShapeDtypeStruct((depth, 8, n), F32),
        compiler_params=_cp("arbitrary", "arbitrary"),
        name="modulation",
    )(cond8, w_mod, b_mod.reshape(depth, 1, n))


def _inproj_kernel(x_ref, mod_ref, w_ref, *refs, n_prev):
    if n_prev:
        pk_ref, pv_ref = refs[:2]
        refs = refs[2:]
    q_ref, k_ref, v_ref, xr_ref, gt_ref, hv_ref, h1_ref, h2_ref = refs
    for i in range(n_prev):
        k_ref[0, i] = pk_ref[0, i]
        v_ref[0, i] = pv_ref[0, i]
    m = mod_ref[0]
    u = (_ln(x_ref[...]) * (1.0 + m[1:2]) + m[0:1]).astype(BF16)

    def proj(lo, hi):
        return jnp.dot(u, w_ref[:, lo:hi], preferred_element_type=F32)

    o = 0
    q_ref[0] = (proj(o, o + D_ATT) * (HEAD_DIM ** -0.5)).astype(q_ref.dtype)
    o += D_ATT
    k_ref[0, n_prev] = proj(o, o + D_ATT).astype(k_ref.dtype)
    o += D_ATT
    v_ref[0, n_prev] = proj(o, o + D_ATT).astype(v_ref.dtype)
    o += D_ATT
    for ref in (xr_ref, gt_ref, hv_ref, h1_ref, h2_ref):
        ref[...] = proj(o, o + D_RNN).astype(ref.dtype)
        o += D_RNN


def _inproj(x2d, mod_l, w_bf, *, nb, L, tm, cond_base, cond_per_batch, kv_dtype, prev_kv=None):
    tpb = L // tm
    d = x2d.shape[1]
    d_in = w_bf.shape[1]
    n_prev = prev_kv[0].shape[1] if prev_kv else 0
    cond = (lambda i: cond_base + i // tpb) if cond_per_batch else (lambda i: cond_base)
    att_spec = pl.BlockSpec((1, tm, D_ATT), lambda i: (i // tpb, i % tpb, 0))
    kv_spec = lambda n: pl.BlockSpec((1, n, tm, D_ATT), lambda i: (i // tpb, 0, i % tpb, 0))
    t_spec = pl.BlockSpec((tm, D_RNN), lambda i: (i % tpb, i // tpb))
    kv_shape = jax.ShapeDtypeStruct((nb, n_prev + 1, L, D_ATT), kv_dtype)
    t_shape = jax.ShapeDtypeStruct((L, nb * D_RNN), BF16)
    return pl.pallas_call(
        functools.partial(_inproj_kernel, n_prev=n_prev),
        grid=(nb * tpb,),
        in_specs=[pl.BlockSpec((tm, d), lambda i: (i, 0)),
                  pl.BlockSpec((1, 6, d), lambda i: (cond(i), 0, 0)),
                  pl.BlockSpec((d, d_in), lambda i: (0, 0))] + ([kv_spec(n_prev)] * 2 if n_prev else []),
        out_specs=[att_spec, kv_spec(n_prev + 1), kv_spec(n_prev + 1), t_spec, t_spec, t_spec, t_spec, t_spec],
        out_shape=[jax.ShapeDtypeStruct((nb, L, D_ATT), BF16), kv_shape, kv_shape,
                   t_shape, t_shape, t_shape, t_shape, t_shape],
        compiler_params=_cp("arbitrary"),
        name="inproj",
    )(x2d, mod_l, w_bf, *(prev_kv or ()))


def _nt(a, b):
    return lax.dot_general(a, b, (((1,), (1,)), ((), ())), preferred_element_type=F32)


def _ctx_attn_kernel(q_ref, k_ref, v_ref, o_ref):
    q = q_ref[0]
    k = k_ref[0, 0].astype(BF16)
    v = v_ref[0, 0].astype(BF16)
    outs = []
    for h in range(N_HEADS):
        sl = slice(h * HEAD_DIM, (h + 1) * HEAD_DIM)
        s = _nt(q[:, sl], k[:, sl])
        e = jnp.exp(s - jnp.max(s, axis=-1, keepdims=True))
        den = jnp.sum(e, axis=-1, keepdims=True)
        outs.append(jnp.dot(e.astype(BF16), v[:, sl], preferred_element_type=F32) / den)
    o_ref[0] = jnp.concatenate(outs, axis=-1).astype(o_ref.dtype)


def _ctx_attention(q, k, v):
    nb, L, _ = q.shape
    last = k.shape[1] - 1
    spec = pl.BlockSpec((1, L, D_ATT), lambda b: (b, 0, 0))
    kv_spec = pl.BlockSpec((1, 1, L, D_ATT), lambda b: (b, last, 0, 0))
    return pl.pallas_call(
        _ctx_attn_kernel,
        grid=(nb,),
        in_specs=[spec, kv_spec, kv_spec],
        out_specs=spec,
        out_shape=jax.ShapeDtypeStruct((nb, L, D_ATT), BF16),
        compiler_params=_cp("arbitrary"),
        name="ctx_attn",
    )(q, k, v)


N_DR = 2 * WIN_R - 1
PAIR_LEFT = N_DR - 1
PAIR_RIGHT = 2 * N_DR - 1
PAIR_NONE = 3 * N_DR - 1


def _nbr_bias_tiles(rpb):
    cols = np.arange(GRID_W)
    c_start = np.clip(cols - WIN_C // 2, 0, GRID_W - WIN_C)
    col_mask = (cols[None, :] >= c_start[:, None]) & (cols[None, :] < c_start[:, None] + WIN_C)
    dc_idx = np.clip(cols[None, :] - cols[:, None], -(WIN_C - 1), WIN_C - 1) + (WIN_C - 1)
    sel = (dc_idx[None] == np.arange(2 * WIN_C - 1)[:, None, None]).astype(np.float32)
    colb = jnp.einsum('hab,bqk->haqk', rpb.astype(F32), jnp.asarray(sel), precision=HIGHEST)
    colb = jnp.where(col_mask[None, None], colb, NEG)
    neg = jnp.full_like(colb, NEG)
    both = jnp.concatenate([colb[:, :-1], colb[:, 1:]], axis=-1)
    left_masked = jnp.concatenate([neg, colb], axis=-1)
    right_masked = jnp.concatenate([colb, neg], axis=-1)
    none = jnp.concatenate([neg[:, :1], neg[:, :1]], axis=-1)
    return jnp.concatenate([both, left_masked, right_masked, none], axis=1)


def _nbr_attn_kernel(q_ref, k_ref, v_ref, kc_ref, vc_ref, bias_ref, o_ref, *, rows):
    g = pl.program_id(1)
    r0 = jnp.clip(Q_ROWS * g - WIN_R // 2, 0, rows - K_ROWS)
    start = pl.multiple_of(r0 * GRID_W, Q_ROWS * GRID_W)
    nk = K_ROWS * GRID_W
    kw = k_ref[0, 0, pl.ds(start, nk), :]
    vw = v_ref[0, 0, pl.ds(start, nk), :]
    kc = kc_ref[0].astype(BF16)
    vc = vc_ref[0].astype(BF16)
    q = q_ref[0]
    tile = []
    for j in range(Q_ROWS):
        r = Q_ROWS * g + j
        lo = jnp.clip(r - WIN_R // 2, 0, rows - WIN_R) - r + (WIN_R - 1)
        tile.append([])
        for p in range(K_ROWS // 2):
            d0 = r0 + 2 * p - r + (WIN_R - 1)
            lv = (d0 >= lo) & (d0 < lo + WIN_R)
            rv = (d0 + 1 >= lo) & (d0 + 1 < lo + WIN_R)
            idx = jnp.where(lv & rv, d0, jnp.where(rv, PAIR_LEFT + d0 + 1, jnp.where(lv, PAIR_RIGHT + d0, PAIR_NONE)))
            tile[j].append(jnp.clip(idx, 0, PAIR_NONE))
    outs = []
    for h in range(N_HEADS):
        sl = slice(h * HEAD_DIM, (h + 1) * HEAD_DIM)
        qh = q[:, sl]
        bias = jnp.concatenate(
            [jnp.concatenate([bias_ref[h, tile[j][p]] for p in range(K_ROWS // 2)], axis=-1)
             for j in range(Q_ROWS)], axis=0)
        s1 = _nt(qh, kw[:, sl]) + bias
        s2 = _nt(qh, kc[:, sl])
        m = jnp.maximum(jnp.max(s1, axis=-1, keepdims=True), jnp.max(s2, axis=-1, keepdims=True))
        e1 = jnp.exp(s1 - m)
        e2 = jnp.exp(s2 - m)
        den = jnp.sum(e1, axis=-1, keepdims=True) + jnp.sum(e2, axis=-1, keepdims=True)
        o = (jnp.dot(e1.astype(BF16), vw[:, sl], preferred_element_type=F32)
             + jnp.dot(e2.astype(BF16), vc[:, sl], preferred_element_type=F32))
        outs.append(o / den)
    o_ref[0] = jnp.concatenate(outs, axis=-1).astype(o_ref.dtype)


def _nbr_attention(q, k, v, kc, vc, bias):
    nb, L, _ = q.shape
    rows = L // GRID_W
    ng = rows // Q_ROWS
    past = kc.shape[1]
    tq = Q_ROWS * GRID_W
    full = pl.BlockSpec((1, 1, L, D_ATT), lambda b, g: (b, 0, 0, 0))
    ctx = pl.BlockSpec((1, past, D_ATT), lambda b, g: (b, 0, 0))
    qspec = pl.BlockSpec((1, tq, D_ATT), lambda b, g: (b, g, 0))
    bspec = pl.BlockSpec(bias.shape, lambda b, g: (0, 0, 0, 0), pipeline_mode=pl.Buffered(1))
    return pl.pallas_call(
        functools.partial(_nbr_attn_kernel, rows=rows),
        grid=(nb, ng),
        in_specs=[qspec, full, full, ctx, ctx, bspec],
        out_specs=qspec,
        out_shape=jax.ShapeDtypeStruct((nb, L, D_ATT), BF16),
        compiler_params=_cp("arbitrary", "arbitrary"),
        name="nbr_attn",
    )(q, k, v, kc, vc, bias)


def _fill_padded(xp_ref, src_ref, L):
    z = jnp.zeros((HALO, xp_ref.shape[1]), F32)
    xp_ref[0:HALO, :] = z
    xp_ref[HALO:HALO + L, :] = src_ref[...].astype(F32)
    xp_ref[HALO + L:2 * HALO + L, :] = z


def _rg_gates_kernel(x_ref, cw_ref, cb_ref, w_ref, b_ref, lam_ref,
                     af_ref, bf_ref, ab_ref, bb_ref, xp_ref, *, L, tc, m):
    nh = cw_ref.shape[0]
    for sc in range(m):
        half = (pl.program_id(0) * m + sc) % nh
        cols = slice(sc * LANES, (sc + 1) * LANES)
        _fill_padded(xp_ref, x_ref.at[:, cols], L)
        sp = jax.nn.softplus(-lam_ref[half])
        cw = cw_ref[half]
        for t0 in range(0, L, tc):
            xc = cb_ref[half]
            for k in range(RG_CONV):
                xc = xc + cw[k:k + 1, :] * xp_ref[pl.ds(HALO + t0 + k - RG_CONV // 2, tc), :]
            z = jnp.dot(xc.astype(BF16), w_ref[half], preferred_element_type=F32) + b_ref[half]
            for e, (a_ref, o_ref) in enumerate(((af_ref, bf_ref), (ab_ref, bb_ref))):
                r = _sigmoid(z[:, e * LANES:(e + 1) * LANES])
                i = _sigmoid(z[:, (2 + e) * LANES:(3 + e) * LANES])
                log_a = (-RG_C) * r * sp[:, e * LANES:(e + 1) * LANES]
                a = jnp.exp(log_a)
                one_minus_a2 = -jnp.tanh(log_a) * (a * a + 1.0)
                a_ref[pl.ds(t0, tc), cols] = a
                o_ref[pl.ds(t0, tc), cols] = jnp.sqrt(one_minus_a2) * i * xc


def _blockdiag2(w2):
    z = jnp.zeros_like(w2[0])
    return jnp.concatenate([jnp.concatenate([w2[0], z], 1), jnp.concatenate([z, w2[1]], 1)], 0)


def _rg_weights(wa, ba, wx, bx, lam):
    ws, bs, ls = [], [], []
    for hf in range(D_RNN // LANES):
        blk = slice(2 * hf, 2 * hf + 2)
        ch = slice(hf * LANES, (hf + 1) * LANES)
        ws.append(jnp.concatenate([_blockdiag2(wa[0, blk]), _blockdiag2(wa[1, blk]),
                                   _blockdiag2(wx[0, blk]), _blockdiag2(wx[1, blk])], axis=1))
        bs.append(jnp.concatenate([ba[0, ch], ba[1, ch], bx[0, ch], bx[1, ch]])[None])
        ls.append(jnp.concatenate([lam[0, ch], lam[1, ch]])[None])
    return jnp.stack(ws).astype(BF16), jnp.stack(bs), jnp.stack(ls)


def _lane_chunks(L, cols):
    m = max(1, min(8, BLOCK_BYTES // (L * LANES * 4)))
    while cols % (m * LANES):
        m //= 2
    return m


def _rg_gates(xr_t, conv_w, conv_b, w_bd, b_cat, lam_cat):
    L, cols = xr_t.shape
    nh = D_RNN // LANES
    tc = min(L, ROW_CHUNK)
    m = _lane_chunks(L, cols)
    col = pl.BlockSpec((L, m * LANES), lambda c: (0, c))
    const = lambda a: pl.BlockSpec(a.shape, lambda c: (0,) * a.ndim)
    shape = jax.ShapeDtypeStruct((L, cols), F32)
    cw = conv_w.reshape(RG_CONV, nh, LANES).transpose(1, 0, 2)
    cb = conv_b.reshape(nh, 1, LANES)
    return pl.pallas_call(
        functools.partial(_rg_gates_kernel, L=L, tc=tc, m=m),
        grid=(cols // (m * LANES),),
        in_specs=[col, const(cw), const(cb), const(w_bd), const(b_cat), const(lam_cat)],
        out_specs=[col, col, col, col],
        out_shape=[shape, shape, shape, shape],
        scratch_shapes=[pltpu.VMEM((L + 2 * HALO, LANES), F32)],
        compiler_params=_cp("arbitrary"),
        name="rg_gates",
    )(xr_t, cw, cb, w_bd, b_cat, lam_cat)


def _rg_scan_kernel(af_ref, bf_ref, ab_ref, bb_ref, h0f_ref, h0b_ref, hf_ref, hb_ref, cf_ref, cb_ref, *, tc):
    @pl.when(pl.program_id(1) == 0)
    def _():
        cf_ref[...] = h0f_ref[...]
        cb_ref[...] = h0b_ref[...]

    def body(t, carry):
        hf, hb = carry
        hf = af_ref[pl.ds(t, 1), :] * hf + bf_ref[pl.ds(t, 1), :]
        hf_ref[pl.ds(t, 1), :] = hf
        tb = tc - 1 - t
        hb = ab_ref[pl.ds(tb, 1), :] * hb + bb_ref[pl.ds(tb, 1), :]
        hb_ref[pl.ds(tb, 1), :] = hb
        return hf, hb

    hf, hb = lax.fori_loop(0, tc, body, (cf_ref[...], cb_ref[...]), unroll=8)
    cf_ref[...] = hf
    cb_ref[...] = hb


def _rg_scan(a_f, b_f, a_b, b_b, h0f, h0b):
    L, cols = a_f.shape
    cw = min(cols, SCAN_LANES)
    tc = min(L, ROW_CHUNK)
    nt = L // tc
    fwd = pl.BlockSpec((tc, cw), lambda c, i: (i, c))
    bwd = pl.BlockSpec((tc, cw), lambda c, i: (nt - 1 - i, c))
    row = pl.BlockSpec((1, cw), lambda c, i: (0, c))
    shape = jax.ShapeDtypeStruct((L, cols), F32)
    return pl.pallas_call(
        functools.partial(_rg_scan_kernel, tc=tc),
        grid=(cols // cw, nt),
        in_specs=[fwd, fwd, bwd, bwd, row, row],
        out_specs=[fwd, bwd],
        out_shape=[shape, shape],
        scratch_shapes=[pltpu.VMEM((1, cw), F32), pltpu.VMEM((1, cw), F32)],
        compiler_params=_cp("arbitrary", "arbitrary"),
        name="rg_scan",
    )(a_f, b_f, a_b, b_b, h0f, h0b)


def _hy_filter_kernel(feat_ref, w1_ref, b1_ref, w2_ref, b2_ref, w3_ref, fr_ref, dec_ref, o_ref, *, tm):
    feats = feat_ref[...]
    dot = functools.partial(jnp.dot, precision=HIGHEST, preferred_element_type=F32)
    h = jnp.sin(fr_ref[0:1, :] * (dot(feats, w1_ref[...]) + b1_ref[...]))
    h = jnp.sin(fr_ref[1:2, :] * (dot(h, w2_ref[...]) + b2_ref[...]))
    h = dot(h, w3_ref[...])
    filt = h * jnp.exp(-feats[:, 0:1] * jnp.abs(dec_ref[...]))
    row = pl.program_id(0) * tm + lax.broadcasted_iota(jnp.int32, filt.shape, 0)
    lane = lax.broadcasted_iota(jnp.int32, filt.shape, 1)
    o_ref[...] = jnp.where((row == 0) & (lane >= 2 * D_HY), 0.0, filt)


def _hy_features(L):
    pos = np.arange(L, dtype=np.float64)
    t = np.linspace(0.0, 1.0, L)
    bands = (POS_EMB_DIM - 1) // 2
    f = np.linspace(1e-4, bands - 1, bands)
    ang = (2.0 * math.pi / L) * pos[:, None] * f[None, :]
    feats = np.concatenate([t[:, None], np.cos(ang), -np.sin(ang)], axis=-1)
    out = np.zeros((L, LANES), np.float32)
    out[:, :POS_EMB_DIM] = feats
    return out


def _two_level(L):
    return L % (FFT_N1 * 8) == 0 and L >= FFT_MIN_LEN


def _digit_swap(x, L):
    n2 = 2 * L // FFT_N1
    return x.reshape(L // n2, n2, -1).transpose(1, 0, 2).reshape(L, -1)


def _hy_filters(L, w1, b1, w2, b2, w3, freq, decay):
    tm = min(L, ROW_CHUNK)
    hid = w2.shape[0]
    n = w3.shape[1]
    w1p = jnp.zeros((LANES, hid), F32).at[:POS_EMB_DIM].set(w1)
    feats = _hy_features(L)
    if _two_level(L):
        feats = _digit_swap(feats, L)
    c = lambda s: pl.BlockSpec(s, lambda i: (0, 0))
    return pl.pallas_call(
        functools.partial(_hy_filter_kernel, tm=tm),
        grid=(L // tm,),
        in_specs=[pl.BlockSpec((tm, LANES), lambda i: (i, 0)),
                  c((LANES, hid)), c((1, hid)), c((hid, hid)), c((1, hid)), c((hid, n)), c((2, hid)), c((1, n))],
        out_specs=pl.BlockSpec((tm, n), lambda i: (i, 0)),
        out_shape=jax.ShapeDtypeStruct((L, n), F32),
        compiler_params=_cp("arbitrary"),
        name="hy_filter",
    )(jnp.asarray(feats), w1p, b1[None], w2, b2[None], w3, freq, decay.reshape(1, n))


def _conv3_kernel(v_ref, a_ref, b_ref, w_ref, cb_ref, ov_ref, oa_ref, ob_ref, xp_ref, xs_ref,
                  *, L, tc, swap, n2, m):
    h1 = L // n2
    pitch = n2 + STRIDE_PAD
    nh = w_ref.shape[0]
    for sc in range(m):
        half = (pl.program_id(0) * m + sc) % nh
        cols = slice(sc * LANES, (sc + 1) * LANES)
        for p, (src, dst) in enumerate(((v_ref, ov_ref), (a_ref, oa_ref), (b_ref, ob_ref))):
            _fill_padded(xp_ref, src.at[:, cols], L)
            w = w_ref[half, p]
            cb = cb_ref[half, p]
            if swap[p]:
                for i1 in range(h1):
                    xs_ref[pl.ds(i1 * pitch, n2 + HY_CONV - 1), :] = (
                        xp_ref[pl.ds(HALO + i1 * n2 - HY_CONV // 2, n2 + HY_CONV - 1), :])

                def body(i2, carry, dst=dst, w=w, cb=cb):
                    y = cb
                    for k in range(HY_CONV):
                        y = y + w[k:k + 1, :] * xs_ref[pl.ds(i2 + k, h1, stride=pitch), :]
                    dst[i2] = y.astype(dst.dtype)
                    return carry

                lax.fori_loop(0, n2, body, 0)
            else:
                for t0 in range(0, L, tc):
                    y = cb
                    for k in range(HY_CONV):
                        y = y + w[k:k + 1, :] * xp_ref[pl.ds(HALO + t0 + k - HY_CONV // 2, tc), :]
                    dst[pl.ds(t0, tc), cols] = y


def _hy_conv3(v_t, a_t, b_t, conv_w, conv_b, swap=(False, False, False)):
    L, cols = v_t.shape
    nh = D_HY // LANES
    tc = min(L, ROW_CHUNK)
    n2 = 2 * L // FFT_N1
    m = 1 if any(swap) else _lane_chunks(L, cols)
    col = pl.BlockSpec((L, m * LANES), lambda c: (0, c))
    col3 = pl.BlockSpec((n2, L // n2, LANES), lambda c: (0, 0, c))
    shape = jax.ShapeDtypeStruct((L, cols), F32)
    shape3 = jax.ShapeDtypeStruct((n2, L // n2, cols), F32)
    w = conv_w.reshape(HY_CONV, 3, nh, LANES).transpose(2, 1, 0, 3)
    cb = conv_b.reshape(3, nh, 1, LANES).transpose(1, 0, 2, 3)
    const = lambda a: pl.BlockSpec(a.shape, lambda c: (0,) * a.ndim)
    return pl.pallas_call(
        functools.partial(_conv3_kernel, L=L, tc=tc, swap=swap, n2=n2, m=m),
        grid=(cols // (m * LANES),),
        in_specs=[col, col, col, const(w), const(cb)],
        out_specs=[col3 if s else col for s in swap],
        out_shape=[shape, shape, shape] if not any(swap) else
                  [jax.ShapeDtypeStruct(shape3.shape, BF16 if i == 0 else F32) if s else shape
                   for i, s in enumerate(swap)],
        scratch_shapes=[pltpu.VMEM((L + 2 * HALO, LANES), F32),
                        pltpu.VMEM(((L // n2) * (n2 + STRIDE_PAD) + HALO, LANES), F32)],
        compiler_params=_cp("arbitrary"),
        name="hy_conv3",
    )(v_t, a_t, b_t, w, cb)


def _phase(p, n2):
    ang = 2.0 * np.pi * (np.asarray(p, np.int64) % n2) / n2
    return np.cos(ang), np.sin(ang)


def _dft_direct_tables(L):
    k = np.arange(L)[:, None]
    t = np.arange(L)[None, :]
    c, s = _phase((2 * k + 1) * t, 4 * L)
    fwd = np.concatenate([c, -s], axis=0)
    inv = (1.0 / L) * np.concatenate([c.T, -s.T], axis=1)
    return fwd.astype(np.float32), inv.astype(np.float32)


def _dft_two_level_tables(L, n1):
    N = 2 * L
    n2 = N // n1
    h1 = n1 // 2
    i2 = np.arange(n2)[:, None, None]
    k1 = np.arange(n1)[None, :, None]
    i1 = np.arange(h1)[None, None, :]
    c, s = _phase(2 * n2 * i1 * k1 + 2 * i2 * k1 + n2 * i1 + i2, 2 * N)
    t1 = np.concatenate([c, -s], axis=1)
    ct, st = c.transpose(0, 2, 1), s.transpose(0, 2, 1)
    t1i = (2.0 / N) * np.concatenate([ct, -st], axis=2)
    kk = np.arange(n2 // 2)[:, None]
    nn = np.arange(n2)[None, :]
    c2, s2 = _phase(nn * kk, n2)
    t2 = np.block([[c2, s2], [-s2, c2]])
    t2i = np.block([[c2.T, -s2.T], [s2.T, c2.T]])
    return tuple(x.astype(np.float32) for x in (t1, t2, t2i, t1i))


def _filter_spectrum(fre, fim, order, skip):
    fw = slice(order * D_HY, (order + 1) * D_HY)
    bw = slice((2 + order) * D_HY, (3 + order) * D_HY)
    return fre[:, fw] + fre[:, bw] + skip, fim[:, fw] - fim[:, bw]


def _spectral_product(xr, xi, fr, fi, nbc):
    yr, yi = [], []
    for b in range(nbc):
        sl = slice(b * D_HY, (b + 1) * D_HY)
        yr.append(xr[:, sl] * fr - xi[:, sl] * fi)
        yi.append(xr[:, sl] * fi + xi[:, sl] * fr)
    return jnp.concatenate([jnp.concatenate(yr, axis=1), jnp.concatenate(yi, axis=1)], axis=0)


def _matmul_kernel(a_ref, b_ref, o_ref):
    o_ref[...] = jnp.dot(a_ref[...], b_ref[...], precision=HIGHEST, preferred_element_type=F32)


def _hy_direct_kernel(z_ref, g_ref, f_ref, skip_ref, t_ref, ti_ref, o_ref, *, L, nbc, order):
    x = jnp.dot(t_ref[...], z_ref[...].astype(BF16), preferred_element_type=F32)
    fr, fi = _filter_spectrum(f_ref[0:L, :], f_ref[L:2 * L, :], order, skip_ref[0])
    y = _spectral_product(x[:L], x[L:], fr, fi, nbc).astype(BF16)
    conv = jnp.dot(ti_ref[...], y, preferred_element_type=F32)
    o_ref[...] = (g_ref[...] * conv).astype(o_ref.dtype)


def _hyena_direct(z, g1, g2, filt, skip):
    L, cols = z.shape
    fwd, inv = _dft_direct_tables(L)
    n = filt.shape[1]
    spec = pl.pallas_call(
        _matmul_kernel,
        out_shape=jax.ShapeDtypeStruct((2 * L, n), F32),
        name="hy_fspec",
    )(jnp.asarray(fwd), filt)
    tn = min(cols, 4 * D_HY)
    nbc = tn // D_HY
    col = pl.BlockSpec((L, tn), lambda c: (0, c))
    const = lambda s: pl.BlockSpec(s, lambda c: (0, 0))
    t_bf, ti_bf = jnp.asarray(fwd).astype(BF16), jnp.asarray(inv).astype(BF16)
    for order, g in enumerate((g1, g2)):
        z = pl.pallas_call(
            functools.partial(_hy_direct_kernel, L=L, nbc=nbc, order=order),
            grid=(cols // tn,),
            in_specs=[col, col, const((2 * L, n)), pl.BlockSpec((1, 1, D_HY), lambda c: (order, 0, 0)),
                      const((2 * L, L)), const((L, 2 * L))],
            out_specs=col,
            out_shape=jax.ShapeDtypeStruct((L, cols), F32 if order == 0 else BF16),
            compiler_params=_cp("arbitrary"),
            name="hy_direct",
        )(z, g, spec, skip[:, None, :], t_bf, ti_bf)
    return z


def _kron_tables(t1, t1i, G):
    n2, two_n1, h1 = t1.shape
    ng = n2 // G
    eye = np.eye(two_n1, dtype=np.float32)
    rep_rows = jnp.asarray(np.repeat(eye, G, axis=0))
    rep_cols = jnp.asarray(np.repeat(eye, G, axis=1))
    c_fwd = t1.reshape(ng, G, two_n1, h1).transpose(0, 2, 1, 3).reshape(ng, two_n1, G * h1)
    c_swapped = t1i.reshape(ng, G * h1, two_n1)
    c_natural = t1i.reshape(ng, G, h1, two_n1).transpose(0, 2, 1, 3).reshape(ng, h1 * G, two_n1)

    def expand(compact, rep, left, row_key, lane_key):
        r, c = (rep.shape[0], compact.shape[2]) if left else (compact.shape[1], rep.shape[1])
        return pl.pallas_call(
            functools.partial(_kron_expand_kernel, left=left, row_key=row_key, lane_key=lane_key),
            grid=(ng,),
            in_specs=[pl.BlockSpec((1,) + compact.shape[1:], lambda i: (i, 0, 0)),
                      pl.BlockSpec(rep.shape, lambda i: (0, 0))],
            out_specs=pl.BlockSpec((1, r, c), lambda i: (i, 0, 0)),
            out_shape=jax.ShapeDtypeStruct((ng, r, c), BF16),
            compiler_params=_cp("arbitrary"),
            name="hy_kron",
        )(jnp.asarray(compact), rep)

    fwd = expand(c_fwd, rep_rows, True, lambda r: r % G, lambda c: c // h1)
    inv_swapped = expand(c_swapped, rep_cols, False, lambda r: r // h1, lambda c: c % G)
    inv_natural = expand(c_natural, rep_cols, False, lambda r: r % G, lambda c: c % G)
    return fwd, inv_swapped, inv_natural


def _kron_expand_kernel(c_ref, rep_ref, o_ref, *, left, row_key, lane_key):
    comp = c_ref[0].astype(BF16)
    rep = rep_ref[...].astype(BF16)
    full = (jnp.dot(rep, comp, preferred_element_type=F32) if left
            else jnp.dot(comp, rep, preferred_element_type=F32))
    rows = lax.broadcasted_iota(jnp.int32, full.shape, 0)
    lanes = lax.broadcasted_iota(jnp.int32, full.shape, 1)
    o_ref[0] = jnp.where(row_key(rows) == lane_key(lanes), full, 0.0).astype(BF16)


def _hy_stage1_kernel(x_ref, t_ref, re_ref, im_ref, *, G, n1, ct):
    h1, cols = x_ref.shape[1], x_ref.shape[2]
    x = x_ref[...].reshape(G * h1, cols).astype(BF16)
    for c0 in range(0, cols, ct):
        a = jnp.dot(t_ref[0], x[:, c0:c0 + ct], preferred_element_type=F32)
        re_ref[:, :, c0:c0 + ct] = a[:n1 * G].reshape(n1, G, ct).astype(BF16)
        im_ref[:, :, c0:c0 + ct] = a[n1 * G:].reshape(n1, G, ct).astype(BF16)


def _hy_stage1(x3, t_fwd, *, G):
    n2, h1, cols = x3.shape
    n1 = t_fwd.shape[1] // (2 * G)
    ospec = pl.BlockSpec((n1, G, cols), lambda j: (0, j, 0))
    oshape = jax.ShapeDtypeStruct((n1, n2, cols), BF16)
    return pl.pallas_call(
        functools.partial(_hy_stage1_kernel, G=G, n1=n1, ct=D_HY),
        grid=(n2 // G,),
        in_specs=[pl.BlockSpec((G, h1, cols), lambda j: (j, 0, 0)),
                  pl.BlockSpec((1,) + t_fwd.shape[1:], lambda j: (j, 0, 0))],
        out_specs=[ospec, ospec],
        out_shape=[oshape, oshape],
        compiler_params=_cp("arbitrary"),
        name="hy_stage1",
    )(x3, t_fwd)


def _hy_fspec2_kernel(re_ref, im_ref, skip_ref, t2_ref, fre_ref, fim_ref, *, kb, k2):
    for j in range(kb):
        a2 = jnp.concatenate([re_ref[j], im_ref[j]], axis=0)
        x = jnp.dot(t2_ref[...], a2, preferred_element_type=F32)
        for order in range(2):
            fr, fi = _filter_spectrum(x[:k2], x[k2:], order, skip_ref[order])
            fre_ref[order, j] = fr
            fim_ref[order, j] = fi


def _hy_stage2_kernel(re_ref, im_ref, fre_ref, fim_ref, t2_ref, t2i_ref, ore_ref, oim_ref, *, kb, k2, n2, nbc):
    for j in range(kb):
        a2 = jnp.concatenate([re_ref[j], im_ref[j]], axis=0)
        x = jnp.dot(t2_ref[...], a2, preferred_element_type=F32)
        fr, fi = fre_ref[0, j], fim_ref[0, j]
        y = _spectral_product(x[:k2], x[k2:], fr, fi, nbc).astype(BF16)
        a = jnp.dot(t2i_ref[...], y, preferred_element_type=F32)
        ore_ref[j] = a[:n2].astype(BF16)
        oim_ref[j] = a[n2:].astype(BF16)


def _hy_stage3_kernel(re_ref, im_ref, t_ref, g_ref, o_ref, *, G, n1, ct):
    cols = re_ref.shape[2]
    for c0 in range(0, cols, ct):
        sl = slice(c0, c0 + ct)
        a2 = jnp.concatenate([re_ref[:, :, sl].reshape(n1 * G, ct), im_ref[:, :, sl].reshape(n1 * G, ct)], axis=0)
        conv = jnp.dot(t_ref[0], a2, preferred_element_type=F32)
        g = g_ref[:, :, sl]
        o_ref[:, :, sl] = (g * conv.reshape(g.shape)).astype(o_ref.dtype)


def _hy_stage31_kernel(re_ref, im_ref, ti_ref, g_ref, tf_ref, ore_ref, oim_ref, *, G, n1, ct):
    cols = re_ref.shape[2]
    for c0 in range(0, cols, ct):
        sl = slice(c0, c0 + ct)
        a2 = jnp.concatenate([re_ref[:, :, sl].reshape(n1 * G, ct), im_ref[:, :, sl].reshape(n1 * G, ct)], axis=0)
        conv = jnp.dot(ti_ref[0], a2, preferred_element_type=F32)
        z = g_ref[:, :, sl].reshape(conv.shape) * conv
        a = jnp.dot(tf_ref[0], z.astype(BF16), preferred_element_type=F32)
        ore_ref[:, :, sl] = a[:n1 * G].reshape(n1, G, ct).astype(BF16)
        oim_ref[:, :, sl] = a[n1 * G:].reshape(n1, G, ct).astype(BF16)


def _two_level_tables(L, *, n1=FFT_N1, G=FFT_G):
    t1, t2, t2i, t1i = _dft_two_level_tables(L, n1)
    return (jnp.asarray(t2).astype(BF16), jnp.asarray(t2i).astype(BF16)) + _kron_tables(t1, t1i, G)


def _hyena_two_level(z3, g1_3, g2, filt, skip, tables, *, n1=FFT_N1, G=FFT_G, kb=FFT_KB):
    n2, h1, cols = z3.shape
    L = n2 * h1
    k2 = n2 // 2
    nbc = cols // D_HY
    t2_bf, t2i_bf, t_fwd, t_inv_swapped, t_inv_natural = tables
    nf = filt.shape[1]

    f_re, f_im = _hy_stage1(filt.reshape(n2, h1, nf), t_fwd, G=G)
    blk = lambda c: pl.BlockSpec((kb, n2, c), lambda i: (i, 0, 0))
    const = lambda s: pl.BlockSpec(s, lambda i: (0, 0))
    fshape = jax.ShapeDtypeStruct((2, n1, k2, D_HY), F32)
    fspec = pl.BlockSpec((2, kb, k2, D_HY), lambda i: (0, i, 0, 0))
    ospec = lambda order: pl.BlockSpec((1, kb, k2, D_HY), lambda i: (order, i, 0, 0))
    fre, fim = pl.pallas_call(
        functools.partial(_hy_fspec2_kernel, kb=kb, k2=k2),
        grid=(n1 // kb,),
        in_specs=[blk(nf), blk(nf), pl.BlockSpec((2, 1, D_HY), lambda i: (0, 0, 0)), const((n2, 2 * n2))],
        out_specs=[fspec, fspec],
        out_shape=[fshape, fshape],
        compiler_params=_cp("arbitrary"),
        name="hy_fspec2",
    )(f_re, f_im, skip[:, None, :], t2_bf)

    ashape = jax.ShapeDtypeStruct((n1, n2, cols), BF16)
    grp = pl.BlockSpec((n1, G, cols), lambda j: (0, j, 0))
    tspec = lambda t: pl.BlockSpec((1,) + t.shape[1:], lambda j: (j, 0, 0))
    a_re, a_im = _hy_stage1(z3, t_fwd, G=G)
    for order in range(2):
        b_re, b_im = pl.pallas_call(
            functools.partial(_hy_stage2_kernel, kb=kb, k2=k2, n2=n2, nbc=nbc),
            grid=(n1 // kb,),
            in_specs=[blk(cols), blk(cols), ospec(order), ospec(order), const((n2, 2 * n2)), const((2 * n2, n2))],
            out_specs=[blk(cols), blk(cols)],
            out_shape=[ashape, ashape],
            compiler_params=_cp("arbitrary"),
            name="hy_stage2",
        )(a_re, a_im, fre, fim, t2_bf, t2i_bf)
        if order == 0:
            a_re, a_im = pl.pallas_call(
                functools.partial(_hy_stage31_kernel, G=G, n1=n1, ct=D_HY),
                grid=(n2 // G,),
                in_specs=[grp, grp, tspec(t_inv_swapped), pl.BlockSpec((G, h1, cols), lambda j: (j, 0, 0)),
                          tspec(t_fwd)],
                out_specs=[grp, grp],
                out_shape=[ashape, ashape],
                compiler_params=_cp("arbitrary"),
                name="hy_stage31",
            )(b_re, b_im, t_inv_swapped, g1_3, t_fwd)
        else:
            io = pl.BlockSpec((h1, G, cols), lambda j: (0, j, 0))
            out = pl.pallas_call(
                functools.partial(_hy_stage3_kernel, G=G, n1=n1, ct=D_HY),
                grid=(n2 // G,),
                in_specs=[grp, grp, tspec(t_inv_natural), io],
                out_specs=io,
                out_shape=jax.ShapeDtypeStruct((h1, n2, cols), BF16),
                compiler_params=_cp("arbitrary"),
                name="hy_stage3",
            )(b_re, b_im, t_inv_natural, g2.reshape(h1, n2, cols))
    return out.reshape(L, cols)


def _hyena(hv_t, h1_t, h2_t, conv_w, conv_b, filt, skip, tables=None):
    L, cols = hv_t.shape
    if _two_level(L):
        z3, g1_3, g2 = _hy_conv3(hv_t, h1_t, h2_t, conv_w, conv_b, swap=(True, True, False))
        return _hyena_two_level(z3, g1_3, g2, filt, skip, tables or _two_level_tables(L))
    z, g1, g2 = _hy_conv3(hv_t, h1_t, h2_t, conv_w, conv_b)
    return _hyena_direct(z, g1, g2, filt, skip)


def _route(logits):
    lane_i = lax.broadcasted_iota(jnp.int32, logits.shape, 1)
    lane = lane_i.astype(F32)
    is_g = (lane_i >= N_EXPERTS) & (lane_i < N_EXPERTS + N_GROUPS)
    big = float(LANES)
    gl = jnp.where(is_g, logits, NEG)
    gmax = jnp.max(gl, axis=-1, keepdims=True)
    gidx = jnp.min(jnp.where(gl == gmax, lane - N_EXPERTS, big), axis=-1, keepdims=True)
    p_group = 1.0 / jnp.sum(jnp.where(is_g, jnp.exp(gl - gmax), 0.0), axis=-1, keepdims=True)
    sel = (lane_i < N_EXPERTS) & ((lane_i // EXPERTS_PER_GROUP) == gidx.astype(jnp.int32))
    el = jnp.where(sel, logits, NEG)
    ee = jnp.where(sel, jnp.exp(el - jnp.max(el, axis=-1, keepdims=True)), 0.0)
    pe = ee / jnp.sum(ee, axis=-1, keepdims=True)
    p1 = jnp.max(jnp.where(sel, pe, -1.0), axis=-1, keepdims=True)
    i1 = jnp.min(jnp.where(sel & (pe == p1), lane, big), axis=-1, keepdims=True)
    rest = sel & (lane != i1)
    p2 = jnp.max(jnp.where(rest, pe, -1.0), axis=-1, keepdims=True)
    i2 = jnp.min(jnp.where(rest & (pe == p2), lane, big), axis=-1, keepdims=True)
    den = p1 + p2
    cmb = jnp.where(lane == i1, p_group * (p1 / den), jnp.where(lane == i2, p_group * (p2 / den), 0.0))
    return jnp.where(lane_i == N_EXPERTS, gidx, cmb)


def _outproj_kernel(att_ref, hf_ref, hb_ref, gt_ref, hy_ref, x_ref, mod_ref, w_ref, g_ref, b_ref,
                    wr_ref, br_ref, x1_ref, u2_ref, cmb_ref):
    rg = (hf_ref[...] + hb_ref[...]) * jax.nn.gelu(gt_ref[...].astype(F32))
    mix = (jnp.dot(att_ref[0], w_ref[0:D_ATT, :], preferred_element_type=F32)
           + jnp.dot(rg.astype(BF16), w_ref[D_ATT:D_ATT + D_RNN, :], preferred_element_type=F32)
           + jnp.dot(hy_ref[...].astype(BF16), w_ref[D_ATT + D_RNN:, :], preferred_element_type=F32))
    m = mod_ref[0]
    x1 = _ln(ALPHA * x_ref[...] + m[2:3] * mix) * g_ref[...] + b_ref[...]
    u2 = _ln(x1) * (1.0 + m[4:5]) + m[3:4]
    u_hi = u2.astype(BF16)
    u_lo = (u2 - u_hi.astype(F32)).astype(BF16)
    wr = wr_ref[...]
    w_hi = wr.astype(BF16)
    w_lo = (wr - w_hi.astype(F32)).astype(BF16)
    hi = jnp.dot(u_hi, jnp.concatenate([w_hi, w_lo], axis=1), preferred_element_type=F32)
    lo = jnp.dot(u_lo, w_hi, preferred_element_type=F32)
    logits = hi[:, 0:LANES] + hi[:, LANES:] + lo + br_ref[...]
    x1_ref[...] = x1
    u2_ref[...] = u_hi
    cmb_ref[...] = _route(logits)


def _outproj(att, hf_t, hb_t, gt_t, hy_t, x2d, mod_l, w_bf, ln_g, ln_b, wr, br, *, nb, L, tm,
             cond_base, cond_per_batch):
    tpb = L // tm
    d = x2d.shape[1]
    cond = (lambda i: cond_base + i // tpb) if cond_per_batch else (lambda i: cond_base)
    t_spec = pl.BlockSpec((tm, D_RNN), lambda i: (i % tpb, i // tpb))
    tok = lambda c: pl.BlockSpec((tm, c), lambda i: (i, 0))
    const = lambda s: pl.BlockSpec(s, lambda i: (0, 0))
    n = nb * L
    return pl.pallas_call(
        _outproj_kernel,
        grid=(nb * tpb,),
        in_specs=[pl.BlockSpec((1, tm, D_ATT), lambda i: (i // tpb, i % tpb, 0)),
                  t_spec, t_spec, t_spec, t_spec, tok(d),
                  pl.BlockSpec((1, 6, d), lambda i: (cond(i), 0, 0)),
                  const((d, d)), const((1, d)), const((1, d)), const((d, LANES)), const((1, LANES))],
        out_specs=[tok(d), tok(d), tok(LANES)],
        out_shape=[jax.ShapeDtypeStruct((n, d), F32), jax.ShapeDtypeStruct((n, d), BF16),
                   jax.ShapeDtypeStruct((n, LANES), F32)],
        compiler_params=_cp("arbitrary"),
        name="outproj",
    )(att, hf_t, hb_t, gt_t, hy_t, x2d, mod_l, w_bf, ln_g[None], ln_b[None], wr, br)


def _moe_kernel(u_ref, cmb_ref, x1_ref, mod_ref, wgu_ref, wd_ref, g_ref, b_ref, o_ref, ys_ref, *, T, BLK):
    SL = T + N_GROUPS * BLK
    u = u_ref[...]
    cmb = cmb_ref[...]
    lane_f = lax.broadcasted_iota(jnp.int32, (T, LANES), 1).astype(F32)
    gid_col = jnp.sum(jnp.where(lane_f == float(N_EXPERTS), cmb, 0.0), axis=-1, keepdims=True)
    onehot = jnp.where(lane_f == gid_col, 1.0, 0.0)
    ri = lax.broadcasted_iota(jnp.int32, (T, T), 0)
    ci = lax.broadcasted_iota(jnp.int32, (T, T), 1)
    before = jnp.where(ci < ri, 1.0, 0.0).astype(BF16)
    rank_col = jnp.sum(onehot * jnp.dot(before, onehot.astype(BF16), preferred_element_type=F32),
                       axis=-1, keepdims=True)
    counts = jnp.sum(onehot, axis=0, keepdims=True)
    lane_row = lax.broadcasted_iota(jnp.int32, (1, LANES), 1)
    base = jnp.int32(0)
    bases = []
    base_row = jnp.zeros((1, LANES), jnp.int32)
    for c in range(N_GROUPS):
        n_c = jnp.sum(jnp.where(lane_row == c, counts, 0.0)).astype(jnp.int32)
        bases.append(base)
        base_row = jnp.where(lane_row == c, base, base_row)
        base = base + ((n_c + (BLK - 1)) // BLK) * BLK
    total = base
    dest_f = jnp.sum(onehot * base_row.astype(F32), axis=-1, keepdims=True) + rank_col
    dest_col = dest_f.astype(jnp.int32)
    dest_row = jnp.broadcast_to(dest_f, (T, LANES)).T[0:1, :].astype(jnp.int32)
    cmb_hi = cmb.astype(BF16)
    cmb_lo = (cmb - cmb_hi.astype(F32)).astype(BF16)
    blk_rows = lax.broadcasted_iota(jnp.int32, (BLK, T), 0)
    blk_lane = lax.broadcasted_iota(jnp.int32, (BLK, LANES), 1)

    def block(b, carry):
        r0 = b * BLK
        rows = pl.ds(pl.multiple_of(r0, BLK), BLK)

        @pl.when(r0 < total)
        def _():
            grp = ((r0 >= bases[1]).astype(jnp.int32) + (r0 >= bases[2]).astype(jnp.int32)
                   + (r0 >= bases[3]).astype(jnp.int32))
            p = jnp.where(blk_rows + r0 == dest_row, 1.0, 0.0).astype(BF16)
            xb = jnp.dot(p, u, preferred_element_type=F32).astype(BF16)
            wb = (jnp.dot(p, cmb_hi, preferred_element_type=F32)
                  + jnp.dot(p, cmb_lo, preferred_element_type=F32))
            gu = jnp.dot(xb, wgu_ref[grp], preferred_element_type=F32)
            hs = []
            for j in range(EXPERTS_PER_GROUP):
                gate = gu[:, 2 * j * D_EXPERT:(2 * j + 1) * D_EXPERT]
                up = gu[:, (2 * j + 1) * D_EXPERT:(2 * j + 2) * D_EXPERT]
                w = jnp.sum(jnp.where(blk_lane == grp * EXPERTS_PER_GROUP + j, wb, 0.0), axis=-1, keepdims=True)
                hs.append((gate * _sigmoid(gate) * up * w).astype(BF16))
            ys_ref[rows, :] = jnp.dot(jnp.concatenate(hs, axis=1), wd_ref[grp],
                                      preferred_element_type=F32).astype(BF16)

        @pl.when(r0 >= total)
        def _():
            ys_ref[rows, :] = jnp.zeros((BLK, u.shape[1]), BF16)

        return carry

    lax.fori_loop(0, SL // BLK, block, 0)
    p_t = jnp.where(lax.broadcasted_iota(jnp.int32, (T, SL), 1) == dest_col, 1.0, 0.0).astype(BF16)
    moe = jnp.dot(p_t, ys_ref[...], preferred_element_type=F32)
    m = mod_ref[0]
    o_ref[...] = _ln(ALPHA * x1_ref[...] + m[5:6] * moe) * g_ref[...] + b_ref[...]


def _group_gate_up(w_gate, w_up):
    _, d, f = w_gate.shape
    gu = jnp.concatenate([w_gate, w_up], axis=-1).astype(BF16).reshape(N_GROUPS, EXPERTS_PER_GROUP, d, 2 * f)
    return gu.transpose(0, 2, 1, 3).reshape(N_GROUPS, d, EXPERTS_PER_GROUP * 2 * f)


def _moe(u2, cmb, x1, mod_l, wgu, wd, ln_g, ln_b, *, tm, tiles_per_cond, cond_base, cond_per_batch):
    n, d = x1.shape
    cond = (lambda i: cond_base + i // tiles_per_cond) if cond_per_batch else (lambda i: cond_base)
    tok = lambda c: pl.BlockSpec((tm, c), lambda i: (i, 0))
    const = lambda s: pl.BlockSpec(s, lambda i: (0,) * len(s))
    resident = lambda s: pl.BlockSpec(s, lambda i: (0,) * len(s), pipeline_mode=pl.Buffered(1))
    return pl.pallas_call(
        functools.partial(_moe_kernel, T=tm, BLK=MOE_BLK),
        grid=(n // tm,),
        in_specs=[tok(d), tok(LANES), tok(d),
                  pl.BlockSpec((1, 6, d), lambda i: (cond(i), 0, 0)),
                  resident(wgu.shape), resident(wd.shape),
                  const((1, d)), const((1, d))],
        out_specs=tok(d),
        out_shape=jax.ShapeDtypeStruct((n, d), F32),
        scratch_shapes=[pltpu.VMEM((tm + N_GROUPS * MOE_BLK, d), BF16)],
        compiler_params=_cp("arbitrary"),
        name="moe",
    )(u2, cmb, x1, mod_l, wgu, wd, ln_g[None], ln_b[None])


def _stream_layer(x2d, mod_l, p, *, nb, L, tm, cond_base, cond_per_batch, ctx, cache=None, h0=None, prev_kv=None):
    kw = dict(nb=nb, L=L, tm=tm, cond_base=cond_base, cond_per_batch=cond_per_batch)
    q, k, v, xr_t, gt_t, hv_t, h1_t, h2_t = _inproj(x2d, mod_l, p["w_in"], kv_dtype=F32 if ctx else BF16,
                                                    prev_kv=prev_kv, **{**kw, "tm": min(L, INPROJ_TILE)})
    if ctx:
        att = _ctx_attention(q, k, v)
    else:
        att = _nbr_attention(q, k, v, cache[0], cache[1], p["nbr_bias"])
    a_f, b_f, a_b, b_b = _rg_gates(xr_t, p["rg_conv_w"], p["rg_conv_b"], *p["rg_w"])
    hf_t, hb_t = _rg_scan(a_f, b_f, a_b, b_b, h0[0], h0[1])
    hy_t = _hyena(hv_t, h1_t, h2_t, p["hy_conv_w"], p["hy_conv_b"], p["hy_filt"][L], p["hy_skip"],
                  p["hy_tables"].get(L))
    x1, u2, cmb = _outproj(att, hf_t, hb_t, gt_t, hy_t, x2d, mod_l, p["w_out"], p["ln1_g"], p["ln1_b"],
                           p["router_w"], p["router_b"], **kw)
    tm_moe = min(MOE_TILE, nb * L)
    x2 = _moe(u2, cmb, x1, mod_l, p["moe_wgu"], p["moe_wd"], p["ln2_g"], p["ln2_b"], tm=tm_moe,
              tiles_per_cond=max(L // tm_moe, 1), cond_base=cond_base, cond_per_batch=cond_per_batch)
    return x2, k, v, hf_t, hb_t


def kernel(x_prompt, x_sample, cache_k, cache_v, state_rglru, c, c_ctx, w_mod, b_mod, w_in, attn_rpb, rg_conv_w, rg_conv_b, rg_wa, rg_ba, rg_wx, rg_bx, rg_lambda, hy_conv_w, hy_conv_b, hy_w1, hy_b1, hy_w2, hy_b2, hy_w3, hy_freq, hy_decay, hy_skip, w_out, ln1_g, ln1_b, router_wg, router_bg, router_we, router_be, moe_w_gate, moe_w_up, moe_w_down, ln2_g, ln2_b):
    nbc, Lc, d = x_prompt.shape
    nbl, Ll, _ = x_sample.shape
    depth = w_mod.shape[0]
    past = cache_k.shape[2]

    cond8 = jnp.zeros((8, d), F32).at[0].set(c_ctx).at[1:1 + nbl].set(c)
    mod = _modulation(cond8, w_mod, b_mod).reshape(depth, 8, 6, d)

    xp = x_prompt.reshape(nbc * Lc, d)
    xs = x_sample.reshape(nbl * Ll, d)
    kv, new_h = None, []
    hy_tables = {L: _two_level_tables(L) for L in {Lc, Ll} if _two_level(L)}
    for l in range(depth):
        pad = jnp.zeros((d, LANES - N_EXPERTS - N_GROUPS), F32)
        p = {
            "w_in": w_in[l].astype(BF16),
            "nbr_bias": _nbr_bias_tiles(attn_rpb[l]),
            "rg_conv_w": rg_conv_w[l], "rg_conv_b": rg_conv_b[l],
            "rg_w": _rg_weights(rg_wa[l], rg_ba[l], rg_wx[l], rg_bx[l], rg_lambda[l]),
            "hy_conv_w": hy_conv_w[l], "hy_conv_b": hy_conv_b[l], "hy_skip": hy_skip[l], "hy_tables": hy_tables,
            "hy_filt": {L: _hy_filters(L, hy_w1[l], hy_b1[l], hy_w2[l], hy_b2[l], hy_w3[l], hy_freq[l], hy_decay[l])
                        for L in sorted({Lc, Ll})},
            "w_out": w_out[l].astype(BF16), "ln1_g": ln1_g[l], "ln1_b": ln1_b[l],
            "router_w": jnp.concatenate([router_we[l], router_wg[l], pad], axis=1),
            "router_b": jnp.concatenate([router_be[l], router_bg[l], pad[0]])[None],
            "moe_wgu": _group_gate_up(moe_w_gate[l], moe_w_up[l]),
            "moe_wd": moe_w_down[l].astype(BF16).reshape(N_GROUPS, EXPERTS_PER_GROUP * D_EXPERT, d),
            "ln2_g": ln2_g[l], "ln2_b": ln2_b[l],
        }
        zeros_h = jnp.zeros((1, nbc * D_RNN), F32)
        xp, k, v, hf_t, hb_t = _stream_layer(
            xp, mod[l], p, nb=nbc, L=Lc, tm=Lc, cond_base=0, cond_per_batch=False, ctx=True,
            h0=(zeros_h, zeros_h), prev_kv=kv)
        kv = (k, v)
        new_h.append(jnp.stack([hf_t[Lc - 1].reshape(nbc, D_RNN), hb_t[0].reshape(nbc, D_RNN)], axis=1))

        cache = (cache_k[:, l].reshape(nbl, past, D_ATT), cache_v[:, l].reshape(nbl, past, D_ATT))
        h0 = (state_rglru[:, l, 0].reshape(1, nbl * D_RNN), state_rglru[:, l, 1].reshape(1, nbl * D_RNN))
        xs, _, _, _, _ = _stream_layer(
            xs, mod[l], p, nb=nbl, L=Ll, tm=min(Ll, ROW_CHUNK), cond_base=1, cond_per_batch=True, ctx=False,
            cache=cache, h0=h0)

    return (xp.reshape(nbc, Lc, d), xs.reshape(nbl, Ll, d),
            kv[0].reshape(nbc, depth, Lc, N_HEADS, HEAD_DIM), kv[1].reshape(nbc, depth, Lc, N_HEADS, HEAD_DIM),
            jnp.stack(new_h, axis=1))
```

```python
import functools
import math

import numpy as np
import jax
import jax.numpy as jnp
from jax import lax
from jax.experimental import pallas as pl
from jax.experimental.pallas import tpu as pltpu

F32 = jnp.float32
BF16 = jnp.bfloat16
HIGHEST = lax.Precision.HIGHEST

DEPTH = 2
D_MODEL = 1024
N_HEADS = 8
HEAD_DIM = 64
D_ATT = N_HEADS * HEAD_DIM
D_RNN = 256
D_HY = 256
GRID_W = 64
WIN_R = 8
WIN_C = 16
RG_CONV = 4
RG_C = 8.0
HY_CONV = 3
POS_EMB_DIM = 33
N_GROUPS = 4
EXPERTS_PER_GROUP = 4
N_EXPERTS = 16
D_EXPERT = 256
ALPHA = (2.0 * DEPTH) ** 0.25
LN_EPS = 1e-5
NEG = -1e30

Q_ROWS = 4
K_ROWS = Q_ROWS + WIN_R
LANES = 128
HALO = 8
ROW_CHUNK = 512
SCAN_LANES = 1024
BLOCK_BYTES = 1 << 20
MOD_COL_TILES = 4
FFT_MIN_LEN = 2048
FFT_KB = 4
INPROJ_TILE = 1024
MOE_TILE = 512
MOE_BLK = 128
STRIDE_PAD = 4
FFT_G = 16
FFT_N1 = 64
VMEM_LIMIT = 52 * 1024 * 1024


def _cp(*sem):
    return pltpu.CompilerParams(dimension_semantics=sem, vmem_limit_bytes=VMEM_LIMIT)


def _sigmoid(x):
    return 0.5 * jnp.tanh(0.5 * x) + 0.5


def _ln(x):
    mu = jnp.mean(x, axis=-1, keepdims=True)
    xc = x - mu
    var = jnp.mean(xc * xc, axis=-1, keepdims=True)
    return xc * lax.rsqrt(var + LN_EPS)


def _mod_kernel(c_ref, w_ref, b_ref, o_ref):
    c = c_ref[...]
    s = c * jax.nn.sigmoid(c)
    o_ref[0] = jnp.dot(s, w_ref[0], precision=HIGHEST, preferred_element_type=F32) + b_ref[0]


def _modulation(cond8, w_mod, b_mod):
    depth, d, n = w_mod.shape
    tn = n // MOD_COL_TILES
    return pl.pallas_call(
        _mod_kernel,
        grid=(depth, n // tn),
        in_specs=[pl.BlockSpec((8, d), lambda l, j: (0, 0)),
                  pl.BlockSpec((1, d, tn), lambda l, j: (l, 0, j)),
                  pl.BlockSpec((1, 1, tn), lambda l, j: (l, 0, j))],
        out_specs=pl.BlockSpec((1, 8, tn), lambda l, j: (l, 0, j)),
        out_shape=jax.ShapeDtypeStruct((depth, 8, n), F32),
        compiler_params=_cp("arbitrary", "arbitrary"),
        name="modulation",
    )(cond8, w_mod, b_mod.reshape(depth, 1, n))


def _inproj_kernel(x_ref, mod_ref, w_ref, *refs, n_prev):
    if n_prev:
        pk_ref, pv_ref = refs[:2]
        refs = refs[2:]
    q_ref, k_ref, v_ref, xr_ref, gt_ref, hv_ref, h1_ref, h2_ref = refs
    for i in range(n_prev):
        k_ref[:, i] = pk_ref[:, i]
        v_ref[:, i] = pv_ref[:, i]
    m = mod_ref[0]
    u = (_ln(x_ref[...]) * (1.0 + m[1:2]) + m[0:1]).astype(BF16)
    bpt, rows = q_ref.shape[0], q_ref.shape[1]

    def proj(lo, hi):
        return jnp.dot(u, w_ref[:, lo:hi], preferred_element_type=F32)

    o = 0
    q_ref[...] = (proj(o, o + D_ATT) * (HEAD_DIM ** -0.5)).astype(q_ref.dtype).reshape(q_ref.shape)
    o += D_ATT
    k_ref[:, n_prev] = proj(o, o + D_ATT).astype(k_ref.dtype).reshape(bpt, rows, D_ATT)
    o += D_ATT
    v_ref[:, n_prev] = proj(o, o + D_ATT).astype(v_ref.dtype).reshape(bpt, rows, D_ATT)
    o += D_ATT
    for ref in (xr_ref, gt_ref, hv_ref, h1_ref, h2_ref):
        y = proj(o, o + D_RNN).astype(ref.dtype)
        for b in range(bpt):
            ref[:, b * D_RNN:(b + 1) * D_RNN] = y[b * rows:(b + 1) * rows]
        o += D_RNN


def _tile_split(nb, L, tm, cond_per_batch):
    if tm <= L:
        return L // tm, 1
    bpt = 1 if cond_per_batch else min(nb, tm // L)
    while nb % bpt:
        bpt -= 1
    return 1, bpt


def _inproj(x2d, mod_l, w_bf, *, nb, L, tm, cond_base, cond_per_batch, kv_dtype, prev_kv=None):
    tpb, bpt = _tile_split(nb, L, tm, cond_per_batch)
    tm = L * bpt // tpb
    d = x2d.shape[1]
    d_in = w_bf.shape[1]
    n_prev = prev_kv[0].shape[1] if prev_kv else 0
    cond = (lambda i: cond_base + i // tpb) if cond_per_batch else (lambda i: cond_base)
    att_spec = pl.BlockSpec((bpt, L // tpb, D_ATT), lambda i: (i // tpb, i % tpb, 0))
    kv_spec = lambda n: pl.BlockSpec((bpt, n, L // tpb, D_ATT), lambda i: (i // tpb, 0, i % tpb, 0))
    t_spec = pl.BlockSpec((L // tpb, bpt * D_RNN), lambda i: (i % tpb, i // tpb))
    kv_shape = jax.ShapeDtypeStruct((nb, n_prev + 1, L, D_ATT), kv_dtype)
    t_shape = jax.ShapeDtypeStruct((L, nb * D_RNN), BF16)
    return pl.pallas_call(
        functools.partial(_inproj_kernel, n_prev=n_prev),
        grid=(nb * tpb // bpt,),
        in_specs=[pl.BlockSpec((tm, d), lambda i: (i, 0)),
                  pl.BlockSpec((1, 6, d), lambda i: (cond(i), 0, 0)),
                  pl.BlockSpec((d, d_in), lambda i: (0, 0))] + ([kv_spec(n_prev)] * 2 if n_prev else []),
        out_specs=[att_spec, kv_spec(n_prev + 1), kv_spec(n_prev + 1), t_spec, t_spec, t_spec, t_spec, t_spec],
        out_shape=[jax.ShapeDtypeStruct((nb, L, D_ATT), BF16), kv_shape, kv_shape,
                   t_shape, t_shape, t_shape, t_shape, t_shape],
        compiler_params=_cp("arbitrary"),
        name="inproj",
    )(x2d, mod_l, w_bf, *(prev_kv or ()))


def _nt(a, b):
    return lax.dot_general(a, b, (((1,), (1,)), ((), ())), preferred_element_type=F32)


def _ctx_attn_kernel(q_ref, k_ref, v_ref, o_ref):
    q = q_ref[0]
    k = k_ref[0, 0].astype(BF16)
    v = v_ref[0, 0].astype(BF16)
    outs = []
    for h in range(N_HEADS):
        sl = slice(h * HEAD_DIM, (h + 1) * HEAD_DIM)
        s = _nt(q[:, sl], k[:, sl])
        e = jnp.exp(s - jnp.max(s, axis=-1, keepdims=True))
        den = jnp.sum(e, axis=-1, keepdims=True)
        outs.append(jnp.dot(e.astype(BF16), v[:, sl], preferred_element_type=F32) / den)
    o_ref[0] = jnp.concatenate(outs, axis=-1).astype(o_ref.dtype)


def _ctx_attention(q, k, v):
    nb, L, _ = q.shape
    last = k.shape[1] - 1
    spec = pl.BlockSpec((1, L, D_ATT), lambda b: (b, 0, 0))
    kv_spec = pl.BlockSpec((1, 1, L, D_ATT), lambda b: (b, last, 0, 0))
    return pl.pallas_call(
        _ctx_attn_kernel,
        grid=(nb,),
        in_specs=[spec, kv_spec, kv_spec],
        out_specs=spec,
        out_shape=jax.ShapeDtypeStruct((nb, L, D_ATT), BF16),
        compiler_params=_cp("arbitrary"),
        name="ctx_attn",
    )(q, k, v)


N_DR = 2 * WIN_R - 1
PAIR_LEFT = N_DR - 1
PAIR_RIGHT = 2 * N_DR - 1
PAIR_NONE = 3 * N_DR - 1


def _nbr_bias_tiles(rpb):
    cols = np.arange(GRID_W)
    c_start = np.clip(cols - WIN_C // 2, 0, GRID_W - WIN_C)
    col_mask = (cols[None, :] >= c_start[:, None]) & (cols[None, :] < c_start[:, None] + WIN_C)
    dc_idx = np.clip(cols[None, :] - cols[:, None], -(WIN_C - 1), WIN_C - 1) + (WIN_C - 1)
    sel = (dc_idx[None] == np.arange(2 * WIN_C - 1)[:, None, None]).astype(np.float32)
    colb = jnp.einsum('hab,bqk->haqk', rpb.astype(F32), jnp.asarray(sel), precision=HIGHEST)
    colb = jnp.where(col_mask[None, None], colb, NEG)
    neg = jnp.full_like(colb, NEG)
    both = jnp.concatenate([colb[:, :-1], colb[:, 1:]], axis=-1)
    left_masked = jnp.concatenate([neg, colb], axis=-1)
    right_masked = jnp.concatenate([colb, neg], axis=-1)
    none = jnp.concatenate([neg[:, :1], neg[:, :1]], axis=-1)
    return jnp.concatenate([both, left_masked, right_masked, none], axis=1)


def _nbr_attn_kernel(q_ref, k_ref, v_ref, kc_ref, vc_ref, bias_ref, o_ref, *, rows):
    g = pl.program_id(1)
    r0 = jnp.clip(Q_ROWS * g - WIN_R // 2, 0, rows - K_ROWS)
    start = pl.multiple_of(r0 * GRID_W, Q_ROWS * GRID_W)
    nk = K_ROWS * GRID_W
    kw = k_ref[0, 0, pl.ds(start, nk), :]
    vw = v_ref[0, 0, pl.ds(start, nk), :]
    kc = kc_ref[0].astype(BF16)
    vc = vc_ref[0].astype(BF16)
    q = q_ref[0]
    tile = []
    for j in range(Q_ROWS):
        r = Q_ROWS * g + j
        lo = jnp.clip(r - WIN_R // 2, 0, rows - WIN_R) - r + (WIN_R - 1)
        tile.append([])
        for p in range(K_ROWS // 2):
            d0 = r0 + 2 * p - r + (WIN_R - 1)
            lv = (d0 >= lo) & (d0 < lo + WIN_R)
            rv = (d0 + 1 >= lo) & (d0 + 1 < lo + WIN_R)
            idx = jnp.where(lv & rv, d0, jnp.where(rv, PAIR_LEFT + d0 + 1, jnp.where(lv, PAIR_RIGHT + d0, PAIR_NONE)))
            tile[j].append(jnp.clip(idx, 0, PAIR_NONE))
    outs = []
    for h in range(N_HEADS):
        sl = slice(h * HEAD_DIM, (h + 1) * HEAD_DIM)
        qh = q[:, sl]
        bias = jnp.concatenate(
            [jnp.concatenate([bias_ref[h, tile[j][p]] for p in range(K_ROWS // 2)], axis=-1)
             for j in range(Q_ROWS)], axis=0)
        s1 = _nt(qh, kw[:, sl]) + bias
        s2 = _nt(qh, kc[:, sl])
        m = jnp.maximum(jnp.max(s1, axis=-1, keepdims=True), jnp.max(s2, axis=-1, keepdims=True))
        e1 = jnp.exp(s1 - m)
        e2 = jnp.exp(s2 - m)
        den = jnp.sum(e1, axis=-1, keepdims=True) + jnp.sum(e2, axis=-1, keepdims=True)
        o = (jnp.dot(e1.astype(BF16), vw[:, sl], preferred_element_type=F32)
             + jnp.dot(e2.astype(BF16), vc[:, sl], preferred_element_type=F32))
        outs.append(o / den)
    o_ref[0] = jnp.concatenate(outs, axis=-1).astype(o_ref.dtype)


def _nbr_attention(q, k, v, kc, vc, bias):
    nb, L, _ = q.shape
    rows = L // GRID_W
    ng = rows // Q_ROWS
    past = kc.shape[1]
    tq = Q_ROWS * GRID_W
    full = pl.BlockSpec((1, 1, L, D_ATT), lambda b, g: (b, 0, 0, 0))
    ctx = pl.BlockSpec((1, past, D_ATT), lambda b, g: (b, 0, 0))
    qspec = pl.BlockSpec((1, tq, D_ATT), lambda b, g: (b, g, 0))
    bspec = pl.BlockSpec(bias.shape, lambda b, g: (0, 0, 0, 0), pipeline_mode=pl.Buffered(1))
    return pl.pallas_call(
        functools.partial(_nbr_attn_kernel, rows=rows),
        grid=(nb, ng),
        in_specs=[qspec, full, full, ctx, ctx, bspec],
        out_specs=qspec,
        out_shape=jax.ShapeDtypeStruct((nb, L, D_ATT), BF16),
        compiler_params=_cp("arbitrary", "arbitrary"),
        name="nbr_attn",
    )(q, k, v, kc, vc, bias)


def _fill_padded(xp_ref, src_ref, L):
    z = jnp.zeros((HALO, xp_ref.shape[1]), F32)
    xp_ref[0:HALO, :] = z
    xp_ref[HALO:HALO + L, :] = src_ref[...].astype(F32)
    xp_ref[HALO + L:2 * HALO + L, :] = z


def _rg_gates_kernel(x_ref, cw_ref, cb_ref, w_ref, b_ref, lam_ref,
                     af_ref, bf_ref, ab_ref, bb_ref, xp_ref, *, L, tc, m):
    nh = cw_ref.shape[0]
    for sc in range(m):
        half = (pl.program_id(0) * m + sc) % nh
        cols = slice(sc * LANES, (sc + 1) * LANES)
        _fill_padded(xp_ref, x_ref.at[:, cols], L)
        sp = jax.nn.softplus(-lam_ref[half])
        cw = cw_ref[half]
        for t0 in range(0, L, tc):
            xc = cb_ref[half]
            for k in range(RG_CONV):
                xc = xc + cw[k:k + 1, :] * xp_ref[pl.ds(HALO + t0 + k - RG_CONV // 2, tc), :]
            z = jnp.dot(xc.astype(BF16), w_ref[half], preferred_element_type=F32) + b_ref[half]
            for e, (a_ref, o_ref) in enumerate(((af_ref, bf_ref), (ab_ref, bb_ref))):
                r = _sigmoid(z[:, e * LANES:(e + 1) * LANES])
                i = _sigmoid(z[:, (2 + e) * LANES:(3 + e) * LANES])
                log_a = (-RG_C) * r * sp[:, e * LANES:(e + 1) * LANES]
                a = jnp.exp(log_a)
                one_minus_a2 = -jnp.tanh(log_a) * (a * a + 1.0)
                a_ref[pl.ds(t0, tc), cols] = a
                o_ref[pl.ds(t0, tc), cols] = jnp.sqrt(one_minus_a2) * i * xc


def _blockdiag2(w2):
    z = jnp.zeros_like(w2[0])
    return jnp.concatenate([jnp.concatenate([w2[0], z], 1), jnp.concatenate([z, w2[1]], 1)], 0)


def _rg_weights(wa, ba, wx, bx, lam):
    ws, bs, ls = [], [], []
    for hf in range(D_RNN // LANES):
        blk = slice(2 * hf, 2 * hf + 2)
        ch = slice(hf * LANES, (hf + 1) * LANES)
        ws.append(jnp.concatenate([_blockdiag2(wa[0, blk]), _blockdiag2(wa[1, blk]),
                                   _blockdiag2(wx[0, blk]), _blockdiag2(wx[1, blk])], axis=1))
        bs.append(jnp.concatenate([ba[0, ch], ba[1, ch], bx[0, ch], bx[1, ch]])[None])
        ls.append(jnp.concatenate([lam[0, ch], lam[1, ch]])[None])
    return jnp.stack(ws).astype(BF16), jnp.stack(bs), jnp.stack(ls)


def _lane_chunks(L, cols):
    m = max(1, min(8, BLOCK_BYTES // (L * LANES * 4)))
    while cols % (m * LANES):
        m //= 2
    return m


def _rg_gates(xr_t, conv_w, conv_b, w_bd, b_cat, lam_cat):
    L, cols = xr_t.shape
    nh = D_RNN // LANES
    tc = min(L, ROW_CHUNK)
    m = _lane_chunks(L, cols)
    col = pl.BlockSpec((L, m * LANES), lambda c: (0, c))
    const = lambda a: pl.BlockSpec(a.shape, lambda c: (0,) * a.ndim)
    shape = jax.ShapeDtypeStruct((L, cols), F32)
    cw = conv_w.reshape(RG_CONV, nh, LANES).transpose(1, 0, 2)
    cb = conv_b.reshape(nh, 1, LANES)
    return pl.pallas_call(
        functools.partial(_rg_gates_kernel, L=L, tc=tc, m=m),
        grid=(cols // (m * LANES),),
        in_specs=[col, const(cw), const(cb), const(w_bd), const(b_cat), const(lam_cat)],
        out_specs=[col, col, col, col],
        out_shape=[shape, shape, shape, shape],
        scratch_shapes=[pltpu.VMEM((L + 2 * HALO, LANES), F32)],
        compiler_params=_cp("arbitrary"),
        name="rg_gates",
    )(xr_t, cw, cb, w_bd, b_cat, lam_cat)


def _rg_scan_kernel(af_ref, bf_ref, ab_ref, bb_ref, h0f_ref, h0b_ref, hf_ref, hb_ref, cf_ref, cb_ref, *, tc):
    @pl.when(pl.program_id(1) == 0)
    def _():
        cf_ref[...] = h0f_ref[...]
        cb_ref[...] = h0b_ref[...]

    def body(t, carry):
        hf, hb = carry
        hf = af_ref[pl.ds(t, 1), :] * hf + bf_ref[pl.ds(t, 1), :]
        hf_ref[pl.ds(t, 1), :] = hf
        tb = tc - 1 - t
        hb = ab_ref[pl.ds(tb, 1), :] * hb + bb_ref[pl.ds(tb, 1), :]
        hb_ref[pl.ds(tb, 1), :] = hb
        return hf, hb

    hf, hb = lax.fori_loop(0, tc, body, (cf_ref[...], cb_ref[...]), unroll=8)
    cf_ref[...] = hf
    cb_ref[...] = hb


def _rg_scan(a_f, b_f, a_b, b_b, h0f, h0b):
    L, cols = a_f.shape
    cw = min(cols, SCAN_LANES)
    tc = min(L, ROW_CHUNK)
    nt = L // tc
    fwd = pl.BlockSpec((tc, cw), lambda c, i: (i, c))
    bwd = pl.BlockSpec((tc, cw), lambda c, i: (nt - 1 - i, c))
    row = pl.BlockSpec((1, cw), lambda c, i: (0, c))
    shape = jax.ShapeDtypeStruct((L, cols), F32)
    return pl.pallas_call(
        functools.partial(_rg_scan_kernel, tc=tc),
        grid=(cols // cw, nt),
        in_specs=[fwd, fwd, bwd, bwd, row, row],
        out_specs=[fwd, bwd],
        out_shape=[shape, shape],
        scratch_shapes=[pltpu.VMEM((1, cw), F32), pltpu.VMEM((1, cw), F32)],
        compiler_params=_cp("arbitrary", "arbitrary"),
        name="rg_scan",
    )(a_f, b_f, a_b, b_b, h0f, h0b)


def _hy_filter_kernel(feat_ref, w1_ref, b1_ref, w2_ref, b2_ref, w3_ref, fr_ref, dec_ref, o_ref, *, tm):
    feats = feat_ref[...]
    dot = functools.partial(jnp.dot, precision=HIGHEST, preferred_element_type=F32)
    h = jnp.sin(fr_ref[0:1, :] * (dot(feats, w1_ref[...]) + b1_ref[...]))
    h = jnp.sin(fr_ref[1:2, :] * (dot(h, w2_ref[...]) + b2_ref[...]))
    h = dot(h, w3_ref[...])
    filt = h * jnp.exp(-feats[:, 0:1] * jnp.abs(dec_ref[...]))
    row = pl.program_id(0) * tm + lax.broadcasted_iota(jnp.int32, filt.shape, 0)
    lane = lax.broadcasted_iota(jnp.int32, filt.shape, 1)
    o_ref[...] = jnp.where((row == 0) & (lane >= 2 * D_HY), 0.0, filt)


def _hy_features(L):
    pos = np.arange(L, dtype=np.float64)
    t = np.linspace(0.0, 1.0, L)
    bands = (POS_EMB_DIM - 1) // 2
    f = np.linspace(1e-4, bands - 1, bands)
    ang = (2.0 * math.pi / L) * pos[:, None] * f[None, :]
    feats = np.concatenate([t[:, None], np.cos(ang), -np.sin(ang)], axis=-1)
    out = np.zeros((L, LANES), np.float32)
    out[:, :POS_EMB_DIM] = feats
    return out


def _two_level(L):
    return L % (FFT_N1 * 8) == 0 and L >= FFT_MIN_LEN


def _digit_swap(x, L):
    n2 = 2 * L // FFT_N1
    return x.reshape(L // n2, n2, -1).transpose(1, 0, 2).reshape(L, -1)


def _hy_filters(L, w1, b1, w2, b2, w3, freq, decay):
    tm = min(L, ROW_CHUNK)
    hid = w2.shape[0]
    n = w3.shape[1]
    w1p = jnp.zeros((LANES, hid), F32).at[:POS_EMB_DIM].set(w1)
    feats = _hy_features(L)
    if _two_level(L):
        feats = _digit_swap(feats, L)
    c = lambda s: pl.BlockSpec(s, lambda i: (0, 0))
    return pl.pallas_call(
        functools.partial(_hy_filter_kernel, tm=tm),
        grid=(L // tm,),
        in_specs=[pl.BlockSpec((tm, LANES), lambda i: (i, 0)),
                  c((LANES, hid)), c((1, hid)), c((hid, hid)), c((1, hid)), c((hid, n)), c((2, hid)), c((1, n))],
        out_specs=pl.BlockSpec((tm, n), lambda i: (i, 0)),
        out_shape=jax.ShapeDtypeStruct((L, n), F32),
        compiler_params=_cp("arbitrary"),
        name="hy_filter",
    )(jnp.asarray(feats), w1p, b1[None], w2, b2[None], w3, freq, decay.reshape(1, n))


def _conv3_kernel(v_ref, a_ref, b_ref, w_ref, cb_ref, ov_ref, oa_ref, ob_ref, xp_ref, xs_ref,
                  *, L, tc, swap, n2, m):
    h1 = L // n2
    pitch = n2 + STRIDE_PAD
    nh = w_ref.shape[0]
    for sc in range(m):
        half = (pl.program_id(0) * m + sc) % nh
        cols = slice(sc * LANES, (sc + 1) * LANES)
        for p, (src, dst) in enumerate(((v_ref, ov_ref), (a_ref, oa_ref), (b_ref, ob_ref))):
            _fill_padded(xp_ref, src.at[:, cols], L)
            w = w_ref[half, p]
            cb = cb_ref[half, p]
            if swap[p]:
                for i1 in range(h1):
                    xs_ref[pl.ds(i1 * pitch, n2 + HY_CONV - 1), :] = (
                        xp_ref[pl.ds(HALO + i1 * n2 - HY_CONV // 2, n2 + HY_CONV - 1), :])

                def body(i2, carry, dst=dst, w=w, cb=cb):
                    y = cb
                    for k in range(HY_CONV):
                        y = y + w[k:k + 1, :] * xs_ref[pl.ds(i2 + k, h1, stride=pitch), :]
                    dst[i2] = y.astype(dst.dtype)
                    return carry

                lax.fori_loop(0, n2, body, 0)
            else:
                for t0 in range(0, L, tc):
                    y = cb
                    for k in range(HY_CONV):
                        y = y + w[k:k + 1, :] * xp_ref[pl.ds(HALO + t0 + k - HY_CONV // 2, tc), :]
                    dst[pl.ds(t0, tc), cols] = y


def _hy_conv3(v_t, a_t, b_t, conv_w, conv_b, swap=(False, False, False)):
    L, cols = v_t.shape
    nh = D_HY // LANES
    tc = min(L, ROW_CHUNK)
    n2 = 2 * L // FFT_N1
    m = 1 if any(swap) else _lane_chunks(L, cols)
    col = pl.BlockSpec((L, m * LANES), lambda c: (0, c))
    col3 = pl.BlockSpec((n2, L // n2, LANES), lambda c: (0, 0, c))
    shape = jax.ShapeDtypeStruct((L, cols), F32)
    shape3 = jax.ShapeDtypeStruct((n2, L // n2, cols), F32)
    w = conv_w.reshape(HY_CONV, 3, nh, LANES).transpose(2, 1, 0, 3)
    cb = conv_b.reshape(3, nh, 1, LANES).transpose(1, 0, 2, 3)
    const = lambda a: pl.BlockSpec(a.shape, lambda c: (0,) * a.ndim)
    return pl.pallas_call(
        functools.partial(_conv3_kernel, L=L, tc=tc, swap=swap, n2=n2, m=m),
        grid=(cols // (m * LANES),),
        in_specs=[col, col, col, const(w), const(cb)],
        out_specs=[col3 if s else col for s in swap],
        out_shape=[shape, shape, shape] if not any(swap) else
                  [jax.ShapeDtypeStruct(shape3.shape, BF16 if i == 0 else F32) if s else shape
                   for i, s in enumerate(swap)],
        scratch_shapes=[pltpu.VMEM((L + 2 * HALO, LANES), F32),
                        pltpu.VMEM(((L // n2) * (n2 + STRIDE_PAD) + HALO, LANES), F32)],
        compiler_params=_cp("arbitrary"),
        name="hy_conv3",
    )(v_t, a_t, b_t, w, cb)


def _phase(p, n2):
    ang = 2.0 * np.pi * (np.asarray(p, np.int64) % n2) / n2
    return np.cos(ang), np.sin(ang)


def _dft_direct_tables(L):
    k = np.arange(L)[:, None]
    t = np.arange(L)[None, :]
    c, s = _phase((2 * k + 1) * t, 4 * L)
    fwd = np.concatenate([c, -s], axis=0)
    inv = (1.0 / L) * np.concatenate([c.T, -s.T], axis=1)
    return fwd.astype(np.float32), inv.astype(np.float32)


def _dft_two_level_tables(L, n1):
    N = 2 * L
    n2 = N // n1
    h1 = n1 // 2
    i2 = np.arange(n2)[:, None, None]
    k1 = np.arange(n1)[None, :, None]
    i1 = np.arange(h1)[None, None, :]
    c, s = _phase(2 * n2 * i1 * k1 + 2 * i2 * k1 + n2 * i1 + i2, 2 * N)
    t1 = np.concatenate([c, -s], axis=1)
    ct, st = c.transpose(0, 2, 1), s.transpose(0, 2, 1)
    t1i = (2.0 / N) * np.concatenate([ct, -st], axis=2)
    kk = np.arange(n2 // 2)[:, None]
    nn = np.arange(n2)[None, :]
    c2, s2 = _phase(nn * kk, n2)
    t2 = np.block([[c2, s2], [-s2, c2]])
    t2i = np.block([[c2.T, -s2.T], [s2.T, c2.T]])
    return tuple(x.astype(np.float32) for x in (t1, t2, t2i, t1i))


def _filter_spectrum(fre, fim, order, skip):
    fw = slice(order * D_HY, (order + 1) * D_HY)
    bw = slice((2 + order) * D_HY, (3 + order) * D_HY)
    return fre[:, fw] + fre[:, bw] + skip, fim[:, fw] - fim[:, bw]


def _spectral_product(xr, xi, fr, fi, nbc):
    yr, yi = [], []
    for b in range(nbc):
        sl = slice(b * D_HY, (b + 1) * D_HY)
        yr.append(xr[:, sl] * fr - xi[:, sl] * fi)
        yi.append(xr[:, sl] * fi + xi[:, sl] * fr)
    return jnp.concatenate([jnp.concatenate(yr, axis=1), jnp.concatenate(yi, axis=1)], axis=0)


def _matmul_kernel(a_ref, b_ref, o_ref):
    o_ref[...] = jnp.dot(a_ref[...], b_ref[...], precision=HIGHEST, preferred_element_type=F32)


def _hy_direct_kernel(z_ref, g_ref, f_ref, skip_ref, t_ref, ti_ref, o_ref, *, L, nbc, order):
    x = jnp.dot(t_ref[...], z_ref[...].astype(BF16), preferred_element_type=F32)
    fr, fi = _filter_spectrum(f_ref[0:L, :], f_ref[L:2 * L, :], order, skip_ref[0])
    y = _spectral_product(x[:L], x[L:], fr, fi, nbc).astype(BF16)
    conv = jnp.dot(ti_ref[...], y, preferred_element_type=F32)
    o_ref[...] = (g_ref[...] * conv).astype(o_ref.dtype)


def _hyena_direct(z, g1, g2, filt, skip):
    L, cols = z.shape
    fwd, inv = _dft_direct_tables(L)
    n = filt.shape[1]
    spec = pl.pallas_call(
        _matmul_kernel,
        out_shape=jax.ShapeDtypeStruct((2 * L, n), F32),
        name="hy_fspec",
    )(jnp.asarray(fwd), filt)
    tn = min(cols, 4 * D_HY)
    nbc = tn // D_HY
    col = pl.BlockSpec((L, tn), lambda c: (0, c))
    const = lambda s: pl.BlockSpec(s, lambda c: (0, 0))
    t_bf, ti_bf = jnp.asarray(fwd).astype(BF16), jnp.asarray(inv).astype(BF16)
    for order, g in enumerate((g1, g2)):
        z = pl.pallas_call(
            functools.partial(_hy_direct_kernel, L=L, nbc=nbc, order=order),
            grid=(cols // tn,),
            in_specs=[col, col, const((2 * L, n)), pl.BlockSpec((1, 1, D_HY), lambda c: (order, 0, 0)),
                      const((2 * L, L)), const((L, 2 * L))],
            out_specs=col,
            out_shape=jax.ShapeDtypeStruct((L, cols), F32 if order == 0 else BF16),
            compiler_params=_cp("arbitrary"),
            name="hy_direct",
        )(z, g, spec, skip[:, None, :], t_bf, ti_bf)
    return z


def _kron_tables(t1, t1i, G):
    n2, two_n1, h1 = t1.shape
    ng = n2 // G
    eye = np.eye(two_n1, dtype=np.float32)
    rep_rows = jnp.asarray(np.repeat(eye, G, axis=0))
    rep_cols = jnp.asarray(np.repeat(eye, G, axis=1))
    c_fwd = t1.reshape(ng, G, two_n1, h1).transpose(0, 2, 1, 3).reshape(ng, two_n1, G * h1)
    c_swapped = t1i.reshape(ng, G * h1, two_n1)
    c_natural = t1i.reshape(ng, G, h1, two_n1).transpose(0, 2, 1, 3).reshape(ng, h1 * G, two_n1)

    def expand(compact, rep, left, row_key, lane_key):
        r, c = (rep.shape[0], compact.shape[2]) if left else (compact.shape[1], rep.shape[1])
        return pl.pallas_call(
            functools.partial(_kron_expand_kernel, left=left, row_key=row_key, lane_key=lane_key),
            grid=(ng,),
            in_specs=[pl.BlockSpec((1,) + compact.shape[1:], lambda i: (i, 0, 0)),
                      pl.BlockSpec(rep.shape, lambda i: (0, 0))],
            out_specs=pl.BlockSpec((1, r, c), lambda i: (i, 0, 0)),
            out_shape=jax.ShapeDtypeStruct((ng, r, c), BF16),
            compiler_params=_cp("arbitrary"),
            name="hy_kron",
        )(jnp.asarray(compact), rep)

    fwd = expand(c_fwd, rep_rows, True, lambda r: r % G, lambda c: c // h1)
    inv_swapped = expand(c_swapped, rep_cols, False, lambda r: r // h1, lambda c: c % G)
    inv_natural = expand(c_natural, rep_cols, False, lambda r: r % G, lambda c: c % G)
    return fwd, inv_swapped, inv_natural


def _kron_expand_kernel(c_ref, rep_ref, o_ref, *, left, row_key, lane_key):
    comp = c_ref[0].astype(BF16)
    rep = rep_ref[...].astype(BF16)
    full = (jnp.dot(rep, comp, preferred_element_type=F32) if left
            else jnp.dot(comp, rep, preferred_element_type=F32))
    rows = lax.broadcasted_iota(jnp.int32, full.shape, 0)
    lanes = lax.broadcasted_iota(jnp.int32, full.shape, 1)
    o_ref[0] = jnp.where(row_key(rows) == lane_key(lanes), full, 0.0).astype(BF16)


def _hy_stage1_kernel(x_ref, t_ref, re_ref, im_ref, *, G, n1, ct):
    h1, cols = x_ref.shape[1], x_ref.shape[2]
    x = x_ref[...].reshape(G * h1, cols).astype(BF16)
    for c0 in range(0, cols, ct):
        a = jnp.dot(t_ref[0], x[:, c0:c0 + ct], preferred_element_type=F32)
        re_ref[:, :, c0:c0 + ct] = a[:n1 * G].reshape(n1, G, ct).astype(BF16)
        im_ref[:, :, c0:c0 + ct] = a[n1 * G:].reshape(n1, G, ct).astype(BF16)


def _hy_stage1(x3, t_fwd, *, G):
    n2, h1, cols = x3.shape
    n1 = t_fwd.shape[1] // (2 * G)
    ospec = pl.BlockSpec((n1, G, cols), lambda j: (0, j, 0))
    oshape = jax.ShapeDtypeStruct((n1, n2, cols), BF16)
    return pl.pallas_call(
        functools.partial(_hy_stage1_kernel, G=G, n1=n1, ct=D_HY),
        grid=(n2 // G,),
        in_specs=[pl.BlockSpec((G, h1, cols), lambda j: (j, 0, 0)),
                  pl.BlockSpec((1,) + t_fwd.shape[1:], lambda j: (j, 0, 0))],
        out_specs=[ospec, ospec],
        out_shape=[oshape, oshape],
        compiler_params=_cp("arbitrary"),
        name="hy_stage1",
    )(x3, t_fwd)


def _hy_fspec2_kernel(re_ref, im_ref, skip_ref, t2_ref, fre_ref, fim_ref, *, kb, k2):
    for j in range(kb):
        a2 = jnp.concatenate([re_ref[j], im_ref[j]], axis=0)
        x = jnp.dot(t2_ref[...], a2, preferred_element_type=F32)
        for order in range(2):
            fr, fi = _filter_spectrum(x[:k2], x[k2:], order, skip_ref[order])
            fre_ref[order, j] = fr
            fim_ref[order, j] = fi


def _hy_stage2_kernel(re_ref, im_ref, fre_ref, fim_ref, t2_ref, t2i_ref, ore_ref, oim_ref, *, kb, k2, n2, nbc):
    for j in range(kb):
        a2 = jnp.concatenate([re_ref[j], im_ref[j]], axis=0)
        x = jnp.dot(t2_ref[...], a2, preferred_element_type=F32)
        fr, fi = fre_ref[0, j], fim_ref[0, j]
        y = _spectral_product(x[:k2], x[k2:], fr, fi, nbc).astype(BF16)
        a = jnp.dot(t2i_ref[...], y, preferred_element_type=F32)
        ore_ref[j] = a[:n2].astype(BF16)
        oim_ref[j] = a[n2:].astype(BF16)


def _hy_stage3_kernel(re_ref, im_ref, t_ref, g_ref, o_ref, *, G, n1, ct):
    cols = re_ref.shape[2]
    for c0 in range(0, cols, ct):
        sl = slice(c0, c0 + ct)
        a2 = jnp.concatenate([re_ref[:, :, sl].reshape(n1 * G, ct), im_ref[:, :, sl].reshape(n1 * G, ct)], axis=0)
        conv = jnp.dot(t_ref[0], a2, preferred_element_type=F32)
        g = g_ref[:, :, sl]
        o_ref[:, :, sl] = (g * conv.reshape(g.shape)).astype(o_ref.dtype)


def _hy_stage31_kernel(re_ref, im_ref, ti_ref, g_ref, tf_ref, ore_ref, oim_ref, *, G, n1, ct):
    cols = re_ref.shape[2]
    for c0 in range(0, cols, ct):
        sl = slice(c0, c0 + ct)
        a2 = jnp.concatenate([re_ref[:, :, sl].reshape(n1 * G, ct), im_ref[:, :, sl].reshape(n1 * G, ct)], axis=0)
        conv = jnp.dot(ti_ref[0], a2, preferred_element_type=F32)
        z = g_ref[:, :, sl].reshape(conv.shape) * conv
        a = jnp.dot(tf_ref[0], z.astype(BF16), preferred_element_type=F32)
        ore_ref[:, :, sl] = a[:n1 * G].reshape(n1, G, ct).astype(BF16)
        oim_ref[:, :, sl] = a[n1 * G:].reshape(n1, G, ct).astype(BF16)


def _two_level_tables(L, *, n1=FFT_N1, G=FFT_G):
    t1, t2, t2i, t1i = _dft_two_level_tables(L, n1)
    return (jnp.asarray(t2).astype(BF16), jnp.asarray(t2i).astype(BF16)) + _kron_tables(t1, t1i, G)


def _hyena_two_level(z3, g1_3, g2, filt, skip, tables, *, n1=FFT_N1, G=FFT_G, kb=FFT_KB):
    n2, h1, cols = z3.shape
    L = n2 * h1
    k2 = n2 // 2
    nbc = cols // D_HY
    t2_bf, t2i_bf, t_fwd, t_inv_swapped, t_inv_natural = tables
    nf = filt.shape[1]

    f_re, f_im = _hy_stage1(filt.reshape(n2, h1, nf), t_fwd, G=G)
    blk = lambda c: pl.BlockSpec((kb, n2, c), lambda i: (i, 0, 0))
    const = lambda s: pl.BlockSpec(s, lambda i: (0, 0))
    fshape = jax.ShapeDtypeStruct((2, n1, k2, D_HY), F32)
    fspec = pl.BlockSpec((2, kb, k2, D_HY), lambda i: (0, i, 0, 0))
    ospec = lambda order: pl.BlockSpec((1, kb, k2, D_HY), lambda i: (order, i, 0, 0))
    fre, fim = pl.pallas_call(
        functools.partial(_hy_fspec2_kernel, kb=kb, k2=k2),
        grid=(n1 // kb,),
        in_specs=[blk(nf), blk(nf), pl.BlockSpec((2, 1, D_HY), lambda i: (0, 0, 0)), const((n2, 2 * n2))],
        out_specs=[fspec, fspec],
        out_shape=[fshape, fshape],
        compiler_params=_cp("arbitrary"),
        name="hy_fspec2",
    )(f_re, f_im, skip[:, None, :], t2_bf)

    ashape = jax.ShapeDtypeStruct((n1, n2, cols), BF16)
    grp = pl.BlockSpec((n1, G, cols), lambda j: (0, j, 0))
    tspec = lambda t: pl.BlockSpec((1,) + t.shape[1:], lambda j: (j, 0, 0))
    a_re, a_im = _hy_stage1(z3, t_fwd, G=G)
    for order in range(2):
        b_re, b_im = pl.pallas_call(
            functools.partial(_hy_stage2_kernel, kb=kb, k2=k2, n2=n2, nbc=nbc),
            grid=(n1 // kb,),
            in_specs=[blk(cols), blk(cols), ospec(order), ospec(order), const((n2, 2 * n2)), const((2 * n2, n2))],
            out_specs=[blk(cols), blk(cols)],
            out_shape=[ashape, ashape],
            compiler_params=_cp("arbitrary"),
            name="hy_stage2",
        )(a_re, a_im, fre, fim, t2_bf, t2i_bf)
        if order == 0:
            a_re, a_im = pl.pallas_call(
                functools.partial(_hy_stage31_kernel, G=G, n1=n1, ct=D_HY),
                grid=(n2 // G,),
                in_specs=[grp, grp, tspec(t_inv_swapped), pl.BlockSpec((G, h1, cols), lambda j: (j, 0, 0)),
                          tspec(t_fwd)],
                out_specs=[grp, grp],
                out_shape=[ashape, ashape],
                compiler_params=_cp("arbitrary"),
                name="hy_stage31",
            )(b_re, b_im, t_inv_swapped, g1_3, t_fwd)
        else:
            io = pl.BlockSpec((h1, G, cols), lambda j: (0, j, 0))
            out = pl.pallas_call(
                functools.partial(_hy_stage3_kernel, G=G, n1=n1, ct=D_HY),
                grid=(n2 // G,),
                in_specs=[grp, grp, tspec(t_inv_natural), io],
                out_specs=io,
                out_shape=jax.ShapeDtypeStruct((h1, n2, cols), BF16),
                compiler_params=_cp("arbitrary"),
                name="hy_stage3",
            )(b_re, b_im, t_inv_natural, g2.reshape(h1, n2, cols))
    return out.reshape(L, cols)


def _hyena(hv_t, h1_t, h2_t, conv_w, conv_b, filt, skip, tables=None):
    L, cols = hv_t.shape
    if _two_level(L):
        z3, g1_3, g2 = _hy_conv3(hv_t, h1_t, h2_t, conv_w, conv_b, swap=(True, True, False))
        return _hyena_two_level(z3, g1_3, g2, filt, skip, tables or _two_level_tables(L))
    z, g1, g2 = _hy_conv3(hv_t, h1_t, h2_t, conv_w, conv_b)
    return _hyena_direct(z, g1, g2, filt, skip)


def _route(logits):
    lane_i = lax.broadcasted_iota(jnp.int32, logits.shape, 1)
    lane = lane_i.astype(F32)
    is_g = (lane_i >= N_EXPERTS) & (lane_i < N_EXPERTS + N_GROUPS)
    big = float(LANES)
    gl = jnp.where(is_g, logits, NEG)
    gmax = jnp.max(gl, axis=-1, keepdims=True)
    gidx = jnp.min(jnp.where(gl == gmax, lane - N_EXPERTS, big), axis=-1, keepdims=True)
    p_group = 1.0 / jnp.sum(jnp.where(is_g, jnp.exp(gl - gmax), 0.0), axis=-1, keepdims=True)
    sel = (lane_i < N_EXPERTS) & ((lane_i // EXPERTS_PER_GROUP) == gidx.astype(jnp.int32))
    el = jnp.where(sel, logits, NEG)
    ee = jnp.where(sel, jnp.exp(el - jnp.max(el, axis=-1, keepdims=True)), 0.0)
    pe = ee / jnp.sum(ee, axis=-1, keepdims=True)
    p1 = jnp.max(jnp.where(sel, pe, -1.0), axis=-1, keepdims=True)
    i1 = jnp.min(jnp.where(sel & (pe == p1), lane, big), axis=-1, keepdims=True)
    rest = sel & (lane != i1)
    p2 = jnp.max(jnp.where(rest, pe, -1.0), axis=-1, keepdims=True)
    i2 = jnp.min(jnp.where(rest & (pe == p2), lane, big), axis=-1, keepdims=True)
    den = p1 + p2
    cmb = jnp.where(lane == i1, p_group * (p1 / den), jnp.where(lane == i2, p_group * (p2 / den), 0.0))
    return jnp.where(lane_i == N_EXPERTS, gidx, cmb)


def _outproj_kernel(att_ref, hf_ref, hb_ref, gt_ref, hy_ref, x_ref, mod_ref, w_ref, g_ref, b_ref,
                    wr_ref, br_ref, x1_ref, u2_ref, cmb_ref):
    bpt = att_ref.shape[0]

    def rows(ref):
        return jnp.concatenate([ref[:, b * D_RNN:(b + 1) * D_RNN] for b in range(bpt)], axis=0)

    rg = (rows(hf_ref) + rows(hb_ref)) * jax.nn.gelu(rows(gt_ref).astype(F32))
    att = att_ref[...].reshape(bpt * att_ref.shape[1], D_ATT)
    mix = (jnp.dot(att, w_ref[0:D_ATT, :], preferred_element_type=F32)
           + jnp.dot(rg.astype(BF16), w_ref[D_ATT:D_ATT + D_RNN, :], preferred_element_type=F32)
           + jnp.dot(rows(hy_ref).astype(BF16), w_ref[D_ATT + D_RNN:, :], preferred_element_type=F32))
    m = mod_ref[0]
    x1 = _ln(ALPHA * x_ref[...] + m[2:3] * mix) * g_ref[...] + b_ref[...]
    u2 = _ln(x1) * (1.0 + m[4:5]) + m[3:4]
    u_hi = u2.astype(BF16)
    u_lo = (u2 - u_hi.astype(F32)).astype(BF16)
    wr = wr_ref[...]
    w_hi = wr.astype(BF16)
    w_lo = (wr - w_hi.astype(F32)).astype(BF16)
    hi = jnp.dot(u_hi, jnp.concatenate([w_hi, w_lo], axis=1), preferred_element_type=F32)
    lo = jnp.dot(u_lo, w_hi, preferred_element_type=F32)
    logits = hi[:, 0:LANES] + hi[:, LANES:] + lo + br_ref[...]
    x1_ref[...] = x1
    u2_ref[...] = u_hi
    cmb_ref[...] = _route(logits)


def _outproj(att, hf_t, hb_t, gt_t, hy_t, x2d, mod_l, w_bf, ln_g, ln_b, wr, br, *, nb, L, tm,
             cond_base, cond_per_batch):
    tpb, bpt = _tile_split(nb, L, tm, cond_per_batch)
    tm = L * bpt // tpb
    d = x2d.shape[1]
    cond = (lambda i: cond_base + i // tpb) if cond_per_batch else (lambda i: cond_base)
    t_spec = pl.BlockSpec((L // tpb, bpt * D_RNN), lambda i: (i % tpb, i // tpb))
    tok = lambda c: pl.BlockSpec((tm, c), lambda i: (i, 0))
    const = lambda s: pl.BlockSpec(s, lambda i: (0, 0))
    n = nb * L
    return pl.pallas_call(
        _outproj_kernel,
        grid=(nb * tpb // bpt,),
        in_specs=[pl.BlockSpec((bpt, L // tpb, D_ATT), lambda i: (i // tpb, i % tpb, 0)),
                  t_spec, t_spec, t_spec, t_spec, tok(d),
                  pl.BlockSpec((1, 6, d), lambda i: (cond(i), 0, 0)),
                  const((d, d)), const((1, d)), const((1, d)), const((d, LANES)), const((1, LANES))],
        out_specs=[tok(d), tok(d), tok(LANES)],
        out_shape=[jax.ShapeDtypeStruct((n, d), F32), jax.ShapeDtypeStruct((n, d), BF16),
                   jax.ShapeDtypeStruct((n, LANES), F32)],
        compiler_params=_cp("arbitrary"),
        name="outproj",
    )(att, hf_t, hb_t, gt_t, hy_t, x2d, mod_l, w_bf, ln_g[None], ln_b[None], wr, br)


def _moe_kernel(u_ref, cmb_ref, x1_ref, mod_ref, wgu_ref, wd_ref, g_ref, b_ref, o_ref, ys_ref, *, T, BLK):
    SL = T + N_GROUPS * BLK
    u = u_ref[...]
    cmb = cmb_ref[...]
    lane_f = lax.broadcasted_iota(jnp.int32, (T, LANES), 1).astype(F32)
    gid_col = jnp.sum(jnp.where(lane_f == float(N_EXPERTS), cmb, 0.0), axis=-1, keepdims=True)
    onehot = jnp.where(lane_f == gid_col, 1.0, 0.0)
    ri = lax.broadcasted_iota(jnp.int32, (T, T), 0)
    ci = lax.broadcasted_iota(jnp.int32, (T, T), 1)
    before = jnp.where(ci < ri, 1.0, 0.0).astype(BF16)
    rank_col = jnp.sum(onehot * jnp.dot(before, onehot.astype(BF16), preferred_element_type=F32),
                       axis=-1, keepdims=True)
    counts = jnp.sum(onehot, axis=0, keepdims=True)
    lane_row = lax.broadcasted_iota(jnp.int32, (1, LANES), 1)
    base = jnp.int32(0)
    bases = []
    base_row = jnp.zeros((1, LANES), jnp.int32)
    for c in range(N_GROUPS):
        n_c = jnp.sum(jnp.where(lane_row == c, counts, 0.0)).astype(jnp.int32)
        bases.append(base)
        base_row = jnp.where(lane_row == c, base, base_row)
        base = base + ((n_c + (BLK - 1)) // BLK) * BLK
    total = base
    dest_f = jnp.sum(onehot * base_row.astype(F32), axis=-1, keepdims=True) + rank_col
    dest_col = dest_f.astype(jnp.int32)
    dest_row = jnp.broadcast_to(dest_f, (T, LANES)).T[0:1, :].astype(jnp.int32)
    cmb_hi = cmb.astype(BF16)
    cmb_lo = (cmb - cmb_hi.astype(F32)).astype(BF16)
    blk_rows = lax.broadcasted_iota(jnp.int32, (BLK, T), 0)
    blk_lane = lax.broadcasted_iota(jnp.int32, (BLK, LANES), 1)

    def block(b, carry):
        r0 = b * BLK
        rows = pl.ds(pl.multiple_of(r0, BLK), BLK)

        @pl.when(r0 < total)
        def _():
            grp = ((r0 >= bases[1]).astype(jnp.int32) + (r0 >= bases[2]).astype(jnp.int32)
                   + (r0 >= bases[3]).astype(jnp.int32))
            p = jnp.where(blk_rows + r0 == dest_row, 1.0, 0.0).astype(BF16)
            xb = jnp.dot(p, u, preferred_element_type=F32).astype(BF16)
            wb = (jnp.dot(p, cmb_hi, preferred_element_type=F32)
                  + jnp.dot(p, cmb_lo, preferred_element_type=F32))
            gu = jnp.dot(xb, wgu_ref[grp], preferred_element_type=F32)
            hs = []
            for j in range(EXPERTS_PER_GROUP):
                gate = gu[:, 2 * j * D_EXPERT:(2 * j + 1) * D_EXPERT]
                up = gu[:, (2 * j + 1) * D_EXPERT:(2 * j + 2) * D_EXPERT]
                w = jnp.sum(jnp.where(blk_lane == grp * EXPERTS_PER_GROUP + j, wb, 0.0), axis=-1, keepdims=True)
                hs.append((gate * _sigmoid(gate) * up * w).astype(BF16))
            ys_ref[rows, :] = jnp.dot(jnp.concatenate(hs, axis=1), wd_ref[grp],
                                      preferred_element_type=F32).astype(BF16)

        @pl.when(r0 >= total)
        def _():
            ys_ref[rows, :] = jnp.zeros((BLK, u.shape[1]), BF16)

        return carry

    lax.fori_loop(0, SL // BLK, block, 0)
    p_t = jnp.where(lax.broadcasted_iota(jnp.int32, (T, SL), 1) == dest_col, 1.0, 0.0).astype(BF16)
    moe = jnp.dot(p_t, ys_ref[...], preferred_element_type=F32)
    m = mod_ref[0]
    o_ref[...] = _ln(ALPHA * x1_ref[...] + m[5:6] * moe) * g_ref[...] + b_ref[...]


def _group_gate_up(w_gate, w_up):
    _, d, f = w_gate.shape
    gu = jnp.concatenate([w_gate, w_up], axis=-1).astype(BF16).reshape(N_GROUPS, EXPERTS_PER_GROUP, d, 2 * f)
    return gu.transpose(0, 2, 1, 3).reshape(N_GROUPS, d, EXPERTS_PER_GROUP * 2 * f)


def _moe(u2, cmb, x1, mod_l, wgu, wd, ln_g, ln_b, *, tm, tiles_per_cond, cond_base, cond_per_batch):
    n, d = x1.shape
    cond = (lambda i: cond_base + i // tiles_per_cond) if cond_per_batch else (lambda i: cond_base)
    tok = lambda c: pl.BlockSpec((tm, c), lambda i: (i, 0))
    const = lambda s: pl.BlockSpec(s, lambda i: (0,) * len(s))
    resident = lambda s: pl.BlockSpec(s, lambda i: (0,) * len(s), pipeline_mode=pl.Buffered(1))
    return pl.pallas_call(
        functools.partial(_moe_kernel, T=tm, BLK=MOE_BLK),
        grid=(n // tm,),
        in_specs=[tok(d), tok(LANES), tok(d),
                  pl.BlockSpec((1, 6, d), lambda i: (cond(i), 0, 0)),
                  resident(wgu.shape), resident(wd.shape),
                  const((1, d)), const((1, d))],
        out_specs=tok(d),
        out_shape=jax.ShapeDtypeStruct((n, d), F32),
        scratch_shapes=[pltpu.VMEM((tm + N_GROUPS * MOE_BLK, d), BF16)],
        compiler_params=_cp("arbitrary"),
        name="moe",
    )(u2, cmb, x1, mod_l, wgu, wd, ln_g[None], ln_b[None])


def _stream_layer(x2d, mod_l, p, *, nb, L, tm, cond_base, cond_per_batch, ctx, cache=None, h0=None, prev_kv=None):
    kw = dict(nb=nb, L=L, tm=tm, cond_base=cond_base, cond_per_batch=cond_per_batch)
    q, k, v, xr_t, gt_t, hv_t, h1_t, h2_t = _inproj(x2d, mod_l, p["w_in"], kv_dtype=F32 if ctx else BF16,
                                                    prev_kv=prev_kv, **{**kw, "tm": INPROJ_TILE})
    if ctx:
        att = _ctx_attention(q, k, v)
    else:
        att = _nbr_attention(q, k, v, cache[0], cache[1], p["nbr_bias"])
    a_f, b_f, a_b, b_b = _rg_gates(xr_t, p["rg_conv_w"], p["rg_conv_b"], *p["rg_w"])
    hf_t, hb_t = _rg_scan(a_f, b_f, a_b, b_b, h0[0], h0[1])
    hy_t = _hyena(hv_t, h1_t, h2_t, p["hy_conv_w"], p["hy_conv_b"], p["hy_filt"][L], p["hy_skip"],
                  p["hy_tables"].get(L))
    x1, u2, cmb = _outproj(att, hf_t, hb_t, gt_t, hy_t, x2d, mod_l, p["w_out"], p["ln1_g"], p["ln1_b"],
                           p["router_w"], p["router_b"], **kw)
    tm_moe = min(MOE_TILE, nb * L)
    x2 = _moe(u2, cmb, x1, mod_l, p["moe_wgu"], p["moe_wd"], p["ln2_g"], p["ln2_b"], tm=tm_moe,
              tiles_per_cond=max(L // tm_moe, 1), cond_base=cond_base, cond_per_batch=cond_per_batch)
    return x2, k, v, hf_t, hb_t


def kernel(x_prompt, x_sample, cache_k, cache_v, state_rglru, c, c_ctx, w_mod, b_mod, w_in, attn_rpb, rg_conv_w, rg_conv_b, rg_wa, rg_ba, rg_wx, rg_bx, rg_lambda, hy_conv_w, hy_conv_b, hy_w1, hy_b1, hy_w2, hy_b2, hy_w3, hy_freq, hy_decay, hy_skip, w_out, ln1_g, ln1_b, router_wg, router_bg, router_we, router_be, moe_w_gate, moe_w_up, moe_w_down, ln2_g, ln2_b):
    nbc, Lc, d = x_prompt.shape
    nbl, Ll, _ = x_sample.shape
    depth = w_mod.shape[0]
    past = cache_k.shape[2]

    cond8 = jnp.zeros((8, d), F32).at[0].set(c_ctx).at[1:1 + nbl].set(c)
    mod = _modulation(cond8, w_mod, b_mod).reshape(depth, 8, 6, d)

    xp = x_prompt.reshape(nbc * Lc, d)
    xs = x_sample.reshape(nbl * Ll, d)
    kv, new_h = None, []
    hy_tables = {L: _two_level_tables(L) for L in {Lc, Ll} if _two_level(L)}
    for l in range(depth):
        pad = jnp.zeros((d, LANES - N_EXPERTS - N_GROUPS), F32)
        p = {
            "w_in": w_in[l].astype(BF16),
            "nbr_bias": _nbr_bias_tiles(attn_rpb[l]),
            "rg_conv_w": rg_conv_w[l], "rg_conv_b": rg_conv_b[l],
            "rg_w": _rg_weights(rg_wa[l], rg_ba[l], rg_wx[l], rg_bx[l], rg_lambda[l]),
            "hy_conv_w": hy_conv_w[l], "hy_conv_b": hy_conv_b[l], "hy_skip": hy_skip[l], "hy_tables": hy_tables,
            "hy_filt": {L: _hy_filters(L, hy_w1[l], hy_b1[l], hy_w2[l], hy_b2[l], hy_w3[l], hy_freq[l], hy_decay[l])
                        for L in sorted({Lc, Ll})},
            "w_out": w_out[l].astype(BF16), "ln1_g": ln1_g[l], "ln1_b": ln1_b[l],
            "router_w": jnp.concatenate([router_we[l], router_wg[l], pad], axis=1),
            "router_b": jnp.concatenate([router_be[l], router_bg[l], pad[0]])[None],
            "moe_wgu": _group_gate_up(moe_w_gate[l], moe_w_up[l]),
            "moe_wd": moe_w_down[l].astype(BF16).reshape(N_GROUPS, EXPERTS_PER_GROUP * D_EXPERT, d),
            "ln2_g": ln2_g[l], "ln2_b": ln2_b[l],
        }
        zeros_h = jnp.zeros((1, nbc * D_RNN), F32)
        xp, k, v, hf_t, hb_t = _stream_layer(
            xp, mod[l], p, nb=nbc, L=Lc, tm=ROW_CHUNK, cond_base=0, cond_per_batch=False, ctx=True,
            h0=(zeros_h, zeros_h), prev_kv=kv)
        kv = (k, v)
        new_h.append(jnp.stack([hf_t[Lc - 1].reshape(nbc, D_RNN), hb_t[0].reshape(nbc, D_RNN)], axis=1))

        cache = (cache_k[:, l].reshape(nbl, past, D_ATT), cache_v[:, l].reshape(nbl, past, D_ATT))
        h0 = (state_rglru[:, l, 0].reshape(1, nbl * D_RNN), state_rglru[:, l, 1].reshape(1, nbl * D_RNN))
        xs, _, _, _, _ = _stream_layer(
            xs, mod[l], p, nb=nbl, L=Ll, tm=ROW_CHUNK, cond_base=1, cond_per_batch=True, ctx=False,
            cache=cache, h0=h0)

    return (xp.reshape(nbc, Lc, d), xs.reshape(nbl, Ll, d),
            kv[0].reshape(nbc, depth, Lc, N_HEADS, HEAD_DIM), kv[1].reshape(nbc, depth, Lc, N_HEADS, HEAD_DIM),
            jnp.stack(new_h, axis=1))
```

```python
import functools
import math

import numpy as np
import jax
import jax.numpy as jnp
from jax import lax
from jax.experimental import pallas as pl
from jax.experimental.pallas import tpu as pltpu

F32 = jnp.float32
BF16 = jnp.bfloat16
HIGHEST = lax.Precision.HIGHEST

DEPTH = 2
D_MODEL = 1024
N_HEADS = 8
HEAD_DIM = 64
D_ATT = N_HEADS * HEAD_DIM
D_RNN = 256
D_HY = 256
GRID_W = 64
WIN_R = 8
WIN_C = 16
RG_CONV = 4
RG_C = 8.0
HY_CONV = 3
POS_EMB_DIM = 33
N_GROUPS = 4
EXPERTS_PER_GROUP = 4
N_EXPERTS = 16
D_EXPERT = 256
ALPHA = (2.0 * DEPTH) ** 0.25
LN_EPS = 1e-5
NEG = -1e30

Q_ROWS = 4
K_ROWS = Q_ROWS + WIN_R
LANES = 128
HALO = 8
ROW_CHUNK = 512
SCAN_LANES = 1024
BLOCK_BYTES = 1 << 20
MOD_COL_TILES = 4
FFT_MIN_LEN = 2048
FFT_KB = 4
CTX_ATTN_TOKENS = 1024
INPROJ_TILE = 1024
MOE_TILE = 512
MOE_BLK = 128
STRIDE_PAD = 4
FFT_G = 16
FFT_N1 = 64
VMEM_LIMIT = 52 * 1024 * 1024


def _cp(*sem):
    return pltpu.CompilerParams(dimension_semantics=sem, vmem_limit_bytes=VMEM_LIMIT)


def _sigmoid(x):
    return 0.5 * jnp.tanh(0.5 * x) + 0.5


def _ln(x):
    mu = jnp.mean(x, axis=-1, keepdims=True)
    xc = x - mu
    var = jnp.mean(xc * xc, axis=-1, keepdims=True)
    return xc * lax.rsqrt(var + LN_EPS)


def _mod_kernel(c_ref, w_ref, b_ref, o_ref):
    c = c_ref[...]
    s = c * jax.nn.sigmoid(c)
    o_ref[0] = jnp.dot(s, w_ref[0], precision=HIGHEST, preferred_element_type=F32) + b_ref[0]


def _modulation(cond8, w_mod, b_mod):
    depth, d, n = w_mod.shape
    tn = n // MOD_COL_TILES
    return pl.pallas_call(
        _mod_kernel,
        grid=(depth, n // tn),
        in_specs=[pl.BlockSpec((8, d), lambda l, j: (0, 0)),
                  pl.BlockSpec((1, d, tn), lambda l, j: (l, 0, j)),
                  pl.BlockSpec((1, 1, tn), lambda l, j: (l, 0, j))],
        out_specs=pl.BlockSpec((1, 8, tn), lambda l, j: (l, 0, j)),
        out_shape=jax.ShapeDtypeStruct((depth, 8, n), F32),
        compiler_params=_cp("arbitrary", "arbitrary"),
        name="modulation",
    )(cond8, w_mod, b_mod.reshape(depth, 1, n))


def _inproj_kernel(x_ref, mod_ref, w_ref, *refs, n_prev):
    if n_prev:
        pk_ref, pv_ref = refs[:2]
        refs = refs[2:]
    q_ref, k_ref, v_ref, xr_ref, gt_ref, hv_ref, h1_ref, h2_ref = refs
    for i in range(n_prev):
        k_ref[:, i] = pk_ref[:, i]
        v_ref[:, i] = pv_ref[:, i]
    m = mod_ref[0]
    u = (_ln(x_ref[...]) * (1.0 + m[1:2]) + m[0:1]).astype(BF16)
    bpt, rows = q_ref.shape[0], q_ref.shape[1]

    def proj(lo, hi):
        return jnp.dot(u, w_ref[:, lo:hi], preferred_element_type=F32)

    o = 0
    q_ref[...] = (proj(o, o + D_ATT) * (HEAD_DIM ** -0.5)).astype(q_ref.dtype).reshape(q_ref.shape)
    o += D_ATT
    k_ref[:, n_prev] = proj(o, o + D_ATT).astype(k_ref.dtype).reshape(bpt, rows, D_ATT)
    o += D_ATT
    v_ref[:, n_prev] = proj(o, o + D_ATT).astype(v_ref.dtype).reshape(bpt, rows, D_ATT)
    o += D_ATT
    for ref in (xr_ref, gt_ref, hv_ref, h1_ref, h2_ref):
        y = proj(o, o + D_RNN).astype(ref.dtype)
        for b in range(bpt):
            ref[:, b * D_RNN:(b + 1) * D_RNN] = y[b * rows:(b + 1) * rows]
        o += D_RNN


def _tile_split(nb, L, tm, cond_per_batch):
    if tm <= L:
        return L // tm, 1
    bpt = 1 if cond_per_batch else min(nb, tm // L)
    while nb % bpt:
        bpt -= 1
    return 1, bpt


def _inproj(x2d, mod_l, w_bf, *, nb, L, tm, cond_base, cond_per_batch, kv_dtype, prev_kv=None):
    tpb, bpt = _tile_split(nb, L, tm, cond_per_batch)
    tm = L * bpt // tpb
    d = x2d.shape[1]
    d_in = w_bf.shape[1]
    n_prev = prev_kv[0].shape[1] if prev_kv else 0
    cond = (lambda i: cond_base + i // tpb) if cond_per_batch else (lambda i: cond_base)
    att_spec = pl.BlockSpec((bpt, L // tpb, D_ATT), lambda i: (i // tpb, i % tpb, 0))
    kv_spec = lambda n: pl.BlockSpec((bpt, n, L // tpb, D_ATT), lambda i: (i // tpb, 0, i % tpb, 0))
    t_spec = pl.BlockSpec((L // tpb, bpt * D_RNN), lambda i: (i % tpb, i // tpb))
    kv_shape = jax.ShapeDtypeStruct((nb, n_prev + 1, L, D_ATT), kv_dtype)
    t_shape = jax.ShapeDtypeStruct((L, nb * D_RNN), BF16)
    return pl.pallas_call(
        functools.partial(_inproj_kernel, n_prev=n_prev),
        grid=(nb * tpb // bpt,),
        in_specs=[pl.BlockSpec((tm, d), lambda i: (i, 0)),
                  pl.BlockSpec((1, 6, d), lambda i: (cond(i), 0, 0)),
                  pl.BlockSpec((d, d_in), lambda i: (0, 0))] + ([kv_spec(n_prev)] * 2 if n_prev else []),
        out_specs=[att_spec, kv_spec(n_prev + 1), kv_spec(n_prev + 1), t_spec, t_spec, t_spec, t_spec, t_spec],
        out_shape=[jax.ShapeDtypeStruct((nb, L, D_ATT), BF16), kv_shape, kv_shape,
                   t_shape, t_shape, t_shape, t_shape, t_shape],
        compiler_params=_cp("arbitrary"),
        name="inproj",
    )(x2d, mod_l, w_bf, *(prev_kv or ()))


def _nt(a, b):
    return lax.dot_general(a, b, (((1,), (1,)), ((), ())), preferred_element_type=F32)


def _ctx_attn_kernel(q_ref, k_ref, v_ref, o_ref):
    for b in range(q_ref.shape[0]):
        q = q_ref[b]
        k = k_ref[b, 0].astype(BF16)
        v = v_ref[b, 0].astype(BF16)
        outs = []
        for h in range(N_HEADS):
            sl = slice(h * HEAD_DIM, (h + 1) * HEAD_DIM)
            s = _nt(q[:, sl], k[:, sl])
            e = jnp.exp(s - jnp.max(s, axis=-1, keepdims=True))
            den = jnp.sum(e, axis=-1, keepdims=True)
            outs.append(jnp.dot(e.astype(BF16), v[:, sl], preferred_element_type=F32) / den)
        o_ref[b] = jnp.concatenate(outs, axis=-1).astype(o_ref.dtype)


def _ctx_attention(q, k, v):
    nb, L, _ = q.shape
    last = k.shape[1] - 1
    _, bpt = _tile_split(nb, L, CTX_ATTN_TOKENS, False)
    spec = pl.BlockSpec((bpt, L, D_ATT), lambda b: (b, 0, 0))
    kv_spec = pl.BlockSpec((bpt, 1, L, D_ATT), lambda b: (b, last, 0, 0))
    return pl.pallas_call(
        _ctx_attn_kernel,
        grid=(nb // bpt,),
        in_specs=[spec, kv_spec, kv_spec],
        out_specs=spec,
        out_shape=jax.ShapeDtypeStruct((nb, L, D_ATT), BF16),
        compiler_params=_cp("arbitrary"),
        name="ctx_attn",
    )(q, k, v)


N_DR = 2 * WIN_R - 1
PAIR_LEFT = N_DR - 1
PAIR_RIGHT = 2 * N_DR - 1
PAIR_NONE = 3 * N_DR - 1


def _nbr_bias_tiles(rpb):
    cols = np.arange(GRID_W)
    c_start = np.clip(cols - WIN_C // 2, 0, GRID_W - WIN_C)
    col_mask = (cols[None, :] >= c_start[:, None]) & (cols[None, :] < c_start[:, None] + WIN_C)
    dc_idx = np.clip(cols[None, :] - cols[:, None], -(WIN_C - 1), WIN_C - 1) + (WIN_C - 1)
    sel = (dc_idx[None] == np.arange(2 * WIN_C - 1)[:, None, None]).astype(np.float32)
    colb = jnp.einsum('hab,bqk->haqk', rpb.astype(F32), jnp.asarray(sel), precision=HIGHEST)
    colb = jnp.where(col_mask[None, None], colb, NEG)
    neg = jnp.full_like(colb, NEG)
    both = jnp.concatenate([colb[:, :-1], colb[:, 1:]], axis=-1)
    left_masked = jnp.concatenate([neg, colb], axis=-1)
    right_masked = jnp.concatenate([colb, neg], axis=-1)
    none = jnp.concatenate([neg[:, :1], neg[:, :1]], axis=-1)
    return jnp.concatenate([both, left_masked, right_masked, none], axis=1)


def _nbr_attn_kernel(q_ref, k_ref, v_ref, kc_ref, vc_ref, bias_ref, o_ref, *, rows):
    g = pl.program_id(1)
    r0 = jnp.clip(Q_ROWS * g - WIN_R // 2, 0, rows - K_ROWS)
    start = pl.multiple_of(r0 * GRID_W, Q_ROWS * GRID_W)
    nk = K_ROWS * GRID_W
    kw = k_ref[0, 0, pl.ds(start, nk), :]
    vw = v_ref[0, 0, pl.ds(start, nk), :]
    kc = kc_ref[0].astype(BF16)
    vc = vc_ref[0].astype(BF16)
    q = q_ref[0]
    tile = []
    for j in range(Q_ROWS):
        r = Q_ROWS * g + j
        lo = jnp.clip(r - WIN_R // 2, 0, rows - WIN_R) - r + (WIN_R - 1)
        tile.append([])
        for p in range(K_ROWS // 2):
            d0 = r0 + 2 * p - r + (WIN_R - 1)
            lv = (d0 >= lo) & (d0 < lo + WIN_R)
            rv = (d0 + 1 >= lo) & (d0 + 1 < lo + WIN_R)
            idx = jnp.where(lv & rv, d0, jnp.where(rv, PAIR_LEFT + d0 + 1, jnp.where(lv, PAIR_RIGHT + d0, PAIR_NONE)))
            tile[j].append(jnp.clip(idx, 0, PAIR_NONE))
    outs = []
    for h in range(N_HEADS):
        sl = slice(h * HEAD_DIM, (h + 1) * HEAD_DIM)
        qh = q[:, sl]
        bias = jnp.concatenate(
            [jnp.concatenate([bias_ref[h, tile[j][p]] for p in range(K_ROWS // 2)], axis=-1)
             for j in range(Q_ROWS)], axis=0)
        s1 = _nt(qh, kw[:, sl]) + bias
        s2 = _nt(qh, kc[:, sl])
        m = jnp.maximum(jnp.max(s1, axis=-1, keepdims=True), jnp.max(s2, axis=-1, keepdims=True))
        e1 = jnp.exp(s1 - m)
        e2 = jnp.exp(s2 - m)
        den = jnp.sum(e1, axis=-1, keepdims=True) + jnp.sum(e2, axis=-1, keepdims=True)
        o = (jnp.dot(e1.astype(BF16), vw[:, sl], preferred_element_type=F32)
             + jnp.dot(e2.astype(BF16), vc[:, sl], preferred_element_type=F32))
        outs.append(o / den)
    o_ref[0] = jnp.concatenate(outs, axis=-1).astype(o_ref.dtype)


def _nbr_attention(q, k, v, kc, vc, bias):
    nb, L, _ = q.shape
    rows = L // GRID_W
    ng = rows // Q_ROWS
    past = kc.shape[1]
    tq = Q_ROWS * GRID_W
    full = pl.BlockSpec((1, 1, L, D_ATT), lambda b, g: (b, 0, 0, 0))
    ctx = pl.BlockSpec((1, past, D_ATT), lambda b, g: (b, 0, 0))
    qspec = pl.BlockSpec((1, tq, D_ATT), lambda b, g: (b, g, 0))
    bspec = pl.BlockSpec(bias.shape, lambda b, g: (0, 0, 0, 0), pipeline_mode=pl.Buffered(1))
    return pl.pallas_call(
        functools.partial(_nbr_attn_kernel, rows=rows),
        grid=(nb, ng),
        in_specs=[qspec, full, full, ctx, ctx, bspec],
        out_specs=qspec,
        out_shape=jax.ShapeDtypeStruct((nb, L, D_ATT), BF16),
        compiler_params=_cp("arbitrary", "arbitrary"),
        name="nbr_attn",
    )(q, k, v, kc, vc, bias)


def _fill_padded(xp_ref, src_ref, L):
    z = jnp.zeros((HALO, xp_ref.shape[1]), F32)
    xp_ref[0:HALO, :] = z
    xp_ref[HALO:HALO + L, :] = src_ref[...].astype(F32)
    xp_ref[HALO + L:2 * HALO + L, :] = z


def _rg_gates_kernel(x_ref, cw_ref, cb_ref, w_ref, b_ref, lam_ref,
                     af_ref, bf_ref, ab_ref, bb_ref, xp_ref, *, L, tc, m):
    nh = cw_ref.shape[0]
    for sc in range(m):
        half = (pl.program_id(0) * m + sc) % nh
        cols = slice(sc * LANES, (sc + 1) * LANES)
        _fill_padded(xp_ref, x_ref.at[:, cols], L)
        sp = jax.nn.softplus(-lam_ref[half])
        cw = cw_ref[half]
        for t0 in range(0, L, tc):
            xc = cb_ref[half]
            for k in range(RG_CONV):
                xc = xc + cw[k:k + 1, :] * xp_ref[pl.ds(HALO + t0 + k - RG_CONV // 2, tc), :]
            z = jnp.dot(xc.astype(BF16), w_ref[half], preferred_element_type=F32) + b_ref[half]
            for e, (a_ref, o_ref) in enumerate(((af_ref, bf_ref), (ab_ref, bb_ref))):
                r = _sigmoid(z[:, e * LANES:(e + 1) * LANES])
                i = _sigmoid(z[:, (2 + e) * LANES:(3 + e) * LANES])
                log_a = (-RG_C) * r * sp[:, e * LANES:(e + 1) * LANES]
                a = jnp.exp(log_a)
                one_minus_a2 = -jnp.tanh(log_a) * (a * a + 1.0)
                a_ref[pl.ds(t0, tc), cols] = a
                o_ref[pl.ds(t0, tc), cols] = jnp.sqrt(one_minus_a2) * i * xc


def _blockdiag2(w2):
    z = jnp.zeros_like(w2[0])
    return jnp.concatenate([jnp.concatenate([w2[0], z], 1), jnp.concatenate([z, w2[1]], 1)], 0)


def _rg_weights(wa, ba, wx, bx, lam):
    ws, bs, ls = [], [], []
    for hf in range(D_RNN // LANES):
        blk = slice(2 * hf, 2 * hf + 2)
        ch = slice(hf * LANES, (hf + 1) * LANES)
        ws.append(jnp.concatenate([_blockdiag2(wa[0, blk]), _blockdiag2(wa[1, blk]),
                                   _blockdiag2(wx[0, blk]), _blockdiag2(wx[1, blk])], axis=1))
        bs.append(jnp.concatenate([ba[0, ch], ba[1, ch], bx[0, ch], bx[1, ch]])[None])
        ls.append(jnp.concatenate([lam[0, ch], lam[1, ch]])[None])
    return jnp.stack(ws).astype(BF16), jnp.stack(bs), jnp.stack(ls)


def _lane_chunks(L, cols):
    m = max(1, min(8, BLOCK_BYTES // (L * LANES * 4)))
    while cols % (m * LANES):
        m //= 2
    return m


def _rg_gates(xr_t, conv_w, conv_b, w_bd, b_cat, lam_cat):
    L, cols = xr_t.shape
    nh = D_RNN // LANES
    tc = min(L, ROW_CHUNK)
    m = _lane_chunks(L, cols)
    col = pl.BlockSpec((L, m * LANES), lambda c: (0, c))
    const = lambda a: pl.BlockSpec(a.shape, lambda c: (0,) * a.ndim)
    shape = jax.ShapeDtypeStruct((L, cols), F32)
    cw = conv_w.reshape(RG_CONV, nh, LANES).transpose(1, 0, 2)
    cb = conv_b.reshape(nh, 1, LANES)
    return pl.pallas_call(
        functools.partial(_rg_gates_kernel, L=L, tc=tc, m=m),
        grid=(cols // (m * LANES),),
        in_specs=[col, const(cw), const(cb), const(w_bd), const(b_cat), const(lam_cat)],
        out_specs=[col, col, col, col],
        out_shape=[shape, shape, shape, shape],
        scratch_shapes=[pltpu.VMEM((L + 2 * HALO, LANES), F32)],
        compiler_params=_cp("arbitrary"),
        name="rg_gates",
    )(xr_t, cw, cb, w_bd, b_cat, lam_cat)


def _rg_scan_kernel(af_ref, bf_ref, ab_ref, bb_ref, h0f_ref, h0b_ref, hf_ref, hb_ref, cf_ref, cb_ref, *, tc):
    @pl.when(pl.program_id(1) == 0)
    def _():
        cf_ref[...] = h0f_ref[...]
        cb_ref[...] = h0b_ref[...]

    def body(t, carry):
        hf, hb = carry
        hf = af_ref[pl.ds(t, 1), :] * hf + bf_ref[pl.ds(t, 1), :]
        hf_ref[pl.ds(t, 1), :] = hf
        tb = tc - 1 - t
        hb = ab_ref[pl.ds(tb, 1), :] * hb + bb_ref[pl.ds(tb, 1), :]
        hb_ref[pl.ds(tb, 1), :] = hb
        return hf, hb

    hf, hb = lax.fori_loop(0, tc, body, (cf_ref[...], cb_ref[...]), unroll=8)
    cf_ref[...] = hf
    cb_ref[...] = hb


def _rg_scan(a_f, b_f, a_b, b_b, h0f, h0b):
    L, cols = a_f.shape
    cw = min(cols, SCAN_LANES)
    tc = min(L, ROW_CHUNK)
    nt = L // tc
    fwd = pl.BlockSpec((tc, cw), lambda c, i: (i, c))
    bwd = pl.BlockSpec((tc, cw), lambda c, i: (nt - 1 - i, c))
    row = pl.BlockSpec((1, cw), lambda c, i: (0, c))
    shape = jax.ShapeDtypeStruct((L, cols), F32)
    return pl.pallas_call(
        functools.partial(_rg_scan_kernel, tc=tc),
        grid=(cols // cw, nt),
        in_specs=[fwd, fwd, bwd, bwd, row, row],
        out_specs=[fwd, bwd],
        out_shape=[shape, shape],
        scratch_shapes=[pltpu.VMEM((1, cw), F32), pltpu.VMEM((1, cw), F32)],
        compiler_params=_cp("arbitrary", "arbitrary"),
        name="rg_scan",
    )(a_f, b_f, a_b, b_b, h0f, h0b)


def _hy_filter_kernel(feat_ref, w1_ref, b1_ref, w2_ref, b2_ref, w3_ref, fr_ref, dec_ref, o_ref, *, tm):
    feats = feat_ref[...]
    dot = functools.partial(jnp.dot, precision=HIGHEST, preferred_element_type=F32)
    h = jnp.sin(fr_ref[0:1, :] * (dot(feats, w1_ref[...]) + b1_ref[...]))
    h = jnp.sin(fr_ref[1:2, :] * (dot(h, w2_ref[...]) + b2_ref[...]))
    h = dot(h, w3_ref[...])
    filt = h * jnp.exp(-feats[:, 0:1] * jnp.abs(dec_ref[...]))
    row = pl.program_id(0) * tm + lax.broadcasted_iota(jnp.int32, filt.shape, 0)
    lane = lax.broadcasted_iota(jnp.int32, filt.shape, 1)
    o_ref[...] = jnp.where((row == 0) & (lane >= 2 * D_HY), 0.0, filt)


def _hy_features(L):
    pos = np.arange(L, dtype=np.float64)
    t = np.linspace(0.0, 1.0, L)
    bands = (POS_EMB_DIM - 1) // 2
    f = np.linspace(1e-4, bands - 1, bands)
    ang = (2.0 * math.pi / L) * pos[:, None] * f[None, :]
    feats = np.concatenate([t[:, None], np.cos(ang), -np.sin(ang)], axis=-1)
    out = np.zeros((L, LANES), np.float32)
    out[:, :POS_EMB_DIM] = feats
    return out


def _two_level(L):
    return L % (FFT_N1 * 8) == 0 and L >= FFT_MIN_LEN


def _digit_swap(x, L):
    n2 = 2 * L // FFT_N1
    return x.reshape(L // n2, n2, -1).transpose(1, 0, 2).reshape(L, -1)


def _hy_filters(L, w1, b1, w2, b2, w3, freq, decay):
    tm = min(L, ROW_CHUNK)
    hid = w2.shape[0]
    n = w3.shape[1]
    w1p = jnp.zeros((LANES, hid), F32).at[:POS_EMB_DIM].set(w1)
    feats = _hy_features(L)
    if _two_level(L):
        feats = _digit_swap(feats, L)
    c = lambda s: pl.BlockSpec(s, lambda i: (0, 0))
    return pl.pallas_call(
        functools.partial(_hy_filter_kernel, tm=tm),
        grid=(L // tm,),
        in_specs=[pl.BlockSpec((tm, LANES), lambda i: (i, 0)),
                  c((LANES, hid)), c((1, hid)), c((hid, hid)), c((1, hid)), c((hid, n)), c((2, hid)), c((1, n))],
        out_specs=pl.BlockSpec((tm, n), lambda i: (i, 0)),
        out_shape=jax.ShapeDtypeStruct((L, n), F32),
        compiler_params=_cp("arbitrary"),
        name="hy_filter",
    )(jnp.asarray(feats), w1p, b1[None], w2, b2[None], w3, freq, decay.reshape(1, n))


def _conv3_kernel(v_ref, a_ref, b_ref, w_ref, cb_ref, ov_ref, oa_ref, ob_ref, xp_ref, xs_ref,
                  *, L, tc, swap, n2, m):
    h1 = L // n2
    pitch = n2 + STRIDE_PAD
    nh = w_ref.shape[0]
    for sc in range(m):
        half = (pl.program_id(0) * m + sc) % nh
        cols = slice(sc * LANES, (sc + 1) * LANES)
        for p, (src, dst) in enumerate(((v_ref, ov_ref), (a_ref, oa_ref), (b_ref, ob_ref))):
            _fill_padded(xp_ref, src.at[:, cols], L)
            w = w_ref[half, p]
            cb = cb_ref[half, p]
            if swap[p]:
                for i1 in range(h1):
                    xs_ref[pl.ds(i1 * pitch, n2 + HY_CONV - 1), :] = (
                        xp_ref[pl.ds(HALO + i1 * n2 - HY_CONV // 2, n2 + HY_CONV - 1), :])

                def body(i2, carry, dst=dst, w=w, cb=cb):
                    y = cb
                    for k in range(HY_CONV):
                        y = y + w[k:k + 1, :] * xs_ref[pl.ds(i2 + k, h1, stride=pitch), :]
                    dst[i2] = y.astype(dst.dtype)
                    return carry

                lax.fori_loop(0, n2, body, 0)
            else:
                for t0 in range(0, L, tc):
                    y = cb
                    for k in range(HY_CONV):
                        y = y + w[k:k + 1, :] * xp_ref[pl.ds(HALO + t0 + k - HY_CONV // 2, tc), :]
                    dst[pl.ds(t0, tc), cols] = y


def _hy_conv3(v_t, a_t, b_t, conv_w, conv_b, swap=(False, False, False)):
    L, cols = v_t.shape
    nh = D_HY // LANES
    tc = min(L, ROW_CHUNK)
    n2 = 2 * L // FFT_N1
    m = 1 if any(swap) else _lane_chunks(L, cols)
    col = pl.BlockSpec((L, m * LANES), lambda c: (0, c))
    col3 = pl.BlockSpec((n2, L // n2, LANES), lambda c: (0, 0, c))
    shape = jax.ShapeDtypeStruct((L, cols), F32)
    shape3 = jax.ShapeDtypeStruct((n2, L // n2, cols), F32)
    w = conv_w.reshape(HY_CONV, 3, nh, LANES).transpose(2, 1, 0, 3)
    cb = conv_b.reshape(3, nh, 1, LANES).transpose(1, 0, 2, 3)
    const = lambda a: pl.BlockSpec(a.shape, lambda c: (0,) * a.ndim)
    return pl.pallas_call(
        functools.partial(_conv3_kernel, L=L, tc=tc, swap=swap, n2=n2, m=m),
        grid=(cols // (m * LANES),),
        in_specs=[col, col, col, const(w), const(cb)],
        out_specs=[col3 if s else col for s in swap],
        out_shape=[shape, shape, shape] if not any(swap) else
                  [jax.ShapeDtypeStruct(shape3.shape, BF16 if i == 0 else F32) if s else shape
                   for i, s in enumerate(swap)],
        scratch_shapes=[pltpu.VMEM((L + 2 * HALO, LANES), F32),
                        pltpu.VMEM(((L // n2) * (n2 + STRIDE_PAD) + HALO, LANES), F32)],
        compiler_params=_cp("arbitrary"),
        name="hy_conv3",
    )(v_t, a_t, b_t, w, cb)


def _phase(p, n2):
    ang = 2.0 * np.pi * (np.asarray(p, np.int64) % n2) / n2
    return np.cos(ang), np.sin(ang)


def _dft_direct_tables(L):
    k = np.arange(L)[:, None]
    t = np.arange(L)[None, :]
    c, s = _phase((2 * k + 1) * t, 4 * L)
    fwd = np.concatenate([c, -s], axis=0)
    inv = (1.0 / L) * np.concatenate([c.T, -s.T], axis=1)
    return fwd.astype(np.float32), inv.astype(np.float32)


def _dft_two_level_tables(L, n1):
    N = 2 * L
    n2 = N // n1
    h1 = n1 // 2
    i2 = np.arange(n2)[:, None, None]
    k1 = np.arange(n1)[None, :, None]
    i1 = np.arange(h1)[None, None, :]
    c, s = _phase(2 * n2 * i1 * k1 + 2 * i2 * k1 + n2 * i1 + i2, 2 * N)
    t1 = np.concatenate([c, -s], axis=1)
    ct, st = c.transpose(0, 2, 1), s.transpose(0, 2, 1)
    t1i = (2.0 / N) * np.concatenate([ct, -st], axis=2)
    kk = np.arange(n2 // 2)[:, None]
    nn = np.arange(n2)[None, :]
    c2, s2 = _phase(nn * kk, n2)
    t2 = np.block([[c2, s2], [-s2, c2]])
    t2i = np.block([[c2.T, -s2.T], [s2.T, c2.T]])
    return tuple(x.astype(np.float32) for x in (t1, t2, t2i, t1i))


def _filter_spectrum(fre, fim, order, skip):
    fw = slice(order * D_HY, (order + 1) * D_HY)
    bw = slice((2 + order) * D_HY, (3 + order) * D_HY)
    return fre[:, fw] + fre[:, bw] + skip, fim[:, fw] - fim[:, bw]


def _spectral_product(xr, xi, fr, fi, nbc):
    yr, yi = [], []
    for b in range(nbc):
        sl = slice(b * D_HY, (b + 1) * D_HY)
        yr.append(xr[:, sl] * fr - xi[:, sl] * fi)
        yi.append(xr[:, sl] * fi + xi[:, sl] * fr)
    return jnp.concatenate([jnp.concatenate(yr, axis=1), jnp.concatenate(yi, axis=1)], axis=0)


def _matmul_kernel(a_ref, b_ref, o_ref):
    o_ref[...] = jnp.dot(a_ref[...], b_ref[...], precision=HIGHEST, preferred_element_type=F32)


def _hy_direct_kernel(z_ref, g_ref, f_ref, skip_ref, t_ref, ti_ref, o_ref, *, L, nbc, order):
    x = jnp.dot(t_ref[...], z_ref[...].astype(BF16), preferred_element_type=F32)
    fr, fi = _filter_spectrum(f_ref[0:L, :], f_ref[L:2 * L, :], order, skip_ref[0])
    y = _spectral_product(x[:L], x[L:], fr, fi, nbc).astype(BF16)
    conv = jnp.dot(ti_ref[...], y, preferred_element_type=F32)
    o_ref[...] = (g_ref[...] * conv).astype(o_ref.dtype)


def _hyena_direct(z, g1, g2, filt, skip):
    L, cols = z.shape
    fwd, inv = _dft_direct_tables(L)
    n = filt.shape[1]
    spec = pl.pallas_call(
        _matmul_kernel,
        out_shape=jax.ShapeDtypeStruct((2 * L, n), F32),
        name="hy_fspec",
    )(jnp.asarray(fwd), filt)
    tn = min(cols, 4 * D_HY)
    nbc = tn // D_HY
    col = pl.BlockSpec((L, tn), lambda c: (0, c))
    const = lambda s: pl.BlockSpec(s, lambda c: (0, 0))
    t_bf, ti_bf = jnp.asarray(fwd).astype(BF16), jnp.asarray(inv).astype(BF16)
    for order, g in enumerate((g1, g2)):
        z = pl.pallas_call(
            functools.partial(_hy_direct_kernel, L=L, nbc=nbc, order=order),
            grid=(cols // tn,),
            in_specs=[col, col, const((2 * L, n)), pl.BlockSpec((1, 1, D_HY), lambda c: (order, 0, 0)),
                      const((2 * L, L)), const((L, 2 * L))],
            out_specs=col,
            out_shape=jax.ShapeDtypeStruct((L, cols), F32 if order == 0 else BF16),
            compiler_params=_cp("arbitrary"),
            name="hy_direct",
        )(z, g, spec, skip[:, None, :], t_bf, ti_bf)
    return z


def _kron_tables(t1, t1i, G):
    n2, two_n1, h1 = t1.shape
    ng = n2 // G
    eye = np.eye(two_n1, dtype=np.float32)
    rep_rows = jnp.asarray(np.repeat(eye, G, axis=0))
    rep_cols = jnp.asarray(np.repeat(eye, G, axis=1))
    c_fwd = t1.reshape(ng, G, two_n1, h1).transpose(0, 2, 1, 3).reshape(ng, two_n1, G * h1)
    c_swapped = t1i.reshape(ng, G * h1, two_n1)
    c_natural = t1i.reshape(ng, G, h1, two_n1).transpose(0, 2, 1, 3).reshape(ng, h1 * G, two_n1)

    def expand(compact, rep, left, row_key, lane_key):
        r, c = (rep.shape[0], compact.shape[2]) if left else (compact.shape[1], rep.shape[1])
        return pl.pallas_call(
            functools.partial(_kron_expand_kernel, left=left, row_key=row_key, lane_key=lane_key),
            grid=(ng,),
            in_specs=[pl.BlockSpec((1,) + compact.shape[1:], lambda i: (i, 0, 0)),
                      pl.BlockSpec(rep.shape, lambda i: (0, 0))],
            out_specs=pl.BlockSpec((1, r, c), lambda i: (i, 0, 0)),
            out_shape=jax.ShapeDtypeStruct((ng, r, c), BF16),
            compiler_params=_cp("arbitrary"),
            name="hy_kron",
        )(jnp.asarray(compact), rep)

    fwd = expand(c_fwd, rep_rows, True, lambda r: r % G, lambda c: c // h1)
    inv_swapped = expand(c_swapped, rep_cols, False, lambda r: r // h1, lambda c: c % G)
    inv_natural = expand(c_natural, rep_cols, False, lambda r: r % G, lambda c: c % G)
    return fwd, inv_swapped, inv_natural


def _kron_expand_kernel(c_ref, rep_ref, o_ref, *, left, row_key, lane_key):
    comp = c_ref[0].astype(BF16)
    rep = rep_ref[...].astype(BF16)
    full = (jnp.dot(rep, comp, preferred_element_type=F32) if left
            else jnp.dot(comp, rep, preferred_element_type=F32))
    rows = lax.broadcasted_iota(jnp.int32, full.shape, 0)
    lanes = lax.broadcasted_iota(jnp.int32, full.shape, 1)
    o_ref[0] = jnp.where(row_key(rows) == lane_key(lanes), full, 0.0).astype(BF16)


def _hy_stage1_kernel(x_ref, t_ref, re_ref, im_ref, *, G, n1, ct):
    h1, cols = x_ref.shape[1], x_ref.shape[2]
    x = x_ref[...].reshape(G * h1, cols).astype(BF16)
    for c0 in range(0, cols, ct):
        a = jnp.dot(t_ref[0], x[:, c0:c0 + ct], preferred_element_type=F32)
        re_ref[:, :, c0:c0 + ct] = a[:n1 * G].reshape(n1, G, ct).astype(BF16)
        im_ref[:, :, c0:c0 + ct] = a[n1 * G:].reshape(n1, G, ct).astype(BF16)


def _hy_stage1(x3, t_fwd, *, G):
    n2, h1, cols = x3.shape
    n1 = t_fwd.shape[1] // (2 * G)
    ospec = pl.BlockSpec((n1, G, cols), lambda j: (0, j, 0))
    oshape = jax.ShapeDtypeStruct((n1, n2, cols), BF16)
    return pl.pallas_call(
        functools.partial(_hy_stage1_kernel, G=G, n1=n1, ct=D_HY),
        grid=(n2 // G,),
        in_specs=[pl.BlockSpec((G, h1, cols), lambda j: (j, 0, 0)),
                  pl.BlockSpec((1,) + t_fwd.shape[1:], lambda j: (j, 0, 0))],
        out_specs=[ospec, ospec],
        out_shape=[oshape, oshape],
        compiler_params=_cp("arbitrary"),
        name="hy_stage1",
    )(x3, t_fwd)


def _hy_fspec2_kernel(re_ref, im_ref, skip_ref, t2_ref, fre_ref, fim_ref, *, kb, k2):
    for j in range(kb):
        a2 = jnp.concatenate([re_ref[j], im_ref[j]], axis=0)
        x = jnp.dot(t2_ref[...], a2, preferred_element_type=F32)
        for order in range(2):
            fr, fi = _filter_spectrum(x[:k2], x[k2:], order, skip_ref[order])
            fre_ref[order, j] = fr
            fim_ref[order, j] = fi


def _hy_stage2_kernel(re_ref, im_ref, fre_ref, fim_ref, t2_ref, t2i_ref, ore_ref, oim_ref, *, kb, k2, n2, nbc):
    for j in range(kb):
        a2 = jnp.concatenate([re_ref[j], im_ref[j]], axis=0)
        x = jnp.dot(t2_ref[...], a2, preferred_element_type=F32)
        fr, fi = fre_ref[0, j], fim_ref[0, j]
        y = _spectral_product(x[:k2], x[k2:], fr, fi, nbc).astype(BF16)
        a = jnp.dot(t2i_ref[...], y, preferred_element_type=F32)
        ore_ref[j] = a[:n2].astype(BF16)
        oim_ref[j] = a[n2:].astype(BF16)


def _hy_stage3_kernel(re_ref, im_ref, t_ref, g_ref, o_ref, *, G, n1, ct):
    cols = re_ref.shape[2]
    for c0 in range(0, cols, ct):
        sl = slice(c0, c0 + ct)
        a2 = jnp.concatenate([re_ref[:, :, sl].reshape(n1 * G, ct), im_ref[:, :, sl].reshape(n1 * G, ct)], axis=0)
        conv = jnp.dot(t_ref[0], a2, preferred_element_type=F32)
        g = g_ref[:, :, sl]
        o_ref[:, :, sl] = (g * conv.reshape(g.shape)).astype(o_ref.dtype)


def _hy_stage31_kernel(re_ref, im_ref, ti_ref, g_ref, tf_ref, ore_ref, oim_ref, *, G, n1, ct):
    cols = re_ref.shape[2]
    for c0 in range(0, cols, ct):
        sl = slice(c0, c0 + ct)
        a2 = jnp.concatenate([re_ref[:, :, sl].reshape(n1 * G, ct), im_ref[:, :, sl].reshape(n1 * G, ct)], axis=0)
        conv = jnp.dot(ti_ref[0], a2, preferred_element_type=F32)
        z = g_ref[:, :, sl].reshape(conv.shape) * conv
        a = jnp.dot(tf_ref[0], z.astype(BF16), preferred_element_type=F32)
        ore_ref[:, :, sl] = a[:n1 * G].reshape(n1, G, ct).astype(BF16)
        oim_ref[:, :, sl] = a[n1 * G:].reshape(n1, G, ct).astype(BF16)


def _two_level_tables(L, *, n1=FFT_N1, G=FFT_G):
    t1, t2, t2i, t1i = _dft_two_level_tables(L, n1)
    return (jnp.asarray(t2).astype(BF16), jnp.asarray(t2i).astype(BF16)) + _kron_tables(t1, t1i, G)


def _hyena_two_level(z3, g1_3, g2, filt, skip, tables, *, n1=FFT_N1, G=FFT_G, kb=FFT_KB):
    n2, h1, cols = z3.shape
    L = n2 * h1
    k2 = n2 // 2
    nbc = cols // D_HY
    t2_bf, t2i_bf, t_fwd, t_inv_swapped, t_inv_natural = tables
    nf = filt.shape[1]

    f_re, f_im = _hy_stage1(filt.reshape(n2, h1, nf), t_fwd, G=G)
    blk = lambda c: pl.BlockSpec((kb, n2, c), lambda i: (i, 0, 0))
    const = lambda s: pl.BlockSpec(s, lambda i: (0, 0))
    fshape = jax.ShapeDtypeStruct((2, n1, k2, D_HY), F32)
    fspec = pl.BlockSpec((2, kb, k2, D_HY), lambda i: (0, i, 0, 0))
    ospec = lambda order: pl.BlockSpec((1, kb, k2, D_HY), lambda i: (order, i, 0, 0))
    fre, fim = pl.pallas_call(
        functools.partial(_hy_fspec2_kernel, kb=kb, k2=k2),
        grid=(n1 // kb,),
        in_specs=[blk(nf), blk(nf), pl.BlockSpec((2, 1, D_HY), lambda i: (0, 0, 0)), const((n2, 2 * n2))],
        out_specs=[fspec, fspec],
        out_shape=[fshape, fshape],
        compiler_params=_cp("arbitrary"),
        name="hy_fspec2",
    )(f_re, f_im, skip[:, None, :], t2_bf)

    ashape = jax.ShapeDtypeStruct((n1, n2, cols), BF16)
    grp = pl.BlockSpec((n1, G, cols), lambda j: (0, j, 0))
    tspec = lambda t: pl.BlockSpec((1,) + t.shape[1:], lambda j: (j, 0, 0))
    a_re, a_im = _hy_stage1(z3, t_fwd, G=G)
    for order in range(2):
        b_re, b_im = pl.pallas_call(
            functools.partial(_hy_stage2_kernel, kb=kb, k2=k2, n2=n2, nbc=nbc),
            grid=(n1 // kb,),
            in_specs=[blk(cols), blk(cols), ospec(order), ospec(order), const((n2, 2 * n2)), const((2 * n2, n2))],
            out_specs=[blk(cols), blk(cols)],
            out_shape=[ashape, ashape],
            compiler_params=_cp("arbitrary"),
            name="hy_stage2",
        )(a_re, a_im, fre, fim, t2_bf, t2i_bf)
        if order == 0:
            a_re, a_im = pl.pallas_call(
                functools.partial(_hy_stage31_kernel, G=G, n1=n1, ct=D_HY),
                grid=(n2 // G,),
                in_specs=[grp, grp, tspec(t_inv_swapped), pl.BlockSpec((G, h1, cols), lambda j: (j, 0, 0)),
                          tspec(t_fwd)],
                out_specs=[grp, grp],
                out_shape=[ashape, ashape],
                compiler_params=_cp("arbitrary"),
                name="hy_stage31",
            )(b_re, b_im, t_inv_swapped, g1_3, t_fwd)
        else:
            io = pl.BlockSpec((h1, G, cols), lambda j: (0, j, 0))
            out = pl.pallas_call(
                functools.partial(_hy_stage3_kernel, G=G, n1=n1, ct=D_HY),
                grid=(n2 // G,),
                in_specs=[grp, grp, tspec(t_inv_natural), io],
                out_specs=io,
                out_shape=jax.ShapeDtypeStruct((h1, n2, cols), BF16),
                compiler_params=_cp("arbitrary"),
                name="hy_stage3",
            )(b_re, b_im, t_inv_natural, g2.reshape(h1, n2, cols))
    return out.reshape(L, cols)


def _hyena(hv_t, h1_t, h2_t, conv_w, conv_b, filt, skip, tables=None):
    L, cols = hv_t.shape
    if _two_level(L):
        z3, g1_3, g2 = _hy_conv3(hv_t, h1_t, h2_t, conv_w, conv_b, swap=(True, True, False))
        return _hyena_two_level(z3, g1_3, g2, filt, skip, tables or _two_level_tables(L))
    z, g1, g2 = _hy_conv3(hv_t, h1_t, h2_t, conv_w, conv_b)
    return _hyena_direct(z, g1, g2, filt, skip)


def _route(logits):
    lane_i = lax.broadcasted_iota(jnp.int32, logits.shape, 1)
    lane = lane_i.astype(F32)
    is_g = (lane_i >= N_EXPERTS) & (lane_i < N_EXPERTS + N_GROUPS)
    big = float(LANES)
    gl = jnp.where(is_g, logits, NEG)
    gmax = jnp.max(gl, axis=-1, keepdims=True)
    gidx = jnp.min(jnp.where(gl == gmax, lane - N_EXPERTS, big), axis=-1, keepdims=True)
    p_group = 1.0 / jnp.sum(jnp.where(is_g, jnp.exp(gl - gmax), 0.0), axis=-1, keepdims=True)
    sel = (lane_i < N_EXPERTS) & ((lane_i // EXPERTS_PER_GROUP) == gidx.astype(jnp.int32))
    el = jnp.where(sel, logits, NEG)
    ee = jnp.where(sel, jnp.exp(el - jnp.max(el, axis=-1, keepdims=True)), 0.0)
    pe = ee / jnp.sum(ee, axis=-1, keepdims=True)
    p1 = jnp.max(jnp.where(sel, pe, -1.0), axis=-1, keepdims=True)
    i1 = jnp.min(jnp.where(sel & (pe == p1), lane, big), axis=-1, keepdims=True)
    rest = sel & (lane != i1)
    p2 = jnp.max(jnp.where(rest, pe, -1.0), axis=-1, keepdims=True)
    i2 = jnp.min(jnp.where(rest & (pe == p2), lane, big), axis=-1, keepdims=True)
    den = p1 + p2
    cmb = jnp.where(lane == i1, p_group * (p1 / den), jnp.where(lane == i2, p_group * (p2 / den), 0.0))
    return jnp.where(lane_i == N_EXPERTS, gidx, cmb)


def _outproj_kernel(att_ref, hf_ref, hb_ref, gt_ref, hy_ref, x_ref, mod_ref, w_ref, g_ref, b_ref,
                    wr_ref, br_ref, x1_ref, u2_ref, cmb_ref):
    bpt = att_ref.shape[0]

    def rows(ref):
        return jnp.concatenate([ref[:, b * D_RNN:(b + 1) * D_RNN] for b in range(bpt)], axis=0)

    rg = (rows(hf_ref) + rows(hb_ref)) * jax.nn.gelu(rows(gt_ref).astype(F32))
    att = att_ref[...].reshape(bpt * att_ref.shape[1], D_ATT)
    mix = (jnp.dot(att, w_ref[0:D_ATT, :], preferred_element_type=F32)
           + jnp.dot(rg.astype(BF16), w_ref[D_ATT:D_ATT + D_RNN, :], preferred_element_type=F32)
           + jnp.dot(rows(hy_ref).astype(BF16), w_ref[D_ATT + D_RNN:, :], preferred_element_type=F32))
    m = mod_ref[0]
    x1 = _ln(ALPHA * x_ref[...] + m[2:3] * mix) * g_ref[...] + b_ref[...]
    u2 = _ln(x1) * (1.0 + m[4:5]) + m[3:4]
    u_hi = u2.astype(BF16)
    u_lo = (u2 - u_hi.astype(F32)).astype(BF16)
    wr = wr_ref[...]
    w_hi = wr.astype(BF16)
    w_lo = (wr - w_hi.astype(F32)).astype(BF16)
    hi = jnp.dot(u_hi, jnp.concatenate([w_hi, w_lo], axis=1), preferred_element_type=F32)
    lo = jnp.dot(u_lo, w_hi, preferred_element_type=F32)
    logits = hi[:, 0:LANES] + hi[:, LANES:] + lo + br_ref[...]
    x1_ref[...] = x1
    u2_ref[...] = u_hi
    cmb_ref[...] = _route(logits)


def _outproj(att, hf_t, hb_t, gt_t, hy_t, x2d, mod_l, w_bf, ln_g, ln_b, wr, br, *, nb, L, tm,
             cond_base, cond_per_batch):
    tpb, bpt = _tile_split(nb, L, tm, cond_per_batch)
    tm = L * bpt // tpb
    d = x2d.shape[1]
    cond = (lambda i: cond_base + i // tpb) if cond_per_batch else (lambda i: cond_base)
    t_spec = pl.BlockSpec((L // tpb, bpt * D_RNN), lambda i: (i % tpb, i // tpb))
    tok = lambda c: pl.BlockSpec((tm, c), lambda i: (i, 0))
    const = lambda s: pl.BlockSpec(s, lambda i: (0, 0))
    n = nb * L
    return pl.pallas_call(
        _outproj_kernel,
        grid=(nb * tpb // bpt,),
        in_specs=[pl.BlockSpec((bpt, L // tpb, D_ATT), lambda i: (i // tpb, i % tpb, 0)),
                  t_spec, t_spec, t_spec, t_spec, tok(d),
                  pl.BlockSpec((1, 6, d), lambda i: (cond(i), 0, 0)),
                  const((d, d)), const((1, d)), const((1, d)), const((d, LANES)), const((1, LANES))],
        out_specs=[tok(d), tok(d), tok(LANES)],
        out_shape=[jax.ShapeDtypeStruct((n, d), F32), jax.ShapeDtypeStruct((n, d), BF16),
                   jax.ShapeDtypeStruct((n, LANES), F32)],
        compiler_params=_cp("arbitrary"),
        name="outproj",
    )(att, hf_t, hb_t, gt_t, hy_t, x2d, mod_l, w_bf, ln_g[None], ln_b[None], wr, br)


def _moe_kernel(u_ref, cmb_ref, x1_ref, mod_ref, wgu_ref, wd_ref, g_ref, b_ref, o_ref, ys_ref, *, T, BLK):
    SL = T + N_GROUPS * BLK
    u = u_ref[...]
    cmb = cmb_ref[...]
    lane_f = lax.broadcasted_iota(jnp.int32, (T, LANES), 1).astype(F32)
    gid_col = jnp.sum(jnp.where(lane_f == float(N_EXPERTS), cmb, 0.0), axis=-1, keepdims=True)
    onehot = jnp.where(lane_f == gid_col, 1.0, 0.0)
    ri = lax.broadcasted_iota(jnp.int32, (T, T), 0)
    ci = lax.broadcasted_iota(jnp.int32, (T, T), 1)
    before = jnp.where(ci < ri, 1.0, 0.0).astype(BF16)
    rank_col = jnp.sum(onehot * jnp.dot(before, onehot.astype(BF16), preferred_element_type=F32),
                       axis=-1, keepdims=True)
    counts = jnp.sum(onehot, axis=0, keepdims=True)
    lane_row = lax.broadcasted_iota(jnp.int32, (1, LANES), 1)
    base = jnp.int32(0)
    bases = []
    base_row = jnp.zeros((1, LANES), jnp.int32)
    for c in range(N_GROUPS):
        n_c = jnp.sum(jnp.where(lane_row == c, counts, 0.0)).astype(jnp.int32)
        bases.append(base)
        base_row = jnp.where(lane_row == c, base, base_row)
        base = base + ((n_c + (BLK - 1)) // BLK) * BLK
    total = base
    dest_f = jnp.sum(onehot * base_row.astype(F32), axis=-1, keepdims=True) + rank_col
    dest_col = dest_f.astype(jnp.int32)
    dest_row = jnp.broadcast_to(dest_f, (T, LANES)).T[0:1, :].astype(jnp.int32)
    cmb_hi = cmb.astype(BF16)
    cmb_lo = (cmb - cmb_hi.astype(F32)).astype(BF16)
    blk_rows = lax.broadcasted_iota(jnp.int32, (BLK, T), 0)
    blk_lane = lax.broadcasted_iota(jnp.int32, (BLK, LANES), 1)

    def block(b, carry):
        r0 = b * BLK
        rows = pl.ds(pl.multiple_of(r0, BLK), BLK)

        @pl.when(r0 < total)
        def _():
            grp = ((r0 >= bases[1]).astype(jnp.int32) + (r0 >= bases[2]).astype(jnp.int32)
                   + (r0 >= bases[3]).astype(jnp.int32))
            p = jnp.where(blk_rows + r0 == dest_row, 1.0, 0.0).astype(BF16)
            xb = jnp.dot(p, u, preferred_element_type=F32).astype(BF16)
            wb = (jnp.dot(p, cmb_hi, preferred_element_type=F32)
                  + jnp.dot(p, cmb_lo, preferred_element_type=F32))
            gu = jnp.dot(xb, wgu_ref[grp], preferred_element_type=F32)
            hs = []
            for j in range(EXPERTS_PER_GROUP):
                gate = gu[:, 2 * j * D_EXPERT:(2 * j + 1) * D_EXPERT]
                up = gu[:, (2 * j + 1) * D_EXPERT:(2 * j + 2) * D_EXPERT]
                w = jnp.sum(jnp.where(blk_lane == grp * EXPERTS_PER_GROUP + j, wb, 0.0), axis=-1, keepdims=True)
                hs.append((gate * _sigmoid(gate) * up * w).astype(BF16))
            ys_ref[rows, :] = jnp.dot(jnp.concatenate(hs, axis=1), wd_ref[grp],
                                      preferred_element_type=F32).astype(BF16)

        @pl.when(r0 >= total)
        def _():
            ys_ref[rows, :] = jnp.zeros((BLK, u.shape[1]), BF16)

        return carry

    lax.fori_loop(0, SL // BLK, block, 0)
    p_t = jnp.where(lax.broadcasted_iota(jnp.int32, (T, SL), 1) == dest_col, 1.0, 0.0).astype(BF16)
    moe = jnp.dot(p_t, ys_ref[...], preferred_element_type=F32)
    m = mod_ref[0]
    o_ref[...] = _ln(ALPHA * x1_ref[...] + m[5:6] * moe) * g_ref[...] + b_ref[...]


def _group_gate_up(w_gate, w_up):
    _, d, f = w_gate.shape
    gu = jnp.concatenate([w_gate, w_up], axis=-1).astype(BF16).reshape(N_GROUPS, EXPERTS_PER_GROUP, d, 2 * f)
    return gu.transpose(0, 2, 1, 3).reshape(N_GROUPS, d, EXPERTS_PER_GROUP * 2 * f)


def _moe(u2, cmb, x1, mod_l, wgu, wd, ln_g, ln_b, *, tm, tiles_per_cond, cond_base, cond_per_batch):
    n, d = x1.shape
    cond = (lambda i: cond_base + i // tiles_per_cond) if cond_per_batch else (lambda i: cond_base)
    tok = lambda c: pl.BlockSpec((tm, c), lambda i: (i, 0))
    const = lambda s: pl.BlockSpec(s, lambda i: (0,) * len(s))
    resident = lambda s: pl.BlockSpec(s, lambda i: (0,) * len(s), pipeline_mode=pl.Buffered(1))
    return pl.pallas_call(
        functools.partial(_moe_kernel, T=tm, BLK=MOE_BLK),
        grid=(n // tm,),
        in_specs=[tok(d), tok(LANES), tok(d),
                  pl.BlockSpec((1, 6, d), lambda i: (cond(i), 0, 0)),
                  resident(wgu.shape), resident(wd.shape),
                  const((1, d)), const((1, d))],
        out_specs=tok(d),
        out_shape=jax.ShapeDtypeStruct((n, d), F32),
        scratch_shapes=[pltpu.VMEM((tm + N_GROUPS * MOE_BLK, d), BF16)],
        compiler_params=_cp("arbitrary"),
        name="moe",
    )(u2, cmb, x1, mod_l, wgu, wd, ln_g[None], ln_b[None])


def _stream_layer(x2d, mod_l, p, *, nb, L, tm, cond_base, cond_per_batch, ctx, cache=None, h0=None, prev_kv=None):
    kw = dict(nb=nb, L=L, tm=tm, cond_base=cond_base, cond_per_batch=cond_per_batch)
    q, k, v, xr_t, gt_t, hv_t, h1_t, h2_t = _inproj(x2d, mod_l, p["w_in"], kv_dtype=F32 if ctx else BF16,
                                                    prev_kv=prev_kv, **{**kw, "tm": INPROJ_TILE})
    if ctx:
        att = _ctx_attention(q, k, v)
    else:
        att = _nbr_attention(q, k, v, cache[0], cache[1], p["nbr_bias"])
    a_f, b_f, a_b, b_b = _rg_gates(xr_t, p["rg_conv_w"], p["rg_conv_b"], *p["rg_w"])
    hf_t, hb_t = _rg_scan(a_f, b_f, a_b, b_b, h0[0], h0[1])
    hy_t = _hyena(hv_t, h1_t, h2_t, p["hy_conv_w"], p["hy_conv_b"], p["hy_filt"][L], p["hy_skip"],
                  p["hy_tables"].get(L))
    x1, u2, cmb = _outproj(att, hf_t, hb_t, gt_t, hy_t, x2d, mod_l, p["w_out"], p["ln1_g"], p["ln1_b"],
                           p["router_w"], p["router_b"], **kw)
    tm_moe = min(MOE_TILE, nb * L)
    x2 = _moe(u2, cmb, x1, mod_l, p["moe_wgu"], p["moe_wd"], p["ln2_g"], p["ln2_b"], tm=tm_moe,
              tiles_per_cond=max(L // tm_moe, 1), cond_base=cond_base, cond_per_batch=cond_per_batch)
    return x2, k, v, hf_t, hb_t


def kernel(x_prompt, x_sample, cache_k, cache_v, state_rglru, c, c_ctx, w_mod, b_mod, w_in, attn_rpb, rg_conv_w, rg_conv_b, rg_wa, rg_ba, rg_wx, rg_bx, rg_lambda, hy_conv_w, hy_conv_b, hy_w1, hy_b1, hy_w2, hy_b2, hy_w3, hy_freq, hy_decay, hy_skip, w_out, ln1_g, ln1_b, router_wg, router_bg, router_we, router_be, moe_w_gate, moe_w_up, moe_w_down, ln2_g, ln2_b):
    nbc, Lc, d = x_prompt.shape
    nbl, Ll, _ = x_sample.shape
    depth = w_mod.shape[0]
    past = cache_k.shape[2]

    cond8 = jnp.zeros((8, d), F32).at[0].set(c_ctx).at[1:1 + nbl].set(c)
    mod = _modulation(cond8, w_mod, b_mod).reshape(depth, 8, 6, d)

    xp = x_prompt.reshape(nbc * Lc, d)
    xs = x_sample.reshape(nbl * Ll, d)
    kv, new_h = None, []
    hy_tables = {L: _two_level_tables(L) for L in {Lc, Ll} if _two_level(L)}
    for l in range(depth):
        pad = jnp.zeros((d, LANES - N_EXPERTS - N_GROUPS), F32)
        p = {
            "w_in": w_in[l].astype(BF16),
            "nbr_bias": _nbr_bias_tiles(attn_rpb[l]),
            "rg_conv_w": rg_conv_w[l], "rg_conv_b": rg_conv_b[l],
            "rg_w": _rg_weights(rg_wa[l], rg_ba[l], rg_wx[l], rg_bx[l], rg_lambda[l]),
            "hy_conv_w": hy_conv_w[l], "hy_conv_b": hy_conv_b[l], "hy_skip": hy_skip[l], "hy_tables": hy_tables,
            "hy_filt": {L: _hy_filters(L, hy_w1[l], hy_b1[l], hy_w2[l], hy_b2[l], hy_w3[l], hy_freq[l], hy_decay[l])
                        for L in sorted({Lc, Ll})},
            "w_out": w_out[l].astype(BF16), "ln1_g": ln1_g[l], "ln1_b": ln1_b[l],
            "router_w": jnp.concatenate([router_we[l], router_wg[l], pad], axis=1),
            "router_b": jnp.concatenate([router_be[l], router_bg[l], pad[0]])[None],
            "moe_wgu": _group_gate_up(moe_w_gate[l], moe_w_up[l]),
            "moe_wd": moe_w_down[l].astype(BF16).reshape(N_GROUPS, EXPERTS_PER_GROUP * D_EXPERT, d),
            "ln2_g": ln2_g[l], "ln2_b": ln2_b[l],
        }
        zeros_h = jnp.zeros((1, nbc * D_RNN), F32)
        xp, k, v, hf_t, hb_t = _stream_layer(
            xp, mod[l], p, nb=nbc, L=Lc, tm=ROW_CHUNK, cond_base=0, cond_per_batch=False, ctx=True,
            h0=(zeros_h, zeros_h), prev_kv=kv)
        kv = (k, v)
        new_h.append(jnp.stack([hf_t[Lc - 1].reshape(nbc, D_RNN), hb_t[0].reshape(nbc, D_RNN)], axis=1))

        cache = (cache_k[:, l].reshape(nbl, past, D_ATT), cache_v[:, l].reshape(nbl, past, D_ATT))
        h0 = (state_rglru[:, l, 0].reshape(1, nbl * D_RNN), state_rglru[:, l, 1].reshape(1, nbl * D_RNN))
        xs, _, _, _, _ = _stream_layer(
            xs, mod[l], p, nb=nbl, L=Ll, tm=ROW_CHUNK, cond_base=1, cond_per_batch=True, ctx=False,
            cache=cache, h0=h0)

    return (xp.reshape(nbc, Lc, d), xs.reshape(nbl, Ll, d),
            kv[0].reshape(nbc, depth, Lc, N_HEADS, HEAD_DIM), kv[1].reshape(nbc, depth, Lc, N_HEADS, HEAD_DIM),
            jnp.stack(new_h, axis=1))
```

```python
import functools
import math

import numpy as np
import jax
import jax.numpy as jnp
from jax import lax
from jax.experimental import pallas as pl
from jax.experimental.pallas import tpu as pltpu

F32 = jnp.float32
BF16 = jnp.bfloat16
HIGHEST = lax.Precision.HIGHEST

DEPTH = 2
D_MODEL = 1024
N_HEADS = 8
HEAD_DIM = 64
D_ATT = N_HEADS * HEAD_DIM
D_RNN = 256
D_HY = 256
GRID_W = 64
WIN_R = 8
WIN_C = 16
RG_CONV = 4
RG_C = 8.0
HY_CONV = 3
POS_EMB_DIM = 33
N_GROUPS = 4
EXPERTS_PER_GROUP = 4
N_EXPERTS = 16
D_EXPERT = 256
ALPHA = (2.0 * DEPTH) ** 0.25
LN_EPS = 1e-5
NEG = -1e30

Q_ROWS = 4
K_ROWS = Q_ROWS + WIN_R
LANES = 128
HALO = 8
ROW_CHUNK = 512
SCAN_LANES = 1024
BLOCK_BYTES = 1 << 20
MOD_COL_TILES = 4
FFT_MIN_LEN = 2048
FFT_KB = 4
INPROJ_TILE = 1024
MOE_TILE = 512
MOE_BLK = 144
STRIDE_PAD = 4
FFT_G = 16
FFT_N1 = 64
VMEM_LIMIT = 52 * 1024 * 1024


def _cp(*sem):
    return pltpu.CompilerParams(dimension_semantics=sem, vmem_limit_bytes=VMEM_LIMIT)


def _sigmoid(x):
    return 0.5 * jnp.tanh(0.5 * x) + 0.5


def _ln(x):
    mu = jnp.mean(x, axis=-1, keepdims=True)
    xc = x - mu
    var = jnp.mean(xc * xc, axis=-1, keepdims=True)
    return xc * lax.rsqrt(var + LN_EPS)


def _mod_kernel(c_ref, w_ref, b_ref, o_ref):
    c = c_ref[...]
    s = c * jax.nn.sigmoid(c)
    o_ref[0] = jnp.dot(s, w_ref[0], precision=HIGHEST, preferred_element_type=F32) + b_ref[0]


def _modulation(cond8, w_mod, b_mod):
    depth, d, n = w_mod.shape
    tn = n // MOD_COL_TILES
    return pl.pallas_call(
        _mod_kernel,
        grid=(depth, n // tn),
        in_specs=[pl.BlockSpec((8, d), lambda l, j: (0, 0)),
                  pl.BlockSpec((1, d, tn), lambda l, j: (l, 0, j)),
                  pl.BlockSpec((1, 1, tn), lambda l, j: (l, 0, j))],
        out_specs=pl.BlockSpec((1, 8, tn), lambda l, j: (l, 0, j)),
        out_shape=jax.ShapeDtypeStruct((depth, 8, n), F32),
        compiler_params=_cp("arbitrary", "arbitrary"),
        name="modulation",
    )(cond8, w_mod, b_mod.reshape(depth, 1, n))


def _inproj_kernel(x_ref, mod_ref, w_ref, *refs, n_prev):
    if n_prev:
        pk_ref, pv_ref = refs[:2]
        refs = refs[2:]
    q_ref, k_ref, v_ref, xr_ref, gt_ref, hv_ref, h1_ref, h2_ref = refs
    for i in range(n_prev):
        k_ref[:, i] = pk_ref[:, i]
        v_ref[:, i] = pv_ref[:, i]
    m = mod_ref[0]
    u = (_ln(x_ref[...]) * (1.0 + m[1:2]) + m[0:1]).astype(BF16)
    bpt, rows = q_ref.shape[0], q_ref.shape[1]

    def proj(lo, hi):
        return jnp.dot(u, w_ref[:, lo:hi], preferred_element_type=F32)

    o = 0
    q_ref[...] = (proj(o, o + D_ATT) * (HEAD_DIM ** -0.5)).astype(q_ref.dtype).reshape(q_ref.shape)
    o += D_ATT
    k_ref[:, n_prev] = proj(o, o + D_ATT).astype(k_ref.dtype).reshape(bpt, rows, D_ATT)
    o += D_ATT
    v_ref[:, n_prev] = proj(o, o + D_ATT).astype(v_ref.dtype).reshape(bpt, rows, D_ATT)
    o += D_ATT
    for ref in (xr_ref, gt_ref, hv_ref, h1_ref, h2_ref):
        y = proj(o, o + D_RNN).astype(ref.dtype)
        for b in range(bpt):
            ref[:, b * D_RNN:(b + 1) * D_RNN] = y[b * rows:(b + 1) * rows]
        o += D_RNN


def _tile_split(nb, L, tm, cond_per_batch):
    if tm <= L:
        return L // tm, 1
    bpt = 1 if cond_per_batch else min(nb, tm // L)
    while nb % bpt:
        bpt -= 1
    return 1, bpt


def _inproj(x2d, mod_l, w_bf, *, nb, L, tm, cond_base, cond_per_batch, kv_dtype, prev_kv=None):
    tpb, bpt = _tile_split(nb, L, tm, cond_per_batch)
    tm = L * bpt // tpb
    d = x2d.shape[1]
    d_in = w_bf.shape[1]
    n_prev = prev_kv[0].shape[1] if prev_kv else 0
    cond = (lambda i: cond_base + i // tpb) if cond_per_batch else (lambda i: cond_base)
    att_spec = pl.BlockSpec((bpt, L // tpb, D_ATT), lambda i: (i // tpb, i % tpb, 0))
    kv_spec = lambda n: pl.BlockSpec((bpt, n, L // tpb, D_ATT), lambda i: (i // tpb, 0, i % tpb, 0))
    t_spec = pl.BlockSpec((L // tpb, bpt * D_RNN), lambda i: (i % tpb, i // tpb))
    kv_shape = jax.ShapeDtypeStruct((nb, n_prev + 1, L, D_ATT), kv_dtype)
    t_shape = jax.ShapeDtypeStruct((L, nb * D_RNN), BF16)
    return pl.pallas_call(
        functools.partial(_inproj_kernel, n_prev=n_prev),
        grid=(nb * tpb // bpt,),
        in_specs=[pl.BlockSpec((tm, d), lambda i: (i, 0)),
                  pl.BlockSpec((1, 6, d), lambda i: (cond(i), 0, 0)),
                  pl.BlockSpec((d, d_in), lambda i: (0, 0))] + ([kv_spec(n_prev)] * 2 if n_prev else []),
        out_specs=[att_spec, kv_spec(n_prev + 1), kv_spec(n_prev + 1), t_spec, t_spec, t_spec, t_spec, t_spec],
        out_shape=[jax.ShapeDtypeStruct((nb, L, D_ATT), BF16), kv_shape, kv_shape,
                   t_shape, t_shape, t_shape, t_shape, t_shape],
        compiler_params=_cp("arbitrary"),
        name="inproj",
    )(x2d, mod_l, w_bf, *(prev_kv or ()))


def _nt(a, b):
    return lax.dot_general(a, b, (((1,), (1,)), ((), ())), preferred_element_type=F32)


def _ctx_attn_kernel(q_ref, k_ref, v_ref, o_ref):
    q = q_ref[0]
    k = k_ref[0, 0].astype(BF16)
    v = v_ref[0, 0].astype(BF16)
    outs = []
    for h in range(N_HEADS):
        sl = slice(h * HEAD_DIM, (h + 1) * HEAD_DIM)
        s = _nt(q[:, sl], k[:, sl])
        e = jnp.exp(s - jnp.max(s, axis=-1, keepdims=True))
        den = jnp.sum(e, axis=-1, keepdims=True)
        outs.append(jnp.dot(e.astype(BF16), v[:, sl], preferred_element_type=F32) / den)
    o_ref[0] = jnp.concatenate(outs, axis=-1).astype(o_ref.dtype)


def _ctx_attention(q, k, v):
    nb, L, _ = q.shape
    last = k.shape[1] - 1
    spec = pl.BlockSpec((1, L, D_ATT), lambda b: (b, 0, 0))
    kv_spec = pl.BlockSpec((1, 1, L, D_ATT), lambda b: (b, last, 0, 0))
    return pl.pallas_call(
        _ctx_attn_kernel,
        grid=(nb,),
        in_specs=[spec, kv_spec, kv_spec],
        out_specs=spec,
        out_shape=jax.ShapeDtypeStruct((nb, L, D_ATT), BF16),
        compiler_params=_cp("arbitrary"),
        name="ctx_attn",
    )(q, k, v)


N_DR = 2 * WIN_R - 1
PAIR_LEFT = N_DR - 1
PAIR_RIGHT = 2 * N_DR - 1
PAIR_NONE = 3 * N_DR - 1


def _nbr_bias_tiles(rpb):
    cols = np.arange(GRID_W)
    c_start = np.clip(cols - WIN_C // 2, 0, GRID_W - WIN_C)
    col_mask = (cols[None, :] >= c_start[:, None]) & (cols[None, :] < c_start[:, None] + WIN_C)
    dc_idx = np.clip(cols[None, :] - cols[:, None], -(WIN_C - 1), WIN_C - 1) + (WIN_C - 1)
    sel = (dc_idx[None] == np.arange(2 * WIN_C - 1)[:, None, None]).astype(np.float32)
    colb = jnp.einsum('hab,bqk->haqk', rpb.astype(F32), jnp.asarray(sel), precision=HIGHEST)
    colb = jnp.where(col_mask[None, None], colb, NEG)
    neg = jnp.full_like(colb, NEG)
    both = jnp.concatenate([colb[:, :-1], colb[:, 1:]], axis=-1)
    left_masked = jnp.concatenate([neg, colb], axis=-1)
    right_masked = jnp.concatenate([colb, neg], axis=-1)
    none = jnp.concatenate([neg[:, :1], neg[:, :1]], axis=-1)
    return jnp.concatenate([both, left_masked, right_masked, none], axis=1)


def _nbr_attn_kernel(q_ref, k_ref, v_ref, kc_ref, vc_ref, bias_ref, o_ref, *, rows):
    g = pl.program_id(1)
    r0 = jnp.clip(Q_ROWS * g - WIN_R // 2, 0, rows - K_ROWS)
    start = pl.multiple_of(r0 * GRID_W, Q_ROWS * GRID_W)
    nk = K_ROWS * GRID_W
    kw = k_ref[0, 0, pl.ds(start, nk), :]
    vw = v_ref[0, 0, pl.ds(start, nk), :]
    kc = kc_ref[0].astype(BF16)
    vc = vc_ref[0].astype(BF16)
    q = q_ref[0]
    tile = []
    for j in range(Q_ROWS):
        r = Q_ROWS * g + j
        lo = jnp.clip(r - WIN_R // 2, 0, rows - WIN_R) - r + (WIN_R - 1)
        tile.append([])
        for p in range(K_ROWS // 2):
            d0 = r0 + 2 * p - r + (WIN_R - 1)
            lv = (d0 >= lo) & (d0 < lo + WIN_R)
            rv = (d0 + 1 >= lo) & (d0 + 1 < lo + WIN_R)
            idx = jnp.where(lv & rv, d0, jnp.where(rv, PAIR_LEFT + d0 + 1, jnp.where(lv, PAIR_RIGHT + d0, PAIR_NONE)))
            tile[j].append(jnp.clip(idx, 0, PAIR_NONE))
    outs = []
    for h in range(N_HEADS):
        sl = slice(h * HEAD_DIM, (h + 1) * HEAD_DIM)
        qh = q[:, sl]
        bias = jnp.concatenate(
            [jnp.concatenate([bias_ref[h, tile[j][p]] for p in range(K_ROWS // 2)], axis=-1)
             for j in range(Q_ROWS)], axis=0)
        s1 = _nt(qh, kw[:, sl]) + bias
        s2 = _nt(qh, kc[:, sl])
        m = jnp.maximum(jnp.max(s1, axis=-1, keepdims=True), jnp.max(s2, axis=-1, keepdims=True))
        e1 = jnp.exp(s1 - m)
        e2 = jnp.exp(s2 - m)
        den = jnp.sum(e1, axis=-1, keepdims=True) + jnp.sum(e2, axis=-1, keepdims=True)
        o = (jnp.dot(e1.astype(BF16), vw[:, sl], preferred_element_type=F32)
             + jnp.dot(e2.astype(BF16), vc[:, sl], preferred_element_type=F32))
        outs.append(o / den)
    o_ref[0] = jnp.concatenate(outs, axis=-1).astype(o_ref.dtype)


def _nbr_attention(q, k, v, kc, vc, bias):
    nb, L, _ = q.shape
    rows = L // GRID_W
    ng = rows // Q_ROWS
    past = kc.shape[1]
    tq = Q_ROWS * GRID_W
    full = pl.BlockSpec((1, 1, L, D_ATT), lambda b, g: (b, 0, 0, 0))
    ctx = pl.BlockSpec((1, past, D_ATT), lambda b, g: (b, 0, 0))
    qspec = pl.BlockSpec((1, tq, D_ATT), lambda b, g: (b, g, 0))
    bspec = pl.BlockSpec(bias.shape, lambda b, g: (0, 0, 0, 0), pipeline_mode=pl.Buffered(1))
    return pl.pallas_call(
        functools.partial(_nbr_attn_kernel, rows=rows),
        grid=(nb, ng),
        in_specs=[qspec, full, full, ctx, ctx, bspec],
        out_specs=qspec,
        out_shape=jax.ShapeDtypeStruct((nb, L, D_ATT), BF16),
        compiler_params=_cp("arbitrary", "arbitrary"),
        name="nbr_attn",
    )(q, k, v, kc, vc, bias)


def _fill_padded(xp_ref, src_ref, L):
    z = jnp.zeros((HALO, xp_ref.shape[1]), F32)
    xp_ref[0:HALO, :] = z
    xp_ref[HALO:HALO + L, :] = src_ref[...].astype(F32)
    xp_ref[HALO + L:2 * HALO + L, :] = z


def _rg_gates_kernel(x_ref, cw_ref, cb_ref, w_ref, b_ref, lam_ref,
                     af_ref, bf_ref, ab_ref, bb_ref, xp_ref, *, L, tc, m):
    nh = cw_ref.shape[0]
    for sc in range(m):
        half = (pl.program_id(0) * m + sc) % nh
        cols = slice(sc * LANES, (sc + 1) * LANES)
        _fill_padded(xp_ref, x_ref.at[:, cols], L)
        sp = jax.nn.softplus(-lam_ref[half])
        cw = cw_ref[half]
        for t0 in range(0, L, tc):
            xc = cb_ref[half]
            for k in range(RG_CONV):
                xc = xc + cw[k:k + 1, :] * xp_ref[pl.ds(HALO + t0 + k - RG_CONV // 2, tc), :]
            z = jnp.dot(xc.astype(BF16), w_ref[half], preferred_element_type=F32) + b_ref[half]
            for e, (a_ref, o_ref) in enumerate(((af_ref, bf_ref), (ab_ref, bb_ref))):
                r = _sigmoid(z[:, e * LANES:(e + 1) * LANES])
                i = _sigmoid(z[:, (2 + e) * LANES:(3 + e) * LANES])
                log_a = (-RG_C) * r * sp[:, e * LANES:(e + 1) * LANES]
                a = jnp.exp(log_a)
                one_minus_a2 = -jnp.tanh(log_a) * (a * a + 1.0)
                a_ref[pl.ds(t0, tc), cols] = a
                o_ref[pl.ds(t0, tc), cols] = jnp.sqrt(one_minus_a2) * i * xc


def _blockdiag2(w2):
    z = jnp.zeros_like(w2[0])
    return jnp.concatenate([jnp.concatenate([w2[0], z], 1), jnp.concatenate([z, w2[1]], 1)], 0)


def _rg_weights(wa, ba, wx, bx, lam):
    ws, bs, ls = [], [], []
    for hf in range(D_RNN // LANES):
        blk = slice(2 * hf, 2 * hf + 2)
        ch = slice(hf * LANES, (hf + 1) * LANES)
        ws.append(jnp.concatenate([_blockdiag2(wa[0, blk]), _blockdiag2(wa[1, blk]),
                                   _blockdiag2(wx[0, blk]), _blockdiag2(wx[1, blk])], axis=1))
        bs.append(jnp.concatenate([ba[0, ch], ba[1, ch], bx[0, ch], bx[1, ch]])[None])
        ls.append(jnp.concatenate([lam[0, ch], lam[1, ch]])[None])
    return jnp.stack(ws).astype(BF16), jnp.stack(bs), jnp.stack(ls)


def _lane_chunks(L, cols):
    m = max(1, min(8, BLOCK_BYTES // (L * LANES * 4)))
    while cols % (m * LANES):
        m //= 2
    return m


def _rg_gates(xr_t, conv_w, conv_b, w_bd, b_cat, lam_cat):
    L, cols = xr_t.shape
    nh = D_RNN // LANES
    tc = min(L, ROW_CHUNK)
    m = _lane_chunks(L, cols)
    col = pl.BlockSpec((L, m * LANES), lambda c: (0, c))
    const = lambda a: pl.BlockSpec(a.shape, lambda c: (0,) * a.ndim)
    shape = jax.ShapeDtypeStruct((L, cols), F32)
    cw = conv_w.reshape(RG_CONV, nh, LANES).transpose(1, 0, 2)
    cb = conv_b.reshape(nh, 1, LANES)
    return pl.pallas_call(
        functools.partial(_rg_gates_kernel, L=L, tc=tc, m=m),
        grid=(cols // (m * LANES),),
        in_specs=[col, const(cw), const(cb), const(w_bd), const(b_cat), const(lam_cat)],
        out_specs=[col, col, col, col],
        out_shape=[shape, shape, shape, shape],
        scratch_shapes=[pltpu.VMEM((L + 2 * HALO, LANES), F32)],
        compiler_params=_cp("arbitrary"),
        name="rg_gates",
    )(xr_t, cw, cb, w_bd, b_cat, lam_cat)


def _rg_scan_kernel(af_ref, bf_ref, ab_ref, bb_ref, h0f_ref, h0b_ref, hf_ref, hb_ref, cf_ref, cb_ref, *, tc):
    @pl.when(pl.program_id(1) == 0)
    def _():
        cf_ref[...] = h0f_ref[...]
        cb_ref[...] = h0b_ref[...]

    def body(t, carry):
        hf, hb = carry
        hf = af_ref[pl.ds(t, 1), :] * hf + bf_ref[pl.ds(t, 1), :]
        hf_ref[pl.ds(t, 1), :] = hf
        tb = tc - 1 - t
        hb = ab_ref[pl.ds(tb, 1), :] * hb + bb_ref[pl.ds(tb, 1), :]
        hb_ref[pl.ds(tb, 1), :] = hb
        return hf, hb

    hf, hb = lax.fori_loop(0, tc, body, (cf_ref[...], cb_ref[...]), unroll=8)
    cf_ref[...] = hf
    cb_ref[...] = hb


def _rg_scan(a_f, b_f, a_b, b_b, h0f, h0b):
    L, cols = a_f.shape
    cw = min(cols, SCAN_LANES)
    tc = min(L, ROW_CHUNK)
    nt = L // tc
    fwd = pl.BlockSpec((tc, cw), lambda c, i: (i, c))
    bwd = pl.BlockSpec((tc, cw), lambda c, i: (nt - 1 - i, c))
    row = pl.BlockSpec((1, cw), lambda c, i: (0, c))
    shape = jax.ShapeDtypeStruct((L, cols), F32)
    return pl.pallas_call(
        functools.partial(_rg_scan_kernel, tc=tc),
        grid=(cols // cw, nt),
        in_specs=[fwd, fwd, bwd, bwd, row, row],
        out_specs=[fwd, bwd],
        out_shape=[shape, shape],
        scratch_shapes=[pltpu.VMEM((1, cw), F32), pltpu.VMEM((1, cw), F32)],
        compiler_params=_cp("arbitrary", "arbitrary"),
        name="rg_scan",
    )(a_f, b_f, a_b, b_b, h0f, h0b)


def _hy_filter_kernel(feat_ref, w1_ref, b1_ref, w2_ref, b2_ref, w3_ref, fr_ref, dec_ref, o_ref, *, tm):
    feats = feat_ref[...]
    dot = functools.partial(jnp.dot, precision=HIGHEST, preferred_element_type=F32)
    h = jnp.sin(fr_ref[0:1, :] * (dot(feats, w1_ref[...]) + b1_ref[...]))
    h = jnp.sin(fr_ref[1:2, :] * (dot(h, w2_ref[...]) + b2_ref[...]))
    h = dot(h, w3_ref[...])
    filt = h * jnp.exp(-feats[:, 0:1] * jnp.abs(dec_ref[...]))
    row = pl.program_id(0) * tm + lax.broadcasted_iota(jnp.int32, filt.shape, 0)
    lane = lax.broadcasted_iota(jnp.int32, filt.shape, 1)
    o_ref[...] = jnp.where((row == 0) & (lane >= 2 * D_HY), 0.0, filt)


def _hy_features(L):
    pos = np.arange(L, dtype=np.float64)
    t = np.linspace(0.0, 1.0, L)
    bands = (POS_EMB_DIM - 1) // 2
    f = np.linspace(1e-4, bands - 1, bands)
    ang = (2.0 * math.pi / L) * pos[:, None] * f[None, :]
    feats = np.concatenate([t[:, None], np.cos(ang), -np.sin(ang)], axis=-1)
    out = np.zeros((L, LANES), np.float32)
    out[:, :POS_EMB_DIM] = feats
    return out


def _two_level(L):
    return L % (FFT_N1 * 8) == 0 and L >= FFT_MIN_LEN


def _digit_swap(x, L):
    n2 = 2 * L // FFT_N1
    return x.reshape(L // n2, n2, -1).transpose(1, 0, 2).reshape(L, -1)


def _hy_filters(L, w1, b1, w2, b2, w3, freq, decay):
    tm = min(L, ROW_CHUNK)
    hid = w2.shape[0]
    n = w3.shape[1]
    w1p = jnp.zeros((LANES, hid), F32).at[:POS_EMB_DIM].set(w1)
    feats = _hy_features(L)
    if _two_level(L):
        feats = _digit_swap(feats, L)
    c = lambda s: pl.BlockSpec(s, lambda i: (0, 0))
    return pl.pallas_call(
        functools.partial(_hy_filter_kernel, tm=tm),
        grid=(L // tm,),
        in_specs=[pl.BlockSpec((tm, LANES), lambda i: (i, 0)),
                  c((LANES, hid)), c((1, hid)), c((hid, hid)), c((1, hid)), c((hid, n)), c((2, hid)), c((1, n))],
        out_specs=pl.BlockSpec((tm, n), lambda i: (i, 0)),
        out_shape=jax.ShapeDtypeStruct((L, n), F32),
        compiler_params=_cp("arbitrary"),
        name="hy_filter",
    )(jnp.asarray(feats), w1p, b1[None], w2, b2[None], w3, freq, decay.reshape(1, n))


def _conv3_kernel(v_ref, a_ref, b_ref, w_ref, cb_ref, ov_ref, oa_ref, ob_ref, xp_ref, xs_ref,
                  *, L, tc, swap, n2, m):
    h1 = L // n2
    pitch = n2 + STRIDE_PAD
    nh = w_ref.shape[0]
    for sc in range(m):
        half = (pl.program_id(0) * m + sc) % nh
        cols = slice(sc * LANES, (sc + 1) * LANES)
        for p, (src, dst) in enumerate(((v_ref, ov_ref), (a_ref, oa_ref), (b_ref, ob_ref))):
            _fill_padded(xp_ref, src.at[:, cols], L)
            w = w_ref[half, p]
            cb = cb_ref[half, p]
            if swap[p]:
                for i1 in range(h1):
                    xs_ref[pl.ds(i1 * pitch, n2 + HY_CONV - 1), :] = (
                        xp_ref[pl.ds(HALO + i1 * n2 - HY_CONV // 2, n2 + HY_CONV - 1), :])

                def body(i2, carry, dst=dst, w=w, cb=cb):
                    y = cb
                    for k in range(HY_CONV):
                        y = y + w[k:k + 1, :] * xs_ref[pl.ds(i2 + k, h1, stride=pitch), :]
                    dst[i2] = y.astype(dst.dtype)
                    return carry

                lax.fori_loop(0, n2, body, 0)
            else:
                for t0 in range(0, L, tc):
                    y = cb
                    for k in range(HY_CONV):
                        y = y + w[k:k + 1, :] * xp_ref[pl.ds(HALO + t0 + k - HY_CONV // 2, tc), :]
                    dst[pl.ds(t0, tc), cols] = y


def _hy_conv3(v_t, a_t, b_t, conv_w, conv_b, swap=(False, False, False)):
    L, cols = v_t.shape
    nh = D_HY // LANES
    tc = min(L, ROW_CHUNK)
    n2 = 2 * L // FFT_N1
    m = 1 if any(swap) else _lane_chunks(L, cols)
    col = pl.BlockSpec((L, m * LANES), lambda c: (0, c))
    col3 = pl.BlockSpec((n2, L // n2, LANES), lambda c: (0, 0, c))
    shape = jax.ShapeDtypeStruct((L, cols), F32)
    shape3 = jax.ShapeDtypeStruct((n2, L // n2, cols), F32)
    w = conv_w.reshape(HY_CONV, 3, nh, LANES).transpose(2, 1, 0, 3)
    cb = conv_b.reshape(3, nh, 1, LANES).transpose(1, 0, 2, 3)
    const = lambda a: pl.BlockSpec(a.shape, lambda c: (0,) * a.ndim)
    return pl.pallas_call(
        functools.partial(_conv3_kernel, L=L, tc=tc, swap=swap, n2=n2, m=m),
        grid=(cols // (m * LANES),),
        in_specs=[col, col, col, const(w), const(cb)],
        out_specs=[col3 if s else col for s in swap],
        out_shape=[shape, shape, shape] if not any(swap) else
                  [jax.ShapeDtypeStruct(shape3.shape, BF16 if i == 0 else F32) if s else shape
                   for i, s in enumerate(swap)],
        scratch_shapes=[pltpu.VMEM((L + 2 * HALO, LANES), F32),
                        pltpu.VMEM(((L // n2) * (n2 + STRIDE_PAD) + HALO, LANES), F32)],
        compiler_params=_cp("arbitrary"),
        name="hy_conv3",
    )(v_t, a_t, b_t, w, cb)


def _phase(p, n2):
    ang = 2.0 * np.pi * (np.asarray(p, np.int64) % n2) / n2
    return np.cos(ang), np.sin(ang)


def _dft_direct_tables(L):
    k = np.arange(L)[:, None]
    t = np.arange(L)[None, :]
    c, s = _phase((2 * k + 1) * t, 4 * L)
    fwd = np.concatenate([c, -s], axis=0)
    inv = (1.0 / L) * np.concatenate([c.T, -s.T], axis=1)
    return fwd.astype(np.float32), inv.astype(np.float32)


def _dft_two_level_tables(L, n1):
    N = 2 * L
    n2 = N // n1
    h1 = n1 // 2
    i2 = np.arange(n2)[:, None, None]
    k1 = np.arange(n1)[None, :, None]
    i1 = np.arange(h1)[None, None, :]
    c, s = _phase(2 * n2 * i1 * k1 + 2 * i2 * k1 + n2 * i1 + i2, 2 * N)
    t1 = np.concatenate([c, -s], axis=1)
    ct, st = c.transpose(0, 2, 1), s.transpose(0, 2, 1)
    t1i = (2.0 / N) * np.concatenate([ct, -st], axis=2)
    kk = np.arange(n2 // 2)[:, None]
    nn = np.arange(n2)[None, :]
    c2, s2 = _phase(nn * kk, n2)
    t2 = np.block([[c2, s2], [-s2, c2]])
    t2i = np.block([[c2.T, -s2.T], [s2.T, c2.T]])
    return tuple(x.astype(np.float32) for x in (t1, t2, t2i, t1i))


def _filter_spectrum(fre, fim, order, skip):
    fw = slice(order * D_HY, (order + 1) * D_HY)
    bw = slice((2 + order) * D_HY, (3 + order) * D_HY)
    return fre[:, fw] + fre[:, bw] + skip, fim[:, fw] - fim[:, bw]


def _spectral_product(xr, xi, fr, fi, nbc):
    yr, yi = [], []
    for b in range(nbc):
        sl = slice(b * D_HY, (b + 1) * D_HY)
        yr.append(xr[:, sl] * fr - xi[:, sl] * fi)
        yi.append(xr[:, sl] * fi + xi[:, sl] * fr)
    return jnp.concatenate([jnp.concatenate(yr, axis=1), jnp.concatenate(yi, axis=1)], axis=0)


def _matmul_kernel(a_ref, b_ref, o_ref):
    o_ref[...] = jnp.dot(a_ref[...], b_ref[...], precision=HIGHEST, preferred_element_type=F32)


def _hy_direct_kernel(z_ref, g_ref, f_ref, skip_ref, t_ref, ti_ref, o_ref, *, L, nbc, order):
    x = jnp.dot(t_ref[...], z_ref[...].astype(BF16), preferred_element_type=F32)
    fr, fi = _filter_spectrum(f_ref[0:L, :], f_ref[L:2 * L, :], order, skip_ref[0])
    y = _spectral_product(x[:L], x[L:], fr, fi, nbc).astype(BF16)
    conv = jnp.dot(ti_ref[...], y, preferred_element_type=F32)
    o_ref[...] = (g_ref[...] * conv).astype(o_ref.dtype)


def _hyena_direct(z, g1, g2, filt, skip):
    L, cols = z.shape
    fwd, inv = _dft_direct_tables(L)
    n = filt.shape[1]
    spec = pl.pallas_call(
        _matmul_kernel,
        out_shape=jax.ShapeDtypeStruct((2 * L, n), F32),
        name="hy_fspec",
    )(jnp.asarray(fwd), filt)
    tn = min(cols, 4 * D_HY)
    nbc = tn // D_HY
    col = pl.BlockSpec((L, tn), lambda c: (0, c))
    const = lambda s: pl.BlockSpec(s, lambda c: (0, 0))
    t_bf, ti_bf = jnp.asarray(fwd).astype(BF16), jnp.asarray(inv).astype(BF16)
    for order, g in enumerate((g1, g2)):
        z = pl.pallas_call(
            functools.partial(_hy_direct_kernel, L=L, nbc=nbc, order=order),
            grid=(cols // tn,),
            in_specs=[col, col, const((2 * L, n)), pl.BlockSpec((1, 1, D_HY), lambda c: (order, 0, 0)),
                      const((2 * L, L)), const((L, 2 * L))],
            out_specs=col,
            out_shape=jax.ShapeDtypeStruct((L, cols), F32 if order == 0 else BF16),
            compiler_params=_cp("arbitrary"),
            name="hy_direct",
        )(z, g, spec, skip[:, None, :], t_bf, ti_bf)
    return z


def _kron_tables(t1, t1i, G):
    n2, two_n1, h1 = t1.shape
    ng = n2 // G
    eye = np.eye(two_n1, dtype=np.float32)
    rep_rows = jnp.asarray(np.repeat(eye, G, axis=0))
    rep_cols = jnp.asarray(np.repeat(eye, G, axis=1))
    c_fwd = t1.reshape(ng, G, two_n1, h1).transpose(0, 2, 1, 3).reshape(ng, two_n1, G * h1)
    c_swapped = t1i.reshape(ng, G * h1, two_n1)
    c_natural = t1i.reshape(ng, G, h1, two_n1).transpose(0, 2, 1, 3).reshape(ng, h1 * G, two_n1)

    def expand(compact, rep, left, row_key, lane_key):
        r, c = (rep.shape[0], compact.shape[2]) if left else (compact.shape[1], rep.shape[1])
        return pl.pallas_call(
            functools.partial(_kron_expand_kernel, left=left, row_key=row_key, lane_key=lane_key),
            grid=(ng,),
            in_specs=[pl.BlockSpec((1,) + compact.shape[1:], lambda i: (i, 0, 0)),
                      pl.BlockSpec(rep.shape, lambda i: (0, 0))],
            out_specs=pl.BlockSpec((1, r, c), lambda i: (i, 0, 0)),
            out_shape=jax.ShapeDtypeStruct((ng, r, c), BF16),
            compiler_params=_cp("arbitrary"),
            name="hy_kron",
        )(jnp.asarray(compact), rep)

    fwd = expand(c_fwd, rep_rows, True, lambda r: r % G, lambda c: c // h1)
    inv_swapped = expand(c_swapped, rep_cols, False, lambda r: r // h1, lambda c: c % G)
    inv_natural = expand(c_natural, rep_cols, False, lambda r: r % G, lambda c: c % G)
    return fwd, inv_swapped, inv_natural


def _kron_expand_kernel(c_ref, rep_ref, o_ref, *, left, row_key, lane_key):
    comp = c_ref[0].astype(BF16)
    rep = rep_ref[...].astype(BF16)
    full = (jnp.dot(rep, comp, preferred_element_type=F32) if left
            else jnp.dot(comp, rep, preferred_element_type=F32))
    rows = lax.broadcasted_iota(jnp.int32, full.shape, 0)
    lanes = lax.broadcasted_iota(jnp.int32, full.shape, 1)
    o_ref[0] = jnp.where(row_key(rows) == lane_key(lanes), full, 0.0).astype(BF16)


def _hy_stage1_kernel(x_ref, t_ref, re_ref, im_ref, *, G, n1, ct):
    h1, cols = x_ref.shape[1], x_ref.shape[2]
    x = x_ref[...].reshape(G * h1, cols).astype(BF16)
    for c0 in range(0, cols, ct):
        a = jnp.dot(t_ref[0], x[:, c0:c0 + ct], preferred_element_type=F32)
        re_ref[:, :, c0:c0 + ct] = a[:n1 * G].reshape(n1, G, ct).astype(BF16)
        im_ref[:, :, c0:c0 + ct] = a[n1 * G:].reshape(n1, G, ct).astype(BF16)


def _hy_stage1(x3, t_fwd, *, G):
    n2, h1, cols = x3.shape
    n1 = t_fwd.shape[1] // (2 * G)
    ospec = pl.BlockSpec((n1, G, cols), lambda j: (0, j, 0))
    oshape = jax.ShapeDtypeStruct((n1, n2, cols), BF16)
    return pl.pallas_call(
        functools.partial(_hy_stage1_kernel, G=G, n1=n1, ct=D_HY),
        grid=(n2 // G,),
        in_specs=[pl.BlockSpec((G, h1, cols), lambda j: (j, 0, 0)),
                  pl.BlockSpec((1,) + t_fwd.shape[1:], lambda j: (j, 0, 0))],
        out_specs=[ospec, ospec],
        out_shape=[oshape, oshape],
        compiler_params=_cp("arbitrary"),
        name="hy_stage1",
    )(x3, t_fwd)


def _hy_fspec2_kernel(re_ref, im_ref, skip_ref, t2_ref, fre_ref, fim_ref, *, kb, k2):
    for j in range(kb):
        a2 = jnp.concatenate([re_ref[j], im_ref[j]], axis=0)
        x = jnp.dot(t2_ref[...], a2, preferred_element_type=F32)
        for order in range(2):
            fr, fi = _filter_spectrum(x[:k2], x[k2:], order, skip_ref[order])
            fre_ref[order, j] = fr
            fim_ref[order, j] = fi


def _hy_stage2_kernel(re_ref, im_ref, fre_ref, fim_ref, t2_ref, t2i_ref, ore_ref, oim_ref, *, kb, k2, n2, nbc):
    for j in range(kb):
        a2 = jnp.concatenate([re_ref[j], im_ref[j]], axis=0)
        x = jnp.dot(t2_ref[...], a2, preferred_element_type=F32)
        fr, fi = fre_ref[0, j], fim_ref[0, j]
        y = _spectral_product(x[:k2], x[k2:], fr, fi, nbc).astype(BF16)
        a = jnp.dot(t2i_ref[...], y, preferred_element_type=F32)
        ore_ref[j] = a[:n2].astype(BF16)
        oim_ref[j] = a[n2:].astype(BF16)


def _hy_stage3_kernel(re_ref, im_ref, t_ref, g_ref, o_ref, *, G, n1, ct):
    cols = re_ref.shape[2]
    for c0 in range(0, cols, ct):
        sl = slice(c0, c0 + ct)
        a2 = jnp.concatenate([re_ref[:, :, sl].reshape(n1 * G, ct), im_ref[:, :, sl].reshape(n1 * G, ct)], axis=0)
        conv = jnp.dot(t_ref[0], a2, preferred_element_type=F32)
        g = g_ref[:, :, sl]
        o_ref[:, :, sl] = (g * conv.reshape(g.shape)).astype(o_ref.dtype)


def _hy_stage31_kernel(re_ref, im_ref, ti_ref, g_ref, tf_ref, ore_ref, oim_ref, *, G, n1, ct):
    cols = re_ref.shape[2]
    for c0 in range(0, cols, ct):
        sl = slice(c0, c0 + ct)
        a2 = jnp.concatenate([re_ref[:, :, sl].reshape(n1 * G, ct), im_ref[:, :, sl].reshape(n1 * G, ct)], axis=0)
        conv = jnp.dot(ti_ref[0], a2, preferred_element_type=F32)
        z = g_ref[:, :, sl].reshape(conv.shape) * conv
        a = jnp.dot(tf_ref[0], z.astype(BF16), preferred_element_type=F32)
        ore_ref[:, :, sl] = a[:n1 * G].reshape(n1, G, ct).astype(BF16)
        oim_ref[:, :, sl] = a[n1 * G:].reshape(n1, G, ct).astype(BF16)


def _two_level_tables(L, *, n1=FFT_N1, G=FFT_G):
    t1, t2, t2i, t1i = _dft_two_level_tables(L, n1)
    return (jnp.asarray(t2).astype(BF16), jnp.asarray(t2i).astype(BF16)) + _kron_tables(t1, t1i, G)


def _hyena_two_level(z3, g1_3, g2, filt, skip, tables, *, n1=FFT_N1, G=FFT_G, kb=FFT_KB):
    n2, h1, cols = z3.shape
    L = n2 * h1
    k2 = n2 // 2
    nbc = cols // D_HY
    t2_bf, t2i_bf, t_fwd, t_inv_swapped, t_inv_natural = tables
    nf = filt.shape[1]

    f_re, f_im = _hy_stage1(filt.reshape(n2, h1, nf), t_fwd, G=G)
    blk = lambda c: pl.BlockSpec((kb, n2, c), lambda i: (i, 0, 0))
    const = lambda s: pl.BlockSpec(s, lambda i: (0, 0))
    fshape = jax.ShapeDtypeStruct((2, n1, k2, D_HY), F32)
    fspec = pl.BlockSpec((2, kb, k2, D_HY), lambda i: (0, i, 0, 0))
    ospec = lambda order: pl.BlockSpec((1, kb, k2, D_HY), lambda i: (order, i, 0, 0))
    fre, fim = pl.pallas_call(
        functools.partial(_hy_fspec2_kernel, kb=kb, k2=k2),
        grid=(n1 // kb,),
        in_specs=[blk(nf), blk(nf), pl.BlockSpec((2, 1, D_HY), lambda i: (0, 0, 0)), const((n2, 2 * n2))],
        out_specs=[fspec, fspec],
        out_shape=[fshape, fshape],
        compiler_params=_cp("arbitrary"),
        name="hy_fspec2",
    )(f_re, f_im, skip[:, None, :], t2_bf)

    ashape = jax.ShapeDtypeStruct((n1, n2, cols), BF16)
    grp = pl.BlockSpec((n1, G, cols), lambda j: (0, j, 0))
    tspec = lambda t: pl.BlockSpec((1,) + t.shape[1:], lambda j: (j, 0, 0))
    a_re, a_im = _hy_stage1(z3, t_fwd, G=G)
    for order in range(2):
        b_re, b_im = pl.pallas_call(
            functools.partial(_hy_stage2_kernel, kb=kb, k2=k2, n2=n2, nbc=nbc),
            grid=(n1 // kb,),
            in_specs=[blk(cols), blk(cols), ospec(order), ospec(order), const((n2, 2 * n2)), const((2 * n2, n2))],
            out_specs=[blk(cols), blk(cols)],
            out_shape=[ashape, ashape],
            compiler_params=_cp("arbitrary"),
            name="hy_stage2",
        )(a_re, a_im, fre, fim, t2_bf, t2i_bf)
        if order == 0:
            a_re, a_im = pl.pallas_call(
                functools.partial(_hy_stage31_kernel, G=G, n1=n1, ct=D_HY),
                grid=(n2 // G,),
                in_specs=[grp, grp, tspec(t_inv_swapped), pl.BlockSpec((G, h1, cols), lambda j: (j, 0, 0)),
                          tspec(t_fwd)],
                out_specs=[grp, grp],
                out_shape=[ashape, ashape],
                compiler_params=_cp("arbitrary"),
                name="hy_stage31",
            )(b_re, b_im, t_inv_swapped, g1_3, t_fwd)
        else:
            io = pl.BlockSpec((h1, G, cols), lambda j: (0, j, 0))
            out = pl.pallas_call(
                functools.partial(_hy_stage3_kernel, G=G, n1=n1, ct=D_HY),
                grid=(n2 // G,),
                in_specs=[grp, grp, tspec(t_inv_natural), io],
                out_specs=io,
                out_shape=jax.ShapeDtypeStruct((h1, n2, cols), BF16),
                compiler_params=_cp("arbitrary"),
                name="hy_stage3",
            )(b_re, b_im, t_inv_natural, g2.reshape(h1, n2, cols))
    return out.reshape(L, cols)


def _hyena(hv_t, h1_t, h2_t, conv_w, conv_b, filt, skip, tables=None):
    L, cols = hv_t.shape
    if _two_level(L):
        z3, g1_3, g2 = _hy_conv3(hv_t, h1_t, h2_t, conv_w, conv_b, swap=(True, True, False))
        return _hyena_two_level(z3, g1_3, g2, filt, skip, tables or _two_level_tables(L))
    z, g1, g2 = _hy_conv3(hv_t, h1_t, h2_t, conv_w, conv_b)
    return _hyena_direct(z, g1, g2, filt, skip)


def _route(logits):
    lane_i = lax.broadcasted_iota(jnp.int32, logits.shape, 1)
    lane = lane_i.astype(F32)
    is_g = (lane_i >= N_EXPERTS) & (lane_i < N_EXPERTS + N_GROUPS)
    big = float(LANES)
    gl = jnp.where(is_g, logits, NEG)
    gmax = jnp.max(gl, axis=-1, keepdims=True)
    gidx = jnp.min(jnp.where(gl == gmax, lane - N_EXPERTS, big), axis=-1, keepdims=True)
    p_group = 1.0 / jnp.sum(jnp.where(is_g, jnp.exp(gl - gmax), 0.0), axis=-1, keepdims=True)
    sel = (lane_i < N_EXPERTS) & ((lane_i // EXPERTS_PER_GROUP) == gidx.astype(jnp.int32))
    el = jnp.where(sel, logits, NEG)
    ee = jnp.where(sel, jnp.exp(el - jnp.max(el, axis=-1, keepdims=True)), 0.0)
    pe = ee / jnp.sum(ee, axis=-1, keepdims=True)
    p1 = jnp.max(jnp.where(sel, pe, -1.0), axis=-1, keepdims=True)
    i1 = jnp.min(jnp.where(sel & (pe == p1), lane, big), axis=-1, keepdims=True)
    rest = sel & (lane != i1)
    p2 = jnp.max(jnp.where(rest, pe, -1.0), axis=-1, keepdims=True)
    i2 = jnp.min(jnp.where(rest & (pe == p2), lane, big), axis=-1, keepdims=True)
    den = p1 + p2
    cmb = jnp.where(lane == i1, p_group * (p1 / den), jnp.where(lane == i2, p_group * (p2 / den), 0.0))
    return jnp.where(lane_i == N_EXPERTS, gidx, cmb)


def _outproj_kernel(att_ref, hf_ref, hb_ref, gt_ref, hy_ref, x_ref, mod_ref, w_ref, g_ref, b_ref,
                    wr_ref, br_ref, x1_ref, u2_ref, cmb_ref):
    bpt = att_ref.shape[0]

    def rows(ref):
        return jnp.concatenate([ref[:, b * D_RNN:(b + 1) * D_RNN] for b in range(bpt)], axis=0)

    rg = (rows(hf_ref) + rows(hb_ref)) * jax.nn.gelu(rows(gt_ref).astype(F32))
    att = att_ref[...].reshape(bpt * att_ref.shape[1], D_ATT)
    mix = (jnp.dot(att, w_ref[0:D_ATT, :], preferred_element_type=F32)
           + jnp.dot(rg.astype(BF16), w_ref[D_ATT:D_ATT + D_RNN, :], preferred_element_type=F32)
           + jnp.dot(rows(hy_ref).astype(BF16), w_ref[D_ATT + D_RNN:, :], preferred_element_type=F32))
    m = mod_ref[0]
    x1 = _ln(ALPHA * x_ref[...] + m[2:3] * mix) * g_ref[...] + b_ref[...]
    u2 = _ln(x1) * (1.0 + m[4:5]) + m[3:4]
    u_hi = u2.astype(BF16)
    u_lo = (u2 - u_hi.astype(F32)).astype(BF16)
    wr = wr_ref[...]
    w_hi = wr.astype(BF16)
    w_lo = (wr - w_hi.astype(F32)).astype(BF16)
    hi = jnp.dot(u_hi, jnp.concatenate([w_hi, w_lo], axis=1), preferred_element_type=F32)
    lo = jnp.dot(u_lo, w_hi, preferred_element_type=F32)
    logits = hi[:, 0:LANES] + hi[:, LANES:] + lo + br_ref[...]
    x1_ref[...] = x1
    u2_ref[...] = u_hi
    cmb_ref[...] = _route(logits)


def _outproj(att, hf_t, hb_t, gt_t, hy_t, x2d, mod_l, w_bf, ln_g, ln_b, wr, br, *, nb, L, tm,
             cond_base, cond_per_batch):
    tpb, bpt = _tile_split(nb, L, tm, cond_per_batch)
    tm = L * bpt // tpb
    d = x2d.shape[1]
    cond = (lambda i: cond_base + i // tpb) if cond_per_batch else (lambda i: cond_base)
    t_spec = pl.BlockSpec((L // tpb, bpt * D_RNN), lambda i: (i % tpb, i // tpb))
    tok = lambda c: pl.BlockSpec((tm, c), lambda i: (i, 0))
    const = lambda s: pl.BlockSpec(s, lambda i: (0, 0))
    n = nb * L
    return pl.pallas_call(
        _outproj_kernel,
        grid=(nb * tpb // bpt,),
        in_specs=[pl.BlockSpec((bpt, L // tpb, D_ATT), lambda i: (i // tpb, i % tpb, 0)),
                  t_spec, t_spec, t_spec, t_spec, tok(d),
                  pl.BlockSpec((1, 6, d), lambda i: (cond(i), 0, 0)),
                  const((d, d)), const((1, d)), const((1, d)), const((d, LANES)), const((1, LANES))],
        out_specs=[tok(d), tok(d), tok(LANES)],
        out_shape=[jax.ShapeDtypeStruct((n, d), F32), jax.ShapeDtypeStruct((n, d), BF16),
                   jax.ShapeDtypeStruct((n, LANES), F32)],
        compiler_params=_cp("arbitrary"),
        name="outproj",
    )(att, hf_t, hb_t, gt_t, hy_t, x2d, mod_l, w_bf, ln_g[None], ln_b[None], wr, br)


def _moe_slots(T, BLK):
    step = BLK * LANES // math.gcd(BLK, LANES)
    return -(-(T + N_GROUPS * (BLK - 1)) // step) * step


def _moe_kernel(u_ref, cmb_ref, x1_ref, mod_ref, wgu_ref, wd_ref, g_ref, b_ref, o_ref, ys_ref, *, T, BLK):
    SL = _moe_slots(T, BLK)
    u = u_ref[...]
    cmb = cmb_ref[...]
    lane_f = lax.broadcasted_iota(jnp.int32, (T, LANES), 1).astype(F32)
    gid_col = jnp.sum(jnp.where(lane_f == float(N_EXPERTS), cmb, 0.0), axis=-1, keepdims=True)
    onehot = jnp.where(lane_f == gid_col, 1.0, 0.0)
    ri = lax.broadcasted_iota(jnp.int32, (T, T), 0)
    ci = lax.broadcasted_iota(jnp.int32, (T, T), 1)
    before = jnp.where(ci < ri, 1.0, 0.0).astype(BF16)
    rank_col = jnp.sum(onehot * jnp.dot(before, onehot.astype(BF16), preferred_element_type=F32),
                       axis=-1, keepdims=True)
    counts = jnp.sum(onehot, axis=0, keepdims=True)
    lane_row = lax.broadcasted_iota(jnp.int32, (1, LANES), 1)
    base = jnp.int32(0)
    bases = []
    base_row = jnp.zeros((1, LANES), jnp.int32)
    for c in range(N_GROUPS):
        n_c = jnp.sum(jnp.where(lane_row == c, counts, 0.0)).astype(jnp.int32)
        bases.append(base)
        base_row = jnp.where(lane_row == c, base, base_row)
        base = base + ((n_c + (BLK - 1)) // BLK) * BLK
    total = base
    dest_f = jnp.sum(onehot * base_row.astype(F32), axis=-1, keepdims=True) + rank_col
    dest_col = dest_f.astype(jnp.int32)
    dest_row = jnp.broadcast_to(dest_f, (T, LANES)).T[0:1, :].astype(jnp.int32)
    cmb_hi = cmb.astype(BF16)
    cmb_lo = (cmb - cmb_hi.astype(F32)).astype(BF16)
    blk_rows = lax.broadcasted_iota(jnp.int32, (BLK, T), 0)
    blk_lane = lax.broadcasted_iota(jnp.int32, (BLK, LANES), 1)

    def block(b, carry):
        r0 = b * BLK
        rows = pl.ds(pl.multiple_of(r0, BLK), BLK)

        @pl.when(r0 < total)
        def _():
            grp = ((r0 >= bases[1]).astype(jnp.int32) + (r0 >= bases[2]).astype(jnp.int32)
                   + (r0 >= bases[3]).astype(jnp.int32))
            p = jnp.where(blk_rows + r0 == dest_row, 1.0, 0.0).astype(BF16)
            xb = jnp.dot(p, u, preferred_element_type=F32).astype(BF16)
            wb = (jnp.dot(p, cmb_hi, preferred_element_type=F32)
                  + jnp.dot(p, cmb_lo, preferred_element_type=F32))
            gu = jnp.dot(xb, wgu_ref[grp], preferred_element_type=F32)
            hs = []
            for j in range(EXPERTS_PER_GROUP):
                gate = gu[:, 2 * j * D_EXPERT:(2 * j + 1) * D_EXPERT]
                up = gu[:, (2 * j + 1) * D_EXPERT:(2 * j + 2) * D_EXPERT]
                w = jnp.sum(jnp.where(blk_lane == grp * EXPERTS_PER_GROUP + j, wb, 0.0), axis=-1, keepdims=True)
                hs.append((gate * _sigmoid(gate) * up * w).astype(BF16))
            ys_ref[rows, :] = jnp.dot(jnp.concatenate(hs, axis=1), wd_ref[grp],
                                      preferred_element_type=F32).astype(BF16)

        @pl.when(r0 >= total)
        def _():
            ys_ref[rows, :] = jnp.zeros((BLK, u.shape[1]), BF16)

        return carry

    lax.fori_loop(0, SL // BLK, block, 0)
    p_t = jnp.where(lax.broadcasted_iota(jnp.int32, (T, SL), 1) == dest_col, 1.0, 0.0).astype(BF16)
    moe = jnp.dot(p_t, ys_ref[...], preferred_element_type=F32)
    m = mod_ref[0]
    o_ref[...] = _ln(ALPHA * x1_ref[...] + m[5:6] * moe) * g_ref[...] + b_ref[...]


def _group_gate_up(w_gate, w_up):
    _, d, f = w_gate.shape
    gu = jnp.concatenate([w_gate, w_up], axis=-1).astype(BF16).reshape(N_GROUPS, EXPERTS_PER_GROUP, d, 2 * f)
    return gu.transpose(0, 2, 1, 3).reshape(N_GROUPS, d, EXPERTS_PER_GROUP * 2 * f)


def _moe(u2, cmb, x1, mod_l, wgu, wd, ln_g, ln_b, *, tm, tiles_per_cond, cond_base, cond_per_batch):
    n, d = x1.shape
    cond = (lambda i: cond_base + i // tiles_per_cond) if cond_per_batch else (lambda i: cond_base)
    tok = lambda c: pl.BlockSpec((tm, c), lambda i: (i, 0))
    const = lambda s: pl.BlockSpec(s, lambda i: (0,) * len(s))
    resident = lambda s: pl.BlockSpec(s, lambda i: (0,) * len(s), pipeline_mode=pl.Buffered(1))
    return pl.pallas_call(
        functools.partial(_moe_kernel, T=tm, BLK=MOE_BLK),
        grid=(n // tm,),
        in_specs=[tok(d), tok(LANES), tok(d),
                  pl.BlockSpec((1, 6, d), lambda i: (cond(i), 0, 0)),
                  resident(wgu.shape), resident(wd.shape),
                  const((1, d)), const((1, d))],
        out_specs=tok(d),
        out_shape=jax.ShapeDtypeStruct((n, d), F32),
        scratch_shapes=[pltpu.VMEM((_moe_slots(tm, MOE_BLK), d), BF16)],
        compiler_params=_cp("arbitrary"),
        name="moe",
    )(u2, cmb, x1, mod_l, wgu, wd, ln_g[None], ln_b[None])


def _stream_layer(x2d, mod_l, p, *, nb, L, tm, cond_base, cond_per_batch, ctx, cache=None, h0=None, prev_kv=None):
    kw = dict(nb=nb, L=L, tm=tm, cond_base=cond_base, cond_per_batch=cond_per_batch)
    q, k, v, xr_t, gt_t, hv_t, h1_t, h2_t = _inproj(x2d, mod_l, p["w_in"], kv_dtype=F32 if ctx else BF16,
                                                    prev_kv=prev_kv, **{**kw, "tm": INPROJ_TILE})
    if ctx:
        att = _ctx_attention(q, k, v)
    else:
        att = _nbr_attention(q, k, v, cache[0], cache[1], p["nbr_bias"])
    a_f, b_f, a_b, b_b = _rg_gates(xr_t, p["rg_conv_w"], p["rg_conv_b"], *p["rg_w"])
    hf_t, hb_t = _rg_scan(a_f, b_f, a_b, b_b, h0[0], h0[1])
    hy_t = _hyena(hv_t, h1_t, h2_t, p["hy_conv_w"], p["hy_conv_b"], p["hy_filt"][L], p["hy_skip"],
                  p["hy_tables"].get(L))
    x1, u2, cmb = _outproj(att, hf_t, hb_t, gt_t, hy_t, x2d, mod_l, p["w_out"], p["ln1_g"], p["ln1_b"],
                           p["router_w"], p["router_b"], **kw)
    tm_moe = min(MOE_TILE, nb * L)
    x2 = _moe(u2, cmb, x1, mod_l, p["moe_wgu"], p["moe_wd"], p["ln2_g"], p["ln2_b"], tm=tm_moe,
              tiles_per_cond=max(L // tm_moe, 1), cond_base=cond_base, cond_per_batch=cond_per_batch)
    return x2, k, v, hf_t, hb_t


def kernel(x_prompt, x_sample, cache_k, cache_v, state_rglru, c, c_ctx, w_mod, b_mod, w_in, attn_rpb, rg_conv_w, rg_conv_b, rg_wa, rg_ba, rg_wx, rg_bx, rg_lambda, hy_conv_w, hy_conv_b, hy_w1, hy_b1, hy_w2, hy_b2, hy_w3, hy_freq, hy_decay, hy_skip, w_out, ln1_g, ln1_b, router_wg, router_bg, router_we, router_be, moe_w_gate, moe_w_up, moe_w_down, ln2_g, ln2_b):
    nbc, Lc, d = x_prompt.shape
    nbl, Ll, _ = x_sample.shape
    depth = w_mod.shape[0]
    past = cache_k.shape[2]

    cond8 = jnp.zeros((8, d), F32).at[0].set(c_ctx).at[1:1 + nbl].set(c)
    mod = _modulation(cond8, w_mod, b_mod).reshape(depth, 8, 6, d)

    xp = x_prompt.reshape(nbc * Lc, d)
    xs = x_sample.reshape(nbl * Ll, d)
    kv, new_h = None, []
    hy_tables = {L: _two_level_tables(L) for L in {Lc, Ll} if _two_level(L)}
    for l in range(depth):
        pad = jnp.zeros((d, LANES - N_EXPERTS - N_GROUPS), F32)
        p = {
            "w_in": w_in[l].astype(BF16),
            "nbr_bias": _nbr_bias_tiles(attn_rpb[l]),
            "rg_conv_w": rg_conv_w[l], "rg_conv_b": rg_conv_b[l],
            "rg_w": _rg_weights(rg_wa[l], rg_ba[l], rg_wx[l], rg_bx[l], rg_lambda[l]),
            "hy_conv_w": hy_conv_w[l], "hy_conv_b": hy_conv_b[l], "hy_skip": hy_skip[l], "hy_tables": hy_tables,
            "hy_filt": {L: _hy_filters(L, hy_w1[l], hy_b1[l], hy_w2[l], hy_b2[l], hy_w3[l], hy_freq[l], hy_decay[l])
                        for L in sorted({Lc, Ll})},
            "w_out": w_out[l].astype(BF16), "ln1_g": ln1_g[l], "ln1_b": ln1_b[l],
            "router_w": jnp.concatenate([router_we[l], router_wg[l], pad], axis=1),
            "router_b": jnp.concatenate([router_be[l], router_bg[l], pad[0]])[None],
            "moe_wgu": _group_gate_up(moe_w_gate[l], moe_w_up[l]),
            "moe_wd": moe_w_down[l].astype(BF16).reshape(N_GROUPS, EXPERTS_PER_GROUP * D_EXPERT, d),
            "ln2_g": ln2_g[l], "ln2_b": ln2_b[l],
        }
        zeros_h = jnp.zeros((1, nbc * D_RNN), F32)
        xp, k, v, hf_t, hb_t = _stream_layer(
            xp, mod[l], p, nb=nbc, L=Lc, tm=ROW_CHUNK, cond_base=0, cond_per_batch=False, ctx=True,
            h0=(zeros_h, zeros_h), prev_kv=kv)
        kv = (k, v)
        new_h.append(jnp.stack([hf_t[Lc - 1].reshape(nbc, D_RNN), hb_t[0].reshape(nbc, D_RNN)], axis=1))

        cache = (cache_k[:, l].reshape(nbl, past, D_ATT), cache_v[:, l].reshape(nbl, past, D_ATT))
        h0 = (state_rglru[:, l, 0].reshape(1, nbl * D_RNN), state_rglru[:, l, 1].reshape(1, nbl * D_RNN))
        xs, _, _, _, _ = _stream_layer(
            xs, mod[l], p, nb=nbl, L=Ll, tm=ROW_CHUNK, cond_base=1, cond_per_batch=True, ctx=False,
            cache=cache, h0=h0)

    return (xp.reshape(nbc, Lc, d), xs.reshape(nbl, Ll, d),
            kv[0].reshape(nbc, depth, Lc, N_HEADS, HEAD_DIM), kv[1].reshape(nbc, depth, Lc, N_HEADS, HEAD_DIM),
            jnp.stack(new_h, axis=1))
```

```python
import functools
import math

import numpy as np
import jax
import jax.numpy as jnp
from jax import lax
from jax.experimental import pallas as pl
from jax.experimental.pallas import tpu as pltpu

F32 = jnp.float32
BF16 = jnp.bfloat16
HIGHEST = lax.Precision.HIGHEST

DEPTH = 2
D_MODEL = 1024
N_HEADS = 8
HEAD_DIM = 64
D_ATT = N_HEADS * HEAD_DIM
D_RNN = 256
D_HY = 256
GRID_W = 64
WIN_R = 8
WIN_C = 16
RG_CONV = 4
RG_C = 8.0
HY_CONV = 3
POS_EMB_DIM = 33
N_GROUPS = 4
EXPERTS_PER_GROUP = 4
N_EXPERTS = 16
D_EXPERT = 256
ALPHA = (2.0 * DEPTH) ** 0.25
LN_EPS = 1e-5
NEG = -1e30

Q_ROWS = 4
K_ROWS = Q_ROWS + WIN_R
LANES = 128
HALO = 8
ROW_CHUNK = 512
SCAN_LANES = 1024
BLOCK_BYTES = 1 << 20
MOD_COL_TILES = 4
FFT_MIN_LEN = 2048
FFT_KB = 8
INPROJ_TILE = 1024
MOE_TILE = 512
MOE_BLK = 128
STRIDE_PAD = 4
FFT_G = 16
FFT_N1 = 64
VMEM_LIMIT = 52 * 1024 * 1024


def _cp(*sem):
    return pltpu.CompilerParams(dimension_semantics=sem, vmem_limit_bytes=VMEM_LIMIT)


def _sigmoid(x):
    return 0.5 * jnp.tanh(0.5 * x) + 0.5


def _ln(x):
    mu = jnp.mean(x, axis=-1, keepdims=True)
    xc = x - mu
    var = jnp.mean(xc * xc, axis=-1, keepdims=True)
    return xc * lax.rsqrt(var + LN_EPS)


def _mod_kernel(c_ref, w_ref, b_ref, o_ref):
    c = c_ref[...]
    s = c * jax.nn.sigmoid(c)
    o_ref[0] = jnp.dot(s, w_ref[0], precision=HIGHEST, preferred_element_type=F32) + b_ref[0]


def _modulation(cond8, w_mod, b_mod):
    depth, d, n = w_mod.shape
    tn = n // MOD_COL_TILES
    return pl.pallas_call(
        _mod_kernel,
        grid=(depth, n // tn),
        in_specs=[pl.BlockSpec((8, d), lambda l, j: (0, 0)),
                  pl.BlockSpec((1, d, tn), lambda l, j: (l, 0, j)),
                  pl.BlockSpec((1, 1, tn), lambda l, j: (l, 0, j))],
        out_specs=pl.BlockSpec((1, 8, tn), lambda l, j: (l, 0, j)),
        out_shape=jax.ShapeDtypeStruct((depth, 8, n), F32),
        compiler_params=_cp("arbitrary", "arbitrary"),
        name="modulation",
    )(cond8, w_mod, b_mod.reshape(depth, 1, n))


def _inproj_kernel(x_ref, mod_ref, w_ref, *refs, n_prev):
    if n_prev:
        pk_ref, pv_ref = refs[:2]
        refs = refs[2:]
    q_ref, k_ref, v_ref, xr_ref, gt_ref, hv_ref, h1_ref, h2_ref = refs
    for i in range(n_prev):
        k_ref[:, i] = pk_ref[:, i]
        v_ref[:, i] = pv_ref[:, i]
    m = mod_ref[0]
    u = (_ln(x_ref[...]) * (1.0 + m[1:2]) + m[0:1]).astype(BF16)
    bpt, rows = q_ref.shape[0], q_ref.shape[1]

    def proj(lo, hi):
        return jnp.dot(u, w_ref[:, lo:hi], preferred_element_type=F32)

    o = 0
    q_ref[...] = (proj(o, o + D_ATT) * (HEAD_DIM ** -0.5)).astype(q_ref.dtype).reshape(q_ref.shape)
    o += D_ATT
    k_ref[:, n_prev] = proj(o, o + D_ATT).astype(k_ref.dtype).reshape(bpt, rows, D_ATT)
    o += D_ATT
    v_ref[:, n_prev] = proj(o, o + D_ATT).astype(v_ref.dtype).reshape(bpt, rows, D_ATT)
    o += D_ATT
    for ref in (xr_ref, gt_ref, hv_ref, h1_ref, h2_ref):
        y = proj(o, o + D_RNN).astype(ref.dtype)
        for b in range(bpt):
            ref[:, b * D_RNN:(b + 1) * D_RNN] = y[b * rows:(b + 1) * rows]
        o += D_RNN


def _tile_split(nb, L, tm, cond_per_batch):
    if tm <= L:
        return L // tm, 1
    bpt = 1 if cond_per_batch else min(nb, tm // L)
    while nb % bpt:
        bpt -= 1
    return 1, bpt


def _inproj(x2d, mod_l, w_bf, *, nb, L, tm, cond_base, cond_per_batch, kv_dtype, prev_kv=None):
    tpb, bpt = _tile_split(nb, L, tm, cond_per_batch)
    tm = L * bpt // tpb
    d = x2d.shape[1]
    d_in = w_bf.shape[1]
    n_prev = prev_kv[0].shape[1] if prev_kv else 0
    cond = (lambda i: cond_base + i // tpb) if cond_per_batch else (lambda i: cond_base)
    att_spec = pl.BlockSpec((bpt, L // tpb, D_ATT), lambda i: (i // tpb, i % tpb, 0))
    kv_spec = lambda n: pl.BlockSpec((bpt, n, L // tpb, D_ATT), lambda i: (i // tpb, 0, i % tpb, 0))
    t_spec = pl.BlockSpec((L // tpb, bpt * D_RNN), lambda i: (i % tpb, i // tpb))
    kv_shape = jax.ShapeDtypeStruct((nb, n_prev + 1, L, D_ATT), kv_dtype)
    t_shape = jax.ShapeDtypeStruct((L, nb * D_RNN), BF16)
    return pl.pallas_call(
        functools.partial(_inproj_kernel, n_prev=n_prev),
        grid=(nb * tpb // bpt,),
        in_specs=[pl.BlockSpec((tm, d), lambda i: (i, 0)),
                  pl.BlockSpec((1, 6, d), lambda i: (cond(i), 0, 0)),
                  pl.BlockSpec((d, d_in), lambda i: (0, 0))] + ([kv_spec(n_prev)] * 2 if n_prev else []),
        out_specs=[att_spec, kv_spec(n_prev + 1), kv_spec(n_prev + 1), t_spec, t_spec, t_spec, t_spec, t_spec],
        out_shape=[jax.ShapeDtypeStruct((nb, L, D_ATT), BF16), kv_shape, kv_shape,
                   t_shape, t_shape, t_shape, t_shape, t_shape],
        compiler_params=_cp("arbitrary"),
        name="inproj",
    )(x2d, mod_l, w_bf, *(prev_kv or ()))


def _nt(a, b):
    return lax.dot_general(a, b, (((1,), (1,)), ((), ())), preferred_element_type=F32)


def _ctx_attn_kernel(q_ref, k_ref, v_ref, o_ref):
    q = q_ref[0]
    k = k_ref[0, 0].astype(BF16)
    v = v_ref[0, 0].astype(BF16)
    outs = []
    for h in range(N_HEADS):
        sl = slice(h * HEAD_DIM, (h + 1) * HEAD_DIM)
        s = _nt(q[:, sl], k[:, sl])
        e = jnp.exp(s - jnp.max(s, axis=-1, keepdims=True))
        den = jnp.sum(e, axis=-1, keepdims=True)
        outs.append(jnp.dot(e.astype(BF16), v[:, sl], preferred_element_type=F32) / den)
    o_ref[0] = jnp.concatenate(outs, axis=-1).astype(o_ref.dtype)


def _ctx_attention(q, k, v):
    nb, L, _ = q.shape
    last = k.shape[1] - 1
    spec = pl.BlockSpec((1, L, D_ATT), lambda b: (b, 0, 0))
    kv_spec = pl.BlockSpec((1, 1, L, D_ATT), lambda b: (b, last, 0, 0))
    return pl.pallas_call(
        _ctx_attn_kernel,
        grid=(nb,),
        in_specs=[spec, kv_spec, kv_spec],
        out_specs=spec,
        out_shape=jax.ShapeDtypeStruct((nb, L, D_ATT), BF16),
        compiler_params=_cp("arbitrary"),
        name="ctx_attn",
    )(q, k, v)


N_DR = 2 * WIN_R - 1
PAIR_LEFT = N_DR - 1
PAIR_RIGHT = 2 * N_DR - 1
PAIR_NONE = 3 * N_DR - 1


def _nbr_bias_tiles(rpb):
    cols = np.arange(GRID_W)
    c_start = np.clip(cols - WIN_C // 2, 0, GRID_W - WIN_C)
    col_mask = (cols[None, :] >= c_start[:, None]) & (cols[None, :] < c_start[:, None] + WIN_C)
    dc_idx = np.clip(cols[None, :] - cols[:, None], -(WIN_C - 1), WIN_C - 1) + (WIN_C - 1)
    sel = (dc_idx[None] == np.arange(2 * WIN_C - 1)[:, None, None]).astype(np.float32)
    colb = jnp.einsum('hab,bqk->haqk', rpb.astype(F32), jnp.asarray(sel), precision=HIGHEST)
    colb = jnp.where(col_mask[None, None], colb, NEG)
    neg = jnp.full_like(colb, NEG)
    both = jnp.concatenate([colb[:, :-1], colb[:, 1:]], axis=-1)
    left_masked = jnp.concatenate([neg, colb], axis=-1)
    right_masked = jnp.concatenate([colb, neg], axis=-1)
    none = jnp.concatenate([neg[:, :1], neg[:, :1]], axis=-1)
    return jnp.concatenate([both, left_masked, right_masked, none], axis=1)


def _nbr_attn_kernel(q_ref, k_ref, v_ref, kc_ref, vc_ref, bias_ref, o_ref, *, rows):
    g = pl.program_id(1)
    r0 = jnp.clip(Q_ROWS * g - WIN_R // 2, 0, rows - K_ROWS)
    start = pl.multiple_of(r0 * GRID_W, Q_ROWS * GRID_W)
    nk = K_ROWS * GRID_W
    kw = k_ref[0, 0, pl.ds(start, nk), :]
    vw = v_ref[0, 0, pl.ds(start, nk), :]
    kc = kc_ref[0].astype(BF16)
    vc = vc_ref[0].astype(BF16)
    q = q_ref[0]
    tile = []
    for j in range(Q_ROWS):
        r = Q_ROWS * g + j
        lo = jnp.clip(r - WIN_R // 2, 0, rows - WIN_R) - r + (WIN_R - 1)
        tile.append([])
        for p in range(K_ROWS // 2):
            d0 = r0 + 2 * p - r + (WIN_R - 1)
            lv = (d0 >= lo) & (d0 < lo + WIN_R)
            rv = (d0 + 1 >= lo) & (d0 + 1 < lo + WIN_R)
            idx = jnp.where(lv & rv, d0, jnp.where(rv, PAIR_LEFT + d0 + 1, jnp.where(lv, PAIR_RIGHT + d0, PAIR_NONE)))
            tile[j].append(jnp.clip(idx, 0, PAIR_NONE))
    outs = []
    for h in range(N_HEADS):
        sl = slice(h * HEAD_DIM, (h + 1) * HEAD_DIM)
        qh = q[:, sl]
        bias = jnp.concatenate(
            [jnp.concatenate([bias_ref[h, tile[j][p]] for p in range(K_ROWS // 2)], axis=-1)
             for j in range(Q_ROWS)], axis=0)
        s1 = _nt(qh, kw[:, sl]) + bias
        s2 = _nt(qh, kc[:, sl])
        m = jnp.maximum(jnp.max(s1, axis=-1, keepdims=True), jnp.max(s2, axis=-1, keepdims=True))
        e1 = jnp.exp(s1 - m)
        e2 = jnp.exp(s2 - m)
        den = jnp.sum(e1, axis=-1, keepdims=True) + jnp.sum(e2, axis=-1, keepdims=True)
        o = (jnp.dot(e1.astype(BF16), vw[:, sl], preferred_element_type=F32)
             + jnp.dot(e2.astype(BF16), vc[:, sl], preferred_element_type=F32))
        outs.append(o / den)
    o_ref[0] = jnp.concatenate(outs, axis=-1).astype(o_ref.dtype)


def _nbr_attention(q, k, v, kc, vc, bias):
    nb, L, _ = q.shape
    rows = L // GRID_W
    ng = rows // Q_ROWS
    past = kc.shape[1]
    tq = Q_ROWS * GRID_W
    full = pl.BlockSpec((1, 1, L, D_ATT), lambda b, g: (b, 0, 0, 0))
    ctx = pl.BlockSpec((1, past, D_ATT), lambda b, g: (b, 0, 0))
    qspec = pl.BlockSpec((1, tq, D_ATT), lambda b, g: (b, g, 0))
    bspec = pl.BlockSpec(bias.shape, lambda b, g: (0, 0, 0, 0), pipeline_mode=pl.Buffered(1))
    return pl.pallas_call(
        functools.partial(_nbr_attn_kernel, rows=rows),
        grid=(nb, ng),
        in_specs=[qspec, full, full, ctx, ctx, bspec],
        out_specs=qspec,
        out_shape=jax.ShapeDtypeStruct((nb, L, D_ATT), BF16),
        compiler_params=_cp("arbitrary", "arbitrary"),
        name="nbr_attn",
    )(q, k, v, kc, vc, bias)


def _fill_padded(xp_ref, src_ref, L):
    z = jnp.zeros((HALO, xp_ref.shape[1]), F32)
    xp_ref[0:HALO, :] = z
    xp_ref[HALO:HALO + L, :] = src_ref[...].astype(F32)
    xp_ref[HALO + L:2 * HALO + L, :] = z


def _rg_gates_kernel(x_ref, cw_ref, cb_ref, w_ref, b_ref, lam_ref,
                     af_ref, bf_ref, ab_ref, bb_ref, xp_ref, *, L, tc, m):
    nh = cw_ref.shape[0]
    for sc in range(m):
        half = (pl.program_id(0) * m + sc) % nh
        cols = slice(sc * LANES, (sc + 1) * LANES)
        _fill_padded(xp_ref, x_ref.at[:, cols], L)
        sp = jax.nn.softplus(-lam_ref[half])
        cw = cw_ref[half]
        for t0 in range(0, L, tc):
            xc = cb_ref[half]
            for k in range(RG_CONV):
                xc = xc + cw[k:k + 1, :] * xp_ref[pl.ds(HALO + t0 + k - RG_CONV // 2, tc), :]
            z = jnp.dot(xc.astype(BF16), w_ref[half], preferred_element_type=F32) + b_ref[half]
            for e, (a_ref, o_ref) in enumerate(((af_ref, bf_ref), (ab_ref, bb_ref))):
                r = _sigmoid(z[:, e * LANES:(e + 1) * LANES])
                i = _sigmoid(z[:, (2 + e) * LANES:(3 + e) * LANES])
                log_a = (-RG_C) * r * sp[:, e * LANES:(e + 1) * LANES]
                a = jnp.exp(log_a)
                one_minus_a2 = -jnp.tanh(log_a) * (a * a + 1.0)
                a_ref[pl.ds(t0, tc), cols] = a
                o_ref[pl.ds(t0, tc), cols] = jnp.sqrt(one_minus_a2) * i * xc


def _blockdiag2(w2):
    z = jnp.zeros_like(w2[0])
    return jnp.concatenate([jnp.concatenate([w2[0], z], 1), jnp.concatenate([z, w2[1]], 1)], 0)


def _rg_weights(wa, ba, wx, bx, lam):
    ws, bs, ls = [], [], []
    for hf in range(D_RNN // LANES):
        blk = slice(2 * hf, 2 * hf + 2)
        ch = slice(hf * LANES, (hf + 1) * LANES)
        ws.append(jnp.concatenate([_blockdiag2(wa[0, blk]), _blockdiag2(wa[1, blk]),
                                   _blockdiag2(wx[0, blk]), _blockdiag2(wx[1, blk])], axis=1))
        bs.append(jnp.concatenate([ba[0, ch], ba[1, ch], bx[0, ch], bx[1, ch]])[None])
        ls.append(jnp.concatenate([lam[0, ch], lam[1, ch]])[None])
    return jnp.stack(ws).astype(BF16), jnp.stack(bs), jnp.stack(ls)


def _lane_chunks(L, cols):
    m = max(1, min(8, BLOCK_BYTES // (L * LANES * 4)))
    while cols % (m * LANES):
        m //= 2
    return m


def _rg_gates(xr_t, conv_w, conv_b, w_bd, b_cat, lam_cat):
    L, cols = xr_t.shape
    nh = D_RNN // LANES
    tc = min(L, ROW_CHUNK)
    m = _lane_chunks(L, cols)
    col = pl.BlockSpec((L, m * LANES), lambda c: (0, c))
    const = lambda a: pl.BlockSpec(a.shape, lambda c: (0,) * a.ndim)
    shape = jax.ShapeDtypeStruct((L, cols), F32)
    cw = conv_w.reshape(RG_CONV, nh, LANES).transpose(1, 0, 2)
    cb = conv_b.reshape(nh, 1, LANES)
    return pl.pallas_call(
        functools.partial(_rg_gates_kernel, L=L, tc=tc, m=m),
        grid=(cols // (m * LANES),),
        in_specs=[col, const(cw), const(cb), const(w_bd), const(b_cat), const(lam_cat)],
        out_specs=[col, col, col, col],
        out_shape=[shape, shape, shape, shape],
        scratch_shapes=[pltpu.VMEM((L + 2 * HALO, LANES), F32)],
        compiler_params=_cp("arbitrary"),
        name="rg_gates",
    )(xr_t, cw, cb, w_bd, b_cat, lam_cat)


def _rg_scan_kernel(af_ref, bf_ref, ab_ref, bb_ref, h0f_ref, h0b_ref, hf_ref, hb_ref, cf_ref, cb_ref, *, tc):
    @pl.when(pl.program_id(1) == 0)
    def _():
        cf_ref[...] = h0f_ref[...]
        cb_ref[...] = h0b_ref[...]

    def body(t, carry):
        hf, hb = carry
        hf = af_ref[pl.ds(t, 1), :] * hf + bf_ref[pl.ds(t, 1), :]
        hf_ref[pl.ds(t, 1), :] = hf
        tb = tc - 1 - t
        hb = ab_ref[pl.ds(tb, 1), :] * hb + bb_ref[pl.ds(tb, 1), :]
        hb_ref[pl.ds(tb, 1), :] = hb
        return hf, hb

    hf, hb = lax.fori_loop(0, tc, body, (cf_ref[...], cb_ref[...]), unroll=8)
    cf_ref[...] = hf
    cb_ref[...] = hb


def _rg_scan(a_f, b_f, a_b, b_b, h0f, h0b):
    L, cols = a_f.shape
    cw = min(cols, SCAN_LANES)
    tc = min(L, ROW_CHUNK)
    nt = L // tc
    fwd = pl.BlockSpec((tc, cw), lambda c, i: (i, c))
    bwd = pl.BlockSpec((tc, cw), lambda c, i: (nt - 1 - i, c))
    row = pl.BlockSpec((1, cw), lambda c, i: (0, c))
    shape = jax.ShapeDtypeStruct((L, cols), F32)
    return pl.pallas_call(
        functools.partial(_rg_scan_kernel, tc=tc),
        grid=(cols // cw, nt),
        in_specs=[fwd, fwd, bwd, bwd, row, row],
        out_specs=[fwd, bwd],
        out_shape=[shape, shape],
        scratch_shapes=[pltpu.VMEM((1, cw), F32), pltpu.VMEM((1, cw), F32)],
        compiler_params=_cp("arbitrary", "arbitrary"),
        name="rg_scan",
    )(a_f, b_f, a_b, b_b, h0f, h0b)


def _hy_filter_kernel(feat_ref, w1_ref, b1_ref, w2_ref, b2_ref, w3_ref, fr_ref, dec_ref, o_ref, *, tm):
    feats = feat_ref[...]
    dot = functools.partial(jnp.dot, precision=HIGHEST, preferred_element_type=F32)
    h = jnp.sin(fr_ref[0:1, :] * (dot(feats, w1_ref[...]) + b1_ref[...]))
    h = jnp.sin(fr_ref[1:2, :] * (dot(h, w2_ref[...]) + b2_ref[...]))
    h = dot(h, w3_ref[...])
    filt = h * jnp.exp(-feats[:, 0:1] * jnp.abs(dec_ref[...]))
    row = pl.program_id(0) * tm + lax.broadcasted_iota(jnp.int32, filt.shape, 0)
    lane = lax.broadcasted_iota(jnp.int32, filt.shape, 1)
    o_ref[...] = jnp.where((row == 0) & (lane >= 2 * D_HY), 0.0, filt)


def _hy_features(L):
    pos = np.arange(L, dtype=np.float64)
    t = np.linspace(0.0, 1.0, L)
    bands = (POS_EMB_DIM - 1) // 2
    f = np.linspace(1e-4, bands - 1, bands)
    ang = (2.0 * math.pi / L) * pos[:, None] * f[None, :]
    feats = np.concatenate([t[:, None], np.cos(ang), -np.sin(ang)], axis=-1)
    out = np.zeros((L, LANES), np.float32)
    out[:, :POS_EMB_DIM] = feats
    return out


def _two_level(L):
    return L % (FFT_N1 * 8) == 0 and L >= FFT_MIN_LEN


def _digit_swap(x, L):
    n2 = 2 * L // FFT_N1
    return x.reshape(L // n2, n2, -1).transpose(1, 0, 2).reshape(L, -1)


def _hy_filters(L, w1, b1, w2, b2, w3, freq, decay):
    tm = min(L, ROW_CHUNK)
    hid = w2.shape[0]
    n = w3.shape[1]
    w1p = jnp.zeros((LANES, hid), F32).at[:POS_EMB_DIM].set(w1)
    feats = _hy_features(L)
    if _two_level(L):
        feats = _digit_swap(feats, L)
    c = lambda s: pl.BlockSpec(s, lambda i: (0, 0))
    return pl.pallas_call(
        functools.partial(_hy_filter_kernel, tm=tm),
        grid=(L // tm,),
        in_specs=[pl.BlockSpec((tm, LANES), lambda i: (i, 0)),
                  c((LANES, hid)), c((1, hid)), c((hid, hid)), c((1, hid)), c((hid, n)), c((2, hid)), c((1, n))],
        out_specs=pl.BlockSpec((tm, n), lambda i: (i, 0)),
        out_shape=jax.ShapeDtypeStruct((L, n), F32),
        compiler_params=_cp("arbitrary"),
        name="hy_filter",
    )(jnp.asarray(feats), w1p, b1[None], w2, b2[None], w3, freq, decay.reshape(1, n))


def _conv3_kernel(v_ref, a_ref, b_ref, w_ref, cb_ref, ov_ref, oa_ref, ob_ref, xp_ref, xs_ref,
                  *, L, tc, swap, n2, m):
    h1 = L // n2
    pitch = n2 + STRIDE_PAD
    nh = w_ref.shape[0]
    for sc in range(m):
        half = (pl.program_id(0) * m + sc) % nh
        cols = slice(sc * LANES, (sc + 1) * LANES)
        for p, (src, dst) in enumerate(((v_ref, ov_ref), (a_ref, oa_ref), (b_ref, ob_ref))):
            _fill_padded(xp_ref, src.at[:, cols], L)
            w = w_ref[half, p]
            cb = cb_ref[half, p]
            if swap[p]:
                for i1 in range(h1):
                    xs_ref[pl.ds(i1 * pitch, n2 + HY_CONV - 1), :] = (
                        xp_ref[pl.ds(HALO + i1 * n2 - HY_CONV // 2, n2 + HY_CONV - 1), :])

                def body(i2, carry, dst=dst, w=w, cb=cb):
                    y = cb
                    for k in range(HY_CONV):
                        y = y + w[k:k + 1, :] * xs_ref[pl.ds(i2 + k, h1, stride=pitch), :]
                    dst[i2] = y.astype(dst.dtype)
                    return carry

                lax.fori_loop(0, n2, body, 0)
            else:
                for t0 in range(0, L, tc):
                    y = cb
                    for k in range(HY_CONV):
                        y = y + w[k:k + 1, :] * xp_ref[pl.ds(HALO + t0 + k - HY_CONV // 2, tc), :]
                    dst[pl.ds(t0, tc), cols] = y.astype(dst.dtype)


def _hy_conv3(v_t, a_t, b_t, conv_w, conv_b, swap=(False, False, False)):
    L, cols = v_t.shape
    nh = D_HY // LANES
    tc = min(L, ROW_CHUNK)
    n2 = 2 * L // FFT_N1
    m = 1 if any(swap) else _lane_chunks(L, cols)
    col = pl.BlockSpec((L, m * LANES), lambda c: (0, c))
    col3 = pl.BlockSpec((n2, L // n2, LANES), lambda c: (0, 0, c))
    shape = jax.ShapeDtypeStruct((L, cols), BF16)
    shape3 = jax.ShapeDtypeStruct((n2, L // n2, cols), BF16)
    w = conv_w.reshape(HY_CONV, 3, nh, LANES).transpose(2, 1, 0, 3)
    cb = conv_b.reshape(3, nh, 1, LANES).transpose(1, 0, 2, 3)
    const = lambda a: pl.BlockSpec(a.shape, lambda c: (0,) * a.ndim)
    return pl.pallas_call(
        functools.partial(_conv3_kernel, L=L, tc=tc, swap=swap, n2=n2, m=m),
        grid=(cols // (m * LANES),),
        in_specs=[col, col, col, const(w), const(cb)],
        out_specs=[col3 if s else col for s in swap],
        out_shape=[shape3 if s else shape for s in swap],
        scratch_shapes=[pltpu.VMEM((L + 2 * HALO, LANES), F32),
                        pltpu.VMEM(((L // n2) * (n2 + STRIDE_PAD) + HALO, LANES), F32)],
        compiler_params=_cp("arbitrary"),
        name="hy_conv3",
    )(v_t, a_t, b_t, w, cb)


def _phase(p, n2):
    ang = 2.0 * np.pi * (np.asarray(p, np.int64) % n2) / n2
    return np.cos(ang), np.sin(ang)


def _dft_direct_tables(L):
    k = np.arange(L)[:, None]
    t = np.arange(L)[None, :]
    c, s = _phase((2 * k + 1) * t, 4 * L)
    fwd = np.concatenate([c, -s], axis=0)
    inv = (1.0 / L) * np.concatenate([c.T, -s.T], axis=1)
    return fwd.astype(np.float32), inv.astype(np.float32)


def _dft_two_level_tables(L, n1):
    N = 2 * L
    n2 = N // n1
    h1 = n1 // 2
    i2 = np.arange(n2)[:, None, None]
    k1 = np.arange(n1)[None, :, None]
    i1 = np.arange(h1)[None, None, :]
    c, s = _phase(2 * n2 * i1 * k1 + 2 * i2 * k1 + n2 * i1 + i2, 2 * N)
    t1 = np.concatenate([c, -s], axis=1)
    ct, st = c.transpose(0, 2, 1), s.transpose(0, 2, 1)
    t1i = (2.0 / N) * np.concatenate([ct, -st], axis=2)
    kk = np.arange(n2 // 2)[:, None]
    nn = np.arange(n2)[None, :]
    c2, s2 = _phase(nn * kk, n2)
    t2 = np.block([[c2, s2], [-s2, c2]])
    t2i = np.block([[c2.T, -s2.T], [s2.T, c2.T]])
    return tuple(x.astype(np.float32) for x in (t1, t2, t2i, t1i))


def _filter_spectrum(fre, fim, order, skip):
    fw = slice(order * D_HY, (order + 1) * D_HY)
    bw = slice((2 + order) * D_HY, (3 + order) * D_HY)
    return fre[:, fw] + fre[:, bw] + skip, fim[:, fw] - fim[:, bw]


def _spectral_product(xr, xi, fr, fi, nbc):
    yr, yi = [], []
    for b in range(nbc):
        sl = slice(b * D_HY, (b + 1) * D_HY)
        yr.append(xr[:, sl] * fr - xi[:, sl] * fi)
        yi.append(xr[:, sl] * fi + xi[:, sl] * fr)
    return jnp.concatenate([jnp.concatenate(yr, axis=1), jnp.concatenate(yi, axis=1)], axis=0)


def _matmul_kernel(a_ref, b_ref, o_ref):
    o_ref[...] = jnp.dot(a_ref[...], b_ref[...], precision=HIGHEST, preferred_element_type=F32)


def _hy_direct_kernel(z_ref, g_ref, f_ref, skip_ref, t_ref, ti_ref, o_ref, *, L, nbc, order):
    x = jnp.dot(t_ref[...], z_ref[...].astype(BF16), preferred_element_type=F32)
    fr, fi = _filter_spectrum(f_ref[0:L, :], f_ref[L:2 * L, :], order, skip_ref[0])
    y = _spectral_product(x[:L], x[L:], fr, fi, nbc).astype(BF16)
    conv = jnp.dot(ti_ref[...], y, preferred_element_type=F32)
    o_ref[...] = (g_ref[...] * conv).astype(o_ref.dtype)


def _hyena_direct(z, g1, g2, filt, skip):
    L, cols = z.shape
    fwd, inv = _dft_direct_tables(L)
    n = filt.shape[1]
    spec = pl.pallas_call(
        _matmul_kernel,
        out_shape=jax.ShapeDtypeStruct((2 * L, n), F32),
        name="hy_fspec",
    )(jnp.asarray(fwd), filt)
    tn = min(cols, 4 * D_HY)
    nbc = tn // D_HY
    col = pl.BlockSpec((L, tn), lambda c: (0, c))
    const = lambda s: pl.BlockSpec(s, lambda c: (0, 0))
    t_bf, ti_bf = jnp.asarray(fwd).astype(BF16), jnp.asarray(inv).astype(BF16)
    for order, g in enumerate((g1, g2)):
        z = pl.pallas_call(
            functools.partial(_hy_direct_kernel, L=L, nbc=nbc, order=order),
            grid=(cols // tn,),
            in_specs=[col, col, const((2 * L, n)), pl.BlockSpec((1, 1, D_HY), lambda c: (order, 0, 0)),
                      const((2 * L, L)), const((L, 2 * L))],
            out_specs=col,
            out_shape=jax.ShapeDtypeStruct((L, cols), F32 if order == 0 else BF16),
            compiler_params=_cp("arbitrary"),
            name="hy_direct",
        )(z, g, spec, skip[:, None, :], t_bf, ti_bf)
    return z


def _kron_tables(t1, t1i, G):
    n2, two_n1, h1 = t1.shape
    ng = n2 // G
    eye = np.eye(two_n1, dtype=np.float32)
    rep_rows = jnp.asarray(np.repeat(eye, G, axis=0))
    rep_cols = jnp.asarray(np.repeat(eye, G, axis=1))
    c_fwd = t1.reshape(ng, G, two_n1, h1).transpose(0, 2, 1, 3).reshape(ng, two_n1, G * h1)
    c_swapped = t1i.reshape(ng, G * h1, two_n1)
    c_natural = t1i.reshape(ng, G, h1, two_n1).transpose(0, 2, 1, 3).reshape(ng, h1 * G, two_n1)

    def expand(compact, rep, left, row_key, lane_key):
        r, c = (rep.shape[0], compact.shape[2]) if left else (compact.shape[1], rep.shape[1])
        return pl.pallas_call(
            functools.partial(_kron_expand_kernel, left=left, row_key=row_key, lane_key=lane_key),
            grid=(ng,),
            in_specs=[pl.BlockSpec((1,) + compact.shape[1:], lambda i: (i, 0, 0)),
                      pl.BlockSpec(rep.shape, lambda i: (0, 0))],
            out_specs=pl.BlockSpec((1, r, c), lambda i: (i, 0, 0)),
            out_shape=jax.ShapeDtypeStruct((ng, r, c), BF16),
            compiler_params=_cp("arbitrary"),
            name="hy_kron",
        )(jnp.asarray(compact), rep)

    fwd = expand(c_fwd, rep_rows, True, lambda r: r % G, lambda c: c // h1)
    inv_swapped = expand(c_swapped, rep_cols, False, lambda r: r // h1, lambda c: c % G)
    inv_natural = expand(c_natural, rep_cols, False, lambda r: r % G, lambda c: c % G)
    return fwd, inv_swapped, inv_natural


def _kron_expand_kernel(c_ref, rep_ref, o_ref, *, left, row_key, lane_key):
    comp = c_ref[0].astype(BF16)
    rep = rep_ref[...].astype(BF16)
    full = (jnp.dot(rep, comp, preferred_element_type=F32) if left
            else jnp.dot(comp, rep, preferred_element_type=F32))
    rows = lax.broadcasted_iota(jnp.int32, full.shape, 0)
    lanes = lax.broadcasted_iota(jnp.int32, full.shape, 1)
    o_ref[0] = jnp.where(row_key(rows) == lane_key(lanes), full, 0.0).astype(BF16)


def _hy_stage1_kernel(x_ref, t_ref, re_ref, im_ref, *, G, n1, ct):
    h1, cols = x_ref.shape[1], x_ref.shape[2]
    x = x_ref[...].reshape(G * h1, cols).astype(BF16)
    for c0 in range(0, cols, ct):
        a = jnp.dot(t_ref[0], x[:, c0:c0 + ct], preferred_element_type=F32)
        re_ref[:, :, c0:c0 + ct] = a[:n1 * G].reshape(n1, G, ct).astype(BF16)
        im_ref[:, :, c0:c0 + ct] = a[n1 * G:].reshape(n1, G, ct).astype(BF16)


def _hy_stage1(x3, t_fwd, *, G):
    n2, h1, cols = x3.shape
    n1 = t_fwd.shape[1] // (2 * G)
    ospec = pl.BlockSpec((n1, G, cols), lambda j: (0, j, 0))
    oshape = jax.ShapeDtypeStruct((n1, n2, cols), BF16)
    return pl.pallas_call(
        functools.partial(_hy_stage1_kernel, G=G, n1=n1, ct=D_HY),
        grid=(n2 // G,),
        in_specs=[pl.BlockSpec((G, h1, cols), lambda j: (j, 0, 0)),
                  pl.BlockSpec((1,) + t_fwd.shape[1:], lambda j: (j, 0, 0))],
        out_specs=[ospec, ospec],
        out_shape=[oshape, oshape],
        compiler_params=_cp("arbitrary"),
        name="hy_stage1",
    )(x3, t_fwd)


def _hy_fspec2_kernel(re_ref, im_ref, skip_ref, t2_ref, fre_ref, fim_ref, *, kb, k2):
    for j in range(kb):
        a2 = jnp.concatenate([re_ref[j], im_ref[j]], axis=0)
        x = jnp.dot(t2_ref[...], a2, preferred_element_type=F32)
        for order in range(2):
            fr, fi = _filter_spectrum(x[:k2], x[k2:], order, skip_ref[order])
            fre_ref[order, j] = fr
            fim_ref[order, j] = fi


def _hy_stage2_kernel(re_ref, im_ref, fre_ref, fim_ref, t2_ref, t2i_ref, ore_ref, oim_ref, *, kb, k2, n2, nbc):
    for j in range(kb):
        a2 = jnp.concatenate([re_ref[j], im_ref[j]], axis=0)
        x = jnp.dot(t2_ref[...], a2, preferred_element_type=F32)
        fr, fi = fre_ref[0, j], fim_ref[0, j]
        y = _spectral_product(x[:k2], x[k2:], fr, fi, nbc).astype(BF16)
        a = jnp.dot(t2i_ref[...], y, preferred_element_type=F32)
        ore_ref[j] = a[:n2].astype(BF16)
        oim_ref[j] = a[n2:].astype(BF16)


def _hy_stage3_kernel(re_ref, im_ref, t_ref, g_ref, o_ref, *, G, n1, ct):
    cols = re_ref.shape[2]
    for c0 in range(0, cols, ct):
        sl = slice(c0, c0 + ct)
        a2 = jnp.concatenate([re_ref[:, :, sl].reshape(n1 * G, ct), im_ref[:, :, sl].reshape(n1 * G, ct)], axis=0)
        conv = jnp.dot(t_ref[0], a2, preferred_element_type=F32)
        g = g_ref[:, :, sl]
        o_ref[:, :, sl] = (g * conv.reshape(g.shape)).astype(o_ref.dtype)


def _hy_stage31_kernel(re_ref, im_ref, ti_ref, g_ref, tf_ref, ore_ref, oim_ref, *, G, n1, ct):
    cols = re_ref.shape[2]
    for c0 in range(0, cols, ct):
        sl = slice(c0, c0 + ct)
        a2 = jnp.concatenate([re_ref[:, :, sl].reshape(n1 * G, ct), im_ref[:, :, sl].reshape(n1 * G, ct)], axis=0)
        conv = jnp.dot(ti_ref[0], a2, preferred_element_type=F32)
        z = g_ref[:, :, sl].reshape(conv.shape) * conv
        a = jnp.dot(tf_ref[0], z.astype(BF16), preferred_element_type=F32)
        ore_ref[:, :, sl] = a[:n1 * G].reshape(n1, G, ct).astype(BF16)
        oim_ref[:, :, sl] = a[n1 * G:].reshape(n1, G, ct).astype(BF16)


def _two_level_tables(L, *, n1=FFT_N1, G=FFT_G):
    t1, t2, t2i, t1i = _dft_two_level_tables(L, n1)
    return (jnp.asarray(t2).astype(BF16), jnp.asarray(t2i).astype(BF16)) + _kron_tables(t1, t1i, G)


def _hyena_two_level(z3, g1_3, g2, filt, skip, tables, *, n1=FFT_N1, G=FFT_G, kb=FFT_KB):
    n2, h1, cols = z3.shape
    L = n2 * h1
    k2 = n2 // 2
    nbc = cols // D_HY
    t2_bf, t2i_bf, t_fwd, t_inv_swapped, t_inv_natural = tables
    nf = filt.shape[1]

    f_re, f_im = _hy_stage1(filt.reshape(n2, h1, nf), t_fwd, G=G)
    blk = lambda c: pl.BlockSpec((kb, n2, c), lambda i: (i, 0, 0))
    const = lambda s: pl.BlockSpec(s, lambda i: (0, 0))
    fshape = jax.ShapeDtypeStruct((2, n1, k2, D_HY), F32)
    fspec = pl.BlockSpec((2, kb, k2, D_HY), lambda i: (0, i, 0, 0))
    ospec = lambda order: pl.BlockSpec((1, kb, k2, D_HY), lambda i: (order, i, 0, 0))
    fre, fim = pl.pallas_call(
        functools.partial(_hy_fspec2_kernel, kb=kb, k2=k2),
        grid=(n1 // kb,),
        in_specs=[blk(nf), blk(nf), pl.BlockSpec((2, 1, D_HY), lambda i: (0, 0, 0)), const((n2, 2 * n2))],
        out_specs=[fspec, fspec],
        out_shape=[fshape, fshape],
        compiler_params=_cp("arbitrary"),
        name="hy_fspec2",
    )(f_re, f_im, skip[:, None, :], t2_bf)

    ashape = jax.ShapeDtypeStruct((n1, n2, cols), BF16)
    grp = pl.BlockSpec((n1, G, cols), lambda j: (0, j, 0))
    tspec = lambda t: pl.BlockSpec((1,) + t.shape[1:], lambda j: (j, 0, 0))
    a_re, a_im = _hy_stage1(z3, t_fwd, G=G)
    for order in range(2):
        b_re, b_im = pl.pallas_call(
            functools.partial(_hy_stage2_kernel, kb=kb, k2=k2, n2=n2, nbc=nbc),
            grid=(n1 // kb,),
            in_specs=[blk(cols), blk(cols), ospec(order), ospec(order), const((n2, 2 * n2)), const((2 * n2, n2))],
            out_specs=[blk(cols), blk(cols)],
            out_shape=[ashape, ashape],
            compiler_params=_cp("arbitrary"),
            name="hy_stage2",
        )(a_re, a_im, fre, fim, t2_bf, t2i_bf)
        if order == 0:
            a_re, a_im = pl.pallas_call(
                functools.partial(_hy_stage31_kernel, G=G, n1=n1, ct=D_HY),
                grid=(n2 // G,),
                in_specs=[grp, grp, tspec(t_inv_swapped), pl.BlockSpec((G, h1, cols), lambda j: (j, 0, 0)),
                          tspec(t_fwd)],
                out_specs=[grp, grp],
                out_shape=[ashape, ashape],
                compiler_params=_cp("arbitrary"),
                name="hy_stage31",
            )(b_re, b_im, t_inv_swapped, g1_3, t_fwd)
        else:
            io = pl.BlockSpec((h1, G, cols), lambda j: (0, j, 0))
            out = pl.pallas_call(
                functools.partial(_hy_stage3_kernel, G=G, n1=n1, ct=D_HY),
                grid=(n2 // G,),
                in_specs=[grp, grp, tspec(t_inv_natural), io],
                out_specs=io,
                out_shape=jax.ShapeDtypeStruct((h1, n2, cols), BF16),
                compiler_params=_cp("arbitrary"),
                name="hy_stage3",
            )(b_re, b_im, t_inv_natural, g2.reshape(h1, n2, cols))
    return out.reshape(L, cols)


def _hyena(hv_t, h1_t, h2_t, conv_w, conv_b, filt, skip, tables=None):
    L, cols = hv_t.shape
    if _two_level(L):
        z3, g1_3, g2 = _hy_conv3(hv_t, h1_t, h2_t, conv_w, conv_b, swap=(True, True, False))
        return _hyena_two_level(z3, g1_3, g2, filt, skip, tables or _two_level_tables(L))
    z, g1, g2 = _hy_conv3(hv_t, h1_t, h2_t, conv_w, conv_b)
    return _hyena_direct(z, g1, g2, filt, skip)


def _route(logits):
    lane_i = lax.broadcasted_iota(jnp.int32, logits.shape, 1)
    lane = lane_i.astype(F32)
    is_g = (lane_i >= N_EXPERTS) & (lane_i < N_EXPERTS + N_GROUPS)
    big = float(LANES)
    gl = jnp.where(is_g, logits, NEG)
    gmax = jnp.max(gl, axis=-1, keepdims=True)
    gidx = jnp.min(jnp.where(gl == gmax, lane - N_EXPERTS, big), axis=-1, keepdims=True)
    p_group = 1.0 / jnp.sum(jnp.where(is_g, jnp.exp(gl - gmax), 0.0), axis=-1, keepdims=True)
    sel = (lane_i < N_EXPERTS) & ((lane_i // EXPERTS_PER_GROUP) == gidx.astype(jnp.int32))
    el = jnp.where(sel, logits, NEG)
    ee = jnp.where(sel, jnp.exp(el - jnp.max(el, axis=-1, keepdims=True)), 0.0)
    pe = ee / jnp.sum(ee, axis=-1, keepdims=True)
    p1 = jnp.max(jnp.where(sel, pe, -1.0), axis=-1, keepdims=True)
    i1 = jnp.min(jnp.where(sel & (pe == p1), lane, big), axis=-1, keepdims=True)
    rest = sel & (lane != i1)
    p2 = jnp.max(jnp.where(rest, pe, -1.0), axis=-1, keepdims=True)
    i2 = jnp.min(jnp.where(rest & (pe == p2), lane, big), axis=-1, keepdims=True)
    den = p1 + p2
    cmb = jnp.where(lane == i1, p_group * (p1 / den), jnp.where(lane == i2, p_group * (p2 / den), 0.0))
    return jnp.where(lane_i == N_EXPERTS, gidx, cmb)


def _outproj_kernel(att_ref, hf_ref, hb_ref, gt_ref, hy_ref, x_ref, mod_ref, w_ref, g_ref, b_ref,
                    wr_ref, br_ref, x1_ref, u2_ref, cmb_ref):
    bpt = att_ref.shape[0]

    def rows(ref):
        return jnp.concatenate([ref[:, b * D_RNN:(b + 1) * D_RNN] for b in range(bpt)], axis=0)

    rg = (rows(hf_ref) + rows(hb_ref)) * jax.nn.gelu(rows(gt_ref).astype(F32))
    att = att_ref[...].reshape(bpt * att_ref.shape[1], D_ATT)
    mix = (jnp.dot(att, w_ref[0:D_ATT, :], preferred_element_type=F32)
           + jnp.dot(rg.astype(BF16), w_ref[D_ATT:D_ATT + D_RNN, :], preferred_element_type=F32)
           + jnp.dot(rows(hy_ref).astype(BF16), w_ref[D_ATT + D_RNN:, :], preferred_element_type=F32))
    m = mod_ref[0]
    x1 = _ln(ALPHA * x_ref[...] + m[2:3] * mix) * g_ref[...] + b_ref[...]
    u2 = _ln(x1) * (1.0 + m[4:5]) + m[3:4]
    u_hi = u2.astype(BF16)
    u_lo = (u2 - u_hi.astype(F32)).astype(BF16)
    wr = wr_ref[...]
    w_hi = wr.astype(BF16)
    w_lo = (wr - w_hi.astype(F32)).astype(BF16)
    hi = jnp.dot(u_hi, jnp.concatenate([w_hi, w_lo], axis=1), preferred_element_type=F32)
    lo = jnp.dot(u_lo, w_hi, preferred_element_type=F32)
    logits = hi[:, 0:LANES] + hi[:, LANES:] + lo + br_ref[...]
    x1_ref[...] = x1
    u2_ref[...] = u_hi
    cmb_ref[...] = _route(logits)


def _outproj(att, hf_t, hb_t, gt_t, hy_t, x2d, mod_l, w_bf, ln_g, ln_b, wr, br, *, nb, L, tm,
             cond_base, cond_per_batch):
    tpb, bpt = _tile_split(nb, L, tm, cond_per_batch)
    tm = L * bpt // tpb
    d = x2d.shape[1]
    cond = (lambda i: cond_base + i // tpb) if cond_per_batch else (lambda i: cond_base)
    t_spec = pl.BlockSpec((L // tpb, bpt * D_RNN), lambda i: (i % tpb, i // tpb))
    tok = lambda c: pl.BlockSpec((tm, c), lambda i: (i, 0))
    const = lambda s: pl.BlockSpec(s, lambda i: (0, 0))
    n = nb * L
    return pl.pallas_call(
        _outproj_kernel,
        grid=(nb * tpb // bpt,),
        in_specs=[pl.BlockSpec((bpt, L // tpb, D_ATT), lambda i: (i // tpb, i % tpb, 0)),
                  t_spec, t_spec, t_spec, t_spec, tok(d),
                  pl.BlockSpec((1, 6, d), lambda i: (cond(i), 0, 0)),
                  const((d, d)), const((1, d)), const((1, d)), const((d, LANES)), const((1, LANES))],
        out_specs=[tok(d), tok(d), tok(LANES)],
        out_shape=[jax.ShapeDtypeStruct((n, d), F32), jax.ShapeDtypeStruct((n, d), BF16),
                   jax.ShapeDtypeStruct((n, LANES), F32)],
        compiler_params=_cp("arbitrary"),
        name="outproj",
    )(att, hf_t, hb_t, gt_t, hy_t, x2d, mod_l, w_bf, ln_g[None], ln_b[None], wr, br)


def _moe_kernel(u_ref, cmb_ref, x1_ref, mod_ref, wgu_ref, wd_ref, g_ref, b_ref, o_ref, ys_ref, *, T, BLK):
    SL = T + N_GROUPS * BLK
    u = u_ref[...]
    cmb = cmb_ref[...]
    lane_f = lax.broadcasted_iota(jnp.int32, (T, LANES), 1).astype(F32)
    gid_col = jnp.sum(jnp.where(lane_f == float(N_EXPERTS), cmb, 0.0), axis=-1, keepdims=True)
    onehot = jnp.where(lane_f == gid_col, 1.0, 0.0)
    ri = lax.broadcasted_iota(jnp.int32, (T, T), 0)
    ci = lax.broadcasted_iota(jnp.int32, (T, T), 1)
    before = jnp.where(ci < ri, 1.0, 0.0).astype(BF16)
    rank_col = jnp.sum(onehot * jnp.dot(before, onehot.astype(BF16), preferred_element_type=F32),
                       axis=-1, keepdims=True)
    counts = jnp.sum(onehot, axis=0, keepdims=True)
    lane_row = lax.broadcasted_iota(jnp.int32, (1, LANES), 1)
    base = jnp.int32(0)
    bases = []
    base_row = jnp.zeros((1, LANES), jnp.int32)
    for c in range(N_GROUPS):
        n_c = jnp.sum(jnp.where(lane_row == c, counts, 0.0)).astype(jnp.int32)
        bases.append(base)
        base_row = jnp.where(lane_row == c, base, base_row)
        base = base + ((n_c + (BLK - 1)) // BLK) * BLK
    total = base
    dest_f = jnp.sum(onehot * base_row.astype(F32), axis=-1, keepdims=True) + rank_col
    dest_col = dest_f.astype(jnp.int32)
    dest_row = jnp.broadcast_to(dest_f, (T, LANES)).T[0:1, :].astype(jnp.int32)
    cmb_hi = cmb.astype(BF16)
    cmb_lo = (cmb - cmb_hi.astype(F32)).astype(BF16)
    blk_rows = lax.broadcasted_iota(jnp.int32, (BLK, T), 0)
    blk_lane = lax.broadcasted_iota(jnp.int32, (BLK, LANES), 1)

    def block(b, carry):
        r0 = b * BLK
        rows = pl.ds(pl.multiple_of(r0, BLK), BLK)

        @pl.when(r0 < total)
        def _():
            grp = ((r0 >= bases[1]).astype(jnp.int32) + (r0 >= bases[2]).astype(jnp.int32)
                   + (r0 >= bases[3]).astype(jnp.int32))
            p = jnp.where(blk_rows + r0 == dest_row, 1.0, 0.0).astype(BF16)
            xb = jnp.dot(p, u, preferred_element_type=F32).astype(BF16)
            wb = (jnp.dot(p, cmb_hi, preferred_element_type=F32)
                  + jnp.dot(p, cmb_lo, preferred_element_type=F32))
            gu = jnp.dot(xb, wgu_ref[grp], preferred_element_type=F32)
            hs = []
            for j in range(EXPERTS_PER_GROUP):
                gate = gu[:, 2 * j * D_EXPERT:(2 * j + 1) * D_EXPERT]
                up = gu[:, (2 * j + 1) * D_EXPERT:(2 * j + 2) * D_EXPERT]
                w = jnp.sum(jnp.where(blk_lane == grp * EXPERTS_PER_GROUP + j, wb, 0.0), axis=-1, keepdims=True)
                hs.append((gate * _sigmoid(gate) * up * w).astype(BF16))
            ys_ref[rows, :] = jnp.dot(jnp.concatenate(hs, axis=1), wd_ref[grp],
                                      preferred_element_type=F32).astype(BF16)

        @pl.when(r0 >= total)
        def _():
            ys_ref[rows, :] = jnp.zeros((BLK, u.shape[1]), BF16)

        return carry

    lax.fori_loop(0, SL // BLK, block, 0)
    p_t = jnp.where(lax.broadcasted_iota(jnp.int32, (T, SL), 1) == dest_col, 1.0, 0.0).astype(BF16)
    moe = jnp.dot(p_t, ys_ref[...], preferred_element_type=F32)
    m = mod_ref[0]
    o_ref[...] = _ln(ALPHA * x1_ref[...] + m[5:6] * moe) * g_ref[...] + b_ref[...]


def _group_gate_up(w_gate, w_up):
    _, d, f = w_gate.shape
    gu = jnp.concatenate([w_gate, w_up], axis=-1).astype(BF16).reshape(N_GROUPS, EXPERTS_PER_GROUP, d, 2 * f)
    return gu.transpose(0, 2, 1, 3).reshape(N_GROUPS, d, EXPERTS_PER_GROUP * 2 * f)


def _moe(u2, cmb, x1, mod_l, wgu, wd, ln_g, ln_b, *, tm, tiles_per_cond, cond_base, cond_per_batch):
    n, d = x1.shape
    cond = (lambda i: cond_base + i // tiles_per_cond) if cond_per_batch else (lambda i: cond_base)
    tok = lambda c: pl.BlockSpec((tm, c), lambda i: (i, 0))
    const = lambda s: pl.BlockSpec(s, lambda i: (0,) * len(s))
    resident = lambda s: pl.BlockSpec(s, lambda i: (0,) * len(s), pipeline_mode=pl.Buffered(1))
    return pl.pallas_call(
        functools.partial(_moe_kernel, T=tm, BLK=MOE_BLK),
        grid=(n // tm,),
        in_specs=[tok(d), tok(LANES), tok(d),
                  pl.BlockSpec((1, 6, d), lambda i: (cond(i), 0, 0)),
                  resident(wgu.shape), resident(wd.shape),
                  const((1, d)), const((1, d))],
        out_specs=tok(d),
        out_shape=jax.ShapeDtypeStruct((n, d), F32),
        scratch_shapes=[pltpu.VMEM((tm + N_GROUPS * MOE_BLK, d), BF16)],
        compiler_params=_cp("arbitrary"),
        name="moe",
    )(u2, cmb, x1, mod_l, wgu, wd, ln_g[None], ln_b[None])


def _stream_layer(x2d, mod_l, p, *, nb, L, tm, cond_base, cond_per_batch, ctx, cache=None, h0=None, prev_kv=None):
    kw = dict(nb=nb, L=L, tm=tm, cond_base=cond_base, cond_per_batch=cond_per_batch)
    q, k, v, xr_t, gt_t, hv_t, h1_t, h2_t = _inproj(x2d, mod_l, p["w_in"], kv_dtype=F32 if ctx else BF16,
                                                    prev_kv=prev_kv, **{**kw, "tm": INPROJ_TILE})
    if ctx:
        att = _ctx_attention(q, k, v)
    else:
        att = _nbr_attention(q, k, v, cache[0], cache[1], p["nbr_bias"])
    a_f, b_f, a_b, b_b = _rg_gates(xr_t, p["rg_conv_w"], p["rg_conv_b"], *p["rg_w"])
    hf_t, hb_t = _rg_scan(a_f, b_f, a_b, b_b, h0[0], h0[1])
    hy_t = _hyena(hv_t, h1_t, h2_t, p["hy_conv_w"], p["hy_conv_b"], p["hy_filt"][L], p["hy_skip"],
                  p["hy_tables"].get(L))
    x1, u2, cmb = _outproj(att, hf_t, hb_t, gt_t, hy_t, x2d, mod_l, p["w_out"], p["ln1_g"], p["ln1_b"],
                           p["router_w"], p["router_b"], **kw)
    tm_moe = min(MOE_TILE, nb * L)
    x2 = _moe(u2, cmb, x1, mod_l, p["moe_wgu"], p["moe_wd"], p["ln2_g"], p["ln2_b"], tm=tm_moe,
              tiles_per_cond=max(L // tm_moe, 1), cond_base=cond_base, cond_per_batch=cond_per_batch)
    return x2, k, v, hf_t, hb_t


def kernel(x_prompt, x_sample, cache_k, cache_v, state_rglru, c, c_ctx, w_mod, b_mod, w_in, attn_rpb, rg_conv_w, rg_conv_b, rg_wa, rg_ba, rg_wx, rg_bx, rg_lambda, hy_conv_w, hy_conv_b, hy_w1, hy_b1, hy_w2, hy_b2, hy_w3, hy_freq, hy_decay, hy_skip, w_out, ln1_g, ln1_b, router_wg, router_bg, router_we, router_be, moe_w_gate, moe_w_up, moe_w_down, ln2_g, ln2_b):
    nbc, Lc, d = x_prompt.shape
    nbl, Ll, _ = x_sample.shape
    depth = w_mod.shape[0]
    past = cache_k.shape[2]

    cond8 = jnp.zeros((8, d), F32).at[0].set(c_ctx).at[1:1 + nbl].set(c)
    mod = _modulation(cond8, w_mod, b_mod).reshape(depth, 8, 6, d)

    xp = x_prompt.reshape(nbc * Lc, d)
    xs = x_sample.reshape(nbl * Ll, d)
    kv, new_h = None, []
    hy_tables = {L: _two_level_tables(L) for L in {Lc, Ll} if _two_level(L)}
    for l in range(depth):
        pad = jnp.zeros((d, LANES - N_EXPERTS - N_GROUPS), F32)
        p = {
            "w_in": w_in[l].astype(BF16),
            "nbr_bias": _nbr_bias_tiles(attn_rpb[l]),
            "rg_conv_w": rg_conv_w[l], "rg_conv_b": rg_conv_b[l],
            "rg_w": _rg_weights(rg_wa[l], rg_ba[l], rg_wx[l], rg_bx[l], rg_lambda[l]),
            "hy_conv_w": hy_conv_w[l], "hy_conv_b": hy_conv_b[l], "hy_skip": hy_skip[l], "hy_tables": hy_tables,
            "hy_filt": {L: _hy_filters(L, hy_w1[l], hy_b1[l], hy_w2[l], hy_b2[l], hy_w3[l], hy_freq[l], hy_decay[l])
                        for L in sorted({Lc, Ll})},
            "w_out": w_out[l].astype(BF16), "ln1_g": ln1_g[l], "ln1_b": ln1_b[l],
            "router_w": jnp.concatenate([router_we[l], router_wg[l], pad], axis=1),
            "router_b": jnp.concatenate([router_be[l], router_bg[l], pad[0]])[None],
            "moe_wgu": _group_gate_up(moe_w_gate[l], moe_w_up[l]),
            "moe_wd": moe_w_down[l].astype(BF16).reshape(N_GROUPS, EXPERTS_PER_GROUP * D_EXPERT, d),
            "ln2_g": ln2_g[l], "ln2_b": ln2_b[l],
        }
        zeros_h = jnp.zeros((1, nbc * D_RNN), F32)
        xp, k, v, hf_t, hb_t = _stream_layer(
            xp, mod[l], p, nb=nbc, L=Lc, tm=ROW_CHUNK, cond_base=0, cond_per_batch=False, ctx=True,
            h0=(zeros_h, zeros_h), prev_kv=kv)
        kv = (k, v)
        new_h.append(jnp.stack([hf_t[Lc - 1].reshape(nbc, D_RNN), hb_t[0].reshape(nbc, D_RNN)], axis=1))

        cache = (cache_k[:, l].reshape(nbl, past, D_ATT), cache_v[:, l].reshape(nbl, past, D_ATT))
        h0 = (state_rglru[:, l, 0].reshape(1, nbl * D_RNN), state_rglru[:, l, 1].reshape(1, nbl * D_RNN))
        xs, _, _, _, _ = _stream_layer(
            xs, mod[l], p, nb=nbl, L=Ll, tm=ROW_CHUNK, cond_base=1, cond_per_batch=True, ctx=False,
            cache=cache, h0=h0)

    return (xp.reshape(nbc, Lc, d), xs.reshape(nbl, Ll, d),
            kv[0].reshape(nbc, depth, Lc, N_HEADS, HEAD_DIM), kv[1].reshape(nbc, depth, Lc, N_HEADS, HEAD_DIM),
            jnp.stack(new_h, axis=1))
```

```python
import functools
import math

import numpy as np
import jax
import jax.numpy as jnp
from jax import lax
from jax.experimental import pallas as pl
from jax.experimental.pallas import tpu as pltpu

F32 = jnp.float32
BF16 = jnp.bfloat16
HIGHEST = lax.Precision.HIGHEST

DEPTH = 2
D_MODEL = 1024
N_HEADS = 8
HEAD_DIM = 64
D_ATT = N_HEADS * HEAD_DIM
D_RNN = 256
D_HY = 256
GRID_W = 64
WIN_R = 8
WIN_C = 16
RG_CONV = 4
RG_C = 8.0
HY_CONV = 3
POS_EMB_DIM = 33
N_GROUPS = 4
EXPERTS_PER_GROUP = 4
N_EXPERTS = 16
D_EXPERT = 256
ALPHA = (2.0 * DEPTH) ** 0.25
LN_EPS = 1e-5
NEG = -1e30

Q_ROWS = 4
K_ROWS = Q_ROWS + WIN_R
LANES = 128
HALO = 8
ROW_CHUNK = 512
SCAN_LANES = 1024
BLOCK_BYTES = 1 << 20
MOD_COL_TILES = 4
FFT_MIN_LEN = 2048
FFT_KB = 8
INPROJ_TILE = 1024
MOE_TILE = 512
MOE_BLK = 128
STRIDE_PAD = 4
FFT_G = 8
FFT_N1 = 64
VMEM_LIMIT = 52 * 1024 * 1024


def _cp(*sem):
    return pltpu.CompilerParams(dimension_semantics=sem, vmem_limit_bytes=VMEM_LIMIT)


def _sigmoid(x):
    return 0.5 * jnp.tanh(0.5 * x) + 0.5


def _ln(x):
    mu = jnp.mean(x, axis=-1, keepdims=True)
    xc = x - mu
    var = jnp.mean(xc * xc, axis=-1, keepdims=True)
    return xc * lax.rsqrt(var + LN_EPS)


def _mod_kernel(c_ref, w_ref, b_ref, o_ref):
    c = c_ref[...]
    s = c * jax.nn.sigmoid(c)
    o_ref[0] = jnp.dot(s, w_ref[0], precision=HIGHEST, preferred_element_type=F32) + b_ref[0]


def _modulation(cond8, w_mod, b_mod):
    depth, d, n = w_mod.shape
    tn = n // MOD_COL_TILES
    return pl.pallas_call(
        _mod_kernel,
        grid=(depth, n // tn),
        in_specs=[pl.BlockSpec((8, d), lambda l, j: (0, 0)),
                  pl.BlockSpec((1, d, tn), lambda l, j: (l, 0, j)),
                  pl.BlockSpec((1, 1, tn), lambda l, j: (l, 0, j))],
        out_specs=pl.BlockSpec((1, 8, tn), lambda l, j: (l, 0, j)),
        out_shape=jax.ShapeDtypeStruct((depth, 8, n), F32),
        compiler_params=_cp("arbitrary", "arbitrary"),
        name="modulation",
    )(cond8, w_mod, b_mod.reshape(depth, 1, n))


def _inproj_kernel(x_ref, mod_ref, w_ref, *refs, n_prev):
    if n_prev:
        pk_ref, pv_ref = refs[:2]
        refs = refs[2:]
    q_ref, k_ref, v_ref, xr_ref, gt_ref, hv_ref, h1_ref, h2_ref = refs
    for i in range(n_prev):
        k_ref[:, i] = pk_ref[:, i]
        v_ref[:, i] = pv_ref[:, i]
    m = mod_ref[0]
    u = (_ln(x_ref[...]) * (1.0 + m[1:2]) + m[0:1]).astype(BF16)
    bpt, rows = q_ref.shape[0], q_ref.shape[1]

    def proj(lo, hi):
        return jnp.dot(u, w_ref[:, lo:hi], preferred_element_type=F32)

    o = 0
    q_ref[...] = (proj(o, o + D_ATT) * (HEAD_DIM ** -0.5)).astype(q_ref.dtype).reshape(q_ref.shape)
    o += D_ATT
    k_ref[:, n_prev] = proj(o, o + D_ATT).astype(k_ref.dtype).reshape(bpt, rows, D_ATT)
    o += D_ATT
    v_ref[:, n_prev] = proj(o, o + D_ATT).astype(v_ref.dtype).reshape(bpt, rows, D_ATT)
    o += D_ATT
    for ref in (xr_ref, gt_ref, hv_ref, h1_ref, h2_ref):
        y = proj(o, o + D_RNN).astype(ref.dtype)
        for b in range(bpt):
            ref[:, b * D_RNN:(b + 1) * D_RNN] = y[b * rows:(b + 1) * rows]
        o += D_RNN


def _tile_split(nb, L, tm, cond_per_batch):
    if tm <= L:
        return L // tm, 1
    bpt = 1 if cond_per_batch else min(nb, tm // L)
    while nb % bpt:
        bpt -= 1
    return 1, bpt


def _inproj(x2d, mod_l, w_bf, *, nb, L, tm, cond_base, cond_per_batch, kv_dtype, prev_kv=None):
    tpb, bpt = _tile_split(nb, L, tm, cond_per_batch)
    tm = L * bpt // tpb
    d = x2d.shape[1]
    d_in = w_bf.shape[1]
    n_prev = prev_kv[0].shape[1] if prev_kv else 0
    cond = (lambda i: cond_base + i // tpb) if cond_per_batch else (lambda i: cond_base)
    att_spec = pl.BlockSpec((bpt, L // tpb, D_ATT), lambda i: (i // tpb, i % tpb, 0))
    kv_spec = lambda n: pl.BlockSpec((bpt, n, L // tpb, D_ATT), lambda i: (i // tpb, 0, i % tpb, 0))
    t_spec = pl.BlockSpec((L // tpb, bpt * D_RNN), lambda i: (i % tpb, i // tpb))
    kv_shape = jax.ShapeDtypeStruct((nb, n_prev + 1, L, D_ATT), kv_dtype)
    t_shape = jax.ShapeDtypeStruct((L, nb * D_RNN), BF16)
    return pl.pallas_call(
        functools.partial(_inproj_kernel, n_prev=n_prev),
        grid=(nb * tpb // bpt,),
        in_specs=[pl.BlockSpec((tm, d), lambda i: (i, 0)),
                  pl.BlockSpec((1, 6, d), lambda i: (cond(i), 0, 0)),
                  pl.BlockSpec((d, d_in), lambda i: (0, 0))] + ([kv_spec(n_prev)] * 2 if n_prev else []),
        out_specs=[att_spec, kv_spec(n_prev + 1), kv_spec(n_prev + 1), t_spec, t_spec, t_spec, t_spec, t_spec],
        out_shape=[jax.ShapeDtypeStruct((nb, L, D_ATT), BF16), kv_shape, kv_shape,
                   t_shape, t_shape, t_shape, t_shape, t_shape],
        compiler_params=_cp("arbitrary"),
        name="inproj",
    )(x2d, mod_l, w_bf, *(prev_kv or ()))


def _nt(a, b):
    return lax.dot_general(a, b, (((1,), (1,)), ((), ())), preferred_element_type=F32)


def _ctx_attn_kernel(q_ref, k_ref, v_ref, o_ref):
    q = q_ref[0]
    k = k_ref[0, 0].astype(BF16)
    v = v_ref[0, 0].astype(BF16)
    outs = []
    for h in range(N_HEADS):
        sl = slice(h * HEAD_DIM, (h + 1) * HEAD_DIM)
        s = _nt(q[:, sl], k[:, sl])
        e = jnp.exp(s - jnp.max(s, axis=-1, keepdims=True))
        den = jnp.sum(e, axis=-1, keepdims=True)
        outs.append(jnp.dot(e.astype(BF16), v[:, sl], preferred_element_type=F32) / den)
    o_ref[0] = jnp.concatenate(outs, axis=-1).astype(o_ref.dtype)


def _ctx_attention(q, k, v):
    nb, L, _ = q.shape
    last = k.shape[1] - 1
    spec = pl.BlockSpec((1, L, D_ATT), lambda b: (b, 0, 0))
    kv_spec = pl.BlockSpec((1, 1, L, D_ATT), lambda b: (b, last, 0, 0))
    return pl.pallas_call(
        _ctx_attn_kernel,
        grid=(nb,),
        in_specs=[spec, kv_spec, kv_spec],
        out_specs=spec,
        out_shape=jax.ShapeDtypeStruct((nb, L, D_ATT), BF16),
        compiler_params=_cp("arbitrary"),
        name="ctx_attn",
    )(q, k, v)


N_DR = 2 * WIN_R - 1
PAIR_LEFT = N_DR - 1
PAIR_RIGHT = 2 * N_DR - 1
PAIR_NONE = 3 * N_DR - 1


def _nbr_bias_tiles(rpb):
    cols = np.arange(GRID_W)
    c_start = np.clip(cols - WIN_C // 2, 0, GRID_W - WIN_C)
    col_mask = (cols[None, :] >= c_start[:, None]) & (cols[None, :] < c_start[:, None] + WIN_C)
    dc_idx = np.clip(cols[None, :] - cols[:, None], -(WIN_C - 1), WIN_C - 1) + (WIN_C - 1)
    sel = (dc_idx[None] == np.arange(2 * WIN_C - 1)[:, None, None]).astype(np.float32)
    colb = jnp.einsum('hab,bqk->haqk', rpb.astype(F32), jnp.asarray(sel), precision=HIGHEST)
    colb = jnp.where(col_mask[None, None], colb, NEG)
    neg = jnp.full_like(colb, NEG)
    both = jnp.concatenate([colb[:, :-1], colb[:, 1:]], axis=-1)
    left_masked = jnp.concatenate([neg, colb], axis=-1)
    right_masked = jnp.concatenate([colb, neg], axis=-1)
    none = jnp.concatenate([neg[:, :1], neg[:, :1]], axis=-1)
    return jnp.concatenate([both, left_masked, right_masked, none], axis=1)


def _nbr_attn_kernel(q_ref, k_ref, v_ref, kc_ref, vc_ref, bias_ref, o_ref, *, rows):
    g = pl.program_id(1)
    r0 = jnp.clip(Q_ROWS * g - WIN_R // 2, 0, rows - K_ROWS)
    start = pl.multiple_of(r0 * GRID_W, Q_ROWS * GRID_W)
    nk = K_ROWS * GRID_W
    kw = k_ref[0, 0, pl.ds(start, nk), :]
    vw = v_ref[0, 0, pl.ds(start, nk), :]
    kc = kc_ref[0].astype(BF16)
    vc = vc_ref[0].astype(BF16)
    q = q_ref[0]
    tile = []
    for j in range(Q_ROWS):
        r = Q_ROWS * g + j
        lo = jnp.clip(r - WIN_R // 2, 0, rows - WIN_R) - r + (WIN_R - 1)
        tile.append([])
        for p in range(K_ROWS // 2):
            d0 = r0 + 2 * p - r + (WIN_R - 1)
            lv = (d0 >= lo) & (d0 < lo + WIN_R)
            rv = (d0 + 1 >= lo) & (d0 + 1 < lo + WIN_R)
            idx = jnp.where(lv & rv, d0, jnp.where(rv, PAIR_LEFT + d0 + 1, jnp.where(lv, PAIR_RIGHT + d0, PAIR_NONE)))
            tile[j].append(jnp.clip(idx, 0, PAIR_NONE))
    outs = []
    for h in range(N_HEADS):
        sl = slice(h * HEAD_DIM, (h + 1) * HEAD_DIM)
        qh = q[:, sl]
        bias = jnp.concatenate(
            [jnp.concatenate([bias_ref[h, tile[j][p]] for p in range(K_ROWS // 2)], axis=-1)
             for j in range(Q_ROWS)], axis=0)
        s1 = _nt(qh, kw[:, sl]) + bias
        s2 = _nt(qh, kc[:, sl])
        m = jnp.maximum(jnp.max(s1, axis=-1, keepdims=True), jnp.max(s2, axis=-1, keepdims=True))
        e1 = jnp.exp(s1 - m)
        e2 = jnp.exp(s2 - m)
        den = jnp.sum(e1, axis=-1, keepdims=True) + jnp.sum(e2, axis=-1, keepdims=True)
        o = (jnp.dot(e1.astype(BF16), vw[:, sl], preferred_element_type=F32)
             + jnp.dot(e2.astype(BF16), vc[:, sl], preferred_element_type=F32))
        outs.append(o / den)
    o_ref[0] = jnp.concatenate(outs, axis=-1).astype(o_ref.dtype)


def _nbr_attention(q, k, v, kc, vc, bias):
    nb, L, _ = q.shape
    rows = L // GRID_W
    ng = rows // Q_ROWS
    past = kc.shape[1]
    tq = Q_ROWS * GRID_W
    full = pl.BlockSpec((1, 1, L, D_ATT), lambda b, g: (b, 0, 0, 0))
    ctx = pl.BlockSpec((1, past, D_ATT), lambda b, g: (b, 0, 0))
    qspec = pl.BlockSpec((1, tq, D_ATT), lambda b, g: (b, g, 0))
    bspec = pl.BlockSpec(bias.shape, lambda b, g: (0, 0, 0, 0), pipeline_mode=pl.Buffered(1))
    return pl.pallas_call(
        functools.partial(_nbr_attn_kernel, rows=rows),
        grid=(nb, ng),
        in_specs=[qspec, full, full, ctx, ctx, bspec],
        out_specs=qspec,
        out_shape=jax.ShapeDtypeStruct((nb, L, D_ATT), BF16),
        compiler_params=_cp("arbitrary", "arbitrary"),
        name="nbr_attn",
    )(q, k, v, kc, vc, bias)


def _fill_padded(xp_ref, src_ref, L):
    z = jnp.zeros((HALO, xp_ref.shape[1]), F32)
    xp_ref[0:HALO, :] = z
    xp_ref[HALO:HALO + L, :] = src_ref[...].astype(F32)
    xp_ref[HALO + L:2 * HALO + L, :] = z


def _rg_gates_kernel(x_ref, cw_ref, cb_ref, w_ref, b_ref, lam_ref,
                     af_ref, bf_ref, ab_ref, bb_ref, xp_ref, *, L, tc, m):
    nh = cw_ref.shape[0]
    for sc in range(m):
        half = (pl.program_id(0) * m + sc) % nh
        cols = slice(sc * LANES, (sc + 1) * LANES)
        _fill_padded(xp_ref, x_ref.at[:, cols], L)
        sp = jax.nn.softplus(-lam_ref[half])
        cw = cw_ref[half]
        for t0 in range(0, L, tc):
            xc = cb_ref[half]
            for k in range(RG_CONV):
                xc = xc + cw[k:k + 1, :] * xp_ref[pl.ds(HALO + t0 + k - RG_CONV // 2, tc), :]
            z = jnp.dot(xc.astype(BF16), w_ref[half], preferred_element_type=F32) + b_ref[half]
            for e, (a_ref, o_ref) in enumerate(((af_ref, bf_ref), (ab_ref, bb_ref))):
                r = _sigmoid(z[:, e * LANES:(e + 1) * LANES])
                i = _sigmoid(z[:, (2 + e) * LANES:(3 + e) * LANES])
                log_a = (-RG_C) * r * sp[:, e * LANES:(e + 1) * LANES]
                a = jnp.exp(log_a)
                one_minus_a2 = -jnp.tanh(log_a) * (a * a + 1.0)
                a_ref[pl.ds(t0, tc), cols] = a
                o_ref[pl.ds(t0, tc), cols] = jnp.sqrt(one_minus_a2) * i * xc


def _blockdiag2(w2):
    z = jnp.zeros_like(w2[0])
    return jnp.concatenate([jnp.concatenate([w2[0], z], 1), jnp.concatenate([z, w2[1]], 1)], 0)


def _rg_weights(wa, ba, wx, bx, lam):
    ws, bs, ls = [], [], []
    for hf in range(D_RNN // LANES):
        blk = slice(2 * hf, 2 * hf + 2)
        ch = slice(hf * LANES, (hf + 1) * LANES)
        ws.append(jnp.concatenate([_blockdiag2(wa[0, blk]), _blockdiag2(wa[1, blk]),
                                   _blockdiag2(wx[0, blk]), _blockdiag2(wx[1, blk])], axis=1))
        bs.append(jnp.concatenate([ba[0, ch], ba[1, ch], bx[0, ch], bx[1, ch]])[None])
        ls.append(jnp.concatenate([lam[0, ch], lam[1, ch]])[None])
    return jnp.stack(ws).astype(BF16), jnp.stack(bs), jnp.stack(ls)


def _lane_chunks(L, cols):
    m = max(1, min(8, BLOCK_BYTES // (L * LANES * 4)))
    while cols % (m * LANES):
        m //= 2
    return m


def _rg_gates(xr_t, conv_w, conv_b, w_bd, b_cat, lam_cat):
    L, cols = xr_t.shape
    nh = D_RNN // LANES
    tc = min(L, ROW_CHUNK)
    m = _lane_chunks(L, cols)
    col = pl.BlockSpec((L, m * LANES), lambda c: (0, c))
    const = lambda a: pl.BlockSpec(a.shape, lambda c: (0,) * a.ndim)
    shape = jax.ShapeDtypeStruct((L, cols), F32)
    cw = conv_w.reshape(RG_CONV, nh, LANES).transpose(1, 0, 2)
    cb = conv_b.reshape(nh, 1, LANES)
    return pl.pallas_call(
        functools.partial(_rg_gates_kernel, L=L, tc=tc, m=m),
        grid=(cols // (m * LANES),),
        in_specs=[col, const(cw), const(cb), const(w_bd), const(b_cat), const(lam_cat)],
        out_specs=[col, col, col, col],
        out_shape=[shape, shape, shape, shape],
        scratch_shapes=[pltpu.VMEM((L + 2 * HALO, LANES), F32)],
        compiler_params=_cp("arbitrary"),
        name="rg_gates",
    )(xr_t, cw, cb, w_bd, b_cat, lam_cat)


def _rg_scan_kernel(af_ref, bf_ref, ab_ref, bb_ref, h0f_ref, h0b_ref, hf_ref, hb_ref, cf_ref, cb_ref, *, tc):
    @pl.when(pl.program_id(1) == 0)
    def _():
        cf_ref[...] = h0f_ref[...]
        cb_ref[...] = h0b_ref[...]

    def body(t, carry):
        hf, hb = carry
        hf = af_ref[pl.ds(t, 1), :] * hf + bf_ref[pl.ds(t, 1), :]
        hf_ref[pl.ds(t, 1), :] = hf
        tb = tc - 1 - t
        hb = ab_ref[pl.ds(tb, 1), :] * hb + bb_ref[pl.ds(tb, 1), :]
        hb_ref[pl.ds(tb, 1), :] = hb
        return hf, hb

    hf, hb = lax.fori_loop(0, tc, body, (cf_ref[...], cb_ref[...]), unroll=8)
    cf_ref[...] = hf
    cb_ref[...] = hb


def _rg_scan(a_f, b_f, a_b, b_b, h0f, h0b):
    L, cols = a_f.shape
    cw = min(cols, SCAN_LANES)
    tc = min(L, ROW_CHUNK)
    nt = L // tc
    fwd = pl.BlockSpec((tc, cw), lambda c, i: (i, c))
    bwd = pl.BlockSpec((tc, cw), lambda c, i: (nt - 1 - i, c))
    row = pl.BlockSpec((1, cw), lambda c, i: (0, c))
    shape = jax.ShapeDtypeStruct((L, cols), F32)
    return pl.pallas_call(
        functools.partial(_rg_scan_kernel, tc=tc),
        grid=(cols // cw, nt),
        in_specs=[fwd, fwd, bwd, bwd, row, row],
        out_specs=[fwd, bwd],
        out_shape=[shape, shape],
        scratch_shapes=[pltpu.VMEM((1, cw), F32), pltpu.VMEM((1, cw), F32)],
        compiler_params=_cp("arbitrary", "arbitrary"),
        name="rg_scan",
    )(a_f, b_f, a_b, b_b, h0f, h0b)


def _hy_filter_kernel(feat_ref, w1_ref, b1_ref, w2_ref, b2_ref, w3_ref, fr_ref, dec_ref, o_ref, *, tm):
    feats = feat_ref[...]
    dot = functools.partial(jnp.dot, precision=HIGHEST, preferred_element_type=F32)
    h = jnp.sin(fr_ref[0:1, :] * (dot(feats, w1_ref[...]) + b1_ref[...]))
    h = jnp.sin(fr_ref[1:2, :] * (dot(h, w2_ref[...]) + b2_ref[...]))
    h = dot(h, w3_ref[...])
    filt = h * jnp.exp(-feats[:, 0:1] * jnp.abs(dec_ref[...]))
    row = pl.program_id(0) * tm + lax.broadcasted_iota(jnp.int32, filt.shape, 0)
    lane = lax.broadcasted_iota(jnp.int32, filt.shape, 1)
    o_ref[...] = jnp.where((row == 0) & (lane >= 2 * D_HY), 0.0, filt)


def _hy_features(L):
    pos = np.arange(L, dtype=np.float64)
    t = np.linspace(0.0, 1.0, L)
    bands = (POS_EMB_DIM - 1) // 2
    f = np.linspace(1e-4, bands - 1, bands)
    ang = (2.0 * math.pi / L) * pos[:, None] * f[None, :]
    feats = np.concatenate([t[:, None], np.cos(ang), -np.sin(ang)], axis=-1)
    out = np.zeros((L, LANES), np.float32)
    out[:, :POS_EMB_DIM] = feats
    return out


def _two_level(L):
    return L % (FFT_N1 * 8) == 0 and L >= FFT_MIN_LEN


def _digit_swap(x, L):
    n2 = 2 * L // FFT_N1
    return x.reshape(L // n2, n2, -1).transpose(1, 0, 2).reshape(L, -1)


def _hy_filters(L, w1, b1, w2, b2, w3, freq, decay):
    tm = min(L, ROW_CHUNK)
    hid = w2.shape[0]
    n = w3.shape[1]
    w1p = jnp.zeros((LANES, hid), F32).at[:POS_EMB_DIM].set(w1)
    feats = _hy_features(L)
    if _two_level(L):
        feats = _digit_swap(feats, L)
    c = lambda s: pl.BlockSpec(s, lambda i: (0, 0))
    return pl.pallas_call(
        functools.partial(_hy_filter_kernel, tm=tm),
        grid=(L // tm,),
        in_specs=[pl.BlockSpec((tm, LANES), lambda i: (i, 0)),
                  c((LANES, hid)), c((1, hid)), c((hid, hid)), c((1, hid)), c((hid, n)), c((2, hid)), c((1, n))],
        out_specs=pl.BlockSpec((tm, n), lambda i: (i, 0)),
        out_shape=jax.ShapeDtypeStruct((L, n), F32),
        compiler_params=_cp("arbitrary"),
        name="hy_filter",
    )(jnp.asarray(feats), w1p, b1[None], w2, b2[None], w3, freq, decay.reshape(1, n))


def _conv3_kernel(v_ref, a_ref, b_ref, w_ref, cb_ref, ov_ref, oa_ref, ob_ref, xp_ref, xs_ref,
                  *, L, tc, swap, n2, m):
    h1 = L // n2
    pitch = n2 + STRIDE_PAD
    nh = w_ref.shape[0]
    for sc in range(m):
        half = (pl.program_id(0) * m + sc) % nh
        cols = slice(sc * LANES, (sc + 1) * LANES)
        for p, (src, dst) in enumerate(((v_ref, ov_ref), (a_ref, oa_ref), (b_ref, ob_ref))):
            _fill_padded(xp_ref, src.at[:, cols], L)
            w = w_ref[half, p]
            cb = cb_ref[half, p]
            if swap[p]:
                for i1 in range(h1):
                    xs_ref[pl.ds(i1 * pitch, n2 + HY_CONV - 1), :] = (
                        xp_ref[pl.ds(HALO + i1 * n2 - HY_CONV // 2, n2 + HY_CONV - 1), :])

                def body(i2, carry, dst=dst, w=w, cb=cb):
                    y = cb
                    for k in range(HY_CONV):
                        y = y + w[k:k + 1, :] * xs_ref[pl.ds(i2 + k, h1, stride=pitch), :]
                    dst[i2] = y.astype(dst.dtype)
                    return carry

                lax.fori_loop(0, n2, body, 0)
            else:
                for t0 in range(0, L, tc):
                    y = cb
                    for k in range(HY_CONV):
                        y = y + w[k:k + 1, :] * xp_ref[pl.ds(HALO + t0 + k - HY_CONV // 2, tc), :]
                    dst[pl.ds(t0, tc), cols] = y.astype(dst.dtype)


def _hy_conv3(v_t, a_t, b_t, conv_w, conv_b, swap=(False, False, False)):
    L, cols = v_t.shape
    nh = D_HY // LANES
    tc = min(L, ROW_CHUNK)
    n2 = 2 * L // FFT_N1
    m = 1 if any(swap) else _lane_chunks(L, cols)
    col = pl.BlockSpec((L, m * LANES), lambda c: (0, c))
    col3 = pl.BlockSpec((n2, L // n2, LANES), lambda c: (0, 0, c))
    shape = jax.ShapeDtypeStruct((L, cols), BF16)
    shape3 = jax.ShapeDtypeStruct((n2, L // n2, cols), BF16)
    w = conv_w.reshape(HY_CONV, 3, nh, LANES).transpose(2, 1, 0, 3)
    cb = conv_b.reshape(3, nh, 1, LANES).transpose(1, 0, 2, 3)
    const = lambda a: pl.BlockSpec(a.shape, lambda c: (0,) * a.ndim)
    return pl.pallas_call(
        functools.partial(_conv3_kernel, L=L, tc=tc, swap=swap, n2=n2, m=m),
        grid=(cols // (m * LANES),),
        in_specs=[col, col, col, const(w), const(cb)],
        out_specs=[col3 if s else col for s in swap],
        out_shape=[shape3 if s else shape for s in swap],
        scratch_shapes=[pltpu.VMEM((L + 2 * HALO, LANES), F32),
                        pltpu.VMEM(((L // n2) * (n2 + STRIDE_PAD) + HALO, LANES), F32)],
        compiler_params=_cp("arbitrary"),
        name="hy_conv3",
    )(v_t, a_t, b_t, w, cb)


def _phase(p, n2):
    ang = 2.0 * np.pi * (np.asarray(p, np.int64) % n2) / n2
    return np.cos(ang), np.sin(ang)


def _dft_direct_tables(L):
    k = np.arange(L)[:, None]
    t = np.arange(L)[None, :]
    c, s = _phase((2 * k + 1) * t, 4 * L)
    fwd = np.concatenate([c, -s], axis=0)
    inv = (1.0 / L) * np.concatenate([c.T, -s.T], axis=1)
    return fwd.astype(np.float32), inv.astype(np.float32)


def _dft_two_level_tables(L, n1):
    N = 2 * L
    n2 = N // n1
    h1 = n1 // 2
    i2 = np.arange(n2)[:, None, None]
    k1 = np.arange(n1)[None, :, None]
    i1 = np.arange(h1)[None, None, :]
    c, s = _phase(2 * n2 * i1 * k1 + 2 * i2 * k1 + n2 * i1 + i2, 2 * N)
    t1 = np.concatenate([c, -s], axis=1)
    ct, st = c.transpose(0, 2, 1), s.transpose(0, 2, 1)
    t1i = (2.0 / N) * np.concatenate([ct, -st], axis=2)
    kk = np.arange(n2 // 2)[:, None]
    nn = np.arange(n2)[None, :]
    c2, s2 = _phase(nn * kk, n2)
    t2 = np.block([[c2, s2], [-s2, c2]])
    t2i = np.block([[c2.T, -s2.T], [s2.T, c2.T]])
    return tuple(x.astype(np.float32) for x in (t1, t2, t2i, t1i))


def _filter_spectrum(fre, fim, order, skip):
    fw = slice(order * D_HY, (order + 1) * D_HY)
    bw = slice((2 + order) * D_HY, (3 + order) * D_HY)
    return fre[:, fw] + fre[:, bw] + skip, fim[:, fw] - fim[:, bw]


def _spectral_product(xr, xi, fr, fi, nbc):
    yr, yi = [], []
    for b in range(nbc):
        sl = slice(b * D_HY, (b + 1) * D_HY)
        yr.append(xr[:, sl] * fr - xi[:, sl] * fi)
        yi.append(xr[:, sl] * fi + xi[:, sl] * fr)
    return jnp.concatenate([jnp.concatenate(yr, axis=1), jnp.concatenate(yi, axis=1)], axis=0)


def _matmul_kernel(a_ref, b_ref, o_ref):
    o_ref[...] = jnp.dot(a_ref[...], b_ref[...], precision=HIGHEST, preferred_element_type=F32)


def _hy_direct_kernel(z_ref, g_ref, f_ref, skip_ref, t_ref, ti_ref, o_ref, *, L, nbc, order):
    x = jnp.dot(t_ref[...], z_ref[...].astype(BF16), preferred_element_type=F32)
    fr, fi = _filter_spectrum(f_ref[0:L, :], f_ref[L:2 * L, :], order, skip_ref[0])
    y = _spectral_product(x[:L], x[L:], fr, fi, nbc).astype(BF16)
    conv = jnp.dot(ti_ref[...], y, preferred_element_type=F32)
    o_ref[...] = (g_ref[...] * conv).astype(o_ref.dtype)


def _hyena_direct(z, g1, g2, filt, skip):
    L, cols = z.shape
    fwd, inv = _dft_direct_tables(L)
    n = filt.shape[1]
    spec = pl.pallas_call(
        _matmul_kernel,
        out_shape=jax.ShapeDtypeStruct((2 * L, n), F32),
        name="hy_fspec",
    )(jnp.asarray(fwd), filt)
    tn = min(cols, 4 * D_HY)
    nbc = tn // D_HY
    col = pl.BlockSpec((L, tn), lambda c: (0, c))
    const = lambda s: pl.BlockSpec(s, lambda c: (0, 0))
    t_bf, ti_bf = jnp.asarray(fwd).astype(BF16), jnp.asarray(inv).astype(BF16)
    for order, g in enumerate((g1, g2)):
        z = pl.pallas_call(
            functools.partial(_hy_direct_kernel, L=L, nbc=nbc, order=order),
            grid=(cols // tn,),
            in_specs=[col, col, const((2 * L, n)), pl.BlockSpec((1, 1, D_HY), lambda c: (order, 0, 0)),
                      const((2 * L, L)), const((L, 2 * L))],
            out_specs=col,
            out_shape=jax.ShapeDtypeStruct((L, cols), F32 if order == 0 else BF16),
            compiler_params=_cp("arbitrary"),
            name="hy_direct",
        )(z, g, spec, skip[:, None, :], t_bf, ti_bf)
    return z


def _kron_tables(t1, t1i, G):
    n2, two_n1, h1 = t1.shape
    ng = n2 // G
    eye = np.eye(two_n1, dtype=np.float32)
    rep_rows = jnp.asarray(np.repeat(eye, G, axis=0))
    rep_cols = jnp.asarray(np.repeat(eye, G, axis=1))
    c_fwd = t1.reshape(ng, G, two_n1, h1).transpose(0, 2, 1, 3).reshape(ng, two_n1, G * h1)
    c_swapped = t1i.reshape(ng, G * h1, two_n1)
    c_natural = t1i.reshape(ng, G, h1, two_n1).transpose(0, 2, 1, 3).reshape(ng, h1 * G, two_n1)

    def expand(compact, rep, left, row_key, lane_key):
        r, c = (rep.shape[0], compact.shape[2]) if left else (compact.shape[1], rep.shape[1])
        return pl.pallas_call(
            functools.partial(_kron_expand_kernel, left=left, row_key=row_key, lane_key=lane_key),
            grid=(ng,),
            in_specs=[pl.BlockSpec((1,) + compact.shape[1:], lambda i: (i, 0, 0)),
                      pl.BlockSpec(rep.shape, lambda i: (0, 0))],
            out_specs=pl.BlockSpec((1, r, c), lambda i: (i, 0, 0)),
            out_shape=jax.ShapeDtypeStruct((ng, r, c), BF16),
            compiler_params=_cp("arbitrary"),
            name="hy_kron",
        )(jnp.asarray(compact), rep)

    fwd = expand(c_fwd, rep_rows, True, lambda r: r % G, lambda c: c // h1)
    inv_swapped = expand(c_swapped, rep_cols, False, lambda r: r // h1, lambda c: c % G)
    inv_natural = expand(c_natural, rep_cols, False, lambda r: r % G, lambda c: c % G)
    return fwd, inv_swapped, inv_natural


def _kron_expand_kernel(c_ref, rep_ref, o_ref, *, left, row_key, lane_key):
    comp = c_ref[0].astype(BF16)
    rep = rep_ref[...].astype(BF16)
    full = (jnp.dot(rep, comp, preferred_element_type=F32) if left
            else jnp.dot(comp, rep, preferred_element_type=F32))
    rows = lax.broadcasted_iota(jnp.int32, full.shape, 0)
    lanes = lax.broadcasted_iota(jnp.int32, full.shape, 1)
    o_ref[0] = jnp.where(row_key(rows) == lane_key(lanes), full, 0.0).astype(BF16)


def _hy_stage1_kernel(x_ref, t_ref, re_ref, im_ref, *, G, n1, ct):
    h1, cols = x_ref.shape[1], x_ref.shape[2]
    x = x_ref[...].reshape(G * h1, cols).astype(BF16)
    for c0 in range(0, cols, ct):
        a = jnp.dot(t_ref[0], x[:, c0:c0 + ct], preferred_element_type=F32)
        re_ref[:, :, c0:c0 + ct] = a[:n1 * G].reshape(n1, G, ct).astype(BF16)
        im_ref[:, :, c0:c0 + ct] = a[n1 * G:].reshape(n1, G, ct).astype(BF16)


def _hy_stage1(x3, t_fwd, *, G):
    n2, h1, cols = x3.shape
    n1 = t_fwd.shape[1] // (2 * G)
    ospec = pl.BlockSpec((n1, G, cols), lambda j: (0, j, 0))
    oshape = jax.ShapeDtypeStruct((n1, n2, cols), BF16)
    return pl.pallas_call(
        functools.partial(_hy_stage1_kernel, G=G, n1=n1, ct=D_HY),
        grid=(n2 // G,),
        in_specs=[pl.BlockSpec((G, h1, cols), lambda j: (j, 0, 0)),
                  pl.BlockSpec((1,) + t_fwd.shape[1:], lambda j: (j, 0, 0))],
        out_specs=[ospec, ospec],
        out_shape=[oshape, oshape],
        compiler_params=_cp("arbitrary"),
        name="hy_stage1",
    )(x3, t_fwd)


def _hy_fspec2_kernel(re_ref, im_ref, skip_ref, t2_ref, fre_ref, fim_ref, *, kb, k2):
    for j in range(kb):
        a2 = jnp.concatenate([re_ref[j], im_ref[j]], axis=0)
        x = jnp.dot(t2_ref[...], a2, preferred_element_type=F32)
        for order in range(2):
            fr, fi = _filter_spectrum(x[:k2], x[k2:], order, skip_ref[order])
            fre_ref[order, j] = fr
            fim_ref[order, j] = fi


def _hy_stage2_kernel(re_ref, im_ref, fre_ref, fim_ref, t2_ref, t2i_ref, ore_ref, oim_ref, *, kb, k2, n2, nbc):
    for j in range(kb):
        a2 = jnp.concatenate([re_ref[j], im_ref[j]], axis=0)
        x = jnp.dot(t2_ref[...], a2, preferred_element_type=F32)
        fr, fi = fre_ref[0, j], fim_ref[0, j]
        y = _spectral_product(x[:k2], x[k2:], fr, fi, nbc).astype(BF16)
        a = jnp.dot(t2i_ref[...], y, preferred_element_type=F32)
        ore_ref[j] = a[:n2].astype(BF16)
        oim_ref[j] = a[n2:].astype(BF16)


def _hy_stage3_kernel(re_ref, im_ref, t_ref, g_ref, o_ref, *, G, n1, ct):
    cols = re_ref.shape[2]
    for c0 in range(0, cols, ct):
        sl = slice(c0, c0 + ct)
        a2 = jnp.concatenate([re_ref[:, :, sl].reshape(n1 * G, ct), im_ref[:, :, sl].reshape(n1 * G, ct)], axis=0)
        conv = jnp.dot(t_ref[0], a2, preferred_element_type=F32)
        g = g_ref[:, :, sl]
        o_ref[:, :, sl] = (g * conv.reshape(g.shape)).astype(o_ref.dtype)


def _hy_stage31_kernel(re_ref, im_ref, ti_ref, g_ref, tf_ref, ore_ref, oim_ref, *, G, n1, ct):
    cols = re_ref.shape[2]
    for c0 in range(0, cols, ct):
        sl = slice(c0, c0 + ct)
        a2 = jnp.concatenate([re_ref[:, :, sl].reshape(n1 * G, ct), im_ref[:, :, sl].reshape(n1 * G, ct)], axis=0)
        conv = jnp.dot(ti_ref[0], a2, preferred_element_type=F32)
        z = g_ref[:, :, sl].reshape(conv.shape) * conv
        a = jnp.dot(tf_ref[0], z.astype(BF16), preferred_element_type=F32)
        ore_ref[:, :, sl] = a[:n1 * G].reshape(n1, G, ct).astype(BF16)
        oim_ref[:, :, sl] = a[n1 * G:].reshape(n1, G, ct).astype(BF16)


def _two_level_tables(L, *, n1=FFT_N1, G=FFT_G):
    t1, t2, t2i, t1i = _dft_two_level_tables(L, n1)
    return (jnp.asarray(t2).astype(BF16), jnp.asarray(t2i).astype(BF16)) + _kron_tables(t1, t1i, G)


def _hyena_two_level(z3, g1_3, g2, filt, skip, tables, *, n1=FFT_N1, G=FFT_G, kb=FFT_KB):
    n2, h1, cols = z3.shape
    L = n2 * h1
    k2 = n2 // 2
    nbc = cols // D_HY
    t2_bf, t2i_bf, t_fwd, t_inv_swapped, t_inv_natural = tables
    nf = filt.shape[1]

    f_re, f_im = _hy_stage1(filt.reshape(n2, h1, nf), t_fwd, G=G)
    blk = lambda c: pl.BlockSpec((kb, n2, c), lambda i: (i, 0, 0))
    const = lambda s: pl.BlockSpec(s, lambda i: (0, 0))
    fshape = jax.ShapeDtypeStruct((2, n1, k2, D_HY), F32)
    fspec = pl.BlockSpec((2, kb, k2, D_HY), lambda i: (0, i, 0, 0))
    ospec = lambda order: pl.BlockSpec((1, kb, k2, D_HY), lambda i: (order, i, 0, 0))
    fre, fim = pl.pallas_call(
        functools.partial(_hy_fspec2_kernel, kb=kb, k2=k2),
        grid=(n1 // kb,),
        in_specs=[blk(nf), blk(nf), pl.BlockSpec((2, 1, D_HY), lambda i: (0, 0, 0)), const((n2, 2 * n2))],
        out_specs=[fspec, fspec],
        out_shape=[fshape, fshape],
        compiler_params=_cp("arbitrary"),
        name="hy_fspec2",
    )(f_re, f_im, skip[:, None, :], t2_bf)

    ashape = jax.ShapeDtypeStruct((n1, n2, cols), BF16)
    grp = pl.BlockSpec((n1, G, cols), lambda j: (0, j, 0))
    tspec = lambda t: pl.BlockSpec((1,) + t.shape[1:], lambda j: (j, 0, 0))
    a_re, a_im = _hy_stage1(z3, t_fwd, G=G)
    for order in range(2):
        b_re, b_im = pl.pallas_call(
            functools.partial(_hy_stage2_kernel, kb=kb, k2=k2, n2=n2, nbc=nbc),
            grid=(n1 // kb,),
            in_specs=[blk(cols), blk(cols), ospec(order), ospec(order), const((n2, 2 * n2)), const((2 * n2, n2))],
            out_specs=[blk(cols), blk(cols)],
            out_shape=[ashape, ashape],
            compiler_params=_cp("arbitrary"),
            name="hy_stage2",
        )(a_re, a_im, fre, fim, t2_bf, t2i_bf)
        if order == 0:
            a_re, a_im = pl.pallas_call(
                functools.partial(_hy_stage31_kernel, G=G, n1=n1, ct=D_HY),
                grid=(n2 // G,),
                in_specs=[grp, grp, tspec(t_inv_swapped), pl.BlockSpec((G, h1, cols), lambda j: (j, 0, 0)),
                          tspec(t_fwd)],
                out_specs=[grp, grp],
                out_shape=[ashape, ashape],
                compiler_params=_cp("arbitrary"),
                name="hy_stage31",
            )(b_re, b_im, t_inv_swapped, g1_3, t_fwd)
        else:
            io = pl.BlockSpec((h1, G, cols), lambda j: (0, j, 0))
            out = pl.pallas_call(
                functools.partial(_hy_stage3_kernel, G=G, n1=n1, ct=D_HY),
                grid=(n2 // G,),
                in_specs=[grp, grp, tspec(t_inv_natural), io],
                out_specs=io,
                out_shape=jax.ShapeDtypeStruct((h1, n2, cols), BF16),
                compiler_params=_cp("arbitrary"),
                name="hy_stage3",
            )(b_re, b_im, t_inv_natural, g2.reshape(h1, n2, cols))
    return out.reshape(L, cols)


def _hyena(hv_t, h1_t, h2_t, conv_w, conv_b, filt, skip, tables=None):
    L, cols = hv_t.shape
    if _two_level(L):
        z3, g1_3, g2 = _hy_conv3(hv_t, h1_t, h2_t, conv_w, conv_b, swap=(True, True, False))
        return _hyena_two_level(z3, g1_3, g2, filt, skip, tables or _two_level_tables(L))
    z, g1, g2 = _hy_conv3(hv_t, h1_t, h2_t, conv_w, conv_b)
    return _hyena_direct(z, g1, g2, filt, skip)


def _route(logits):
    lane_i = lax.broadcasted_iota(jnp.int32, logits.shape, 1)
    lane = lane_i.astype(F32)
    is_g = (lane_i >= N_EXPERTS) & (lane_i < N_EXPERTS + N_GROUPS)
    big = float(LANES)
    gl = jnp.where(is_g, logits, NEG)
    gmax = jnp.max(gl, axis=-1, keepdims=True)
    gidx = jnp.min(jnp.where(gl == gmax, lane - N_EXPERTS, big), axis=-1, keepdims=True)
    p_group = 1.0 / jnp.sum(jnp.where(is_g, jnp.exp(gl - gmax), 0.0), axis=-1, keepdims=True)
    sel = (lane_i < N_EXPERTS) & ((lane_i // EXPERTS_PER_GROUP) == gidx.astype(jnp.int32))
    el = jnp.where(sel, logits, NEG)
    ee = jnp.where(sel, jnp.exp(el - jnp.max(el, axis=-1, keepdims=True)), 0.0)
    pe = ee / jnp.sum(ee, axis=-1, keepdims=True)
    p1 = jnp.max(jnp.where(sel, pe, -1.0), axis=-1, keepdims=True)
    i1 = jnp.min(jnp.where(sel & (pe == p1), lane, big), axis=-1, keepdims=True)
    rest = sel & (lane != i1)
    p2 = jnp.max(jnp.where(rest, pe, -1.0), axis=-1, keepdims=True)
    i2 = jnp.min(jnp.where(rest & (pe == p2), lane, big), axis=-1, keepdims=True)
    den = p1 + p2
    cmb = jnp.where(lane == i1, p_group * (p1 / den), jnp.where(lane == i2, p_group * (p2 / den), 0.0))
    return jnp.where(lane_i == N_EXPERTS, gidx, cmb)


def _outproj_kernel(att_ref, hf_ref, hb_ref, gt_ref, hy_ref, x_ref, mod_ref, w_ref, g_ref, b_ref,
                    wr_ref, br_ref, x1_ref, u2_ref, cmb_ref):
    bpt = att_ref.shape[0]

    def rows(ref):
        return jnp.concatenate([ref[:, b * D_RNN:(b + 1) * D_RNN] for b in range(bpt)], axis=0)

    rg = (rows(hf_ref) + rows(hb_ref)) * jax.nn.gelu(rows(gt_ref).astype(F32))
    att = att_ref[...].reshape(bpt * att_ref.shape[1], D_ATT)
    mix = (jnp.dot(att, w_ref[0:D_ATT, :], preferred_element_type=F32)
           + jnp.dot(rg.astype(BF16), w_ref[D_ATT:D_ATT + D_RNN, :], preferred_element_type=F32)
           + jnp.dot(rows(hy_ref).astype(BF16), w_ref[D_ATT + D_RNN:, :], preferred_element_type=F32))
    m = mod_ref[0]
    x1 = _ln(ALPHA * x_ref[...] + m[2:3] * mix) * g_ref[...] + b_ref[...]
    u2 = _ln(x1) * (1.0 + m[4:5]) + m[3:4]
    u_hi = u2.astype(BF16)
    u_lo = (u2 - u_hi.astype(F32)).astype(BF16)
    wr = wr_ref[...]
    w_hi = wr.astype(BF16)
    w_lo = (wr - w_hi.astype(F32)).astype(BF16)
    hi = jnp.dot(u_hi, jnp.concatenate([w_hi, w_lo], axis=1), preferred_element_type=F32)
    lo = jnp.dot(u_lo, w_hi, preferred_element_type=F32)
    logits = hi[:, 0:LANES] + hi[:, LANES:] + lo + br_ref[...]
    x1_ref[...] = x1
    u2_ref[...] = u_hi
    cmb_ref[...] = _route(logits)


def _outproj(att, hf_t, hb_t, gt_t, hy_t, x2d, mod_l, w_bf, ln_g, ln_b, wr, br, *, nb, L, tm,
             cond_base, cond_per_batch):
    tpb, bpt = _tile_split(nb, L, tm, cond_per_batch)
    tm = L * bpt // tpb
    d = x2d.shape[1]
    cond = (lambda i: cond_base + i // tpb) if cond_per_batch else (lambda i: cond_base)
    t_spec = pl.BlockSpec((L // tpb, bpt * D_RNN), lambda i: (i % tpb, i // tpb))
    tok = lambda c: pl.BlockSpec((tm, c), lambda i: (i, 0))
    const = lambda s: pl.BlockSpec(s, lambda i: (0, 0))
    n = nb * L
    return pl.pallas_call(
        _outproj_kernel,
        grid=(nb * tpb // bpt,),
        in_specs=[pl.BlockSpec((bpt, L // tpb, D_ATT), lambda i: (i // tpb, i % tpb, 0)),
                  t_spec, t_spec, t_spec, t_spec, tok(d),
                  pl.BlockSpec((1, 6, d), lambda i: (cond(i), 0, 0)),
                  const((d, d)), const((1, d)), const((1, d)), const((d, LANES)), const((1, LANES))],
        out_specs=[tok(d), tok(d), tok(LANES)],
        out_shape=[jax.ShapeDtypeStruct((n, d), F32), jax.ShapeDtypeStruct((n, d), BF16),
                   jax.ShapeDtypeStruct((n, LANES), F32)],
        compiler_params=_cp("arbitrary"),
        name="outproj",
    )(att, hf_t, hb_t, gt_t, hy_t, x2d, mod_l, w_bf, ln_g[None], ln_b[None], wr, br)


def _moe_kernel(u_ref, cmb_ref, x1_ref, mod_ref, wgu_ref, wd_ref, g_ref, b_ref, o_ref, ys_ref, *, T, BLK):
    SL = T + N_GROUPS * BLK
    u = u_ref[...]
    cmb = cmb_ref[...]
    lane_f = lax.broadcasted_iota(jnp.int32, (T, LANES), 1).astype(F32)
    gid_col = jnp.sum(jnp.where(lane_f == float(N_EXPERTS), cmb, 0.0), axis=-1, keepdims=True)
    onehot = jnp.where(lane_f == gid_col, 1.0, 0.0)
    ri = lax.broadcasted_iota(jnp.int32, (T, T), 0)
    ci = lax.broadcasted_iota(jnp.int32, (T, T), 1)
    before = jnp.where(ci < ri, 1.0, 0.0).astype(BF16)
    rank_col = jnp.sum(onehot * jnp.dot(before, onehot.astype(BF16), preferred_element_type=F32),
                       axis=-1, keepdims=True)
    counts = jnp.sum(onehot, axis=0, keepdims=True)
    lane_row = lax.broadcasted_iota(jnp.int32, (1, LANES), 1)
    base = jnp.int32(0)
    bases = []
    base_row = jnp.zeros((1, LANES), jnp.int32)
    for c in range(N_GROUPS):
        n_c = jnp.sum(jnp.where(lane_row == c, counts, 0.0)).astype(jnp.int32)
        bases.append(base)
        base_row = jnp.where(lane_row == c, base, base_row)
        base = base + ((n_c + (BLK - 1)) // BLK) * BLK
    total = base
    dest_f = jnp.sum(onehot * base_row.astype(F32), axis=-1, keepdims=True) + rank_col
    dest_col = dest_f.astype(jnp.int32)
    dest_row = jnp.broadcast_to(dest_f, (T, LANES)).T[0:1, :].astype(jnp.int32)
    cmb_hi = cmb.astype(BF16)
    cmb_lo = (cmb - cmb_hi.astype(F32)).astype(BF16)
    blk_rows = lax.broadcasted_iota(jnp.int32, (BLK, T), 0)
    blk_lane = lax.broadcasted_iota(jnp.int32, (BLK, LANES), 1)

    def block(b, carry):
        r0 = b * BLK
        rows = pl.ds(pl.multiple_of(r0, BLK), BLK)

        @pl.when(r0 < total)
        def _():
            grp = ((r0 >= bases[1]).astype(jnp.int32) + (r0 >= bases[2]).astype(jnp.int32)
                   + (r0 >= bases[3]).astype(jnp.int32))
            p = jnp.where(blk_rows + r0 == dest_row, 1.0, 0.0).astype(BF16)
            xb = jnp.dot(p, u, preferred_element_type=F32).astype(BF16)
            wb = (jnp.dot(p, cmb_hi, preferred_element_type=F32)
                  + jnp.dot(p, cmb_lo, preferred_element_type=F32))
            gu = jnp.dot(xb, wgu_ref[grp], preferred_element_type=F32)
            hs = []
            for j in range(EXPERTS_PER_GROUP):
                gate = gu[:, 2 * j * D_EXPERT:(2 * j + 1) * D_EXPERT]
                up = gu[:, (2 * j + 1) * D_EXPERT:(2 * j + 2) * D_EXPERT]
                w = jnp.sum(jnp.where(blk_lane == grp * EXPERTS_PER_GROUP + j, wb, 0.0), axis=-1, keepdims=True)
                hs.append((gate * _sigmoid(gate) * up * w).astype(BF16))
            ys_ref[rows, :] = jnp.dot(jnp.concatenate(hs, axis=1), wd_ref[grp],
                                      preferred_element_type=F32).astype(BF16)

        @pl.when(r0 >= total)
        def _():
            ys_ref[rows, :] = jnp.zeros((BLK, u.shape[1]), BF16)

        return carry

    lax.fori_loop(0, SL // BLK, block, 0)
    p_t = jnp.where(lax.broadcasted_iota(jnp.int32, (T, SL), 1) == dest_col, 1.0, 0.0).astype(BF16)
    moe = jnp.dot(p_t, ys_ref[...], preferred_element_type=F32)
    m = mod_ref[0]
    o_ref[...] = _ln(ALPHA * x1_ref[...] + m[5:6] * moe) * g_ref[...] + b_ref[...]


def _group_gate_up(w_gate, w_up):
    _, d, f = w_gate.shape
    gu = jnp.concatenate([w_gate, w_up], axis=-1).astype(BF16).reshape(N_GROUPS, EXPERTS_PER_GROUP, d, 2 * f)
    return gu.transpose(0, 2, 1, 3).reshape(N_GROUPS, d, EXPERTS_PER_GROUP * 2 * f)


def _moe(u2, cmb, x1, mod_l, wgu, wd, ln_g, ln_b, *, tm, tiles_per_cond, cond_base, cond_per_batch):
    n, d = x1.shape
    cond = (lambda i: cond_base + i // tiles_per_cond) if cond_per_batch else (lambda i: cond_base)
    tok = lambda c: pl.BlockSpec((tm, c), lambda i: (i, 0))
    const = lambda s: pl.BlockSpec(s, lambda i: (0,) * len(s))
    resident = lambda s: pl.BlockSpec(s, lambda i: (0,) * len(s), pipeline_mode=pl.Buffered(1))
    return pl.pallas_call(
        functools.partial(_moe_kernel, T=tm, BLK=MOE_BLK),
        grid=(n // tm,),
        in_specs=[tok(d), tok(LANES), tok(d),
                  pl.BlockSpec((1, 6, d), lambda i: (cond(i), 0, 0)),
                  resident(wgu.shape), resident(wd.shape),
                  const((1, d)), const((1, d))],
        out_specs=tok(d),
        out_shape=jax.ShapeDtypeStruct((n, d), F32),
        scratch_shapes=[pltpu.VMEM((tm + N_GROUPS * MOE_BLK, d), BF16)],
        compiler_params=_cp("arbitrary"),
        name="moe",
    )(u2, cmb, x1, mod_l, wgu, wd, ln_g[None], ln_b[None])


def _stream_layer(x2d, mod_l, p, *, nb, L, tm, cond_base, cond_per_batch, ctx, cache=None, h0=None, prev_kv=None):
    kw = dict(nb=nb, L=L, tm=tm, cond_base=cond_base, cond_per_batch=cond_per_batch)
    q, k, v, xr_t, gt_t, hv_t, h1_t, h2_t = _inproj(x2d, mod_l, p["w_in"], kv_dtype=F32 if ctx else BF16,
                                                    prev_kv=prev_kv, **{**kw, "tm": INPROJ_TILE})
    if ctx:
        att = _ctx_attention(q, k, v)
    else:
        att = _nbr_attention(q, k, v, cache[0], cache[1], p["nbr_bias"])
    a_f, b_f, a_b, b_b = _rg_gates(xr_t, p["rg_conv_w"], p["rg_conv_b"], *p["rg_w"])
    hf_t, hb_t = _rg_scan(a_f, b_f, a_b, b_b, h0[0], h0[1])
    hy_t = _hyena(hv_t, h1_t, h2_t, p["hy_conv_w"], p["hy_conv_b"], p["hy_filt"][L], p["hy_skip"],
                  p["hy_tables"].get(L))
    x1, u2, cmb = _outproj(att, hf_t, hb_t, gt_t, hy_t, x2d, mod_l, p["w_out"], p["ln1_g"], p["ln1_b"],
                           p["router_w"], p["router_b"], **kw)
    tm_moe = min(MOE_TILE, nb * L)
    x2 = _moe(u2, cmb, x1, mod_l, p["moe_wgu"], p["moe_wd"], p["ln2_g"], p["ln2_b"], tm=tm_moe,
              tiles_per_cond=max(L // tm_moe, 1), cond_base=cond_base, cond_per_batch=cond_per_batch)
    return x2, k, v, hf_t, hb_t


def kernel(x_prompt, x_sample, cache_k, cache_v, state_rglru, c, c_ctx, w_mod, b_mod, w_in, attn_rpb, rg_conv_w, rg_conv_b, rg_wa, rg_ba, rg_wx, rg_bx, rg_lambda, hy_conv_w, hy_conv_b, hy_w1, hy_b1, hy_w2, hy_b2, hy_w3, hy_freq, hy_decay, hy_skip, w_out, ln1_g, ln1_b, router_wg, router_bg, router_we, router_be, moe_w_gate, moe_w_up, moe_w_down, ln2_g, ln2_b):
    nbc, Lc, d = x_prompt.shape
    nbl, Ll, _ = x_sample.shape
    depth = w_mod.shape[0]
    past = cache_k.shape[2]

    cond8 = jnp.zeros((8, d), F32).at[0].set(c_ctx).at[1:1 + nbl].set(c)
    mod = _modulation(cond8, w_mod, b_mod).reshape(depth, 8, 6, d)

    xp = x_prompt.reshape(nbc * Lc, d)
    xs = x_sample.reshape(nbl * Ll, d)
    kv, new_h = None, []
    hy_tables = {L: _two_level_tables(L) for L in {Lc, Ll} if _two_level(L)}
    for l in range(depth):
        pad = jnp.zeros((d, LANES - N_EXPERTS - N_GROUPS), F32)
        p = {
            "w_in": w_in[l].astype(BF16),
            "nbr_bias": _nbr_bias_tiles(attn_rpb[l]),
            "rg_conv_w": rg_conv_w[l], "rg_conv_b": rg_conv_b[l],
            "rg_w": _rg_weights(rg_wa[l], rg_ba[l], rg_wx[l], rg_bx[l], rg_lambda[l]),
            "hy_conv_w": hy_conv_w[l], "hy_conv_b": hy_conv_b[l], "hy_skip": hy_skip[l], "hy_tables": hy_tables,
            "hy_filt": {L: _hy_filters(L, hy_w1[l], hy_b1[l], hy_w2[l], hy_b2[l], hy_w3[l], hy_freq[l], hy_decay[l])
                        for L in sorted({Lc, Ll})},
            "w_out": w_out[l].astype(BF16), "ln1_g": ln1_g[l], "ln1_b": ln1_b[l],
            "router_w": jnp.concatenate([router_we[l], router_wg[l], pad], axis=1),
            "router_b": jnp.concatenate([router_be[l], router_bg[l], pad[0]])[None],
            "moe_wgu": _group_gate_up(moe_w_gate[l], moe_w_up[l]),
            "moe_wd": moe_w_down[l].astype(BF16).reshape(N_GROUPS, EXPERTS_PER_GROUP * D_EXPERT, d),
            "ln2_g": ln2_g[l], "ln2_b": ln2_b[l],
        }
        zeros_h = jnp.zeros((1, nbc * D_RNN), F32)
        xp, k, v, hf_t, hb_t = _stream_layer(
            xp, mod[l], p, nb=nbc, L=Lc, tm=ROW_CHUNK, cond_base=0, cond_per_batch=False, ctx=True,
            h0=(zeros_h, zeros_h), prev_kv=kv)
        kv = (k, v)
        new_h.append(jnp.stack([hf_t[Lc - 1].reshape(nbc, D_RNN), hb_t[0].reshape(nbc, D_RNN)], axis=1))

        cache = (cache_k[:, l].reshape(nbl, past, D_ATT), cache_v[:, l].reshape(nbl, past, D_ATT))
        h0 = (state_rglru[:, l, 0].reshape(1, nbl * D_RNN), state_rglru[:, l, 1].reshape(1, nbl * D_RNN))
        xs, _, _, _, _ = _stream_layer(
            xs, mod[l], p, nb=nbl, L=Ll, tm=ROW_CHUNK, cond_base=1, cond_per_batch=True, ctx=False,
            cache=cache, h0=h0)

    return (xp.reshape(nbc, Lc, d), xs.reshape(nbl, Ll, d),
            kv[0].reshape(nbc, depth, Lc, N_HEADS, HEAD_DIM), kv[1].reshape(nbc, depth, Lc, N_HEADS, HEAD_DIM),
            jnp.stack(new_h, axis=1))
```
